```python
import math
import numpy as np
import jax
import jax.numpy as jnp
from jax import lax

D_MODEL = 1024
BATCH = 2
SEQ = 16384
DEPTH = 2

MEM_LEN = 256
NORM_EPS = 1e-6
ROPE_THETA = 10000.0
NEG_INF = -1e30
FORCE_SCORE = 1e9

NSA_HEADS = 8
NSA_KV_GROUPS = 2
NSA_HEAD_DIM = 64
NSA_CMP_BLOCK = 32
NSA_CMP_STRIDE = 16
NSA_SEL_BLOCK = 64
NSA_TOPK = 16
NSA_WINDOW = 512
NSA_Q_BLOCK = 128
NSA_Q_WIDTH = NSA_HEADS * NSA_HEAD_DIM
NSA_KV_WIDTH = NSA_KV_GROUPS * NSA_HEAD_DIM

SSM_INNER = 512
SSM_HEAD_DIM = 64
SSM_HEADS = SSM_INNER // SSM_HEAD_DIM
SSM_GROUPS = 2
SSM_STATE = 64
SSM_CONV = 4
SSM_CHUNK = 128
SSM_CONV_DIM = SSM_INNER + 2 * SSM_GROUPS * SSM_STATE
DT_MIN = 1e-3
DT_MAX = 1e-1

HG_HEADS = 4
HG_KEY_DIM = 128
HG_VAL_DIM = 128
HG_WIDTH = HG_HEADS * HG_KEY_DIM
HG_CHUNK = 64

XA_HEADS = 4
XA_HEAD_DIM = 128
XA_WIDTH = XA_HEADS * XA_HEAD_DIM

FFN_HIDDEN = -(-(8 * D_MODEL) // (3 * 256)) * 256

SPLIT_SIZES = (
    NSA_Q_WIDTH,
    NSA_KV_WIDTH, NSA_KV_WIDTH,
    NSA_KV_WIDTH, NSA_KV_WIDTH,
    NSA_KV_WIDTH, NSA_KV_WIDTH,
    3 * NSA_HEADS,
    SSM_INNER,
    SSM_CONV_DIM,
    SSM_HEADS,
    HG_WIDTH, HG_WIDTH,
    HG_HEADS * HG_VAL_DIM,
    HG_HEADS * HG_VAL_DIM,
    D_MODEL, D_MODEL, D_MODEL,
)
IN_DIM = sum(SPLIT_SIZES)

kernel_name = 'hybrid_nsa_ssd_hgrn2_decoder'


def rms_norm(x, w):
    xf = x.astype(jnp.float32)
    y = xf * lax.rsqrt(jnp.mean(xf * xf, axis=-1, keepdims=True) + NORM_EPS)
    return (y * w.astype(jnp.float32)).astype(x.dtype)


def rope(x, pos):
    half = x.shape[-1] // 2
    inv = 1.0 / (ROPE_THETA ** (jnp.arange(half, dtype=jnp.float32) / half))
    ang = pos.astype(jnp.float32)[:, None] * inv[None, :]
    cos = jnp.cos(ang)[:, None, :]
    sin = jnp.sin(ang)[:, None, :]
    xf = x.astype(jnp.float32)
    x1, x2 = xf[..., :half], xf[..., half:]
    return jnp.concatenate([x1 * cos - x2 * sin, x2 * cos + x1 * sin], axis=-1).astype(x.dtype)


def masked_softmax(s, mask):
    s = jnp.where(mask, s.astype(jnp.float32), NEG_INF)
    return jnp.where(mask, jax.nn.softmax(s, axis=-1), 0.0)


def compress_blocks(kv, pos_emb, w1, w2):
    bsz, s, g, hd = kv.shape
    nb = (s - NSA_CMP_BLOCK) // NSA_CMP_STRIDE + 1
    idx = jnp.arange(nb)[:, None] * NSA_CMP_STRIDE + jnp.arange(NSA_CMP_BLOCK)[None, :]
    blk = kv[:, idx] + pos_emb[None, None, :, None, :]
    blk = blk.transpose(0, 1, 3, 2, 4).reshape(bsz, nb, g, NSA_CMP_BLOCK * hd)
    return jax.nn.silu(blk @ w1) @ w2


def nsa_mixer(q, k_cmp, v_cmp, k_slc, v_slc, k_win, v_win, gate_logits, q_norm, k_norm,
              cmp_pos_k, cmp_pos_v, cmp_w1_k, cmp_w2_k, cmp_w1_v, cmp_w2_v):
    bsz, s, _ = q.shape
    H, G, hd = NSA_HEADS, NSA_KV_GROUPS, NSA_HEAD_DIM
    R = H // G
    Lc, stride, Ls, W, QB = NSA_CMP_BLOCK, NSA_CMP_STRIDE, NSA_SEL_BLOCK, NSA_WINDOW, NSA_Q_BLOCK
    scale = hd ** -0.5
    pos = jnp.arange(s)

    def heads(t, n):
        return t.reshape(bsz, s, n, hd)

    q = rope(rms_norm(heads(q, H), q_norm), pos).reshape(bsz, s, G, R, hd)
    ks = rope(rms_norm(heads(k_slc, G), k_norm), pos)
    kw = rope(rms_norm(heads(k_win, G), k_norm), pos)
    vs = heads(v_slc, G)
    vw = heads(v_win, G)

    nb = (s - Lc) // stride + 1
    pos_c = jnp.arange(nb) * stride + Lc - 1
    kc = rope(rms_norm(compress_blocks(heads(k_cmp, G), cmp_pos_k, cmp_w1_k, cmp_w2_k), k_norm), pos_c)
    vc = compress_blocks(heads(v_cmp, G), cmp_pos_v, cmp_w1_v, cmp_w2_v)

    nsel = s // Ls
    n_top = min(NSA_TOPK, nsel)
    ks_blk = ks.reshape(bsz, nsel, Ls, G, hd).transpose(0, 3, 1, 2, 4)
    vs_blk = vs.reshape(bsz, nsel, Ls, G, hd).transpose(0, 3, 1, 2, 4)
    c_start = jnp.arange(nb) * stride
    s_start = jnp.arange(nsel) * Ls
    cover = jnp.clip(jnp.minimum(c_start[:, None] + Lc, s_start[None, :] + Ls)
                     - jnp.maximum(c_start[:, None], s_start[None, :]), 0, None)
    w_cover = cover.astype(jnp.float32) / stride

    kw_pad = jnp.pad(kw, ((0, 0), (W, 0), (0, 0), (0, 0)))
    vw_pad = jnp.pad(vw, ((0, 0), (W, 0), (0, 0), (0, 0)))
    gates = jax.nn.sigmoid(gate_logits.astype(jnp.float32)).reshape(bsz, s, G, R, 3)
    b_idx = jnp.arange(bsz)[:, None, None, None]
    g_idx = jnp.arange(G)[None, :, None, None]
    blk_ids = jnp.arange(nsel)

    def block_fn(qb):
        start = qb * QB
        t = start + jnp.arange(QB)
        qblk = lax.dynamic_slice_in_dim(q, start, QB, axis=1)
        s_c = jnp.einsum('bqgrd,bngd->bgrqn', qblk, kc) * scale
        p_c = masked_softmax(s_c, pos_c[None, :] <= t[:, None])
        o_c = jnp.einsum('bgrqn,bngd->bqgrd', p_c, vc)
        imp = jnp.einsum('bgrqn,nj->bgqj', p_c, w_cover)
        cur = (t // Ls)[:, None]
        forced = (blk_ids[None, :] == 0) | (blk_ids[None, :] == cur) | (blk_ids[None, :] == cur - 1)
        valid = blk_ids[None, :] * Ls <= t[:, None]
        imp = jnp.where(forced, FORCE_SCORE, jnp.where(valid, imp, NEG_INF))
        _, sel = lax.top_k(imp, n_top)
        k_sel = ks_blk[b_idx, g_idx, sel]
        v_sel = vs_blk[b_idx, g_idx, sel]
        key_pos = sel[..., None] * Ls + jnp.arange(Ls)
        mask_s = (key_pos <= t[None, None, :, None, None]).reshape(bsz, G, 1, QB, n_top * Ls)
        s_s = jnp.einsum('bqgrd,bgqkld->bgrqkl', qblk, k_sel).reshape(bsz, G, R, QB, n_top * Ls) * scale
        p_s = masked_softmax(s_s, mask_s).reshape(bsz, G, R, QB, n_top, Ls)
        o_s = jnp.einsum('bgrqkl,bgqkld->bqgrd', p_s, v_sel)
        k_w = lax.dynamic_slice_in_dim(kw_pad, start, QB + W, axis=1)
        v_w = lax.dynamic_slice_in_dim(vw_pad, start, QB + W, axis=1)
        key_pos_w = start - W + jnp.arange(QB + W)
        dlt = t[:, None] - key_pos_w[None, :]
        mask_w = (dlt >= 0) & (dlt < W) & (key_pos_w[None, :] >= 0)
        s_w = jnp.einsum('bqgrd,bkgd->bgrqk', qblk, k_w) * scale
        p_w = masked_softmax(s_w, mask_w)
        o_w = jnp.einsum('bgrqk,bkgd->bqgrd', p_w, v_w)
        gt = lax.dynamic_slice_in_dim(gates, start, QB, axis=1)
        return gt[..., 0:1] * o_c + gt[..., 1:2] * o_s + gt[..., 2:3] * o_w

    out = lax.map(block_fn, jnp.arange(s // QB))
    return out.transpose(1, 0, 2, 3, 4, 5).reshape(bsz, s, H * hd)


def ssd_chunked(x, a, bm, cm):
    bsz, s, H, P = x.shape
    G, N = bm.shape[-2], bm.shape[-1]
    Hg, L = H // G, SSM_CHUNK
    nc = s // L
    x = x.reshape(bsz, nc, L, G, Hg, P)
    a = a.reshape(bsz, nc, L, G, Hg)
    bm = bm.reshape(bsz, nc, L, G, N)
    cm = cm.reshape(bsz, nc, L, G, N)
    a_cum = jnp.cumsum(a, axis=2)
    tri = jnp.tril(jnp.ones((L, L), dtype=bool))[None, None, :, :, None, None]
    seg = a_cum[:, :, :, None] - a_cum[:, :, None, :]
    decay = jnp.exp(jnp.where(tri, seg, NEG_INF))
    cb = jnp.einsum('bctgn,bcsgn->bctsg', cm, bm)
    y_diag = jnp.einsum('bctsgh,bcsghp->bctghp', cb[..., None] * decay, x)
    decay_end = jnp.exp(a_cum[:, :, -1:] - a_cum)
    states = jnp.einsum('bcsgn,bcsghp->bcghpn', bm, x * decay_end[..., None])
    chunk_decay = jnp.exp(a_cum[:, :, -1])

    def step(h, inp):
        st, dec = inp
        return h * dec[..., None, None] + st, h

    h0 = jnp.zeros((bsz, G, Hg, P, N), jnp.float32)
    _, prev = lax.scan(step, h0, (states.transpose(1, 0, 2, 3, 4, 5), chunk_decay.transpose(1, 0, 2, 3)))
    prev = prev.transpose(1, 0, 2, 3, 4, 5)
    y_off = jnp.einsum('bctgn,bcghpn->bctghp', cm, prev) * jnp.exp(a_cum)[..., None]
    return (y_diag + y_off).reshape(bsz, s, H, P)


def mamba2_mixer(z, xbc, dt_raw, conv_w, conv_b, dt_bias, a_log, d_skip, norm_w):
    bsz, s, _ = xbc.shape
    G, N = SSM_GROUPS, SSM_STATE
    xpad = jnp.pad(xbc, ((0, 0), (SSM_CONV - 1, 0), (0, 0)))
    conv = conv_b
    for k in range(SSM_CONV):
        conv = conv + xpad[:, k:k + s] * conv_w[k]
    xbc = jax.nn.silu(conv.astype(jnp.float32))
    xs = xbc[..., :SSM_INNER].reshape(bsz, s, SSM_HEADS, SSM_HEAD_DIM)
    bm = xbc[..., SSM_INNER:SSM_INNER + G * N].reshape(bsz, s, G, N)
    cm = xbc[..., SSM_INNER + G * N:].reshape(bsz, s, G, N)
    dt = jax.nn.softplus(dt_raw.astype(jnp.float32) + dt_bias)
    a = -jnp.exp(a_log.astype(jnp.float32))
    y = ssd_chunked(xs * dt[..., None], dt * a, bm, cm) + xs * d_skip[:, None]
    y = y.reshape(bsz, s, SSM_INNER) * jax.nn.silu(z.astype(jnp.float32))
    y = rms_norm(y.reshape(bsz, s, G, SSM_INNER // G), norm_w.reshape(G, SSM_INNER // G))
    return y.reshape(bsz, s, SSM_INNER)


def hgrn2_mixer(q_raw, f_raw, i_raw, g_raw, lb, norm_w):
    bsz, s, _ = q_raw.shape
    C, dk, dv = HG_CHUNK, HG_KEY_DIM, HG_VAL_DIM
    nc = s // C
    q = jax.nn.silu(q_raw.astype(jnp.float32)) * dk ** -0.5
    fr = f_raw.astype(jnp.float32)
    log_f = jnp.log(lb + (1.0 - lb) * jax.nn.sigmoid(fr))
    k = (1.0 - lb) * jax.nn.sigmoid(-fr)
    v = i_raw.astype(jnp.float32)

    def to_chunks(t, d):
        return t.reshape(bsz, nc, C, HG_HEADS, d).transpose(1, 0, 3, 2, 4)

    tri = jnp.tril(jnp.ones((C, C), dtype=bool))[None, None, :, :, None]

    def step(state, inp):
        qc, kc, vc, gc = inp
        b = jnp.cumsum(gc, axis=2)
        o_inter = jnp.einsum('bhtd,bhde->bhte', qc * jnp.exp(b), state)
        seg = b[:, :, :, None, :] - b[:, :, None, :, :]
        dec = jnp.exp(jnp.where(tri, seg, NEG_INF))
        att = jnp.einsum('bhtd,bhsd,bhtsd->bhts', qc, kc, dec)
        o_intra = jnp.einsum('bhts,bhse->bhte', att, vc)
        b_last = b[:, :, -1:, :]
        state = state * jnp.exp(b_last[:, :, 0, :, None]) + jnp.einsum('bhsd,bhse->bhde', kc * jnp.exp(b_last - b), vc)
        return state, o_inter + o_intra

    s0 = jnp.zeros((bsz, HG_HEADS, dk, dv), jnp.float32)
    _, o = lax.scan(step, s0, (to_chunks(q, dk), to_chunks(k, dk), to_chunks(v, dv), to_chunks(log_f, dk)))
    o = o.transpose(1, 0, 3, 2, 4).reshape(bsz, s, HG_HEADS, dv)
    o = rms_norm(o, norm_w) * jax.nn.silu(g_raw.astype(jnp.float32)).reshape(bsz, s, HG_HEADS, dv)
    return o.reshape(bsz, s, HG_HEADS * dv)


def memory_cross_attention(h, mem, mem_norm, w_q, w_k, w_v, w_o, q_norm, k_norm):
    bsz, s, _ = h.shape
    m_len = mem.shape[1]
    q = rms_norm((h @ w_q).reshape(bsz, s, XA_HEADS, XA_HEAD_DIM), q_norm)
    m = rms_norm(mem, mem_norm)
    k = rms_norm((m @ w_k).reshape(bsz, m_len, XA_HEADS, XA_HEAD_DIM), k_norm)
    v = (m @ w_v).reshape(bsz, m_len, XA_HEADS, XA_HEAD_DIM)
    sc = jnp.einsum('bshd,bmhd->bhsm', q, k).astype(jnp.float32) * XA_HEAD_DIM ** -0.5
    p = jax.nn.softmax(sc, axis=-1)
    o = jnp.einsum('bhsm,bmhd->bshd', p, v)
    return o.reshape(bsz, s, XA_WIDTH) @ w_o


def swiglu_ffn(h, w_gate, w_up, w_down):
    return (jax.nn.silu(h @ w_gate) * (h @ w_up)) @ w_down


def setup_inputs(seed: int = 0) -> dict:
    key = jax.random.key(seed)
    keys = iter(jax.random.split(key, 64))

    def nrm(shape, scale):
        return jax.random.normal(next(keys), shape, jnp.float32) * scale

    def gain(shape):
        return 1.0 + nrm(shape, 0.05)

    L, D, hd = DEPTH, D_MODEL, NSA_HEAD_DIM
    dt0 = jnp.exp(jax.random.uniform(next(keys), (L, SSM_HEADS), jnp.float32, math.log(DT_MIN), math.log(DT_MAX)))
    a0 = jax.random.uniform(next(keys), (L, SSM_HEADS), jnp.float32, 1.0, 16.0)
    return {
        'x': nrm((BATCH, SEQ, D), 1.0),
        'mem': nrm((BATCH, MEM_LEN, D), 1.0),
        'norm_mix': gain((L, D)),
        'w_in': nrm((L, D, IN_DIM), D ** -0.5),
        'nsa_q_norm': gain((L, hd)),
        'nsa_k_norm': gain((L, hd)),
        'nsa_cmp_pos_k': nrm((L, NSA_CMP_BLOCK, hd), 0.1),
        'nsa_cmp_pos_v': nrm((L, NSA_CMP_BLOCK, hd), 0.1),
        'nsa_cmp_w1_k': nrm((L, NSA_CMP_BLOCK * hd, hd), (NSA_CMP_BLOCK * hd) ** -0.5),
        'nsa_cmp_w2_k': nrm((L, hd, hd), hd ** -0.5),
        'nsa_cmp_w1_v': nrm((L, NSA_CMP_BLOCK * hd, hd), (NSA_CMP_BLOCK * hd) ** -0.5),
        'nsa_cmp_w2_v': nrm((L, hd, hd), hd ** -0.5),
        'w_nsa_o': nrm((L, NSA_Q_WIDTH, D), NSA_Q_WIDTH ** -0.5),
        'ssm_conv_w': nrm((L, SSM_CONV, SSM_CONV_DIM), SSM_CONV ** -0.5),
        'ssm_conv_b': nrm((L, SSM_CONV_DIM), 0.02),
        'ssm_dt_bias': dt0 + jnp.log(-jnp.expm1(-dt0)),
        'ssm_a_log': jnp.log(a0),
        'ssm_d': 1.0 + nrm((L, SSM_HEADS), 0.1),
        'ssm_norm': gain((L, SSM_INNER)),
        'w_ssm_o': nrm((L, SSM_INNER, D), SSM_INNER ** -0.5),
        'hg_lb_logits': nrm((L, HG_WIDTH), 0.1),
        'hg_norm': gain((L, HG_VAL_DIM)),
        'w_hg_o': nrm((L, HG_HEADS * HG_VAL_DIM, D), (HG_HEADS * HG_VAL_DIM) ** -0.5),
        'w_out': nrm((L, D, D), D ** -0.5),
        'norm_xa': gain((L, D)),
        'norm_mem': gain((L, D)),
        'xa_w_q': nrm((L, D, XA_WIDTH), D ** -0.5),
        'xa_w_k': nrm((L, D, XA_WIDTH), D ** -0.5),
        'xa_w_v': nrm((L, D, XA_WIDTH), D ** -0.5),
        'xa_q_norm': gain((L, XA_HEAD_DIM)),
        'xa_k_norm': gain((L, XA_HEAD_DIM)),
        'xa_w_o': nrm((L, XA_WIDTH, D), XA_WIDTH ** -0.5),
        'norm_ffn': gain((L, D)),
        'ffn_w_gate': nrm((L, D, FFN_HIDDEN), D ** -0.5),
        'ffn_w_up': nrm((L, D, FFN_HIDDEN), D ** -0.5),
        'ffn_w_down': nrm((L, FFN_HIDDEN, D), FFN_HIDDEN ** -0.5),
    }


def reference(x, mem, norm_mix, w_in, nsa_q_norm, nsa_k_norm, nsa_cmp_pos_k, nsa_cmp_pos_v,
              nsa_cmp_w1_k, nsa_cmp_w2_k, nsa_cmp_w1_v, nsa_cmp_w2_v, w_nsa_o,
              ssm_conv_w, ssm_conv_b, ssm_dt_bias, ssm_a_log, ssm_d, ssm_norm, w_ssm_o,
              hg_lb_logits, hg_norm, w_hg_o, w_out,
              norm_xa, norm_mem, xa_w_q, xa_w_k, xa_w_v, xa_q_norm, xa_k_norm, xa_w_o,
              norm_ffn, ffn_w_gate, ffn_w_up, ffn_w_down):
    lb_sm = jax.nn.softmax(hg_lb_logits.astype(jnp.float32), axis=0)
    lb_all = jnp.cumsum(lb_sm, axis=0) - lb_sm[0:1]
    split_points = np.cumsum(SPLIT_SIZES)[:-1].tolist()
    for l in range(DEPTH):
        h = rms_norm(x, norm_mix[l])
        (a_q, a_kc, a_vc, a_ks, a_vs, a_kw, a_vw, a_gate,
         b_z, b_xbc, b_dt, c_q, c_f, c_i, c_g, g_a, g_b, g_c) = jnp.split(h @ w_in[l], split_points, axis=-1)
        y_a = nsa_mixer(a_q, a_kc, a_vc, a_ks, a_vs, a_kw, a_vw, a_gate, nsa_q_norm[l], nsa_k_norm[l],
                        nsa_cmp_pos_k[l], nsa_cmp_pos_v[l], nsa_cmp_w1_k[l], nsa_cmp_w2_k[l],
                        nsa_cmp_w1_v[l], nsa_cmp_w2_v[l]) @ w_nsa_o[l]
        y_b = mamba2_mixer(b_z, b_xbc, b_dt, ssm_conv_w[l], ssm_conv_b[l], ssm_dt_bias[l],
                           ssm_a_log[l], ssm_d[l], ssm_norm[l]) @ w_ssm_o[l]
        y_c = hgrn2_mixer(c_q, c_f, c_i, c_g, lb_all[l], hg_norm[l]) @ w_hg_o[l]
        merged = jax.nn.sigmoid(g_a) * y_a + jax.nn.sigmoid(g_b) * y_b + jax.nn.sigmoid(g_c) * y_c
        x = x + (merged @ w_out[l]).astype(x.dtype)
        x = x + memory_cross_attention(rms_norm(x, norm_xa[l]), mem, norm_mem[l], xa_w_q[l], xa_w_k[l],
                                       xa_w_v[l], xa_w_o[l], xa_q_norm[l], xa_k_norm[l]).astype(x.dtype)
        x = x + swiglu_ffn(rms_norm(x, norm_ffn[l]), ffn_w_gate[l], ffn_w_up[l], ffn_w_down[l]).astype(x.dtype)
    return x
```

```python
import functools
import math

import numpy as np
import jax
import jax.numpy as jnp
from jax import lax
from jax.experimental import pallas as pl
from jax.experimental.pallas import tpu as pltpu

F32 = jnp.float32
BF16 = jnp.bfloat16

D_MODEL = 1024
NORM_EPS = 1e-6
ROPE_THETA = 10000.0
NEG_INF = -1e30
FORCE_SCORE = 1e9

NSA_HEADS = 8
NSA_KV_GROUPS = 2
NSA_REP = NSA_HEADS // NSA_KV_GROUPS
NSA_HEAD_DIM = 64
NSA_CMP_BLOCK = 32
NSA_CMP_STRIDE = 16
NSA_SEL_BLOCK = 64
NSA_TOPK = 16
NSA_WINDOW = 512
NSA_Q_WIDTH = NSA_HEADS * NSA_HEAD_DIM
NSA_KV_WIDTH = NSA_KV_GROUPS * NSA_HEAD_DIM

SSM_INNER = 512
SSM_HEAD_DIM = 64
SSM_HEADS = SSM_INNER // SSM_HEAD_DIM
SSM_GROUPS = 2
SSM_STATE = 64
SSM_CONV = 4
SSM_CONV_DIM = SSM_INNER + 2 * SSM_GROUPS * SSM_STATE

HG_HEADS = 4
HG_KEY_DIM = 128
HG_VAL_DIM = 128
HG_WIDTH = HG_HEADS * HG_KEY_DIM
HG_SUB = 16

XA_HEADS = 4
XA_HEAD_DIM = 128
XA_WIDTH = XA_HEADS * XA_HEAD_DIM

FFN_HIDDEN = -(-(8 * D_MODEL) // (3 * 256)) * 256

SPLIT_SIZES = (
    NSA_Q_WIDTH, NSA_KV_WIDTH, NSA_KV_WIDTH, NSA_KV_WIDTH, NSA_KV_WIDTH, NSA_KV_WIDTH, NSA_KV_WIDTH,
    3 * NSA_HEADS, SSM_INNER, SSM_CONV_DIM, SSM_HEADS, HG_WIDTH, HG_WIDTH, HG_HEADS * HG_VAL_DIM,
    HG_HEADS * HG_VAL_DIM, D_MODEL, D_MODEL, D_MODEL,
)

LANES = 128
TOK = 128
VMEM_LIMIT = 56 * 1024 * 1024

PROJ_WIDTH = 8192
COL_GA, COL_GB, COL_GC = 0, 1024, 2048
COL_Q = 3072
COL_Z = 3584
COL_HQ, COL_HF, COL_HI, COL_HG = 4096, 4608, 5120, 5632
COL_XBC = 6144
COL_KC, COL_VC, COL_KS, COL_VS, COL_KW, COL_VW = 6912, 7040, 7168, 7296, 7424, 7552
COL_NG = 7680
COL_DT = 7808


def _cparams(sem):
    return pltpu.CompilerParams(dimension_semantics=sem, vmem_limit_bytes=VMEM_LIMIT)


def _dot(a, b):
    return jnp.dot(a, b, preferred_element_type=F32)


def _dot_nt(a, b):
    return lax.dot_general(a, b, (((1,), (1,)), ((), ())), preferred_element_type=F32)


def _split3(a):
    hi = a.astype(BF16)
    r1 = a - hi.astype(F32)
    mid = r1.astype(BF16)
    lo = (r1 - mid.astype(F32)).astype(BF16)
    return hi, mid, lo


def _dot_sel_r(a, sel):
    hi, mid, lo = _split3(a)
    return _dot(hi, sel) + _dot(mid, sel) + _dot(lo, sel)


def _dot_sel_l(sel, a):
    hi, mid, lo = _split3(a)
    return _dot(sel, hi) + _dot(sel, mid) + _dot(sel, lo)


def _silu(x):
    return x * jax.nn.sigmoid(x)


def _rms(x, w):
    return x * lax.rsqrt(jnp.mean(x * x, axis=-1, keepdims=True) + NORM_EPS) * w


def _inproj_kernel(x_ref, nw_ref, w_ref, o_ref, h_ref):
    @pl.when(pl.program_id(1) == 0)
    def _():
        h_ref[...] = _rms(x_ref[...], nw_ref[...]).astype(BF16)

    o_ref[...] = _dot(h_ref[...], w_ref[...])


def _in_proj(x2, norm_w, w_packed):
    t = x2.shape[0]
    tm = min(1024, t)
    tn = 1024
    return pl.pallas_call(
        _inproj_kernel,
        grid=(t // tm, PROJ_WIDTH // tn),
        in_specs=[
            pl.BlockSpec((tm, D_MODEL), lambda i, j: (i, 0)),
            pl.BlockSpec((1, D_MODEL), lambda i, j: (0, 0)),
            pl.BlockSpec((D_MODEL, tn), lambda i, j: (0, j)),
        ],
        out_specs=pl.BlockSpec((tm, tn), lambda i, j: (i, j)),
        out_shape=jax.ShapeDtypeStruct((t, PROJ_WIDTH), F32),
        scratch_shapes=[pltpu.VMEM((tm, D_MODEL), BF16)],
        compiler_params=_cparams(("parallel", "arbitrary")),
        name="in_proj",
    )(x2, norm_w, w_packed)


def _norm_rope_pair(x, w, cos, sin):
    lane = lax.broadcasted_iota(jnp.int32, x.shape, 1)
    lo = lane < NSA_HEAD_DIM
    sq = x * x
    s_lo = jnp.sum(jnp.where(lo, sq, 0.0), axis=1, keepdims=True)
    s_hi = jnp.sum(jnp.where(lo, 0.0, sq), axis=1, keepdims=True)
    ms = jnp.where(lo, s_lo, s_hi) * (1.0 / NSA_HEAD_DIM)
    y = x * lax.rsqrt(ms + NORM_EPS) * w
    half = NSA_HEAD_DIM // 2
    first = (lane % NSA_HEAD_DIM) < half
    rot = jnp.where(first, -pltpu.roll(y, LANES - half, 1), pltpu.roll(y, half, 1))
    return y * cos + rot * sin


def _nsa_prep_kernel(q_ref, ks_ref, vs_ref, kw_ref, vw_ref, cos_ref, sin_ref, qn_ref, kn_ref,
                     qp_ref, ksn_ref, kwn_ref, vst_ref, vwt_ref):
    cos = cos_ref[...]
    sin = sin_ref[...]
    lane = lax.broadcasted_iota(jnp.int32, (TOK, LANES), 1)
    scale = NSA_HEAD_DIM ** -0.5
    for c in range(NSA_HEADS // 2):
        y = _norm_rope_pair(q_ref[:, c * LANES:(c + 1) * LANES], qn_ref[...], cos, sin) * scale
        y_sw = pltpu.roll(y, NSA_HEAD_DIM, 1)
        for hh in range(2):
            h = 2 * c + hh
            g, r = divmod(h, NSA_REP)
            src = y if hh == g else y_sw
            keep = (lane // NSA_HEAD_DIM) == g
            qp_ref[0, g, :, r * LANES:(r + 1) * LANES] = jnp.where(keep, src, 0.0).astype(BF16)
    ksn_ref[0] = _norm_rope_pair(ks_ref[...], kn_ref[...], cos, sin).astype(BF16)
    kwn_ref[0] = _norm_rope_pair(kw_ref[...], kn_ref[...], cos, sin).astype(BF16)
    vst_ref[0, 0] = vs_ref[...].T.astype(BF16)
    vwt_ref[0, 0] = vw_ref[...].T.astype(BF16)


def _nsa_prep(proj, cos_t, sin_t, qn2, kn2, bsz, s):
    nt = s // TOK

    def col(c0, w):
        return pl.BlockSpec((TOK, w), lambda b, i: (b * nt + i, c0 // w))

    tab = pl.BlockSpec((TOK, LANES), lambda b, i: (i, 0))
    vec = pl.BlockSpec((1, LANES), lambda b, i: (0, 0))
    return pl.pallas_call(
        _nsa_prep_kernel,
        grid=(bsz, nt),
        in_specs=[col(COL_Q, NSA_Q_WIDTH), col(COL_KS, LANES), col(COL_VS, LANES), col(COL_KW, LANES),
                  col(COL_VW, LANES), tab, tab, vec, vec],
        out_specs=[
            pl.BlockSpec((1, NSA_KV_GROUPS, TOK, NSA_REP * LANES), lambda b, i: (b, 0, i, 0)),
            pl.BlockSpec((1, TOK, LANES), lambda b, i: (b, i, 0)),
            pl.BlockSpec((1, TOK, LANES), lambda b, i: (b, i, 0)),
            pl.BlockSpec((1, 1, LANES, TOK), lambda b, i: (b, i, 0, 0)),
            pl.BlockSpec((1, 1, LANES, TOK), lambda b, i: (b, i, 0, 0)),
        ],
        out_shape=[
            jax.ShapeDtypeStruct((bsz, NSA_KV_GROUPS, s, NSA_REP * LANES), BF16),
            jax.ShapeDtypeStruct((bsz, s, LANES), BF16),
            jax.ShapeDtypeStruct((bsz, s, LANES), BF16),
            jax.ShapeDtypeStruct((bsz, nt, LANES, TOK), BF16),
            jax.ShapeDtypeStruct((bsz, nt, LANES, TOK), BF16),
        ],
        compiler_params=_cparams(("parallel", "parallel")),
        name="nsa_prep",
    )(proj, proj, proj, proj, proj, cos_t, sin_t, qn2, kn2)


def _compress_kernel(kc_ref, vc_ref, posk_ref, posv_ref, w1k_ref, w2k_ref, w1v_ref, w2v_ref, kn_ref,
                     cos_ref, sin_ref, kco_ref, vct_ref, sh_ref, *, nbp):
    half_blk = NSA_CMP_BLOCK // 2

    def compress(x_ref, pos_ref, w1_ref, w2_ref):
        acc_a = jnp.zeros((nbp, LANES), F32)
        acc_b = jnp.zeros((nbp, LANES), F32)
        for l in range(half_blk):
            xl = x_ref[pl.ds(l, nbp, stride=NSA_CMP_STRIDE), :]
            acc_a = acc_a + _dot((xl + pos_ref[l:l + 1, :]).astype(BF16), w1_ref[l])
            acc_b = acc_b + _dot((xl + pos_ref[half_blk + l:half_blk + l + 1, :]).astype(BF16),
                                 w1_ref[half_blk + l])
        sh_ref[pl.ds(0, nbp), :] = acc_b
        sh_ref[pl.ds(nbp, 8), :] = jnp.zeros((8, LANES), F32)
        hid = acc_a + sh_ref[pl.ds(1, nbp), :]
        return _dot(_silu(hid).astype(BF16), w2_ref[...])

    kc = compress(kc_ref, posk_ref, w1k_ref, w2k_ref)
    kco_ref[0] = _norm_rope_pair(kc, kn_ref[...], cos_ref[...], sin_ref[...]).astype(BF16)
    vc = compress(vc_ref, posv_ref, w1v_ref, w2v_ref)
    vct_ref[0] = vc.T.astype(BF16)


def _compress(proj, posk, posv, w1k, w2k, w1v, w2v, kn2, cos_c, sin_c, bsz, s):
    nbp = s // NSA_CMP_STRIDE

    def full(shape):
        return pl.BlockSpec(shape, lambda b: (0,) * len(shape))

    return pl.pallas_call(
        functools.partial(_compress_kernel, nbp=nbp),
        grid=(bsz,),
        in_specs=[
            pl.BlockSpec((s, LANES), lambda b: (b, COL_KC // LANES)),
            pl.BlockSpec((s, LANES), lambda b: (b, COL_VC // LANES)),
            full((NSA_CMP_BLOCK, LANES)), full((NSA_CMP_BLOCK, LANES)),
            full((NSA_CMP_BLOCK, LANES, LANES)), full((LANES, LANES)),
            full((NSA_CMP_BLOCK, LANES, LANES)), full((LANES, LANES)),
            full((1, LANES)), full((nbp, LANES)), full((nbp, LANES)),
        ],
        out_specs=[
            pl.BlockSpec((1, nbp, LANES), lambda b: (b, 0, 0)),
            pl.BlockSpec((1, LANES, nbp), lambda b: (b, 0, 0)),
        ],
        out_shape=[
            jax.ShapeDtypeStruct((bsz, nbp, LANES), BF16),
            jax.ShapeDtypeStruct((bsz, LANES, nbp), BF16),
        ],
        scratch_shapes=[pltpu.VMEM((nbp + 8, LANES), F32)],
        compiler_params=_cparams(("parallel",)),
        name="nsa_compress",
    )(proj, proj, posk, posv, w1k, w2k, w1v, w2v, kn2, cos_c, sin_c)


def _q_rows(qp):
    return jnp.concatenate([qp[:, r * LANES:(r + 1) * LANES] for r in range(NSA_REP)], axis=0)


def _heads_to_token_major(acc_g):
    outs = []
    for p in range(NSA_REP // 2):
        blk = jnp.concatenate([acc_g[:, (2 * p) * TOK:(2 * p + 1) * TOK],
                               acc_g[:, (2 * p + 1) * TOK:(2 * p + 2) * TOK]], axis=0)
        outs.append(blk.T)
    return jnp.concatenate(outs, axis=1)


def _cmp_topk_kernel(qp_ref, kc_ref, vct_ref, oc_ref, mask_ref, ps_ref, *, nbp, nsel):
    qi = pl.program_id(1)
    ncol = NSA_REP * TOK
    n_io = lax.broadcasted_iota(jnp.int32, (nbp, ncol), 0)
    t_io = qi * TOK + lax.broadcasted_iota(jnp.int32, (nbp, ncol), 1) % TOK
    allowed = n_io * NSA_CMP_STRIDE + (NSA_CMP_BLOCK - 1) <= t_io
    j_io = lax.broadcasted_iota(jnp.int32, (nsel, TOK), 0)
    t_sel = qi * TOK + lax.broadcasted_iota(jnp.int32, (nsel, TOK), 1)
    cur = t_sel // NSA_SEL_BLOCK
    forced = (j_io == 0) | (j_io == cur) | (j_io == cur - 1)
    valid = j_io * NSA_SEL_BLOCK <= t_sel
    kc = kc_ref[0]
    vct = vct_ref[0]
    ps_ref[pl.ds(0, 8), :] = jnp.zeros((8, TOK), F32)
    outs = []
    for g in range(NSA_KV_GROUPS):
        s_t = _dot_nt(kc, _q_rows(qp_ref[0, g]))
        s_t = jnp.where(allowed, s_t, NEG_INF)
        m = jnp.max(s_t, axis=0, keepdims=True)
        p = jnp.where(allowed, jnp.exp(s_t - m), 0.0)
        l = jnp.sum(p, axis=0, keepdims=True)
        p = p * jnp.where(l > 0.0, 1.0 / l, 0.0)
        o_t = _dot(vct, p.astype(BF16))
        outs.append(_heads_to_token_major(o_t[g * NSA_HEAD_DIM:(g + 1) * NSA_HEAD_DIM, :]))
        psum = p[:, 0:TOK]
        for r in range(1, NSA_REP):
            psum = psum + p[:, r * TOK:(r + 1) * TOK]
        ps_ref[pl.ds(8, nbp), :] = psum
        per = NSA_SEL_BLOCK // NSA_CMP_STRIDE
        imp = ps_ref[pl.ds(7, nsel, stride=per), :] + ps_ref[pl.ds(8 + per - 1, nsel, stride=per), :]
        for c in range(per - 1):
            imp = imp + 2.0 * ps_ref[pl.ds(8 + c, nsel, stride=per), :]
        imp = jnp.where(forced, FORCE_SCORE, jnp.where(valid, imp, NEG_INF))

        def pick(_, carry):
            imp_c, sel_c = carry
            mx = jnp.max(imp_c, axis=0, keepdims=True)
            idx = jnp.min(jnp.where(imp_c == mx, j_io, nsel), axis=0, keepdims=True)
            hit = j_io == idx
            return jnp.where(hit, -jnp.inf, imp_c), jnp.where(hit, 1.0, sel_c)

        _, sel = lax.fori_loop(0, min(NSA_TOPK, nsel), pick, (imp, jnp.zeros((nsel, TOK), F32)))
        mask_ref[0, g, 0] = jnp.where(valid, sel, 0.0)
    oc_ref[0] = jnp.concatenate(outs, axis=1)


def _cmp_topk(qp, kc, vct, bsz, s):
    nt = s // TOK
    nbp = s // NSA_CMP_STRIDE
    nsel = s // NSA_SEL_BLOCK
    return pl.pallas_call(
        functools.partial(_cmp_topk_kernel, nbp=nbp, nsel=nsel),
        grid=(bsz, nt),
        in_specs=[
            pl.BlockSpec((1, NSA_KV_GROUPS, TOK, NSA_REP * LANES), lambda b, i: (b, 0, i, 0)),
            pl.BlockSpec((1, nbp, LANES), lambda b, i: (b, 0, 0)),
            pl.BlockSpec((1, LANES, nbp), lambda b, i: (b, 0, 0)),
        ],
        out_specs=[
            pl.BlockSpec((1, TOK, NSA_Q_WIDTH), lambda b, i: (b, i, 0)),
            pl.BlockSpec((1, NSA_KV_GROUPS, 1, nsel, TOK), lambda b, i: (b, 0, i, 0, 0)),
        ],
        out_shape=[
            jax.ShapeDtypeStruct((bsz, s, NSA_Q_WIDTH), F32),
            jax.ShapeDtypeStruct((bsz, NSA_KV_GROUPS, nt, nsel, TOK), F32),
        ],
        scratch_shapes=[pltpu.VMEM((nbp + 8, TOK), F32)],
        compiler_params=_cparams(("parallel", "parallel")),
        name="nsa_cmp_topk",
    )(qp, kc, vct)


def _flash_kernel(*refs, selected, nt):
    if selected:
        flags_ref, qp_ref, k_ref, vt_ref, mask_ref, o_ref, m_ref, l_ref, acc_ref = refs
    else:
        qp_ref, k_ref, vt_ref, o_ref, m_ref, l_ref, acc_ref = refs
    b = pl.program_id(0)
    g = pl.program_id(1)
    qi = pl.program_id(2)
    ncol = NSA_REP * TOK
    q_rows = _q_rows(qp_ref[0, 0])
    m_ref[...] = jnp.full((1, ncol), NEG_INF, F32)
    l_ref[...] = jnp.zeros((1, ncol), F32)
    acc_ref[...] = jnp.zeros((LANES, ncol), F32)
    t_io = qi * TOK + lax.broadcasted_iota(jnp.int32, (TOK, ncol), 1) % TOK
    sub = lax.broadcasted_iota(jnp.int32, (TOK, ncol), 0)
    row8 = lax.broadcasted_iota(jnp.int32, (8, TOK), 0)
    words = TOK // 32

    def tile(j):
        kb = k_ref[0, pl.ds(pl.multiple_of(j * TOK, TOK), TOK), :]
        s_t = _dot_nt(kb, q_rows)
        key_pos = j * TOK + sub
        allowed = key_pos <= t_io
        if selected:
            m8 = mask_ref[0, 0, 0, pl.ds(pl.multiple_of((j // 4) * 8, 8), 8), :]
            r0 = (2 * j) % 8
            m0 = jnp.sum(jnp.where(row8 == r0, m8, 0.0), axis=0, keepdims=True)
            m1 = jnp.sum(jnp.where(row8 == r0 + 1, m8, 0.0), axis=0, keepdims=True)
            m0 = jnp.concatenate([m0] * NSA_REP, axis=1)
            m1 = jnp.concatenate([m1] * NSA_REP, axis=1)
            allowed = allowed & (jnp.where(sub < NSA_SEL_BLOCK, m0, m1) > 0.5)
        else:
            allowed = allowed & (t_io - key_pos < NSA_WINDOW)
        s_t = jnp.where(allowed, s_t, NEG_INF)
        m_old = m_ref[...]
        m_new = jnp.maximum(m_old, jnp.max(s_t, axis=0, keepdims=True))
        alpha = jnp.exp(m_old - m_new)
        p = jnp.where(allowed, jnp.exp(s_t - m_new), 0.0)
        l_ref[...] = alpha * l_ref[...] + jnp.sum(p, axis=0, keepdims=True)
        acc_ref[...] = acc_ref[...] * alpha + _dot(vt_ref[0, j], p.astype(BF16))
        m_ref[...] = m_new

    if selected:
        base = ((b * NSA_KV_GROUPS + g) * nt + qi) * words

        def body(j, c):
            bit = (flags_ref[base + j // 32] >> (j % 32)) & 1

            @pl.when(bit == 1)
            def _():
                tile(j)
            return c

        lax.fori_loop(0, qi + 1, body, 0)
    else:
        def body(j, c):
            tile(j)
            return c

        lax.fori_loop(jnp.maximum(qi - NSA_WINDOW // TOK, 0), qi + 1, body, 0)

    acc_g = acc_ref[pl.ds(pl.multiple_of(g * NSA_HEAD_DIM, NSA_HEAD_DIM), NSA_HEAD_DIM), :]
    o_ref[0] = _heads_to_token_major(acc_g / l_ref[...])


def _flash(qp, k, vt, bsz, s, mask=None, flags=None):
    nt = s // TOK
    selected = mask is not None
    nsel = s // NSA_SEL_BLOCK
    gw = NSA_REP * NSA_HEAD_DIM

    def im(f):
        return (lambda b, g, i, fl: f(b, g, i)) if selected else f

    in_specs = [
        pl.BlockSpec((1, 1, TOK, NSA_REP * LANES), im(lambda b, g, i: (b, g, i, 0))),
        pl.BlockSpec((1, s, LANES), im(lambda b, g, i: (b, 0, 0))),
        pl.BlockSpec((1, nt, LANES, TOK), im(lambda b, g, i: (b, 0, 0, 0))),
    ]
    args = [qp, k, vt]
    if selected:
        in_specs.append(pl.BlockSpec((1, 1, 1, nsel, TOK), im(lambda b, g, i: (b, g, i, 0, 0))))
        args = [flags] + args + [mask]
    grid_spec = pltpu.PrefetchScalarGridSpec(
        num_scalar_prefetch=1 if selected else 0,
        grid=(bsz, NSA_KV_GROUPS, nt),
        in_specs=in_specs,
        out_specs=pl.BlockSpec((1, TOK, gw), im(lambda b, g, i: (b, i, g))),
        scratch_shapes=[pltpu.VMEM((1, NSA_REP * TOK), F32), pltpu.VMEM((1, NSA_REP * TOK), F32),
                        pltpu.VMEM((LANES, NSA_REP * TOK), F32)],
    )
    return pl.pallas_call(
        functools.partial(_flash_kernel, selected=selected, nt=nt),
        grid_spec=grid_spec,
        out_shape=jax.ShapeDtypeStruct((bsz, s, NSA_Q_WIDTH), F32),
        compiler_params=_cparams(("parallel", "parallel", "parallel")),
        name="nsa_selected" if selected else "nsa_window",
    )(*args)


def _pair_flags(mask, bsz, s):
    nt = s // TOK
    act = mask.reshape(bsz, NSA_KV_GROUPS, nt, nt, 2 * TOK).max(axis=-1) > 0.5
    act = jnp.pad(act, ((0, 0), (0, 0), (0, 0), (0, TOK - nt))) if nt < TOK else act
    bits = act.reshape(bsz, NSA_KV_GROUPS, nt, TOK // 32, 32).astype(jnp.uint32)
    words = jnp.sum(bits << jnp.arange(32, dtype=jnp.uint32), axis=-1, dtype=jnp.uint32)
    return lax.bitcast_convert_type(words, jnp.int32).reshape(-1)


def _ssd_kernel(z_ref, xbc_ref, dt_ref, cw_ref, cb_ref, dtb_ref, alog_ref, dskip_ref, nw_ref, e8_ref,
                tri_ref, y_ref, ext_ref, st_ref):
    @pl.when(pl.program_id(1) == 0)
    def _():
        ext_ref[pl.ds(0, 8), :] = jnp.zeros((8, SSM_CONV_DIM), F32)
        st_ref[...] = jnp.zeros_like(st_ref)

    xb = xbc_ref[...]
    ext_ref[pl.ds(8, TOK), :] = xb
    conv = cb_ref[...] + xb * cw_ref[SSM_CONV - 1:SSM_CONV, :]
    for k in range(SSM_CONV - 1):
        conv = conv + ext_ref[pl.ds(8 - (SSM_CONV - 1) + k, TOK), :] * cw_ref[k:k + 1, :]
    ext_ref[pl.ds(0, 8), :] = xb[TOK - 8:TOK, :]
    u = _silu(conv)
    xs = u[:, :SSM_INNER]
    gn = SSM_GROUPS * SSM_STATE
    bm = u[:, SSM_INNER:SSM_INNER + gn]
    cm = u[:, SSM_INNER + gn:]
    dt = jax.nn.softplus(dt_ref[...] + dtb_ref[...])
    da = dt * (-jnp.exp(alog_ref[...]))
    tri = tri_ref[...]
    e8 = e8_ref[...]
    cum = _dot_sel_l(tri, da)
    cum_t = cum.T
    cum_e = _dot_sel_r(cum, e8)
    dt_e = _dot_sel_r(dt, e8)
    xdt = xs * dt_e
    lane = lax.broadcasted_iota(jnp.int32, (TOK, LANES), 1)
    t_io = lax.broadcasted_iota(jnp.int32, (TOK, TOK), 0)
    s_io = lax.broadcasted_iota(jnp.int32, (TOK, TOK), 1)
    causal = s_io <= t_io
    hg = SSM_HEADS // SSM_GROUPS
    y_parts = []
    cbs = []
    for g in range(SSM_GROUPS):
        in_g = (lane // SSM_STATE) == g
        cbs.append(_dot_nt(jnp.where(in_g, cm, 0.0).astype(BF16), bm.astype(BF16)))
    for c in range(SSM_HEADS // 2):
        acc = jnp.zeros((TOK, LANES), F32)
        xpair = xdt[:, c * LANES:(c + 1) * LANES]
        for hh in range(2):
            h = 2 * c + hh
            seg = cum[:, h:h + 1] - cum_t[h:h + 1, :]
            decay = jnp.where(causal, jnp.exp(jnp.where(causal, seg, 0.0)), 0.0)
            mat = (cbs[h // hg] * decay).astype(BF16)
            xh = jnp.where((lane // SSM_HEAD_DIM) == hh, xpair, 0.0).astype(BF16)
            acc = acc + _dot(mat, xh)
        y_parts.append(acc)
    y_diag = jnp.concatenate(y_parts, axis=1)
    prev = st_ref[...]
    y_off = _dot(cm.astype(BF16), prev.astype(BF16)) * jnp.exp(cum_e)
    cum_last = cum_e[TOK - 1:TOK, :]
    xdec = (xdt * jnp.exp(cum_last - cum_e)).astype(BF16)
    new = _dot(bm.T.astype(BF16), xdec)
    row_g = lax.broadcasted_iota(jnp.int32, (LANES, SSM_INNER), 0) // SSM_STATE
    col_g = lax.broadcasted_iota(jnp.int32, (LANES, SSM_INNER), 1) // (hg * SSM_HEAD_DIM)
    st_ref[...] = prev * jnp.exp(cum_last) + jnp.where(row_g == col_g, new, 0.0)
    y = (y_diag + y_off + xs * dskip_ref[...]) * _silu(z_ref[...])
    gw = SSM_INNER // SSM_GROUPS
    outs = []
    for g in range(SSM_GROUPS):
        yg = y[:, g * gw:(g + 1) * gw]
        outs.append(_rms(yg, nw_ref[:, g * gw:(g + 1) * gw]))
    y_ref[...] = jnp.concatenate(outs, axis=1)


def _ssd(proj, cw, cb, dtb, alog, dskip, nw, e8, tri, bsz, s):
    nt = s // TOK

    def col(c0, w):
        return pl.BlockSpec((TOK, w), lambda b, i: (b * nt + i, c0 // w))

    def full(shape):
        return pl.BlockSpec(shape, lambda b, i: (0,) * len(shape))

    return pl.pallas_call(
        _ssd_kernel,
        grid=(bsz, nt),
        in_specs=[col(COL_Z, SSM_INNER), col(COL_XBC, SSM_CONV_DIM), col(COL_DT, LANES),
                  full((SSM_CONV, SSM_CONV_DIM)), full((1, SSM_CONV_DIM)), full((1, LANES)), full((1, LANES)),
                  full((1, SSM_INNER)), full((1, SSM_INNER)), full((LANES, SSM_INNER)), full((TOK, TOK))],
        out_specs=pl.BlockSpec((TOK, SSM_INNER), lambda b, i: (b * nt + i, 0)),
        out_shape=jax.ShapeDtypeStruct((bsz * s, SSM_INNER), F32),
        scratch_shapes=[pltpu.VMEM((TOK + 8, SSM_CONV_DIM), F32), pltpu.VMEM((LANES, SSM_INNER), F32)],
        compiler_params=_cparams(("parallel", "arbitrary")),
        name="ssd",
    )(proj, proj, proj, cw, cb, dtb, alog, dskip, nw, e8, tri)


def _hgrn2_kernel(q_ref, f_ref, i_ref, g_ref, lb_ref, nw_ref, tri_ref, y_ref, st_ref, kp_ref, bp_ref, vp_ref):
    @pl.when(pl.program_id(1) == 0)
    def _():
        st_ref[...] = jnp.zeros_like(st_ref)
        kp_ref[...] = jnp.zeros_like(kp_ref)
        bp_ref[...] = jnp.zeros_like(bp_ref)
        vp_ref[...] = jnp.zeros_like(vp_ref)

    nsub = TOK // HG_SUB
    row = lax.broadcasted_iota(jnp.int32, (TOK, HG_KEY_DIM), 0)
    sub_id = row // HG_SUB
    in_sub = row % HG_SUB
    tri = tri_ref[...]
    outs = []
    for h in range(HG_HEADS):
        sl = slice(h * HG_KEY_DIM, (h + 1) * HG_KEY_DIM)
        lb = lb_ref[:, sl]
        fr = f_ref[:, sl]
        q = _silu(q_ref[:, sl]) * (HG_KEY_DIM ** -0.5)
        log_f = jnp.log(lb + (1.0 - lb) * jax.nn.sigmoid(fr))
        k = (1.0 - lb) * jax.nn.sigmoid(-fr)
        v = i_ref[:, sl]
        bcum = _dot_sel_l(tri, log_f)
        st = st_ref[h]
        o = _dot_nt((q * jnp.exp(bcum)).astype(BF16), st.astype(BF16))
        r_rows = [bcum[i * HG_SUB - 1:i * HG_SUB, :] for i in range(1, nsub)]
        r_loc = jnp.zeros_like(bcum)
        for i in range(1, nsub):
            r_loc = jnp.where(sub_id == i, r_rows[i - 1], r_loc)
        q_loc = q * jnp.exp(bcum - r_loc)
        q_aug = jnp.concatenate(
            [jnp.where(sub_id == i, q_loc, 0.0).astype(BF16) for i in range(1, nsub)], axis=1)
        k_aug = jnp.concatenate(
            [jnp.where(sub_id < i, k * jnp.exp(jnp.where(sub_id < i, r_rows[i - 1] - bcum, 0.0)), 0.0).astype(BF16)
             for i in range(1, nsub)], axis=1)
        att = _dot_nt(q_aug, k_aug)
        o = o + _dot(att.astype(BF16), v.astype(BF16))
        kp_ref[pl.ds(HG_SUB, TOK), :] = k
        bp_ref[pl.ds(HG_SUB, TOK), :] = bcum
        vp_ref[pl.ds(HG_SUB, TOK), :] = v
        o = o + jnp.sum(q * k, axis=1, keepdims=True) * v
        for d in range(1, HG_SUB):
            ks = kp_ref[pl.ds(HG_SUB - d, TOK), :]
            bs = bp_ref[pl.ds(HG_SUB - d, TOK), :]
            vs = vp_ref[pl.ds(HG_SUB - d, TOK), :]
            ok = in_sub >= d
            w = jnp.sum(q * ks * jnp.exp(jnp.where(ok, bcum - bs, 0.0)), axis=1, keepdims=True)
            o = o + jnp.where(ok, w * vs, 0.0)
        b_last = bcum[TOK - 1:TOK, :]
        kdec = (k * jnp.exp(b_last - bcum)).astype(BF16)
        st_ref[h] = st * jnp.exp(b_last) + _dot(v.T.astype(BF16), kdec)
        outs.append(_rms(o, nw_ref[...]) * _silu(g_ref[:, sl]))
    y_ref[...] = jnp.concatenate(outs, axis=1)


def _hgrn2(proj, lb, nw, tri, bsz, s):
    nt = s // TOK

    def col(c0):
        return pl.BlockSpec((TOK, HG_WIDTH), lambda b, i: (b * nt + i, c0 // HG_WIDTH))

    def full(shape):
        return pl.BlockSpec(shape, lambda b, i: (0,) * len(shape))

    pad = pltpu.VMEM((TOK + HG_SUB, HG_KEY_DIM), F32)
    return pl.pallas_call(
        _hgrn2_kernel,
        grid=(bsz, nt),
        in_specs=[col(COL_HQ), col(COL_HF), col(COL_HI), col(COL_HG), full((1, HG_WIDTH)),
                  full((1, HG_VAL_DIM)), full((TOK, TOK))],
        out_specs=pl.BlockSpec((TOK, HG_WIDTH), lambda b, i: (b * nt + i, 0)),
        out_shape=jax.ShapeDtypeStruct((bsz * s, HG_WIDTH), F32),
        scratch_shapes=[pltpu.VMEM((HG_HEADS, HG_VAL_DIM, HG_KEY_DIM), F32), pad, pad, pad],
        compiler_params=_cparams(("parallel", "arbitrary")),
        name="hgrn2",
    )(proj, proj, proj, proj, lb, nw, tri)


def _merge_kernel(ga_ref, gb_ref, gc_ref, ng_ref, oc_ref, os_ref, ow_ref, yb_ref, yc_ref, x_ref,
                  eg_ref, wa_ref, wb_ref, wc_ref, wo_ref, o_ref):
    sg = jax.nn.sigmoid(ng_ref[...])
    nsa = (_dot_sel_r(sg, eg_ref[0]) * oc_ref[...] + _dot_sel_r(sg, eg_ref[1]) * os_ref[...]
           + _dot_sel_r(sg, eg_ref[2]) * ow_ref[...])
    ya = _dot(nsa.astype(BF16), wa_ref[...])
    yb = _dot(yb_ref[...].astype(BF16), wb_ref[...])
    yc = _dot(yc_ref[...].astype(BF16), wc_ref[...])
    merged = (jax.nn.sigmoid(ga_ref[...]) * ya + jax.nn.sigmoid(gb_ref[...]) * yb
              + jax.nn.sigmoid(gc_ref[...]) * yc)
    o_ref[...] = x_ref[...] + _dot(merged.astype(BF16), wo_ref[...])


def _merge(proj, oc, os_, ow, yb, yc, x2, eg, wa, wb, wc, wo):
    t = x2.shape[0]
    tm = min(512, t)

    def col(c0, w):
        return pl.BlockSpec((tm, w), lambda i: (i, c0 // w))

    def full(shape):
        return pl.BlockSpec(shape, lambda i: (0,) * len(shape))

    row512 = pl.BlockSpec((tm, 512), lambda i: (i, 0))
    return pl.pallas_call(
        _merge_kernel,
        grid=(t // tm,),
        in_specs=[col(COL_GA, D_MODEL), col(COL_GB, D_MODEL), col(COL_GC, D_MODEL), col(COL_NG, LANES),
                  row512, row512, row512, row512, row512,
                  pl.BlockSpec((tm, D_MODEL), lambda i: (i, 0)),
                  full((3, LANES, NSA_Q_WIDTH)), full((NSA_Q_WIDTH, D_MODEL)), full((SSM_INNER, D_MODEL)),
                  full((HG_WIDTH, D_MODEL)), full((D_MODEL, D_MODEL))],
        out_specs=pl.BlockSpec((tm, D_MODEL), lambda i: (i, 0)),
        out_shape=jax.ShapeDtypeStruct((t, D_MODEL), F32),
        compiler_params=_cparams(("parallel",)),
        name="merge",
    )(proj, proj, proj, proj, oc, os_, ow, yb, yc, x2, eg, wa, wb, wc, wo)


def _xa_mem_kernel(mem_ref, nm_ref, wk_ref, wv_ref, kn_ref, k_ref, v_ref):
    m = _rms(mem_ref[0], nm_ref[...]).astype(BF16)
    k = _dot(m, wk_ref[...])
    ks = [_rms(k[:, h * XA_HEAD_DIM:(h + 1) * XA_HEAD_DIM], kn_ref[...]) for h in range(XA_HEADS)]
    k_ref[0] = jnp.concatenate(ks, axis=1).astype(BF16)
    v_ref[0] = _dot(m, wv_ref[...]).astype(BF16)


def _xa_mem(mem, nm, wk, wv, kn):
    bsz, ml, _ = mem.shape

    def full(shape):
        return pl.BlockSpec(shape, lambda b: (0,) * len(shape))

    return pl.pallas_call(
        _xa_mem_kernel,
        grid=(bsz,),
        in_specs=[pl.BlockSpec((1, ml, D_MODEL), lambda b: (b, 0, 0)), full((1, D_MODEL)),
                  full((D_MODEL, XA_WIDTH)), full((D_MODEL, XA_WIDTH)), full((1, XA_HEAD_DIM))],
        out_specs=[pl.BlockSpec((1, ml, XA_WIDTH), lambda b: (b, 0, 0)),
                   pl.BlockSpec((1, ml, XA_WIDTH), lambda b: (b, 0, 0))],
        out_shape=[jax.ShapeDtypeStruct((bsz, ml, XA_WIDTH), BF16),
                   jax.ShapeDtypeStruct((bsz, ml, XA_WIDTH), BF16)],
        compiler_params=_cparams(("parallel",)),
        name="xa_mem",
    )(mem, nm, wk, wv, kn)


def _xa_kernel(x_ref, nx_ref, wq_ref, qn_ref, k_ref, v_ref, wo_ref, o_ref):
    x = x_ref[...]
    h = _rms(x, nx_ref[...]).astype(BF16)
    q = _dot(h, wq_ref[...])
    outs = []
    for hd in range(XA_HEADS):
        sl = slice(hd * XA_HEAD_DIM, (hd + 1) * XA_HEAD_DIM)
        qh = _rms(q[:, sl], qn_ref[...]).astype(BF16)
        sc = _dot_nt(qh, k_ref[0, :, sl]) * (XA_HEAD_DIM ** -0.5)
        sc = sc - jnp.max(sc, axis=-1, keepdims=True)
        p = jnp.exp(sc)
        p = p / jnp.sum(p, axis=-1, keepdims=True)
        outs.append(_dot(p.astype(BF16), v_ref[0, :, sl]))
    o = jnp.concatenate(outs, axis=1).astype(BF16)
    o_ref[...] = x + _dot(o, wo_ref[...])


def _xa(x2, nx, wq, qn, k, v, wo, s):
    t = x2.shape[0]
    tm = min(512, s)
    ml = k.shape[1]
    per_b = s // tm

    def full(shape):
        return pl.BlockSpec(shape, lambda i: (0,) * len(shape))

    return pl.pallas_call(
        _xa_kernel,
        grid=(t // tm,),
        in_specs=[pl.BlockSpec((tm, D_MODEL), lambda i: (i, 0)), full((1, D_MODEL)), full((D_MODEL, XA_WIDTH)),
                  full((1, XA_HEAD_DIM)),
                  pl.BlockSpec((1, ml, XA_WIDTH), lambda i: (i // per_b, 0, 0)),
                  pl.BlockSpec((1, ml, XA_WIDTH), lambda i: (i // per_b, 0, 0)),
                  full((XA_WIDTH, D_MODEL))],
        out_specs=pl.BlockSpec((tm, D_MODEL), lambda i: (i, 0)),
        out_shape=jax.ShapeDtypeStruct((t, D_MODEL), F32),
        compiler_params=_cparams(("parallel",)),
        name="cross_attn",
    )(x2, nx, wq, qn, k, v, wo)


def _ffn_kernel(x_ref, nw_ref, wg_ref, wu_ref, wd_ref, o_ref, h_ref, acc_ref):
    j = pl.program_id(1)

    @pl.when(j == 0)
    def _():
        h_ref[...] = _rms(x_ref[...], nw_ref[...]).astype(BF16)
        acc_ref[...] = jnp.zeros_like(acc_ref)

    h = h_ref[...]
    a = _silu(_dot(h, wg_ref[...])) * _dot(h, wu_ref[...])
    acc_ref[...] += _dot(a.astype(BF16), wd_ref[...])

    @pl.when(j == pl.num_programs(1) - 1)
    def _():
        o_ref[...] = x_ref[...] + acc_ref[...]


def _ffn(x2, nw, wg, wu, wd):
    t = x2.shape[0]
    tm = min(1024, t)
    th = FFN_HIDDEN // 2
    return pl.pallas_call(
        _ffn_kernel,
        grid=(t // tm, FFN_HIDDEN // th),
        in_specs=[pl.BlockSpec((tm, D_MODEL), lambda i, j: (i, 0)),
                  pl.BlockSpec((1, D_MODEL), lambda i, j: (0, 0)),
                  pl.BlockSpec((D_MODEL, th), lambda i, j: (0, j)),
                  pl.BlockSpec((D_MODEL, th), lambda i, j: (0, j)),
                  pl.BlockSpec((th, D_MODEL), lambda i, j: (j, 0))],
        out_specs=pl.BlockSpec((tm, D_MODEL), lambda i, j: (i, 0)),
        out_shape=jax.ShapeDtypeStruct((t, D_MODEL), F32),
        scratch_shapes=[pltpu.VMEM((tm, D_MODEL), BF16), pltpu.VMEM((tm, D_MODEL), F32)],
        compiler_params=_cparams(("parallel", "arbitrary")),
        name="ffn",
    )(x2, nw, wg, wu, wd)


def _pack_w_in(w):
    parts = {}
    off = 0
    names = ("q", "kc", "vc", "ks", "vs", "kw", "vw", "ng", "z", "xbc", "dt", "hq", "hf", "hi", "hg",
             "ga", "gb", "gc")
    for name, size in zip(names, SPLIT_SIZES):
        parts[name] = w[:, off:off + size]
        off += size

    def padded(a, width):
        return jnp.pad(a, ((0, 0), (0, width - a.shape[1])))

    cols = [parts["ga"], parts["gb"], parts["gc"], parts["q"], parts["z"], parts["hq"], parts["hf"],
            parts["hi"], parts["hg"], parts["xbc"], parts["kc"], parts["vc"], parts["ks"], parts["vs"],
            parts["kw"], parts["vw"], padded(parts["ng"], LANES), padded(parts["dt"], LANES)]
    packed = jnp.concatenate(cols, axis=1)
    return padded(packed, PROJ_WIDTH).astype(BF16)


def _rope_tables(pos):
    half = NSA_HEAD_DIM // 2
    inv = 1.0 / (ROPE_THETA ** (jnp.arange(half, dtype=F32) / half))
    ang = pos.astype(F32)[:, None] * inv[None, :]
    reps = LANES // half
    return jnp.tile(jnp.cos(ang), (1, reps)), jnp.tile(jnp.sin(ang), (1, reps))


def _block_diag2(w):
    z = jnp.zeros_like(w)
    return jnp.concatenate([jnp.concatenate([w, z], axis=-1), jnp.concatenate([z, w], axis=-1)], axis=-2)


def kernel(x, mem, norm_mix, w_in, nsa_q_norm, nsa_k_norm, nsa_cmp_pos_k, nsa_cmp_pos_v, nsa_cmp_w1_k, nsa_cmp_w2_k, nsa_cmp_w1_v, nsa_cmp_w2_v, w_nsa_o, ssm_conv_w, ssm_conv_b, ssm_dt_bias, ssm_a_log, ssm_d, ssm_norm, w_ssm_o, hg_lb_logits, hg_norm, w_hg_o, w_out, norm_xa, norm_mem, xa_w_q, xa_w_k, xa_w_v, xa_q_norm, xa_k_norm, xa_w_o, norm_ffn, ffn_w_gate, ffn_w_up, ffn_w_down):
    bsz, s, d = x.shape
    depth = w_in.shape[0]
    assert d == D_MODEL and s % (8 * TOK) == 0 and s // TOK <= TOK
    t = bsz * s
    hd = NSA_HEAD_DIM

    cos_t, sin_t = _rope_tables(jnp.arange(s))
    nbp = s // NSA_CMP_STRIDE
    cos_c, sin_c = _rope_tables(jnp.arange(nbp) * NSA_CMP_STRIDE + NSA_CMP_BLOCK - 1)
    tri = jnp.tril(jnp.ones((TOK, TOK), F32)).astype(BF16)
    rows = np.arange(LANES)[:, None]
    cols = np.arange(NSA_Q_WIDTH)[None, :]
    e8 = jnp.asarray(rows == cols // SSM_HEAD_DIM, F32).astype(BF16)
    eg = jnp.stack([jnp.asarray(rows == 3 * (cols // hd) + c, F32) for c in range(3)]).astype(BF16)

    lb_sm = jax.nn.softmax(hg_lb_logits.astype(F32), axis=0)
    lb_all = jnp.cumsum(lb_sm, axis=0) - lb_sm[0:1]

    def pad_lanes(v):
        return jnp.pad(v, (0, LANES - v.shape[0]))[None, :]

    x2 = x.reshape(t, d)
    for l in range(depth):
        proj = _in_proj(x2, norm_mix[l][None, :], _pack_w_in(w_in[l]))

        qn2 = jnp.tile(nsa_q_norm[l], 2)[None, :]
        kn2 = jnp.tile(nsa_k_norm[l], 2)[None, :]
        qp, ksn, kwn, vst, vwt = _nsa_prep(proj, cos_t, sin_t, qn2, kn2, bsz, s)
        w1k = _block_diag2(nsa_cmp_w1_k[l].reshape(NSA_CMP_BLOCK, hd, hd)).astype(BF16)
        w1v = _block_diag2(nsa_cmp_w1_v[l].reshape(NSA_CMP_BLOCK, hd, hd)).astype(BF16)
        kc, vct = _compress(proj, jnp.tile(nsa_cmp_pos_k[l], (1, 2)), jnp.tile(nsa_cmp_pos_v[l], (1, 2)),
                            w1k, _block_diag2(nsa_cmp_w2_k[l]).astype(BF16),
                            w1v, _block_diag2(nsa_cmp_w2_v[l]).astype(BF16), kn2, cos_c, sin_c, bsz, s)
        o_c, mask = _cmp_topk(qp, kc, vct, bsz, s)
        o_s = _flash(qp, ksn, vst, bsz, s, mask=mask, flags=_pair_flags(mask, bsz, s))
        o_w = _flash(qp, kwn, vwt, bsz, s)

        y_b = _ssd(proj, ssm_conv_w[l], ssm_conv_b[l][None, :], pad_lanes(ssm_dt_bias[l]),
                   pad_lanes(ssm_a_log[l]), jnp.repeat(ssm_d[l], SSM_HEAD_DIM)[None, :],
                   ssm_norm[l][None, :], e8, tri, bsz, s)
        y_c = _hgrn2(proj, lb_all[l][None, :], hg_norm[l][None, :], tri, bsz, s)

        x2 = _merge(proj, o_c.reshape(t, -1), o_s.reshape(t, -1), o_w.reshape(t, -1), y_b, y_c, x2, eg,
                    w_nsa_o[l].astype(BF16), w_ssm_o[l].astype(BF16), w_hg_o[l].astype(BF16),
                    w_out[l].astype(BF16))

        k_m, v_m = _xa_mem(mem, norm_mem[l][None, :], xa_w_k[l].astype(BF16), xa_w_v[l].astype(BF16),
                           xa_k_norm[l][None, :])
        x2 = _xa(x2, norm_xa[l][None, :], xa_w_q[l].astype(BF16), xa_q_norm[l][None, :], k_m, v_m,
                 xa_w_o[l].astype(BF16), s)
        x2 = _ffn(x2, norm_ffn[l][None, :], ffn_w_gate[l].astype(BF16), ffn_w_up[l].astype(BF16),
                  ffn_w_down[l].astype(BF16))
    return x2.reshape(bsz, s, d)
```

```python
import functools
import math

import numpy as np
import jax
import jax.numpy as jnp
from jax import lax
from jax.experimental import pallas as pl
from jax.experimental.pallas import tpu as pltpu

F32 = jnp.float32
BF16 = jnp.bfloat16

D_MODEL = 1024
NORM_EPS = 1e-6
ROPE_THETA = 10000.0
NEG_INF = -1e30
FORCE_SCORE = 1e9

NSA_HEADS = 8
NSA_KV_GROUPS = 2
NSA_REP = NSA_HEADS // NSA_KV_GROUPS
NSA_HEAD_DIM = 64
NSA_CMP_BLOCK = 32
NSA_CMP_STRIDE = 16
NSA_SEL_BLOCK = 64
NSA_TOPK = 16
NSA_WINDOW = 512
NSA_Q_WIDTH = NSA_HEADS * NSA_HEAD_DIM
NSA_KV_WIDTH = NSA_KV_GROUPS * NSA_HEAD_DIM

SSM_INNER = 512
SSM_HEAD_DIM = 64
SSM_HEADS = SSM_INNER // SSM_HEAD_DIM
SSM_GROUPS = 2
SSM_STATE = 64
SSM_CONV = 4
SSM_CONV_DIM = SSM_INNER + 2 * SSM_GROUPS * SSM_STATE

HG_HEADS = 4
HG_KEY_DIM = 128
HG_VAL_DIM = 128
HG_WIDTH = HG_HEADS * HG_KEY_DIM
HG_SUB = 16

XA_HEADS = 4
XA_HEAD_DIM = 128
XA_WIDTH = XA_HEADS * XA_HEAD_DIM

FFN_HIDDEN = -(-(8 * D_MODEL) // (3 * 256)) * 256

SPLIT_SIZES = (
    NSA_Q_WIDTH, NSA_KV_WIDTH, NSA_KV_WIDTH, NSA_KV_WIDTH, NSA_KV_WIDTH, NSA_KV_WIDTH, NSA_KV_WIDTH,
    3 * NSA_HEADS, SSM_INNER, SSM_CONV_DIM, SSM_HEADS, HG_WIDTH, HG_WIDTH, HG_HEADS * HG_VAL_DIM,
    HG_HEADS * HG_VAL_DIM, D_MODEL, D_MODEL, D_MODEL,
)

LANES = 128
TOK = 128
SEL_TILE = 256
LOG2_E = math.log2(math.e)
MASK_FLOOR = -1e20
VMEM_LIMIT = 56 * 1024 * 1024

PROJ_WIDTH = 8192
COL_GA, COL_GB, COL_GC = 0, 1024, 2048
COL_Q = 3072
COL_Z = 3584
COL_HQ, COL_HF, COL_HI, COL_HG = 4096, 4608, 5120, 5632
COL_XBC = 6144
COL_KC, COL_VC, COL_KS, COL_VS, COL_KW, COL_VW = 6912, 7040, 7168, 7296, 7424, 7552
COL_NG = 7680
COL_DT = 7808


def _cparams(sem):
    return pltpu.CompilerParams(dimension_semantics=sem, vmem_limit_bytes=VMEM_LIMIT)


def _dot(a, b):
    return jnp.dot(a, b, preferred_element_type=F32)


def _dot_nt(a, b):
    return lax.dot_general(a, b, (((1,), (1,)), ((), ())), preferred_element_type=F32)


def _split3(a):
    hi = a.astype(BF16)
    r1 = a - hi.astype(F32)
    mid = r1.astype(BF16)
    lo = (r1 - mid.astype(F32)).astype(BF16)
    return hi, mid, lo


def _dot_sel_r(a, sel):
    hi, mid, lo = _split3(a)
    return _dot(hi, sel) + _dot(mid, sel) + _dot(lo, sel)


def _dot_sel_l(sel, a):
    hi, mid, lo = _split3(a)
    return _dot(sel, hi) + _dot(sel, mid) + _dot(sel, lo)


def _silu(x):
    return x * jax.nn.sigmoid(x)


def _rms(x, w):
    return x * lax.rsqrt(jnp.mean(x * x, axis=-1, keepdims=True) + NORM_EPS) * w


def _inproj_kernel(x_ref, nw_ref, w_ref, o_ref, h_ref):
    @pl.when(pl.program_id(1) == 0)
    def _():
        h_ref[...] = _rms(x_ref[...], nw_ref[...]).astype(BF16)

    o_ref[...] = _dot(h_ref[...], w_ref[...])


def _in_proj(x2, norm_w, w_packed):
    t = x2.shape[0]
    tm = min(1024, t)
    tn = 1024
    return pl.pallas_call(
        _inproj_kernel,
        grid=(t // tm, PROJ_WIDTH // tn),
        in_specs=[
            pl.BlockSpec((tm, D_MODEL), lambda i, j: (i, 0)),
            pl.BlockSpec((1, D_MODEL), lambda i, j: (0, 0)),
            pl.BlockSpec((D_MODEL, tn), lambda i, j: (0, j)),
        ],
        out_specs=pl.BlockSpec((tm, tn), lambda i, j: (i, j)),
        out_shape=jax.ShapeDtypeStruct((t, PROJ_WIDTH), F32),
        scratch_shapes=[pltpu.VMEM((tm, D_MODEL), BF16)],
        compiler_params=_cparams(("parallel", "arbitrary")),
        name="in_proj",
    )(x2, norm_w, w_packed)


def _norm_rope_pair(x, w, cos, sin):
    lane = lax.broadcasted_iota(jnp.int32, x.shape, 1)
    lo = lane < NSA_HEAD_DIM
    sq = x * x
    s_lo = jnp.sum(jnp.where(lo, sq, 0.0), axis=1, keepdims=True)
    s_hi = jnp.sum(jnp.where(lo, 0.0, sq), axis=1, keepdims=True)
    ms = jnp.where(lo, s_lo, s_hi) * (1.0 / NSA_HEAD_DIM)
    y = x * lax.rsqrt(ms + NORM_EPS) * w
    half = NSA_HEAD_DIM // 2
    first = (lane % NSA_HEAD_DIM) < half
    rot = jnp.where(first, -pltpu.roll(y, LANES - half, 1), pltpu.roll(y, half, 1))
    return y * cos + rot * sin


def _nsa_prep_kernel(q_ref, ks_ref, vs_ref, kw_ref, vw_ref, cos_ref, sin_ref, qn_ref, kn_ref,
                     qp_ref, ksn_ref, kwn_ref, vst_ref, vwt_ref):
    cos = cos_ref[...]
    sin = sin_ref[...]
    lane = lax.broadcasted_iota(jnp.int32, (TOK, LANES), 1)
    scale = NSA_HEAD_DIM ** -0.5 * LOG2_E
    for c in range(NSA_HEADS // 2):
        y = _norm_rope_pair(q_ref[:, c * LANES:(c + 1) * LANES], qn_ref[...], cos, sin) * scale
        y_sw = pltpu.roll(y, NSA_HEAD_DIM, 1)
        for hh in range(2):
            h = 2 * c + hh
            g, r = divmod(h, NSA_REP)
            src = y if hh == g else y_sw
            keep = (lane // NSA_HEAD_DIM) == g
            qp_ref[0, g, :, r * LANES:(r + 1) * LANES] = jnp.where(keep, src, 0.0).astype(BF16)
    ksn_ref[0] = _norm_rope_pair(ks_ref[...], kn_ref[...], cos, sin).astype(BF16)
    kwn_ref[0] = _norm_rope_pair(kw_ref[...], kn_ref[...], cos, sin).astype(BF16)
    vs_t = vs_ref[...].T.astype(BF16)
    for g in range(NSA_KV_GROUPS):
        vst_ref[0, g, 0] = vs_t[g * NSA_HEAD_DIM:(g + 1) * NSA_HEAD_DIM, :]
    vwt_ref[0, 0] = vw_ref[...].T.astype(BF16)


def _nsa_prep(proj, cos_t, sin_t, qn2, kn2, bsz, s):
    nt = s // TOK

    def col(c0, w):
        return pl.BlockSpec((TOK, w), lambda b, i: (b * nt + i, c0 // w))

    tab = pl.BlockSpec((TOK, LANES), lambda b, i: (i, 0))
    vec = pl.BlockSpec((1, LANES), lambda b, i: (0, 0))
    return pl.pallas_call(
        _nsa_prep_kernel,
        grid=(bsz, nt),
        in_specs=[col(COL_Q, NSA_Q_WIDTH), col(COL_KS, LANES), col(COL_VS, LANES), col(COL_KW, LANES),
                  col(COL_VW, LANES), tab, tab, vec, vec],
        out_specs=[
            pl.BlockSpec((1, NSA_KV_GROUPS, TOK, NSA_REP * LANES), lambda b, i: (b, 0, i, 0)),
            pl.BlockSpec((1, TOK, LANES), lambda b, i: (b, i, 0)),
            pl.BlockSpec((1, TOK, LANES), lambda b, i: (b, i, 0)),
            pl.BlockSpec((1, NSA_KV_GROUPS, 1, NSA_HEAD_DIM, TOK), lambda b, i: (b, 0, i // 2, 0, i % 2)),
            pl.BlockSpec((1, 1, LANES, TOK), lambda b, i: (b, i, 0, 0)),
        ],
        out_shape=[
            jax.ShapeDtypeStruct((bsz, NSA_KV_GROUPS, s, NSA_REP * LANES), BF16),
            jax.ShapeDtypeStruct((bsz, s, LANES), BF16),
            jax.ShapeDtypeStruct((bsz, s, LANES), BF16),
            jax.ShapeDtypeStruct((bsz, NSA_KV_GROUPS, nt // 2, NSA_HEAD_DIM, SEL_TILE), BF16),
            jax.ShapeDtypeStruct((bsz, nt, LANES, TOK), BF16),
        ],
        compiler_params=_cparams(("parallel", "parallel")),
        name="nsa_prep",
    )(proj, proj, proj, proj, proj, cos_t, sin_t, qn2, kn2)


def _compress_kernel(kc_ref, vc_ref, posk_ref, posv_ref, w1k_ref, w2k_ref, w1v_ref, w2v_ref, kn_ref,
                     cos_ref, sin_ref, kco_ref, vct_ref, sh_ref, *, nbp):
    half_blk = NSA_CMP_BLOCK // 2

    def compress(x_ref, pos_ref, w1_ref, w2_ref):
        acc_a = jnp.zeros((nbp, LANES), F32)
        acc_b = jnp.zeros((nbp, LANES), F32)
        for l in range(half_blk):
            xl = x_ref[pl.ds(l, nbp, stride=NSA_CMP_STRIDE), :]
            acc_a = acc_a + _dot((xl + pos_ref[l:l + 1, :]).astype(BF16), w1_ref[l])
            acc_b = acc_b + _dot((xl + pos_ref[half_blk + l:half_blk + l + 1, :]).astype(BF16),
                                 w1_ref[half_blk + l])
        sh_ref[pl.ds(0, nbp), :] = acc_b
        sh_ref[pl.ds(nbp, 8), :] = jnp.zeros((8, LANES), F32)
        hid = acc_a + sh_ref[pl.ds(1, nbp), :]
        return _dot(_silu(hid).astype(BF16), w2_ref[...])

    kc = compress(kc_ref, posk_ref, w1k_ref, w2k_ref)
    kco_ref[0] = _norm_rope_pair(kc, kn_ref[...], cos_ref[...], sin_ref[...]).astype(BF16)
    vc = compress(vc_ref, posv_ref, w1v_ref, w2v_ref)
    vct_ref[0] = vc.T.astype(BF16)


def _compress(proj, posk, posv, w1k, w2k, w1v, w2v, kn2, cos_c, sin_c, bsz, s):
    nbp = s // NSA_CMP_STRIDE

    def full(shape):
        return pl.BlockSpec(shape, lambda b: (0,) * len(shape))

    return pl.pallas_call(
        functools.partial(_compress_kernel, nbp=nbp),
        grid=(bsz,),
        in_specs=[
            pl.BlockSpec((s, LANES), lambda b: (b, COL_KC // LANES)),
            pl.BlockSpec((s, LANES), lambda b: (b, COL_VC // LANES)),
            full((NSA_CMP_BLOCK, LANES)), full((NSA_CMP_BLOCK, LANES)),
            full((NSA_CMP_BLOCK, LANES, LANES)), full((LANES, LANES)),
            full((NSA_CMP_BLOCK, LANES, LANES)), full((LANES, LANES)),
            full((1, LANES)), full((nbp, LANES)), full((nbp, LANES)),
        ],
        out_specs=[
            pl.BlockSpec((1, nbp, LANES), lambda b: (b, 0, 0)),
            pl.BlockSpec((1, LANES, nbp), lambda b: (b, 0, 0)),
        ],
        out_shape=[
            jax.ShapeDtypeStruct((bsz, nbp, LANES), BF16),
            jax.ShapeDtypeStruct((bsz, LANES, nbp), BF16),
        ],
        scratch_shapes=[pltpu.VMEM((nbp + 8, LANES), F32)],
        compiler_params=_cparams(("parallel",)),
        name="nsa_compress",
    )(proj, proj, posk, posv, w1k, w2k, w1v, w2v, kn2, cos_c, sin_c)


def _q_rows(qp):
    return jnp.concatenate([qp[:, r * LANES:(r + 1) * LANES] for r in range(NSA_REP)], axis=0)


def _heads_to_token_major(acc_g):
    outs = []
    for p in range(NSA_REP // 2):
        blk = jnp.concatenate([acc_g[:, (2 * p) * TOK:(2 * p + 1) * TOK],
                               acc_g[:, (2 * p + 1) * TOK:(2 * p + 2) * TOK]], axis=0)
        outs.append(blk.T)
    return jnp.concatenate(outs, axis=1)


def _cmp_topk_kernel(qp_ref, kc_ref, vct_ref, oc_ref, mask_ref, ps_ref, *, nbp, nsel):
    qi = pl.program_id(1)
    ncol = NSA_REP * TOK
    n_io = lax.broadcasted_iota(jnp.int32, (nbp, ncol), 0)
    t_io = qi * TOK + lax.broadcasted_iota(jnp.int32, (nbp, ncol), 1) % TOK
    allowed = n_io * NSA_CMP_STRIDE + (NSA_CMP_BLOCK - 1) <= t_io
    j_io = lax.broadcasted_iota(jnp.int32, (nsel, TOK), 0)
    t_sel = qi * TOK + lax.broadcasted_iota(jnp.int32, (nsel, TOK), 1)
    cur = t_sel // NSA_SEL_BLOCK
    forced = (j_io == 0) | (j_io == cur) | (j_io == cur - 1)
    valid = j_io * NSA_SEL_BLOCK <= t_sel
    kc = kc_ref[0]
    vct = vct_ref[0]
    ps_ref[pl.ds(0, 8), :] = jnp.zeros((8, TOK), F32)
    outs = []
    for g in range(NSA_KV_GROUPS):
        s_t = _dot_nt(kc, _q_rows(qp_ref[0, g]))
        s_t = jnp.where(allowed, s_t, NEG_INF)
        m = jnp.max(s_t, axis=0, keepdims=True)
        p = jnp.where(allowed, jnp.exp2(s_t - m), 0.0)
        l = jnp.sum(p, axis=0, keepdims=True)
        p = p * jnp.where(l > 0.0, 1.0 / l, 0.0)
        o_t = _dot(vct, p.astype(BF16))
        outs.append(_heads_to_token_major(o_t[g * NSA_HEAD_DIM:(g + 1) * NSA_HEAD_DIM, :]))
        psum = p[:, 0:TOK]
        for r in range(1, NSA_REP):
            psum = psum + p[:, r * TOK:(r + 1) * TOK]
        ps_ref[pl.ds(8, nbp), :] = psum
        per = NSA_SEL_BLOCK // NSA_CMP_STRIDE
        imp = ps_ref[pl.ds(7, nsel, stride=per), :] + ps_ref[pl.ds(8 + per - 1, nsel, stride=per), :]
        for c in range(per - 1):
            imp = imp + 2.0 * ps_ref[pl.ds(8 + c, nsel, stride=per), :]
        imp = jnp.where(forced, FORCE_SCORE, jnp.where(valid, imp, NEG_INF))

        def pick(_, carry):
            imp_c, sel_c = carry
            mx = jnp.max(imp_c, axis=0, keepdims=True)
            idx = jnp.min(jnp.where(imp_c == mx, j_io, nsel), axis=0, keepdims=True)
            hit = j_io == idx
            return jnp.where(hit, -jnp.inf, imp_c), jnp.where(hit, 1.0, sel_c)

        _, sel = lax.fori_loop(0, min(NSA_TOPK, nsel), pick, (imp, jnp.zeros((nsel, TOK), F32)))
        mask_ref[0, g, 0] = jnp.where(valid & (sel > 0.5), 0.0, NEG_INF)
    oc_ref[0] = jnp.concatenate(outs, axis=1)


def _cmp_topk(qp, kc, vct, bsz, s):
    nt = s // TOK
    nbp = s // NSA_CMP_STRIDE
    nsel = s // NSA_SEL_BLOCK
    return pl.pallas_call(
        functools.partial(_cmp_topk_kernel, nbp=nbp, nsel=nsel),
        grid=(bsz, nt),
        in_specs=[
            pl.BlockSpec((1, NSA_KV_GROUPS, TOK, NSA_REP * LANES), lambda b, i: (b, 0, i, 0)),
            pl.BlockSpec((1, nbp, LANES), lambda b, i: (b, 0, 0)),
            pl.BlockSpec((1, LANES, nbp), lambda b, i: (b, 0, 0)),
        ],
        out_specs=[
            pl.BlockSpec((1, TOK, NSA_Q_WIDTH), lambda b, i: (b, i, 0)),
            pl.BlockSpec((1, NSA_KV_GROUPS, 1, nsel, TOK), lambda b, i: (b, 0, i, 0, 0)),
        ],
        out_shape=[
            jax.ShapeDtypeStruct((bsz, s, NSA_Q_WIDTH), F32),
            jax.ShapeDtypeStruct((bsz, NSA_KV_GROUPS, nt, nsel, TOK), F32),
        ],
        scratch_shapes=[pltpu.VMEM((nbp + 8, TOK), F32)],
        compiler_params=_cparams(("parallel", "parallel")),
        name="nsa_cmp_topk",
    )(qp, kc, vct)


def _window_kernel(qp_ref, k_ref, vt_ref, o_ref, m_ref, l_ref, acc_ref):
    g = pl.program_id(1)
    qi = pl.program_id(2)
    ncol = NSA_REP * TOK
    q_rows = _q_rows(qp_ref[0, 0])
    m_ref[...] = jnp.full((1, ncol), NEG_INF, F32)
    l_ref[...] = jnp.zeros((1, ncol), F32)
    acc_ref[...] = jnp.zeros((LANES, ncol), F32)
    t_io = qi * TOK + lax.broadcasted_iota(jnp.int32, (TOK, ncol), 1) % TOK
    sub = lax.broadcasted_iota(jnp.int32, (TOK, ncol), 0)

    def body(j, c):
        kb = k_ref[0, pl.ds(pl.multiple_of(j * TOK, TOK), TOK), :]
        s_t = _dot_nt(kb, q_rows)
        key_pos = j * TOK + sub
        allowed = (key_pos <= t_io) & (t_io - key_pos < NSA_WINDOW)
        s_t = jnp.where(allowed, s_t, NEG_INF)
        m_old = m_ref[...]
        m_new = jnp.maximum(m_old, jnp.max(s_t, axis=0, keepdims=True))
        alpha = jnp.exp2(m_old - m_new)
        p = jnp.where(allowed, jnp.exp2(s_t - m_new), 0.0)
        l_ref[...] = alpha * l_ref[...] + jnp.sum(p, axis=0, keepdims=True)
        acc_ref[...] = acc_ref[...] * alpha + _dot(vt_ref[0, j], p.astype(BF16))
        m_ref[...] = m_new
        return c

    lax.fori_loop(jnp.maximum(qi - NSA_WINDOW // TOK, 0), qi + 1, body, 0)
    acc_g = acc_ref[pl.ds(pl.multiple_of(g * NSA_HEAD_DIM, NSA_HEAD_DIM), NSA_HEAD_DIM), :]
    o_ref[0] = _heads_to_token_major(acc_g / l_ref[...])


def _window(qp, k, vt, bsz, s):
    nt = s // TOK
    gw = NSA_REP * NSA_HEAD_DIM
    return pl.pallas_call(
        _window_kernel,
        grid=(bsz, NSA_KV_GROUPS, nt),
        in_specs=[
            pl.BlockSpec((1, 1, TOK, NSA_REP * LANES), lambda b, g, i: (b, g, i, 0)),
            pl.BlockSpec((1, s, LANES), lambda b, g, i: (b, 0, 0)),
            pl.BlockSpec((1, nt, LANES, TOK), lambda b, g, i: (b, 0, 0, 0)),
        ],
        out_specs=pl.BlockSpec((1, TOK, gw), lambda b, g, i: (b, i, g)),
        out_shape=jax.ShapeDtypeStruct((bsz, s, NSA_Q_WIDTH), F32),
        scratch_shapes=[pltpu.VMEM((1, NSA_REP * TOK), F32), pltpu.VMEM((1, NSA_REP * TOK), F32),
                        pltpu.VMEM((LANES, NSA_REP * TOK), F32)],
        compiler_params=_cparams(("parallel", "parallel", "parallel")),
        name="nsa_window",
    )(qp, k, vt)


def _selected_kernel(list_ref, cnt_ref, qp_ref, k_ref, vt_ref, bias_ref, o_ref, m_ref, l_ref, acc_ref,
                     s_ref, p_ref, a_ref, *, nt):
    b = pl.program_id(0)
    qi = pl.program_id(1)
    ncol = NSA_REP * TOK
    blk_per_tile = SEL_TILE // NSA_SEL_BLOCK
    n_kt = nt // (SEL_TILE // TOK)
    q_rows = [_q_rows(qp_ref[0, g]) for g in range(NSA_KV_GROUPS)]
    m_ref[...] = jnp.full(m_ref.shape, NEG_INF, F32)
    l_ref[...] = jnp.zeros(l_ref.shape, F32)
    acc_ref[...] = jnp.zeros(acc_ref.shape, F32)
    p_ref[...] = jnp.zeros(p_ref.shape, BF16)
    a_ref[...] = jnp.ones(a_ref.shape, F32)
    row8 = lax.broadcasted_iota(jnp.int32, (8, TOK), 0)
    base = (b * nt + qi) * n_kt
    count = cnt_ref[b * nt + qi]
    n_diag = (qi * TOK) // SEL_TILE

    def tile_at(i):
        return jnp.where(i < count, list_ref[base + jnp.minimum(i, n_kt - 1)], n_diag)

    def scores(j, par):
        kb = k_ref[0, pl.ds(pl.multiple_of(j * SEL_TILE, SEL_TILE), SEL_TILE), :]
        for g in range(NSA_KV_GROUPS):
            s_ref[par, g] = _dot_nt(kb, q_rows[g])

    def values(j, par):
        for g in range(NSA_KV_GROUPS):
            acc_ref[g] = acc_ref[g] * a_ref[par, g] + _dot(vt_ref[0, g, j], p_ref[par, g])

    def softmax(j, par, causal):
        r0 = (j * blk_per_tile) % 8
        for g in range(NSA_KV_GROUPS):
            s_t = s_ref[par, g]
            b8 = bias_ref[0, g, 0, pl.ds(pl.multiple_of((j * blk_per_tile // 8) * 8, 8), 8), :]
            parts = []
            for c in range(blk_per_tile):
                row = jnp.sum(jnp.where(row8 == r0 + c, b8, 0.0), axis=0, keepdims=True)
                row = jnp.concatenate([row] * NSA_REP, axis=1)
                parts.append(s_t[c * NSA_SEL_BLOCK:(c + 1) * NSA_SEL_BLOCK, :] + row)
            s_t = jnp.concatenate(parts, axis=0)
            if causal:
                t_io = qi * TOK + lax.broadcasted_iota(jnp.int32, (SEL_TILE, ncol), 1) % TOK
                key_pos = j * SEL_TILE + lax.broadcasted_iota(jnp.int32, (SEL_TILE, ncol), 0)
                s_t = jnp.where(key_pos <= t_io, s_t, NEG_INF)
            m_old = m_ref[g]
            m_new = jnp.maximum(m_old, jnp.max(s_t, axis=0, keepdims=True))
            alpha = jnp.exp2(m_old - m_new)
            p = jnp.exp2(s_t - jnp.maximum(m_new, MASK_FLOOR))
            l_ref[g] = alpha * l_ref[g] + jnp.sum(p, axis=0, keepdims=True)
            m_ref[g] = m_new
            p_ref[par, g] = p.astype(BF16)
            a_ref[par, g] = alpha

    scores(tile_at(0), 0)

    def body(i, c):
        values(tile_at(jnp.maximum(i - 1, 0)), 0)
        softmax(tile_at(i), 0, False)
        scores(tile_at(i + 1), 0)
        return c

    lax.fori_loop(0, count, body, 0)
    values(tile_at(jnp.maximum(count - 1, 0)), 0)
    softmax(n_diag, 0, True)
    values(n_diag, 0)
    outs =[_heads_to_token_major(acc_ref[g] / l_ref[g]) for g in range(NSA_KV_GROUPS)]
    o_ref[0] = jnp.concatenate(outs, axis=1)


def _selected(qp, k, vt, bias, tile_list, tile_count, bsz, s):
    nt = s // TOK
    nsel = s // NSA_SEL_BLOCK
    ncol = NSA_REP * TOK
    grid_spec = pltpu.PrefetchScalarGridSpec(
        num_scalar_prefetch=2,
        grid=(bsz, nt),
        in_specs=[
            pl.BlockSpec((1, NSA_KV_GROUPS, TOK, NSA_REP * LANES), lambda b, i, tl, tc: (b, 0, i, 0)),
            pl.BlockSpec((1, s, LANES), lambda b, i, tl, tc: (b, 0, 0)),
            pl.BlockSpec((1, NSA_KV_GROUPS, s // SEL_TILE, NSA_HEAD_DIM, SEL_TILE),
                         lambda b, i, tl, tc: (b, 0, 0, 0, 0)),
            pl.BlockSpec((1, NSA_KV_GROUPS, 1, nsel, TOK), lambda b, i, tl, tc: (b, 0, i, 0, 0)),
        ],
        out_specs=pl.BlockSpec((1, TOK, NSA_Q_WIDTH), lambda b, i, tl, tc: (b, i, 0)),
        scratch_shapes=[pltpu.VMEM((NSA_KV_GROUPS, 1, ncol), F32), pltpu.VMEM((NSA_KV_GROUPS, 1, ncol), F32),
                        pltpu.VMEM((NSA_KV_GROUPS, NSA_HEAD_DIM, ncol), F32),
                        pltpu.VMEM((1, NSA_KV_GROUPS, SEL_TILE, ncol), F32),
                        pltpu.VMEM((1, NSA_KV_GROUPS, SEL_TILE, ncol), BF16),
                        pltpu.VMEM((1, NSA_KV_GROUPS, 1, ncol), F32)],
    )
    return pl.pallas_call(
        functools.partial(_selected_kernel, nt=nt),
        grid_spec=grid_spec,
        out_shape=jax.ShapeDtypeStruct((bsz, s, NSA_Q_WIDTH), F32),
        compiler_params=_cparams(("parallel", "parallel")),
        name="nsa_selected",
    )(tile_list, tile_count, qp, k, vt, bias)


def _active_tiles(bias, bsz, s):
    nt = s // TOK
    n_kt = s // SEL_TILE
    rows = SEL_TILE // NSA_SEL_BLOCK * TOK
    act = bias.reshape(bsz, NSA_KV_GROUPS, nt, n_kt, rows).max(axis=(1, 4)) > -1.0
    n_diag = (jnp.arange(nt) * TOK) // SEL_TILE
    act = act & (jnp.arange(n_kt)[None, None, :] < n_diag[None, :, None])
    order = jnp.argsort(jnp.where(act, 0, 1), axis=-1, stable=True).astype(jnp.int32)
    return order.reshape(-1), jnp.sum(act, axis=-1, dtype=jnp.int32).reshape(-1)


def _ssd_kernel(z_ref, xbc_ref, dt_ref, cw_ref, cb_ref, dtb_ref, alog_ref, dskip_ref, nw_ref, e8_ref,
                tri_ref, y_ref, ext_ref, st_ref):
    @pl.when(pl.program_id(1) == 0)
    def _():
        ext_ref[pl.ds(0, 8), :] = jnp.zeros((8, SSM_CONV_DIM), F32)
        st_ref[...] = jnp.zeros_like(st_ref)

    xb = xbc_ref[...]
    ext_ref[pl.ds(8, TOK), :] = xb
    conv = cb_ref[...] + xb * cw_ref[SSM_CONV - 1:SSM_CONV, :]
    for k in range(SSM_CONV - 1):
        conv = conv + ext_ref[pl.ds(8 - (SSM_CONV - 1) + k, TOK), :] * cw_ref[k:k + 1, :]
    ext_ref[pl.ds(0, 8), :] = xb[TOK - 8:TOK, :]
    u = _silu(conv)
    xs = u[:, :SSM_INNER]
    gn = SSM_GROUPS * SSM_STATE
    bm = u[:, SSM_INNER:SSM_INNER + gn]
    cm = u[:, SSM_INNER + gn:]
    dt = jax.nn.softplus(dt_ref[...] + dtb_ref[...])
    da = dt * (-jnp.exp(alog_ref[...]))
    tri = tri_ref[...]
    e8 = e8_ref[...]
    cum = _dot_sel_l(tri, da)
    cum_t = cum.T
    cum_e = _dot_sel_r(cum, e8)
    dt_e = _dot_sel_r(dt, e8)
    xdt = xs * dt_e
    lane = lax.broadcasted_iota(jnp.int32, (TOK, LANES), 1)
    t_io = lax.broadcasted_iota(jnp.int32, (TOK, TOK), 0)
    s_io = lax.broadcasted_iota(jnp.int32, (TOK, TOK), 1)
    causal = s_io <= t_io
    hg = SSM_HEADS // SSM_GROUPS
    y_parts = []
    cbs = []
    for g in range(SSM_GROUPS):
        in_g = (lane // SSM_STATE) == g
        cbs.append(_dot_nt(jnp.where(in_g, cm, 0.0).astype(BF16), bm.astype(BF16)))
    for c in range(SSM_HEADS // 2):
        acc = jnp.zeros((TOK, LANES), F32)
        xpair = xdt[:, c * LANES:(c + 1) * LANES]
        for hh in range(2):
            h = 2 * c + hh
            seg = cum[:, h:h + 1] - cum_t[h:h + 1, :]
            decay = jnp.where(causal, jnp.exp(jnp.where(causal, seg, 0.0)), 0.0)
            mat = (cbs[h // hg] * decay).astype(BF16)
            xh = jnp.where((lane // SSM_HEAD_DIM) == hh, xpair, 0.0).astype(BF16)
            acc = acc + _dot(mat, xh)
        y_parts.append(acc)
    y_diag = jnp.concatenate(y_parts, axis=1)
    prev = st_ref[...]
    y_off = _dot(cm.astype(BF16), prev.astype(BF16)) * jnp.exp(cum_e)
    cum_last = cum_e[TOK - 1:TOK, :]
    xdec = (xdt * jnp.exp(cum_last - cum_e)).astype(BF16)
    new = _dot(bm.T.astype(BF16), xdec)
    row_g = lax.broadcasted_iota(jnp.int32, (LANES, SSM_INNER), 0) // SSM_STATE
    col_g = lax.broadcasted_iota(jnp.int32, (LANES, SSM_INNER), 1) // (hg * SSM_HEAD_DIM)
    st_ref[...] = prev * jnp.exp(cum_last) + jnp.where(row_g == col_g, new, 0.0)
    y = (y_diag + y_off + xs * dskip_ref[...]) * _silu(z_ref[...])
    gw = SSM_INNER // SSM_GROUPS
    outs = []
    for g in range(SSM_GROUPS):
        yg = y[:, g * gw:(g + 1) * gw]
        outs.append(_rms(yg, nw_ref[:, g * gw:(g + 1) * gw]))
    y_ref[...] = jnp.concatenate(outs, axis=1)


def _ssd(proj, cw, cb, dtb, alog, dskip, nw, e8, tri, bsz, s):
    nt = s // TOK

    def col(c0, w):
        return pl.BlockSpec((TOK, w), lambda b, i: (b * nt + i, c0 // w))

    def full(shape):
        return pl.BlockSpec(shape, lambda b, i: (0,) * len(shape))

    return pl.pallas_call(
        _ssd_kernel,
        grid=(bsz, nt),
        in_specs=[col(COL_Z, SSM_INNER), col(COL_XBC, SSM_CONV_DIM), col(COL_DT, LANES),
                  full((SSM_CONV, SSM_CONV_DIM)), full((1, SSM_CONV_DIM)), full((1, LANES)), full((1, LANES)),
                  full((1, SSM_INNER)), full((1, SSM_INNER)), full((LANES, SSM_INNER)), full((TOK, TOK))],
        out_specs=pl.BlockSpec((TOK, SSM_INNER), lambda b, i: (b * nt + i, 0)),
        out_shape=jax.ShapeDtypeStruct((bsz * s, SSM_INNER), F32),
        scratch_shapes=[pltpu.VMEM((TOK + 8, SSM_CONV_DIM), F32), pltpu.VMEM((LANES, SSM_INNER), F32)],
        compiler_params=_cparams(("parallel", "arbitrary")),
        name="ssd",
    )(proj, proj, proj, cw, cb, dtb, alog, dskip, nw, e8, tri)


def _hgrn2_kernel(q_ref, f_ref, i_ref, g_ref, lb_ref, nw_ref, tri_ref, y_ref, st_ref, kp_ref, bp_ref, vp_ref):
    @pl.when(pl.program_id(1) == 0)
    def _():
        st_ref[...] = jnp.zeros_like(st_ref)
        kp_ref[...] = jnp.zeros_like(kp_ref)
        bp_ref[...] = jnp.zeros_like(bp_ref)
        vp_ref[...] = jnp.zeros_like(vp_ref)

    nsub = TOK // HG_SUB
    row = lax.broadcasted_iota(jnp.int32, (TOK, HG_KEY_DIM), 0)
    sub_id = row // HG_SUB
    in_sub = row % HG_SUB
    tri = tri_ref[...]
    outs = []
    for h in range(HG_HEADS):
        sl = slice(h * HG_KEY_DIM, (h + 1) * HG_KEY_DIM)
        lb = lb_ref[:, sl]
        fr = f_ref[:, sl]
        q = _silu(q_ref[:, sl]) * (HG_KEY_DIM ** -0.5)
        log_f = jnp.log(lb + (1.0 - lb) * jax.nn.sigmoid(fr))
        k = (1.0 - lb) * jax.nn.sigmoid(-fr)
        v = i_ref[:, sl]
        bcum = _dot_sel_l(tri, log_f)
        st = st_ref[h]
        o = _dot_nt((q * jnp.exp(bcum)).astype(BF16), st.astype(BF16))
        r_rows = [bcum[i * HG_SUB - 1:i * HG_SUB, :] for i in range(1, nsub)]
        r_loc = jnp.zeros_like(bcum)
        for i in range(1, nsub):
            r_loc = jnp.where(sub_id == i, r_rows[i - 1], r_loc)
        q_loc = q * jnp.exp(bcum - r_loc)
        q_aug = jnp.concatenate(
            [jnp.where(sub_id == i, q_loc, 0.0).astype(BF16) for i in range(1, nsub)], axis=1)
        k_aug = jnp.concatenate(
            [jnp.where(sub_id < i, k * jnp.exp(jnp.where(sub_id < i, r_rows[i - 1] - bcum, 0.0)), 0.0).astype(BF16)
             for i in range(1, nsub)], axis=1)
        att = _dot_nt(q_aug, k_aug)
        o = o + _dot(att.astype(BF16), v.astype(BF16))
        kp_ref[pl.ds(HG_SUB, TOK), :] = k
        bp_ref[pl.ds(HG_SUB, TOK), :] = bcum
        vp_ref[pl.ds(HG_SUB, TOK), :] = v
        o = o + jnp.sum(q * k, axis=1, keepdims=True) * v
        for d in range(1, HG_SUB):
            ks = kp_ref[pl.ds(HG_SUB - d, TOK), :]
            bs = bp_ref[pl.ds(HG_SUB - d, TOK), :]
            vs = vp_ref[pl.ds(HG_SUB - d, TOK), :]
            ok = in_sub >= d
            w = jnp.sum(q * ks * jnp.exp(jnp.where(ok, bcum - bs, 0.0)), axis=1, keepdims=True)
            o = o + jnp.where(ok, w * vs, 0.0)
        b_last = bcum[TOK - 1:TOK, :]
        kdec = (k * jnp.exp(b_last - bcum)).astype(BF16)
        st_ref[h] = st * jnp.exp(b_last) + _dot(v.T.astype(BF16), kdec)
        outs.append(_rms(o, nw_ref[...]) * _silu(g_ref[:, sl]))
    y_ref[...] = jnp.concatenate(outs, axis=1)


def _hgrn2(proj, lb, nw, tri, bsz, s):
    nt = s // TOK

    def col(c0):
        return pl.BlockSpec((TOK, HG_WIDTH), lambda b, i: (b * nt + i, c0 // HG_WIDTH))

    def full(shape):
        return pl.BlockSpec(shape, lambda b, i: (0,) * len(shape))

    pad = pltpu.VMEM((TOK + HG_SUB, HG_KEY_DIM), F32)
    return pl.pallas_call(
        _hgrn2_kernel,
        grid=(bsz, nt),
        in_specs=[col(COL_HQ), col(COL_HF), col(COL_HI), col(COL_HG), full((1, HG_WIDTH)),
                  full((1, HG_VAL_DIM)), full((TOK, TOK))],
        out_specs=pl.BlockSpec((TOK, HG_WIDTH), lambda b, i: (b * nt + i, 0)),
        out_shape=jax.ShapeDtypeStruct((bsz * s, HG_WIDTH), F32),
        scratch_shapes=[pltpu.VMEM((HG_HEADS, HG_VAL_DIM, HG_KEY_DIM), F32), pad, pad, pad],
        compiler_params=_cparams(("parallel", "arbitrary")),
        name="hgrn2",
    )(proj, proj, proj, proj, lb, nw, tri)


def _merge_kernel(ga_ref, gb_ref, gc_ref, ng_ref, oc_ref, os_ref, ow_ref, yb_ref, yc_ref, x_ref,
                  eg_ref, wa_ref, wb_ref, wc_ref, wo_ref, o_ref):
    sg = jax.nn.sigmoid(ng_ref[...])
    nsa = (_dot_sel_r(sg, eg_ref[0]) * oc_ref[...] + _dot_sel_r(sg, eg_ref[1]) * os_ref[...]
           + _dot_sel_r(sg, eg_ref[2]) * ow_ref[...])
    ya = _dot(nsa.astype(BF16), wa_ref[...])
    yb = _dot(yb_ref[...].astype(BF16), wb_ref[...])
    yc = _dot(yc_ref[...].astype(BF16), wc_ref[...])
    merged = (jax.nn.sigmoid(ga_ref[...]) * ya + jax.nn.sigmoid(gb_ref[...]) * yb
              + jax.nn.sigmoid(gc_ref[...]) * yc)
    o_ref[...] = x_ref[...] + _dot(merged.astype(BF16), wo_ref[...])


def _merge(proj, oc, os_, ow, yb, yc, x2, eg, wa, wb, wc, wo):
    t = x2.shape[0]
    tm = min(512, t)

    def col(c0, w):
        return pl.BlockSpec((tm, w), lambda i: (i, c0 // w))

    def full(shape):
        return pl.BlockSpec(shape, lambda i: (0,) * len(shape))

    row512 = pl.BlockSpec((tm, 512), lambda i: (i, 0))
    return pl.pallas_call(
        _merge_kernel,
        grid=(t // tm,),
        in_specs=[col(COL_GA, D_MODEL), col(COL_GB, D_MODEL), col(COL_GC, D_MODEL), col(COL_NG, LANES),
                  row512, row512, row512, row512, row512,
                  pl.BlockSpec((tm, D_MODEL), lambda i: (i, 0)),
                  full((3, LANES, NSA_Q_WIDTH)), full((NSA_Q_WIDTH, D_MODEL)), full((SSM_INNER, D_MODEL)),
                  full((HG_WIDTH, D_MODEL)), full((D_MODEL, D_MODEL))],
        out_specs=pl.BlockSpec((tm, D_MODEL), lambda i: (i, 0)),
        out_shape=jax.ShapeDtypeStruct((t, D_MODEL), F32),
        compiler_params=_cparams(("parallel",)),
        name="merge",
    )(proj, proj, proj, proj, oc, os_, ow, yb, yc, x2, eg, wa, wb, wc, wo)


def _xa_mem_kernel(mem_ref, nm_ref, wk_ref, wv_ref, kn_ref, k_ref, v_ref):
    m = _rms(mem_ref[0], nm_ref[...]).astype(BF16)
    k = _dot(m, wk_ref[...])
    ks = [_rms(k[:, h * XA_HEAD_DIM:(h + 1) * XA_HEAD_DIM], kn_ref[...]) for h in range(XA_HEADS)]
    k_ref[0] = jnp.concatenate(ks, axis=1).astype(BF16)
    v_ref[0] = _dot(m, wv_ref[...]).astype(BF16)


def _xa_mem(mem, nm, wk, wv, kn):
    bsz, ml, _ = mem.shape

    def full(shape):
        return pl.BlockSpec(shape, lambda b: (0,) * len(shape))

    return pl.pallas_call(
        _xa_mem_kernel,
        grid=(bsz,),
        in_specs=[pl.BlockSpec((1, ml, D_MODEL), lambda b: (b, 0, 0)), full((1, D_MODEL)),
                  full((D_MODEL, XA_WIDTH)), full((D_MODEL, XA_WIDTH)), full((1, XA_HEAD_DIM))],
        out_specs=[pl.BlockSpec((1, ml, XA_WIDTH), lambda b: (b, 0, 0)),
                   pl.BlockSpec((1, ml, XA_WIDTH), lambda b: (b, 0, 0))],
        out_shape=[jax.ShapeDtypeStruct((bsz, ml, XA_WIDTH), BF16),
                   jax.ShapeDtypeStruct((bsz, ml, XA_WIDTH), BF16)],
        compiler_params=_cparams(("parallel",)),
        name="xa_mem",
    )(mem, nm, wk, wv, kn)


def _xa_kernel(x_ref, nx_ref, wq_ref, qn_ref, k_ref, v_ref, wo_ref, o_ref):
    x = x_ref[...]
    h = _rms(x, nx_ref[...]).astype(BF16)
    q = _dot(h, wq_ref[...])
    outs = []
    for hd in range(XA_HEADS):
        sl = slice(hd * XA_HEAD_DIM, (hd + 1) * XA_HEAD_DIM)
        qh = _rms(q[:, sl], qn_ref[...]).astype(BF16)
        sc = _dot_nt(qh, k_ref[0, :, sl]) * (XA_HEAD_DIM ** -0.5)
        sc = sc - jnp.max(sc, axis=-1, keepdims=True)
        p = jnp.exp(sc)
        p = p / jnp.sum(p, axis=-1, keepdims=True)
        outs.append(_dot(p.astype(BF16), v_ref[0, :, sl]))
    o = jnp.concatenate(outs, axis=1).astype(BF16)
    o_ref[...] = x + _dot(o, wo_ref[...])


def _xa(x2, nx, wq, qn, k, v, wo, s):
    t = x2.shape[0]
    tm = min(512, s)
    ml = k.shape[1]
    per_b = s // tm

    def full(shape):
        return pl.BlockSpec(shape, lambda i: (0,) * len(shape))

    return pl.pallas_call(
        _xa_kernel,
        grid=(t // tm,),
        in_specs=[pl.BlockSpec((tm, D_MODEL), lambda i: (i, 0)), full((1, D_MODEL)), full((D_MODEL, XA_WIDTH)),
                  full((1, XA_HEAD_DIM)),
                  pl.BlockSpec((1, ml, XA_WIDTH), lambda i: (i // per_b, 0, 0)),
                  pl.BlockSpec((1, ml, XA_WIDTH), lambda i: (i // per_b, 0, 0)),
                  full((XA_WIDTH, D_MODEL))],
        out_specs=pl.BlockSpec((tm, D_MODEL), lambda i: (i, 0)),
        out_shape=jax.ShapeDtypeStruct((t, D_MODEL), F32),
        compiler_params=_cparams(("parallel",)),
        name="cross_attn",
    )(x2, nx, wq, qn, k, v, wo)


def _ffn_kernel(x_ref, nw_ref, wg_ref, wu_ref, wd_ref, o_ref, h_ref, acc_ref):
    j = pl.program_id(1)

    @pl.when(j == 0)
    def _():
        h_ref[...] = _rms(x_ref[...], nw_ref[...]).astype(BF16)
        acc_ref[...] = jnp.zeros_like(acc_ref)

    h = h_ref[...]
    a = _silu(_dot(h, wg_ref[...])) * _dot(h, wu_ref[...])
    acc_ref[...] += _dot(a.astype(BF16), wd_ref[...])

    @pl.when(j == pl.num_programs(1) - 1)
    def _():
        o_ref[...] = x_ref[...] + acc_ref[...]


def _ffn(x2, nw, wg, wu, wd):
    t = x2.shape[0]
    tm = min(1024, t)
    th = FFN_HIDDEN // 2
    return pl.pallas_call(
        _ffn_kernel,
        grid=(t // tm, FFN_HIDDEN // th),
        in_specs=[pl.BlockSpec((tm, D_MODEL), lambda i, j: (i, 0)),
                  pl.BlockSpec((1, D_MODEL), lambda i, j: (0, 0)),
                  pl.BlockSpec((D_MODEL, th), lambda i, j: (0, j)),
                  pl.BlockSpec((D_MODEL, th), lambda i, j: (0, j)),
                  pl.BlockSpec((th, D_MODEL), lambda i, j: (j, 0))],
        out_specs=pl.BlockSpec((tm, D_MODEL), lambda i, j: (i, 0)),
        out_shape=jax.ShapeDtypeStruct((t, D_MODEL), F32),
        scratch_shapes=[pltpu.VMEM((tm, D_MODEL), BF16), pltpu.VMEM((tm, D_MODEL), F32)],
        compiler_params=_cparams(("parallel", "arbitrary")),
        name="ffn",
    )(x2, nw, wg, wu, wd)


def _pack_w_in(w):
    parts = {}
    off = 0
    names = ("q", "kc", "vc", "ks", "vs", "kw", "vw", "ng", "z", "xbc", "dt", "hq", "hf", "hi", "hg",
             "ga", "gb", "gc")
    for name, size in zip(names, SPLIT_SIZES):
        parts[name] = w[:, off:off + size]
        off += size

    def padded(a, width):
        return jnp.pad(a, ((0, 0), (0, width - a.shape[1])))

    cols = [parts["ga"], parts["gb"], parts["gc"], parts["q"], parts["z"], parts["hq"], parts["hf"],
            parts["hi"], parts["hg"], parts["xbc"], parts["kc"], parts["vc"], parts["ks"], parts["vs"],
            parts["kw"], parts["vw"], padded(parts["ng"], LANES), padded(parts["dt"], LANES)]
    packed = jnp.concatenate(cols, axis=1)
    return padded(packed, PROJ_WIDTH).astype(BF16)


def _rope_tables(pos):
    half = NSA_HEAD_DIM // 2
    inv = 1.0 / (ROPE_THETA ** (jnp.arange(half, dtype=F32) / half))
    ang = pos.astype(F32)[:, None] * inv[None, :]
    reps = LANES // half
    return jnp.tile(jnp.cos(ang), (1, reps)), jnp.tile(jnp.sin(ang), (1, reps))


def _block_diag2(w):
    z = jnp.zeros_like(w)
    return jnp.concatenate([jnp.concatenate([w, z], axis=-1), jnp.concatenate([z, w], axis=-1)], axis=-2)


def kernel(x, mem, norm_mix, w_in, nsa_q_norm, nsa_k_norm, nsa_cmp_pos_k, nsa_cmp_pos_v, nsa_cmp_w1_k, nsa_cmp_w2_k, nsa_cmp_w1_v, nsa_cmp_w2_v, w_nsa_o, ssm_conv_w, ssm_conv_b, ssm_dt_bias, ssm_a_log, ssm_d, ssm_norm, w_ssm_o, hg_lb_logits, hg_norm, w_hg_o, w_out, norm_xa, norm_mem, xa_w_q, xa_w_k, xa_w_v, xa_q_norm, xa_k_norm, xa_w_o, norm_ffn, ffn_w_gate, ffn_w_up, ffn_w_down):
    bsz, s, d = x.shape
    depth = w_in.shape[0]
    assert d == D_MODEL and s % (8 * TOK) == 0 and s // TOK <= TOK
    t = bsz * s
    hd = NSA_HEAD_DIM

    cos_t, sin_t = _rope_tables(jnp.arange(s))
    nbp = s // NSA_CMP_STRIDE
    cos_c, sin_c = _rope_tables(jnp.arange(nbp) * NSA_CMP_STRIDE + NSA_CMP_BLOCK - 1)
    tri = jnp.tril(jnp.ones((TOK, TOK), F32)).astype(BF16)
    rows = np.arange(LANES)[:, None]
    cols = np.arange(NSA_Q_WIDTH)[None, :]
    e8 = jnp.asarray(rows == cols // SSM_HEAD_DIM, F32).astype(BF16)
    eg = jnp.stack([jnp.asarray(rows == 3 * (cols // hd) + c, F32) for c in range(3)]).astype(BF16)

    lb_sm = jax.nn.softmax(hg_lb_logits.astype(F32), axis=0)
    lb_all = jnp.cumsum(lb_sm, axis=0) - lb_sm[0:1]

    def pad_lanes(v):
        return jnp.pad(v, (0, LANES - v.shape[0]))[None, :]

    x2 = x.reshape(t, d)
    for l in range(depth):
        proj = _in_proj(x2, norm_mix[l][None, :], _pack_w_in(w_in[l]))

        qn2 = jnp.tile(nsa_q_norm[l], 2)[None, :]
        kn2 = jnp.tile(nsa_k_norm[l], 2)[None, :]
        qp, ksn, kwn, vst, vwt = _nsa_prep(proj, cos_t, sin_t, qn2, kn2, bsz, s)
        w1k = _block_diag2(nsa_cmp_w1_k[l].reshape(NSA_CMP_BLOCK, hd, hd)).astype(BF16)
        w1v = _block_diag2(nsa_cmp_w1_v[l].reshape(NSA_CMP_BLOCK, hd, hd)).astype(BF16)
        kc, vct = _compress(proj, jnp.tile(nsa_cmp_pos_k[l], (1, 2)), jnp.tile(nsa_cmp_pos_v[l], (1, 2)),
                            w1k, _block_diag2(nsa_cmp_w2_k[l]).astype(BF16),
                            w1v, _block_diag2(nsa_cmp_w2_v[l]).astype(BF16), kn2, cos_c, sin_c, bsz, s)
        o_c, sel_bias = _cmp_topk(qp, kc, vct, bsz, s)
        tile_list, tile_count = _active_tiles(sel_bias, bsz, s)
        o_s = _selected(qp, ksn, vst, sel_bias, tile_list, tile_count, bsz, s)
        o_w = _window(qp, kwn, vwt, bsz, s)

        y_b = _ssd(proj, ssm_conv_w[l], ssm_conv_b[l][None, :], pad_lanes(ssm_dt_bias[l]),
                   pad_lanes(ssm_a_log[l]), jnp.repeat(ssm_d[l], SSM_HEAD_DIM)[None, :],
                   ssm_norm[l][None, :], e8, tri, bsz, s)
        y_c = _hgrn2(proj, lb_all[l][None, :], hg_norm[l][None, :], tri, bsz, s)

        x2 = _merge(proj, o_c.reshape(t, -1), o_s.reshape(t, -1), o_w.reshape(t, -1), y_b, y_c, x2, eg,
                    w_nsa_o[l].astype(BF16), w_ssm_o[l].astype(BF16), w_hg_o[l].astype(BF16),
                    w_out[l].astype(BF16))

        k_m, v_m = _xa_mem(mem, norm_mem[l][None, :], xa_w_k[l].astype(BF16), xa_w_v[l].astype(BF16),
                           xa_k_norm[l][None, :])
        x2 = _xa(x2, norm_xa[l][None, :], xa_w_q[l].astype(BF16), xa_q_norm[l][None, :], k_m, v_m,
                 xa_w_o[l].astype(BF16), s)
        x2 = _ffn(x2, norm_ffn[l][None, :], ffn_w_gate[l].astype(BF16), ffn_w_up[l].astype(BF16),
                  ffn_w_down[l].astype(BF16))
    return x2.reshape(bsz, s, d)
```

```python
import functools
import math

import numpy as np
import jax
import jax.numpy as jnp
from jax import lax
from jax.experimental import pallas as pl
from jax.experimental.pallas import tpu as pltpu

F32 = jnp.float32
BF16 = jnp.bfloat16

D_MODEL = 1024
NORM_EPS = 1e-6
ROPE_THETA = 10000.0
NEG_INF = -1e30
FORCE_SCORE = 1e9

NSA_HEADS = 8
NSA_KV_GROUPS = 2
NSA_REP = NSA_HEADS // NSA_KV_GROUPS
NSA_HEAD_DIM = 64
NSA_CMP_BLOCK = 32
NSA_CMP_STRIDE = 16
NSA_SEL_BLOCK = 64
NSA_TOPK = 16
NSA_WINDOW = 512
NSA_Q_WIDTH = NSA_HEADS * NSA_HEAD_DIM
NSA_KV_WIDTH = NSA_KV_GROUPS * NSA_HEAD_DIM

SSM_INNER = 512
SSM_HEAD_DIM = 64
SSM_HEADS = SSM_INNER // SSM_HEAD_DIM
SSM_GROUPS = 2
SSM_STATE = 64
SSM_CONV = 4
SSM_CONV_DIM = SSM_INNER + 2 * SSM_GROUPS * SSM_STATE

HG_HEADS = 4
HG_KEY_DIM = 128
HG_VAL_DIM = 128
HG_WIDTH = HG_HEADS * HG_KEY_DIM
HG_SUB = 16

XA_HEADS = 4
XA_HEAD_DIM = 128
XA_WIDTH = XA_HEADS * XA_HEAD_DIM

FFN_HIDDEN = -(-(8 * D_MODEL) // (3 * 256)) * 256

SPLIT_SIZES = (
    NSA_Q_WIDTH, NSA_KV_WIDTH, NSA_KV_WIDTH, NSA_KV_WIDTH, NSA_KV_WIDTH, NSA_KV_WIDTH, NSA_KV_WIDTH,
    3 * NSA_HEADS, SSM_INNER, SSM_CONV_DIM, SSM_HEADS, HG_WIDTH, HG_WIDTH, HG_HEADS * HG_VAL_DIM,
    HG_HEADS * HG_VAL_DIM, D_MODEL, D_MODEL, D_MODEL,
)

LANES = 128
TOK = 128
SEL_TILE = 256
SEL_V_ROWS = NSA_HEAD_DIM + 16
WIN_TILES = NSA_WINDOW // TOK + 1
LOG2_E = math.log2(math.e)
MASK_FLOOR = -1e20
VMEM_LIMIT = 56 * 1024 * 1024

PROJ_WIDTH = 8192
COL_GA, COL_GB, COL_GC = 0, 1024, 2048
COL_Q = 3072
COL_Z = 3584
COL_HQ, COL_HF, COL_HI, COL_HG = 4096, 4608, 5120, 5632
COL_XBC = 6144
COL_KC, COL_VC, COL_KS, COL_VS, COL_KW, COL_VW = 6912, 7040, 7168, 7296, 7424, 7552
COL_NG = 7680
COL_DT = 7808


def _cparams(sem):
    return pltpu.CompilerParams(dimension_semantics=sem, vmem_limit_bytes=VMEM_LIMIT)


def _dot(a, b):
    return jnp.dot(a, b, preferred_element_type=F32)


def _dot_nt(a, b):
    return lax.dot_general(a, b, (((1,), (1,)), ((), ())), preferred_element_type=F32)


def _split3(a):
    hi = a.astype(BF16)
    r1 = a - hi.astype(F32)
    mid = r1.astype(BF16)
    lo = (r1 - mid.astype(F32)).astype(BF16)
    return hi, mid, lo


def _dot_sel_r(a, sel):
    hi, mid, lo = _split3(a)
    return _dot(hi, sel) + _dot(mid, sel) + _dot(lo, sel)


def _dot_sel_l(sel, a):
    hi, mid, lo = _split3(a)
    return _dot(sel, hi) + _dot(sel, mid) + _dot(sel, lo)


def _silu(x):
    return x * jax.nn.sigmoid(x)


def _rms(x, w):
    return x * lax.rsqrt(jnp.mean(x * x, axis=-1, keepdims=True) + NORM_EPS) * w


def _inproj_kernel(x_ref, nw_ref, w_ref, o_ref, h_ref):
    @pl.when(pl.program_id(1) == 0)
    def _():
        h_ref[...] = _rms(x_ref[...], nw_ref[...]).astype(BF16)

    o_ref[...] = _dot(h_ref[...], w_ref[...])


def _in_proj(x2, norm_w, w_packed):
    t = x2.shape[0]
    tm = min(1024, t)
    tn = 1024
    return pl.pallas_call(
        _inproj_kernel,
        grid=(t // tm, PROJ_WIDTH // tn),
        in_specs=[
            pl.BlockSpec((tm, D_MODEL), lambda i, j: (i, 0)),
            pl.BlockSpec((1, D_MODEL), lambda i, j: (0, 0)),
            pl.BlockSpec((D_MODEL, tn), lambda i, j: (0, j)),
        ],
        out_specs=pl.BlockSpec((tm, tn), lambda i, j: (i, j)),
        out_shape=jax.ShapeDtypeStruct((t, PROJ_WIDTH), F32),
        scratch_shapes=[pltpu.VMEM((tm, D_MODEL), BF16)],
        compiler_params=_cparams(("parallel", "arbitrary")),
        name="in_proj",
    )(x2, norm_w, w_packed)


def _norm_rope_pair(x, w, cos, sin):
    lane = lax.broadcasted_iota(jnp.int32, x.shape, 1)
    lo = lane < NSA_HEAD_DIM
    sq = x * x
    s_lo = jnp.sum(jnp.where(lo, sq, 0.0), axis=1, keepdims=True)
    s_hi = jnp.sum(jnp.where(lo, 0.0, sq), axis=1, keepdims=True)
    ms = jnp.where(lo, s_lo, s_hi) * (1.0 / NSA_HEAD_DIM)
    y = x * lax.rsqrt(ms + NORM_EPS) * w
    half = NSA_HEAD_DIM // 2
    first = (lane % NSA_HEAD_DIM) < half
    rot = jnp.where(first, -pltpu.roll(y, LANES - half, 1), pltpu.roll(y, half, 1))
    return y * cos + rot * sin


def _nsa_prep_kernel(q_ref, ks_ref, vs_ref, kw_ref, vw_ref, cos_ref, sin_ref, qn_ref, kn_ref,
                     qp_ref, ksn_ref, kwn_ref, vst_ref, vwt_ref):
    cos = cos_ref[...]
    sin = sin_ref[...]
    lane = lax.broadcasted_iota(jnp.int32, (TOK, LANES), 1)
    scale = NSA_HEAD_DIM ** -0.5 * LOG2_E
    for c in range(NSA_HEADS // 2):
        y = _norm_rope_pair(q_ref[:, c * LANES:(c + 1) * LANES], qn_ref[...], cos, sin) * scale
        y_sw = pltpu.roll(y, NSA_HEAD_DIM, 1)
        for hh in range(2):
            h = 2 * c + hh
            g, r = divmod(h, NSA_REP)
            src = y if hh == g else y_sw
            keep = (lane // NSA_HEAD_DIM) == g
            qp_ref[0, g, :, r * LANES:(r + 1) * LANES] = jnp.where(keep, src, 0.0).astype(BF16)
    ksn_ref[0] = _norm_rope_pair(ks_ref[...], kn_ref[...], cos, sin).astype(BF16)
    kwn_ref[0] = _norm_rope_pair(kw_ref[...], kn_ref[...], cos, sin).astype(BF16)
    vs_t = vs_ref[...].T.astype(BF16)
    vw_t = vw_ref[...].T.astype(BF16)
    ones_rows = (lax.broadcasted_iota(jnp.int32, (SEL_V_ROWS - NSA_HEAD_DIM, TOK), 0) == 0).astype(BF16)
    for g in range(NSA_KV_GROUPS):
        vst_ref[0, g, 0, 0:NSA_HEAD_DIM, :] = vs_t[g * NSA_HEAD_DIM:(g + 1) * NSA_HEAD_DIM, :]
        vst_ref[0, g, 0, NSA_HEAD_DIM:SEL_V_ROWS, :] = ones_rows
        vwt_ref[0, g, 0] = vw_t[g * NSA_HEAD_DIM:(g + 1) * NSA_HEAD_DIM, :]


def _nsa_prep(proj, cos_t, sin_t, qn2, kn2, bsz, s):
    nt = s // TOK

    def col(c0, w):
        return pl.BlockSpec((TOK, w), lambda b, i: (b * nt + i, c0 // w))

    tab = pl.BlockSpec((TOK, LANES), lambda b, i: (i, 0))
    vec = pl.BlockSpec((1, LANES), lambda b, i: (0, 0))
    return pl.pallas_call(
        _nsa_prep_kernel,
        grid=(bsz, nt),
        in_specs=[col(COL_Q, NSA_Q_WIDTH), col(COL_KS, LANES), col(COL_VS, LANES), col(COL_KW, LANES),
                  col(COL_VW, LANES), tab, tab, vec, vec],
        out_specs=[
            pl.BlockSpec((1, NSA_KV_GROUPS, TOK, NSA_REP * LANES), lambda b, i: (b, 0, i, 0)),
            pl.BlockSpec((1, TOK, LANES), lambda b, i: (b, i, 0)),
            pl.BlockSpec((1, TOK, LANES), lambda b, i: (b, i, 0)),
            pl.BlockSpec((1, NSA_KV_GROUPS, 1, SEL_V_ROWS, TOK), lambda b, i: (b, 0, i // 2, 0, i % 2)),
            pl.BlockSpec((1, NSA_KV_GROUPS, 1, NSA_HEAD_DIM, TOK), lambda b, i: (b, 0, i, 0, 0)),
        ],
        out_shape=[
            jax.ShapeDtypeStruct((bsz, NSA_KV_GROUPS, s, NSA_REP * LANES), BF16),
            jax.ShapeDtypeStruct((bsz, s, LANES), BF16),
            jax.ShapeDtypeStruct((bsz, s, LANES), BF16),
            jax.ShapeDtypeStruct((bsz, NSA_KV_GROUPS, nt // 2, SEL_V_ROWS, SEL_TILE), BF16),
            jax.ShapeDtypeStruct((bsz, NSA_KV_GROUPS, nt, NSA_HEAD_DIM, TOK), BF16),
        ],
        compiler_params=_cparams(("parallel", "parallel")),
        name="nsa_prep",
    )(proj, proj, proj, proj, proj, cos_t, sin_t, qn2, kn2)


def _compress_kernel(kc_ref, vc_ref, posk_ref, posv_ref, w1k_ref, w2k_ref, w1v_ref, w2v_ref, kn_ref,
                     cos_ref, sin_ref, kco_ref, vct_ref, sh_ref, *, nbp):
    half_blk = NSA_CMP_BLOCK // 2

    def compress(x_ref, pos_ref, w1_ref, w2_ref):
        acc_a = jnp.zeros((nbp, LANES), F32)
        acc_b = jnp.zeros((nbp, LANES), F32)
        for l in range(half_blk):
            xl = x_ref[pl.ds(l, nbp, stride=NSA_CMP_STRIDE), :]
            acc_a = acc_a + _dot((xl + pos_ref[l:l + 1, :]).astype(BF16), w1_ref[l])
            acc_b = acc_b + _dot((xl + pos_ref[half_blk + l:half_blk + l + 1, :]).astype(BF16),
                                 w1_ref[half_blk + l])
        sh_ref[pl.ds(0, nbp), :] = acc_b
        sh_ref[pl.ds(nbp, 8), :] = jnp.zeros((8, LANES), F32)
        hid = acc_a + sh_ref[pl.ds(1, nbp), :]
        return _dot(_silu(hid).astype(BF16), w2_ref[...])

    kc = compress(kc_ref, posk_ref, w1k_ref, w2k_ref)
    kco_ref[0] = _norm_rope_pair(kc, kn_ref[...], cos_ref[...], sin_ref[...]).astype(BF16)
    vc = compress(vc_ref, posv_ref, w1v_ref, w2v_ref)
    vct_ref[0] = vc.T.astype(BF16)


def _compress(proj, posk, posv, w1k, w2k, w1v, w2v, kn2, cos_c, sin_c, bsz, s):
    nbp = s // NSA_CMP_STRIDE

    def full(shape):
        return pl.BlockSpec(shape, lambda b: (0,) * len(shape))

    return pl.pallas_call(
        functools.partial(_compress_kernel, nbp=nbp),
        grid=(bsz,),
        in_specs=[
            pl.BlockSpec((s, LANES), lambda b: (b, COL_KC // LANES)),
            pl.BlockSpec((s, LANES), lambda b: (b, COL_VC // LANES)),
            full((NSA_CMP_BLOCK, LANES)), full((NSA_CMP_BLOCK, LANES)),
            full((NSA_CMP_BLOCK, LANES, LANES)), full((LANES, LANES)),
            full((NSA_CMP_BLOCK, LANES, LANES)), full((LANES, LANES)),
            full((1, LANES)), full((nbp, LANES)), full((nbp, LANES)),
        ],
        out_specs=[
            pl.BlockSpec((1, nbp, LANES), lambda b: (b, 0, 0)),
            pl.BlockSpec((1, LANES, nbp), lambda b: (b, 0, 0)),
        ],
        out_shape=[
            jax.ShapeDtypeStruct((bsz, nbp, LANES), BF16),
            jax.ShapeDtypeStruct((bsz, LANES, nbp), BF16),
        ],
        scratch_shapes=[pltpu.VMEM((nbp + 8, LANES), F32)],
        compiler_params=_cparams(("parallel",)),
        name="nsa_compress",
    )(proj, proj, posk, posv, w1k, w2k, w1v, w2v, kn2, cos_c, sin_c)


def _q_rows(qp):
    return jnp.concatenate([qp[:, r * LANES:(r + 1) * LANES] for r in range(NSA_REP)], axis=0)


def _heads_to_token_major(acc_g):
    outs = []
    for p in range(NSA_REP // 2):
        blk = jnp.concatenate([acc_g[:, (2 * p) * TOK:(2 * p + 1) * TOK],
                               acc_g[:, (2 * p + 1) * TOK:(2 * p + 2) * TOK]], axis=0)
        outs.append(blk.T)
    return jnp.concatenate(outs, axis=1)


def _cmp_topk_kernel(qp_ref, kc_ref, vct_ref, oc_ref, mask_ref, ps_ref, imp_ref, *, nbp, nsel):
    qi = pl.program_id(1)
    ncol = NSA_REP * TOK
    per = NSA_SEL_BLOCK // NSA_CMP_STRIDE
    j_io = lax.broadcasted_iota(jnp.int32, (nsel, TOK), 0)
    t_sel = qi * TOK + lax.broadcasted_iota(jnp.int32, (nsel, TOK), 1)
    cur = t_sel // NSA_SEL_BLOCK
    forced = (j_io == 0) | (j_io == cur) | (j_io == cur - 1)
    valid = j_io * NSA_SEL_BLOCK <= t_sel

    def attend(nrows):
        rel = (lax.broadcasted_iota(jnp.int32, (nrows, ncol), 0) * NSA_CMP_STRIDE + (NSA_CMP_BLOCK - 1)
               - lax.broadcasted_iota(jnp.int32, (nrows, ncol), 1) % TOK)
        allowed = rel <= qi * TOK
        kc = kc_ref[0, 0:nrows, :]
        vct = vct_ref[0, :, 0:nrows]
        ps_ref[...] = jnp.zeros(ps_ref.shape, F32)
        outs = []
        for g in range(NSA_KV_GROUPS):
            s_t = jnp.where(allowed, _dot_nt(kc, _q_rows(qp_ref[0, g])), NEG_INF)
            m = jnp.max(s_t, axis=0, keepdims=True)
            p = jnp.exp2(s_t - jnp.maximum(m, MASK_FLOOR))
            l = jnp.sum(p, axis=0, keepdims=True)
            inv = jnp.where(l > 0.0, 1.0 / l, 0.0)
            o_t = _dot(vct, p.astype(BF16)) * inv
            outs.append(_heads_to_token_major(o_t[g * NSA_HEAD_DIM:(g + 1) * NSA_HEAD_DIM, :]))
            pn = p * inv
            psum = pn[:, 0:TOK]
            for r in range(1, NSA_REP):
                psum = psum + pn[:, r * TOK:(r + 1) * TOK]
            ps_ref[pl.ds(8, nrows), :] = psum
            imp = ps_ref[pl.ds(7, nsel, stride=per), :] + ps_ref[pl.ds(8 + per - 1, nsel, stride=per), :]
            for c in range(per - 1):
                imp = imp + 2.0 * ps_ref[pl.ds(8 + c, nsel, stride=per), :]
            imp_ref[g] = jnp.where(forced, FORCE_SCORE, jnp.where(valid, imp, NEG_INF))
        oc_ref[0] = jnp.concatenate(outs, axis=1)

    n_vis = (qi * TOK + TOK - NSA_CMP_BLOCK) // NSA_CMP_STRIDE + 1
    quarter = nbp // 4
    for k in range(1, 5):
        @pl.when((n_vis > (k - 1) * quarter) & (n_vis <= k * quarter))
        def _(k=k):
            attend(k * quarter)

    def pick(_, carry):
        nxt = []
        for imp_c in carry:
            mx = jnp.max(imp_c, axis=0, keepdims=True)
            idx = jnp.min(jnp.where(imp_c == mx, j_io, nsel), axis=0, keepdims=True)
            nxt.append(jnp.where(j_io == idx, -jnp.inf, imp_c))
        return tuple(nxt)

    picked = lax.fori_loop(0, min(NSA_TOPK, nsel), pick,
                           tuple(imp_ref[g] for g in range(NSA_KV_GROUPS)))
    for g in range(NSA_KV_GROUPS):
        mask_ref[0, g, 0] = jnp.where(valid & (picked[g] == -jnp.inf), 0.0, NEG_INF)


def _cmp_topk(qp, kc, vct, bsz, s):
    nt = s // TOK
    nbp = s // NSA_CMP_STRIDE
    nsel = s // NSA_SEL_BLOCK
    return pl.pallas_call(
        functools.partial(_cmp_topk_kernel, nbp=nbp, nsel=nsel),
        grid=(bsz, nt),
        in_specs=[
            pl.BlockSpec((1, NSA_KV_GROUPS, TOK, NSA_REP * LANES), lambda b, i: (b, 0, i, 0)),
            pl.BlockSpec((1, nbp, LANES), lambda b, i: (b, 0, 0)),
            pl.BlockSpec((1, LANES, nbp), lambda b, i: (b, 0, 0)),
        ],
        out_specs=[
            pl.BlockSpec((1, TOK, NSA_Q_WIDTH), lambda b, i: (b, i, 0)),
            pl.BlockSpec((1, NSA_KV_GROUPS, 1, nsel, TOK), lambda b, i: (b, 0, i, 0, 0)),
        ],
        out_shape=[
            jax.ShapeDtypeStruct((bsz, s, NSA_Q_WIDTH), F32),
            jax.ShapeDtypeStruct((bsz, NSA_KV_GROUPS, nt, nsel, TOK), F32),
        ],
        scratch_shapes=[pltpu.VMEM((nbp + 8, TOK), F32), pltpu.VMEM((NSA_KV_GROUPS, nsel, TOK), F32)],
        compiler_params=_cparams(("parallel", "parallel")),
        name="nsa_cmp_topk",
    )(qp, kc, vct)


def _window_kernel(qp_ref, k_ref, vt_ref, o_ref):
    qi = pl.program_id(1)
    ncol = NSA_REP * TOK
    wk = WIN_TILES * TOK

    def run(first_tile, interior):
        kwin = k_ref[0, pl.ds(pl.multiple_of(first_tile * TOK, TOK), wk), :]
        outs = []
        for g in range(NSA_KV_GROUPS):
            s_t = _dot_nt(kwin, _q_rows(qp_ref[0, g]))
            if interior:
                row = lax.broadcasted_iota(jnp.int32, (TOK, ncol), 0)
                col = lax.broadcasted_iota(jnp.int32, (TOK, ncol), 1) % TOK
                oldest = jnp.where(row > col, s_t[:TOK], NEG_INF)
                newest = jnp.where(row <= col, s_t[wk - TOK:], NEG_INF)
                s_t = jnp.concatenate([oldest, s_t[TOK:wk - TOK], newest], axis=0)
            else:
                key = first_tile * TOK + lax.broadcasted_iota(jnp.int32, (wk, ncol), 0)
                t_io = qi * TOK + lax.broadcasted_iota(jnp.int32, (wk, ncol), 1) % TOK
                s_t = jnp.where((key <= t_io) & (t_io - key < NSA_WINDOW), s_t, NEG_INF)
            m = jnp.max(s_t, axis=0, keepdims=True)
            p = jnp.exp2(s_t - m)
            l = jnp.sum(p, axis=0, keepdims=True)
            vt = jnp.concatenate([vt_ref[0, g, first_tile + k] for k in range(WIN_TILES)], axis=1)
            outs.append(_heads_to_token_major(_dot(vt, p.astype(BF16)) / l))
        o_ref[0] = jnp.concatenate(outs, axis=1)

    @pl.when(qi >= WIN_TILES - 1)
    def _():
        run(qi - (WIN_TILES - 1), True)

    @pl.when(qi < WIN_TILES - 1)
    def _():
        run(0, False)


def _window(qp, k, vt, bsz, s):
    nt = s // TOK
    return pl.pallas_call(
        _window_kernel,
        grid=(bsz, nt),
        in_specs=[
            pl.BlockSpec((1, NSA_KV_GROUPS, TOK, NSA_REP * LANES), lambda b, i: (b, 0, i, 0)),
            pl.BlockSpec((1, s, LANES), lambda b, i: (b, 0, 0)),
            pl.BlockSpec((1, NSA_KV_GROUPS, nt, NSA_HEAD_DIM, TOK), lambda b, i: (b, 0, 0, 0, 0)),
        ],
        out_specs=pl.BlockSpec((1, TOK, NSA_Q_WIDTH), lambda b, i: (b, i, 0)),
        out_shape=jax.ShapeDtypeStruct((bsz, s, NSA_Q_WIDTH), F32),
        compiler_params=_cparams(("parallel", "parallel")),
        name="nsa_window",
    )(qp, k, vt)


def _selected_kernel(list_ref, cnt_ref, qp_ref, k_ref, vt_ref, bias_ref, o_ref, m_ref, acc_ref,
                     s_ref, p_ref, a_ref, *, nt):
    b = pl.program_id(0)
    qi = pl.program_id(1)
    ncol = NSA_REP * TOK
    blk_per_tile = SEL_TILE // NSA_SEL_BLOCK
    n_kt = nt // (SEL_TILE // TOK)
    q_rows = [_q_rows(qp_ref[0, g]) for g in range(NSA_KV_GROUPS)]
    m_ref[...] = jnp.full(m_ref.shape, NEG_INF, F32)
    acc_ref[...] = jnp.zeros(acc_ref.shape, F32)
    p_ref[...] = jnp.zeros(p_ref.shape, BF16)
    a_ref[...] = jnp.ones(a_ref.shape, F32)
    row8 = lax.broadcasted_iota(jnp.int32, (8, TOK), 0)
    base = (b * nt + qi) * n_kt
    count = cnt_ref[b * nt + qi]
    n_diag = (qi * TOK) // SEL_TILE

    def tile_at(i):
        return jnp.where(i < count, list_ref[base + jnp.minimum(i, n_kt - 1)], n_diag)

    def scores(j):
        kb = k_ref[0, pl.ds(pl.multiple_of(j * SEL_TILE, SEL_TILE), SEL_TILE), :]
        for g in range(NSA_KV_GROUPS):
            s_ref[g] = _dot_nt(kb, q_rows[g])

    def values(j):
        for g in range(NSA_KV_GROUPS):
            acc_ref[g] = acc_ref[g] * a_ref[g] + _dot(vt_ref[0, g, j], p_ref[g])

    def softmax(j, causal):
        r0 = (j * blk_per_tile) % 8
        for g in range(NSA_KV_GROUPS):
            s_t = s_ref[g]
            b8 = bias_ref[0, g, 0, pl.ds(pl.multiple_of((j * blk_per_tile // 8) * 8, 8), 8), :]
            parts = []
            for c in range(blk_per_tile):
                row = jnp.sum(jnp.where(row8 == r0 + c, b8, 0.0), axis=0, keepdims=True)
                row = jnp.concatenate([row] * NSA_REP, axis=1)
                parts.append(s_t[c * NSA_SEL_BLOCK:(c + 1) * NSA_SEL_BLOCK, :] + row)
            s_t = jnp.concatenate(parts, axis=0)
            if causal:
                t_io = qi * TOK + lax.broadcasted_iota(jnp.int32, (SEL_TILE, ncol), 1) % TOK
                key_pos = j * SEL_TILE + lax.broadcasted_iota(jnp.int32, (SEL_TILE, ncol), 0)
                s_t = jnp.where(key_pos <= t_io, s_t, NEG_INF)
            m_old = m_ref[g]
            m_new = jnp.maximum(m_old, jnp.max(s_t, axis=0, keepdims=True))
            alpha = jnp.exp2(m_old - m_new)
            p = jnp.exp2(s_t - jnp.maximum(m_new, MASK_FLOOR))
            m_ref[g] = m_new
            p_ref[g] = p.astype(BF16)
            a_ref[g] = alpha

    scores(tile_at(0))

    def body(i, c):
        values(tile_at(jnp.maximum(i - 1, 0)))
        softmax(tile_at(i), False)
        scores(tile_at(i + 1))
        return c

    lax.fori_loop(0, count, body, 0)
    values(tile_at(jnp.maximum(count - 1, 0)))
    softmax(n_diag, True)
    values(n_diag)
    outs = []
    for g in range(NSA_KV_GROUPS):
        acc = acc_ref[g]
        outs.append(_heads_to_token_major(acc[:NSA_HEAD_DIM] / acc[NSA_HEAD_DIM:NSA_HEAD_DIM + 1]))
    o_ref[0] = jnp.concatenate(outs, axis=1)


def _selected(qp, k, vt, bias, tile_list, tile_count, bsz, s):
    nt = s // TOK
    nsel = s // NSA_SEL_BLOCK
    ncol = NSA_REP * TOK
    grid_spec = pltpu.PrefetchScalarGridSpec(
        num_scalar_prefetch=2,
        grid=(bsz, nt),
        in_specs=[
            pl.BlockSpec((1, NSA_KV_GROUPS, TOK, NSA_REP * LANES), lambda b, i, tl, tc: (b, 0, i, 0)),
            pl.BlockSpec((1, s, LANES), lambda b, i, tl, tc: (b, 0, 0)),
            pl.BlockSpec((1, NSA_KV_GROUPS, s // SEL_TILE, SEL_V_ROWS, SEL_TILE),
                         lambda b, i, tl, tc: (b, 0, 0, 0, 0)),
            pl.BlockSpec((1, NSA_KV_GROUPS, 1, nsel, TOK), lambda b, i, tl, tc: (b, 0, i, 0, 0)),
        ],
        out_specs=pl.BlockSpec((1, TOK, NSA_Q_WIDTH), lambda b, i, tl, tc: (b, i, 0)),
        scratch_shapes=[pltpu.VMEM((NSA_KV_GROUPS, 1, ncol), F32),
                        pltpu.VMEM((NSA_KV_GROUPS, SEL_V_ROWS, ncol), F32),
                        pltpu.VMEM((NSA_KV_GROUPS, SEL_TILE, ncol), F32),
                        pltpu.VMEM((NSA_KV_GROUPS, SEL_TILE, ncol), BF16),
                        pltpu.VMEM((NSA_KV_GROUPS, 1, ncol), F32)],
    )
    return pl.pallas_call(
        functools.partial(_selected_kernel, nt=nt),
        grid_spec=grid_spec,
        out_shape=jax.ShapeDtypeStruct((bsz, s, NSA_Q_WIDTH), F32),
        compiler_params=_cparams(("parallel", "parallel")),
        name="nsa_selected",
    )(tile_list, tile_count, qp, k, vt, bias)


def _active_tiles(bias, bsz, s):
    nt = s // TOK
    n_kt = s // SEL_TILE
    rows = SEL_TILE // NSA_SEL_BLOCK * TOK
    act = bias.reshape(bsz, NSA_KV_GROUPS, nt, n_kt, rows).max(axis=(1, 4)) > -1.0
    n_diag = (jnp.arange(nt) * TOK) // SEL_TILE
    act = act & (jnp.arange(n_kt)[None, None, :] < n_diag[None, :, None])
    order = jnp.argsort(jnp.where(act, 0, 1), axis=-1, stable=True).astype(jnp.int32)
    return order.reshape(-1), jnp.sum(act, axis=-1, dtype=jnp.int32).reshape(-1)


def _ssd_kernel(z_ref, xbc_ref, dt_ref, cw_ref, cb_ref, dtb_ref, alog_ref, dskip_ref, nw_ref, e8_ref,
                tri_ref, y_ref, ext_ref, st_ref):
    @pl.when(pl.program_id(1) == 0)
    def _():
        ext_ref[pl.ds(0, 8), :] = jnp.zeros((8, SSM_CONV_DIM), F32)
        st_ref[...] = jnp.zeros_like(st_ref)

    xb = xbc_ref[...]
    ext_ref[pl.ds(8, TOK), :] = xb
    conv = cb_ref[...] + xb * cw_ref[SSM_CONV - 1:SSM_CONV, :]
    for k in range(SSM_CONV - 1):
        conv = conv + ext_ref[pl.ds(8 - (SSM_CONV - 1) + k, TOK), :] * cw_ref[k:k + 1, :]
    ext_ref[pl.ds(0, 8), :] = xb[TOK - 8:TOK, :]
    u = _silu(conv)
    xs = u[:, :SSM_INNER]
    gn = SSM_GROUPS * SSM_STATE
    bm = u[:, SSM_INNER:SSM_INNER + gn]
    cm = u[:, SSM_INNER + gn:]
    dt = jax.nn.softplus(dt_ref[...] + dtb_ref[...])
    da = dt * (-jnp.exp(alog_ref[...]))
    tri = tri_ref[...]
    e8 = e8_ref[...]
    cum = _dot_sel_l(tri, da)
    cum_t = cum.T
    cum_e = _dot_sel_r(cum, e8)
    dt_e = _dot_sel_r(dt, e8)
    xdt = xs * dt_e
    lane = lax.broadcasted_iota(jnp.int32, (TOK, LANES), 1)
    t_io = lax.broadcasted_iota(jnp.int32, (TOK, TOK), 0)
    s_io = lax.broadcasted_iota(jnp.int32, (TOK, TOK), 1)
    causal = s_io <= t_io
    hg = SSM_HEADS // SSM_GROUPS
    y_parts = []
    cbs = []
    for g in range(SSM_GROUPS):
        in_g = (lane // SSM_STATE) == g
        cbs.append(_dot_nt(jnp.where(in_g, cm, 0.0).astype(BF16), bm.astype(BF16)))
    for c in range(SSM_HEADS // 2):
        acc = jnp.zeros((TOK, LANES), F32)
        xpair = xdt[:, c * LANES:(c + 1) * LANES]
        for hh in range(2):
            h = 2 * c + hh
            seg = cum[:, h:h + 1] - cum_t[h:h + 1, :]
            decay = jnp.where(causal, jnp.exp(jnp.where(causal, seg, 0.0)), 0.0)
            mat = (cbs[h // hg] * decay).astype(BF16)
            xh = jnp.where((lane // SSM_HEAD_DIM) == hh, xpair, 0.0).astype(BF16)
            acc = acc + _dot(mat, xh)
        y_parts.append(acc)
    y_diag = jnp.concatenate(y_parts, axis=1)
    prev = st_ref[...]
    y_off = _dot(cm.astype(BF16), prev.astype(BF16)) * jnp.exp(cum_e)
    cum_last = cum_e[TOK - 1:TOK, :]
    xdec = (xdt * jnp.exp(cum_last - cum_e)).astype(BF16)
    new = _dot(bm.T.astype(BF16), xdec)
    row_g = lax.broadcasted_iota(jnp.int32, (LANES, SSM_INNER), 0) // SSM_STATE
    col_g = lax.broadcasted_iota(jnp.int32, (LANES, SSM_INNER), 1) // (hg * SSM_HEAD_DIM)
    st_ref[...] = prev * jnp.exp(cum_last) + jnp.where(row_g == col_g, new, 0.0)
    y = (y_diag + y_off + xs * dskip_ref[...]) * _silu(z_ref[...])
    gw = SSM_INNER // SSM_GROUPS
    outs = []
    for g in range(SSM_GROUPS):
        yg = y[:, g * gw:(g + 1) * gw]
        outs.append(_rms(yg, nw_ref[:, g * gw:(g + 1) * gw]))
    y_ref[...] = jnp.concatenate(outs, axis=1)


def _ssd(proj, cw, cb, dtb, alog, dskip, nw, e8, tri, bsz, s):
    nt = s // TOK

    def col(c0, w):
        return pl.BlockSpec((TOK, w), lambda b, i: (b * nt + i, c0 // w))

    def full(shape):
        return pl.BlockSpec(shape, lambda b, i: (0,) * len(shape))

    return pl.pallas_call(
        _ssd_kernel,
        grid=(bsz, nt),
        in_specs=[col(COL_Z, SSM_INNER), col(COL_XBC, SSM_CONV_DIM), col(COL_DT, LANES),
                  full((SSM_CONV, SSM_CONV_DIM)), full((1, SSM_CONV_DIM)), full((1, LANES)), full((1, LANES)),
                  full((1, SSM_INNER)), full((1, SSM_INNER)), full((LANES, SSM_INNER)), full((TOK, TOK))],
        out_specs=pl.BlockSpec((TOK, SSM_INNER), lambda b, i: (b * nt + i, 0)),
        out_shape=jax.ShapeDtypeStruct((bsz * s, SSM_INNER), F32),
        scratch_shapes=[pltpu.VMEM((TOK + 8, SSM_CONV_DIM), F32), pltpu.VMEM((LANES, SSM_INNER), F32)],
        compiler_params=_cparams(("parallel", "arbitrary")),
        name="ssd",
    )(proj, proj, proj, cw, cb, dtb, alog, dskip, nw, e8, tri)


def _hgrn2_kernel(q_ref, f_ref, i_ref, g_ref, lb_ref, nw_ref, tri_ref, y_ref, st_ref, kp_ref, bp_ref, vp_ref):
    @pl.when(pl.program_id(1) == 0)
    def _():
        st_ref[...] = jnp.zeros_like(st_ref)
        kp_ref[...] = jnp.zeros_like(kp_ref)
        bp_ref[...] = jnp.zeros_like(bp_ref)
        vp_ref[...] = jnp.zeros_like(vp_ref)

    nsub = TOK // HG_SUB
    row = lax.broadcasted_iota(jnp.int32, (TOK, HG_KEY_DIM), 0)
    sub_id = row // HG_SUB
    in_sub = row % HG_SUB
    tri = tri_ref[...]
    outs = []
    for h in range(HG_HEADS):
        sl = slice(h * HG_KEY_DIM, (h + 1) * HG_KEY_DIM)
        lb = lb_ref[:, sl]
        fr = f_ref[:, sl]
        q = _silu(q_ref[:, sl]) * (HG_KEY_DIM ** -0.5)
        log_f = jnp.log(lb + (1.0 - lb) * jax.nn.sigmoid(fr))
        k = (1.0 - lb) * jax.nn.sigmoid(-fr)
        v = i_ref[:, sl]
        bcum = _dot_sel_l(tri, log_f)
        st = st_ref[h]
        o = _dot_nt((q * jnp.exp(bcum)).astype(BF16), st.astype(BF16))
        r_rows = [bcum[i * HG_SUB - 1:i * HG_SUB, :] for i in range(1, nsub)]
        r_loc = jnp.zeros_like(bcum)
        for i in range(1, nsub):
            r_loc = jnp.where(sub_id == i, r_rows[i - 1], r_loc)
        q_loc = q * jnp.exp(bcum - r_loc)
        q_aug = jnp.concatenate(
            [jnp.where(sub_id == i, q_loc, 0.0).astype(BF16) for i in range(1, nsub)], axis=1)
        k_aug = jnp.concatenate(
            [jnp.where(sub_id < i, k * jnp.exp(jnp.where(sub_id < i, r_rows[i - 1] - bcum, 0.0)), 0.0).astype(BF16)
             for i in range(1, nsub)], axis=1)
        att = _dot_nt(q_aug, k_aug)
        o = o + _dot(att.astype(BF16), v.astype(BF16))
        kp_ref[pl.ds(HG_SUB, TOK), :] = k
        bp_ref[pl.ds(HG_SUB, TOK), :] = bcum
        vp_ref[pl.ds(HG_SUB, TOK), :] = v
        o = o + jnp.sum(q * k, axis=1, keepdims=True) * v
        for d in range(1, HG_SUB):
            ks = kp_ref[pl.ds(HG_SUB - d, TOK), :]
            bs = bp_ref[pl.ds(HG_SUB - d, TOK), :]
            vs = vp_ref[pl.ds(HG_SUB - d, TOK), :]
            ok = in_sub >= d
            w = jnp.sum(q * ks * jnp.exp(jnp.where(ok, bcum - bs, 0.0)), axis=1, keepdims=True)
            o = o + jnp.where(ok, w * vs, 0.0)
        b_last = bcum[TOK - 1:TOK, :]
        kdec = (k * jnp.exp(b_last - bcum)).astype(BF16)
        st_ref[h] = st * jnp.exp(b_last) + _dot(v.T.astype(BF16), kdec)
        outs.append(_rms(o, nw_ref[...]) * _silu(g_ref[:, sl]))
    y_ref[...] = jnp.concatenate(outs, axis=1)


def _hgrn2(proj, lb, nw, tri, bsz, s):
    nt = s // TOK

    def col(c0):
        return pl.BlockSpec((TOK, HG_WIDTH), lambda b, i: (b * nt + i, c0 // HG_WIDTH))

    def full(shape):
        return pl.BlockSpec(shape, lambda b, i: (0,) * len(shape))

    pad = pltpu.VMEM((TOK + HG_SUB, HG_KEY_DIM), F32)
    return pl.pallas_call(
        _hgrn2_kernel,
        grid=(bsz, nt),
        in_specs=[col(COL_HQ), col(COL_HF), col(COL_HI), col(COL_HG), full((1, HG_WIDTH)),
                  full((1, HG_VAL_DIM)), full((TOK, TOK))],
        out_specs=pl.BlockSpec((TOK, HG_WIDTH), lambda b, i: (b * nt + i, 0)),
        out_shape=jax.ShapeDtypeStruct((bsz * s, HG_WIDTH), F32),
        scratch_shapes=[pltpu.VMEM((HG_HEADS, HG_VAL_DIM, HG_KEY_DIM), F32), pad, pad, pad],
        compiler_params=_cparams(("parallel", "arbitrary")),
        name="hgrn2",
    )(proj, proj, proj, proj, lb, nw, tri)


def _merge_kernel(ga_ref, gb_ref, gc_ref, ng_ref, oc_ref, os_ref, ow_ref, yb_ref, yc_ref, x_ref,
                  eg_ref, wa_ref, wb_ref, wc_ref, wo_ref, o_ref):
    sg = jax.nn.sigmoid(ng_ref[...])
    nsa = (_dot_sel_r(sg, eg_ref[0]) * oc_ref[...] + _dot_sel_r(sg, eg_ref[1]) * os_ref[...]
           + _dot_sel_r(sg, eg_ref[2]) * ow_ref[...])
    ya = _dot(nsa.astype(BF16), wa_ref[...])
    yb = _dot(yb_ref[...].astype(BF16), wb_ref[...])
    yc = _dot(yc_ref[...].astype(BF16), wc_ref[...])
    merged = (jax.nn.sigmoid(ga_ref[...]) * ya + jax.nn.sigmoid(gb_ref[...]) * yb
              + jax.nn.sigmoid(gc_ref[...]) * yc)
    o_ref[...] = x_ref[...] + _dot(merged.astype(BF16), wo_ref[...])


def _merge(proj, oc, os_, ow, yb, yc, x2, eg, wa, wb, wc, wo):
    t = x2.shape[0]
    tm = min(512, t)

    def col(c0, w):
        return pl.BlockSpec((tm, w), lambda i: (i, c0 // w))

    def full(shape):
        return pl.BlockSpec(shape, lambda i: (0,) * len(shape))

    row512 = pl.BlockSpec((tm, 512), lambda i: (i, 0))
    return pl.pallas_call(
        _merge_kernel,
        grid=(t // tm,),
        in_specs=[col(COL_GA, D_MODEL), col(COL_GB, D_MODEL), col(COL_GC, D_MODEL), col(COL_NG, LANES),
                  row512, row512, row512, row512, row512,
                  pl.BlockSpec((tm, D_MODEL), lambda i: (i, 0)),
                  full((3, LANES, NSA_Q_WIDTH)), full((NSA_Q_WIDTH, D_MODEL)), full((SSM_INNER, D_MODEL)),
                  full((HG_WIDTH, D_MODEL)), full((D_MODEL, D_MODEL))],
        out_specs=pl.BlockSpec((tm, D_MODEL), lambda i: (i, 0)),
        out_shape=jax.ShapeDtypeStruct((t, D_MODEL), F32),
        compiler_params=_cparams(("parallel",)),
        name="merge",
    )(proj, proj, proj, proj, oc, os_, ow, yb, yc, x2, eg, wa, wb, wc, wo)


def _xa_mem_kernel(mem_ref, nm_ref, wk_ref, wv_ref, kn_ref, k_ref, v_ref):
    m = _rms(mem_ref[0], nm_ref[...]).astype(BF16)
    k = _dot(m, wk_ref[...])
    ks = [_rms(k[:, h * XA_HEAD_DIM:(h + 1) * XA_HEAD_DIM], kn_ref[...]) for h in range(XA_HEADS)]
    k_ref[0] = jnp.concatenate(ks, axis=1).astype(BF16)
    v_ref[0] = _dot(m, wv_ref[...]).astype(BF16)


def _xa_mem(mem, nm, wk, wv, kn):
    bsz, ml, _ = mem.shape

    def full(shape):
        return pl.BlockSpec(shape, lambda b: (0,) * len(shape))

    return pl.pallas_call(
        _xa_mem_kernel,
        grid=(bsz,),
        in_specs=[pl.BlockSpec((1, ml, D_MODEL), lambda b: (b, 0, 0)), full((1, D_MODEL)),
                  full((D_MODEL, XA_WIDTH)), full((D_MODEL, XA_WIDTH)), full((1, XA_HEAD_DIM))],
        out_specs=[pl.BlockSpec((1, ml, XA_WIDTH), lambda b: (b, 0, 0)),
                   pl.BlockSpec((1, ml, XA_WIDTH), lambda b: (b, 0, 0))],
        out_shape=[jax.ShapeDtypeStruct((bsz, ml, XA_WIDTH), BF16),
                   jax.ShapeDtypeStruct((bsz, ml, XA_WIDTH), BF16)],
        compiler_params=_cparams(("parallel",)),
        name="xa_mem",
    )(mem, nm, wk, wv, kn)


def _xa_kernel(x_ref, nx_ref, wq_ref, qn_ref, k_ref, v_ref, wo_ref, o_ref):
    x = x_ref[...]
    h = _rms(x, nx_ref[...]).astype(BF16)
    q = _dot(h, wq_ref[...])
    outs = []
    for hd in range(XA_HEADS):
        sl = slice(hd * XA_HEAD_DIM, (hd + 1) * XA_HEAD_DIM)
        qh = _rms(q[:, sl], qn_ref[...]).astype(BF16)
        sc = _dot_nt(qh, k_ref[0, :, sl]) * (XA_HEAD_DIM ** -0.5)
        sc = sc - jnp.max(sc, axis=-1, keepdims=True)
        p = jnp.exp(sc)
        p = p / jnp.sum(p, axis=-1, keepdims=True)
        outs.append(_dot(p.astype(BF16), v_ref[0, :, sl]))
    o = jnp.concatenate(outs, axis=1).astype(BF16)
    o_ref[...] = x + _dot(o, wo_ref[...])


def _xa(x2, nx, wq, qn, k, v, wo, s):
    t = x2.shape[0]
    tm = min(512, s)
    ml = k.shape[1]
    per_b = s // tm

    def full(shape):
        return pl.BlockSpec(shape, lambda i: (0,) * len(shape))

    return pl.pallas_call(
        _xa_kernel,
        grid=(t // tm,),
        in_specs=[pl.BlockSpec((tm, D_MODEL), lambda i: (i, 0)), full((1, D_MODEL)), full((D_MODEL, XA_WIDTH)),
                  full((1, XA_HEAD_DIM)),
                  pl.BlockSpec((1, ml, XA_WIDTH), lambda i: (i // per_b, 0, 0)),
                  pl.BlockSpec((1, ml, XA_WIDTH), lambda i: (i // per_b, 0, 0)),
                  full((XA_WIDTH, D_MODEL))],
        out_specs=pl.BlockSpec((tm, D_MODEL), lambda i: (i, 0)),
        out_shape=jax.ShapeDtypeStruct((t, D_MODEL), F32),
        compiler_params=_cparams(("parallel",)),
        name="cross_attn",
    )(x2, nx, wq, qn, k, v, wo)


def _ffn_kernel(x_ref, nw_ref, wg_ref, wu_ref, wd_ref, o_ref, h_ref, acc_ref):
    j = pl.program_id(1)

    @pl.when(j == 0)
    def _():
        h_ref[...] = _rms(x_ref[...], nw_ref[...]).astype(BF16)
        acc_ref[...] = jnp.zeros_like(acc_ref)

    h = h_ref[...]
    a = _silu(_dot(h, wg_ref[...])) * _dot(h, wu_ref[...])
    acc_ref[...] += _dot(a.astype(BF16), wd_ref[...])

    @pl.when(j == pl.num_programs(1) - 1)
    def _():
        o_ref[...] = x_ref[...] + acc_ref[...]


def _ffn(x2, nw, wg, wu, wd):
    t = x2.shape[0]
    tm = min(1024, t)
    th = FFN_HIDDEN // 2
    return pl.pallas_call(
        _ffn_kernel,
        grid=(t // tm, FFN_HIDDEN // th),
        in_specs=[pl.BlockSpec((tm, D_MODEL), lambda i, j: (i, 0)),
                  pl.BlockSpec((1, D_MODEL), lambda i, j: (0, 0)),
                  pl.BlockSpec((D_MODEL, th), lambda i, j: (0, j)),
                  pl.BlockSpec((D_MODEL, th), lambda i, j: (0, j)),
                  pl.BlockSpec((th, D_MODEL), lambda i, j: (j, 0))],
        out_specs=pl.BlockSpec((tm, D_MODEL), lambda i, j: (i, 0)),
        out_shape=jax.ShapeDtypeStruct((t, D_MODEL), F32),
        scratch_shapes=[pltpu.VMEM((tm, D_MODEL), BF16), pltpu.VMEM((tm, D_MODEL), F32)],
        compiler_params=_cparams(("parallel", "arbitrary")),
        name="ffn",
    )(x2, nw, wg, wu, wd)


def _pack_w_in(w):
    parts = {}
    off = 0
    names = ("q", "kc", "vc", "ks", "vs", "kw", "vw", "ng", "z", "xbc", "dt", "hq", "hf", "hi", "hg",
             "ga", "gb", "gc")
    for name, size in zip(names, SPLIT_SIZES):
        parts[name] = w[:, off:off + size]
        off += size

    def padded(a, width):
        return jnp.pad(a, ((0, 0), (0, width - a.shape[1])))

    cols = [parts["ga"], parts["gb"], parts["gc"], parts["q"], parts["z"], parts["hq"], parts["hf"],
            parts["hi"], parts["hg"], parts["xbc"], parts["kc"], parts["vc"], parts["ks"], parts["vs"],
            parts["kw"], parts["vw"], padded(parts["ng"], LANES), padded(parts["dt"], LANES)]
    packed = jnp.concatenate(cols, axis=1)
    return padded(packed, PROJ_WIDTH).astype(BF16)


def _rope_tables(pos):
    half = NSA_HEAD_DIM // 2
    inv = 1.0 / (ROPE_THETA ** (jnp.arange(half, dtype=F32) / half))
    ang = pos.astype(F32)[:, None] * inv[None, :]
    reps = LANES // half
    return jnp.tile(jnp.cos(ang), (1, reps)), jnp.tile(jnp.sin(ang), (1, reps))


def _block_diag2(w):
    z = jnp.zeros_like(w)
    return jnp.concatenate([jnp.concatenate([w, z], axis=-1), jnp.concatenate([z, w], axis=-1)], axis=-2)


def kernel(x, mem, norm_mix, w_in, nsa_q_norm, nsa_k_norm, nsa_cmp_pos_k, nsa_cmp_pos_v, nsa_cmp_w1_k, nsa_cmp_w2_k, nsa_cmp_w1_v, nsa_cmp_w2_v, w_nsa_o, ssm_conv_w, ssm_conv_b, ssm_dt_bias, ssm_a_log, ssm_d, ssm_norm, w_ssm_o, hg_lb_logits, hg_norm, w_hg_o, w_out, norm_xa, norm_mem, xa_w_q, xa_w_k, xa_w_v, xa_q_norm, xa_k_norm, xa_w_o, norm_ffn, ffn_w_gate, ffn_w_up, ffn_w_down):
    bsz, s, d = x.shape
    depth = w_in.shape[0]
    assert d == D_MODEL and s % (8 * TOK) == 0 and s // TOK <= TOK
    t = bsz * s
    hd = NSA_HEAD_DIM

    cos_t, sin_t = _rope_tables(jnp.arange(s))
    nbp = s // NSA_CMP_STRIDE
    cos_c, sin_c = _rope_tables(jnp.arange(nbp) * NSA_CMP_STRIDE + NSA_CMP_BLOCK - 1)
    tri = jnp.tril(jnp.ones((TOK, TOK), F32)).astype(BF16)
    rows = np.arange(LANES)[:, None]
    cols = np.arange(NSA_Q_WIDTH)[None, :]
    e8 = jnp.asarray(rows == cols // SSM_HEAD_DIM, F32).astype(BF16)
    eg = jnp.stack([jnp.asarray(rows == 3 * (cols // hd) + c, F32) for c in range(3)]).astype(BF16)

    lb_sm = jax.nn.softmax(hg_lb_logits.astype(F32), axis=0)
    lb_all = jnp.cumsum(lb_sm, axis=0) - lb_sm[0:1]

    def pad_lanes(v):
        return jnp.pad(v, (0, LANES - v.shape[0]))[None, :]

    x2 = x.reshape(t, d)
    for l in range(depth):
        proj = _in_proj(x2, norm_mix[l][None, :], _pack_w_in(w_in[l]))

        qn2 = jnp.tile(nsa_q_norm[l], 2)[None, :]
        kn2 = jnp.tile(nsa_k_norm[l], 2)[None, :]
        qp, ksn, kwn, vst, vwt = _nsa_prep(proj, cos_t, sin_t, qn2, kn2, bsz, s)
        w1k = _block_diag2(nsa_cmp_w1_k[l].reshape(NSA_CMP_BLOCK, hd, hd)).astype(BF16)
        w1v = _block_diag2(nsa_cmp_w1_v[l].reshape(NSA_CMP_BLOCK, hd, hd)).astype(BF16)
        kc, vct = _compress(proj, jnp.tile(nsa_cmp_pos_k[l], (1, 2)), jnp.tile(nsa_cmp_pos_v[l], (1, 2)),
                            w1k, _block_diag2(nsa_cmp_w2_k[l]).astype(BF16),
                            w1v, _block_diag2(nsa_cmp_w2_v[l]).astype(BF16), kn2, cos_c, sin_c, bsz, s)
        o_c, sel_bias = _cmp_topk(qp, kc, vct, bsz, s)
        tile_list, tile_count = _active_tiles(sel_bias, bsz, s)
        o_s = _selected(qp, ksn, vst, sel_bias, tile_list, tile_count, bsz, s)
        o_w = _window(qp, kwn, vwt, bsz, s)

        y_b = _ssd(proj, ssm_conv_w[l], ssm_conv_b[l][None, :], pad_lanes(ssm_dt_bias[l]),
                   pad_lanes(ssm_a_log[l]), jnp.repeat(ssm_d[l], SSM_HEAD_DIM)[None, :],
                   ssm_norm[l][None, :], e8, tri, bsz, s)
        y_c = _hgrn2(proj, lb_all[l][None, :], hg_norm[l][None, :], tri, bsz, s)

        x2 = _merge(proj, o_c.reshape(t, -1), o_s.reshape(t, -1), o_w.reshape(t, -1), y_b, y_c, x2, eg,
                    w_nsa_o[l].astype(BF16), w_ssm_o[l].astype(BF16), w_hg_o[l].astype(BF16),
                    w_out[l].astype(BF16))

        k_m, v_m = _xa_mem(mem, norm_mem[l][None, :], xa_w_k[l].astype(BF16), xa_w_v[l].astype(BF16),
                           xa_k_norm[l][None, :])
        x2 = _xa(x2, norm_xa[l][None, :], xa_w_q[l].astype(BF16), xa_q_norm[l][None, :], k_m, v_m,
                 xa_w_o[l].astype(BF16), s)
        x2 = _ffn(x2, norm_ffn[l][None, :], ffn_w_gate[l].astype(BF16), ffn_w_up[l].astype(BF16),
                  ffn_w_down[l].astype(BF16))
    return x2.reshape(bsz, s, d)
```

```python
import functools
import math

import numpy as np
import jax
import jax.numpy as jnp
from jax import lax
from jax.experimental import pallas as pl
from jax.experimental.pallas import tpu as pltpu

F32 = jnp.float32
BF16 = jnp.bfloat16

D_MODEL = 1024
NORM_EPS = 1e-6
ROPE_THETA = 10000.0
NEG_INF = -1e30
FORCE_SCORE = 1e9

NSA_HEADS = 8
NSA_KV_GROUPS = 2
NSA_REP = NSA_HEADS // NSA_KV_GROUPS
NSA_HEAD_DIM = 64
NSA_CMP_BLOCK = 32
NSA_CMP_STRIDE = 16
NSA_SEL_BLOCK = 64
NSA_TOPK = 16
N_FORCED = 3
NSA_WINDOW = 512
NSA_Q_WIDTH = NSA_HEADS * NSA_HEAD_DIM
NSA_KV_WIDTH = NSA_KV_GROUPS * NSA_HEAD_DIM

SSM_INNER = 512
SSM_HEAD_DIM = 64
SSM_HEADS = SSM_INNER // SSM_HEAD_DIM
SSM_GROUPS = 2
SSM_STATE = 64
SSM_CONV = 4
SSM_CONV_DIM = SSM_INNER + 2 * SSM_GROUPS * SSM_STATE

HG_HEADS = 4
HG_KEY_DIM = 128
HG_VAL_DIM = 128
HG_WIDTH = HG_HEADS * HG_KEY_DIM
HG_SUB = 16

XA_HEADS = 4
XA_HEAD_DIM = 128
XA_WIDTH = XA_HEADS * XA_HEAD_DIM

FFN_HIDDEN = -(-(8 * D_MODEL) // (3 * 256)) * 256

SPLIT_SIZES = (
    NSA_Q_WIDTH, NSA_KV_WIDTH, NSA_KV_WIDTH, NSA_KV_WIDTH, NSA_KV_WIDTH, NSA_KV_WIDTH, NSA_KV_WIDTH,
    3 * NSA_HEADS, SSM_INNER, SSM_CONV_DIM, SSM_HEADS, HG_WIDTH, HG_WIDTH, HG_HEADS * HG_VAL_DIM,
    HG_HEADS * HG_VAL_DIM, D_MODEL, D_MODEL, D_MODEL,
)

LANES = 128
TOK = 128
SEL_TILE = 512
SEL_V_ROWS = NSA_HEAD_DIM + 16
WIN_TILES = NSA_WINDOW // TOK + 1
LOG2_E = math.log2(math.e)
MASK_FLOOR = -1e20
VMEM_LIMIT = 56 * 1024 * 1024

PROJ_WIDTH = 8192
COL_GA, COL_GB, COL_GC = 0, 1024, 2048
COL_Q = 3072
COL_Z = 3584
COL_HQ, COL_HF, COL_HI, COL_HG = 4096, 4608, 5120, 5632
COL_XBC = 6144
COL_KC, COL_VC, COL_KS, COL_VS, COL_KW, COL_VW = 6912, 7040, 7168, 7296, 7424, 7552
COL_NG = 7680
COL_DT = 7808


def _cparams(sem):
    return pltpu.CompilerParams(dimension_semantics=sem, vmem_limit_bytes=VMEM_LIMIT)


def _dot(a, b):
    return jnp.dot(a, b, preferred_element_type=F32)


def _dot_nt(a, b):
    return lax.dot_general(a, b, (((1,), (1,)), ((), ())), preferred_element_type=F32)


def _split3(a):
    hi = a.astype(BF16)
    r1 = a - hi.astype(F32)
    mid = r1.astype(BF16)
    lo = (r1 - mid.astype(F32)).astype(BF16)
    return hi, mid, lo


def _dot_sel_r(a, sel):
    hi, mid, lo = _split3(a)
    return _dot(hi, sel) + _dot(mid, sel) + _dot(lo, sel)


def _dot_sel_l(sel, a):
    hi, mid, lo = _split3(a)
    return _dot(sel, hi) + _dot(sel, mid) + _dot(sel, lo)


def _silu(x):
    return x * jax.nn.sigmoid(x)


def _rms(x, w):
    return x * lax.rsqrt(jnp.mean(x * x, axis=-1, keepdims=True) + NORM_EPS) * w


def _inproj_kernel(x_ref, nw_ref, w_ref, o_ref, h_ref):
    @pl.when(pl.program_id(1) == 0)
    def _():
        h_ref[...] = _rms(x_ref[...], nw_ref[...]).astype(BF16)

    o_ref[...] = _dot(h_ref[...], w_ref[...])


def _in_proj(x2, norm_w, w_packed):
    t = x2.shape[0]
    tm = min(1024, t)
    tn = 1024
    return pl.pallas_call(
        _inproj_kernel,
        grid=(t // tm, PROJ_WIDTH // tn),
        in_specs=[
            pl.BlockSpec((tm, D_MODEL), lambda i, j: (i, 0)),
            pl.BlockSpec((1, D_MODEL), lambda i, j: (0, 0)),
            pl.BlockSpec((D_MODEL, tn), lambda i, j: (0, j)),
        ],
        out_specs=pl.BlockSpec((tm, tn), lambda i, j: (i, j)),
        out_shape=jax.ShapeDtypeStruct((t, PROJ_WIDTH), F32),
        scratch_shapes=[pltpu.VMEM((tm, D_MODEL), BF16)],
        compiler_params=_cparams(("parallel", "arbitrary")),
        name="in_proj",
    )(x2, norm_w, w_packed)


def _norm_rope_pair(x, w, cos, sin):
    lane = lax.broadcasted_iota(jnp.int32, x.shape, 1)
    lo = lane < NSA_HEAD_DIM
    sq = x * x
    s_lo = jnp.sum(jnp.where(lo, sq, 0.0), axis=1, keepdims=True)
    s_hi = jnp.sum(jnp.where(lo, 0.0, sq), axis=1, keepdims=True)
    ms = jnp.where(lo, s_lo, s_hi) * (1.0 / NSA_HEAD_DIM)
    y = x * lax.rsqrt(ms + NORM_EPS) * w
    half = NSA_HEAD_DIM // 2
    first = (lane % NSA_HEAD_DIM) < half
    rot = jnp.where(first, -pltpu.roll(y, LANES - half, 1), pltpu.roll(y, half, 1))
    return y * cos + rot * sin


def _nsa_prep_kernel(q_ref, ks_ref, vs_ref, kw_ref, vw_ref, cos_ref, sin_ref, qn_ref, kn_ref,
                     qp_ref, ksn_ref, kwn_ref, vst_ref, vwt_ref):
    cos = cos_ref[...]
    sin = sin_ref[...]
    lane = lax.broadcasted_iota(jnp.int32, (TOK, LANES), 1)
    scale = NSA_HEAD_DIM ** -0.5 * LOG2_E
    for c in range(NSA_HEADS // 2):
        y = _norm_rope_pair(q_ref[:, c * LANES:(c + 1) * LANES], qn_ref[...], cos, sin) * scale
        y_sw = pltpu.roll(y, NSA_HEAD_DIM, 1)
        for hh in range(2):
            h = 2 * c + hh
            g, r = divmod(h, NSA_REP)
            src = y if hh == g else y_sw
            keep = (lane // NSA_HEAD_DIM) == g
            qp_ref[0, g, :, r * LANES:(r + 1) * LANES] = jnp.where(keep, src, 0.0).astype(BF16)
    ks_n = _norm_rope_pair(ks_ref[...], kn_ref[...], cos, sin)
    row = lax.broadcasted_iota(jnp.int32, (TOK, LANES), 0)
    blk = (pl.program_id(1) * (TOK // NSA_SEL_BLOCK) + row // NSA_SEL_BLOCK) % NSA_HEAD_DIM
    onehot = ((lane % NSA_HEAD_DIM) == blk).astype(F32)
    for g in range(NSA_KV_GROUPS):
        ksn_ref[0, g] = jnp.where((lane // NSA_HEAD_DIM) == g, ks_n, onehot).astype(BF16)
    kwn_ref[0] = _norm_rope_pair(kw_ref[...], kn_ref[...], cos, sin).astype(BF16)
    vs_t = vs_ref[...].T.astype(BF16)
    vw_t = vw_ref[...].T.astype(BF16)
    ones_rows = (lax.broadcasted_iota(jnp.int32, (SEL_V_ROWS - NSA_HEAD_DIM, TOK), 0) == 0).astype(BF16)
    for g in range(NSA_KV_GROUPS):
        vst_ref[0, g, 0, 0:NSA_HEAD_DIM, :] = vs_t[g * NSA_HEAD_DIM:(g + 1) * NSA_HEAD_DIM, :]
        vst_ref[0, g, 0, NSA_HEAD_DIM:SEL_V_ROWS, :] = ones_rows
        vwt_ref[0, g, 0] = vw_t[g * NSA_HEAD_DIM:(g + 1) * NSA_HEAD_DIM, :]


def _nsa_prep(proj, cos_t, sin_t, qn2, kn2, bsz, s):
    nt = s // TOK

    def col(c0, w):
        return pl.BlockSpec((TOK, w), lambda b, i: (b * nt + i, c0 // w))

    tab = pl.BlockSpec((TOK, LANES), lambda b, i: (i, 0))
    vec = pl.BlockSpec((1, LANES), lambda b, i: (0, 0))
    return pl.pallas_call(
        _nsa_prep_kernel,
        grid=(bsz, nt),
        in_specs=[col(COL_Q, NSA_Q_WIDTH), col(COL_KS, LANES), col(COL_VS, LANES), col(COL_KW, LANES),
                  col(COL_VW, LANES), tab, tab, vec, vec],
        out_specs=[
            pl.BlockSpec((1, NSA_KV_GROUPS, TOK, NSA_REP * LANES), lambda b, i: (b, 0, i, 0)),
            pl.BlockSpec((1, NSA_KV_GROUPS, TOK, LANES), lambda b, i: (b, 0, i, 0)),
            pl.BlockSpec((1, TOK, LANES), lambda b, i: (b, i, 0)),
            pl.BlockSpec((1, NSA_KV_GROUPS, 1, SEL_V_ROWS, TOK),
                         lambda b, i: (b, 0, i // (SEL_TILE // TOK), 0, i % (SEL_TILE // TOK))),
            pl.BlockSpec((1, NSA_KV_GROUPS, 1, NSA_HEAD_DIM, TOK), lambda b, i: (b, 0, i, 0, 0)),
        ],
        out_shape=[
            jax.ShapeDtypeStruct((bsz, NSA_KV_GROUPS, s, NSA_REP * LANES), BF16),
            jax.ShapeDtypeStruct((bsz, NSA_KV_GROUPS, s, LANES), BF16),
            jax.ShapeDtypeStruct((bsz, s, LANES), BF16),
            jax.ShapeDtypeStruct((bsz, NSA_KV_GROUPS, s // SEL_TILE, SEL_V_ROWS, SEL_TILE), BF16),
            jax.ShapeDtypeStruct((bsz, NSA_KV_GROUPS, nt, NSA_HEAD_DIM, TOK), BF16),
        ],
        compiler_params=_cparams(("parallel", "parallel")),
        name="nsa_prep",
    )(proj, proj, proj, proj, proj, cos_t, sin_t, qn2, kn2)


def _compress_kernel(kc_ref, vc_ref, posk_ref, posv_ref, w1k_ref, w2k_ref, w1v_ref, w2v_ref, kn_ref,
                     cos_ref, sin_ref, kco_ref, vct_ref, sh_ref, *, nbp):
    half_blk = NSA_CMP_BLOCK // 2

    def compress(x_ref, pos_ref, w1_ref, w2_ref):
        acc_a = jnp.zeros((nbp, LANES), F32)
        acc_b = jnp.zeros((nbp, LANES), F32)
        for l in range(half_blk):
            xl = x_ref[pl.ds(l, nbp, stride=NSA_CMP_STRIDE), :]
            acc_a = acc_a + _dot((xl + pos_ref[l:l + 1, :]).astype(BF16), w1_ref[l])
            acc_b = acc_b + _dot((xl + pos_ref[half_blk + l:half_blk + l + 1, :]).astype(BF16),
                                 w1_ref[half_blk + l])
        sh_ref[pl.ds(0, nbp), :] = acc_b
        sh_ref[pl.ds(nbp, 8), :] = jnp.zeros((8, LANES), F32)
        hid = acc_a + sh_ref[pl.ds(1, nbp), :]
        return _dot(_silu(hid).astype(BF16), w2_ref[...])

    kc = compress(kc_ref, posk_ref, w1k_ref, w2k_ref)
    kco_ref[0] = _norm_rope_pair(kc, kn_ref[...], cos_ref[...], sin_ref[...]).astype(BF16)
    vc = compress(vc_ref, posv_ref, w1v_ref, w2v_ref)
    vct_ref[0] = vc.T.astype(BF16)


def _compress(proj, posk, posv, w1k, w2k, w1v, w2v, kn2, cos_c, sin_c, bsz, s):
    nbp = s // NSA_CMP_STRIDE

    def full(shape):
        return pl.BlockSpec(shape, lambda b: (0,) * len(shape))

    return pl.pallas_call(
        functools.partial(_compress_kernel, nbp=nbp),
        grid=(bsz,),
        in_specs=[
            pl.BlockSpec((s, LANES), lambda b: (b, COL_KC // LANES)),
            pl.BlockSpec((s, LANES), lambda b: (b, COL_VC // LANES)),
            full((NSA_CMP_BLOCK, LANES)), full((NSA_CMP_BLOCK, LANES)),
            full((NSA_CMP_BLOCK, LANES, LANES)), full((LANES, LANES)),
            full((NSA_CMP_BLOCK, LANES, LANES)), full((LANES, LANES)),
            full((1, LANES)), full((nbp, LANES)), full((nbp, LANES)),
        ],
        out_specs=[
            pl.BlockSpec((1, nbp, LANES), lambda b: (b, 0, 0)),
            pl.BlockSpec((1, LANES, nbp), lambda b: (b, 0, 0)),
        ],
        out_shape=[
            jax.ShapeDtypeStruct((bsz, nbp, LANES), BF16),
            jax.ShapeDtypeStruct((bsz, LANES, nbp), BF16),
        ],
        scratch_shapes=[pltpu.VMEM((nbp + 8, LANES), F32)],
        compiler_params=_cparams(("parallel",)),
        name="nsa_compress",
    )(proj, proj, posk, posv, w1k, w2k, w1v, w2v, kn2, cos_c, sin_c)


def _q_rows(qp):
    return jnp.concatenate([qp[:, r * LANES:(r + 1) * LANES] for r in range(NSA_REP)], axis=0)


def _heads_to_token_major(acc_g):
    outs = []
    for p in range(NSA_REP // 2):
        blk = jnp.concatenate([acc_g[:, (2 * p) * TOK:(2 * p + 1) * TOK],
                               acc_g[:, (2 * p + 1) * TOK:(2 * p + 2) * TOK]], axis=0)
        outs.append(blk.T)
    return jnp.concatenate(outs, axis=1)


def _cmp_topk_kernel(qp_ref, kc_ref, vct_ref, oc_ref, mask_ref, ps_ref, *, nbp, nsel):
    qi = pl.program_id(1)
    ncol = NSA_REP * TOK
    per = NSA_SEL_BLOCK // NSA_CMP_STRIDE

    def attend(nrows):
        nblk = nrows // per
        j_io = lax.broadcasted_iota(jnp.int32, (nblk, TOK), 0)
        t_sel = qi * TOK + lax.broadcasted_iota(jnp.int32, (nblk, TOK), 1)
        cur = t_sel // NSA_SEL_BLOCK
        forced = (j_io == 0) | (j_io == cur) | (j_io == cur - 1)
        valid = j_io * NSA_SEL_BLOCK <= t_sel
        rel = (lax.broadcasted_iota(jnp.int32, (nrows, ncol), 0) * NSA_CMP_STRIDE + (NSA_CMP_BLOCK - 1)
               - lax.broadcasted_iota(jnp.int32, (nrows, ncol), 1) % TOK)
        allowed = rel <= qi * TOK
        kc = kc_ref[0, 0:nrows, :]
        vct = vct_ref[0, :, 0:nrows]
        ps_ref[pl.ds(0, 8), :] = jnp.zeros((8, TOK), F32)
        outs = []
        imps = []
        for g in range(NSA_KV_GROUPS):
            s_t = jnp.where(allowed, _dot_nt(kc, _q_rows(qp_ref[0, g])), NEG_INF)
            m = jnp.max(s_t, axis=0, keepdims=True)
            p = jnp.exp2(s_t - jnp.maximum(m, MASK_FLOOR))
            l = jnp.sum(p, axis=0, keepdims=True)
            inv = jnp.where(l > 0.0, 1.0 / l, 0.0)
            o_t = _dot(vct, p.astype(BF16)) * inv
            outs.append(_heads_to_token_major(o_t[g * NSA_HEAD_DIM:(g + 1) * NSA_HEAD_DIM, :]))
            pn = p * inv
            psum = pn[:, 0:TOK]
            for r in range(1, NSA_REP):
                psum = psum + pn[:, r * TOK:(r + 1) * TOK]
            ps_ref[pl.ds(8, nrows), :] = psum
            imp = ps_ref[pl.ds(7, nblk, stride=per), :] + ps_ref[pl.ds(8 + per - 1, nblk, stride=per), :]
            for c in range(per - 1):
                imp = imp + 2.0 * ps_ref[pl.ds(8 + c, nblk, stride=per), :]
            imps.append(jnp.where(forced, -jnp.inf, jnp.where(valid, imp, NEG_INF)))
        oc_ref[0] = jnp.concatenate(outs, axis=1)

        def pick(_, carry):
            nxt = []
            for imp_c in carry:
                mx = jnp.max(imp_c, axis=0, keepdims=True)
                idx = jnp.min(jnp.where(imp_c == mx, j_io, nblk), axis=0, keepdims=True)
                nxt.append(jnp.where(j_io == idx, -jnp.inf, imp_c))
            return tuple(nxt)

        picked = lax.fori_loop(0, min(NSA_TOPK, nsel) - N_FORCED, pick, tuple(imps))
        for g in range(NSA_KV_GROUPS):
            mask_ref[0, g, 0, 0:nblk, :] = jnp.where(valid & (picked[g] == -jnp.inf), 0.0, NEG_INF)
            if nblk < nsel:
                mask_ref[0, g, 0, nblk:nsel, :] = jnp.full((nsel - nblk, TOK), NEG_INF, F32)

    n_vis = (qi * TOK + TOK - NSA_CMP_BLOCK) // NSA_CMP_STRIDE + 1
    quarter = nbp // 4
    for k in range(1, 5):
        @pl.when((n_vis > (k - 1) * quarter) & (n_vis <= k * quarter))
        def _(k=k):
            attend(k * quarter)


def _cmp_topk(qp, kc, vct, bsz, s):
    nt = s // TOK
    nbp = s // NSA_CMP_STRIDE
    nsel = s // NSA_SEL_BLOCK
    return pl.pallas_call(
        functools.partial(_cmp_topk_kernel, nbp=nbp, nsel=nsel),
        grid=(bsz, nt),
        in_specs=[
            pl.BlockSpec((1, NSA_KV_GROUPS, TOK, NSA_REP * LANES), lambda b, i: (b, 0, i, 0)),
            pl.BlockSpec((1, nbp, LANES), lambda b, i: (b, 0, 0)),
            pl.BlockSpec((1, LANES, nbp), lambda b, i: (b, 0, 0)),
        ],
        out_specs=[
            pl.BlockSpec((1, TOK, NSA_Q_WIDTH), lambda b, i: (b, i, 0)),
            pl.BlockSpec((1, NSA_KV_GROUPS, 1, nsel, TOK), lambda b, i: (b, 0, i, 0, 0)),
        ],
        out_shape=[
            jax.ShapeDtypeStruct((bsz, s, NSA_Q_WIDTH), F32),
            jax.ShapeDtypeStruct((bsz, NSA_KV_GROUPS, nt, nsel, TOK), F32),
        ],
        scratch_shapes=[pltpu.VMEM((nbp + 8, TOK), F32)],
        compiler_params=_cparams(("parallel", "parallel")),
        name="nsa_cmp_topk",
    )(qp, kc, vct)


def _window_kernel(qp_ref, k_ref, vt_ref, o_ref):
    qi = pl.program_id(1)
    ncol = NSA_REP * TOK
    wk = WIN_TILES * TOK

    def run(first_tile, interior):
        kwin = k_ref[0, pl.ds(pl.multiple_of(first_tile * TOK, TOK), wk), :]
        outs = []
        for g in range(NSA_KV_GROUPS):
            s_t = _dot_nt(kwin, _q_rows(qp_ref[0, g]))
            if interior:
                row = lax.broadcasted_iota(jnp.int32, (TOK, ncol), 0)
                col = lax.broadcasted_iota(jnp.int32, (TOK, ncol), 1) % TOK
                oldest = jnp.where(row > col, s_t[:TOK], NEG_INF)
                newest = jnp.where(row <= col, s_t[wk - TOK:], NEG_INF)
                s_t = jnp.concatenate([oldest, s_t[TOK:wk - TOK], newest], axis=0)
            else:
                key = first_tile * TOK + lax.broadcasted_iota(jnp.int32, (wk, ncol), 0)
                t_io = qi * TOK + lax.broadcasted_iota(jnp.int32, (wk, ncol), 1) % TOK
                s_t = jnp.where((key <= t_io) & (t_io - key < NSA_WINDOW), s_t, NEG_INF)
            m = jnp.max(s_t, axis=0, keepdims=True)
            p = jnp.exp2(s_t - m)
            l = jnp.sum(p, axis=0, keepdims=True)
            vt = jnp.concatenate([vt_ref[0, g, first_tile + k] for k in range(WIN_TILES)], axis=1)
            outs.append(_heads_to_token_major(_dot(vt, p.astype(BF16)) / l))
        o_ref[0] = jnp.concatenate(outs, axis=1)

    @pl.when(qi >= WIN_TILES - 1)
    def _():
        run(qi - (WIN_TILES - 1), True)

    @pl.when(qi < WIN_TILES - 1)
    def _():
        run(0, False)


def _window(qp, k, vt, bsz, s):
    nt = s // TOK
    return pl.pallas_call(
        _window_kernel,
        grid=(bsz, nt),
        in_specs=[
            pl.BlockSpec((1, NSA_KV_GROUPS, TOK, NSA_REP * LANES), lambda b, i: (b, 0, i, 0)),
            pl.BlockSpec((1, s, LANES), lambda b, i: (b, 0, 0)),
            pl.BlockSpec((1, NSA_KV_GROUPS, nt, NSA_HEAD_DIM, TOK), lambda b, i: (b, 0, 0, 0, 0)),
        ],
        out_specs=pl.BlockSpec((1, TOK, NSA_Q_WIDTH), lambda b, i: (b, i, 0)),
        out_shape=jax.ShapeDtypeStruct((bsz, s, NSA_Q_WIDTH), F32),
        compiler_params=_cparams(("parallel", "parallel")),
        name="nsa_window",
    )(qp, k, vt)


def _selected_kernel(list_ref, cnt_ref, qp_ref, k_ref, vt_ref, bias_ref, o_ref, m_ref, acc_ref,
                     s_ref, p_ref, a_ref, qa_ref, *, nt, nsel):
    b = pl.program_id(0)
    qi = pl.program_id(1)
    ncol = NSA_REP * TOK
    blk_per_tile = SEL_TILE // NSA_SEL_BLOCK
    n_kt = nt // (SEL_TILE // TOK)
    m_ref[...] = jnp.full(m_ref.shape, NEG_INF, F32)
    acc_ref[...] = jnp.zeros(acc_ref.shape, F32)
    p_ref[...] = jnp.zeros(p_ref.shape, BF16)
    a_ref[...] = jnp.ones(a_ref.shape, F32)
    base = (b * nt + qi) * n_kt
    count = cnt_ref[b * nt + qi]
    n_diag = (qi * TOK) // SEL_TILE

    lane = lax.broadcasted_iota(jnp.int32, (TOK, LANES), 1)
    n_var = qa_ref.shape[0]
    for g in range(NSA_KV_GROUPS):
        bias_g = bias_ref[0, g, 0]
        if nsel % LANES:
            bias_g = jnp.concatenate([bias_g, jnp.zeros((LANES - nsel % LANES, TOK), F32)], axis=0)
        own = (lane // NSA_HEAD_DIM) == g
        for v in range(n_var):
            src = bias_g[(v // 2) * LANES:(v // 2 + 1) * LANES, :].T
            if v % 2 != 1 - g:
                src = pltpu.roll(src, NSA_HEAD_DIM, 1)
            src = src.astype(BF16)
            for r in range(NSA_REP):
                qa_ref[v, g, r * TOK:(r + 1) * TOK, :] = jnp.where(own, qp_ref[0, g, :, r * LANES:(r + 1) * LANES], src)

    def tile_at(i):
        return jnp.where(i < count, list_ref[base + jnp.minimum(i, n_kt - 1)], n_diag)

    def scores(j):
        v = (j * blk_per_tile) // NSA_HEAD_DIM
        for g in range(NSA_KV_GROUPS):
            kb = k_ref[0, g, pl.ds(pl.multiple_of(j * SEL_TILE, SEL_TILE), SEL_TILE), :]
            s_ref[g] = _dot_nt(kb, qa_ref[v, g])

    def values(j):
        for g in range(NSA_KV_GROUPS):
            acc_ref[g] = acc_ref[g] * a_ref[g] + _dot(vt_ref[0, g, j], p_ref[g])

    def softmax(j, causal):
        for g in range(NSA_KV_GROUPS):
            s_t = s_ref[g]
            if causal:
                t_io = qi * TOK + lax.broadcasted_iota(jnp.int32, (SEL_TILE, ncol), 1) % TOK
                key_pos = j * SEL_TILE + lax.broadcasted_iota(jnp.int32, (SEL_TILE, ncol), 0)
                s_t = jnp.where(key_pos <= t_io, s_t, NEG_INF)
            m_old = m_ref[g]
            m_new = jnp.maximum(m_old, jnp.max(s_t, axis=0, keepdims=True))
            alpha = jnp.exp2(m_old - m_new)
            p = jnp.exp2(s_t - jnp.maximum(m_new, MASK_FLOOR))
            m_ref[g] = m_new
            p_ref[g] = p.astype(BF16)
            a_ref[g] = alpha

    scores(tile_at(0))

    def body(i, c):
        values(tile_at(jnp.maximum(i - 1, 0)))
        softmax(tile_at(i), False)
        scores(tile_at(i + 1))
        return c

    lax.fori_loop(0, count, body, 0)
    values(tile_at(jnp.maximum(count - 1, 0)))
    softmax(n_diag, True)
    values(n_diag)
    outs = []
    for g in range(NSA_KV_GROUPS):
        acc = acc_ref[g]
        outs.append(_heads_to_token_major(acc[:NSA_HEAD_DIM] / acc[NSA_HEAD_DIM:NSA_HEAD_DIM + 1]))
    o_ref[0] = jnp.concatenate(outs, axis=1)


def _selected(qp, k, vt, bias, tile_list, tile_count, bsz, s):
    nt = s // TOK
    nsel = s // NSA_SEL_BLOCK
    ncol = NSA_REP * TOK
    grid_spec = pltpu.PrefetchScalarGridSpec(
        num_scalar_prefetch=2,
        grid=(bsz, nt),
        in_specs=[
            pl.BlockSpec((1, NSA_KV_GROUPS, TOK, NSA_REP * LANES), lambda b, i, tl, tc: (b, 0, i, 0)),
            pl.BlockSpec((1, NSA_KV_GROUPS, s, LANES), lambda b, i, tl, tc: (b, 0, 0, 0)),
            pl.BlockSpec((1, NSA_KV_GROUPS, s // SEL_TILE, SEL_V_ROWS, SEL_TILE),
                         lambda b, i, tl, tc: (b, 0, 0, 0, 0)),
            pl.BlockSpec((1, NSA_KV_GROUPS, 1, nsel, TOK), lambda b, i, tl, tc: (b, 0, i, 0, 0)),
        ],
        out_specs=pl.BlockSpec((1, TOK, NSA_Q_WIDTH), lambda b, i, tl, tc: (b, i, 0)),
        scratch_shapes=[pltpu.VMEM((NSA_KV_GROUPS, 1, ncol), F32),
                        pltpu.VMEM((NSA_KV_GROUPS, SEL_V_ROWS, ncol), F32),
                        pltpu.VMEM((NSA_KV_GROUPS, SEL_TILE, ncol), F32),
                        pltpu.VMEM((NSA_KV_GROUPS, SEL_TILE, ncol), BF16),
                        pltpu.VMEM((NSA_KV_GROUPS, 1, ncol), F32),
                        pltpu.VMEM((max(nsel // NSA_HEAD_DIM, 1), NSA_KV_GROUPS, ncol, LANES), BF16)],
    )
    return pl.pallas_call(
        functools.partial(_selected_kernel, nt=nt, nsel=nsel),
        grid_spec=grid_spec,
        out_shape=jax.ShapeDtypeStruct((bsz, s, NSA_Q_WIDTH), F32),
        compiler_params=_cparams(("parallel", "parallel")),
        name="nsa_selected",
    )(tile_list, tile_count, qp, k, vt, bias)


def _active_tiles(bias, bsz, s):
    nt = s // TOK
    n_kt = s // SEL_TILE
    rows = SEL_TILE // NSA_SEL_BLOCK * TOK
    act = bias.reshape(bsz, NSA_KV_GROUPS, nt, n_kt, rows).max(axis=(1, 4)) > -1.0
    n_diag = (jnp.arange(nt) * TOK) // SEL_TILE
    act = act & (jnp.arange(n_kt)[None, None, :] < n_diag[None, :, None])
    order = jnp.argsort(jnp.where(act, 0, 1), axis=-1, stable=True).astype(jnp.int32)
    return order.reshape(-1), jnp.sum(act, axis=-1, dtype=jnp.int32).reshape(-1)


def _ssd_kernel(z_ref, xbc_ref, dt_ref, cw_ref, cb_ref, dtb_ref, alog_ref, dskip_ref, nw_ref, e8_ref,
                tri_ref, y_ref, ext_ref, st_ref):
    @pl.when(pl.program_id(1) == 0)
    def _():
        ext_ref[pl.ds(0, 8), :] = jnp.zeros((8, SSM_CONV_DIM), F32)
        st_ref[...] = jnp.zeros_like(st_ref)

    xb = xbc_ref[...]
    ext_ref[pl.ds(8, TOK), :] = xb
    conv = cb_ref[...] + xb * cw_ref[SSM_CONV - 1:SSM_CONV, :]
    for k in range(SSM_CONV - 1):
        conv = conv + ext_ref[pl.ds(8 - (SSM_CONV - 1) + k, TOK), :] * cw_ref[k:k + 1, :]
    ext_ref[pl.ds(0, 8), :] = xb[TOK - 8:TOK, :]
    u = _silu(conv)
    xs = u[:, :SSM_INNER]
    gn = SSM_GROUPS * SSM_STATE
    bm = u[:, SSM_INNER:SSM_INNER + gn]
    cm = u[:, SSM_INNER + gn:]
    dt = jax.nn.softplus(dt_ref[...] + dtb_ref[...])
    da = dt * (-jnp.exp(alog_ref[...]))
    tri = tri_ref[...]
    e8 = e8_ref[...]
    cum = _dot_sel_l(tri, da)
    cum_t = cum.T
    cum_e = _dot_sel_r(cum, e8)
    dt_e = _dot_sel_r(dt, e8)
    xdt = xs * dt_e
    lane = lax.broadcasted_iota(jnp.int32, (TOK, LANES), 1)
    t_io = lax.broadcasted_iota(jnp.int32, (TOK, TOK), 0)
    s_io = lax.broadcasted_iota(jnp.int32, (TOK, TOK), 1)
    causal = s_io <= t_io
    hg = SSM_HEADS // SSM_GROUPS
    y_parts = []
    cbs = []
    for g in range(SSM_GROUPS):
        in_g = (lane // SSM_STATE) == g
        cbs.append(_dot_nt(jnp.where(in_g, cm, 0.0).astype(BF16), bm.astype(BF16)))
    for c in range(SSM_HEADS // 2):
        acc = jnp.zeros((TOK, LANES), F32)
        xpair = xdt[:, c * LANES:(c + 1) * LANES]
        for hh in range(2):
            h = 2 * c + hh
            seg = cum[:, h:h + 1] - cum_t[h:h + 1, :]
            decay = jnp.where(causal, jnp.exp(jnp.where(causal, seg, 0.0)), 0.0)
            mat = (cbs[h // hg] * decay).astype(BF16)
            xh = jnp.where((lane // SSM_HEAD_DIM) == hh, xpair, 0.0).astype(BF16)
            acc = acc + _dot(mat, xh)
        y_parts.append(acc)
    y_diag = jnp.concatenate(y_parts, axis=1)
    prev = st_ref[...]
    y_off = _dot(cm.astype(BF16), prev.astype(BF16)) * jnp.exp(cum_e)
    cum_last = cum_e[TOK - 1:TOK, :]
    xdec = (xdt * jnp.exp(cum_last - cum_e)).astype(BF16)
    new = _dot(bm.T.astype(BF16), xdec)
    row_g = lax.broadcasted_iota(jnp.int32, (LANES, SSM_INNER), 0) // SSM_STATE
    col_g = lax.broadcasted_iota(jnp.int32, (LANES, SSM_INNER), 1) // (hg * SSM_HEAD_DIM)
    st_ref[...] = prev * jnp.exp(cum_last) + jnp.where(row_g == col_g, new, 0.0)
    y = (y_diag + y_off + xs * dskip_ref[...]) * _silu(z_ref[...])
    gw = SSM_INNER // SSM_GROUPS
    outs = []
    for g in range(SSM_GROUPS):
        yg = y[:, g * gw:(g + 1) * gw]
        outs.append(_rms(yg, nw_ref[:, g * gw:(g + 1) * gw]))
    y_ref[...] = jnp.concatenate(outs, axis=1)


def _ssd(proj, cw, cb, dtb, alog, dskip, nw, e8, tri, bsz, s):
    nt = s // TOK

    def col(c0, w):
        return pl.BlockSpec((TOK, w), lambda b, i: (b * nt + i, c0 // w))

    def full(shape):
        return pl.BlockSpec(shape, lambda b, i: (0,) * len(shape))

    return pl.pallas_call(
        _ssd_kernel,
        grid=(bsz, nt),
        in_specs=[col(COL_Z, SSM_INNER), col(COL_XBC, SSM_CONV_DIM), col(COL_DT, LANES),
                  full((SSM_CONV, SSM_CONV_DIM)), full((1, SSM_CONV_DIM)), full((1, LANES)), full((1, LANES)),
                  full((1, SSM_INNER)), full((1, SSM_INNER)), full((LANES, SSM_INNER)), full((TOK, TOK))],
        out_specs=pl.BlockSpec((TOK, SSM_INNER), lambda b, i: (b * nt + i, 0)),
        out_shape=jax.ShapeDtypeStruct((bsz * s, SSM_INNER), F32),
        scratch_shapes=[pltpu.VMEM((TOK + 8, SSM_CONV_DIM), F32), pltpu.VMEM((LANES, SSM_INNER), F32)],
        compiler_params=_cparams(("parallel", "arbitrary")),
        name="ssd",
    )(proj, proj, proj, cw, cb, dtb, alog, dskip, nw, e8, tri)


def _hgrn2_kernel(q_ref, f_ref, i_ref, g_ref, lb_ref, nw_ref, tri_ref, y_ref, st_ref, kp_ref, bp_ref, vp_ref):
    @pl.when(pl.program_id(1) == 0)
    def _():
        st_ref[...] = jnp.zeros_like(st_ref)
        kp_ref[...] = jnp.zeros_like(kp_ref)
        bp_ref[...] = jnp.zeros_like(bp_ref)
        vp_ref[...] = jnp.zeros_like(vp_ref)

    nsub = TOK // HG_SUB
    row = lax.broadcasted_iota(jnp.int32, (TOK, HG_KEY_DIM), 0)
    sub_id = row // HG_SUB
    in_sub = row % HG_SUB
    tri = tri_ref[...]
    outs = []
    for h in range(HG_HEADS):
        sl = slice(h * HG_KEY_DIM, (h + 1) * HG_KEY_DIM)
        lb = lb_ref[:, sl]
        fr = f_ref[:, sl]
        q = _silu(q_ref[:, sl]) * (HG_KEY_DIM ** -0.5)
        log_f = jnp.log(lb + (1.0 - lb) * jax.nn.sigmoid(fr))
        k = (1.0 - lb) * jax.nn.sigmoid(-fr)
        v = i_ref[:, sl]
        bcum = _dot_sel_l(tri, log_f)
        st = st_ref[h]
        o = _dot_nt((q * jnp.exp(bcum)).astype(BF16), st.astype(BF16))
        r_rows = [bcum[i * HG_SUB - 1:i * HG_SUB, :] for i in range(1, nsub)]
        r_loc = jnp.zeros_like(bcum)
        for i in range(1, nsub):
            r_loc = jnp.where(sub_id == i, r_rows[i - 1], r_loc)
        q_loc = q * jnp.exp(bcum - r_loc)
        q_aug = jnp.concatenate(
            [jnp.where(sub_id == i, q_loc, 0.0).astype(BF16) for i in range(1, nsub)], axis=1)
        k_aug = jnp.concatenate(
            [jnp.where(sub_id < i, k * jnp.exp(jnp.where(sub_id < i, r_rows[i - 1] - bcum, 0.0)), 0.0).astype(BF16)
             for i in range(1, nsub)], axis=1)
        att = _dot_nt(q_aug, k_aug)
        o = o + _dot(att.astype(BF16), v.astype(BF16))
        kp_ref[pl.ds(HG_SUB, TOK), :] = k
        bp_ref[pl.ds(HG_SUB, TOK), :] = bcum
        vp_ref[pl.ds(HG_SUB, TOK), :] = v
        o = o + jnp.sum(q * k, axis=1, keepdims=True) * v
        for d in range(1, HG_SUB):
            ks = kp_ref[pl.ds(HG_SUB - d, TOK), :]
            bs = bp_ref[pl.ds(HG_SUB - d, TOK), :]
            vs = vp_ref[pl.ds(HG_SUB - d, TOK), :]
            ok = in_sub >= d
            w = jnp.sum(q * ks * jnp.exp(jnp.where(ok, bcum - bs, 0.0)), axis=1, keepdims=True)
            o = o + jnp.where(ok, w * vs, 0.0)
        b_last = bcum[TOK - 1:TOK, :]
        kdec = (k * jnp.exp(b_last - bcum)).astype(BF16)
        st_ref[h] = st * jnp.exp(b_last) + _dot(v.T.astype(BF16), kdec)
        outs.append(_rms(o, nw_ref[...]) * _silu(g_ref[:, sl]))
    y_ref[...] = jnp.concatenate(outs, axis=1)


def _hgrn2(proj, lb, nw, tri, bsz, s):
    nt = s // TOK

    def col(c0):
        return pl.BlockSpec((TOK, HG_WIDTH), lambda b, i: (b * nt + i, c0 // HG_WIDTH))

    def full(shape):
        return pl.BlockSpec(shape, lambda b, i: (0,) * len(shape))

    pad = pltpu.VMEM((TOK + HG_SUB, HG_KEY_DIM), F32)
    return pl.pallas_call(
        _hgrn2_kernel,
        grid=(bsz, nt),
        in_specs=[col(COL_HQ), col(COL_HF), col(COL_HI), col(COL_HG), full((1, HG_WIDTH)),
                  full((1, HG_VAL_DIM)), full((TOK, TOK))],
        out_specs=pl.BlockSpec((TOK, HG_WIDTH), lambda b, i: (b * nt + i, 0)),
        out_shape=jax.ShapeDtypeStruct((bsz * s, HG_WIDTH), F32),
        scratch_shapes=[pltpu.VMEM((HG_HEADS, HG_VAL_DIM, HG_KEY_DIM), F32), pad, pad, pad],
        compiler_params=_cparams(("parallel", "arbitrary")),
        name="hgrn2",
    )(proj, proj, proj, proj, lb, nw, tri)


def _merge_kernel(ga_ref, gb_ref, gc_ref, ng_ref, oc_ref, os_ref, ow_ref, yb_ref, yc_ref, x_ref,
                  eg_ref, wa_ref, wb_ref, wc_ref, wo_ref, o_ref):
    sg = jax.nn.sigmoid(ng_ref[...])
    nsa = (_dot_sel_r(sg, eg_ref[0]) * oc_ref[...] + _dot_sel_r(sg, eg_ref[1]) * os_ref[...]
           + _dot_sel_r(sg, eg_ref[2]) * ow_ref[...])
    ya = _dot(nsa.astype(BF16), wa_ref[...])
    yb = _dot(yb_ref[...].astype(BF16), wb_ref[...])
    yc = _dot(yc_ref[...].astype(BF16), wc_ref[...])
    merged = (jax.nn.sigmoid(ga_ref[...]) * ya + jax.nn.sigmoid(gb_ref[...]) * yb
              + jax.nn.sigmoid(gc_ref[...]) * yc)
    o_ref[...] = x_ref[...] + _dot(merged.astype(BF16), wo_ref[...])


def _merge(proj, oc, os_, ow, yb, yc, x2, eg, wa, wb, wc, wo):
    t = x2.shape[0]
    tm = min(512, t)

    def col(c0, w):
        return pl.BlockSpec((tm, w), lambda i: (i, c0 // w))

    def full(shape):
        return pl.BlockSpec(shape, lambda i: (0,) * len(shape))

    row512 = pl.BlockSpec((tm, 512), lambda i: (i, 0))
    return pl.pallas_call(
        _merge_kernel,
        grid=(t // tm,),
        in_specs=[col(COL_GA, D_MODEL), col(COL_GB, D_MODEL), col(COL_GC, D_MODEL), col(COL_NG, LANES),
                  row512, row512, row512, row512, row512,
                  pl.BlockSpec((tm, D_MODEL), lambda i: (i, 0)),
                  full((3, LANES, NSA_Q_WIDTH)), full((NSA_Q_WIDTH, D_MODEL)), full((SSM_INNER, D_MODEL)),
                  full((HG_WIDTH, D_MODEL)), full((D_MODEL, D_MODEL))],
        out_specs=pl.BlockSpec((tm, D_MODEL), lambda i: (i, 0)),
        out_shape=jax.ShapeDtypeStruct((t, D_MODEL), F32),
        compiler_params=_cparams(("parallel",)),
        name="merge",
    )(proj, proj, proj, proj, oc, os_, ow, yb, yc, x2, eg, wa, wb, wc, wo)


def _xa_mem_kernel(mem_ref, nm_ref, wk_ref, wv_ref, kn_ref, k_ref, v_ref):
    m = _rms(mem_ref[0], nm_ref[...]).astype(BF16)
    k = _dot(m, wk_ref[...])
    ks = [_rms(k[:, h * XA_HEAD_DIM:(h + 1) * XA_HEAD_DIM], kn_ref[...]) for h in range(XA_HEADS)]
    k_ref[0] = jnp.concatenate(ks, axis=1).astype(BF16)
    v_ref[0] = _dot(m, wv_ref[...]).astype(BF16)


def _xa_mem(mem, nm, wk, wv, kn):
    bsz, ml, _ = mem.shape

    def full(shape):
        return pl.BlockSpec(shape, lambda b: (0,) * len(shape))

    return pl.pallas_call(
        _xa_mem_kernel,
        grid=(bsz,),
        in_specs=[pl.BlockSpec((1, ml, D_MODEL), lambda b: (b, 0, 0)), full((1, D_MODEL)),
                  full((D_MODEL, XA_WIDTH)), full((D_MODEL, XA_WIDTH)), full((1, XA_HEAD_DIM))],
        out_specs=[pl.BlockSpec((1, ml, XA_WIDTH), lambda b: (b, 0, 0)),
                   pl.BlockSpec((1, ml, XA_WIDTH), lambda b: (b, 0, 0))],
        out_shape=[jax.ShapeDtypeStruct((bsz, ml, XA_WIDTH), BF16),
                   jax.ShapeDtypeStruct((bsz, ml, XA_WIDTH), BF16)],
        compiler_params=_cparams(("parallel",)),
        name="xa_mem",
    )(mem, nm, wk, wv, kn)


def _xa_kernel(x_ref, nx_ref, wq_ref, qn_ref, k_ref, v_ref, wo_ref, o_ref):
    x = x_ref[...]
    h = _rms(x, nx_ref[...]).astype(BF16)
    q = _dot(h, wq_ref[...])
    outs = []
    for hd in range(XA_HEADS):
        sl = slice(hd * XA_HEAD_DIM, (hd + 1) * XA_HEAD_DIM)
        qh = _rms(q[:, sl], qn_ref[...]).astype(BF16)
        sc = _dot_nt(qh, k_ref[0, :, sl]) * (XA_HEAD_DIM ** -0.5)
        sc = sc - jnp.max(sc, axis=-1, keepdims=True)
        p = jnp.exp(sc)
        p = p / jnp.sum(p, axis=-1, keepdims=True)
        outs.append(_dot(p.astype(BF16), v_ref[0, :, sl]))
    o = jnp.concatenate(outs, axis=1).astype(BF16)
    o_ref[...] = x + _dot(o, wo_ref[...])


def _xa(x2, nx, wq, qn, k, v, wo, s):
    t = x2.shape[0]
    tm = min(512, s)
    ml = k.shape[1]
    per_b = s // tm

    def full(shape):
        return pl.BlockSpec(shape, lambda i: (0,) * len(shape))

    return pl.pallas_call(
        _xa_kernel,
        grid=(t // tm,),
        in_specs=[pl.BlockSpec((tm, D_MODEL), lambda i: (i, 0)), full((1, D_MODEL)), full((D_MODEL, XA_WIDTH)),
                  full((1, XA_HEAD_DIM)),
                  pl.BlockSpec((1, ml, XA_WIDTH), lambda i: (i // per_b, 0, 0)),
                  pl.BlockSpec((1, ml, XA_WIDTH), lambda i: (i // per_b, 0, 0)),
                  full((XA_WIDTH, D_MODEL))],
        out_specs=pl.BlockSpec((tm, D_MODEL), lambda i: (i, 0)),
        out_shape=jax.ShapeDtypeStruct((t, D_MODEL), F32),
        compiler_params=_cparams(("parallel",)),
        name="cross_attn",
    )(x2, nx, wq, qn, k, v, wo)


def _ffn_kernel(x_ref, nw_ref, wg_ref, wu_ref, wd_ref, o_ref, h_ref, acc_ref):
    j = pl.program_id(1)

    @pl.when(j == 0)
    def _():
        h_ref[...] = _rms(x_ref[...], nw_ref[...]).astype(BF16)
        acc_ref[...] = jnp.zeros_like(acc_ref)

    h = h_ref[...]
    a = _silu(_dot(h, wg_ref[...])) * _dot(h, wu_ref[...])
    acc_ref[...] += _dot(a.astype(BF16), wd_ref[...])

    @pl.when(j == pl.num_programs(1) - 1)
    def _():
        o_ref[...] = x_ref[...] + acc_ref[...]


def _ffn(x2, nw, wg, wu, wd):
    t = x2.shape[0]
    tm = min(1024, t)
    th = FFN_HIDDEN // 2
    return pl.pallas_call(
        _ffn_kernel,
        grid=(t // tm, FFN_HIDDEN // th),
        in_specs=[pl.BlockSpec((tm, D_MODEL), lambda i, j: (i, 0)),
                  pl.BlockSpec((1, D_MODEL), lambda i, j: (0, 0)),
                  pl.BlockSpec((D_MODEL, th), lambda i, j: (0, j)),
                  pl.BlockSpec((D_MODEL, th), lambda i, j: (0, j)),
                  pl.BlockSpec((th, D_MODEL), lambda i, j: (j, 0))],
        out_specs=pl.BlockSpec((tm, D_MODEL), lambda i, j: (i, 0)),
        out_shape=jax.ShapeDtypeStruct((t, D_MODEL), F32),
        scratch_shapes=[pltpu.VMEM((tm, D_MODEL), BF16), pltpu.VMEM((tm, D_MODEL), F32)],
        compiler_params=_cparams(("parallel", "arbitrary")),
        name="ffn",
    )(x2, nw, wg, wu, wd)


def _pack_w_in(w):
    parts = {}
    off = 0
    names = ("q", "kc", "vc", "ks", "vs", "kw", "vw", "ng", "z", "xbc", "dt", "hq", "hf", "hi", "hg",
             "ga", "gb", "gc")
    for name, size in zip(names, SPLIT_SIZES):
        parts[name] = w[:, off:off + size]
        off += size

    def padded(a, width):
        return jnp.pad(a, ((0, 0), (0, width - a.shape[1])))

    cols = [parts["ga"], parts["gb"], parts["gc"], parts["q"], parts["z"], parts["hq"], parts["hf"],
            parts["hi"], parts["hg"], parts["xbc"], parts["kc"], parts["vc"], parts["ks"], parts["vs"],
            parts["kw"], parts["vw"], padded(parts["ng"], LANES), padded(parts["dt"], LANES)]
    packed = jnp.concatenate(cols, axis=1)
    return padded(packed, PROJ_WIDTH).astype(BF16)


def _rope_tables(pos):
    half = NSA_HEAD_DIM // 2
    inv = 1.0 / (ROPE_THETA ** (jnp.arange(half, dtype=F32) / half))
    ang = pos.astype(F32)[:, None] * inv[None, :]
    reps = LANES // half
    return jnp.tile(jnp.cos(ang), (1, reps)), jnp.tile(jnp.sin(ang), (1, reps))


def _block_diag2(w):
    z = jnp.zeros_like(w)
    return jnp.concatenate([jnp.concatenate([w, z], axis=-1), jnp.concatenate([z, w], axis=-1)], axis=-2)


def kernel(x, mem, norm_mix, w_in, nsa_q_norm, nsa_k_norm, nsa_cmp_pos_k, nsa_cmp_pos_v, nsa_cmp_w1_k, nsa_cmp_w2_k, nsa_cmp_w1_v, nsa_cmp_w2_v, w_nsa_o, ssm_conv_w, ssm_conv_b, ssm_dt_bias, ssm_a_log, ssm_d, ssm_norm, w_ssm_o, hg_lb_logits, hg_norm, w_hg_o, w_out, norm_xa, norm_mem, xa_w_q, xa_w_k, xa_w_v, xa_q_norm, xa_k_norm, xa_w_o, norm_ffn, ffn_w_gate, ffn_w_up, ffn_w_down):
    bsz, s, d = x.shape
    depth = w_in.shape[0]
    assert d == D_MODEL and s % (8 * TOK) == 0 and s // TOK <= TOK
    t = bsz * s
    hd = NSA_HEAD_DIM

    cos_t, sin_t = _rope_tables(jnp.arange(s))
    nbp = s // NSA_CMP_STRIDE
    cos_c, sin_c = _rope_tables(jnp.arange(nbp) * NSA_CMP_STRIDE + NSA_CMP_BLOCK - 1)
    tri = jnp.tril(jnp.ones((TOK, TOK), F32)).astype(BF16)
    rows = np.arange(LANES)[:, None]
    cols = np.arange(NSA_Q_WIDTH)[None, :]
    e8 = jnp.asarray(rows == cols // SSM_HEAD_DIM, F32).astype(BF16)
    eg = jnp.stack([jnp.asarray(rows == 3 * (cols // hd) + c, F32) for c in range(3)]).astype(BF16)

    lb_sm = jax.nn.softmax(hg_lb_logits.astype(F32), axis=0)
    lb_all = jnp.cumsum(lb_sm, axis=0) - lb_sm[0:1]

    def pad_lanes(v):
        return jnp.pad(v, (0, LANES - v.shape[0]))[None, :]

    x2 = x.reshape(t, d)
    for l in range(depth):
        proj = _in_proj(x2, norm_mix[l][None, :], _pack_w_in(w_in[l]))

        qn2 = jnp.tile(nsa_q_norm[l], 2)[None, :]
        kn2 = jnp.tile(nsa_k_norm[l], 2)[None, :]
        qp, ksn, kwn, vst, vwt = _nsa_prep(proj, cos_t, sin_t, qn2, kn2, bsz, s)
        w1k = _block_diag2(nsa_cmp_w1_k[l].reshape(NSA_CMP_BLOCK, hd, hd)).astype(BF16)
        w1v = _block_diag2(nsa_cmp_w1_v[l].reshape(NSA_CMP_BLOCK, hd, hd)).astype(BF16)
        kc, vct = _compress(proj, jnp.tile(nsa_cmp_pos_k[l], (1, 2)), jnp.tile(nsa_cmp_pos_v[l], (1, 2)),
                            w1k, _block_diag2(nsa_cmp_w2_k[l]).astype(BF16),
                            w1v, _block_diag2(nsa_cmp_w2_v[l]).astype(BF16), kn2, cos_c, sin_c, bsz, s)
        o_c, sel_bias = _cmp_topk(qp, kc, vct, bsz, s)
        tile_list, tile_count = _active_tiles(sel_bias, bsz, s)
        o_s = _selected(qp, ksn, vst, sel_bias, tile_list, tile_count, bsz, s)
        o_w = _window(qp, kwn, vwt, bsz, s)

        y_b = _ssd(proj, ssm_conv_w[l], ssm_conv_b[l][None, :], pad_lanes(ssm_dt_bias[l]),
                   pad_lanes(ssm_a_log[l]), jnp.repeat(ssm_d[l], SSM_HEAD_DIM)[None, :],
                   ssm_norm[l][None, :], e8, tri, bsz, s)
        y_c = _hgrn2(proj, lb_all[l][None, :], hg_norm[l][None, :], tri, bsz, s)

        x2 = _merge(proj, o_c.reshape(t, -1), o_s.reshape(t, -1), o_w.reshape(t, -1), y_b, y_c, x2, eg,
                    w_nsa_o[l].astype(BF16), w_ssm_o[l].astype(BF16), w_hg_o[l].astype(BF16),
                    w_out[l].astype(BF16))

        k_m, v_m = _xa_mem(mem, norm_mem[l][None, :], xa_w_k[l].astype(BF16), xa_w_v[l].astype(BF16),
                           xa_k_norm[l][None, :])
        x2 = _xa(x2, norm_xa[l][None, :], xa_w_q[l].astype(BF16), xa_q_norm[l][None, :], k_m, v_m,
                 xa_w_o[l].astype(BF16), s)
        x2 = _ffn(x2, norm_ffn[l][None, :], ffn_w_gate[l].astype(BF16), ffn_w_up[l].astype(BF16),
                  ffn_w_down[l].astype(BF16))
    return x2.reshape(bsz, s, d)
```

```python
import functools
import math

import numpy as np
import jax
import jax.numpy as jnp
from jax import lax
from jax.experimental import pallas as pl
from jax.experimental.pallas import tpu as pltpu

F32 = jnp.float32
BF16 = jnp.bfloat16

D_MODEL = 1024
NORM_EPS = 1e-6
ROPE_THETA = 10000.0
NEG_INF = -1e30
FORCE_SCORE = 1e9

NSA_HEADS = 8
NSA_KV_GROUPS = 2
NSA_REP = NSA_HEADS // NSA_KV_GROUPS
NSA_HEAD_DIM = 64
NSA_CMP_BLOCK = 32
NSA_CMP_STRIDE = 16
NSA_SEL_BLOCK = 64
NSA_TOPK = 16
N_FORCED = 3
NSA_WINDOW = 512
NSA_Q_WIDTH = NSA_HEADS * NSA_HEAD_DIM
NSA_KV_WIDTH = NSA_KV_GROUPS * NSA_HEAD_DIM

SSM_INNER = 512
SSM_HEAD_DIM = 64
SSM_HEADS = SSM_INNER // SSM_HEAD_DIM
SSM_GROUPS = 2
SSM_STATE = 64
SSM_CONV = 4
SSM_CONV_DIM = SSM_INNER + 2 * SSM_GROUPS * SSM_STATE

HG_HEADS = 4
HG_KEY_DIM = 128
HG_VAL_DIM = 128
HG_WIDTH = HG_HEADS * HG_KEY_DIM
HG_SUB = 16

XA_HEADS = 4
XA_HEAD_DIM = 128
XA_WIDTH = XA_HEADS * XA_HEAD_DIM

FFN_HIDDEN = -(-(8 * D_MODEL) // (3 * 256)) * 256

SPLIT_SIZES = (
    NSA_Q_WIDTH, NSA_KV_WIDTH, NSA_KV_WIDTH, NSA_KV_WIDTH, NSA_KV_WIDTH, NSA_KV_WIDTH, NSA_KV_WIDTH,
    3 * NSA_HEADS, SSM_INNER, SSM_CONV_DIM, SSM_HEADS, HG_WIDTH, HG_WIDTH, HG_HEADS * HG_VAL_DIM,
    HG_HEADS * HG_VAL_DIM, D_MODEL, D_MODEL, D_MODEL,
)

LANES = 128
TOK = 128
SEL_Q = 2
SEL_TILE = 512
SEL_V_ROWS = NSA_HEAD_DIM + 16
WIN_TILES = NSA_WINDOW // TOK + 1
LOG2_E = math.log2(math.e)
MASK_FLOOR = -1e20
VMEM_LIMIT = 56 * 1024 * 1024

PROJ_WIDTH = 8192
COL_GA, COL_GB, COL_GC = 0, 1024, 2048
COL_Q = 3072
COL_Z = 3584
COL_HQ, COL_HF, COL_HI, COL_HG = 4096, 4608, 5120, 5632
COL_XBC = 6144
COL_KC, COL_VC, COL_KS, COL_VS, COL_KW, COL_VW = 6912, 7040, 7168, 7296, 7424, 7552
COL_NG = 7680
COL_DT = 7808


def _cparams(sem):
    return pltpu.CompilerParams(dimension_semantics=sem, vmem_limit_bytes=VMEM_LIMIT)


def _dot(a, b):
    return jnp.dot(a, b, preferred_element_type=F32)


def _dot_nt(a, b):
    return lax.dot_general(a, b, (((1,), (1,)), ((), ())), preferred_element_type=F32)


def _split3(a):
    hi = a.astype(BF16)
    r1 = a - hi.astype(F32)
    mid = r1.astype(BF16)
    lo = (r1 - mid.astype(F32)).astype(BF16)
    return hi, mid, lo


def _dot_sel_r(a, sel):
    hi, mid, lo = _split3(a)
    return _dot(hi, sel) + _dot(mid, sel) + _dot(lo, sel)


def _dot_sel_l(sel, a):
    hi, mid, lo = _split3(a)
    return _dot(sel, hi) + _dot(sel, mid) + _dot(sel, lo)


def _silu(x):
    return x * jax.nn.sigmoid(x)


def _rms(x, w):
    return x * lax.rsqrt(jnp.mean(x * x, axis=-1, keepdims=True) + NORM_EPS) * w


def _inproj_kernel(x_ref, nw_ref, w_ref, o_ref, h_ref):
    @pl.when(pl.program_id(1) == 0)
    def _():
        h_ref[...] = _rms(x_ref[...], nw_ref[...]).astype(BF16)

    o_ref[...] = _dot(h_ref[...], w_ref[...])


def _in_proj(x2, norm_w, w_packed):
    t = x2.shape[0]
    tm = min(2048, t)
    tn = 1024
    return pl.pallas_call(
        _inproj_kernel,
        grid=(t // tm, PROJ_WIDTH // tn),
        in_specs=[
            pl.BlockSpec((tm, D_MODEL), lambda i, j: (i, 0)),
            pl.BlockSpec((1, D_MODEL), lambda i, j: (0, 0)),
            pl.BlockSpec((D_MODEL, tn), lambda i, j: (0, j)),
        ],
        out_specs=pl.BlockSpec((tm, tn), lambda i, j: (i, j)),
        out_shape=jax.ShapeDtypeStruct((t, PROJ_WIDTH), F32),
        scratch_shapes=[pltpu.VMEM((tm, D_MODEL), BF16)],
        compiler_params=_cparams(("parallel", "arbitrary")),
        name="in_proj",
    )(x2, norm_w, w_packed)


def _norm_rope_pair(x, w, cos, sin):
    lane = lax.broadcasted_iota(jnp.int32, x.shape, 1)
    lo = lane < NSA_HEAD_DIM
    sq = x * x
    s_lo = jnp.sum(jnp.where(lo, sq, 0.0), axis=1, keepdims=True)
    s_hi = jnp.sum(jnp.where(lo, 0.0, sq), axis=1, keepdims=True)
    ms = jnp.where(lo, s_lo, s_hi) * (1.0 / NSA_HEAD_DIM)
    y = x * lax.rsqrt(ms + NORM_EPS) * w
    half = NSA_HEAD_DIM // 2
    first = (lane % NSA_HEAD_DIM) < half
    rot = jnp.where(first, -pltpu.roll(y, LANES - half, 1), pltpu.roll(y, half, 1))
    return y * cos + rot * sin


def _nsa_prep_kernel(q_ref, ks_ref, vs_ref, kw_ref, vw_ref, cos_ref, sin_ref, qn_ref, kn_ref,
                     qp_ref, ksn_ref, kwn_ref, vst_ref, vwt_ref):
    cos = cos_ref[...]
    sin = sin_ref[...]
    lane = lax.broadcasted_iota(jnp.int32, (TOK, LANES), 1)
    scale = NSA_HEAD_DIM ** -0.5 * LOG2_E
    for c in range(NSA_HEADS // 2):
        y = _norm_rope_pair(q_ref[:, c * LANES:(c + 1) * LANES], qn_ref[...], cos, sin) * scale
        y_sw = pltpu.roll(y, NSA_HEAD_DIM, 1)
        for hh in range(2):
            h = 2 * c + hh
            g, r = divmod(h, NSA_REP)
            src = y if hh == g else y_sw
            keep = (lane // NSA_HEAD_DIM) == g
            qp_ref[0, g, :, r * LANES:(r + 1) * LANES] = jnp.where(keep, src, 0.0).astype(BF16)
    ks_n = _norm_rope_pair(ks_ref[...], kn_ref[...], cos, sin)
    row = lax.broadcasted_iota(jnp.int32, (TOK, LANES), 0)
    blk = (pl.program_id(1) * (TOK // NSA_SEL_BLOCK) + row // NSA_SEL_BLOCK) % NSA_HEAD_DIM
    onehot = ((lane % NSA_HEAD_DIM) == blk).astype(F32)
    for g in range(NSA_KV_GROUPS):
        ksn_ref[0, g] = jnp.where((lane // NSA_HEAD_DIM) == g, ks_n, onehot).astype(BF16)
    kwn_ref[0] = _norm_rope_pair(kw_ref[...], kn_ref[...], cos, sin).astype(BF16)
    vs_t = vs_ref[...].T.astype(BF16)
    vw_t = vw_ref[...].T.astype(BF16)
    ones_rows = (lax.broadcasted_iota(jnp.int32, (SEL_V_ROWS - NSA_HEAD_DIM, TOK), 0) == 0).astype(BF16)
    for g in range(NSA_KV_GROUPS):
        vst_ref[0, g, 0, 0:NSA_HEAD_DIM, :] = vs_t[g * NSA_HEAD_DIM:(g + 1) * NSA_HEAD_DIM, :]
        vst_ref[0, g, 0, NSA_HEAD_DIM:SEL_V_ROWS, :] = ones_rows
        vwt_ref[0, g, 0] = vw_t[g * NSA_HEAD_DIM:(g + 1) * NSA_HEAD_DIM, :]


def _nsa_prep(proj, cos_t, sin_t, qn2, kn2, bsz, s):
    nt = s // TOK

    def col(c0, w):
        return pl.BlockSpec((TOK, w), lambda b, i: (b * nt + i, c0 // w))

    tab = pl.BlockSpec((TOK, LANES), lambda b, i: (i, 0))
    vec = pl.BlockSpec((1, LANES), lambda b, i: (0, 0))
    return pl.pallas_call(
        _nsa_prep_kernel,
        grid=(bsz, nt),
        in_specs=[col(COL_Q, NSA_Q_WIDTH), col(COL_KS, LANES), col(COL_VS, LANES), col(COL_KW, LANES),
                  col(COL_VW, LANES), tab, tab, vec, vec],
        out_specs=[
            pl.BlockSpec((1, NSA_KV_GROUPS, TOK, NSA_REP * LANES), lambda b, i: (b, 0, i, 0)),
            pl.BlockSpec((1, NSA_KV_GROUPS, TOK, LANES), lambda b, i: (b, 0, i, 0)),
            pl.BlockSpec((1, TOK, LANES), lambda b, i: (b, i, 0)),
            pl.BlockSpec((1, NSA_KV_GROUPS, 1, SEL_V_ROWS, TOK),
                         lambda b, i: (b, 0, i // (SEL_TILE // TOK), 0, i % (SEL_TILE // TOK))),
            pl.BlockSpec((1, NSA_KV_GROUPS, 1, NSA_HEAD_DIM, TOK), lambda b, i: (b, 0, i, 0, 0)),
        ],
        out_shape=[
            jax.ShapeDtypeStruct((bsz, NSA_KV_GROUPS, s, NSA_REP * LANES), BF16),
            jax.ShapeDtypeStruct((bsz, NSA_KV_GROUPS, s, LANES), BF16),
            jax.ShapeDtypeStruct((bsz, s, LANES), BF16),
            jax.ShapeDtypeStruct((bsz, NSA_KV_GROUPS, s // SEL_TILE, SEL_V_ROWS, SEL_TILE), BF16),
            jax.ShapeDtypeStruct((bsz, NSA_KV_GROUPS, nt, NSA_HEAD_DIM, TOK), BF16),
        ],
        compiler_params=_cparams(("parallel", "parallel")),
        name="nsa_prep",
    )(proj, proj, proj, proj, proj, cos_t, sin_t, qn2, kn2)


def _compress_kernel(kc_ref, vc_ref, posk_ref, posv_ref, w1k_ref, w2k_ref, w1v_ref, w2v_ref, kn_ref,
                     cos_ref, sin_ref, kco_ref, vct_ref, sh_ref, *, nbp):
    half_blk = NSA_CMP_BLOCK // 2

    def compress(x_ref, pos_ref, w1_ref, w2_ref):
        acc_a = jnp.zeros((nbp, LANES), F32)
        acc_b = jnp.zeros((nbp, LANES), F32)
        for l in range(half_blk):
            xl = x_ref[pl.ds(l, nbp, stride=NSA_CMP_STRIDE), :]
            acc_a = acc_a + _dot((xl + pos_ref[l:l + 1, :]).astype(BF16), w1_ref[l])
            acc_b = acc_b + _dot((xl + pos_ref[half_blk + l:half_blk + l + 1, :]).astype(BF16),
                                 w1_ref[half_blk + l])
        sh_ref[pl.ds(0, nbp), :] = acc_b
        sh_ref[pl.ds(nbp, 8), :] = jnp.zeros((8, LANES), F32)
        hid = acc_a + sh_ref[pl.ds(1, nbp), :]
        return _dot(_silu(hid).astype(BF16), w2_ref[...])

    kc = compress(kc_ref, posk_ref, w1k_ref, w2k_ref)
    kco_ref[0] = _norm_rope_pair(kc, kn_ref[...], cos_ref[...], sin_ref[...]).astype(BF16)
    vc = compress(vc_ref, posv_ref, w1v_ref, w2v_ref)
    vct_ref[0] = vc.T.astype(BF16)


def _compress(proj, posk, posv, w1k, w2k, w1v, w2v, kn2, cos_c, sin_c, bsz, s):
    nbp = s // NSA_CMP_STRIDE

    def full(shape):
        return pl.BlockSpec(shape, lambda b: (0,) * len(shape))

    return pl.pallas_call(
        functools.partial(_compress_kernel, nbp=nbp),
        grid=(bsz,),
        in_specs=[
            pl.BlockSpec((s, LANES), lambda b: (b, COL_KC // LANES)),
            pl.BlockSpec((s, LANES), lambda b: (b, COL_VC // LANES)),
            full((NSA_CMP_BLOCK, LANES)), full((NSA_CMP_BLOCK, LANES)),
            full((NSA_CMP_BLOCK, LANES, LANES)), full((LANES, LANES)),
            full((NSA_CMP_BLOCK, LANES, LANES)), full((LANES, LANES)),
            full((1, LANES)), full((nbp, LANES)), full((nbp, LANES)),
        ],
        out_specs=[
            pl.BlockSpec((1, nbp, LANES), lambda b: (b, 0, 0)),
            pl.BlockSpec((1, LANES, nbp), lambda b: (b, 0, 0)),
        ],
        out_shape=[
            jax.ShapeDtypeStruct((bsz, nbp, LANES), BF16),
            jax.ShapeDtypeStruct((bsz, LANES, nbp), BF16),
        ],
        scratch_shapes=[pltpu.VMEM((nbp + 8, LANES), F32)],
        compiler_params=_cparams(("parallel",)),
        name="nsa_compress",
    )(proj, proj, posk, posv, w1k, w2k, w1v, w2v, kn2, cos_c, sin_c)


def _q_rows(qp):
    return jnp.concatenate([qp[:, r * LANES:(r + 1) * LANES] for r in range(NSA_REP)], axis=0)


def _heads_to_token_major(acc_g):
    outs = []
    for p in range(NSA_REP // 2):
        blk = jnp.concatenate([acc_g[:, (2 * p) * TOK:(2 * p + 1) * TOK],
                               acc_g[:, (2 * p + 1) * TOK:(2 * p + 2) * TOK]], axis=0)
        outs.append(blk.T)
    return jnp.concatenate(outs, axis=1)


def _cmp_topk_kernel(qp_ref, kc_ref, vct_ref, oc_ref, mask_ref, ps_ref, *, nbp, nsel):
    qi = pl.program_id(1)
    ncol = NSA_REP * TOK
    per = NSA_SEL_BLOCK // NSA_CMP_STRIDE

    def attend(nrows):
        nblk = nrows // per
        j_io = lax.broadcasted_iota(jnp.int32, (nblk, TOK), 0)
        t_sel = qi * TOK + lax.broadcasted_iota(jnp.int32, (nblk, TOK), 1)
        cur = t_sel // NSA_SEL_BLOCK
        forced = (j_io == 0) | (j_io == cur) | (j_io == cur - 1)
        valid = j_io * NSA_SEL_BLOCK <= t_sel
        rel = (lax.broadcasted_iota(jnp.int32, (nrows, ncol), 0) * NSA_CMP_STRIDE + (NSA_CMP_BLOCK - 1)
               - lax.broadcasted_iota(jnp.int32, (nrows, ncol), 1) % TOK)
        allowed = rel <= qi * TOK
        kc = kc_ref[0, 0:nrows, :]
        vct = vct_ref[0, :, 0:nrows]
        ps_ref[pl.ds(0, 8), :] = jnp.zeros((8, TOK), F32)
        outs = []
        imps = []
        for g in range(NSA_KV_GROUPS):
            s_t = jnp.where(allowed, _dot_nt(kc, _q_rows(qp_ref[0, g])), NEG_INF)
            m = jnp.max(s_t, axis=0, keepdims=True)
            p = jnp.exp2(s_t - jnp.maximum(m, MASK_FLOOR))
            l = jnp.sum(p, axis=0, keepdims=True)
            inv = jnp.where(l > 0.0, 1.0 / l, 0.0)
            o_t = _dot(vct, p.astype(BF16)) * inv
            outs.append(_heads_to_token_major(o_t[g * NSA_HEAD_DIM:(g + 1) * NSA_HEAD_DIM, :]))
            pn = p * inv
            psum = pn[:, 0:TOK]
            for r in range(1, NSA_REP):
                psum = psum + pn[:, r * TOK:(r + 1) * TOK]
            ps_ref[pl.ds(8, nrows), :] = psum
            imp = ps_ref[pl.ds(7, nblk, stride=per), :] + ps_ref[pl.ds(8 + per - 1, nblk, stride=per), :]
            for c in range(per - 1):
                imp = imp + 2.0 * ps_ref[pl.ds(8 + c, nblk, stride=per), :]
            imps.append(jnp.where(forced, -jnp.inf, jnp.where(valid, imp, NEG_INF)))
        oc_ref[0] = jnp.concatenate(outs, axis=1)

        def pick(_, carry):
            nxt = []
            for imp_c in carry:
                mx = jnp.max(imp_c, axis=0, keepdims=True)
                idx = jnp.min(jnp.where(imp_c == mx, j_io, nblk), axis=0, keepdims=True)
                nxt.append(jnp.where(j_io == idx, -jnp.inf, imp_c))
            return tuple(nxt)

        picked = lax.fori_loop(0, min(NSA_TOPK, nsel) - N_FORCED, pick, tuple(imps))
        for g in range(NSA_KV_GROUPS):
            mask_ref[0, g, 0, 0:nblk, :] = jnp.where(valid & (picked[g] == -jnp.inf), 0.0, NEG_INF)
            if nblk < nsel:
                mask_ref[0, g, 0, nblk:nsel, :] = jnp.full((nsel - nblk, TOK), NEG_INF, F32)

    n_vis = (qi * TOK + TOK - NSA_CMP_BLOCK) // NSA_CMP_STRIDE + 1
    quarter = nbp // 4
    for k in range(1, 5):
        @pl.when((n_vis > (k - 1) * quarter) & (n_vis <= k * quarter))
        def _(k=k):
            attend(k * quarter)


def _cmp_topk(qp, kc, vct, bsz, s):
    nt = s // TOK
    nbp = s // NSA_CMP_STRIDE
    nsel = s // NSA_SEL_BLOCK
    return pl.pallas_call(
        functools.partial(_cmp_topk_kernel, nbp=nbp, nsel=nsel),
        grid=(bsz, nt),
        in_specs=[
            pl.BlockSpec((1, NSA_KV_GROUPS, TOK, NSA_REP * LANES), lambda b, i: (b, 0, i, 0)),
            pl.BlockSpec((1, nbp, LANES), lambda b, i: (b, 0, 0)),
            pl.BlockSpec((1, LANES, nbp), lambda b, i: (b, 0, 0)),
        ],
        out_specs=[
            pl.BlockSpec((1, TOK, NSA_Q_WIDTH), lambda b, i: (b, i, 0)),
            pl.BlockSpec((1, NSA_KV_GROUPS, 1, nsel, TOK), lambda b, i: (b, 0, i, 0, 0)),
        ],
        out_shape=[
            jax.ShapeDtypeStruct((bsz, s, NSA_Q_WIDTH), F32),
            jax.ShapeDtypeStruct((bsz, NSA_KV_GROUPS, nt, nsel, TOK), F32),
        ],
        scratch_shapes=[pltpu.VMEM((nbp + 8, TOK), F32)],
        compiler_params=_cparams(("parallel", "parallel")),
        name="nsa_cmp_topk",
    )(qp, kc, vct)


def _window_kernel(qp_ref, k_ref, vt_ref, o_ref):
    qi = pl.program_id(1)
    ncol = NSA_REP * TOK
    wk = WIN_TILES * TOK

    def run(first_tile, interior):
        kwin = k_ref[0, pl.ds(pl.multiple_of(first_tile * TOK, TOK), wk), :]
        outs = []
        for g in range(NSA_KV_GROUPS):
            s_t = _dot_nt(kwin, _q_rows(qp_ref[0, g]))
            if interior:
                row = lax.broadcasted_iota(jnp.int32, (TOK, ncol), 0)
                col = lax.broadcasted_iota(jnp.int32, (TOK, ncol), 1) % TOK
                oldest = jnp.where(row > col, s_t[:TOK], NEG_INF)
                newest = jnp.where(row <= col, s_t[wk - TOK:], NEG_INF)
                s_t = jnp.concatenate([oldest, s_t[TOK:wk - TOK], newest], axis=0)
            else:
                key = first_tile * TOK + lax.broadcasted_iota(jnp.int32, (wk, ncol), 0)
                t_io = qi * TOK + lax.broadcasted_iota(jnp.int32, (wk, ncol), 1) % TOK
                s_t = jnp.where((key <= t_io) & (t_io - key < NSA_WINDOW), s_t, NEG_INF)
            m = jnp.max(s_t, axis=0, keepdims=True)
            p = jnp.exp2(s_t - m)
            l = jnp.sum(p, axis=0, keepdims=True)
            vt = jnp.concatenate([vt_ref[0, g, first_tile + k] for k in range(WIN_TILES)], axis=1)
            outs.append(_heads_to_token_major(_dot(vt, p.astype(BF16)) / l))
        o_ref[0] = jnp.concatenate(outs, axis=1)

    @pl.when(qi >= WIN_TILES - 1)
    def _():
        run(qi - (WIN_TILES - 1), True)

    @pl.when(qi < WIN_TILES - 1)
    def _():
        run(0, False)


def _window(qp, k, vt, bsz, s):
    nt = s // TOK
    return pl.pallas_call(
        _window_kernel,
        grid=(bsz, nt),
        in_specs=[
            pl.BlockSpec((1, NSA_KV_GROUPS, TOK, NSA_REP * LANES), lambda b, i: (b, 0, i, 0)),
            pl.BlockSpec((1, s, LANES), lambda b, i: (b, 0, 0)),
            pl.BlockSpec((1, NSA_KV_GROUPS, nt, NSA_HEAD_DIM, TOK), lambda b, i: (b, 0, 0, 0, 0)),
        ],
        out_specs=pl.BlockSpec((1, TOK, NSA_Q_WIDTH), lambda b, i: (b, i, 0)),
        out_shape=jax.ShapeDtypeStruct((bsz, s, NSA_Q_WIDTH), F32),
        compiler_params=_cparams(("parallel", "parallel")),
        name="nsa_window",
    )(qp, k, vt)


def _selected_kernel(list_ref, cnt_ref, qp_ref, k_ref, vt_ref, bias_ref, o_ref, m_ref, acc_ref,
                     s_ref, p_ref, a_ref, qa_ref, *, nt, nsel):
    b = pl.program_id(0)
    qi = pl.program_id(1)
    ncol = NSA_REP * TOK
    blk_per_tile = SEL_TILE // NSA_SEL_BLOCK
    n_kt = nt // (SEL_TILE // TOK)
    n_q = nt // SEL_Q
    units = [(g, h) for g in range(NSA_KV_GROUPS) for h in range(SEL_Q)]
    m_ref[...] = jnp.full(m_ref.shape, NEG_INF, F32)
    acc_ref[...] = jnp.zeros(acc_ref.shape, F32)
    p_ref[...] = jnp.zeros(p_ref.shape, BF16)
    a_ref[...] = jnp.ones(a_ref.shape, F32)
    base = (b * n_q + qi) * n_kt
    count = cnt_ref[b * n_q + qi]
    n_diag = (qi * SEL_Q * TOK) // SEL_TILE

    lane = lax.broadcasted_iota(jnp.int32, (TOK, LANES), 1)
    n_var = qa_ref.shape[0]
    for g, h in units:
        bias_g = bias_ref[0, g, h]
        if nsel % LANES:
            bias_g = jnp.concatenate([bias_g, jnp.zeros((LANES - nsel % LANES, TOK), F32)], axis=0)
        own = (lane // NSA_HEAD_DIM) == g
        for v in range(n_var):
            src = bias_g[(v // 2) * LANES:(v // 2 + 1) * LANES, :].T
            if v % 2 != 1 - g:
                src = pltpu.roll(src, NSA_HEAD_DIM, 1)
            src = src.astype(BF16)
            for r in range(NSA_REP):
                q_r = qp_ref[0, g, h * TOK:(h + 1) * TOK, r * LANES:(r + 1) * LANES]
                qa_ref[v, g, h, r * TOK:(r + 1) * TOK, :] = jnp.where(own, q_r, src)

    def tile_at(i):
        return jnp.where(i < count, list_ref[base + jnp.minimum(i, n_kt - 1)], n_diag)

    def scores(j):
        v = (j * blk_per_tile) // NSA_HEAD_DIM
        for g in range(NSA_KV_GROUPS):
            kb = k_ref[0, g, pl.ds(pl.multiple_of(j * SEL_TILE, SEL_TILE), SEL_TILE), :]
            for h in range(SEL_Q):
                s_ref[g, h] = _dot_nt(kb, qa_ref[v, g, h])

    def values(j):
        for g, h in units:
            acc_ref[g, h] = acc_ref[g, h] * a_ref[g, h] + _dot(vt_ref[0, g, j], p_ref[g, h])

    def softmax(j, causal):
        for g, h in units:
            s_t = s_ref[g, h]
            if causal:
                t_io = ((qi * SEL_Q + h) * TOK
                        + lax.broadcasted_iota(jnp.int32, (SEL_TILE, ncol), 1) % TOK)
                key_pos = j * SEL_TILE + lax.broadcasted_iota(jnp.int32, (SEL_TILE, ncol), 0)
                s_t = jnp.where(key_pos <= t_io, s_t, NEG_INF)
            m_old = m_ref[g, h]
            m_new = jnp.maximum(m_old, jnp.max(s_t, axis=0, keepdims=True))
            alpha = jnp.exp2(m_old - m_new)
            p = jnp.exp2(s_t - jnp.maximum(m_new, MASK_FLOOR))
            m_ref[g, h] = m_new
            p_ref[g, h] = p.astype(BF16)
            a_ref[g, h] = alpha

    scores(tile_at(0))

    def body(i, c):
        values(tile_at(jnp.maximum(i - 1, 0)))
        softmax(tile_at(i), False)
        scores(tile_at(i + 1))
        return c

    lax.fori_loop(0, count, body, 0)
    values(tile_at(jnp.maximum(count - 1, 0)))
    softmax(n_diag, True)
    values(n_diag)
    for h in range(SEL_Q):
        outs = []
        for g in range(NSA_KV_GROUPS):
            acc = acc_ref[g, h]
            outs.append(_heads_to_token_major(acc[:NSA_HEAD_DIM] / acc[NSA_HEAD_DIM:NSA_HEAD_DIM + 1]))
        o_ref[0, h * TOK:(h + 1) * TOK, :] = jnp.concatenate(outs, axis=1)


def _selected(qp, k, vt, bias, tile_list, tile_count, bsz, s):
    nt = s // TOK
    nsel = s // NSA_SEL_BLOCK
    ncol = NSA_REP * TOK
    unit = (NSA_KV_GROUPS, SEL_Q)
    grid_spec = pltpu.PrefetchScalarGridSpec(
        num_scalar_prefetch=2,
        grid=(bsz, nt // SEL_Q),
        in_specs=[
            pl.BlockSpec((1, NSA_KV_GROUPS, SEL_Q * TOK, NSA_REP * LANES), lambda b, i, tl, tc: (b, 0, i, 0)),
            pl.BlockSpec((1, NSA_KV_GROUPS, s, LANES), lambda b, i, tl, tc: (b, 0, 0, 0)),
            pl.BlockSpec((1, NSA_KV_GROUPS, s // SEL_TILE, SEL_V_ROWS, SEL_TILE),
                         lambda b, i, tl, tc: (b, 0, 0, 0, 0)),
            pl.BlockSpec((1, NSA_KV_GROUPS, SEL_Q, nsel, TOK), lambda b, i, tl, tc: (b, 0, i, 0, 0)),
        ],
        out_specs=pl.BlockSpec((1, SEL_Q * TOK, NSA_Q_WIDTH), lambda b, i, tl, tc: (b, i, 0)),
        scratch_shapes=[pltpu.VMEM(unit + (1, ncol), F32),
                        pltpu.VMEM(unit + (SEL_V_ROWS, ncol), F32),
                        pltpu.VMEM(unit + (SEL_TILE, ncol), F32),
                        pltpu.VMEM(unit + (SEL_TILE, ncol), BF16),
                        pltpu.VMEM(unit + (1, ncol), F32),
                        pltpu.VMEM((max(nsel // NSA_HEAD_DIM, 1),) + unit + (ncol, LANES), BF16)],
    )
    return pl.pallas_call(
        functools.partial(_selected_kernel, nt=nt, nsel=nsel),
        grid_spec=grid_spec,
        out_shape=jax.ShapeDtypeStruct((bsz, s, NSA_Q_WIDTH), F32),
        compiler_params=_cparams(("parallel", "parallel")),
        name="nsa_selected",
    )(tile_list, tile_count, qp, k, vt, bias)


def _active_tiles(bias, bsz, s):
    n_q = s // (SEL_Q * TOK)
    n_kt = s // SEL_TILE
    rows = SEL_TILE // NSA_SEL_BLOCK * TOK
    act = bias.reshape(bsz, NSA_KV_GROUPS, n_q, SEL_Q, n_kt, rows).max(axis=(1, 3, 5)) > -1.0
    n_diag = (jnp.arange(n_q) * SEL_Q * TOK) // SEL_TILE
    act = act & (jnp.arange(n_kt)[None, None, :] < n_diag[None, :, None])
    order = jnp.argsort(jnp.where(act, 0, 1), axis=-1, stable=True).astype(jnp.int32)
    return order.reshape(-1), jnp.sum(act, axis=-1, dtype=jnp.int32).reshape(-1)


def _ssd_kernel(z_ref, xbc_ref, dt_ref, cw_ref, cb_ref, dtb_ref, alog_ref, dskip_ref, nw_ref, e8_ref,
                tri_ref, y_ref, ext_ref, st_ref):
    @pl.when(pl.program_id(1) == 0)
    def _():
        ext_ref[pl.ds(0, 8), :] = jnp.zeros((8, SSM_CONV_DIM), F32)
        st_ref[...] = jnp.zeros_like(st_ref)

    xb = xbc_ref[...]
    ext_ref[pl.ds(8, TOK), :] = xb
    conv = cb_ref[...] + xb * cw_ref[SSM_CONV - 1:SSM_CONV, :]
    for k in range(SSM_CONV - 1):
        conv = conv + ext_ref[pl.ds(8 - (SSM_CONV - 1) + k, TOK), :] * cw_ref[k:k + 1, :]
    ext_ref[pl.ds(0, 8), :] = xb[TOK - 8:TOK, :]
    u = _silu(conv)
    xs = u[:, :SSM_INNER]
    gn = SSM_GROUPS * SSM_STATE
    bm = u[:, SSM_INNER:SSM_INNER + gn]
    cm = u[:, SSM_INNER + gn:]
    dt = jax.nn.softplus(dt_ref[...] + dtb_ref[...])
    da = dt * (-jnp.exp(alog_ref[...]))
    tri = tri_ref[...]
    e8 = e8_ref[...]
    cum = _dot_sel_l(tri, da)
    cum_t = cum.T
    cum_e = _dot_sel_r(cum, e8)
    dt_e = _dot_sel_r(dt, e8)
    xdt = xs * dt_e
    lane = lax.broadcasted_iota(jnp.int32, (TOK, LANES), 1)
    t_io = lax.broadcasted_iota(jnp.int32, (TOK, TOK), 0)
    s_io = lax.broadcasted_iota(jnp.int32, (TOK, TOK), 1)
    causal = s_io <= t_io
    hg = SSM_HEADS // SSM_GROUPS
    y_parts = []
    cbs = []
    for g in range(SSM_GROUPS):
        in_g = (lane // SSM_STATE) == g
        cbs.append(_dot_nt(jnp.where(in_g, cm, 0.0).astype(BF16), bm.astype(BF16)))
    for c in range(SSM_HEADS // 2):
        acc = jnp.zeros((TOK, LANES), F32)
        xpair = xdt[:, c * LANES:(c + 1) * LANES]
        for hh in range(2):
            h = 2 * c + hh
            seg = cum[:, h:h + 1] - cum_t[h:h + 1, :]
            decay = jnp.where(causal, jnp.exp(jnp.where(causal, seg, 0.0)), 0.0)
            mat = (cbs[h // hg] * decay).astype(BF16)
            xh = jnp.where((lane // SSM_HEAD_DIM) == hh, xpair, 0.0).astype(BF16)
            acc = acc + _dot(mat, xh)
        y_parts.append(acc)
    y_diag = jnp.concatenate(y_parts, axis=1)
    prev = st_ref[...]
    y_off = _dot(cm.astype(BF16), prev.astype(BF16)) * jnp.exp(cum_e)
    cum_last = cum_e[TOK - 1:TOK, :]
    xdec = (xdt * jnp.exp(cum_last - cum_e)).astype(BF16)
    new = _dot(bm.T.astype(BF16), xdec)
    row_g = lax.broadcasted_iota(jnp.int32, (LANES, SSM_INNER), 0) // SSM_STATE
    col_g = lax.broadcasted_iota(jnp.int32, (LANES, SSM_INNER), 1) // (hg * SSM_HEAD_DIM)
    st_ref[...] = prev * jnp.exp(cum_last) + jnp.where(row_g == col_g, new, 0.0)
    y = (y_diag + y_off + xs * dskip_ref[...]) * _silu(z_ref[...])
    gw = SSM_INNER // SSM_GROUPS
    outs = []
    for g in range(SSM_GROUPS):
        yg = y[:, g * gw:(g + 1) * gw]
        outs.append(_rms(yg, nw_ref[:, g * gw:(g + 1) * gw]))
    y_ref[...] = jnp.concatenate(outs, axis=1)


def _ssd(proj, cw, cb, dtb, alog, dskip, nw, e8, tri, bsz, s):
    nt = s // TOK

    def col(c0, w):
        return pl.BlockSpec((TOK, w), lambda b, i: (b * nt + i, c0 // w))

    def full(shape):
        return pl.BlockSpec(shape, lambda b, i: (0,) * len(shape))

    return pl.pallas_call(
        _ssd_kernel,
        grid=(bsz, nt),
        in_specs=[col(COL_Z, SSM_INNER), col(COL_XBC, SSM_CONV_DIM), col(COL_DT, LANES),
                  full((SSM_CONV, SSM_CONV_DIM)), full((1, SSM_CONV_DIM)), full((1, LANES)), full((1, LANES)),
                  full((1, SSM_INNER)), full((1, SSM_INNER)), full((LANES, SSM_INNER)), full((TOK, TOK))],
        out_specs=pl.BlockSpec((TOK, SSM_INNER), lambda b, i: (b * nt + i, 0)),
        out_shape=jax.ShapeDtypeStruct((bsz * s, SSM_INNER), F32),
        scratch_shapes=[pltpu.VMEM((TOK + 8, SSM_CONV_DIM), F32), pltpu.VMEM((LANES, SSM_INNER), F32)],
        compiler_params=_cparams(("parallel", "arbitrary")),
        name="ssd",
    )(proj, proj, proj, cw, cb, dtb, alog, dskip, nw, e8, tri)


def _hgrn2_kernel(q_ref, f_ref, i_ref, g_ref, lb_ref, nw_ref, tri_ref, y_ref, st_ref, kp_ref, bp_ref, vp_ref):
    @pl.when(pl.program_id(1) == 0)
    def _():
        st_ref[...] = jnp.zeros_like(st_ref)
        kp_ref[...] = jnp.zeros_like(kp_ref)
        bp_ref[...] = jnp.zeros_like(bp_ref)
        vp_ref[...] = jnp.zeros_like(vp_ref)

    nsub = TOK // HG_SUB
    row = lax.broadcasted_iota(jnp.int32, (TOK, HG_KEY_DIM), 0)
    sub_id = row // HG_SUB
    in_sub = row % HG_SUB
    tri = tri_ref[...]
    outs = []
    for h in range(HG_HEADS):
        sl = slice(h * HG_KEY_DIM, (h + 1) * HG_KEY_DIM)
        lb = lb_ref[:, sl]
        fr = f_ref[:, sl]
        q = _silu(q_ref[:, sl]) * (HG_KEY_DIM ** -0.5)
        log_f = jnp.log(lb + (1.0 - lb) * jax.nn.sigmoid(fr))
        k = (1.0 - lb) * jax.nn.sigmoid(-fr)
        v = i_ref[:, sl]
        bcum = _dot_sel_l(tri, log_f)
        st = st_ref[h]
        o = _dot_nt((q * jnp.exp(bcum)).astype(BF16), st.astype(BF16))
        r_rows = [bcum[i * HG_SUB - 1:i * HG_SUB, :] for i in range(1, nsub)]
        r_loc = jnp.zeros_like(bcum)
        for i in range(1, nsub):
            r_loc = jnp.where(sub_id == i, r_rows[i - 1], r_loc)
        q_loc = q * jnp.exp(bcum - r_loc)
        q_aug = jnp.concatenate(
            [jnp.where(sub_id == i, q_loc, 0.0).astype(BF16) for i in range(1, nsub)], axis=1)
        k_aug = jnp.concatenate(
            [jnp.where(sub_id < i, k * jnp.exp(jnp.where(sub_id < i, r_rows[i - 1] - bcum, 0.0)), 0.0).astype(BF16)
             for i in range(1, nsub)], axis=1)
        att = _dot_nt(q_aug, k_aug)
        o = o + _dot(att.astype(BF16), v.astype(BF16))
        kp_ref[pl.ds(HG_SUB, TOK), :] = k
        bp_ref[pl.ds(HG_SUB, TOK), :] = bcum
        vp_ref[pl.ds(HG_SUB, TOK), :] = v
        o = o + jnp.sum(q * k, axis=1, keepdims=True) * v
        for d in range(1, HG_SUB):
            ks = kp_ref[pl.ds(HG_SUB - d, TOK), :]
            bs = bp_ref[pl.ds(HG_SUB - d, TOK), :]
            vs = vp_ref[pl.ds(HG_SUB - d, TOK), :]
            ok = in_sub >= d
            w = jnp.sum(q * ks * jnp.exp(jnp.where(ok, bcum - bs, 0.0)), axis=1, keepdims=True)
            o = o + jnp.where(ok, w * vs, 0.0)
        b_last = bcum[TOK - 1:TOK, :]
        kdec = (k * jnp.exp(b_last - bcum)).astype(BF16)
        st_ref[h] = st * jnp.exp(b_last) + _dot(v.T.astype(BF16), kdec)
        outs.append(_rms(o, nw_ref[...]) * _silu(g_ref[:, sl]))
    y_ref[...] = jnp.concatenate(outs, axis=1)


def _hgrn2(proj, lb, nw, tri, bsz, s):
    nt = s // TOK

    def col(c0):
        return pl.BlockSpec((TOK, HG_WIDTH), lambda b, i: (b * nt + i, c0 // HG_WIDTH))

    def full(shape):
        return pl.BlockSpec(shape, lambda b, i: (0,) * len(shape))

    pad = pltpu.VMEM((TOK + HG_SUB, HG_KEY_DIM), F32)
    return pl.pallas_call(
        _hgrn2_kernel,
        grid=(bsz, nt),
        in_specs=[col(COL_HQ), col(COL_HF), col(COL_HI), col(COL_HG), full((1, HG_WIDTH)),
                  full((1, HG_VAL_DIM)), full((TOK, TOK))],
        out_specs=pl.BlockSpec((TOK, HG_WIDTH), lambda b, i: (b * nt + i, 0)),
        out_shape=jax.ShapeDtypeStruct((bsz * s, HG_WIDTH), F32),
        scratch_shapes=[pltpu.VMEM((HG_HEADS, HG_VAL_DIM, HG_KEY_DIM), F32), pad, pad, pad],
        compiler_params=_cparams(("parallel", "arbitrary")),
        name="hgrn2",
    )(proj, proj, proj, proj, lb, nw, tri)


def _merge_kernel(ga_ref, gb_ref, gc_ref, ng_ref, oc_ref, os_ref, ow_ref, yb_ref, yc_ref, x_ref,
                  eg_ref, wa_ref, wb_ref, wc_ref, wo_ref, o_ref):
    sg = jax.nn.sigmoid(ng_ref[...])
    nsa = (_dot_sel_r(sg, eg_ref[0]) * oc_ref[...] + _dot_sel_r(sg, eg_ref[1]) * os_ref[...]
           + _dot_sel_r(sg, eg_ref[2]) * ow_ref[...])
    ya = _dot(nsa.astype(BF16), wa_ref[...])
    yb = _dot(yb_ref[...].astype(BF16), wb_ref[...])
    yc = _dot(yc_ref[...].astype(BF16), wc_ref[...])
    merged = (jax.nn.sigmoid(ga_ref[...]) * ya + jax.nn.sigmoid(gb_ref[...]) * yb
              + jax.nn.sigmoid(gc_ref[...]) * yc)
    o_ref[...] = x_ref[...] + _dot(merged.astype(BF16), wo_ref[...])


def _merge(proj, oc, os_, ow, yb, yc, x2, eg, wa, wb, wc, wo):
    t = x2.shape[0]
    tm = min(512, t)

    def col(c0, w):
        return pl.BlockSpec((tm, w), lambda i: (i, c0 // w))

    def full(shape):
        return pl.BlockSpec(shape, lambda i: (0,) * len(shape))

    row512 = pl.BlockSpec((tm, 512), lambda i: (i, 0))
    return pl.pallas_call(
        _merge_kernel,
        grid=(t // tm,),
        in_specs=[col(COL_GA, D_MODEL), col(COL_GB, D_MODEL), col(COL_GC, D_MODEL), col(COL_NG, LANES),
                  row512, row512, row512, row512, row512,
                  pl.BlockSpec((tm, D_MODEL), lambda i: (i, 0)),
                  full((3, LANES, NSA_Q_WIDTH)), full((NSA_Q_WIDTH, D_MODEL)), full((SSM_INNER, D_MODEL)),
                  full((HG_WIDTH, D_MODEL)), full((D_MODEL, D_MODEL))],
        out_specs=pl.BlockSpec((tm, D_MODEL), lambda i: (i, 0)),
        out_shape=jax.ShapeDtypeStruct((t, D_MODEL), F32),
        compiler_params=_cparams(("parallel",)),
        name="merge",
    )(proj, proj, proj, proj, oc, os_, ow, yb, yc, x2, eg, wa, wb, wc, wo)


def _xa_mem_kernel(mem_ref, nm_ref, wk_ref, wv_ref, kn_ref, k_ref, v_ref):
    m = _rms(mem_ref[0], nm_ref[...]).astype(BF16)
    k = _dot(m, wk_ref[...])
    ks = [_rms(k[:, h * XA_HEAD_DIM:(h + 1) * XA_HEAD_DIM], kn_ref[...]) for h in range(XA_HEADS)]
    k_ref[0] = jnp.concatenate(ks, axis=1).astype(BF16)
    v_ref[0] = _dot(m, wv_ref[...]).astype(BF16)


def _xa_mem(mem, nm, wk, wv, kn):
    bsz, ml, _ = mem.shape

    def full(shape):
        return pl.BlockSpec(shape, lambda b: (0,) * len(shape))

    return pl.pallas_call(
        _xa_mem_kernel,
        grid=(bsz,),
        in_specs=[pl.BlockSpec((1, ml, D_MODEL), lambda b: (b, 0, 0)), full((1, D_MODEL)),
                  full((D_MODEL, XA_WIDTH)), full((D_MODEL, XA_WIDTH)), full((1, XA_HEAD_DIM))],
        out_specs=[pl.BlockSpec((1, ml, XA_WIDTH), lambda b: (b, 0, 0)),
                   pl.BlockSpec((1, ml, XA_WIDTH), lambda b: (b, 0, 0))],
        out_shape=[jax.ShapeDtypeStruct((bsz, ml, XA_WIDTH), BF16),
                   jax.ShapeDtypeStruct((bsz, ml, XA_WIDTH), BF16)],
        compiler_params=_cparams(("parallel",)),
        name="xa_mem",
    )(mem, nm, wk, wv, kn)


def _xa_kernel(x_ref, nx_ref, wq_ref, qn_ref, k_ref, v_ref, wo_ref, o_ref):
    x = x_ref[...]
    h = _rms(x, nx_ref[...]).astype(BF16)
    q = _dot(h, wq_ref[...])
    outs = []
    for hd in range(XA_HEADS):
        sl = slice(hd * XA_HEAD_DIM, (hd + 1) * XA_HEAD_DIM)
        qh = _rms(q[:, sl], qn_ref[...]).astype(BF16)
        sc = _dot_nt(qh, k_ref[0, :, sl]) * (XA_HEAD_DIM ** -0.5)
        sc = sc - jnp.max(sc, axis=-1, keepdims=True)
        p = jnp.exp(sc)
        p = p / jnp.sum(p, axis=-1, keepdims=True)
        outs.append(_dot(p.astype(BF16), v_ref[0, :, sl]))
    o = jnp.concatenate(outs, axis=1).astype(BF16)
    o_ref[...] = x + _dot(o, wo_ref[...])


def _xa(x2, nx, wq, qn, k, v, wo, s):
    t = x2.shape[0]
    tm = min(512, s)
    ml = k.shape[1]
    per_b = s // tm

    def full(shape):
        return pl.BlockSpec(shape, lambda i: (0,) * len(shape))

    return pl.pallas_call(
        _xa_kernel,
        grid=(t // tm,),
        in_specs=[pl.BlockSpec((tm, D_MODEL), lambda i: (i, 0)), full((1, D_MODEL)), full((D_MODEL, XA_WIDTH)),
                  full((1, XA_HEAD_DIM)),
                  pl.BlockSpec((1, ml, XA_WIDTH), lambda i: (i // per_b, 0, 0)),
                  pl.BlockSpec((1, ml, XA_WIDTH), lambda i: (i // per_b, 0, 0)),
                  full((XA_WIDTH, D_MODEL))],
        out_specs=pl.BlockSpec((tm, D_MODEL), lambda i: (i, 0)),
        out_shape=jax.ShapeDtypeStruct((t, D_MODEL), F32),
        compiler_params=_cparams(("parallel",)),
        name="cross_attn",
    )(x2, nx, wq, qn, k, v, wo)


def _ffn_kernel(x_ref, nw_ref, wg_ref, wu_ref, wd_ref, o_ref, h_ref, acc_ref):
    j = pl.program_id(1)

    @pl.when(j == 0)
    def _():
        h_ref[...] = _rms(x_ref[...], nw_ref[...]).astype(BF16)
        acc_ref[...] = jnp.zeros_like(acc_ref)

    h = h_ref[...]
    a = _silu(_dot(h, wg_ref[...])) * _dot(h, wu_ref[...])
    acc_ref[...] += _dot(a.astype(BF16), wd_ref[...])

    @pl.when(j == pl.num_programs(1) - 1)
    def _():
        o_ref[...] = x_ref[...] + acc_ref[...]


def _ffn(x2, nw, wg, wu, wd):
    t = x2.shape[0]
    tm = min(1024, t)
    th = FFN_HIDDEN // 2
    return pl.pallas_call(
        _ffn_kernel,
        grid=(t // tm, FFN_HIDDEN // th),
        in_specs=[pl.BlockSpec((tm, D_MODEL), lambda i, j: (i, 0)),
                  pl.BlockSpec((1, D_MODEL), lambda i, j: (0, 0)),
                  pl.BlockSpec((D_MODEL, th), lambda i, j: (0, j)),
                  pl.BlockSpec((D_MODEL, th), lambda i, j: (0, j)),
                  pl.BlockSpec((th, D_MODEL), lambda i, j: (j, 0))],
        out_specs=pl.BlockSpec((tm, D_MODEL), lambda i, j: (i, 0)),
        out_shape=jax.ShapeDtypeStruct((t, D_MODEL), F32),
        scratch_shapes=[pltpu.VMEM((tm, D_MODEL), BF16), pltpu.VMEM((tm, D_MODEL), F32)],
        compiler_params=_cparams(("parallel", "arbitrary")),
        name="ffn",
    )(x2, nw, wg, wu, wd)


def _pack_w_in(w):
    parts = {}
    off = 0
    names = ("q", "kc", "vc", "ks", "vs", "kw", "vw", "ng", "z", "xbc", "dt", "hq", "hf", "hi", "hg",
             "ga", "gb", "gc")
    for name, size in zip(names, SPLIT_SIZES):
        parts[name] = w[:, off:off + size]
        off += size

    def padded(a, width):
        return jnp.pad(a, ((0, 0), (0, width - a.shape[1])))

    cols = [parts["ga"], parts["gb"], parts["gc"], parts["q"], parts["z"], parts["hq"], parts["hf"],
            parts["hi"], parts["hg"], parts["xbc"], parts["kc"], parts["vc"], parts["ks"], parts["vs"],
            parts["kw"], parts["vw"], padded(parts["ng"], LANES), padded(parts["dt"], LANES)]
    packed = jnp.concatenate(cols, axis=1)
    return padded(packed, PROJ_WIDTH).astype(BF16)


def _rope_tables(pos):
    half = NSA_HEAD_DIM // 2
    inv = 1.0 / (ROPE_THETA ** (jnp.arange(half, dtype=F32) / half))
    ang = pos.astype(F32)[:, None] * inv[None, :]
    reps = LANES // half
    return jnp.tile(jnp.cos(ang), (1, reps)), jnp.tile(jnp.sin(ang), (1, reps))


def _block_diag2(w):
    z = jnp.zeros_like(w)
    return jnp.concatenate([jnp.concatenate([w, z], axis=-1), jnp.concatenate([z, w], axis=-1)], axis=-2)


def kernel(x, mem, norm_mix, w_in, nsa_q_norm, nsa_k_norm, nsa_cmp_pos_k, nsa_cmp_pos_v, nsa_cmp_w1_k, nsa_cmp_w2_k, nsa_cmp_w1_v, nsa_cmp_w2_v, w_nsa_o, ssm_conv_w, ssm_conv_b, ssm_dt_bias, ssm_a_log, ssm_d, ssm_norm, w_ssm_o, hg_lb_logits, hg_norm, w_hg_o, w_out, norm_xa, norm_mem, xa_w_q, xa_w_k, xa_w_v, xa_q_norm, xa_k_norm, xa_w_o, norm_ffn, ffn_w_gate, ffn_w_up, ffn_w_down):
    bsz, s, d = x.shape
    depth = w_in.shape[0]
    assert d == D_MODEL and s % (8 * TOK) == 0 and s // TOK <= TOK
    t = bsz * s
    hd = NSA_HEAD_DIM

    cos_t, sin_t = _rope_tables(jnp.arange(s))
    nbp = s // NSA_CMP_STRIDE
    cos_c, sin_c = _rope_tables(jnp.arange(nbp) * NSA_CMP_STRIDE + NSA_CMP_BLOCK - 1)
    tri = jnp.tril(jnp.ones((TOK, TOK), F32)).astype(BF16)
    rows = np.arange(LANES)[:, None]
    cols = np.arange(NSA_Q_WIDTH)[None, :]
    e8 = jnp.asarray(rows == cols // SSM_HEAD_DIM, F32).astype(BF16)
    eg = jnp.stack([jnp.asarray(rows == 3 * (cols // hd) + c, F32) for c in range(3)]).astype(BF16)

    lb_sm = jax.nn.softmax(hg_lb_logits.astype(F32), axis=0)
    lb_all = jnp.cumsum(lb_sm, axis=0) - lb_sm[0:1]

    def pad_lanes(v):
        return jnp.pad(v, (0, LANES - v.shape[0]))[None, :]

    x2 = x.reshape(t, d)
    for l in range(depth):
        proj = _in_proj(x2, norm_mix[l][None, :], _pack_w_in(w_in[l]))

        qn2 = jnp.tile(nsa_q_norm[l], 2)[None, :]
        kn2 = jnp.tile(nsa_k_norm[l], 2)[None, :]
        qp, ksn, kwn, vst, vwt = _nsa_prep(proj, cos_t, sin_t, qn2, kn2, bsz, s)
        w1k = _block_diag2(nsa_cmp_w1_k[l].reshape(NSA_CMP_BLOCK, hd, hd)).astype(BF16)
        w1v = _block_diag2(nsa_cmp_w1_v[l].reshape(NSA_CMP_BLOCK, hd, hd)).astype(BF16)
        kc, vct = _compress(proj, jnp.tile(nsa_cmp_pos_k[l], (1, 2)), jnp.tile(nsa_cmp_pos_v[l], (1, 2)),
                            w1k, _block_diag2(nsa_cmp_w2_k[l]).astype(BF16),
                            w1v, _block_diag2(nsa_cmp_w2_v[l]).astype(BF16), kn2, cos_c, sin_c, bsz, s)
        o_c, sel_bias = _cmp_topk(qp, kc, vct, bsz, s)
        tile_list, tile_count = _active_tiles(sel_bias, bsz, s)
        o_s = _selected(qp, ksn, vst, sel_bias, tile_list, tile_count, bsz, s)
        o_w = _window(qp, kwn, vwt, bsz, s)

        y_b = _ssd(proj, ssm_conv_w[l], ssm_conv_b[l][None, :], pad_lanes(ssm_dt_bias[l]),
                   pad_lanes(ssm_a_log[l]), jnp.repeat(ssm_d[l], SSM_HEAD_DIM)[None, :],
                   ssm_norm[l][None, :], e8, tri, bsz, s)
        y_c = _hgrn2(proj, lb_all[l][None, :], hg_norm[l][None, :], tri, bsz, s)

        x2 = _merge(proj, o_c.reshape(t, -1), o_s.reshape(t, -1), o_w.reshape(t, -1), y_b, y_c, x2, eg,
                    w_nsa_o[l].astype(BF16), w_ssm_o[l].astype(BF16), w_hg_o[l].astype(BF16),
                    w_out[l].astype(BF16))

        k_m, v_m = _xa_mem(mem, norm_mem[l][None, :], xa_w_k[l].astype(BF16), xa_w_v[l].astype(BF16),
                           xa_k_norm[l][None, :])
        x2 = _xa(x2, norm_xa[l][None, :], xa_w_q[l].astype(BF16), xa_q_norm[l][None, :], k_m, v_m,
                 xa_w_o[l].astype(BF16), s)
        x2 = _ffn(x2, norm_ffn[l][None, :], ffn_w_gate[l].astype(BF16), ffn_w_up[l].astype(BF16),
                  ffn_w_down[l].astype(BF16))
    return x2.reshape(bsz, s, d)
```

```python
import functools
import math

import numpy as np
import jax
import jax.numpy as jnp
from jax import lax
from jax.experimental import pallas as pl
from jax.experimental.pallas import tpu as pltpu

F32 = jnp.float32
BF16 = jnp.bfloat16

D_MODEL = 1024
NORM_EPS = 1e-6
ROPE_THETA = 10000.0
NEG_INF = -1e30
FORCE_SCORE = 1e9

NSA_HEADS = 8
NSA_KV_GROUPS = 2
NSA_REP = NSA_HEADS // NSA_KV_GROUPS
NSA_HEAD_DIM = 64
NSA_CMP_BLOCK = 32
NSA_CMP_STRIDE = 16
NSA_SEL_BLOCK = 64
NSA_TOPK = 16
N_FORCED = 3
NSA_WINDOW = 512
NSA_Q_WIDTH = NSA_HEADS * NSA_HEAD_DIM
NSA_KV_WIDTH = NSA_KV_GROUPS * NSA_HEAD_DIM

SSM_INNER = 512
SSM_HEAD_DIM = 64
SSM_HEADS = SSM_INNER // SSM_HEAD_DIM
SSM_GROUPS = 2
SSM_STATE = 64
SSM_CONV = 4
SSM_CONV_DIM = SSM_INNER + 2 * SSM_GROUPS * SSM_STATE

HG_HEADS = 4
HG_KEY_DIM = 128
HG_VAL_DIM = 128
HG_WIDTH = HG_HEADS * HG_KEY_DIM
HG_SUB = 16

XA_HEADS = 4
XA_HEAD_DIM = 128
XA_WIDTH = XA_HEADS * XA_HEAD_DIM

FFN_HIDDEN = -(-(8 * D_MODEL) // (3 * 256)) * 256

SPLIT_SIZES = (
    NSA_Q_WIDTH, NSA_KV_WIDTH, NSA_KV_WIDTH, NSA_KV_WIDTH, NSA_KV_WIDTH, NSA_KV_WIDTH, NSA_KV_WIDTH,
    3 * NSA_HEADS, SSM_INNER, SSM_CONV_DIM, SSM_HEADS, HG_WIDTH, HG_WIDTH, HG_HEADS * HG_VAL_DIM,
    HG_HEADS * HG_VAL_DIM, D_MODEL, D_MODEL, D_MODEL,
)

LANES = 128
TOK = 128
PREP_TILES = 4
SEL_Q = 4
SEL_TILE = 512
SEL_V_ROWS = NSA_HEAD_DIM + 16
WIN_TILES = NSA_WINDOW // TOK + 1
LOG2_E = math.log2(math.e)
MASK_FLOOR = -1e20
VMEM_LIMIT = 56 * 1024 * 1024

PROJ_WIDTH = 8192
COL_GA, COL_GB, COL_GC = 0, 1024, 2048
COL_Q = 3072
COL_Z = 3584
COL_HQ, COL_HF, COL_HI, COL_HG = 4096, 4608, 5120, 5632
COL_XBC = 6144
COL_KC, COL_VC, COL_KS, COL_VS, COL_KW, COL_VW = 6912, 7040, 7168, 7296, 7424, 7552
COL_NG = 7680
COL_DT = 7808


def _cparams(sem):
    return pltpu.CompilerParams(dimension_semantics=sem, vmem_limit_bytes=VMEM_LIMIT)


def _dot(a, b):
    return jnp.dot(a, b, preferred_element_type=F32)


def _dot_nt(a, b):
    return lax.dot_general(a, b, (((1,), (1,)), ((), ())), preferred_element_type=F32)


def _split3(a):
    hi = a.astype(BF16)
    r1 = a - hi.astype(F32)
    mid = r1.astype(BF16)
    lo = (r1 - mid.astype(F32)).astype(BF16)
    return hi, mid, lo


def _dot_sel_r(a, sel):
    hi, mid, lo = _split3(a)
    return _dot(hi, sel) + _dot(mid, sel) + _dot(lo, sel)


def _dot_sel_l(sel, a):
    hi, mid, lo = _split3(a)
    return _dot(sel, hi) + _dot(sel, mid) + _dot(sel, lo)


def _silu(x):
    return x * jax.nn.sigmoid(x)


def _rms(x, w):
    return x * lax.rsqrt(jnp.mean(x * x, axis=-1, keepdims=True) + NORM_EPS) * w


def _inproj_kernel(x_ref, nw_ref, w_ref, o_ref, h_ref):
    @pl.when(pl.program_id(1) == 0)
    def _():
        h_ref[...] = _rms(x_ref[...], nw_ref[...]).astype(BF16)

    o_ref[...] = _dot(h_ref[...], w_ref[...])


def _in_proj(x2, norm_w, w_packed):
    t = x2.shape[0]
    tm = min(2048, t)
    tn = 1024
    return pl.pallas_call(
        _inproj_kernel,
        grid=(t // tm, PROJ_WIDTH // tn),
        in_specs=[
            pl.BlockSpec((tm, D_MODEL), lambda i, j: (i, 0)),
            pl.BlockSpec((1, D_MODEL), lambda i, j: (0, 0)),
            pl.BlockSpec((D_MODEL, tn), lambda i, j: (0, j)),
        ],
        out_specs=pl.BlockSpec((tm, tn), lambda i, j: (i, j)),
        out_shape=jax.ShapeDtypeStruct((t, PROJ_WIDTH), F32),
        scratch_shapes=[pltpu.VMEM((tm, D_MODEL), BF16)],
        compiler_params=_cparams(("parallel", "arbitrary")),
        name="in_proj",
    )(x2, norm_w, w_packed)


def _norm_rope_pair(x, w, cos, sin):
    lane = lax.broadcasted_iota(jnp.int32, x.shape, 1)
    lo = lane < NSA_HEAD_DIM
    sq = x * x
    s_lo = jnp.sum(jnp.where(lo, sq, 0.0), axis=1, keepdims=True)
    s_hi = jnp.sum(jnp.where(lo, 0.0, sq), axis=1, keepdims=True)
    ms = jnp.where(lo, s_lo, s_hi) * (1.0 / NSA_HEAD_DIM)
    y = x * lax.rsqrt(ms + NORM_EPS) * w
    half = NSA_HEAD_DIM // 2
    first = (lane % NSA_HEAD_DIM) < half
    rot = jnp.where(first, -pltpu.roll(y, LANES - half, 1), pltpu.roll(y, half, 1))
    return y * cos + rot * sin


def _nsa_prep_kernel(q_ref, ks_ref, vs_ref, kw_ref, vw_ref, cos_ref, sin_ref, qn_ref, kn_ref,
                     qp_ref, ksn_ref, kwn_ref, vst_ref, vwt_ref):
    lane = lax.broadcasted_iota(jnp.int32, (TOK, LANES), 1)
    row = lax.broadcasted_iota(jnp.int32, (TOK, LANES), 0)
    scale = NSA_HEAD_DIM ** -0.5 * LOG2_E
    ones_rows = (lax.broadcasted_iota(jnp.int32, (SEL_V_ROWS - NSA_HEAD_DIM, TOK), 0) == 0).astype(BF16)
    for u in range(PREP_TILES):
        rows = slice(u * TOK, (u + 1) * TOK)
        cos = cos_ref[rows, :]
        sin = sin_ref[rows, :]
        for c in range(NSA_HEADS // 2):
            y = _norm_rope_pair(q_ref[rows, c * LANES:(c + 1) * LANES], qn_ref[...], cos, sin) * scale
            y_sw = pltpu.roll(y, NSA_HEAD_DIM, 1)
            for hh in range(2):
                h = 2 * c + hh
                g, r = divmod(h, NSA_REP)
                src = y if hh == g else y_sw
                keep = (lane // NSA_HEAD_DIM) == g
                qp_ref[0, g, rows, r * LANES:(r + 1) * LANES] = jnp.where(keep, src, 0.0).astype(BF16)
        ks_n = _norm_rope_pair(ks_ref[rows, :], kn_ref[...], cos, sin)
        tile = pl.program_id(1) * PREP_TILES + u
        blk = (tile * (TOK // NSA_SEL_BLOCK) + row // NSA_SEL_BLOCK) % NSA_HEAD_DIM
        onehot = ((lane % NSA_HEAD_DIM) == blk).astype(F32)
        for g in range(NSA_KV_GROUPS):
            ksn_ref[0, g, rows, :] = jnp.where((lane // NSA_HEAD_DIM) == g, ks_n, onehot).astype(BF16)
        kwn_ref[0, rows, :] = _norm_rope_pair(kw_ref[rows, :], kn_ref[...], cos, sin).astype(BF16)
        vs_t = vs_ref[rows, :].T.astype(BF16)
        vw_t = vw_ref[rows, :].T.astype(BF16)
        for g in range(NSA_KV_GROUPS):
            vst_ref[0, g, 0, 0:NSA_HEAD_DIM, rows] = vs_t[g * NSA_HEAD_DIM:(g + 1) * NSA_HEAD_DIM, :]
            vst_ref[0, g, 0, NSA_HEAD_DIM:SEL_V_ROWS, rows] = ones_rows
            vwt_ref[0, g, u] = vw_t[g * NSA_HEAD_DIM:(g + 1) * NSA_HEAD_DIM, :]


def _nsa_prep(proj, cos_t, sin_t, qn2, kn2, bsz, s):
    tq = PREP_TILES * TOK
    n_step = s // tq

    def col(c0, w):
        return pl.BlockSpec((tq, w), lambda b, i: (b * n_step + i, c0 // w))

    tab = pl.BlockSpec((tq, LANES), lambda b, i: (i, 0))
    vec = pl.BlockSpec((1, LANES), lambda b, i: (0, 0))
    return pl.pallas_call(
        _nsa_prep_kernel,
        grid=(bsz, n_step),
        in_specs=[col(COL_Q, NSA_Q_WIDTH), col(COL_KS, LANES), col(COL_VS, LANES), col(COL_KW, LANES),
                  col(COL_VW, LANES), tab, tab, vec, vec],
        out_specs=[
            pl.BlockSpec((1, NSA_KV_GROUPS, tq, NSA_REP * LANES), lambda b, i: (b, 0, i, 0)),
            pl.BlockSpec((1, NSA_KV_GROUPS, tq, LANES), lambda b, i: (b, 0, i, 0)),
            pl.BlockSpec((1, tq, LANES), lambda b, i: (b, i, 0)),
            pl.BlockSpec((1, NSA_KV_GROUPS, 1, SEL_V_ROWS, SEL_TILE), lambda b, i: (b, 0, i, 0, 0)),
            pl.BlockSpec((1, NSA_KV_GROUPS, PREP_TILES, NSA_HEAD_DIM, TOK), lambda b, i: (b, 0, i, 0, 0)),
        ],
        out_shape=[
            jax.ShapeDtypeStruct((bsz, NSA_KV_GROUPS, s, NSA_REP * LANES), BF16),
            jax.ShapeDtypeStruct((bsz, NSA_KV_GROUPS, s, LANES), BF16),
            jax.ShapeDtypeStruct((bsz, s, LANES), BF16),
            jax.ShapeDtypeStruct((bsz, NSA_KV_GROUPS, s // SEL_TILE, SEL_V_ROWS, SEL_TILE), BF16),
            jax.ShapeDtypeStruct((bsz, NSA_KV_GROUPS, s // TOK, NSA_HEAD_DIM, TOK), BF16),
        ],
        compiler_params=_cparams(("parallel", "parallel")),
        name="nsa_prep",
    )(proj, proj, proj, proj, proj, cos_t, sin_t, qn2, kn2)


def _compress_kernel(kc_ref, vc_ref, posk_ref, posv_ref, w1k_ref, w2k_ref, w1v_ref, w2v_ref, kn_ref,
                     cos_ref, sin_ref, kco_ref, vct_ref, sh_ref, *, nbp):
    half_blk = NSA_CMP_BLOCK // 2

    def compress(x_ref, pos_ref, w1_ref, w2_ref):
        acc_a = jnp.zeros((nbp, LANES), F32)
        acc_b = jnp.zeros((nbp, LANES), F32)
        for l in range(half_blk):
            xl = x_ref[pl.ds(l, nbp, stride=NSA_CMP_STRIDE), :]
            acc_a = acc_a + _dot((xl + pos_ref[l:l + 1, :]).astype(BF16), w1_ref[l])
            acc_b = acc_b + _dot((xl + pos_ref[half_blk + l:half_blk + l + 1, :]).astype(BF16),
                                 w1_ref[half_blk + l])
        sh_ref[pl.ds(0, nbp), :] = acc_b
        sh_ref[pl.ds(nbp, 8), :] = jnp.zeros((8, LANES), F32)
        hid = acc_a + sh_ref[pl.ds(1, nbp), :]
        return _dot(_silu(hid).astype(BF16), w2_ref[...])

    kc = compress(kc_ref, posk_ref, w1k_ref, w2k_ref)
    kco_ref[0] = _norm_rope_pair(kc, kn_ref[...], cos_ref[...], sin_ref[...]).astype(BF16)
    vc = compress(vc_ref, posv_ref, w1v_ref, w2v_ref)
    vct_ref[0] = vc.T.astype(BF16)


def _compress(proj, posk, posv, w1k, w2k, w1v, w2v, kn2, cos_c, sin_c, bsz, s):
    nbp = s // NSA_CMP_STRIDE

    def full(shape):
        return pl.BlockSpec(shape, lambda b: (0,) * len(shape))

    return pl.pallas_call(
        functools.partial(_compress_kernel, nbp=nbp),
        grid=(bsz,),
        in_specs=[
            pl.BlockSpec((s, LANES), lambda b: (b, COL_KC // LANES)),
            pl.BlockSpec((s, LANES), lambda b: (b, COL_VC // LANES)),
            full((NSA_CMP_BLOCK, LANES)), full((NSA_CMP_BLOCK, LANES)),
            full((NSA_CMP_BLOCK, LANES, LANES)), full((LANES, LANES)),
            full((NSA_CMP_BLOCK, LANES, LANES)), full((LANES, LANES)),
            full((1, LANES)), full((nbp, LANES)), full((nbp, LANES)),
        ],
        out_specs=[
            pl.BlockSpec((1, nbp, LANES), lambda b: (b, 0, 0)),
            pl.BlockSpec((1, LANES, nbp), lambda b: (b, 0, 0)),
        ],
        out_shape=[
            jax.ShapeDtypeStruct((bsz, nbp, LANES), BF16),
            jax.ShapeDtypeStruct((bsz, LANES, nbp), BF16),
        ],
        scratch_shapes=[pltpu.VMEM((nbp + 8, LANES), F32)],
        compiler_params=_cparams(("parallel",)),
        name="nsa_compress",
    )(proj, proj, posk, posv, w1k, w2k, w1v, w2v, kn2, cos_c, sin_c)


def _q_rows(qp):
    return jnp.concatenate([qp[:, r * LANES:(r + 1) * LANES] for r in range(NSA_REP)], axis=0)


def _heads_to_token_major(acc_g):
    outs = []
    for p in range(NSA_REP // 2):
        blk = jnp.concatenate([acc_g[:, (2 * p) * TOK:(2 * p + 1) * TOK],
                               acc_g[:, (2 * p + 1) * TOK:(2 * p + 2) * TOK]], axis=0)
        outs.append(blk.T)
    return jnp.concatenate(outs, axis=1)


def _cmp_topk_kernel(qp_ref, kc_ref, vct_ref, oc_ref, mask_ref, ps_ref, *, nbp, nsel):
    qi = pl.program_id(1)
    ncol = NSA_REP * TOK
    per = NSA_SEL_BLOCK // NSA_CMP_STRIDE

    def attend(nrows):
        nblk = nrows // per
        j_io = lax.broadcasted_iota(jnp.int32, (nblk, TOK), 0)
        t_sel = qi * TOK + lax.broadcasted_iota(jnp.int32, (nblk, TOK), 1)
        cur = t_sel // NSA_SEL_BLOCK
        forced = (j_io == 0) | (j_io == cur) | (j_io == cur - 1)
        valid = j_io * NSA_SEL_BLOCK <= t_sel
        rel = (lax.broadcasted_iota(jnp.int32, (nrows, ncol), 0) * NSA_CMP_STRIDE + (NSA_CMP_BLOCK - 1)
               - lax.broadcasted_iota(jnp.int32, (nrows, ncol), 1) % TOK)
        allowed = rel <= qi * TOK
        kc = kc_ref[0, 0:nrows, :]
        vct = vct_ref[0, :, 0:nrows]
        ps_ref[pl.ds(0, 8), :] = jnp.zeros((8, TOK), F32)
        outs = []
        imps = []
        for g in range(NSA_KV_GROUPS):
            s_t = jnp.where(allowed, _dot_nt(kc, _q_rows(qp_ref[0, g])), NEG_INF)
            m = jnp.max(s_t, axis=0, keepdims=True)
            p = jnp.exp2(s_t - jnp.maximum(m, MASK_FLOOR))
            l = jnp.sum(p, axis=0, keepdims=True)
            inv = jnp.where(l > 0.0, 1.0 / l, 0.0)
            o_t = _dot(vct, p.astype(BF16)) * inv
            outs.append(_heads_to_token_major(o_t[g * NSA_HEAD_DIM:(g + 1) * NSA_HEAD_DIM, :]))
            pn = p * inv
            psum = pn[:, 0:TOK]
            for r in range(1, NSA_REP):
                psum = psum + pn[:, r * TOK:(r + 1) * TOK]
            ps_ref[pl.ds(8, nrows), :] = psum
            imp = ps_ref[pl.ds(7, nblk, stride=per), :] + ps_ref[pl.ds(8 + per - 1, nblk, stride=per), :]
            for c in range(per - 1):
                imp = imp + 2.0 * ps_ref[pl.ds(8 + c, nblk, stride=per), :]
            imps.append(jnp.where(forced, -jnp.inf, jnp.where(valid, imp, NEG_INF)))
        oc_ref[0] = jnp.concatenate(outs, axis=1)

        def pick(_, carry):
            nxt = []
            for imp_c in carry:
                mx = jnp.max(imp_c, axis=0, keepdims=True)
                idx = jnp.min(jnp.where(imp_c == mx, j_io, nblk), axis=0, keepdims=True)
                nxt.append(jnp.where(j_io == idx, -jnp.inf, imp_c))
            return tuple(nxt)

        picked = lax.fori_loop(0, min(NSA_TOPK, nsel) - N_FORCED, pick, tuple(imps))
        for g in range(NSA_KV_GROUPS):
            mask_ref[0, g, 0, 0:nblk, :] = jnp.where(valid & (picked[g] == -jnp.inf), 0.0, NEG_INF)
            if nblk < nsel:
                mask_ref[0, g, 0, nblk:nsel, :] = jnp.full((nsel - nblk, TOK), NEG_INF, F32)

    n_vis = (qi * TOK + TOK - NSA_CMP_BLOCK) // NSA_CMP_STRIDE + 1
    quarter = nbp // 4
    for k in range(1, 5):
        @pl.when((n_vis > (k - 1) * quarter) & (n_vis <= k * quarter))
        def _(k=k):
            attend(k * quarter)


def _cmp_topk(qp, kc, vct, bsz, s):
    nt = s // TOK
    nbp = s // NSA_CMP_STRIDE
    nsel = s // NSA_SEL_BLOCK
    return pl.pallas_call(
        functools.partial(_cmp_topk_kernel, nbp=nbp, nsel=nsel),
        grid=(bsz, nt),
        in_specs=[
            pl.BlockSpec((1, NSA_KV_GROUPS, TOK, NSA_REP * LANES), lambda b, i: (b, 0, i, 0)),
            pl.BlockSpec((1, nbp, LANES), lambda b, i: (b, 0, 0)),
            pl.BlockSpec((1, LANES, nbp), lambda b, i: (b, 0, 0)),
        ],
        out_specs=[
            pl.BlockSpec((1, TOK, NSA_Q_WIDTH), lambda b, i: (b, i, 0)),
            pl.BlockSpec((1, NSA_KV_GROUPS, 1, nsel, TOK), lambda b, i: (b, 0, i, 0, 0)),
        ],
        out_shape=[
            jax.ShapeDtypeStruct((bsz, s, NSA_Q_WIDTH), F32),
            jax.ShapeDtypeStruct((bsz, NSA_KV_GROUPS, nt, nsel, TOK), F32),
        ],
        scratch_shapes=[pltpu.VMEM((nbp + 8, TOK), F32)],
        compiler_params=_cparams(("parallel", "parallel")),
        name="nsa_cmp_topk",
    )(qp, kc, vct)


def _window_kernel(qp_ref, k_ref, vt_ref, o_ref):
    qi = pl.program_id(1)
    ncol = NSA_REP * TOK
    wk = WIN_TILES * TOK

    def run(first_tile, interior):
        kwin = k_ref[0, pl.ds(pl.multiple_of(first_tile * TOK, TOK), wk), :]
        outs = []
        for g in range(NSA_KV_GROUPS):
            s_t = _dot_nt(kwin, _q_rows(qp_ref[0, g]))
            if interior:
                row = lax.broadcasted_iota(jnp.int32, (TOK, ncol), 0)
                col = lax.broadcasted_iota(jnp.int32, (TOK, ncol), 1) % TOK
                oldest = jnp.where(row > col, s_t[:TOK], NEG_INF)
                newest = jnp.where(row <= col, s_t[wk - TOK:], NEG_INF)
                s_t = jnp.concatenate([oldest, s_t[TOK:wk - TOK], newest], axis=0)
            else:
                key = first_tile * TOK + lax.broadcasted_iota(jnp.int32, (wk, ncol), 0)
                t_io = qi * TOK + lax.broadcasted_iota(jnp.int32, (wk, ncol), 1) % TOK
                s_t = jnp.where((key <= t_io) & (t_io - key < NSA_WINDOW), s_t, NEG_INF)
            m = jnp.max(s_t, axis=0, keepdims=True)
            p = jnp.exp2(s_t - m)
            l = jnp.sum(p, axis=0, keepdims=True)
            vt = jnp.concatenate([vt_ref[0, g, first_tile + k] for k in range(WIN_TILES)], axis=1)
            outs.append(_heads_to_token_major(_dot(vt, p.astype(BF16)) / l))
        o_ref[0] = jnp.concatenate(outs, axis=1)

    @pl.when(qi >= WIN_TILES - 1)
    def _():
        run(qi - (WIN_TILES - 1), True)

    @pl.when(qi < WIN_TILES - 1)
    def _():
        run(0, False)


def _window(qp, k, vt, bsz, s):
    nt = s // TOK
    return pl.pallas_call(
        _window_kernel,
        grid=(bsz, nt),
        in_specs=[
            pl.BlockSpec((1, NSA_KV_GROUPS, TOK, NSA_REP * LANES), lambda b, i: (b, 0, i, 0)),
            pl.BlockSpec((1, s, LANES), lambda b, i: (b, 0, 0)),
            pl.BlockSpec((1, NSA_KV_GROUPS, nt, NSA_HEAD_DIM, TOK), lambda b, i: (b, 0, 0, 0, 0)),
        ],
        out_specs=pl.BlockSpec((1, TOK, NSA_Q_WIDTH), lambda b, i: (b, i, 0)),
        out_shape=jax.ShapeDtypeStruct((bsz, s, NSA_Q_WIDTH), F32),
        compiler_params=_cparams(("parallel", "parallel")),
        name="nsa_window",
    )(qp, k, vt)


def _selected_kernel(list_ref, cnt_ref, qp_ref, k_ref, vt_ref, bias_ref, o_ref, m_ref, acc_ref,
                     s_ref, p_ref, a_ref, qa_ref, *, nt, nsel):
    b = pl.program_id(0)
    qi = pl.program_id(1)
    ncol = NSA_REP * TOK
    blk_per_tile = SEL_TILE // NSA_SEL_BLOCK
    n_kt = nt // (SEL_TILE // TOK)
    n_q = nt // SEL_Q
    units = [(g, h) for g in range(NSA_KV_GROUPS) for h in range(SEL_Q)]
    m_ref[...] = jnp.full(m_ref.shape, NEG_INF, F32)
    acc_ref[...] = jnp.zeros(acc_ref.shape, F32)
    p_ref[...] = jnp.zeros(p_ref.shape, BF16)
    a_ref[...] = jnp.ones(a_ref.shape, F32)
    base = (b * n_q + qi) * n_kt
    count = cnt_ref[b * n_q + qi]
    n_diag = (qi * SEL_Q * TOK) // SEL_TILE

    lane = lax.broadcasted_iota(jnp.int32, (TOK, LANES), 1)
    n_var = qa_ref.shape[0]
    for g, h in units:
        bias_g = bias_ref[0, g, h]
        if nsel % LANES:
            bias_g = jnp.concatenate([bias_g, jnp.zeros((LANES - nsel % LANES, TOK), F32)], axis=0)
        own = (lane // NSA_HEAD_DIM) == g
        for v in range(n_var):
            src = bias_g[(v // 2) * LANES:(v // 2 + 1) * LANES, :].T
            if v % 2 != 1 - g:
                src = pltpu.roll(src, NSA_HEAD_DIM, 1)
            src = src.astype(BF16)
            for r in range(NSA_REP):
                q_r = qp_ref[0, g, h * TOK:(h + 1) * TOK, r * LANES:(r + 1) * LANES]
                qa_ref[v, g, h, r * TOK:(r + 1) * TOK, :] = jnp.where(own, q_r, src)

    def tile_at(i):
        return jnp.where(i < count, list_ref[base + jnp.minimum(i, n_kt - 1)], n_diag)

    def scores(j):
        v = (j * blk_per_tile) // NSA_HEAD_DIM
        for g in range(NSA_KV_GROUPS):
            kb = k_ref[0, g, pl.ds(pl.multiple_of(j * SEL_TILE, SEL_TILE), SEL_TILE), :]
            for h in range(SEL_Q):
                s_ref[g, h] = _dot_nt(kb, qa_ref[v, g, h])

    def values(j):
        for g, h in units:
            acc_ref[g, h] = acc_ref[g, h] * a_ref[g, h] + _dot(vt_ref[0, g, j], p_ref[g, h])

    def softmax(j, causal):
        for g, h in units:
            s_t = s_ref[g, h]
            if causal:
                t_io = ((qi * SEL_Q + h) * TOK
                        + lax.broadcasted_iota(jnp.int32, (SEL_TILE, ncol), 1) % TOK)
                key_pos = j * SEL_TILE + lax.broadcasted_iota(jnp.int32, (SEL_TILE, ncol), 0)
                s_t = jnp.where(key_pos <= t_io, s_t, NEG_INF)
            m_old = m_ref[g, h]
            m_new = jnp.maximum(m_old, jnp.max(s_t, axis=0, keepdims=True))
            alpha = jnp.exp2(m_old - m_new)
            p = jnp.exp2(s_t - jnp.maximum(m_new, MASK_FLOOR))
            m_ref[g, h] = m_new
            p_ref[g, h] = p.astype(BF16)
            a_ref[g, h] = alpha

    scores(tile_at(0))

    def body(i, c):
        values(tile_at(jnp.maximum(i - 1, 0)))
        softmax(tile_at(i), False)
        scores(tile_at(i + 1))
        return c

    lax.fori_loop(0, count, body, 0)
    values(tile_at(jnp.maximum(count - 1, 0)))
    softmax(n_diag, True)
    values(n_diag)
    for h in range(SEL_Q):
        outs = []
        for g in range(NSA_KV_GROUPS):
            acc = acc_ref[g, h]
            outs.append(_heads_to_token_major(acc[:NSA_HEAD_DIM] / acc[NSA_HEAD_DIM:NSA_HEAD_DIM + 1]))
        o_ref[0, h * TOK:(h + 1) * TOK, :] = jnp.concatenate(outs, axis=1)


def _selected(qp, k, vt, bias, tile_list, tile_count, bsz, s):
    nt = s // TOK
    nsel = s // NSA_SEL_BLOCK
    ncol = NSA_REP * TOK
    unit = (NSA_KV_GROUPS, SEL_Q)
    grid_spec = pltpu.PrefetchScalarGridSpec(
        num_scalar_prefetch=2,
        grid=(bsz, nt // SEL_Q),
        in_specs=[
            pl.BlockSpec((1, NSA_KV_GROUPS, SEL_Q * TOK, NSA_REP * LANES), lambda b, i, tl, tc: (b, 0, i, 0)),
            pl.BlockSpec((1, NSA_KV_GROUPS, s, LANES), lambda b, i, tl, tc: (b, 0, 0, 0)),
            pl.BlockSpec((1, NSA_KV_GROUPS, s // SEL_TILE, SEL_V_ROWS, SEL_TILE),
                         lambda b, i, tl, tc: (b, 0, 0, 0, 0)),
            pl.BlockSpec((1, NSA_KV_GROUPS, SEL_Q, nsel, TOK), lambda b, i, tl, tc: (b, 0, i, 0, 0)),
        ],
        out_specs=pl.BlockSpec((1, SEL_Q * TOK, NSA_Q_WIDTH), lambda b, i, tl, tc: (b, i, 0)),
        scratch_shapes=[pltpu.VMEM(unit + (1, ncol), F32),
                        pltpu.VMEM(unit + (SEL_V_ROWS, ncol), F32),
                        pltpu.VMEM(unit + (SEL_TILE, ncol), F32),
                        pltpu.VMEM(unit + (SEL_TILE, ncol), BF16),
                        pltpu.VMEM(unit + (1, ncol), F32),
                        pltpu.VMEM((max(nsel // NSA_HEAD_DIM, 1),) + unit + (ncol, LANES), BF16)],
    )
    return pl.pallas_call(
        functools.partial(_selected_kernel, nt=nt, nsel=nsel),
        grid_spec=grid_spec,
        out_shape=jax.ShapeDtypeStruct((bsz, s, NSA_Q_WIDTH), F32),
        compiler_params=_cparams(("parallel", "parallel")),
        name="nsa_selected",
    )(tile_list, tile_count, qp, k, vt, bias)


def _active_tiles(bias, bsz, s):
    n_q = s // (SEL_Q * TOK)
    n_kt = s // SEL_TILE
    rows = SEL_TILE // NSA_SEL_BLOCK * TOK
    act = bias.reshape(bsz, NSA_KV_GROUPS, n_q, SEL_Q, n_kt, rows).max(axis=(1, 3, 5)) > -1.0
    n_diag = (jnp.arange(n_q) * SEL_Q * TOK) // SEL_TILE
    act = act & (jnp.arange(n_kt)[None, None, :] < n_diag[None, :, None])
    order = jnp.argsort(jnp.where(act, 0, 1), axis=-1, stable=True).astype(jnp.int32)
    return order.reshape(-1), jnp.sum(act, axis=-1, dtype=jnp.int32).reshape(-1)


def _ssd_kernel(z_ref, xbc_ref, dt_ref, cw_ref, cb_ref, dtb_ref, alog_ref, dskip_ref, nw_ref, e8_ref,
                tri_ref, y_ref, ext_ref, st_ref):
    @pl.when(pl.program_id(1) == 0)
    def _():
        ext_ref[pl.ds(0, 8), :] = jnp.zeros((8, SSM_CONV_DIM), F32)
        st_ref[...] = jnp.zeros_like(st_ref)

    xb = xbc_ref[...]
    ext_ref[pl.ds(8, TOK), :] = xb
    conv = cb_ref[...] + xb * cw_ref[SSM_CONV - 1:SSM_CONV, :]
    for k in range(SSM_CONV - 1):
        conv = conv + ext_ref[pl.ds(8 - (SSM_CONV - 1) + k, TOK), :] * cw_ref[k:k + 1, :]
    ext_ref[pl.ds(0, 8), :] = xb[TOK - 8:TOK, :]
    u = _silu(conv)
    xs = u[:, :SSM_INNER]
    gn = SSM_GROUPS * SSM_STATE
    bm = u[:, SSM_INNER:SSM_INNER + gn]
    cm = u[:, SSM_INNER + gn:]
    dt = jax.nn.softplus(dt_ref[...] + dtb_ref[...])
    da = dt * (-jnp.exp(alog_ref[...]))
    tri = tri_ref[...]
    e8 = e8_ref[...]
    cum = _dot_sel_l(tri, da)
    cum_t = cum.T
    cum_e = _dot_sel_r(cum, e8)
    dt_e = _dot_sel_r(dt, e8)
    xdt = xs * dt_e
    lane = lax.broadcasted_iota(jnp.int32, (TOK, LANES), 1)
    t_io = lax.broadcasted_iota(jnp.int32, (TOK, TOK), 0)
    s_io = lax.broadcasted_iota(jnp.int32, (TOK, TOK), 1)
    causal = s_io <= t_io
    hg = SSM_HEADS // SSM_GROUPS
    y_parts = []
    cbs = []
    for g in range(SSM_GROUPS):
        in_g = (lane // SSM_STATE) == g
        cbs.append(_dot_nt(jnp.where(in_g, cm, 0.0).astype(BF16), bm.astype(BF16)))
    for c in range(SSM_HEADS // 2):
        acc = jnp.zeros((TOK, LANES), F32)
        xpair = xdt[:, c * LANES:(c + 1) * LANES]
        for hh in range(2):
            h = 2 * c + hh
            seg = cum[:, h:h + 1] - cum_t[h:h + 1, :]
            decay = jnp.where(causal, jnp.exp(jnp.where(causal, seg, 0.0)), 0.0)
            mat = (cbs[h // hg] * decay).astype(BF16)
            xh = jnp.where((lane // SSM_HEAD_DIM) == hh, xpair, 0.0).astype(BF16)
            acc = acc + _dot(mat, xh)
        y_parts.append(acc)
    y_diag = jnp.concatenate(y_parts, axis=1)
    prev = st_ref[...]
    y_off = _dot(cm.astype(BF16), prev.astype(BF16)) * jnp.exp(cum_e)
    cum_last = cum_e[TOK - 1:TOK, :]
    xdec = (xdt * jnp.exp(cum_last - cum_e)).astype(BF16)
    new = _dot(bm.T.astype(BF16), xdec)
    row_g = lax.broadcasted_iota(jnp.int32, (LANES, SSM_INNER), 0) // SSM_STATE
    col_g = lax.broadcasted_iota(jnp.int32, (LANES, SSM_INNER), 1) // (hg * SSM_HEAD_DIM)
    st_ref[...] = prev * jnp.exp(cum_last) + jnp.where(row_g == col_g, new, 0.0)
    y = (y_diag + y_off + xs * dskip_ref[...]) * _silu(z_ref[...])
    gw = SSM_INNER // SSM_GROUPS
    outs = []
    for g in range(SSM_GROUPS):
        yg = y[:, g * gw:(g + 1) * gw]
        outs.append(_rms(yg, nw_ref[:, g * gw:(g + 1) * gw]))
    y_ref[...] = jnp.concatenate(outs, axis=1)


def _ssd(proj, cw, cb, dtb, alog, dskip, nw, e8, tri, bsz, s):
    nt = s // TOK

    def col(c0, w):
        return pl.BlockSpec((TOK, w), lambda b, i: (b * nt + i, c0 // w))

    def full(shape):
        return pl.BlockSpec(shape, lambda b, i: (0,) * len(shape))

    return pl.pallas_call(
        _ssd_kernel,
        grid=(bsz, nt),
        in_specs=[col(COL_Z, SSM_INNER), col(COL_XBC, SSM_CONV_DIM), col(COL_DT, LANES),
                  full((SSM_CONV, SSM_CONV_DIM)), full((1, SSM_CONV_DIM)), full((1, LANES)), full((1, LANES)),
                  full((1, SSM_INNER)), full((1, SSM_INNER)), full((LANES, SSM_INNER)), full((TOK, TOK))],
        out_specs=pl.BlockSpec((TOK, SSM_INNER), lambda b, i: (b * nt + i, 0)),
        out_shape=jax.ShapeDtypeStruct((bsz * s, SSM_INNER), F32),
        scratch_shapes=[pltpu.VMEM((TOK + 8, SSM_CONV_DIM), F32), pltpu.VMEM((LANES, SSM_INNER), F32)],
        compiler_params=_cparams(("parallel", "arbitrary")),
        name="ssd",
    )(proj, proj, proj, cw, cb, dtb, alog, dskip, nw, e8, tri)


def _hgrn2_kernel(q_ref, f_ref, i_ref, g_ref, lb_ref, nw_ref, tri_ref, y_ref, st_ref, kp_ref, bp_ref, vp_ref):
    @pl.when(pl.program_id(1) == 0)
    def _():
        st_ref[...] = jnp.zeros_like(st_ref)
        kp_ref[...] = jnp.zeros_like(kp_ref)
        bp_ref[...] = jnp.zeros_like(bp_ref)
        vp_ref[...] = jnp.zeros_like(vp_ref)

    nsub = TOK // HG_SUB
    row = lax.broadcasted_iota(jnp.int32, (TOK, HG_KEY_DIM), 0)
    sub_id = row // HG_SUB
    in_sub = row % HG_SUB
    tri = tri_ref[...]
    outs = []
    for h in range(HG_HEADS):
        sl = slice(h * HG_KEY_DIM, (h + 1) * HG_KEY_DIM)
        lb = lb_ref[:, sl]
        fr = f_ref[:, sl]
        q = _silu(q_ref[:, sl]) * (HG_KEY_DIM ** -0.5)
        log_f = jnp.log(lb + (1.0 - lb) * jax.nn.sigmoid(fr))
        k = (1.0 - lb) * jax.nn.sigmoid(-fr)
        v = i_ref[:, sl]
        bcum = _dot_sel_l(tri, log_f)
        st = st_ref[h]
        o = _dot_nt((q * jnp.exp(bcum)).astype(BF16), st.astype(BF16))
        r_rows = [bcum[i * HG_SUB - 1:i * HG_SUB, :] for i in range(1, nsub)]
        r_loc = jnp.zeros_like(bcum)
        for i in range(1, nsub):
            r_loc = jnp.where(sub_id == i, r_rows[i - 1], r_loc)
        q_loc = q * jnp.exp(bcum - r_loc)
        q_aug = jnp.concatenate(
            [jnp.where(sub_id == i, q_loc, 0.0).astype(BF16) for i in range(1, nsub)], axis=1)
        k_aug = jnp.concatenate(
            [jnp.where(sub_id < i, k * jnp.exp(jnp.where(sub_id < i, r_rows[i - 1] - bcum, 0.0)), 0.0).astype(BF16)
             for i in range(1, nsub)], axis=1)
        att = _dot_nt(q_aug, k_aug)
        o = o + _dot(att.astype(BF16), v.astype(BF16))
        kp_ref[pl.ds(HG_SUB, TOK), :] = k
        bp_ref[pl.ds(HG_SUB, TOK), :] = bcum
        vp_ref[pl.ds(HG_SUB, TOK), :] = v
        o = o + jnp.sum(q * k, axis=1, keepdims=True) * v
        for d in range(1, HG_SUB):
            ks = kp_ref[pl.ds(HG_SUB - d, TOK), :]
            bs = bp_ref[pl.ds(HG_SUB - d, TOK), :]
            vs = vp_ref[pl.ds(HG_SUB - d, TOK), :]
            ok = in_sub >= d
            w = jnp.sum(q * ks * jnp.exp(jnp.where(ok, bcum - bs, 0.0)), axis=1, keepdims=True)
            o = o + jnp.where(ok, w * vs, 0.0)
        b_last = bcum[TOK - 1:TOK, :]
        kdec = (k * jnp.exp(b_last - bcum)).astype(BF16)
        st_ref[h] = st * jnp.exp(b_last) + _dot(v.T.astype(BF16), kdec)
        outs.append(_rms(o, nw_ref[...]) * _silu(g_ref[:, sl]))
    y_ref[...] = jnp.concatenate(outs, axis=1)


def _hgrn2(proj, lb, nw, tri, bsz, s):
    nt = s // TOK

    def col(c0):
        return pl.BlockSpec((TOK, HG_WIDTH), lambda b, i: (b * nt + i, c0 // HG_WIDTH))

    def full(shape):
        return pl.BlockSpec(shape, lambda b, i: (0,) * len(shape))

    pad = pltpu.VMEM((TOK + HG_SUB, HG_KEY_DIM), F32)
    return pl.pallas_call(
        _hgrn2_kernel,
        grid=(bsz, nt),
        in_specs=[col(COL_HQ), col(COL_HF), col(COL_HI), col(COL_HG), full((1, HG_WIDTH)),
                  full((1, HG_VAL_DIM)), full((TOK, TOK))],
        out_specs=pl.BlockSpec((TOK, HG_WIDTH), lambda b, i: (b * nt + i, 0)),
        out_shape=jax.ShapeDtypeStruct((bsz * s, HG_WIDTH), F32),
        scratch_shapes=[pltpu.VMEM((HG_HEADS, HG_VAL_DIM, HG_KEY_DIM), F32), pad, pad, pad],
        compiler_params=_cparams(("parallel", "arbitrary")),
        name="hgrn2",
    )(proj, proj, proj, proj, lb, nw, tri)


def _merge_kernel(ga_ref, gb_ref, gc_ref, ng_ref, oc_ref, os_ref, ow_ref, yb_ref, yc_ref, x_ref,
                  eg_ref, wa_ref, wb_ref, wc_ref, wo_ref, o_ref):
    sg = jax.nn.sigmoid(ng_ref[...])
    nsa = (_dot_sel_r(sg, eg_ref[0]) * oc_ref[...] + _dot_sel_r(sg, eg_ref[1]) * os_ref[...]
           + _dot_sel_r(sg, eg_ref[2]) * ow_ref[...])
    ya = _dot(nsa.astype(BF16), wa_ref[...])
    yb = _dot(yb_ref[...].astype(BF16), wb_ref[...])
    yc = _dot(yc_ref[...].astype(BF16), wc_ref[...])
    merged = (jax.nn.sigmoid(ga_ref[...]) * ya + jax.nn.sigmoid(gb_ref[...]) * yb
              + jax.nn.sigmoid(gc_ref[...]) * yc)
    o_ref[...] = x_ref[...] + _dot(merged.astype(BF16), wo_ref[...])


def _merge(proj, oc, os_, ow, yb, yc, x2, eg, wa, wb, wc, wo):
    t = x2.shape[0]
    tm = min(512, t)

    def col(c0, w):
        return pl.BlockSpec((tm, w), lambda i: (i, c0 // w))

    def full(shape):
        return pl.BlockSpec(shape, lambda i: (0,) * len(shape))

    row512 = pl.BlockSpec((tm, 512), lambda i: (i, 0))
    return pl.pallas_call(
        _merge_kernel,
        grid=(t // tm,),
        in_specs=[col(COL_GA, D_MODEL), col(COL_GB, D_MODEL), col(COL_GC, D_MODEL), col(COL_NG, LANES),
                  row512, row512, row512, row512, row512,
                  pl.BlockSpec((tm, D_MODEL), lambda i: (i, 0)),
                  full((3, LANES, NSA_Q_WIDTH)), full((NSA_Q_WIDTH, D_MODEL)), full((SSM_INNER, D_MODEL)),
                  full((HG_WIDTH, D_MODEL)), full((D_MODEL, D_MODEL))],
        out_specs=pl.BlockSpec((tm, D_MODEL), lambda i: (i, 0)),
        out_shape=jax.ShapeDtypeStruct((t, D_MODEL), F32),
        compiler_params=_cparams(("parallel",)),
        name="merge",
    )(proj, proj, proj, proj, oc, os_, ow, yb, yc, x2, eg, wa, wb, wc, wo)


def _xa_mem_kernel(mem_ref, nm_ref, wk_ref, wv_ref, kn_ref, k_ref, v_ref):
    m = _rms(mem_ref[0], nm_ref[...]).astype(BF16)
    k = _dot(m, wk_ref[...])
    ks = [_rms(k[:, h * XA_HEAD_DIM:(h + 1) * XA_HEAD_DIM], kn_ref[...]) for h in range(XA_HEADS)]
    k_ref[0] = jnp.concatenate(ks, axis=1).astype(BF16)
    v_ref[0] = _dot(m, wv_ref[...]).astype(BF16)


def _xa_mem(mem, nm, wk, wv, kn):
    bsz, ml, _ = mem.shape

    def full(shape):
        return pl.BlockSpec(shape, lambda b: (0,) * len(shape))

    return pl.pallas_call(
        _xa_mem_kernel,
        grid=(bsz,),
        in_specs=[pl.BlockSpec((1, ml, D_MODEL), lambda b: (b, 0, 0)), full((1, D_MODEL)),
                  full((D_MODEL, XA_WIDTH)), full((D_MODEL, XA_WIDTH)), full((1, XA_HEAD_DIM))],
        out_specs=[pl.BlockSpec((1, ml, XA_WIDTH), lambda b: (b, 0, 0)),
                   pl.BlockSpec((1, ml, XA_WIDTH), lambda b: (b, 0, 0))],
        out_shape=[jax.ShapeDtypeStruct((bsz, ml, XA_WIDTH), BF16),
                   jax.ShapeDtypeStruct((bsz, ml, XA_WIDTH), BF16)],
        compiler_params=_cparams(("parallel",)),
        name="xa_mem",
    )(mem, nm, wk, wv, kn)


def _xa_kernel(x_ref, nx_ref, wq_ref, qn_ref, k_ref, v_ref, wo_ref, o_ref):
    x = x_ref[...]
    h = _rms(x, nx_ref[...]).astype(BF16)
    q = _dot(h, wq_ref[...])
    outs = []
    for hd in range(XA_HEADS):
        sl = slice(hd * XA_HEAD_DIM, (hd + 1) * XA_HEAD_DIM)
        qh = _rms(q[:, sl], qn_ref[...]).astype(BF16)
        sc = _dot_nt(qh, k_ref[0, :, sl]) * (XA_HEAD_DIM ** -0.5)
        sc = sc - jnp.max(sc, axis=-1, keepdims=True)
        p = jnp.exp(sc)
        p = p / jnp.sum(p, axis=-1, keepdims=True)
        outs.append(_dot(p.astype(BF16), v_ref[0, :, sl]))
    o = jnp.concatenate(outs, axis=1).astype(BF16)
    o_ref[...] = x + _dot(o, wo_ref[...])


def _xa(x2, nx, wq, qn, k, v, wo, s):
    t = x2.shape[0]
    tm = min(512, s)
    ml = k.shape[1]
    per_b = s // tm

    def full(shape):
        return pl.BlockSpec(shape, lambda i: (0,) * len(shape))

    return pl.pallas_call(
        _xa_kernel,
        grid=(t // tm,),
        in_specs=[pl.BlockSpec((tm, D_MODEL), lambda i: (i, 0)), full((1, D_MODEL)), full((D_MODEL, XA_WIDTH)),
                  full((1, XA_HEAD_DIM)),
                  pl.BlockSpec((1, ml, XA_WIDTH), lambda i: (i // per_b, 0, 0)),
                  pl.BlockSpec((1, ml, XA_WIDTH), lambda i: (i // per_b, 0, 0)),
                  full((XA_WIDTH, D_MODEL))],
        out_specs=pl.BlockSpec((tm, D_MODEL), lambda i: (i, 0)),
        out_shape=jax.ShapeDtypeStruct((t, D_MODEL), F32),
        compiler_params=_cparams(("parallel",)),
        name="cross_attn",
    )(x2, nx, wq, qn, k, v, wo)


def _ffn_kernel(x_ref, nw_ref, wg_ref, wu_ref, wd_ref, o_ref, h_ref, acc_ref):
    j = pl.program_id(1)

    @pl.when(j == 0)
    def _():
        h_ref[...] = _rms(x_ref[...], nw_ref[...]).astype(BF16)
        acc_ref[...] = jnp.zeros_like(acc_ref)

    h = h_ref[...]
    a = _silu(_dot(h, wg_ref[...])) * _dot(h, wu_ref[...])
    acc_ref[...] += _dot(a.astype(BF16), wd_ref[...])

    @pl.when(j == pl.num_programs(1) - 1)
    def _():
        o_ref[...] = x_ref[...] + acc_ref[...]


def _ffn(x2, nw, wg, wu, wd):
    t = x2.shape[0]
    tm = min(1024, t)
    th = FFN_HIDDEN // 2
    return pl.pallas_call(
        _ffn_kernel,
        grid=(t // tm, FFN_HIDDEN // th),
        in_specs=[pl.BlockSpec((tm, D_MODEL), lambda i, j: (i, 0)),
                  pl.BlockSpec((1, D_MODEL), lambda i, j: (0, 0)),
                  pl.BlockSpec((D_MODEL, th), lambda i, j: (0, j)),
                  pl.BlockSpec((D_MODEL, th), lambda i, j: (0, j)),
                  pl.BlockSpec((th, D_MODEL), lambda i, j: (j, 0))],
        out_specs=pl.BlockSpec((tm, D_MODEL), lambda i, j: (i, 0)),
        out_shape=jax.ShapeDtypeStruct((t, D_MODEL), F32),
        scratch_shapes=[pltpu.VMEM((tm, D_MODEL), BF16), pltpu.VMEM((tm, D_MODEL), F32)],
        compiler_params=_cparams(("parallel", "arbitrary")),
        name="ffn",
    )(x2, nw, wg, wu, wd)


def _pack_w_in(w):
    parts = {}
    off = 0
    names = ("q", "kc", "vc", "ks", "vs", "kw", "vw", "ng", "z", "xbc", "dt", "hq", "hf", "hi", "hg",
             "ga", "gb", "gc")
    for name, size in zip(names, SPLIT_SIZES):
        parts[name] = w[:, off:off + size]
        off += size

    def padded(a, width):
        return jnp.pad(a, ((0, 0), (0, width - a.shape[1])))

    cols = [parts["ga"], parts["gb"], parts["gc"], parts["q"], parts["z"], parts["hq"], parts["hf"],
            parts["hi"], parts["hg"], parts["xbc"], parts["kc"], parts["vc"], parts["ks"], parts["vs"],
            parts["kw"], parts["vw"], padded(parts["ng"], LANES), padded(parts["dt"], LANES)]
    packed = jnp.concatenate(cols, axis=1)
    return padded(packed, PROJ_WIDTH).astype(BF16)


def _rope_tables(pos):
    half = NSA_HEAD_DIM // 2
    inv = 1.0 / (ROPE_THETA ** (jnp.arange(half, dtype=F32) / half))
    ang = pos.astype(F32)[:, None] * inv[None, :]
    reps = LANES // half
    return jnp.tile(jnp.cos(ang), (1, reps)), jnp.tile(jnp.sin(ang), (1, reps))


def _block_diag2(w):
    z = jnp.zeros_like(w)
    return jnp.concatenate([jnp.concatenate([w, z], axis=-1), jnp.concatenate([z, w], axis=-1)], axis=-2)


def kernel(x, mem, norm_mix, w_in, nsa_q_norm, nsa_k_norm, nsa_cmp_pos_k, nsa_cmp_pos_v, nsa_cmp_w1_k, nsa_cmp_w2_k, nsa_cmp_w1_v, nsa_cmp_w2_v, w_nsa_o, ssm_conv_w, ssm_conv_b, ssm_dt_bias, ssm_a_log, ssm_d, ssm_norm, w_ssm_o, hg_lb_logits, hg_norm, w_hg_o, w_out, norm_xa, norm_mem, xa_w_q, xa_w_k, xa_w_v, xa_q_norm, xa_k_norm, xa_w_o, norm_ffn, ffn_w_gate, ffn_w_up, ffn_w_down):
    bsz, s, d = x.shape
    depth = w_in.shape[0]
    assert d == D_MODEL and s % (8 * TOK) == 0 and s // TOK <= TOK
    t = bsz * s
    hd = NSA_HEAD_DIM

    cos_t, sin_t = _rope_tables(jnp.arange(s))
    nbp = s // NSA_CMP_STRIDE
    cos_c, sin_c = _rope_tables(jnp.arange(nbp) * NSA_CMP_STRIDE + NSA_CMP_BLOCK - 1)
    tri = jnp.tril(jnp.ones((TOK, TOK), F32)).astype(BF16)
    rows = np.arange(LANES)[:, None]
    cols = np.arange(NSA_Q_WIDTH)[None, :]
    e8 = jnp.asarray(rows == cols // SSM_HEAD_DIM, F32).astype(BF16)
    eg = jnp.stack([jnp.asarray(rows == 3 * (cols // hd) + c, F32) for c in range(3)]).astype(BF16)

    lb_sm = jax.nn.softmax(hg_lb_logits.astype(F32), axis=0)
    lb_all = jnp.cumsum(lb_sm, axis=0) - lb_sm[0:1]

    def pad_lanes(v):
        return jnp.pad(v, (0, LANES - v.shape[0]))[None, :]

    x2 = x.reshape(t, d)
    for l in range(depth):
        proj = _in_proj(x2, norm_mix[l][None, :], _pack_w_in(w_in[l]))

        qn2 = jnp.tile(nsa_q_norm[l], 2)[None, :]
        kn2 = jnp.tile(nsa_k_norm[l], 2)[None, :]
        qp, ksn, kwn, vst, vwt = _nsa_prep(proj, cos_t, sin_t, qn2, kn2, bsz, s)
        w1k = _block_diag2(nsa_cmp_w1_k[l].reshape(NSA_CMP_BLOCK, hd, hd)).astype(BF16)
        w1v = _block_diag2(nsa_cmp_w1_v[l].reshape(NSA_CMP_BLOCK, hd, hd)).astype(BF16)
        kc, vct = _compress(proj, jnp.tile(nsa_cmp_pos_k[l], (1, 2)), jnp.tile(nsa_cmp_pos_v[l], (1, 2)),
                            w1k, _block_diag2(nsa_cmp_w2_k[l]).astype(BF16),
                            w1v, _block_diag2(nsa_cmp_w2_v[l]).astype(BF16), kn2, cos_c, sin_c, bsz, s)
        o_c, sel_bias = _cmp_topk(qp, kc, vct, bsz, s)
        tile_list, tile_count = _active_tiles(sel_bias, bsz, s)
        o_s = _selected(qp, ksn, vst, sel_bias, tile_list, tile_count, bsz, s)
        o_w = _window(qp, kwn, vwt, bsz, s)

        y_b = _ssd(proj, ssm_conv_w[l], ssm_conv_b[l][None, :], pad_lanes(ssm_dt_bias[l]),
                   pad_lanes(ssm_a_log[l]), jnp.repeat(ssm_d[l], SSM_HEAD_DIM)[None, :],
                   ssm_norm[l][None, :], e8, tri, bsz, s)
        y_c = _hgrn2(proj, lb_all[l][None, :], hg_norm[l][None, :], tri, bsz, s)

        x2 = _merge(proj, o_c.reshape(t, -1), o_s.reshape(t, -1), o_w.reshape(t, -1), y_b, y_c, x2, eg,
                    w_nsa_o[l].astype(BF16), w_ssm_o[l].astype(BF16), w_hg_o[l].astype(BF16),
                    w_out[l].astype(BF16))

        k_m, v_m = _xa_mem(mem, norm_mem[l][None, :], xa_w_k[l].astype(BF16), xa_w_v[l].astype(BF16),
                           xa_k_norm[l][None, :])
        x2 = _xa(x2, norm_xa[l][None, :], xa_w_q[l].astype(BF16), xa_q_norm[l][None, :], k_m, v_m,
                 xa_w_o[l].astype(BF16), s)
        x2 = _ffn(x2, norm_ffn[l][None, :], ffn_w_gate[l].astype(BF16), ffn_w_up[l].astype(BF16),
                  ffn_w_down[l].astype(BF16))
    return x2.reshape(bsz, s, d)
```

```python
import functools
import math

import numpy as np
import jax
import jax.numpy as jnp
from jax import lax
from jax.experimental import pallas as pl
from jax.experimental.pallas import tpu as pltpu

F32 = jnp.float32
BF16 = jnp.bfloat16

D_MODEL = 1024
NORM_EPS = 1e-6
ROPE_THETA = 10000.0
NEG_INF = -1e30
FORCE_SCORE = 1e9

NSA_HEADS = 8
NSA_KV_GROUPS = 2
NSA_REP = NSA_HEADS // NSA_KV_GROUPS
NSA_HEAD_DIM = 64
NSA_CMP_BLOCK = 32
NSA_CMP_STRIDE = 16
NSA_SEL_BLOCK = 64
NSA_TOPK = 16
N_FORCED = 3
NSA_WINDOW = 512
NSA_Q_WIDTH = NSA_HEADS * NSA_HEAD_DIM
NSA_KV_WIDTH = NSA_KV_GROUPS * NSA_HEAD_DIM

SSM_INNER = 512
SSM_HEAD_DIM = 64
SSM_HEADS = SSM_INNER // SSM_HEAD_DIM
SSM_GROUPS = 2
SSM_STATE = 64
SSM_CONV = 4
SSM_CONV_DIM = SSM_INNER + 2 * SSM_GROUPS * SSM_STATE

HG_HEADS = 4
HG_KEY_DIM = 128
HG_VAL_DIM = 128
HG_WIDTH = HG_HEADS * HG_KEY_DIM
HG_SUB = 16

XA_HEADS = 4
XA_HEAD_DIM = 128
XA_WIDTH = XA_HEADS * XA_HEAD_DIM

FFN_HIDDEN = -(-(8 * D_MODEL) // (3 * 256)) * 256

SPLIT_SIZES = (
    NSA_Q_WIDTH, NSA_KV_WIDTH, NSA_KV_WIDTH, NSA_KV_WIDTH, NSA_KV_WIDTH, NSA_KV_WIDTH, NSA_KV_WIDTH,
    3 * NSA_HEADS, SSM_INNER, SSM_CONV_DIM, SSM_HEADS, HG_WIDTH, HG_WIDTH, HG_HEADS * HG_VAL_DIM,
    HG_HEADS * HG_VAL_DIM, D_MODEL, D_MODEL, D_MODEL,
)

LANES = 128
TOK = 128
GATE_ROWS = 32
PREP_TILES = 4
SEL_Q = 4
SEL_TILE = 512
SEL_V_ROWS = NSA_HEAD_DIM + 16
WIN_TILES = NSA_WINDOW // TOK + 1
LOG2_E = math.log2(math.e)
MASK_FLOOR = -1e20
VMEM_LIMIT = 56 * 1024 * 1024

PROJ_WIDTH = 8192
COL_GA, COL_GB, COL_GC = 0, 1024, 2048
COL_Q = 3072
COL_Z = 3584
COL_HQ, COL_HF, COL_HI, COL_HG = 4096, 4608, 5120, 5632
COL_XBC = 6144
COL_KC, COL_VC, COL_KS, COL_VS, COL_KW, COL_VW = 6912, 7040, 7168, 7296, 7424, 7552
COL_NG = 7680
COL_DT = 7808


def _cparams(sem):
    return pltpu.CompilerParams(dimension_semantics=sem, vmem_limit_bytes=VMEM_LIMIT)


def _dot(a, b):
    return jnp.dot(a, b, preferred_element_type=F32)


def _dot_nt(a, b):
    return lax.dot_general(a, b, (((1,), (1,)), ((), ())), preferred_element_type=F32)


def _split3(a):
    hi = a.astype(BF16)
    r1 = a - hi.astype(F32)
    mid = r1.astype(BF16)
    lo = (r1 - mid.astype(F32)).astype(BF16)
    return hi, mid, lo


def _dot_sel_r(a, sel):
    hi, mid, lo = _split3(a)
    return _dot(hi, sel) + _dot(mid, sel) + _dot(lo, sel)


def _dot_sel_l(sel, a):
    hi, mid, lo = _split3(a)
    return _dot(sel, hi) + _dot(sel, mid) + _dot(sel, lo)


def _silu(x):
    return x * jax.nn.sigmoid(x)


def _rms(x, w):
    return x * lax.rsqrt(jnp.mean(x * x, axis=-1, keepdims=True) + NORM_EPS) * w


def _inproj_kernel(x_ref, nw_ref, w_ref, o_ref, h_ref):
    @pl.when(pl.program_id(1) == 0)
    def _():
        h_ref[...] = _rms(x_ref[...], nw_ref[...]).astype(BF16)

    o_ref[...] = _dot(h_ref[...], w_ref[...])


def _in_proj(x2, norm_w, w_packed):
    t = x2.shape[0]
    tm = min(2048, t)
    tn = 1024
    return pl.pallas_call(
        _inproj_kernel,
        grid=(t // tm, PROJ_WIDTH // tn),
        in_specs=[
            pl.BlockSpec((tm, D_MODEL), lambda i, j: (i, 0)),
            pl.BlockSpec((1, D_MODEL), lambda i, j: (0, 0)),
            pl.BlockSpec((D_MODEL, tn), lambda i, j: (0, j)),
        ],
        out_specs=pl.BlockSpec((tm, tn), lambda i, j: (i, j)),
        out_shape=jax.ShapeDtypeStruct((t, PROJ_WIDTH), F32),
        scratch_shapes=[pltpu.VMEM((tm, D_MODEL), BF16)],
        compiler_params=_cparams(("parallel", "arbitrary")),
        name="in_proj",
    )(x2, norm_w, w_packed)


def _norm_rope_pair(x, w, cos, sin):
    lane = lax.broadcasted_iota(jnp.int32, x.shape, 1)
    lo = lane < NSA_HEAD_DIM
    sq = x * x
    s_lo = jnp.sum(jnp.where(lo, sq, 0.0), axis=1, keepdims=True)
    s_hi = jnp.sum(jnp.where(lo, 0.0, sq), axis=1, keepdims=True)
    ms = jnp.where(lo, s_lo, s_hi) * (1.0 / NSA_HEAD_DIM)
    y = x * lax.rsqrt(ms + NORM_EPS) * w
    half = NSA_HEAD_DIM // 2
    first = (lane % NSA_HEAD_DIM) < half
    rot = jnp.where(first, -pltpu.roll(y, LANES - half, 1), pltpu.roll(y, half, 1))
    return y * cos + rot * sin


def _nsa_prep_kernel(q_ref, ks_ref, vs_ref, kw_ref, vw_ref, ng_ref, cos_ref, sin_ref, qn_ref, kn_ref,
                     qp_ref, ksn_ref, kwn_ref, vst_ref, vwt_ref, gt_ref):
    lane = lax.broadcasted_iota(jnp.int32, (TOK, LANES), 1)
    row = lax.broadcasted_iota(jnp.int32, (TOK, LANES), 0)
    scale = NSA_HEAD_DIM ** -0.5 * LOG2_E
    ones_rows = (lax.broadcasted_iota(jnp.int32, (SEL_V_ROWS - NSA_HEAD_DIM, TOK), 0) == 0).astype(BF16)
    for u in range(PREP_TILES):
        rows = slice(u * TOK, (u + 1) * TOK)
        cos = cos_ref[rows, :]
        sin = sin_ref[rows, :]
        for c in range(NSA_HEADS // 2):
            y = _norm_rope_pair(q_ref[rows, c * LANES:(c + 1) * LANES], qn_ref[...], cos, sin) * scale
            y_sw = pltpu.roll(y, NSA_HEAD_DIM, 1)
            for hh in range(2):
                h = 2 * c + hh
                g, r = divmod(h, NSA_REP)
                src = y if hh == g else y_sw
                keep = (lane // NSA_HEAD_DIM) == g
                qp_ref[0, g, rows, r * LANES:(r + 1) * LANES] = jnp.where(keep, src, 0.0).astype(BF16)
        ks_n = _norm_rope_pair(ks_ref[rows, :], kn_ref[...], cos, sin)
        tile = pl.program_id(1) * PREP_TILES + u
        blk = (tile * (TOK // NSA_SEL_BLOCK) + row // NSA_SEL_BLOCK) % NSA_HEAD_DIM
        onehot = ((lane % NSA_HEAD_DIM) == blk).astype(F32)
        for g in range(NSA_KV_GROUPS):
            ksn_ref[0, g, rows, :] = jnp.where((lane // NSA_HEAD_DIM) == g, ks_n, onehot).astype(BF16)
        kwn_ref[0, rows, :] = _norm_rope_pair(kw_ref[rows, :], kn_ref[...], cos, sin).astype(BF16)
        vs_t = vs_ref[rows, :].T.astype(BF16)
        vw_t = vw_ref[rows, :].T.astype(BF16)
        for g in range(NSA_KV_GROUPS):
            vst_ref[0, g, 0, 0:NSA_HEAD_DIM, rows] = vs_t[g * NSA_HEAD_DIM:(g + 1) * NSA_HEAD_DIM, :]
            vst_ref[0, g, 0, NSA_HEAD_DIM:SEL_V_ROWS, rows] = ones_rows
            vwt_ref[0, g, u] = vw_t[g * NSA_HEAD_DIM:(g + 1) * NSA_HEAD_DIM, :]
        gt_ref[0, u] = jax.nn.sigmoid(ng_ref[rows, :]).T[0:GATE_ROWS, :]


def _nsa_prep(proj, cos_t, sin_t, qn2, kn2, bsz, s):
    tq = PREP_TILES * TOK
    n_step = s // tq

    def col(c0, w):
        return pl.BlockSpec((tq, w), lambda b, i: (b * n_step + i, c0 // w))

    tab = pl.BlockSpec((tq, LANES), lambda b, i: (i, 0))
    vec = pl.BlockSpec((1, LANES), lambda b, i: (0, 0))
    return pl.pallas_call(
        _nsa_prep_kernel,
        grid=(bsz, n_step),
        in_specs=[col(COL_Q, NSA_Q_WIDTH), col(COL_KS, LANES), col(COL_VS, LANES), col(COL_KW, LANES),
                  col(COL_VW, LANES), col(COL_NG, LANES), tab, tab, vec, vec],
        out_specs=[
            pl.BlockSpec((1, NSA_KV_GROUPS, tq, NSA_REP * LANES), lambda b, i: (b, 0, i, 0)),
            pl.BlockSpec((1, NSA_KV_GROUPS, tq, LANES), lambda b, i: (b, 0, i, 0)),
            pl.BlockSpec((1, tq, LANES), lambda b, i: (b, i, 0)),
            pl.BlockSpec((1, NSA_KV_GROUPS, 1, SEL_V_ROWS, SEL_TILE), lambda b, i: (b, 0, i, 0, 0)),
            pl.BlockSpec((1, NSA_KV_GROUPS, PREP_TILES, NSA_HEAD_DIM, TOK), lambda b, i: (b, 0, i, 0, 0)),
            pl.BlockSpec((1, PREP_TILES, GATE_ROWS, TOK), lambda b, i: (b, i, 0, 0)),
        ],
        out_shape=[
            jax.ShapeDtypeStruct((bsz, NSA_KV_GROUPS, s, NSA_REP * LANES), BF16),
            jax.ShapeDtypeStruct((bsz, NSA_KV_GROUPS, s, LANES), BF16),
            jax.ShapeDtypeStruct((bsz, s, LANES), BF16),
            jax.ShapeDtypeStruct((bsz, NSA_KV_GROUPS, s // SEL_TILE, SEL_V_ROWS, SEL_TILE), BF16),
            jax.ShapeDtypeStruct((bsz, NSA_KV_GROUPS, s // TOK, NSA_HEAD_DIM, TOK), BF16),
            jax.ShapeDtypeStruct((bsz, s // TOK, GATE_ROWS, TOK), F32),
        ],
        compiler_params=_cparams(("parallel", "parallel")),
        name="nsa_prep",
    )(proj, proj, proj, proj, proj, proj, cos_t, sin_t, qn2, kn2)


def _compress_kernel(kc_ref, vc_ref, posk_ref, posv_ref, w1k_ref, w2k_ref, w1v_ref, w2v_ref, kn_ref,
                     cos_ref, sin_ref, kco_ref, vct_ref, sh_ref, *, nbp):
    half_blk = NSA_CMP_BLOCK // 2

    def compress(x_ref, pos_ref, w1_ref, w2_ref):
        acc_a = jnp.zeros((nbp, LANES), F32)
        acc_b = jnp.zeros((nbp, LANES), F32)
        for l in range(half_blk):
            xl = x_ref[pl.ds(l, nbp, stride=NSA_CMP_STRIDE), :]
            acc_a = acc_a + _dot((xl + pos_ref[l:l + 1, :]).astype(BF16), w1_ref[l])
            acc_b = acc_b + _dot((xl + pos_ref[half_blk + l:half_blk + l + 1, :]).astype(BF16),
                                 w1_ref[half_blk + l])
        sh_ref[pl.ds(0, nbp), :] = acc_b
        sh_ref[pl.ds(nbp, 8), :] = jnp.zeros((8, LANES), F32)
        hid = acc_a + sh_ref[pl.ds(1, nbp), :]
        return _dot(_silu(hid).astype(BF16), w2_ref[...])

    kc = compress(kc_ref, posk_ref, w1k_ref, w2k_ref)
    kco_ref[0] = _norm_rope_pair(kc, kn_ref[...], cos_ref[...], sin_ref[...]).astype(BF16)
    vc = compress(vc_ref, posv_ref, w1v_ref, w2v_ref)
    vct_ref[0] = vc.T.astype(BF16)


def _compress(proj, posk, posv, w1k, w2k, w1v, w2v, kn2, cos_c, sin_c, bsz, s):
    nbp = s // NSA_CMP_STRIDE

    def full(shape):
        return pl.BlockSpec(shape, lambda b: (0,) * len(shape))

    return pl.pallas_call(
        functools.partial(_compress_kernel, nbp=nbp),
        grid=(bsz,),
        in_specs=[
            pl.BlockSpec((s, LANES), lambda b: (b, COL_KC // LANES)),
            pl.BlockSpec((s, LANES), lambda b: (b, COL_VC // LANES)),
            full((NSA_CMP_BLOCK, LANES)), full((NSA_CMP_BLOCK, LANES)),
            full((NSA_CMP_BLOCK, LANES, LANES)), full((LANES, LANES)),
            full((NSA_CMP_BLOCK, LANES, LANES)), full((LANES, LANES)),
            full((1, LANES)), full((nbp, LANES)), full((nbp, LANES)),
        ],
        out_specs=[
            pl.BlockSpec((1, nbp, LANES), lambda b: (b, 0, 0)),
            pl.BlockSpec((1, LANES, nbp), lambda b: (b, 0, 0)),
        ],
        out_shape=[
            jax.ShapeDtypeStruct((bsz, nbp, LANES), BF16),
            jax.ShapeDtypeStruct((bsz, LANES, nbp), BF16),
        ],
        scratch_shapes=[pltpu.VMEM((nbp + 8, LANES), F32)],
        compiler_params=_cparams(("parallel",)),
        name="nsa_compress",
    )(proj, proj, posk, posv, w1k, w2k, w1v, w2v, kn2, cos_c, sin_c)


def _q_rows(qp):
    return jnp.concatenate([qp[:, r * LANES:(r + 1) * LANES] for r in range(NSA_REP)], axis=0)


def _gate_row(gt, g, branch):
    rows = [(g * NSA_REP + r) * 3 + branch for r in range(NSA_REP)]
    return jnp.concatenate([gt[i:i + 1, :] for i in rows], axis=1)


def _heads_to_token_major(acc_g):
    outs = []
    for p in range(NSA_REP // 2):
        blk = jnp.concatenate([acc_g[:, (2 * p) * TOK:(2 * p + 1) * TOK],
                               acc_g[:, (2 * p + 1) * TOK:(2 * p + 2) * TOK]], axis=0)
        outs.append(blk.T)
    return jnp.concatenate(outs, axis=1)


def _cmp_topk_kernel(qp_ref, kc_ref, vct_ref, gt_ref, oc_ref, mask_ref, ps_ref, *, nbp, nsel):
    qi = pl.program_id(1)
    ncol = NSA_REP * TOK
    per = NSA_SEL_BLOCK // NSA_CMP_STRIDE

    def attend(nrows):
        nblk = nrows // per
        j_io = lax.broadcasted_iota(jnp.int32, (nblk, TOK), 0)
        t_sel = qi * TOK + lax.broadcasted_iota(jnp.int32, (nblk, TOK), 1)
        cur = t_sel // NSA_SEL_BLOCK
        forced = (j_io == 0) | (j_io == cur) | (j_io == cur - 1)
        valid = j_io * NSA_SEL_BLOCK <= t_sel
        rel = (lax.broadcasted_iota(jnp.int32, (nrows, ncol), 0) * NSA_CMP_STRIDE + (NSA_CMP_BLOCK - 1)
               - lax.broadcasted_iota(jnp.int32, (nrows, ncol), 1) % TOK)
        allowed = rel <= qi * TOK
        kc = kc_ref[0, 0:nrows, :]
        vct = vct_ref[0, :, 0:nrows]
        ps_ref[pl.ds(0, 8), :] = jnp.zeros((8, TOK), F32)
        outs = []
        imps = []
        for g in range(NSA_KV_GROUPS):
            s_t = jnp.where(allowed, _dot_nt(kc, _q_rows(qp_ref[0, g])), NEG_INF)
            m = jnp.max(s_t, axis=0, keepdims=True)
            p = jnp.exp2(s_t - jnp.maximum(m, MASK_FLOOR))
            l = jnp.sum(p, axis=0, keepdims=True)
            inv = jnp.where(l > 0.0, 1.0 / l, 0.0)
            o_t = _dot(vct, p.astype(BF16)) * inv
            gate = _gate_row(gt_ref[0, 0], g, 0)
            outs.append(_heads_to_token_major(o_t[g * NSA_HEAD_DIM:(g + 1) * NSA_HEAD_DIM, :] * gate))
            pn = p * inv
            psum = pn[:, 0:TOK]
            for r in range(1, NSA_REP):
                psum = psum + pn[:, r * TOK:(r + 1) * TOK]
            ps_ref[pl.ds(8, nrows), :] = psum
            imp = ps_ref[pl.ds(7, nblk, stride=per), :] + ps_ref[pl.ds(8 + per - 1, nblk, stride=per), :]
            for c in range(per - 1):
                imp = imp + 2.0 * ps_ref[pl.ds(8 + c, nblk, stride=per), :]
            imps.append(jnp.where(forced, -jnp.inf, jnp.where(valid, imp, NEG_INF)))
        oc_ref[0] = jnp.concatenate(outs, axis=1)

        def pick(_, carry):
            nxt = []
            for imp_c in carry:
                mx = jnp.max(imp_c, axis=0, keepdims=True)
                idx = jnp.min(jnp.where(imp_c == mx, j_io, nblk), axis=0, keepdims=True)
                nxt.append(jnp.where(j_io == idx, -jnp.inf, imp_c))
            return tuple(nxt)

        picked = lax.fori_loop(0, min(NSA_TOPK, nsel) - N_FORCED, pick, tuple(imps))
        for g in range(NSA_KV_GROUPS):
            mask_ref[0, g, 0, 0:nblk, :] = jnp.where(valid & (picked[g] == -jnp.inf), 0.0, NEG_INF)
            if nblk < nsel:
                mask_ref[0, g, 0, nblk:nsel, :] = jnp.full((nsel - nblk, TOK), NEG_INF, F32)

    n_vis = (qi * TOK + TOK - NSA_CMP_BLOCK) // NSA_CMP_STRIDE + 1
    quarter = nbp // 4
    for k in range(1, 5):
        @pl.when((n_vis > (k - 1) * quarter) & (n_vis <= k * quarter))
        def _(k=k):
            attend(k * quarter)


def _cmp_topk(qp, kc, vct, gt, bsz, s):
    nt = s // TOK
    nbp = s // NSA_CMP_STRIDE
    nsel = s // NSA_SEL_BLOCK
    return pl.pallas_call(
        functools.partial(_cmp_topk_kernel, nbp=nbp, nsel=nsel),
        grid=(bsz, nt),
        in_specs=[
            pl.BlockSpec((1, NSA_KV_GROUPS, TOK, NSA_REP * LANES), lambda b, i: (b, 0, i, 0)),
            pl.BlockSpec((1, nbp, LANES), lambda b, i: (b, 0, 0)),
            pl.BlockSpec((1, LANES, nbp), lambda b, i: (b, 0, 0)),
            pl.BlockSpec((1, 1, GATE_ROWS, TOK), lambda b, i: (b, i, 0, 0)),
        ],
        out_specs=[
            pl.BlockSpec((1, TOK, NSA_Q_WIDTH), lambda b, i: (b, i, 0)),
            pl.BlockSpec((1, NSA_KV_GROUPS, 1, nsel, TOK), lambda b, i: (b, 0, i, 0, 0)),
        ],
        out_shape=[
            jax.ShapeDtypeStruct((bsz, s, NSA_Q_WIDTH), F32),
            jax.ShapeDtypeStruct((bsz, NSA_KV_GROUPS, nt, nsel, TOK), F32),
        ],
        scratch_shapes=[pltpu.VMEM((nbp + 8, TOK), F32)],
        compiler_params=_cparams(("parallel", "parallel")),
        name="nsa_cmp_topk",
    )(qp, kc, vct, gt)


def _window_kernel(qp_ref, k_ref, vt_ref, gt_ref, o_ref):
    qi = pl.program_id(1)
    ncol = NSA_REP * TOK
    wk = WIN_TILES * TOK

    def run(first_tile, interior):
        kwin = k_ref[0, pl.ds(pl.multiple_of(first_tile * TOK, TOK), wk), :]
        outs = []
        for g in range(NSA_KV_GROUPS):
            s_t = _dot_nt(kwin, _q_rows(qp_ref[0, g]))
            if interior:
                row = lax.broadcasted_iota(jnp.int32, (TOK, ncol), 0)
                col = lax.broadcasted_iota(jnp.int32, (TOK, ncol), 1) % TOK
                oldest = jnp.where(row > col, s_t[:TOK], NEG_INF)
                newest = jnp.where(row <= col, s_t[wk - TOK:], NEG_INF)
                s_t = jnp.concatenate([oldest, s_t[TOK:wk - TOK], newest], axis=0)
            else:
                key = first_tile * TOK + lax.broadcasted_iota(jnp.int32, (wk, ncol), 0)
                t_io = qi * TOK + lax.broadcasted_iota(jnp.int32, (wk, ncol), 1) % TOK
                s_t = jnp.where((key <= t_io) & (t_io - key < NSA_WINDOW), s_t, NEG_INF)
            m = jnp.max(s_t, axis=0, keepdims=True)
            p = jnp.exp2(s_t - m)
            l = jnp.sum(p, axis=0, keepdims=True)
            vt = jnp.concatenate([vt_ref[0, g, first_tile + k] for k in range(WIN_TILES)], axis=1)
            gate = _gate_row(gt_ref[0, 0], g, 2)
            outs.append(_heads_to_token_major(_dot(vt, p.astype(BF16)) * (gate / l)))
        o_ref[0] = jnp.concatenate(outs, axis=1)

    @pl.when(qi >= WIN_TILES - 1)
    def _():
        run(qi - (WIN_TILES - 1), True)

    @pl.when(qi < WIN_TILES - 1)
    def _():
        run(0, False)


def _window(qp, k, vt, gt, bsz, s):
    nt = s // TOK
    return pl.pallas_call(
        _window_kernel,
        grid=(bsz, nt),
        in_specs=[
            pl.BlockSpec((1, NSA_KV_GROUPS, TOK, NSA_REP * LANES), lambda b, i: (b, 0, i, 0)),
            pl.BlockSpec((1, s, LANES), lambda b, i: (b, 0, 0)),
            pl.BlockSpec((1, NSA_KV_GROUPS, nt, NSA_HEAD_DIM, TOK), lambda b, i: (b, 0, 0, 0, 0)),
            pl.BlockSpec((1, 1, GATE_ROWS, TOK), lambda b, i: (b, i, 0, 0)),
        ],
        out_specs=pl.BlockSpec((1, TOK, NSA_Q_WIDTH), lambda b, i: (b, i, 0)),
        out_shape=jax.ShapeDtypeStruct((bsz, s, NSA_Q_WIDTH), F32),
        compiler_params=_cparams(("parallel", "parallel")),
        name="nsa_window",
    )(qp, k, vt, gt)


def _selected_kernel(list_ref, cnt_ref, qp_ref, k_ref, vt_ref, bias_ref, gt_ref, o_ref, m_ref, acc_ref,
                     s_ref, p_ref, a_ref, qa_ref, *, nt, nsel):
    b = pl.program_id(0)
    qi = pl.program_id(1)
    ncol = NSA_REP * TOK
    blk_per_tile = SEL_TILE // NSA_SEL_BLOCK
    n_kt = nt // (SEL_TILE // TOK)
    n_q = nt // SEL_Q
    units = [(g, h) for g in range(NSA_KV_GROUPS) for h in range(SEL_Q)]
    m_ref[...] = jnp.full(m_ref.shape, NEG_INF, F32)
    acc_ref[...] = jnp.zeros(acc_ref.shape, F32)
    p_ref[...] = jnp.zeros(p_ref.shape, BF16)
    a_ref[...] = jnp.ones(a_ref.shape, F32)
    base = (b * n_q + qi) * n_kt
    count = cnt_ref[b * n_q + qi]
    n_diag = (qi * SEL_Q * TOK) // SEL_TILE

    lane = lax.broadcasted_iota(jnp.int32, (TOK, LANES), 1)
    n_var = qa_ref.shape[0]
    for g, h in units:
        bias_g = bias_ref[0, g, h]
        if nsel % LANES:
            bias_g = jnp.concatenate([bias_g, jnp.zeros((LANES - nsel % LANES, TOK), F32)], axis=0)
        own = (lane // NSA_HEAD_DIM) == g
        for v in range(n_var):
            src = bias_g[(v // 2) * LANES:(v // 2 + 1) * LANES, :].T
            if v % 2 != 1 - g:
                src = pltpu.roll(src, NSA_HEAD_DIM, 1)
            src = src.astype(BF16)
            for r in range(NSA_REP):
                q_r = qp_ref[0, g, h * TOK:(h + 1) * TOK, r * LANES:(r + 1) * LANES]
                qa_ref[v, g, h, r * TOK:(r + 1) * TOK, :] = jnp.where(own, q_r, src)

    def tile_at(i):
        return jnp.where(i < count, list_ref[base + jnp.minimum(i, n_kt - 1)], n_diag)

    def scores(j):
        v = (j * blk_per_tile) // NSA_HEAD_DIM
        for g in range(NSA_KV_GROUPS):
            kb = k_ref[0, g, pl.ds(pl.multiple_of(j * SEL_TILE, SEL_TILE), SEL_TILE), :]
            for h in range(SEL_Q):
                s_ref[g, h] = _dot_nt(kb, qa_ref[v, g, h])

    def values(j):
        for g, h in units:
            acc_ref[g, h] = acc_ref[g, h] * a_ref[g, h] + _dot(vt_ref[0, g, j], p_ref[g, h])

    def softmax(j, causal):
        for g, h in units:
            s_t = s_ref[g, h]
            if causal:
                t_io = ((qi * SEL_Q + h) * TOK
                        + lax.broadcasted_iota(jnp.int32, (SEL_TILE, ncol), 1) % TOK)
                key_pos = j * SEL_TILE + lax.broadcasted_iota(jnp.int32, (SEL_TILE, ncol), 0)
                s_t = jnp.where(key_pos <= t_io, s_t, NEG_INF)
            m_old = m_ref[g, h]
            m_new = jnp.maximum(m_old, jnp.max(s_t, axis=0, keepdims=True))
            alpha = jnp.exp2(m_old - m_new)
            p = jnp.exp2(s_t - jnp.maximum(m_new, MASK_FLOOR))
            m_ref[g, h] = m_new
            p_ref[g, h] = p.astype(BF16)
            a_ref[g, h] = alpha

    scores(tile_at(0))

    def body(i, c):
        values(tile_at(jnp.maximum(i - 1, 0)))
        softmax(tile_at(i), False)
        scores(tile_at(i + 1))
        return c

    lax.fori_loop(0, count, body, 0)
    values(tile_at(jnp.maximum(count - 1, 0)))
    softmax(n_diag, True)
    values(n_diag)
    for h in range(SEL_Q):
        outs = []
        for g in range(NSA_KV_GROUPS):
            acc = acc_ref[g, h]
            gate = _gate_row(gt_ref[0, h], g, 1)
            outs.append(_heads_to_token_major(acc[:NSA_HEAD_DIM] * (gate / acc[NSA_HEAD_DIM:NSA_HEAD_DIM + 1])))
        o_ref[0, h * TOK:(h + 1) * TOK, :] = jnp.concatenate(outs, axis=1)


def _selected(qp, k, vt, bias, gt, tile_list, tile_count, bsz, s):
    nt = s // TOK
    nsel = s // NSA_SEL_BLOCK
    ncol = NSA_REP * TOK
    unit = (NSA_KV_GROUPS, SEL_Q)
    grid_spec = pltpu.PrefetchScalarGridSpec(
        num_scalar_prefetch=2,
        grid=(bsz, nt // SEL_Q),
        in_specs=[
            pl.BlockSpec((1, NSA_KV_GROUPS, SEL_Q * TOK, NSA_REP * LANES), lambda b, i, tl, tc: (b, 0, i, 0)),
            pl.BlockSpec((1, NSA_KV_GROUPS, s, LANES), lambda b, i, tl, tc: (b, 0, 0, 0)),
            pl.BlockSpec((1, NSA_KV_GROUPS, s // SEL_TILE, SEL_V_ROWS, SEL_TILE),
                         lambda b, i, tl, tc: (b, 0, 0, 0, 0)),
            pl.BlockSpec((1, NSA_KV_GROUPS, SEL_Q, nsel, TOK), lambda b, i, tl, tc: (b, 0, i, 0, 0)),
            pl.BlockSpec((1, SEL_Q, GATE_ROWS, TOK), lambda b, i, tl, tc: (b, i, 0, 0)),
        ],
        out_specs=pl.BlockSpec((1, SEL_Q * TOK, NSA_Q_WIDTH), lambda b, i, tl, tc: (b, i, 0)),
        scratch_shapes=[pltpu.VMEM(unit + (1, ncol), F32),
                        pltpu.VMEM(unit + (SEL_V_ROWS, ncol), F32),
                        pltpu.VMEM(unit + (SEL_TILE, ncol), F32),
                        pltpu.VMEM(unit + (SEL_TILE, ncol), BF16),
                        pltpu.VMEM(unit + (1, ncol), F32),
                        pltpu.VMEM((max(nsel // NSA_HEAD_DIM, 1),) + unit + (ncol, LANES), BF16)],
    )
    return pl.pallas_call(
        functools.partial(_selected_kernel, nt=nt, nsel=nsel),
        grid_spec=grid_spec,
        out_shape=jax.ShapeDtypeStruct((bsz, s, NSA_Q_WIDTH), F32),
        compiler_params=_cparams(("parallel", "parallel")),
        name="nsa_selected",
    )(tile_list, tile_count, qp, k, vt, bias, gt)


def _active_tiles(bias, bsz, s):
    n_q = s // (SEL_Q * TOK)
    n_kt = s // SEL_TILE
    rows = SEL_TILE // NSA_SEL_BLOCK * TOK
    act = bias.reshape(bsz, NSA_KV_GROUPS, n_q, SEL_Q, n_kt, rows).max(axis=(1, 3, 5)) > -1.0
    n_diag = (jnp.arange(n_q) * SEL_Q * TOK) // SEL_TILE
    act = act & (jnp.arange(n_kt)[None, None, :] < n_diag[None, :, None])
    order = jnp.argsort(jnp.where(act, 0, 1), axis=-1, stable=True).astype(jnp.int32)
    return order.reshape(-1), jnp.sum(act, axis=-1, dtype=jnp.int32).reshape(-1)


def _ssd_kernel(z_ref, xbc_ref, dt_ref, cw_ref, cb_ref, dtb_ref, alog_ref, dskip_ref, nw_ref, e8_ref,
                tri_ref, y_ref, ext_ref, st_ref):
    @pl.when(pl.program_id(1) == 0)
    def _():
        ext_ref[pl.ds(0, 8), :] = jnp.zeros((8, SSM_CONV_DIM), F32)
        st_ref[...] = jnp.zeros_like(st_ref)

    xb = xbc_ref[...]
    ext_ref[pl.ds(8, TOK), :] = xb
    conv = cb_ref[...] + xb * cw_ref[SSM_CONV - 1:SSM_CONV, :]
    for k in range(SSM_CONV - 1):
        conv = conv + ext_ref[pl.ds(8 - (SSM_CONV - 1) + k, TOK), :] * cw_ref[k:k + 1, :]
    ext_ref[pl.ds(0, 8), :] = xb[TOK - 8:TOK, :]
    u = _silu(conv)
    xs = u[:, :SSM_INNER]
    gn = SSM_GROUPS * SSM_STATE
    bm = u[:, SSM_INNER:SSM_INNER + gn]
    cm = u[:, SSM_INNER + gn:]
    dt = jax.nn.softplus(dt_ref[...] + dtb_ref[...])
    da = dt * (-jnp.exp(alog_ref[...]))
    tri = tri_ref[...]
    e8 = e8_ref[...]
    cum = _dot_sel_l(tri, da)
    cum_t = cum.T
    cum_e = _dot_sel_r(cum, e8)
    dt_e = _dot_sel_r(dt, e8)
    xdt = xs * dt_e
    lane = lax.broadcasted_iota(jnp.int32, (TOK, LANES), 1)
    t_io = lax.broadcasted_iota(jnp.int32, (TOK, TOK), 0)
    s_io = lax.broadcasted_iota(jnp.int32, (TOK, TOK), 1)
    causal = s_io <= t_io
    hg = SSM_HEADS // SSM_GROUPS
    y_parts = []
    cbs = []
    for g in range(SSM_GROUPS):
        in_g = (lane // SSM_STATE) == g
        cbs.append(_dot_nt(jnp.where(in_g, cm, 0.0).astype(BF16), bm.astype(BF16)))
    for c in range(SSM_HEADS // 2):
        acc = jnp.zeros((TOK, LANES), F32)
        xpair = xdt[:, c * LANES:(c + 1) * LANES]
        for hh in range(2):
            h = 2 * c + hh
            seg = cum[:, h:h + 1] - cum_t[h:h + 1, :]
            decay = jnp.where(causal, jnp.exp(jnp.where(causal, seg, 0.0)), 0.0)
            mat = (cbs[h // hg] * decay).astype(BF16)
            xh = jnp.where((lane // SSM_HEAD_DIM) == hh, xpair, 0.0).astype(BF16)
            acc = acc + _dot(mat, xh)
        y_parts.append(acc)
    y_diag = jnp.concatenate(y_parts, axis=1)
    prev = st_ref[...]
    y_off = _dot(cm.astype(BF16), prev.astype(BF16)) * jnp.exp(cum_e)
    cum_last = cum_e[TOK - 1:TOK, :]
    xdec = (xdt * jnp.exp(cum_last - cum_e)).astype(BF16)
    new = _dot(bm.T.astype(BF16), xdec)
    row_g = lax.broadcasted_iota(jnp.int32, (LANES, SSM_INNER), 0) // SSM_STATE
    col_g = lax.broadcasted_iota(jnp.int32, (LANES, SSM_INNER), 1) // (hg * SSM_HEAD_DIM)
    st_ref[...] = prev * jnp.exp(cum_last) + jnp.where(row_g == col_g, new, 0.0)
    y = (y_diag + y_off + xs * dskip_ref[...]) * _silu(z_ref[...])
    gw = SSM_INNER // SSM_GROUPS
    outs = []
    for g in range(SSM_GROUPS):
        yg = y[:, g * gw:(g + 1) * gw]
        outs.append(_rms(yg, nw_ref[:, g * gw:(g + 1) * gw]))
    y_ref[...] = jnp.concatenate(outs, axis=1)


def _ssd(proj, cw, cb, dtb, alog, dskip, nw, e8, tri, bsz, s):
    nt = s // TOK

    def col(c0, w):
        return pl.BlockSpec((TOK, w), lambda b, i: (b * nt + i, c0 // w))

    def full(shape):
        return pl.BlockSpec(shape, lambda b, i: (0,) * len(shape))

    return pl.pallas_call(
        _ssd_kernel,
        grid=(bsz, nt),
        in_specs=[col(COL_Z, SSM_INNER), col(COL_XBC, SSM_CONV_DIM), col(COL_DT, LANES),
                  full((SSM_CONV, SSM_CONV_DIM)), full((1, SSM_CONV_DIM)), full((1, LANES)), full((1, LANES)),
                  full((1, SSM_INNER)), full((1, SSM_INNER)), full((LANES, SSM_INNER)), full((TOK, TOK))],
        out_specs=pl.BlockSpec((TOK, SSM_INNER), lambda b, i: (b * nt + i, 0)),
        out_shape=jax.ShapeDtypeStruct((bsz * s, SSM_INNER), F32),
        scratch_shapes=[pltpu.VMEM((TOK + 8, SSM_CONV_DIM), F32), pltpu.VMEM((LANES, SSM_INNER), F32)],
        compiler_params=_cparams(("parallel", "arbitrary")),
        name="ssd",
    )(proj, proj, proj, cw, cb, dtb, alog, dskip, nw, e8, tri)


def _hgrn2_kernel(q_ref, f_ref, i_ref, g_ref, lb_ref, nw_ref, tri_ref, y_ref, st_ref, kp_ref, bp_ref, vp_ref):
    @pl.when(pl.program_id(1) == 0)
    def _():
        st_ref[...] = jnp.zeros_like(st_ref)
        kp_ref[...] = jnp.zeros_like(kp_ref)
        bp_ref[...] = jnp.zeros_like(bp_ref)
        vp_ref[...] = jnp.zeros_like(vp_ref)

    nsub = TOK // HG_SUB
    in_sub = lax.broadcasted_iota(jnp.int32, (TOK, HG_KEY_DIM), 0) % HG_SUB
    tri = tri_ref[...]
    outs = []
    for h in range(HG_HEADS):
        sl = slice(h * HG_KEY_DIM, (h + 1) * HG_KEY_DIM)
        lb = lb_ref[:, sl]
        fr = f_ref[:, sl]
        q = _silu(q_ref[:, sl]) * (HG_KEY_DIM ** -0.5)
        log_f = jnp.log(lb + (1.0 - lb) * jax.nn.sigmoid(fr))
        k = (1.0 - lb) * jax.nn.sigmoid(-fr)
        v = i_ref[:, sl]
        bcum = _dot_sel_l(tri, log_f)
        st = st_ref[h]
        o = _dot_nt((q * jnp.exp(bcum)).astype(BF16), st.astype(BF16))
        ends = [bcum[(j + 1) * HG_SUB - 1:(j + 1) * HG_SUB, :] for j in range(nsub)]
        blk = (HG_SUB, HG_KEY_DIM)
        zero_blk = jnp.zeros(blk, BF16)
        e_prev = jnp.concatenate([jnp.zeros(blk, F32)] + [jnp.broadcast_to(ends[j], blk) for j in range(nsub - 1)],
                                 axis=0)
        e_own = jnp.concatenate([jnp.broadcast_to(ends[j], blk) for j in range(nsub)], axis=0)
        q_loc = (q * jnp.exp(bcum - e_prev)).astype(BF16)
        k_loc = k * jnp.exp(e_own - bcum)
        q_cols = []
        k_cols = []
        for i in range(1, nsub):
            q_cols.append(jnp.concatenate(
                [zero_blk] * i + [q_loc[i * HG_SUB:(i + 1) * HG_SUB]] + [zero_blk] * (nsub - 1 - i), axis=0))
            parts = []
            for j in range(i):
                kj = k_loc[j * HG_SUB:(j + 1) * HG_SUB]
                if j < i - 1:
                    kj = kj * jnp.exp(ends[i - 1] - ends[j])
                parts.append(kj.astype(BF16))
            k_cols.append(jnp.concatenate(parts + [zero_blk] * (nsub - i), axis=0))
        q_aug = jnp.concatenate(q_cols, axis=1)
        k_aug = jnp.concatenate(k_cols, axis=1)
        att = _dot_nt(q_aug, k_aug)
        o = o + _dot(att.astype(BF16), v.astype(BF16))
        kp_ref[pl.ds(HG_SUB, TOK), :] = k
        bp_ref[pl.ds(HG_SUB, TOK), :] = bcum
        vp_ref[pl.ds(HG_SUB, TOK), :] = v
        o = o + jnp.sum(q * k, axis=1, keepdims=True) * v
        for d in range(1, HG_SUB):
            ks = kp_ref[pl.ds(HG_SUB - d, TOK), :]
            bs = bp_ref[pl.ds(HG_SUB - d, TOK), :]
            vs = vp_ref[pl.ds(HG_SUB - d, TOK), :]
            ok = in_sub >= d
            w = jnp.sum(q * ks * jnp.exp(bcum - bs), axis=1, keepdims=True)
            o = o + jnp.where(ok, w * vs, 0.0)
        b_last = bcum[TOK - 1:TOK, :]
        kdec = (k * jnp.exp(b_last - bcum)).astype(BF16)
        st_ref[h] = st * jnp.exp(b_last) + _dot(v.T.astype(BF16), kdec)
        outs.append(_rms(o, nw_ref[...]) * _silu(g_ref[:, sl]))
    y_ref[...] = jnp.concatenate(outs, axis=1)


def _hgrn2(proj, lb, nw, tri, bsz, s):
    nt = s // TOK

    def col(c0):
        return pl.BlockSpec((TOK, HG_WIDTH), lambda b, i: (b * nt + i, c0 // HG_WIDTH))

    def full(shape):
        return pl.BlockSpec(shape, lambda b, i: (0,) * len(shape))

    pad = pltpu.VMEM((TOK + HG_SUB, HG_KEY_DIM), F32)
    return pl.pallas_call(
        _hgrn2_kernel,
        grid=(bsz, nt),
        in_specs=[col(COL_HQ), col(COL_HF), col(COL_HI), col(COL_HG), full((1, HG_WIDTH)),
                  full((1, HG_VAL_DIM)), full((TOK, TOK))],
        out_specs=pl.BlockSpec((TOK, HG_WIDTH), lambda b, i: (b * nt + i, 0)),
        out_shape=jax.ShapeDtypeStruct((bsz * s, HG_WIDTH), F32),
        scratch_shapes=[pltpu.VMEM((HG_HEADS, HG_VAL_DIM, HG_KEY_DIM), F32), pad, pad, pad],
        compiler_params=_cparams(("parallel", "arbitrary")),
        name="hgrn2",
    )(proj, proj, proj, proj, lb, nw, tri)


def _merge_kernel(ga_ref, gb_ref, gc_ref, oc_ref, os_ref, ow_ref, yb_ref, yc_ref, x_ref,
                  wa_ref, wb_ref, wc_ref, wo_ref, o_ref):
    nsa = oc_ref[...] + os_ref[...] + ow_ref[...]
    ya = _dot(nsa.astype(BF16), wa_ref[...])
    yb = _dot(yb_ref[...].astype(BF16), wb_ref[...])
    yc = _dot(yc_ref[...].astype(BF16), wc_ref[...])
    merged = (jax.nn.sigmoid(ga_ref[...]) * ya + jax.nn.sigmoid(gb_ref[...]) * yb
              + jax.nn.sigmoid(gc_ref[...]) * yc)
    o_ref[...] = x_ref[...] + _dot(merged.astype(BF16), wo_ref[...])


def _merge(proj, oc, os_, ow, yb, yc, x2, wa, wb, wc, wo):
    t = x2.shape[0]
    tm = min(512, t)

    def col(c0, w):
        return pl.BlockSpec((tm, w), lambda i: (i, c0 // w))

    def full(shape):
        return pl.BlockSpec(shape, lambda i: (0,) * len(shape))

    row512 = pl.BlockSpec((tm, 512), lambda i: (i, 0))
    return pl.pallas_call(
        _merge_kernel,
        grid=(t // tm,),
        in_specs=[col(COL_GA, D_MODEL), col(COL_GB, D_MODEL), col(COL_GC, D_MODEL),
                  row512, row512, row512, row512, row512,
                  pl.BlockSpec((tm, D_MODEL), lambda i: (i, 0)),
                  full((NSA_Q_WIDTH, D_MODEL)), full((SSM_INNER, D_MODEL)),
                  full((HG_WIDTH, D_MODEL)), full((D_MODEL, D_MODEL))],
        out_specs=pl.BlockSpec((tm, D_MODEL), lambda i: (i, 0)),
        out_shape=jax.ShapeDtypeStruct((t, D_MODEL), F32),
        compiler_params=_cparams(("parallel",)),
        name="merge",
    )(proj, proj, proj, oc, os_, ow, yb, yc, x2, wa, wb, wc, wo)


def _xa_mem_kernel(mem_ref, nm_ref, wk_ref, wv_ref, kn_ref, k_ref, v_ref):
    m = _rms(mem_ref[0], nm_ref[...]).astype(BF16)
    k = _dot(m, wk_ref[...])
    ks = [_rms(k[:, h * XA_HEAD_DIM:(h + 1) * XA_HEAD_DIM], kn_ref[...]) for h in range(XA_HEADS)]
    k_ref[0] = jnp.concatenate(ks, axis=1).astype(BF16)
    v_ref[0] = _dot(m, wv_ref[...]).astype(BF16)


def _xa_mem(mem, nm, wk, wv, kn):
    bsz, ml, _ = mem.shape

    def full(shape):
        return pl.BlockSpec(shape, lambda b: (0,) * len(shape))

    return pl.pallas_call(
        _xa_mem_kernel,
        grid=(bsz,),
        in_specs=[pl.BlockSpec((1, ml, D_MODEL), lambda b: (b, 0, 0)), full((1, D_MODEL)),
                  full((D_MODEL, XA_WIDTH)), full((D_MODEL, XA_WIDTH)), full((1, XA_HEAD_DIM))],
        out_specs=[pl.BlockSpec((1, ml, XA_WIDTH), lambda b: (b, 0, 0)),
                   pl.BlockSpec((1, ml, XA_WIDTH), lambda b: (b, 0, 0))],
        out_shape=[jax.ShapeDtypeStruct((bsz, ml, XA_WIDTH), BF16),
                   jax.ShapeDtypeStruct((bsz, ml, XA_WIDTH), BF16)],
        compiler_params=_cparams(("parallel",)),
        name="xa_mem",
    )(mem, nm, wk, wv, kn)


def _xa_kernel(x_ref, nx_ref, wq_ref, qn_ref, k_ref, v_ref, wo_ref, o_ref):
    x = x_ref[...]
    h = _rms(x, nx_ref[...]).astype(BF16)
    q = _dot(h, wq_ref[...])
    outs = []
    for hd in range(XA_HEADS):
        sl = slice(hd * XA_HEAD_DIM, (hd + 1) * XA_HEAD_DIM)
        qh = _rms(q[:, sl], qn_ref[...]).astype(BF16)
        sc = _dot_nt(qh, k_ref[0, :, sl]) * (XA_HEAD_DIM ** -0.5)
        sc = sc - jnp.max(sc, axis=-1, keepdims=True)
        p = jnp.exp(sc)
        p = p / jnp.sum(p, axis=-1, keepdims=True)
        outs.append(_dot(p.astype(BF16), v_ref[0, :, sl]))
    o = jnp.concatenate(outs, axis=1).astype(BF16)
    o_ref[...] = x + _dot(o, wo_ref[...])


def _xa(x2, nx, wq, qn, k, v, wo, s):
    t = x2.shape[0]
    tm = min(512, s)
    ml = k.shape[1]
    per_b = s // tm

    def full(shape):
        return pl.BlockSpec(shape, lambda i: (0,) * len(shape))

    return pl.pallas_call(
        _xa_kernel,
        grid=(t // tm,),
        in_specs=[pl.BlockSpec((tm, D_MODEL), lambda i: (i, 0)), full((1, D_MODEL)), full((D_MODEL, XA_WIDTH)),
                  full((1, XA_HEAD_DIM)),
                  pl.BlockSpec((1, ml, XA_WIDTH), lambda i: (i // per_b, 0, 0)),
                  pl.BlockSpec((1, ml, XA_WIDTH), lambda i: (i // per_b, 0, 0)),
                  full((XA_WIDTH, D_MODEL))],
        out_specs=pl.BlockSpec((tm, D_MODEL), lambda i: (i, 0)),
        out_shape=jax.ShapeDtypeStruct((t, D_MODEL), F32),
        compiler_params=_cparams(("parallel",)),
        name="cross_attn",
    )(x2, nx, wq, qn, k, v, wo)


def _ffn_kernel(x_ref, nw_ref, wg_ref, wu_ref, wd_ref, o_ref, h_ref, acc_ref):
    j = pl.program_id(1)

    @pl.when(j == 0)
    def _():
        h_ref[...] = _rms(x_ref[...], nw_ref[...]).astype(BF16)
        acc_ref[...] = jnp.zeros_like(acc_ref)

    h = h_ref[...]
    a = _silu(_dot(h, wg_ref[...])) * _dot(h, wu_ref[...])
    acc_ref[...] += _dot(a.astype(BF16), wd_ref[...])

    @pl.when(j == pl.num_programs(1) - 1)
    def _():
        o_ref[...] = x_ref[...] + acc_ref[...]


def _ffn(x2, nw, wg, wu, wd):
    t = x2.shape[0]
    tm = min(1024, t)
    th = FFN_HIDDEN // 2
    return pl.pallas_call(
        _ffn_kernel,
        grid=(t // tm, FFN_HIDDEN // th),
        in_specs=[pl.BlockSpec((tm, D_MODEL), lambda i, j: (i, 0)),
                  pl.BlockSpec((1, D_MODEL), lambda i, j: (0, 0)),
                  pl.BlockSpec((D_MODEL, th), lambda i, j: (0, j)),
                  pl.BlockSpec((D_MODEL, th), lambda i, j: (0, j)),
                  pl.BlockSpec((th, D_MODEL), lambda i, j: (j, 0))],
        out_specs=pl.BlockSpec((tm, D_MODEL), lambda i, j: (i, 0)),
        out_shape=jax.ShapeDtypeStruct((t, D_MODEL), F32),
        scratch_shapes=[pltpu.VMEM((tm, D_MODEL), BF16), pltpu.VMEM((tm, D_MODEL), F32)],
        compiler_params=_cparams(("parallel", "arbitrary")),
        name="ffn",
    )(x2, nw, wg, wu, wd)


def _pack_w_in(w):
    parts = {}
    off = 0
    names = ("q", "kc", "vc", "ks", "vs", "kw", "vw", "ng", "z", "xbc", "dt", "hq", "hf", "hi", "hg",
             "ga", "gb", "gc")
    for name, size in zip(names, SPLIT_SIZES):
        parts[name] = w[:, off:off + size]
        off += size

    def padded(a, width):
        return jnp.pad(a, ((0, 0), (0, width - a.shape[1])))

    cols = [parts["ga"], parts["gb"], parts["gc"], parts["q"], parts["z"], parts["hq"], parts["hf"],
            parts["hi"], parts["hg"], parts["xbc"], parts["kc"], parts["vc"], parts["ks"], parts["vs"],
            parts["kw"], parts["vw"], padded(parts["ng"], LANES), padded(parts["dt"], LANES)]
    packed = jnp.concatenate(cols, axis=1)
    return padded(packed, PROJ_WIDTH).astype(BF16)


def _rope_tables(pos):
    half = NSA_HEAD_DIM // 2
    inv = 1.0 / (ROPE_THETA ** (jnp.arange(half, dtype=F32) / half))
    ang = pos.astype(F32)[:, None] * inv[None, :]
    reps = LANES // half
    return jnp.tile(jnp.cos(ang), (1, reps)), jnp.tile(jnp.sin(ang), (1, reps))


def _block_diag2(w):
    z = jnp.zeros_like(w)
    return jnp.concatenate([jnp.concatenate([w, z], axis=-1), jnp.concatenate([z, w], axis=-1)], axis=-2)


def kernel(x, mem, norm_mix, w_in, nsa_q_norm, nsa_k_norm, nsa_cmp_pos_k, nsa_cmp_pos_v, nsa_cmp_w1_k, nsa_cmp_w2_k, nsa_cmp_w1_v, nsa_cmp_w2_v, w_nsa_o, ssm_conv_w, ssm_conv_b, ssm_dt_bias, ssm_a_log, ssm_d, ssm_norm, w_ssm_o, hg_lb_logits, hg_norm, w_hg_o, w_out, norm_xa, norm_mem, xa_w_q, xa_w_k, xa_w_v, xa_q_norm, xa_k_norm, xa_w_o, norm_ffn, ffn_w_gate, ffn_w_up, ffn_w_down):
    bsz, s, d = x.shape
    depth = w_in.shape[0]
    assert d == D_MODEL and s % (8 * TOK) == 0 and s // TOK <= TOK
    t = bsz * s
    hd = NSA_HEAD_DIM

    cos_t, sin_t = _rope_tables(jnp.arange(s))
    nbp = s // NSA_CMP_STRIDE
    cos_c, sin_c = _rope_tables(jnp.arange(nbp) * NSA_CMP_STRIDE + NSA_CMP_BLOCK - 1)
    tri = jnp.tril(jnp.ones((TOK, TOK), F32)).astype(BF16)
    rows = np.arange(LANES)[:, None]
    cols = np.arange(NSA_Q_WIDTH)[None, :]
    e8 = jnp.asarray(rows == cols // SSM_HEAD_DIM, F32).astype(BF16)

    lb_sm = jax.nn.softmax(hg_lb_logits.astype(F32), axis=0)
    lb_all = jnp.cumsum(lb_sm, axis=0) - lb_sm[0:1]

    def pad_lanes(v):
        return jnp.pad(v, (0, LANES - v.shape[0]))[None, :]

    x2 = x.reshape(t, d)
    for l in range(depth):
        proj = _in_proj(x2, norm_mix[l][None, :], _pack_w_in(w_in[l]))

        qn2 = jnp.tile(nsa_q_norm[l], 2)[None, :]
        kn2 = jnp.tile(nsa_k_norm[l], 2)[None, :]
        qp, ksn, kwn, vst, vwt, gt = _nsa_prep(proj, cos_t, sin_t, qn2, kn2, bsz, s)
        w1k = _block_diag2(nsa_cmp_w1_k[l].reshape(NSA_CMP_BLOCK, hd, hd)).astype(BF16)
        w1v = _block_diag2(nsa_cmp_w1_v[l].reshape(NSA_CMP_BLOCK, hd, hd)).astype(BF16)
        kc, vct = _compress(proj, jnp.tile(nsa_cmp_pos_k[l], (1, 2)), jnp.tile(nsa_cmp_pos_v[l], (1, 2)),
                            w1k, _block_diag2(nsa_cmp_w2_k[l]).astype(BF16),
                            w1v, _block_diag2(nsa_cmp_w2_v[l]).astype(BF16), kn2, cos_c, sin_c, bsz, s)
        o_c, sel_bias = _cmp_topk(qp, kc, vct, gt, bsz, s)
        tile_list, tile_count = _active_tiles(sel_bias, bsz, s)
        o_s = _selected(qp, ksn, vst, sel_bias, gt, tile_list, tile_count, bsz, s)
        o_w = _window(qp, kwn, vwt, gt, bsz, s)

        y_b = _ssd(proj, ssm_conv_w[l], ssm_conv_b[l][None, :], pad_lanes(ssm_dt_bias[l]),
                   pad_lanes(ssm_a_log[l]), jnp.repeat(ssm_d[l], SSM_HEAD_DIM)[None, :],
                   ssm_norm[l][None, :], e8, tri, bsz, s)
        y_c = _hgrn2(proj, lb_all[l][None, :], hg_norm[l][None, :], tri, bsz, s)

        x2 = _merge(proj, o_c.reshape(t, -1), o_s.reshape(t, -1), o_w.reshape(t, -1), y_b, y_c, x2,
                    w_nsa_o[l].astype(BF16), w_ssm_o[l].astype(BF16), w_hg_o[l].astype(BF16),
                    w_out[l].astype(BF16))

        k_m, v_m = _xa_mem(mem, norm_mem[l][None, :], xa_w_k[l].astype(BF16), xa_w_v[l].astype(BF16),
                           xa_k_norm[l][None, :])
        x2 = _xa(x2, norm_xa[l][None, :], xa_w_q[l].astype(BF16), xa_q_norm[l][None, :], k_m, v_m,
                 xa_w_o[l].astype(BF16), s)
        x2 = _ffn(x2, norm_ffn[l][None, :], ffn_w_gate[l].astype(BF16), ffn_w_up[l].astype(BF16),
                  ffn_w_down[l].astype(BF16))
    return x2.reshape(bsz, s, d)
```

```python
import functools
import math

import numpy as np
import jax
import jax.numpy as jnp
from jax import lax
from jax.experimental import pallas as pl
from jax.experimental.pallas import tpu as pltpu

F32 = jnp.float32
BF16 = jnp.bfloat16

D_MODEL = 1024
NORM_EPS = 1e-6
ROPE_THETA = 10000.0
NEG_INF = -1e30
FORCE_SCORE = 1e9

NSA_HEADS = 8
NSA_KV_GROUPS = 2
NSA_REP = NSA_HEADS // NSA_KV_GROUPS
NSA_HEAD_DIM = 64
NSA_CMP_BLOCK = 32
NSA_CMP_STRIDE = 16
NSA_SEL_BLOCK = 64
NSA_TOPK = 16
N_FORCED = 3
NSA_WINDOW = 512
NSA_Q_WIDTH = NSA_HEADS * NSA_HEAD_DIM
NSA_KV_WIDTH = NSA_KV_GROUPS * NSA_HEAD_DIM

SSM_INNER = 512
SSM_HEAD_DIM = 64
SSM_HEADS = SSM_INNER // SSM_HEAD_DIM
SSM_GROUPS = 2
SSM_STATE = 64
SSM_CONV = 4
SSM_CONV_DIM = SSM_INNER + 2 * SSM_GROUPS * SSM_STATE

HG_HEADS = 4
HG_KEY_DIM = 128
HG_VAL_DIM = 128
HG_WIDTH = HG_HEADS * HG_KEY_DIM
HG_SUB = 16

XA_HEADS = 4
XA_HEAD_DIM = 128
XA_WIDTH = XA_HEADS * XA_HEAD_DIM

FFN_HIDDEN = -(-(8 * D_MODEL) // (3 * 256)) * 256

SPLIT_SIZES = (
    NSA_Q_WIDTH, NSA_KV_WIDTH, NSA_KV_WIDTH, NSA_KV_WIDTH, NSA_KV_WIDTH, NSA_KV_WIDTH, NSA_KV_WIDTH,
    3 * NSA_HEADS, SSM_INNER, SSM_CONV_DIM, SSM_HEADS, HG_WIDTH, HG_WIDTH, HG_HEADS * HG_VAL_DIM,
    HG_HEADS * HG_VAL_DIM, D_MODEL, D_MODEL, D_MODEL,
)

LANES = 128
TOK = 128
GATE_ROWS = 32
PREP_TILES = 4
SEL_Q = 4
SEL_TILE = 512
SEL_V_ROWS = NSA_HEAD_DIM + 16
WIN_TILES = NSA_WINDOW // TOK + 1
LOG2_E = math.log2(math.e)
MASK_FLOOR = -1e20
VMEM_LIMIT = 56 * 1024 * 1024

PROJ_WIDTH = 8192
GATE_WIDTH = 3 * D_MODEL
COL_GA, COL_GB, COL_GC = 0, 1024, 2048
COL_Q = 0
COL_Z = 512
COL_HQ, COL_HF, COL_HI, COL_HG = 1024, 1536, 2048, 2560
COL_XBC = 3072
COL_KC, COL_VC, COL_KS, COL_VS, COL_KW, COL_VW = 3840, 3968, 4096, 4224, 4352, 4480
COL_NG = 4608
COL_DT = 4736


def _cparams(sem):
    return pltpu.CompilerParams(dimension_semantics=sem, vmem_limit_bytes=VMEM_LIMIT)


def _dot(a, b):
    return jnp.dot(a, b, preferred_element_type=F32)


def _dot_nt(a, b):
    return lax.dot_general(a, b, (((1,), (1,)), ((), ())), preferred_element_type=F32)


def _split3(a):
    hi = a.astype(BF16)
    r1 = a - hi.astype(F32)
    mid = r1.astype(BF16)
    lo = (r1 - mid.astype(F32)).astype(BF16)
    return hi, mid, lo


def _dot_sel_r(a, sel):
    hi, mid, lo = _split3(a)
    return _dot(hi, sel) + _dot(mid, sel) + _dot(lo, sel)


def _dot_sel_l(sel, a):
    hi, mid, lo = _split3(a)
    return _dot(sel, hi) + _dot(sel, mid) + _dot(sel, lo)


def _silu(x):
    return x * jax.nn.sigmoid(x)


def _rms(x, w):
    return x * lax.rsqrt(jnp.mean(x * x, axis=-1, keepdims=True) + NORM_EPS) * w


def _inproj_kernel(x_ref, nw_ref, w_ref, og_ref, of_ref, h_ref, *, n_gate):
    j = pl.program_id(1)

    @pl.when(j == 0)
    def _():
        h_ref[...] = _rms(x_ref[...], nw_ref[...]).astype(BF16)

    @pl.when(j < n_gate)
    def _():
        og_ref[...] = _dot(h_ref[...], w_ref[...]).astype(BF16)

    @pl.when(j >= n_gate)
    def _():
        of_ref[...] = _dot(h_ref[...], w_ref[...])


def _in_proj(x2, norm_w, w_packed):
    t = x2.shape[0]
    tm = min(2048, t)
    tn = 1024
    n_gate = GATE_WIDTH // tn
    return pl.pallas_call(
        functools.partial(_inproj_kernel, n_gate=n_gate),
        grid=(t // tm, PROJ_WIDTH // tn),
        in_specs=[
            pl.BlockSpec((tm, D_MODEL), lambda i, j: (i, 0)),
            pl.BlockSpec((1, D_MODEL), lambda i, j: (0, 0)),
            pl.BlockSpec((D_MODEL, tn), lambda i, j: (0, j)),
        ],
        out_specs=[pl.BlockSpec((tm, tn), lambda i, j: (i, jnp.minimum(j, n_gate - 1))),
                   pl.BlockSpec((tm, tn), lambda i, j: (i, jnp.maximum(j - n_gate, 0)))],
        out_shape=[jax.ShapeDtypeStruct((t, GATE_WIDTH), BF16),
                   jax.ShapeDtypeStruct((t, PROJ_WIDTH - GATE_WIDTH), F32)],
        scratch_shapes=[pltpu.VMEM((tm, D_MODEL), BF16)],
        compiler_params=_cparams(("parallel", "arbitrary")),
        name="in_proj",
    )(x2, norm_w, w_packed)


def _norm_rope_pair(x, w, cos, sin):
    lane = lax.broadcasted_iota(jnp.int32, x.shape, 1)
    lo = lane < NSA_HEAD_DIM
    sq = x * x
    s_lo = jnp.sum(jnp.where(lo, sq, 0.0), axis=1, keepdims=True)
    s_hi = jnp.sum(jnp.where(lo, 0.0, sq), axis=1, keepdims=True)
    ms = jnp.where(lo, s_lo, s_hi) * (1.0 / NSA_HEAD_DIM)
    y = x * lax.rsqrt(ms + NORM_EPS) * w
    half = NSA_HEAD_DIM // 2
    first = (lane % NSA_HEAD_DIM) < half
    rot = jnp.where(first, -pltpu.roll(y, LANES - half, 1), pltpu.roll(y, half, 1))
    return y * cos + rot * sin


def _nsa_prep_kernel(q_ref, ks_ref, vs_ref, kw_ref, vw_ref, ng_ref, cos_ref, sin_ref, qn_ref, kn_ref,
                     qp_ref, ksn_ref, kwn_ref, vst_ref, vwt_ref, gt_ref):
    lane = lax.broadcasted_iota(jnp.int32, (TOK, LANES), 1)
    row = lax.broadcasted_iota(jnp.int32, (TOK, LANES), 0)
    scale = NSA_HEAD_DIM ** -0.5 * LOG2_E
    ones_rows = (lax.broadcasted_iota(jnp.int32, (SEL_V_ROWS - NSA_HEAD_DIM, TOK), 0) == 0).astype(BF16)
    for u in range(PREP_TILES):
        rows = slice(u * TOK, (u + 1) * TOK)
        cos = cos_ref[rows, :]
        sin = sin_ref[rows, :]
        for c in range(NSA_HEADS // 2):
            y = _norm_rope_pair(q_ref[rows, c * LANES:(c + 1) * LANES], qn_ref[...], cos, sin) * scale
            y_sw = pltpu.roll(y, NSA_HEAD_DIM, 1)
            for hh in range(2):
                h = 2 * c + hh
                g, r = divmod(h, NSA_REP)
                src = y if hh == g else y_sw
                keep = (lane // NSA_HEAD_DIM) == g
                qp_ref[0, g, rows, r * LANES:(r + 1) * LANES] = jnp.where(keep, src, 0.0).astype(BF16)
        ks_n = _norm_rope_pair(ks_ref[rows, :], kn_ref[...], cos, sin)
        tile = pl.program_id(1) * PREP_TILES + u
        blk = (tile * (TOK // NSA_SEL_BLOCK) + row // NSA_SEL_BLOCK) % NSA_HEAD_DIM
        onehot = ((lane % NSA_HEAD_DIM) == blk).astype(F32)
        for g in range(NSA_KV_GROUPS):
            ksn_ref[0, g, rows, :] = jnp.where((lane // NSA_HEAD_DIM) == g, ks_n, onehot).astype(BF16)
        kwn_ref[0, rows, :] = _norm_rope_pair(kw_ref[rows, :], kn_ref[...], cos, sin).astype(BF16)
        vs_t = vs_ref[rows, :].T.astype(BF16)
        vw_t = vw_ref[rows, :].T.astype(BF16)
        for g in range(NSA_KV_GROUPS):
            vst_ref[0, g, 0, 0:NSA_HEAD_DIM, rows] = vs_t[g * NSA_HEAD_DIM:(g + 1) * NSA_HEAD_DIM, :]
            vst_ref[0, g, 0, NSA_HEAD_DIM:SEL_V_ROWS, rows] = ones_rows
            vwt_ref[0, g, u] = vw_t[g * NSA_HEAD_DIM:(g + 1) * NSA_HEAD_DIM, :]
        gt_ref[0, u] = jax.nn.sigmoid(ng_ref[rows, :]).T[0:GATE_ROWS, :]


def _nsa_prep(proj, cos_t, sin_t, qn2, kn2, bsz, s):
    tq = PREP_TILES * TOK
    n_step = s // tq

    def col(c0, w):
        return pl.BlockSpec((tq, w), lambda b, i: (b * n_step + i, c0 // w))

    tab = pl.BlockSpec((tq, LANES), lambda b, i: (i, 0))
    vec = pl.BlockSpec((1, LANES), lambda b, i: (0, 0))
    return pl.pallas_call(
        _nsa_prep_kernel,
        grid=(bsz, n_step),
        in_specs=[col(COL_Q, NSA_Q_WIDTH), col(COL_KS, LANES), col(COL_VS, LANES), col(COL_KW, LANES),
                  col(COL_VW, LANES), col(COL_NG, LANES), tab, tab, vec, vec],
        out_specs=[
            pl.BlockSpec((1, NSA_KV_GROUPS, tq, NSA_REP * LANES), lambda b, i: (b, 0, i, 0)),
            pl.BlockSpec((1, NSA_KV_GROUPS, tq, LANES), lambda b, i: (b, 0, i, 0)),
            pl.BlockSpec((1, tq, LANES), lambda b, i: (b, i, 0)),
            pl.BlockSpec((1, NSA_KV_GROUPS, 1, SEL_V_ROWS, SEL_TILE), lambda b, i: (b, 0, i, 0, 0)),
            pl.BlockSpec((1, NSA_KV_GROUPS, PREP_TILES, NSA_HEAD_DIM, TOK), lambda b, i: (b, 0, i, 0, 0)),
            pl.BlockSpec((1, PREP_TILES, GATE_ROWS, TOK), lambda b, i: (b, i, 0, 0)),
        ],
        out_shape=[
            jax.ShapeDtypeStruct((bsz, NSA_KV_GROUPS, s, NSA_REP * LANES), BF16),
            jax.ShapeDtypeStruct((bsz, NSA_KV_GROUPS, s, LANES), BF16),
            jax.ShapeDtypeStruct((bsz, s, LANES), BF16),
            jax.ShapeDtypeStruct((bsz, NSA_KV_GROUPS, s // SEL_TILE, SEL_V_ROWS, SEL_TILE), BF16),
            jax.ShapeDtypeStruct((bsz, NSA_KV_GROUPS, s // TOK, NSA_HEAD_DIM, TOK), BF16),
            jax.ShapeDtypeStruct((bsz, s // TOK, GATE_ROWS, TOK), F32),
        ],
        compiler_params=_cparams(("parallel", "parallel")),
        name="nsa_prep",
    )(proj, proj, proj, proj, proj, proj, cos_t, sin_t, qn2, kn2)


def _compress_kernel(kc_ref, vc_ref, posk_ref, posv_ref, w1k_ref, w2k_ref, w1v_ref, w2v_ref, kn_ref,
                     cos_ref, sin_ref, kco_ref, vct_ref, sh_ref, *, nbp):
    half_blk = NSA_CMP_BLOCK // 2

    def compress(x_ref, pos_ref, w1_ref, w2_ref):
        acc_a = jnp.zeros((nbp, LANES), F32)
        acc_b = jnp.zeros((nbp, LANES), F32)
        for l in range(half_blk):
            xl = x_ref[pl.ds(l, nbp, stride=NSA_CMP_STRIDE), :]
            acc_a = acc_a + _dot((xl + pos_ref[l:l + 1, :]).astype(BF16), w1_ref[l])
            acc_b = acc_b + _dot((xl + pos_ref[half_blk + l:half_blk + l + 1, :]).astype(BF16),
                                 w1_ref[half_blk + l])
        sh_ref[pl.ds(0, nbp), :] = acc_b
        sh_ref[pl.ds(nbp, 8), :] = jnp.zeros((8, LANES), F32)
        hid = acc_a + sh_ref[pl.ds(1, nbp), :]
        return _dot(_silu(hid).astype(BF16), w2_ref[...])

    kc = compress(kc_ref, posk_ref, w1k_ref, w2k_ref)
    kco_ref[0] = _norm_rope_pair(kc, kn_ref[...], cos_ref[...], sin_ref[...]).astype(BF16)
    vc = compress(vc_ref, posv_ref, w1v_ref, w2v_ref)
    vct_ref[0] = vc.T.astype(BF16)


def _compress(proj, posk, posv, w1k, w2k, w1v, w2v, kn2, cos_c, sin_c, bsz, s):
    nbp = s // NSA_CMP_STRIDE

    def full(shape):
        return pl.BlockSpec(shape, lambda b: (0,) * len(shape))

    return pl.pallas_call(
        functools.partial(_compress_kernel, nbp=nbp),
        grid=(bsz,),
        in_specs=[
            pl.BlockSpec((s, LANES), lambda b: (b, COL_KC // LANES)),
            pl.BlockSpec((s, LANES), lambda b: (b, COL_VC // LANES)),
            full((NSA_CMP_BLOCK, LANES)), full((NSA_CMP_BLOCK, LANES)),
            full((NSA_CMP_BLOCK, LANES, LANES)), full((LANES, LANES)),
            full((NSA_CMP_BLOCK, LANES, LANES)), full((LANES, LANES)),
            full((1, LANES)), full((nbp, LANES)), full((nbp, LANES)),
        ],
        out_specs=[
            pl.BlockSpec((1, nbp, LANES), lambda b: (b, 0, 0)),
            pl.BlockSpec((1, LANES, nbp), lambda b: (b, 0, 0)),
        ],
        out_shape=[
            jax.ShapeDtypeStruct((bsz, nbp, LANES), BF16),
            jax.ShapeDtypeStruct((bsz, LANES, nbp), BF16),
        ],
        scratch_shapes=[pltpu.VMEM((nbp + 8, LANES), F32)],
        compiler_params=_cparams(("parallel",)),
        name="nsa_compress",
    )(proj, proj, posk, posv, w1k, w2k, w1v, w2v, kn2, cos_c, sin_c)


def _q_rows(qp):
    return jnp.concatenate([qp[:, r * LANES:(r + 1) * LANES] for r in range(NSA_REP)], axis=0)


def _gate_row(gt, g, branch):
    rows = [(g * NSA_REP + r) * 3 + branch for r in range(NSA_REP)]
    return jnp.concatenate([gt[i:i + 1, :] for i in rows], axis=1)


def _heads_to_token_major(acc_g):
    outs = []
    for p in range(NSA_REP // 2):
        blk = jnp.concatenate([acc_g[:, (2 * p) * TOK:(2 * p + 1) * TOK],
                               acc_g[:, (2 * p + 1) * TOK:(2 * p + 2) * TOK]], axis=0)
        outs.append(blk.T)
    return jnp.concatenate(outs, axis=1)


def _cmp_topk_kernel(qp_ref, kc_ref, vct_ref, gt_ref, oc_ref, mask_ref, ps_ref, *, nbp, nsel):
    qi = pl.program_id(1)
    ncol = NSA_REP * TOK
    per = NSA_SEL_BLOCK // NSA_CMP_STRIDE

    def attend(nrows):
        nblk = nrows // per
        j_io = lax.broadcasted_iota(jnp.int32, (nblk, TOK), 0)
        t_sel = qi * TOK + lax.broadcasted_iota(jnp.int32, (nblk, TOK), 1)
        cur = t_sel // NSA_SEL_BLOCK
        forced = (j_io == 0) | (j_io == cur) | (j_io == cur - 1)
        valid = j_io * NSA_SEL_BLOCK <= t_sel
        rel = (lax.broadcasted_iota(jnp.int32, (nrows, ncol), 0) * NSA_CMP_STRIDE + (NSA_CMP_BLOCK - 1)
               - lax.broadcasted_iota(jnp.int32, (nrows, ncol), 1) % TOK)
        allowed = rel <= qi * TOK
        kc = kc_ref[0, 0:nrows, :]
        vct = vct_ref[0, :, 0:nrows]
        ps_ref[pl.ds(0, 8), :] = jnp.zeros((8, TOK), F32)
        outs = []
        imps = []
        for g in range(NSA_KV_GROUPS):
            s_t = jnp.where(allowed, _dot_nt(kc, _q_rows(qp_ref[0, g])), NEG_INF)
            m = jnp.max(s_t, axis=0, keepdims=True)
            p = jnp.exp2(s_t - jnp.maximum(m, MASK_FLOOR))
            l = jnp.sum(p, axis=0, keepdims=True)
            inv = jnp.where(l > 0.0, 1.0 / l, 0.0)
            o_t = _dot(vct, p.astype(BF16)) * inv
            gate = _gate_row(gt_ref[0, 0], g, 0)
            outs.append(_heads_to_token_major(o_t[g * NSA_HEAD_DIM:(g + 1) * NSA_HEAD_DIM, :] * gate))
            pn = p * inv
            psum = pn[:, 0:TOK]
            for r in range(1, NSA_REP):
                psum = psum + pn[:, r * TOK:(r + 1) * TOK]
            ps_ref[pl.ds(8, nrows), :] = psum
            imp = ps_ref[pl.ds(7, nblk, stride=per), :] + ps_ref[pl.ds(8 + per - 1, nblk, stride=per), :]
            for c in range(per - 1):
                imp = imp + 2.0 * ps_ref[pl.ds(8 + c, nblk, stride=per), :]
            imps.append(jnp.where(forced, -jnp.inf, jnp.where(valid, imp, NEG_INF)))
        oc_ref[0] = jnp.concatenate(outs, axis=1)

        def pick(_, carry):
            nxt = []
            for imp_c in carry:
                mx = jnp.max(imp_c, axis=0, keepdims=True)
                idx = jnp.min(jnp.where(imp_c == mx, j_io, nblk), axis=0, keepdims=True)
                nxt.append(jnp.where(j_io == idx, -jnp.inf, imp_c))
            return tuple(nxt)

        picked = lax.fori_loop(0, min(NSA_TOPK, nsel) - N_FORCED, pick, tuple(imps))
        for g in range(NSA_KV_GROUPS):
            mask_ref[0, g, 0, 0:nblk, :] = jnp.where(valid & (picked[g] == -jnp.inf), 0.0, NEG_INF)
            if nblk < nsel:
                mask_ref[0, g, 0, nblk:nsel, :] = jnp.full((nsel - nblk, TOK), NEG_INF, F32)

    n_vis = (qi * TOK + TOK - NSA_CMP_BLOCK) // NSA_CMP_STRIDE + 1
    quarter = nbp // 4
    for k in range(1, 5):
        @pl.when((n_vis > (k - 1) * quarter) & (n_vis <= k * quarter))
        def _(k=k):
            attend(k * quarter)


def _cmp_topk(qp, kc, vct, gt, bsz, s):
    nt = s // TOK
    nbp = s // NSA_CMP_STRIDE
    nsel = s // NSA_SEL_BLOCK
    return pl.pallas_call(
        functools.partial(_cmp_topk_kernel, nbp=nbp, nsel=nsel),
        grid=(bsz, nt),
        in_specs=[
            pl.BlockSpec((1, NSA_KV_GROUPS, TOK, NSA_REP * LANES), lambda b, i: (b, 0, i, 0)),
            pl.BlockSpec((1, nbp, LANES), lambda b, i: (b, 0, 0)),
            pl.BlockSpec((1, LANES, nbp), lambda b, i: (b, 0, 0)),
            pl.BlockSpec((1, 1, GATE_ROWS, TOK), lambda b, i: (b, i, 0, 0)),
        ],
        out_specs=[
            pl.BlockSpec((1, TOK, NSA_Q_WIDTH), lambda b, i: (b, i, 0)),
            pl.BlockSpec((1, NSA_KV_GROUPS, 1, nsel, TOK), lambda b, i: (b, 0, i, 0, 0)),
        ],
        out_shape=[
            jax.ShapeDtypeStruct((bsz, s, NSA_Q_WIDTH), F32),
            jax.ShapeDtypeStruct((bsz, NSA_KV_GROUPS, nt, nsel, TOK), F32),
        ],
        scratch_shapes=[pltpu.VMEM((nbp + 8, TOK), F32)],
        compiler_params=_cparams(("parallel", "parallel")),
        name="nsa_cmp_topk",
    )(qp, kc, vct, gt)


def _window_kernel(qp_ref, k_ref, vt_ref, gt_ref, o_ref):
    qi = pl.program_id(1)
    ncol = NSA_REP * TOK
    wk = WIN_TILES * TOK

    def run(first_tile, interior):
        kwin = k_ref[0, pl.ds(pl.multiple_of(first_tile * TOK, TOK), wk), :]
        outs = []
        for g in range(NSA_KV_GROUPS):
            s_t = _dot_nt(kwin, _q_rows(qp_ref[0, g]))
            if interior:
                row = lax.broadcasted_iota(jnp.int32, (TOK, ncol), 0)
                col = lax.broadcasted_iota(jnp.int32, (TOK, ncol), 1) % TOK
                oldest = jnp.where(row > col, s_t[:TOK], NEG_INF)
                newest = jnp.where(row <= col, s_t[wk - TOK:], NEG_INF)
                s_t = jnp.concatenate([oldest, s_t[TOK:wk - TOK], newest], axis=0)
            else:
                key = first_tile * TOK + lax.broadcasted_iota(jnp.int32, (wk, ncol), 0)
                t_io = qi * TOK + lax.broadcasted_iota(jnp.int32, (wk, ncol), 1) % TOK
                s_t = jnp.where((key <= t_io) & (t_io - key < NSA_WINDOW), s_t, NEG_INF)
            m = jnp.max(s_t, axis=0, keepdims=True)
            p = jnp.exp2(s_t - m)
            l = jnp.sum(p, axis=0, keepdims=True)
            vt = jnp.concatenate([vt_ref[0, g, first_tile + k] for k in range(WIN_TILES)], axis=1)
            gate = _gate_row(gt_ref[0, 0], g, 2)
            outs.append(_heads_to_token_major(_dot(vt, p.astype(BF16)) * (gate / l)))
        o_ref[0] = jnp.concatenate(outs, axis=1)

    @pl.when(qi >= WIN_TILES - 1)
    def _():
        run(qi - (WIN_TILES - 1), True)

    @pl.when(qi < WIN_TILES - 1)
    def _():
        run(0, False)


def _window(qp, k, vt, gt, bsz, s):
    nt = s // TOK
    return pl.pallas_call(
        _window_kernel,
        grid=(bsz, nt),
        in_specs=[
            pl.BlockSpec((1, NSA_KV_GROUPS, TOK, NSA_REP * LANES), lambda b, i: (b, 0, i, 0)),
            pl.BlockSpec((1, s, LANES), lambda b, i: (b, 0, 0)),
            pl.BlockSpec((1, NSA_KV_GROUPS, nt, NSA_HEAD_DIM, TOK), lambda b, i: (b, 0, 0, 0, 0)),
            pl.BlockSpec((1, 1, GATE_ROWS, TOK), lambda b, i: (b, i, 0, 0)),
        ],
        out_specs=pl.BlockSpec((1, TOK, NSA_Q_WIDTH), lambda b, i: (b, i, 0)),
        out_shape=jax.ShapeDtypeStruct((bsz, s, NSA_Q_WIDTH), F32),
        compiler_params=_cparams(("parallel", "parallel")),
        name="nsa_window",
    )(qp, k, vt, gt)


def _selected_kernel(list_ref, cnt_ref, qp_ref, k_ref, vt_ref, bias_ref, gt_ref, o_ref, m_ref, acc_ref,
                     s_ref, p_ref, a_ref, qa_ref, *, nt, nsel):
    b = pl.program_id(0)
    qi = pl.program_id(1)
    ncol = NSA_REP * TOK
    blk_per_tile = SEL_TILE // NSA_SEL_BLOCK
    n_kt = nt // (SEL_TILE // TOK)
    n_q = nt // SEL_Q
    units = [(g, h) for g in range(NSA_KV_GROUPS) for h in range(SEL_Q)]
    m_ref[...] = jnp.full(m_ref.shape, NEG_INF, F32)
    acc_ref[...] = jnp.zeros(acc_ref.shape, F32)
    p_ref[...] = jnp.zeros(p_ref.shape, BF16)
    a_ref[...] = jnp.ones(a_ref.shape, F32)
    base = (b * n_q + qi) * n_kt
    count = cnt_ref[b * n_q + qi]
    n_diag = (qi * SEL_Q * TOK) // SEL_TILE

    lane = lax.broadcasted_iota(jnp.int32, (TOK, LANES), 1)
    n_var = qa_ref.shape[0]
    for g, h in units:
        bias_g = bias_ref[0, g, h]
        if nsel % LANES:
            bias_g = jnp.concatenate([bias_g, jnp.zeros((LANES - nsel % LANES, TOK), F32)], axis=0)
        own = (lane // NSA_HEAD_DIM) == g
        for v in range(n_var):
            src = bias_g[(v // 2) * LANES:(v // 2 + 1) * LANES, :].T
            if v % 2 != 1 - g:
                src = pltpu.roll(src, NSA_HEAD_DIM, 1)
            src = src.astype(BF16)
            for r in range(NSA_REP):
                q_r = qp_ref[0, g, h * TOK:(h + 1) * TOK, r * LANES:(r + 1) * LANES]
                qa_ref[v, g, h, r * TOK:(r + 1) * TOK, :] = jnp.where(own, q_r, src)

    def tile_at(i):
        return jnp.where(i < count, list_ref[base + jnp.minimum(i, n_kt - 1)], n_diag)

    def scores(j):
        v = (j * blk_per_tile) // NSA_HEAD_DIM
        for g in range(NSA_KV_GROUPS):
            kb = k_ref[0, g, pl.ds(pl.multiple_of(j * SEL_TILE, SEL_TILE), SEL_TILE), :]
            for h in range(SEL_Q):
                s_ref[g, h] = _dot_nt(kb, qa_ref[v, g, h])

    def values(j):
        for g, h in units:
            acc_ref[g, h] = acc_ref[g, h] * a_ref[g, h] + _dot(vt_ref[0, g, j], p_ref[g, h])

    def softmax(j, causal):
        for g, h in units:
            s_t = s_ref[g, h]
            if causal:
                t_io = ((qi * SEL_Q + h) * TOK
                        + lax.broadcasted_iota(jnp.int32, (SEL_TILE, ncol), 1) % TOK)
                key_pos = j * SEL_TILE + lax.broadcasted_iota(jnp.int32, (SEL_TILE, ncol), 0)
                s_t = jnp.where(key_pos <= t_io, s_t, NEG_INF)
            m_old = m_ref[g, h]
            m_new = jnp.maximum(m_old, jnp.max(s_t, axis=0, keepdims=True))
            alpha = jnp.exp2(m_old - m_new)
            p = jnp.exp2(s_t - jnp.maximum(m_new, MASK_FLOOR))
            m_ref[g, h] = m_new
            p_ref[g, h] = p.astype(BF16)
            a_ref[g, h] = alpha

    scores(tile_at(0))

    def body(i, c):
        values(tile_at(jnp.maximum(i - 1, 0)))
        softmax(tile_at(i), False)
        scores(tile_at(i + 1))
        return c

    lax.fori_loop(0, count, body, 0)
    values(tile_at(jnp.maximum(count - 1, 0)))
    softmax(n_diag, True)
    values(n_diag)
    for h in range(SEL_Q):
        outs = []
        for g in range(NSA_KV_GROUPS):
            acc = acc_ref[g, h]
            gate = _gate_row(gt_ref[0, h], g, 1)
            outs.append(_heads_to_token_major(acc[:NSA_HEAD_DIM] * (gate / acc[NSA_HEAD_DIM:NSA_HEAD_DIM + 1])))
        o_ref[0, h * TOK:(h + 1) * TOK, :] = jnp.concatenate(outs, axis=1)


def _selected(qp, k, vt, bias, gt, tile_list, tile_count, bsz, s):
    nt = s // TOK
    nsel = s // NSA_SEL_BLOCK
    ncol = NSA_REP * TOK
    unit = (NSA_KV_GROUPS, SEL_Q)
    grid_spec = pltpu.PrefetchScalarGridSpec(
        num_scalar_prefetch=2,
        grid=(bsz, nt // SEL_Q),
        in_specs=[
            pl.BlockSpec((1, NSA_KV_GROUPS, SEL_Q * TOK, NSA_REP * LANES), lambda b, i, tl, tc: (b, 0, i, 0)),
            pl.BlockSpec((1, NSA_KV_GROUPS, s, LANES), lambda b, i, tl, tc: (b, 0, 0, 0)),
            pl.BlockSpec((1, NSA_KV_GROUPS, s // SEL_TILE, SEL_V_ROWS, SEL_TILE),
                         lambda b, i, tl, tc: (b, 0, 0, 0, 0)),
            pl.BlockSpec((1, NSA_KV_GROUPS, SEL_Q, nsel, TOK), lambda b, i, tl, tc: (b, 0, i, 0, 0)),
            pl.BlockSpec((1, SEL_Q, GATE_ROWS, TOK), lambda b, i, tl, tc: (b, i, 0, 0)),
        ],
        out_specs=pl.BlockSpec((1, SEL_Q * TOK, NSA_Q_WIDTH), lambda b, i, tl, tc: (b, i, 0)),
        scratch_shapes=[pltpu.VMEM(unit + (1, ncol), F32),
                        pltpu.VMEM(unit + (SEL_V_ROWS, ncol), F32),
                        pltpu.VMEM(unit + (SEL_TILE, ncol), F32),
                        pltpu.VMEM(unit + (SEL_TILE, ncol), BF16),
                        pltpu.VMEM(unit + (1, ncol), F32),
                        pltpu.VMEM((max(nsel // NSA_HEAD_DIM, 1),) + unit + (ncol, LANES), BF16)],
    )
    return pl.pallas_call(
        functools.partial(_selected_kernel, nt=nt, nsel=nsel),
        grid_spec=grid_spec,
        out_shape=jax.ShapeDtypeStruct((bsz, s, NSA_Q_WIDTH), F32),
        compiler_params=_cparams(("parallel", "parallel")),
        name="nsa_selected",
    )(tile_list, tile_count, qp, k, vt, bias, gt)


def _active_tiles(bias, bsz, s):
    n_q = s // (SEL_Q * TOK)
    n_kt = s // SEL_TILE
    rows = SEL_TILE // NSA_SEL_BLOCK * TOK
    act = bias.reshape(bsz, NSA_KV_GROUPS, n_q, SEL_Q, n_kt, rows).max(axis=(1, 3, 5)) > -1.0
    n_diag = (jnp.arange(n_q) * SEL_Q * TOK) // SEL_TILE
    act = act & (jnp.arange(n_kt)[None, None, :] < n_diag[None, :, None])
    order = jnp.argsort(jnp.where(act, 0, 1), axis=-1, stable=True).astype(jnp.int32)
    return order.reshape(-1), jnp.sum(act, axis=-1, dtype=jnp.int32).reshape(-1)


def _ssd_kernel(z_ref, xbc_ref, dt_ref, cw_ref, cb_ref, dtb_ref, alog_ref, dskip_ref, nw_ref, e8_ref,
                tri_ref, y_ref, ext_ref, st_ref):
    @pl.when(pl.program_id(1) == 0)
    def _():
        ext_ref[pl.ds(0, 8), :] = jnp.zeros((8, SSM_CONV_DIM), F32)
        st_ref[...] = jnp.zeros_like(st_ref)

    xb = xbc_ref[...]
    ext_ref[pl.ds(8, TOK), :] = xb
    conv = cb_ref[...] + xb * cw_ref[SSM_CONV - 1:SSM_CONV, :]
    for k in range(SSM_CONV - 1):
        conv = conv + ext_ref[pl.ds(8 - (SSM_CONV - 1) + k, TOK), :] * cw_ref[k:k + 1, :]
    ext_ref[pl.ds(0, 8), :] = xb[TOK - 8:TOK, :]
    u = _silu(conv)
    xs = u[:, :SSM_INNER]
    gn = SSM_GROUPS * SSM_STATE
    bm = u[:, SSM_INNER:SSM_INNER + gn]
    cm = u[:, SSM_INNER + gn:]
    dt = jax.nn.softplus(dt_ref[...] + dtb_ref[...])
    da = dt * (-jnp.exp(alog_ref[...]))
    tri = tri_ref[...]
    e8 = e8_ref[...]
    cum = _dot_sel_l(tri, da)
    cum_t = cum.T
    cum_e = _dot_sel_r(cum, e8)
    dt_e = _dot_sel_r(dt, e8)
    xdt = xs * dt_e
    lane = lax.broadcasted_iota(jnp.int32, (TOK, LANES), 1)
    t_io = lax.broadcasted_iota(jnp.int32, (TOK, TOK), 0)
    s_io = lax.broadcasted_iota(jnp.int32, (TOK, TOK), 1)
    causal = s_io <= t_io
    hg = SSM_HEADS // SSM_GROUPS
    y_parts = []
    cbs = []
    for g in range(SSM_GROUPS):
        in_g = (lane // SSM_STATE) == g
        cbs.append(_dot_nt(jnp.where(in_g, cm, 0.0).astype(BF16), bm.astype(BF16)))
    for c in range(SSM_HEADS // 2):
        acc = jnp.zeros((TOK, LANES), F32)
        xpair = xdt[:, c * LANES:(c + 1) * LANES]
        for hh in range(2):
            h = 2 * c + hh
            seg = cum[:, h:h + 1] - cum_t[h:h + 1, :]
            decay = jnp.where(causal, jnp.exp(jnp.where(causal, seg, 0.0)), 0.0)
            mat = (cbs[h // hg] * decay).astype(BF16)
            xh = jnp.where((lane // SSM_HEAD_DIM) == hh, xpair, 0.0).astype(BF16)
            acc = acc + _dot(mat, xh)
        y_parts.append(acc)
    y_diag = jnp.concatenate(y_parts, axis=1)
    prev = st_ref[...]
    y_off = _dot(cm.astype(BF16), prev.astype(BF16)) * jnp.exp(cum_e)
    cum_last = cum_e[TOK - 1:TOK, :]
    xdec = (xdt * jnp.exp(cum_last - cum_e)).astype(BF16)
    new = _dot(bm.T.astype(BF16), xdec)
    row_g = lax.broadcasted_iota(jnp.int32, (LANES, SSM_INNER), 0) // SSM_STATE
    col_g = lax.broadcasted_iota(jnp.int32, (LANES, SSM_INNER), 1) // (hg * SSM_HEAD_DIM)
    st_ref[...] = prev * jnp.exp(cum_last) + jnp.where(row_g == col_g, new, 0.0)
    y = (y_diag + y_off + xs * dskip_ref[...]) * _silu(z_ref[...])
    gw = SSM_INNER // SSM_GROUPS
    outs = []
    for g in range(SSM_GROUPS):
        yg = y[:, g * gw:(g + 1) * gw]
        outs.append(_rms(yg, nw_ref[:, g * gw:(g + 1) * gw]))
    y_ref[...] = jnp.concatenate(outs, axis=1).astype(BF16)


def _ssd(proj, cw, cb, dtb, alog, dskip, nw, e8, tri, bsz, s):
    nt = s // TOK

    def col(c0, w):
        return pl.BlockSpec((TOK, w), lambda b, i: (b * nt + i, c0 // w))

    def full(shape):
        return pl.BlockSpec(shape, lambda b, i: (0,) * len(shape))

    return pl.pallas_call(
        _ssd_kernel,
        grid=(bsz, nt),
        in_specs=[col(COL_Z, SSM_INNER), col(COL_XBC, SSM_CONV_DIM), col(COL_DT, LANES),
                  full((SSM_CONV, SSM_CONV_DIM)), full((1, SSM_CONV_DIM)), full((1, LANES)), full((1, LANES)),
                  full((1, SSM_INNER)), full((1, SSM_INNER)), full((LANES, SSM_INNER)), full((TOK, TOK))],
        out_specs=pl.BlockSpec((TOK, SSM_INNER), lambda b, i: (b * nt + i, 0)),
        out_shape=jax.ShapeDtypeStruct((bsz * s, SSM_INNER), BF16),
        scratch_shapes=[pltpu.VMEM((TOK + 8, SSM_CONV_DIM), F32), pltpu.VMEM((LANES, SSM_INNER), F32)],
        compiler_params=_cparams(("parallel", "arbitrary")),
        name="ssd",
    )(proj, proj, proj, cw, cb, dtb, alog, dskip, nw, e8, tri)


def _hgrn2_kernel(q_ref, f_ref, i_ref, g_ref, lb_ref, nw_ref, tri_ref, y_ref, st_ref, kp_ref, bp_ref, vp_ref):
    @pl.when(pl.program_id(1) == 0)
    def _():
        st_ref[...] = jnp.zeros_like(st_ref)
        kp_ref[...] = jnp.zeros_like(kp_ref)
        bp_ref[...] = jnp.zeros_like(bp_ref)
        vp_ref[...] = jnp.zeros_like(vp_ref)

    nsub = TOK // HG_SUB
    in_sub = lax.broadcasted_iota(jnp.int32, (TOK, HG_KEY_DIM), 0) % HG_SUB
    tri = tri_ref[...]
    outs = []
    for h in range(HG_HEADS):
        sl = slice(h * HG_KEY_DIM, (h + 1) * HG_KEY_DIM)
        lb = lb_ref[:, sl]
        fr = f_ref[:, sl]
        q = _silu(q_ref[:, sl]) * (HG_KEY_DIM ** -0.5)
        log_f = jnp.log(lb + (1.0 - lb) * jax.nn.sigmoid(fr))
        k = (1.0 - lb) * jax.nn.sigmoid(-fr)
        v = i_ref[:, sl]
        bcum = _dot_sel_l(tri, log_f)
        st = st_ref[h]
        o = _dot_nt((q * jnp.exp(bcum)).astype(BF16), st.astype(BF16))
        ends = [bcum[(j + 1) * HG_SUB - 1:(j + 1) * HG_SUB, :] for j in range(nsub)]
        blk = (HG_SUB, HG_KEY_DIM)
        zero_blk = jnp.zeros(blk, BF16)
        e_prev = jnp.concatenate([jnp.zeros(blk, F32)] + [jnp.broadcast_to(ends[j], blk) for j in range(nsub - 1)],
                                 axis=0)
        e_own = jnp.concatenate([jnp.broadcast_to(ends[j], blk) for j in range(nsub)], axis=0)
        q_loc = (q * jnp.exp(bcum - e_prev)).astype(BF16)
        k_loc = k * jnp.exp(e_own - bcum)
        q_cols = []
        k_cols = []
        for i in range(1, nsub):
            q_cols.append(jnp.concatenate(
                [zero_blk] * i + [q_loc[i * HG_SUB:(i + 1) * HG_SUB]] + [zero_blk] * (nsub - 1 - i), axis=0))
            parts = []
            for j in range(i):
                kj = k_loc[j * HG_SUB:(j + 1) * HG_SUB]
                if j < i - 1:
                    kj = kj * jnp.exp(ends[i - 1] - ends[j])
                parts.append(kj.astype(BF16))
            k_cols.append(jnp.concatenate(parts + [zero_blk] * (nsub - i), axis=0))
        q_aug = jnp.concatenate(q_cols, axis=1)
        k_aug = jnp.concatenate(k_cols, axis=1)
        att = _dot_nt(q_aug, k_aug)
        o = o + _dot(att.astype(BF16), v.astype(BF16))
        kp_ref[pl.ds(HG_SUB, TOK), :] = k
        bp_ref[pl.ds(HG_SUB, TOK), :] = bcum
        vp_ref[pl.ds(HG_SUB, TOK), :] = v
        o = o + jnp.sum(q * k, axis=1, keepdims=True) * v
        for d in range(1, HG_SUB):
            ks = kp_ref[pl.ds(HG_SUB - d, TOK), :]
            bs = bp_ref[pl.ds(HG_SUB - d, TOK), :]
            vs = vp_ref[pl.ds(HG_SUB - d, TOK), :]
            ok = in_sub >= d
            w = jnp.sum(q * ks * jnp.exp(bcum - bs), axis=1, keepdims=True)
            o = o + jnp.where(ok, w * vs, 0.0)
        b_last = bcum[TOK - 1:TOK, :]
        kdec = (k * jnp.exp(b_last - bcum)).astype(BF16)
        st_ref[h] = st * jnp.exp(b_last) + _dot(v.T.astype(BF16), kdec)
        outs.append(_rms(o, nw_ref[...]) * _silu(g_ref[:, sl]))
    y_ref[...] = jnp.concatenate(outs, axis=1).astype(BF16)


def _hgrn2(proj, lb, nw, tri, bsz, s):
    nt = s // TOK

    def col(c0):
        return pl.BlockSpec((TOK, HG_WIDTH), lambda b, i: (b * nt + i, c0 // HG_WIDTH))

    def full(shape):
        return pl.BlockSpec(shape, lambda b, i: (0,) * len(shape))

    pad = pltpu.VMEM((TOK + HG_SUB, HG_KEY_DIM), F32)
    return pl.pallas_call(
        _hgrn2_kernel,
        grid=(bsz, nt),
        in_specs=[col(COL_HQ), col(COL_HF), col(COL_HI), col(COL_HG), full((1, HG_WIDTH)),
                  full((1, HG_VAL_DIM)), full((TOK, TOK))],
        out_specs=pl.BlockSpec((TOK, HG_WIDTH), lambda b, i: (b * nt + i, 0)),
        out_shape=jax.ShapeDtypeStruct((bsz * s, HG_WIDTH), BF16),
        scratch_shapes=[pltpu.VMEM((HG_HEADS, HG_VAL_DIM, HG_KEY_DIM), F32), pad, pad, pad],
        compiler_params=_cparams(("parallel", "arbitrary")),
        name="hgrn2",
    )(proj, proj, proj, proj, lb, nw, tri)


def _merge_kernel(ga_ref, gb_ref, gc_ref, oc_ref, os_ref, ow_ref, yb_ref, yc_ref, x_ref,
                  wa_ref, wb_ref, wc_ref, wo_ref, o_ref):
    nsa = oc_ref[...] + os_ref[...] + ow_ref[...]
    ya = _dot(nsa.astype(BF16), wa_ref[...])
    yb = _dot(yb_ref[...], wb_ref[...])
    yc = _dot(yc_ref[...], wc_ref[...])
    merged = (jax.nn.sigmoid(ga_ref[...].astype(F32)) * ya + jax.nn.sigmoid(gb_ref[...].astype(F32)) * yb
              + jax.nn.sigmoid(gc_ref[...].astype(F32)) * yc)
    o_ref[...] = x_ref[...] + _dot(merged.astype(BF16), wo_ref[...])


def _merge(gates, oc, os_, ow, yb, yc, x2, wa, wb, wc, wo):
    t = x2.shape[0]
    tm = min(512, t)

    def col(c0, w):
        return pl.BlockSpec((tm, w), lambda i: (i, c0 // w))

    def full(shape):
        return pl.BlockSpec(shape, lambda i: (0,) * len(shape))

    row512 = pl.BlockSpec((tm, 512), lambda i: (i, 0))
    return pl.pallas_call(
        _merge_kernel,
        grid=(t // tm,),
        in_specs=[col(COL_GA, D_MODEL), col(COL_GB, D_MODEL), col(COL_GC, D_MODEL),
                  row512, row512, row512, row512, row512,
                  pl.BlockSpec((tm, D_MODEL), lambda i: (i, 0)),
                  full((NSA_Q_WIDTH, D_MODEL)), full((SSM_INNER, D_MODEL)),
                  full((HG_WIDTH, D_MODEL)), full((D_MODEL, D_MODEL))],
        out_specs=pl.BlockSpec((tm, D_MODEL), lambda i: (i, 0)),
        out_shape=jax.ShapeDtypeStruct((t, D_MODEL), F32),
        compiler_params=_cparams(("parallel",)),
        name="merge",
    )(gates, gates, gates, oc, os_, ow, yb, yc, x2, wa, wb, wc, wo)


def _xa_mem_kernel(mem_ref, nm_ref, wk_ref, wv_ref, kn_ref, k_ref, v_ref):
    m = _rms(mem_ref[0], nm_ref[...]).astype(BF16)
    k = _dot(m, wk_ref[...])
    ks = [_rms(k[:, h * XA_HEAD_DIM:(h + 1) * XA_HEAD_DIM], kn_ref[...]) for h in range(XA_HEADS)]
    k_ref[0] = jnp.concatenate(ks, axis=1).astype(BF16)
    v_ref[0] = _dot(m, wv_ref[...]).astype(BF16)


def _xa_mem(mem, nm, wk, wv, kn):
    bsz, ml, _ = mem.shape

    def full(shape):
        return pl.BlockSpec(shape, lambda b: (0,) * len(shape))

    return pl.pallas_call(
        _xa_mem_kernel,
        grid=(bsz,),
        in_specs=[pl.BlockSpec((1, ml, D_MODEL), lambda b: (b, 0, 0)), full((1, D_MODEL)),
                  full((D_MODEL, XA_WIDTH)), full((D_MODEL, XA_WIDTH)), full((1, XA_HEAD_DIM))],
        out_specs=[pl.BlockSpec((1, ml, XA_WIDTH), lambda b: (b, 0, 0)),
                   pl.BlockSpec((1, ml, XA_WIDTH), lambda b: (b, 0, 0))],
        out_shape=[jax.ShapeDtypeStruct((bsz, ml, XA_WIDTH), BF16),
                   jax.ShapeDtypeStruct((bsz, ml, XA_WIDTH), BF16)],
        compiler_params=_cparams(("parallel",)),
        name="xa_mem",
    )(mem, nm, wk, wv, kn)


def _xa_kernel(x_ref, nx_ref, wq_ref, qn_ref, k_ref, v_ref, wo_ref, o_ref):
    x = x_ref[...]
    h = _rms(x, nx_ref[...]).astype(BF16)
    q = _dot(h, wq_ref[...])
    outs = []
    for hd in range(XA_HEADS):
        sl = slice(hd * XA_HEAD_DIM, (hd + 1) * XA_HEAD_DIM)
        qh = _rms(q[:, sl], qn_ref[...]).astype(BF16)
        sc = _dot_nt(qh, k_ref[0, :, sl]) * (XA_HEAD_DIM ** -0.5)
        sc = sc - jnp.max(sc, axis=-1, keepdims=True)
        p = jnp.exp(sc)
        p = p / jnp.sum(p, axis=-1, keepdims=True)
        outs.append(_dot(p.astype(BF16), v_ref[0, :, sl]))
    o = jnp.concatenate(outs, axis=1).astype(BF16)
    o_ref[...] = x + _dot(o, wo_ref[...])


def _xa(x2, nx, wq, qn, k, v, wo, s):
    t = x2.shape[0]
    tm = min(512, s)
    ml = k.shape[1]
    per_b = s // tm

    def full(shape):
        return pl.BlockSpec(shape, lambda i: (0,) * len(shape))

    return pl.pallas_call(
        _xa_kernel,
        grid=(t // tm,),
        in_specs=[pl.BlockSpec((tm, D_MODEL), lambda i: (i, 0)), full((1, D_MODEL)), full((D_MODEL, XA_WIDTH)),
                  full((1, XA_HEAD_DIM)),
                  pl.BlockSpec((1, ml, XA_WIDTH), lambda i: (i // per_b, 0, 0)),
                  pl.BlockSpec((1, ml, XA_WIDTH), lambda i: (i // per_b, 0, 0)),
                  full((XA_WIDTH, D_MODEL))],
        out_specs=pl.BlockSpec((tm, D_MODEL), lambda i: (i, 0)),
        out_shape=jax.ShapeDtypeStruct((t, D_MODEL), F32),
        compiler_params=_cparams(("parallel",)),
        name="cross_attn",
    )(x2, nx, wq, qn, k, v, wo)


def _ffn_kernel(x_ref, nw_ref, wg_ref, wu_ref, wd_ref, o_ref, h_ref, acc_ref):
    j = pl.program_id(1)

    @pl.when(j == 0)
    def _():
        h_ref[...] = _rms(x_ref[...], nw_ref[...]).astype(BF16)
        acc_ref[...] = jnp.zeros_like(acc_ref)

    h = h_ref[...]
    a = _silu(_dot(h, wg_ref[...])) * _dot(h, wu_ref[...])
    acc_ref[...] += _dot(a.astype(BF16), wd_ref[...])

    @pl.when(j == pl.num_programs(1) - 1)
    def _():
        o_ref[...] = x_ref[...] + acc_ref[...]


def _ffn(x2, nw, wg, wu, wd):
    t = x2.shape[0]
    tm = min(1024, t)
    th = FFN_HIDDEN // 2
    return pl.pallas_call(
        _ffn_kernel,
        grid=(t // tm, FFN_HIDDEN // th),
        in_specs=[pl.BlockSpec((tm, D_MODEL), lambda i, j: (i, 0)),
                  pl.BlockSpec((1, D_MODEL), lambda i, j: (0, 0)),
                  pl.BlockSpec((D_MODEL, th), lambda i, j: (0, j)),
                  pl.BlockSpec((D_MODEL, th), lambda i, j: (0, j)),
                  pl.BlockSpec((th, D_MODEL), lambda i, j: (j, 0))],
        out_specs=pl.BlockSpec((tm, D_MODEL), lambda i, j: (i, 0)),
        out_shape=jax.ShapeDtypeStruct((t, D_MODEL), F32),
        scratch_shapes=[pltpu.VMEM((tm, D_MODEL), BF16), pltpu.VMEM((tm, D_MODEL), F32)],
        compiler_params=_cparams(("parallel", "arbitrary")),
        name="ffn",
    )(x2, nw, wg, wu, wd)


def _pack_w_in(w):
    parts = {}
    off = 0
    names = ("q", "kc", "vc", "ks", "vs", "kw", "vw", "ng", "z", "xbc", "dt", "hq", "hf", "hi", "hg",
             "ga", "gb", "gc")
    for name, size in zip(names, SPLIT_SIZES):
        parts[name] = w[:, off:off + size]
        off += size

    def padded(a, width):
        return jnp.pad(a, ((0, 0), (0, width - a.shape[1])))

    cols = [parts["ga"], parts["gb"], parts["gc"], parts["q"], parts["z"], parts["hq"], parts["hf"],
            parts["hi"], parts["hg"], parts["xbc"], parts["kc"], parts["vc"], parts["ks"], parts["vs"],
            parts["kw"], parts["vw"], padded(parts["ng"], LANES), padded(parts["dt"], LANES)]
    packed = jnp.concatenate(cols, axis=1)
    return padded(packed, PROJ_WIDTH).astype(BF16)


def _rope_tables(pos):
    half = NSA_HEAD_DIM // 2
    inv = 1.0 / (ROPE_THETA ** (jnp.arange(half, dtype=F32) / half))
    ang = pos.astype(F32)[:, None] * inv[None, :]
    reps = LANES // half
    return jnp.tile(jnp.cos(ang), (1, reps)), jnp.tile(jnp.sin(ang), (1, reps))


def _block_diag2(w):
    z = jnp.zeros_like(w)
    return jnp.concatenate([jnp.concatenate([w, z], axis=-1), jnp.concatenate([z, w], axis=-1)], axis=-2)


def kernel(x, mem, norm_mix, w_in, nsa_q_norm, nsa_k_norm, nsa_cmp_pos_k, nsa_cmp_pos_v, nsa_cmp_w1_k, nsa_cmp_w2_k, nsa_cmp_w1_v, nsa_cmp_w2_v, w_nsa_o, ssm_conv_w, ssm_conv_b, ssm_dt_bias, ssm_a_log, ssm_d, ssm_norm, w_ssm_o, hg_lb_logits, hg_norm, w_hg_o, w_out, norm_xa, norm_mem, xa_w_q, xa_w_k, xa_w_v, xa_q_norm, xa_k_norm, xa_w_o, norm_ffn, ffn_w_gate, ffn_w_up, ffn_w_down):
    bsz, s, d = x.shape
    depth = w_in.shape[0]
    assert d == D_MODEL and s % (8 * TOK) == 0 and s // TOK <= TOK
    t = bsz * s
    hd = NSA_HEAD_DIM

    cos_t, sin_t = _rope_tables(jnp.arange(s))
    nbp = s // NSA_CMP_STRIDE
    cos_c, sin_c = _rope_tables(jnp.arange(nbp) * NSA_CMP_STRIDE + NSA_CMP_BLOCK - 1)
    tri = jnp.tril(jnp.ones((TOK, TOK), F32)).astype(BF16)
    rows = np.arange(LANES)[:, None]
    cols = np.arange(NSA_Q_WIDTH)[None, :]
    e8 = jnp.asarray(rows == cols // SSM_HEAD_DIM, F32).astype(BF16)

    lb_sm = jax.nn.softmax(hg_lb_logits.astype(F32), axis=0)
    lb_all = jnp.cumsum(lb_sm, axis=0) - lb_sm[0:1]

    def pad_lanes(v):
        return jnp.pad(v, (0, LANES - v.shape[0]))[None, :]

    x2 = x.reshape(t, d)
    for l in range(depth):
        gates, proj = _in_proj(x2, norm_mix[l][None, :], _pack_w_in(w_in[l]))

        qn2 = jnp.tile(nsa_q_norm[l], 2)[None, :]
        kn2 = jnp.tile(nsa_k_norm[l], 2)[None, :]
        qp, ksn, kwn, vst, vwt, gt = _nsa_prep(proj, cos_t, sin_t, qn2, kn2, bsz, s)
        w1k = _block_diag2(nsa_cmp_w1_k[l].reshape(NSA_CMP_BLOCK, hd, hd)).astype(BF16)
        w1v = _block_diag2(nsa_cmp_w1_v[l].reshape(NSA_CMP_BLOCK, hd, hd)).astype(BF16)
        kc, vct = _compress(proj, jnp.tile(nsa_cmp_pos_k[l], (1, 2)), jnp.tile(nsa_cmp_pos_v[l], (1, 2)),
                            w1k, _block_diag2(nsa_cmp_w2_k[l]).astype(BF16),
                            w1v, _block_diag2(nsa_cmp_w2_v[l]).astype(BF16), kn2, cos_c, sin_c, bsz, s)
        o_c, sel_bias = _cmp_topk(qp, kc, vct, gt, bsz, s)
        tile_list, tile_count = _active_tiles(sel_bias, bsz, s)
        o_s = _selected(qp, ksn, vst, sel_bias, gt, tile_list, tile_count, bsz, s)
        o_w = _window(qp, kwn, vwt, gt, bsz, s)

        y_b = _ssd(proj, ssm_conv_w[l], ssm_conv_b[l][None, :], pad_lanes(ssm_dt_bias[l]),
                   pad_lanes(ssm_a_log[l]), jnp.repeat(ssm_d[l], SSM_HEAD_DIM)[None, :],
                   ssm_norm[l][None, :], e8, tri, bsz, s)
        y_c = _hgrn2(proj, lb_all[l][None, :], hg_norm[l][None, :], tri, bsz, s)

        x2 = _merge(gates, o_c.reshape(t, -1), o_s.reshape(t, -1), o_w.reshape(t, -1), y_b, y_c, x2,
                    w_nsa_o[l].astype(BF16), w_ssm_o[l].astype(BF16), w_hg_o[l].astype(BF16),
                    w_out[l].astype(BF16))

        k_m, v_m = _xa_mem(mem, norm_mem[l][None, :], xa_w_k[l].astype(BF16), xa_w_v[l].astype(BF16),
                           xa_k_norm[l][None, :])
        x2 = _xa(x2, norm_xa[l][None, :], xa_w_q[l].astype(BF16), xa_q_norm[l][None, :], k_m, v_m,
                 xa_w_o[l].astype(BF16), s)
        x2 = _ffn(x2, norm_ffn[l][None, :], ffn_w_gate[l].astype(BF16), ffn_w_up[l].astype(BF16),
                  ffn_w_down[l].astype(BF16))
    return x2.reshape(bsz, s, d)
```

```python
import functools
import math

import numpy as np
import jax
import jax.numpy as jnp
from jax import lax
from jax.experimental import pallas as pl
from jax.experimental.pallas import tpu as pltpu

F32 = jnp.float32
BF16 = jnp.bfloat16

D_MODEL = 1024
NORM_EPS = 1e-6
ROPE_THETA = 10000.0
NEG_INF = -1e30
FORCE_SCORE = 1e9

NSA_HEADS = 8
NSA_KV_GROUPS = 2
NSA_REP = NSA_HEADS // NSA_KV_GROUPS
NSA_HEAD_DIM = 64
NSA_CMP_BLOCK = 32
NSA_CMP_STRIDE = 16
NSA_SEL_BLOCK = 64
NSA_TOPK = 16
N_FORCED = 3
NSA_WINDOW = 512
NSA_Q_WIDTH = NSA_HEADS * NSA_HEAD_DIM
NSA_KV_WIDTH = NSA_KV_GROUPS * NSA_HEAD_DIM

SSM_INNER = 512
SSM_HEAD_DIM = 64
SSM_HEADS = SSM_INNER // SSM_HEAD_DIM
SSM_GROUPS = 2
SSM_STATE = 64
SSM_CONV = 4
SSM_CONV_DIM = SSM_INNER + 2 * SSM_GROUPS * SSM_STATE

HG_HEADS = 4
HG_KEY_DIM = 128
HG_VAL_DIM = 128
HG_WIDTH = HG_HEADS * HG_KEY_DIM
HG_SUB = 16

XA_HEADS = 4
XA_HEAD_DIM = 128
XA_WIDTH = XA_HEADS * XA_HEAD_DIM

FFN_HIDDEN = -(-(8 * D_MODEL) // (3 * 256)) * 256

SPLIT_SIZES = (
    NSA_Q_WIDTH, NSA_KV_WIDTH, NSA_KV_WIDTH, NSA_KV_WIDTH, NSA_KV_WIDTH, NSA_KV_WIDTH, NSA_KV_WIDTH,
    3 * NSA_HEADS, SSM_INNER, SSM_CONV_DIM, SSM_HEADS, HG_WIDTH, HG_WIDTH, HG_HEADS * HG_VAL_DIM,
    HG_HEADS * HG_VAL_DIM, D_MODEL, D_MODEL, D_MODEL,
)

LANES = 128
TOK = 128
GATE_ROWS = 32
PREP_TILES = 4
SEL_Q = 4
SEL_TILE = 512
SEL_V_ROWS = NSA_HEAD_DIM + 16
WIN_TILES = NSA_WINDOW // TOK + 1
LOG2_E = math.log2(math.e)
MASK_FLOOR = -1e20
VMEM_LIMIT = 56 * 1024 * 1024

PROJ_WIDTH = 8192
GATE_WIDTH = 3 * D_MODEL
COL_GA, COL_GB, COL_GC = 0, 1024, 2048
COL_Q = 0
COL_Z = 512
COL_HQ, COL_HF, COL_HI, COL_HG = 1024, 1536, 2048, 2560
COL_XBC = 3072
COL_KC, COL_VC, COL_KS, COL_VS, COL_KW, COL_VW = 3840, 3968, 4096, 4224, 4352, 4480
COL_NG = 4608
COL_DT = 4736


def _cparams(sem):
    return pltpu.CompilerParams(dimension_semantics=sem, vmem_limit_bytes=VMEM_LIMIT)


def _dot(a, b):
    return jnp.dot(a, b, preferred_element_type=F32)


def _dot_nt(a, b):
    return lax.dot_general(a, b, (((1,), (1,)), ((), ())), preferred_element_type=F32)


def _split3(a):
    hi = a.astype(BF16)
    r1 = a - hi.astype(F32)
    mid = r1.astype(BF16)
    lo = (r1 - mid.astype(F32)).astype(BF16)
    return hi, mid, lo


def _dot_sel_r(a, sel):
    hi, mid, lo = _split3(a)
    return _dot(hi, sel) + _dot(mid, sel) + _dot(lo, sel)


def _dot_sel_l(sel, a):
    hi, mid, lo = _split3(a)
    return _dot(sel, hi) + _dot(sel, mid) + _dot(sel, lo)


def _silu(x):
    return x * jax.nn.sigmoid(x)


def _rms(x, w):
    return x * lax.rsqrt(jnp.mean(x * x, axis=-1, keepdims=True) + NORM_EPS) * w


def _inproj_kernel(x_ref, nw_ref, w_ref, og_ref, of_ref, h_ref, *, n_gate):
    j = pl.program_id(1)

    @pl.when(j == 0)
    def _():
        h_ref[...] = _rms(x_ref[...], nw_ref[...]).astype(BF16)

    @pl.when(j < n_gate)
    def _():
        og_ref[...] = _dot(h_ref[...], w_ref[...]).astype(BF16)

    @pl.when(j >= n_gate)
    def _():
        of_ref[...] = _dot(h_ref[...], w_ref[...])


def _in_proj(x2, norm_w, w_packed):
    t = x2.shape[0]
    tm = min(2048, t)
    tn = 1024
    n_gate = GATE_WIDTH // tn
    return pl.pallas_call(
        functools.partial(_inproj_kernel, n_gate=n_gate),
        grid=(t // tm, PROJ_WIDTH // tn),
        in_specs=[
            pl.BlockSpec((tm, D_MODEL), lambda i, j: (i, 0)),
            pl.BlockSpec((1, D_MODEL), lambda i, j: (0, 0)),
            pl.BlockSpec((D_MODEL, tn), lambda i, j: (0, j)),
        ],
        out_specs=[pl.BlockSpec((tm, tn), lambda i, j: (i, jnp.minimum(j, n_gate - 1))),
                   pl.BlockSpec((tm, tn), lambda i, j: (i, jnp.maximum(j - n_gate, 0)))],
        out_shape=[jax.ShapeDtypeStruct((t, GATE_WIDTH), BF16),
                   jax.ShapeDtypeStruct((t, PROJ_WIDTH - GATE_WIDTH), F32)],
        scratch_shapes=[pltpu.VMEM((tm, D_MODEL), BF16)],
        compiler_params=_cparams(("parallel", "arbitrary")),
        name="in_proj",
    )(x2, norm_w, w_packed)


def _norm_rope_pair(x, w, cos, sin):
    lane = lax.broadcasted_iota(jnp.int32, x.shape, 1)
    lo = lane < NSA_HEAD_DIM
    sq = x * x
    s_lo = jnp.sum(jnp.where(lo, sq, 0.0), axis=1, keepdims=True)
    s_hi = jnp.sum(jnp.where(lo, 0.0, sq), axis=1, keepdims=True)
    ms = jnp.where(lo, s_lo, s_hi) * (1.0 / NSA_HEAD_DIM)
    y = x * lax.rsqrt(ms + NORM_EPS) * w
    half = NSA_HEAD_DIM // 2
    first = (lane % NSA_HEAD_DIM) < half
    rot = jnp.where(first, -pltpu.roll(y, LANES - half, 1), pltpu.roll(y, half, 1))
    return y * cos + rot * sin


def _nsa_prep_kernel(q_ref, ks_ref, vs_ref, kw_ref, vw_ref, ng_ref, cos_ref, sin_ref, qn_ref, kn_ref,
                     qp_ref, ksn_ref, kwn_ref, vst_ref, vwt_ref, gt_ref):
    lane = lax.broadcasted_iota(jnp.int32, (TOK, LANES), 1)
    row = lax.broadcasted_iota(jnp.int32, (TOK, LANES), 0)
    scale = NSA_HEAD_DIM ** -0.5 * LOG2_E
    ones_rows = (lax.broadcasted_iota(jnp.int32, (SEL_V_ROWS - NSA_HEAD_DIM, TOK), 0) == 0).astype(BF16)
    for u in range(PREP_TILES):
        rows = slice(u * TOK, (u + 1) * TOK)
        cos = cos_ref[rows, :]
        sin = sin_ref[rows, :]
        for c in range(NSA_HEADS // 2):
            y = _norm_rope_pair(q_ref[rows, c * LANES:(c + 1) * LANES], qn_ref[...], cos, sin) * scale
            y_sw = pltpu.roll(y, NSA_HEAD_DIM, 1)
            for hh in range(2):
                h = 2 * c + hh
                g, r = divmod(h, NSA_REP)
                src = y if hh == g else y_sw
                keep = (lane // NSA_HEAD_DIM) == g
                qp_ref[0, g, rows, r * LANES:(r + 1) * LANES] = jnp.where(keep, src, 0.0).astype(BF16)
        ks_n = _norm_rope_pair(ks_ref[rows, :], kn_ref[...], cos, sin)
        tile = pl.program_id(1) * PREP_TILES + u
        blk = (tile * (TOK // NSA_SEL_BLOCK) + row // NSA_SEL_BLOCK) % NSA_HEAD_DIM
        onehot = ((lane % NSA_HEAD_DIM) == blk).astype(F32)
        for g in range(NSA_KV_GROUPS):
            ksn_ref[0, g, rows, :] = jnp.where((lane // NSA_HEAD_DIM) == g, ks_n, onehot).astype(BF16)
        kwn_ref[0, rows, :] = _norm_rope_pair(kw_ref[rows, :], kn_ref[...], cos, sin).astype(BF16)
        vs_t = vs_ref[rows, :].T.astype(BF16)
        vw_t = vw_ref[rows, :].T.astype(BF16)
        for g in range(NSA_KV_GROUPS):
            vst_ref[0, g, 0, 0:NSA_HEAD_DIM, rows] = vs_t[g * NSA_HEAD_DIM:(g + 1) * NSA_HEAD_DIM, :]
            vst_ref[0, g, 0, NSA_HEAD_DIM:SEL_V_ROWS, rows] = ones_rows
            vwt_ref[0, g, u] = vw_t[g * NSA_HEAD_DIM:(g + 1) * NSA_HEAD_DIM, :]
        gt_ref[0, u] = jax.nn.sigmoid(ng_ref[rows, :]).T[0:GATE_ROWS, :]


def _nsa_prep(proj, cos_t, sin_t, qn2, kn2, bsz, s):
    tq = PREP_TILES * TOK
    n_step = s // tq

    def col(c0, w):
        return pl.BlockSpec((tq, w), lambda b, i: (b * n_step + i, c0 // w))

    tab = pl.BlockSpec((tq, LANES), lambda b, i: (i, 0))
    vec = pl.BlockSpec((1, LANES), lambda b, i: (0, 0))
    return pl.pallas_call(
        _nsa_prep_kernel,
        grid=(bsz, n_step),
        in_specs=[col(COL_Q, NSA_Q_WIDTH), col(COL_KS, LANES), col(COL_VS, LANES), col(COL_KW, LANES),
                  col(COL_VW, LANES), col(COL_NG, LANES), tab, tab, vec, vec],
        out_specs=[
            pl.BlockSpec((1, NSA_KV_GROUPS, tq, NSA_REP * LANES), lambda b, i: (b, 0, i, 0)),
            pl.BlockSpec((1, NSA_KV_GROUPS, tq, LANES), lambda b, i: (b, 0, i, 0)),
            pl.BlockSpec((1, tq, LANES), lambda b, i: (b, i, 0)),
            pl.BlockSpec((1, NSA_KV_GROUPS, 1, SEL_V_ROWS, SEL_TILE), lambda b, i: (b, 0, i, 0, 0)),
            pl.BlockSpec((1, NSA_KV_GROUPS, PREP_TILES, NSA_HEAD_DIM, TOK), lambda b, i: (b, 0, i, 0, 0)),
            pl.BlockSpec((1, PREP_TILES, GATE_ROWS, TOK), lambda b, i: (b, i, 0, 0)),
        ],
        out_shape=[
            jax.ShapeDtypeStruct((bsz, NSA_KV_GROUPS, s, NSA_REP * LANES), BF16),
            jax.ShapeDtypeStruct((bsz, NSA_KV_GROUPS, s, LANES), BF16),
            jax.ShapeDtypeStruct((bsz, s, LANES), BF16),
            jax.ShapeDtypeStruct((bsz, NSA_KV_GROUPS, s // SEL_TILE, SEL_V_ROWS, SEL_TILE), BF16),
            jax.ShapeDtypeStruct((bsz, NSA_KV_GROUPS, s // TOK, NSA_HEAD_DIM, TOK), BF16),
            jax.ShapeDtypeStruct((bsz, s // TOK, GATE_ROWS, TOK), F32),
        ],
        compiler_params=_cparams(("parallel", "parallel")),
        name="nsa_prep",
    )(proj, proj, proj, proj, proj, proj, cos_t, sin_t, qn2, kn2)


def _compress_kernel(kc_ref, vc_ref, posk_ref, posv_ref, w1k_ref, w2k_ref, w1v_ref, w2v_ref, kn_ref,
                     cos_ref, sin_ref, kco_ref, vct_ref, sh_ref, *, nbp):
    half_blk = NSA_CMP_BLOCK // 2

    def compress(x_ref, pos_ref, w1_ref, w2_ref):
        acc_a = jnp.zeros((nbp, LANES), F32)
        acc_b = jnp.zeros((nbp, LANES), F32)
        for l in range(half_blk):
            xl = x_ref[pl.ds(l, nbp, stride=NSA_CMP_STRIDE), :]
            acc_a = acc_a + _dot((xl + pos_ref[l:l + 1, :]).astype(BF16), w1_ref[l])
            acc_b = acc_b + _dot((xl + pos_ref[half_blk + l:half_blk + l + 1, :]).astype(BF16),
                                 w1_ref[half_blk + l])
        sh_ref[pl.ds(0, nbp), :] = acc_b
        sh_ref[pl.ds(nbp, 8), :] = jnp.zeros((8, LANES), F32)
        hid = acc_a + sh_ref[pl.ds(1, nbp), :]
        return _dot(_silu(hid).astype(BF16), w2_ref[...])

    kc = compress(kc_ref, posk_ref, w1k_ref, w2k_ref)
    kco_ref[0] = _norm_rope_pair(kc, kn_ref[...], cos_ref[...], sin_ref[...]).astype(BF16)
    vc = compress(vc_ref, posv_ref, w1v_ref, w2v_ref)
    vct_ref[0] = vc.T.astype(BF16)


def _compress(proj, posk, posv, w1k, w2k, w1v, w2v, kn2, cos_c, sin_c, bsz, s):
    nbp = s // NSA_CMP_STRIDE

    def full(shape):
        return pl.BlockSpec(shape, lambda b: (0,) * len(shape))

    return pl.pallas_call(
        functools.partial(_compress_kernel, nbp=nbp),
        grid=(bsz,),
        in_specs=[
            pl.BlockSpec((s, LANES), lambda b: (b, COL_KC // LANES)),
            pl.BlockSpec((s, LANES), lambda b: (b, COL_VC // LANES)),
            full((NSA_CMP_BLOCK, LANES)), full((NSA_CMP_BLOCK, LANES)),
            full((NSA_CMP_BLOCK, LANES, LANES)), full((LANES, LANES)),
            full((NSA_CMP_BLOCK, LANES, LANES)), full((LANES, LANES)),
            full((1, LANES)), full((nbp, LANES)), full((nbp, LANES)),
        ],
        out_specs=[
            pl.BlockSpec((1, nbp, LANES), lambda b: (b, 0, 0)),
            pl.BlockSpec((1, LANES, nbp), lambda b: (b, 0, 0)),
        ],
        out_shape=[
            jax.ShapeDtypeStruct((bsz, nbp, LANES), BF16),
            jax.ShapeDtypeStruct((bsz, LANES, nbp), BF16),
        ],
        scratch_shapes=[pltpu.VMEM((nbp + 8, LANES), F32)],
        compiler_params=_cparams(("parallel",)),
        name="nsa_compress",
    )(proj, proj, posk, posv, w1k, w2k, w1v, w2v, kn2, cos_c, sin_c)


def _q_rows(qp):
    return jnp.concatenate([qp[:, r * LANES:(r + 1) * LANES] for r in range(NSA_REP)], axis=0)


def _gate_row(gt, g, branch):
    rows = [(g * NSA_REP + r) * 3 + branch for r in range(NSA_REP)]
    return jnp.concatenate([gt[i:i + 1, :] for i in rows], axis=1)


def _heads_to_token_major(acc_g):
    outs = []
    for p in range(NSA_REP // 2):
        blk = jnp.concatenate([acc_g[:, (2 * p) * TOK:(2 * p + 1) * TOK],
                               acc_g[:, (2 * p + 1) * TOK:(2 * p + 2) * TOK]], axis=0)
        outs.append(blk.T)
    return jnp.concatenate(outs, axis=1)


def _cmp_topk_kernel(qp_ref, kc_ref, vct_ref, gt_ref, oc_ref, mask_ref, ps_ref, *, nbp, nsel):
    qi = pl.program_id(1)
    ncol = NSA_REP * TOK
    per = NSA_SEL_BLOCK // NSA_CMP_STRIDE

    def attend(nrows):
        nblk = nrows // per
        j_io = lax.broadcasted_iota(jnp.int32, (nblk, TOK), 0)
        t_sel = qi * TOK + lax.broadcasted_iota(jnp.int32, (nblk, TOK), 1)
        cur = t_sel // NSA_SEL_BLOCK
        forced = (j_io == 0) | (j_io == cur) | (j_io == cur - 1)
        valid = j_io * NSA_SEL_BLOCK <= t_sel
        rel = (lax.broadcasted_iota(jnp.int32, (nrows, ncol), 0) * NSA_CMP_STRIDE + (NSA_CMP_BLOCK - 1)
               - lax.broadcasted_iota(jnp.int32, (nrows, ncol), 1) % TOK)
        allowed = rel <= qi * TOK
        kc = kc_ref[0, 0:nrows, :]
        vct = vct_ref[0, :, 0:nrows]
        for g in range(NSA_KV_GROUPS):
            ps_ref[g, pl.ds(0, 8), :] = jnp.zeros((8, TOK), F32)
        scores = [_dot_nt(kc, _q_rows(qp_ref[0, g])) for g in range(NSA_KV_GROUPS)]
        outs = []
        imps = []
        for g in range(NSA_KV_GROUPS):
            s_t = jnp.where(allowed, scores[g], NEG_INF)
            m = jnp.max(s_t, axis=0, keepdims=True)
            p = jnp.exp2(s_t - jnp.maximum(m, MASK_FLOOR))
            l = jnp.sum(p, axis=0, keepdims=True)
            inv = jnp.where(l > 0.0, 1.0 / l, 0.0)
            o_t = _dot(vct, p.astype(BF16)) * inv
            gate = _gate_row(gt_ref[0, 0], g, 0)
            outs.append(_heads_to_token_major(o_t[g * NSA_HEAD_DIM:(g + 1) * NSA_HEAD_DIM, :] * gate))
            pn = p * inv
            psum = pn[:, 0:TOK]
            for r in range(1, NSA_REP):
                psum = psum + pn[:, r * TOK:(r + 1) * TOK]
            ps_ref[g, pl.ds(8, nrows), :] = psum
            imp = (ps_ref[g, pl.ds(7, nblk, stride=per), :]
                   + ps_ref[g, pl.ds(8 + per - 1, nblk, stride=per), :])
            for c in range(per - 1):
                imp = imp + 2.0 * ps_ref[g, pl.ds(8 + c, nblk, stride=per), :]
            imps.append(jnp.where(forced, -jnp.inf, jnp.where(valid, imp, NEG_INF)))
        oc_ref[0] = jnp.concatenate(outs, axis=1)

        def pick(_, carry):
            nxt = []
            for imp_c in carry:
                mx = jnp.max(imp_c, axis=0, keepdims=True)
                idx = jnp.min(jnp.where(imp_c == mx, j_io, nblk), axis=0, keepdims=True)
                nxt.append(jnp.where(j_io == idx, -jnp.inf, imp_c))
            return tuple(nxt)

        picked = lax.fori_loop(0, min(NSA_TOPK, nsel) - N_FORCED, pick, tuple(imps))
        for g in range(NSA_KV_GROUPS):
            mask_ref[0, g, 0, 0:nblk, :] = jnp.where(valid & (picked[g] == -jnp.inf), 0.0, NEG_INF)
            if nblk < nsel:
                mask_ref[0, g, 0, nblk:nsel, :] = jnp.full((nsel - nblk, TOK), NEG_INF, F32)

    n_vis = (qi * TOK + TOK - NSA_CMP_BLOCK) // NSA_CMP_STRIDE + 1
    quarter = nbp // 4
    for k in range(1, 5):
        @pl.when((n_vis > (k - 1) * quarter) & (n_vis <= k * quarter))
        def _(k=k):
            attend(k * quarter)


def _cmp_topk(qp, kc, vct, gt, bsz, s):
    nt = s // TOK
    nbp = s // NSA_CMP_STRIDE
    nsel = s // NSA_SEL_BLOCK
    return pl.pallas_call(
        functools.partial(_cmp_topk_kernel, nbp=nbp, nsel=nsel),
        grid=(bsz, nt),
        in_specs=[
            pl.BlockSpec((1, NSA_KV_GROUPS, TOK, NSA_REP * LANES), lambda b, i: (b, 0, i, 0)),
            pl.BlockSpec((1, nbp, LANES), lambda b, i: (b, 0, 0)),
            pl.BlockSpec((1, LANES, nbp), lambda b, i: (b, 0, 0)),
            pl.BlockSpec((1, 1, GATE_ROWS, TOK), lambda b, i: (b, i, 0, 0)),
        ],
        out_specs=[
            pl.BlockSpec((1, TOK, NSA_Q_WIDTH), lambda b, i: (b, i, 0)),
            pl.BlockSpec((1, NSA_KV_GROUPS, 1, nsel, TOK), lambda b, i: (b, 0, i, 0, 0)),
        ],
        out_shape=[
            jax.ShapeDtypeStruct((bsz, s, NSA_Q_WIDTH), F32),
            jax.ShapeDtypeStruct((bsz, NSA_KV_GROUPS, nt, nsel, TOK), F32),
        ],
        scratch_shapes=[pltpu.VMEM((NSA_KV_GROUPS, nbp + 8, TOK), F32)],
        compiler_params=_cparams(("parallel", "parallel")),
        name="nsa_cmp_topk",
    )(qp, kc, vct, gt)


def _window_kernel(qp_ref, k_ref, vt_ref, gt_ref, o_ref):
    qi = pl.program_id(1)
    ncol = NSA_REP * TOK
    wk = WIN_TILES * TOK

    def run(first_tile, interior):
        kwin = k_ref[0, pl.ds(pl.multiple_of(first_tile * TOK, TOK), wk), :]
        scores = [_dot_nt(kwin, _q_rows(qp_ref[0, g])) for g in range(NSA_KV_GROUPS)]
        probs = []
        scales = []
        for g in range(NSA_KV_GROUPS):
            s_t = scores[g]
            if interior:
                row = lax.broadcasted_iota(jnp.int32, (TOK, ncol), 0)
                col = lax.broadcasted_iota(jnp.int32, (TOK, ncol), 1) % TOK
                oldest = jnp.where(row > col, s_t[:TOK], NEG_INF)
                newest = jnp.where(row <= col, s_t[wk - TOK:], NEG_INF)
                s_t = jnp.concatenate([oldest, s_t[TOK:wk - TOK], newest], axis=0)
            else:
                key = first_tile * TOK + lax.broadcasted_iota(jnp.int32, (wk, ncol), 0)
                t_io = qi * TOK + lax.broadcasted_iota(jnp.int32, (wk, ncol), 1) % TOK
                s_t = jnp.where((key <= t_io) & (t_io - key < NSA_WINDOW), s_t, NEG_INF)
            m = jnp.max(s_t, axis=0, keepdims=True)
            p = jnp.exp2(s_t - m)
            l = jnp.sum(p, axis=0, keepdims=True)
            probs.append(p.astype(BF16))
            scales.append(_gate_row(gt_ref[0, 0], g, 2) / l)
        outs = []
        for g in range(NSA_KV_GROUPS):
            vt = jnp.concatenate([vt_ref[0, g, first_tile + k] for k in range(WIN_TILES)], axis=1)
            outs.append(_heads_to_token_major(_dot(vt, probs[g]) * scales[g]))
        o_ref[0] = jnp.concatenate(outs, axis=1)

    @pl.when(qi >= WIN_TILES - 1)
    def _():
        run(qi - (WIN_TILES - 1), True)

    @pl.when(qi < WIN_TILES - 1)
    def _():
        run(0, False)


def _window(qp, k, vt, gt, bsz, s):
    nt = s // TOK
    return pl.pallas_call(
        _window_kernel,
        grid=(bsz, nt),
        in_specs=[
            pl.BlockSpec((1, NSA_KV_GROUPS, TOK, NSA_REP * LANES), lambda b, i: (b, 0, i, 0)),
            pl.BlockSpec((1, s, LANES), lambda b, i: (b, 0, 0)),
            pl.BlockSpec((1, NSA_KV_GROUPS, nt, NSA_HEAD_DIM, TOK), lambda b, i: (b, 0, 0, 0, 0)),
            pl.BlockSpec((1, 1, GATE_ROWS, TOK), lambda b, i: (b, i, 0, 0)),
        ],
        out_specs=pl.BlockSpec((1, TOK, NSA_Q_WIDTH), lambda b, i: (b, i, 0)),
        out_shape=jax.ShapeDtypeStruct((bsz, s, NSA_Q_WIDTH), F32),
        compiler_params=_cparams(("parallel", "parallel")),
        name="nsa_window",
    )(qp, k, vt, gt)


def _selected_kernel(list_ref, cnt_ref, qp_ref, k_ref, vt_ref, bias_ref, gt_ref, o_ref, m_ref, acc_ref,
                     s_ref, p_ref, a_ref, qa_ref, *, nt, nsel):
    b = pl.program_id(0)
    qi = pl.program_id(1)
    ncol = NSA_REP * TOK
    blk_per_tile = SEL_TILE // NSA_SEL_BLOCK
    n_kt = nt // (SEL_TILE // TOK)
    n_q = nt // SEL_Q
    units = [(g, h) for g in range(NSA_KV_GROUPS) for h in range(SEL_Q)]
    m_ref[...] = jnp.full(m_ref.shape, NEG_INF, F32)
    acc_ref[...] = jnp.zeros(acc_ref.shape, F32)
    p_ref[...] = jnp.zeros(p_ref.shape, BF16)
    a_ref[...] = jnp.ones(a_ref.shape, F32)
    base = (b * n_q + qi) * n_kt
    count = cnt_ref[b * n_q + qi]
    n_diag = (qi * SEL_Q * TOK) // SEL_TILE

    lane = lax.broadcasted_iota(jnp.int32, (TOK, LANES), 1)
    n_var = qa_ref.shape[0]
    for g, h in units:
        bias_g = bias_ref[0, g, h]
        if nsel % LANES:
            bias_g = jnp.concatenate([bias_g, jnp.zeros((LANES - nsel % LANES, TOK), F32)], axis=0)
        own = (lane // NSA_HEAD_DIM) == g
        for v in range(n_var):
            src = bias_g[(v // 2) * LANES:(v // 2 + 1) * LANES, :].T
            if v % 2 != 1 - g:
                src = pltpu.roll(src, NSA_HEAD_DIM, 1)
            src = src.astype(BF16)
            for r in range(NSA_REP):
                q_r = qp_ref[0, g, h * TOK:(h + 1) * TOK, r * LANES:(r + 1) * LANES]
                qa_ref[v, g, h, r * TOK:(r + 1) * TOK, :] = jnp.where(own, q_r, src)

    def tile_at(i):
        return jnp.where(i < count, list_ref[base + jnp.minimum(i, n_kt - 1)], n_diag)

    def scores(j):
        v = (j * blk_per_tile) // NSA_HEAD_DIM
        for g in range(NSA_KV_GROUPS):
            kb = k_ref[0, g, pl.ds(pl.multiple_of(j * SEL_TILE, SEL_TILE), SEL_TILE), :]
            for h in range(SEL_Q):
                s_ref[g, h] = _dot_nt(kb, qa_ref[v, g, h])

    def values(j):
        for g, h in units:
            acc_ref[g, h] = acc_ref[g, h] * a_ref[g, h] + _dot(vt_ref[0, g, j], p_ref[g, h])

    def softmax(j, causal):
        for g, h in units:
            s_t = s_ref[g, h]
            if causal:
                t_io = ((qi * SEL_Q + h) * TOK
                        + lax.broadcasted_iota(jnp.int32, (SEL_TILE, ncol), 1) % TOK)
                key_pos = j * SEL_TILE + lax.broadcasted_iota(jnp.int32, (SEL_TILE, ncol), 0)
                s_t = jnp.where(key_pos <= t_io, s_t, NEG_INF)
            m_old = m_ref[g, h]
            m_new = jnp.maximum(m_old, jnp.max(s_t, axis=0, keepdims=True))
            alpha = jnp.exp2(m_old - m_new)
            p = jnp.exp2(s_t - jnp.maximum(m_new, MASK_FLOOR))
            m_ref[g, h] = m_new
            p_ref[g, h] = p.astype(BF16)
            a_ref[g, h] = alpha

    scores(tile_at(0))

    def body(i, c):
        values(tile_at(jnp.maximum(i - 1, 0)))
        softmax(tile_at(i), False)
        scores(tile_at(i + 1))
        return c

    lax.fori_loop(0, count, body, 0)
    values(tile_at(jnp.maximum(count - 1, 0)))
    softmax(n_diag, True)
    values(n_diag)
    for h in range(SEL_Q):
        outs = []
        for g in range(NSA_KV_GROUPS):
            acc = acc_ref[g, h]
            gate = _gate_row(gt_ref[0, h], g, 1)
            outs.append(_heads_to_token_major(acc[:NSA_HEAD_DIM] * (gate / acc[NSA_HEAD_DIM:NSA_HEAD_DIM + 1])))
        o_ref[0, h * TOK:(h + 1) * TOK, :] = jnp.concatenate(outs, axis=1)


def _selected(qp, k, vt, bias, gt, tile_list, tile_count, bsz, s):
    nt = s // TOK
    nsel = s // NSA_SEL_BLOCK
    ncol = NSA_REP * TOK
    unit = (NSA_KV_GROUPS, SEL_Q)
    grid_spec = pltpu.PrefetchScalarGridSpec(
        num_scalar_prefetch=2,
        grid=(bsz, nt // SEL_Q),
        in_specs=[
            pl.BlockSpec((1, NSA_KV_GROUPS, SEL_Q * TOK, NSA_REP * LANES), lambda b, i, tl, tc: (b, 0, i, 0)),
            pl.BlockSpec((1, NSA_KV_GROUPS, s, LANES), lambda b, i, tl, tc: (b, 0, 0, 0)),
            pl.BlockSpec((1, NSA_KV_GROUPS, s // SEL_TILE, SEL_V_ROWS, SEL_TILE),
                         lambda b, i, tl, tc: (b, 0, 0, 0, 0)),
            pl.BlockSpec((1, NSA_KV_GROUPS, SEL_Q, nsel, TOK), lambda b, i, tl, tc: (b, 0, i, 0, 0)),
            pl.BlockSpec((1, SEL_Q, GATE_ROWS, TOK), lambda b, i, tl, tc: (b, i, 0, 0)),
        ],
        out_specs=pl.BlockSpec((1, SEL_Q * TOK, NSA_Q_WIDTH), lambda b, i, tl, tc: (b, i, 0)),
        scratch_shapes=[pltpu.VMEM(unit + (1, ncol), F32),
                        pltpu.VMEM(unit + (SEL_V_ROWS, ncol), F32),
                        pltpu.VMEM(unit + (SEL_TILE, ncol), F32),
                        pltpu.VMEM(unit + (SEL_TILE, ncol), BF16),
                        pltpu.VMEM(unit + (1, ncol), F32),
                        pltpu.VMEM((max(nsel // NSA_HEAD_DIM, 1),) + unit + (ncol, LANES), BF16)],
    )
    return pl.pallas_call(
        functools.partial(_selected_kernel, nt=nt, nsel=nsel),
        grid_spec=grid_spec,
        out_shape=jax.ShapeDtypeStruct((bsz, s, NSA_Q_WIDTH), F32),
        compiler_params=_cparams(("parallel", "parallel")),
        name="nsa_selected",
    )(tile_list, tile_count, qp, k, vt, bias, gt)


def _active_tiles(bias, bsz, s):
    n_q = s // (SEL_Q * TOK)
    n_kt = s // SEL_TILE
    rows = SEL_TILE // NSA_SEL_BLOCK * TOK
    act = bias.reshape(bsz, NSA_KV_GROUPS, n_q, SEL_Q, n_kt, rows).max(axis=(1, 3, 5)) > -1.0
    n_diag = (jnp.arange(n_q) * SEL_Q * TOK) // SEL_TILE
    act = act & (jnp.arange(n_kt)[None, None, :] < n_diag[None, :, None])
    order = jnp.argsort(jnp.where(act, 0, 1), axis=-1, stable=True).astype(jnp.int32)
    return order.reshape(-1), jnp.sum(act, axis=-1, dtype=jnp.int32).reshape(-1)


def _ssd_kernel(z_ref, xbc_ref, dt_ref, cw_ref, cb_ref, dtb_ref, alog_ref, dskip_ref, nw_ref, e8_ref,
                tri_ref, y_ref, ext_ref, st_ref):
    @pl.when(pl.program_id(1) == 0)
    def _():
        ext_ref[pl.ds(0, 8), :] = jnp.zeros((8, SSM_CONV_DIM), F32)
        st_ref[...] = jnp.zeros_like(st_ref)

    xb = xbc_ref[...]
    ext_ref[pl.ds(8, TOK), :] = xb
    conv = cb_ref[...] + xb * cw_ref[SSM_CONV - 1:SSM_CONV, :]
    for k in range(SSM_CONV - 1):
        conv = conv + ext_ref[pl.ds(8 - (SSM_CONV - 1) + k, TOK), :] * cw_ref[k:k + 1, :]
    ext_ref[pl.ds(0, 8), :] = xb[TOK - 8:TOK, :]
    u = _silu(conv)
    xs = u[:, :SSM_INNER]
    gn = SSM_GROUPS * SSM_STATE
    bm = u[:, SSM_INNER:SSM_INNER + gn]
    cm = u[:, SSM_INNER + gn:]
    dt = jax.nn.softplus(dt_ref[...] + dtb_ref[...])
    da = dt * (-jnp.exp(alog_ref[...]))
    tri = tri_ref[...]
    e8 = e8_ref[...]
    cum = _dot_sel_l(tri, da)
    cum_t = cum.T
    cum_e = _dot_sel_r(cum, e8)
    dt_e = _dot_sel_r(dt, e8)
    xdt = xs * dt_e
    lane = lax.broadcasted_iota(jnp.int32, (TOK, LANES), 1)
    t_io = lax.broadcasted_iota(jnp.int32, (TOK, TOK), 0)
    s_io = lax.broadcasted_iota(jnp.int32, (TOK, TOK), 1)
    causal = s_io <= t_io
    hg = SSM_HEADS // SSM_GROUPS
    y_parts = []
    cbs = []
    for g in range(SSM_GROUPS):
        in_g = (lane // SSM_STATE) == g
        cbs.append(_dot_nt(jnp.where(in_g, cm, 0.0).astype(BF16), bm.astype(BF16)))
    for c in range(SSM_HEADS // 2):
        acc = jnp.zeros((TOK, LANES), F32)
        xpair = xdt[:, c * LANES:(c + 1) * LANES]
        for hh in range(2):
            h = 2 * c + hh
            seg = cum[:, h:h + 1] - cum_t[h:h + 1, :]
            decay = jnp.where(causal, jnp.exp(jnp.where(causal, seg, 0.0)), 0.0)
            mat = (cbs[h // hg] * decay).astype(BF16)
            xh = jnp.where((lane // SSM_HEAD_DIM) == hh, xpair, 0.0).astype(BF16)
            acc = acc + _dot(mat, xh)
        y_parts.append(acc)
    y_diag = jnp.concatenate(y_parts, axis=1)
    prev = st_ref[...]
    y_off = _dot(cm.astype(BF16), prev.astype(BF16)) * jnp.exp(cum_e)
    cum_last = cum_e[TOK - 1:TOK, :]
    xdec = (xdt * jnp.exp(cum_last - cum_e)).astype(BF16)
    new = _dot(bm.T.astype(BF16), xdec)
    row_g = lax.broadcasted_iota(jnp.int32, (LANES, SSM_INNER), 0) // SSM_STATE
    col_g = lax.broadcasted_iota(jnp.int32, (LANES, SSM_INNER), 1) // (hg * SSM_HEAD_DIM)
    st_ref[...] = prev * jnp.exp(cum_last) + jnp.where(row_g == col_g, new, 0.0)
    y = (y_diag + y_off + xs * dskip_ref[...]) * _silu(z_ref[...])
    gw = SSM_INNER // SSM_GROUPS
    outs = []
    for g in range(SSM_GROUPS):
        yg = y[:, g * gw:(g + 1) * gw]
        outs.append(_rms(yg, nw_ref[:, g * gw:(g + 1) * gw]))
    y_ref[...] = jnp.concatenate(outs, axis=1).astype(BF16)


SSD_INPUTS = 11


def _hgrn2_kernel(q_ref, f_ref, i_ref, g_ref, lb_ref, nw_ref, tri_ref, y_ref, st_ref, kp_ref, bp_ref, vp_ref):
    @pl.when(pl.program_id(1) == 0)
    def _():
        st_ref[...] = jnp.zeros_like(st_ref)
        kp_ref[...] = jnp.zeros_like(kp_ref)
        bp_ref[...] = jnp.zeros_like(bp_ref)
        vp_ref[...] = jnp.zeros_like(vp_ref)

    nsub = TOK // HG_SUB
    in_sub = lax.broadcasted_iota(jnp.int32, (TOK, HG_KEY_DIM), 0) % HG_SUB
    tri = tri_ref[...]
    outs = []
    for h in range(HG_HEADS):
        sl = slice(h * HG_KEY_DIM, (h + 1) * HG_KEY_DIM)
        lb = lb_ref[:, sl]
        fr = f_ref[:, sl]
        q = _silu(q_ref[:, sl]) * (HG_KEY_DIM ** -0.5)
        log_f = jnp.log(lb + (1.0 - lb) * jax.nn.sigmoid(fr))
        k = (1.0 - lb) * jax.nn.sigmoid(-fr)
        v = i_ref[:, sl]
        bcum = _dot_sel_l(tri, log_f)
        st = st_ref[h]
        o = _dot_nt((q * jnp.exp(bcum)).astype(BF16), st.astype(BF16))
        ends = [bcum[(j + 1) * HG_SUB - 1:(j + 1) * HG_SUB, :] for j in range(nsub)]
        blk = (HG_SUB, HG_KEY_DIM)
        zero_blk = jnp.zeros(blk, BF16)
        e_prev = jnp.concatenate([jnp.zeros(blk, F32)] + [jnp.broadcast_to(ends[j], blk) for j in range(nsub - 1)],
                                 axis=0)
        e_own = jnp.concatenate([jnp.broadcast_to(ends[j], blk) for j in range(nsub)], axis=0)
        q_loc = (q * jnp.exp(bcum - e_prev)).astype(BF16)
        k_loc = k * jnp.exp(e_own - bcum)
        q_cols = []
        k_cols = []
        for i in range(1, nsub):
            q_cols.append(jnp.concatenate(
                [zero_blk] * i + [q_loc[i * HG_SUB:(i + 1) * HG_SUB]] + [zero_blk] * (nsub - 1 - i), axis=0))
            parts = []
            for j in range(i):
                kj = k_loc[j * HG_SUB:(j + 1) * HG_SUB]
                if j < i - 1:
                    kj = kj * jnp.exp(ends[i - 1] - ends[j])
                parts.append(kj.astype(BF16))
            k_cols.append(jnp.concatenate(parts + [zero_blk] * (nsub - i), axis=0))
        q_aug = jnp.concatenate(q_cols, axis=1)
        k_aug = jnp.concatenate(k_cols, axis=1)
        att = _dot_nt(q_aug, k_aug)
        o = o + _dot(att.astype(BF16), v.astype(BF16))
        kp_ref[pl.ds(HG_SUB, TOK), :] = k
        bp_ref[pl.ds(HG_SUB, TOK), :] = bcum
        vp_ref[pl.ds(HG_SUB, TOK), :] = v
        o = o + jnp.sum(q * k, axis=1, keepdims=True) * v
        for d in range(1, HG_SUB):
            ks = kp_ref[pl.ds(HG_SUB - d, TOK), :]
            bs = bp_ref[pl.ds(HG_SUB - d, TOK), :]
            vs = vp_ref[pl.ds(HG_SUB - d, TOK), :]
            ok = in_sub >= d
            w = jnp.sum(q * ks * jnp.exp(bcum - bs), axis=1, keepdims=True)
            o = o + jnp.where(ok, w * vs, 0.0)
        b_last = bcum[TOK - 1:TOK, :]
        kdec = (k * jnp.exp(b_last - bcum)).astype(BF16)
        st_ref[h] = st * jnp.exp(b_last) + _dot(v.T.astype(BF16), kdec)
        outs.append(_rms(o, nw_ref[...]) * _silu(g_ref[:, sl]))
    y_ref[...] = jnp.concatenate(outs, axis=1).astype(BF16)


HG_INPUTS = 7


def _recurrent_kernel(*refs):
    n_in = SSD_INPUTS + HG_INPUTS
    ssd_out, hg_out = refs[n_in], refs[n_in + 1]
    scratch = refs[n_in + 2:]
    _ssd_kernel(*refs[:SSD_INPUTS], ssd_out, *scratch[:2])
    _hgrn2_kernel(*refs[SSD_INPUTS:n_in], hg_out, *scratch[2:])


def _recurrent_mixers(proj, cw, cb, dtb, alog, dskip, ssd_nw, e8, lb, hg_nw, tri, bsz, s):
    nt = s // TOK

    def col(c0, w):
        return pl.BlockSpec((TOK, w), lambda b, i: (b * nt + i, c0 // w))

    def full(shape):
        return pl.BlockSpec(shape, lambda b, i: (0,) * len(shape))

    pad = pltpu.VMEM((TOK + HG_SUB, HG_KEY_DIM), F32)
    ssd_specs = [col(COL_Z, SSM_INNER), col(COL_XBC, SSM_CONV_DIM), col(COL_DT, LANES),
                 full((SSM_CONV, SSM_CONV_DIM)), full((1, SSM_CONV_DIM)), full((1, LANES)), full((1, LANES)),
                 full((1, SSM_INNER)), full((1, SSM_INNER)), full((LANES, SSM_INNER)), full((TOK, TOK))]
    hg_specs = [col(COL_HQ, HG_WIDTH), col(COL_HF, HG_WIDTH), col(COL_HI, HG_WIDTH), col(COL_HG, HG_WIDTH),
                full((1, HG_WIDTH)), full((1, HG_VAL_DIM)), full((TOK, TOK))]
    assert len(ssd_specs) == SSD_INPUTS and len(hg_specs) == HG_INPUTS
    return pl.pallas_call(
        _recurrent_kernel,
        grid=(bsz, nt),
        in_specs=ssd_specs + hg_specs,
        out_specs=[pl.BlockSpec((TOK, SSM_INNER), lambda b, i: (b * nt + i, 0)),
                   pl.BlockSpec((TOK, HG_WIDTH), lambda b, i: (b * nt + i, 0))],
        out_shape=[jax.ShapeDtypeStruct((bsz * s, SSM_INNER), BF16),
                   jax.ShapeDtypeStruct((bsz * s, HG_WIDTH), BF16)],
        scratch_shapes=[pltpu.VMEM((TOK + 8, SSM_CONV_DIM), F32), pltpu.VMEM((LANES, SSM_INNER), F32),
                        pltpu.VMEM((HG_HEADS, HG_VAL_DIM, HG_KEY_DIM), F32), pad, pad, pad],
        compiler_params=_cparams(("parallel", "arbitrary")),
        name="ssd_hgrn2",
    )(proj, proj, proj, cw, cb, dtb, alog, dskip, ssd_nw, e8, tri, proj, proj, proj, proj, lb, hg_nw, tri)


def _merge_kernel(ga_ref, gb_ref, gc_ref, oc_ref, os_ref, ow_ref, yb_ref, yc_ref, x_ref,
                  wa_ref, wb_ref, wc_ref, wo_ref, o_ref):
    nsa = oc_ref[...] + os_ref[...] + ow_ref[...]
    ya = _dot(nsa.astype(BF16), wa_ref[...])
    yb = _dot(yb_ref[...], wb_ref[...])
    yc = _dot(yc_ref[...], wc_ref[...])
    merged = (jax.nn.sigmoid(ga_ref[...].astype(F32)) * ya + jax.nn.sigmoid(gb_ref[...].astype(F32)) * yb
              + jax.nn.sigmoid(gc_ref[...].astype(F32)) * yc)
    o_ref[...] = x_ref[...] + _dot(merged.astype(BF16), wo_ref[...])


def _merge(gates, oc, os_, ow, yb, yc, x2, wa, wb, wc, wo):
    t = x2.shape[0]
    tm = min(512, t)

    def col(c0, w):
        return pl.BlockSpec((tm, w), lambda i: (i, c0 // w))

    def full(shape):
        return pl.BlockSpec(shape, lambda i: (0,) * len(shape))

    row512 = pl.BlockSpec((tm, 512), lambda i: (i, 0))
    return pl.pallas_call(
        _merge_kernel,
        grid=(t // tm,),
        in_specs=[col(COL_GA, D_MODEL), col(COL_GB, D_MODEL), col(COL_GC, D_MODEL),
                  row512, row512, row512, row512, row512,
                  pl.BlockSpec((tm, D_MODEL), lambda i: (i, 0)),
                  full((NSA_Q_WIDTH, D_MODEL)), full((SSM_INNER, D_MODEL)),
                  full((HG_WIDTH, D_MODEL)), full((D_MODEL, D_MODEL))],
        out_specs=pl.BlockSpec((tm, D_MODEL), lambda i: (i, 0)),
        out_shape=jax.ShapeDtypeStruct((t, D_MODEL), F32),
        compiler_params=_cparams(("parallel",)),
        name="merge",
    )(gates, gates, gates, oc, os_, ow, yb, yc, x2, wa, wb, wc, wo)


def _xa_mem_kernel(mem_ref, nm_ref, wk_ref, wv_ref, kn_ref, k_ref, v_ref):
    m = _rms(mem_ref[0], nm_ref[...]).astype(BF16)
    k = _dot(m, wk_ref[...])
    ks = [_rms(k[:, h * XA_HEAD_DIM:(h + 1) * XA_HEAD_DIM], kn_ref[...]) for h in range(XA_HEADS)]
    k_ref[0] = jnp.concatenate(ks, axis=1).astype(BF16)
    v_ref[0] = _dot(m, wv_ref[...]).astype(BF16)


def _xa_mem(mem, nm, wk, wv, kn):
    bsz, ml, _ = mem.shape

    def full(shape):
        return pl.BlockSpec(shape, lambda b: (0,) * len(shape))

    return pl.pallas_call(
        _xa_mem_kernel,
        grid=(bsz,),
        in_specs=[pl.BlockSpec((1, ml, D_MODEL), lambda b: (b, 0, 0)), full((1, D_MODEL)),
                  full((D_MODEL, XA_WIDTH)), full((D_MODEL, XA_WIDTH)), full((1, XA_HEAD_DIM))],
        out_specs=[pl.BlockSpec((1, ml, XA_WIDTH), lambda b: (b, 0, 0)),
                   pl.BlockSpec((1, ml, XA_WIDTH), lambda b: (b, 0, 0))],
        out_shape=[jax.ShapeDtypeStruct((bsz, ml, XA_WIDTH), BF16),
                   jax.ShapeDtypeStruct((bsz, ml, XA_WIDTH), BF16)],
        compiler_params=_cparams(("parallel",)),
        name="xa_mem",
    )(mem, nm, wk, wv, kn)


def _xa_kernel(x_ref, nx_ref, wq_ref, qn_ref, k_ref, v_ref, wo_ref, o_ref):
    x = x_ref[...]
    h = _rms(x, nx_ref[...]).astype(BF16)
    q = _dot(h, wq_ref[...])
    outs = []
    for hd in range(XA_HEADS):
        sl = slice(hd * XA_HEAD_DIM, (hd + 1) * XA_HEAD_DIM)
        qh = _rms(q[:, sl], qn_ref[...]).astype(BF16)
        sc = _dot_nt(qh, k_ref[0, :, sl]) * (XA_HEAD_DIM ** -0.5)
        sc = sc - jnp.max(sc, axis=-1, keepdims=True)
        p = jnp.exp(sc)
        p = p / jnp.sum(p, axis=-1, keepdims=True)
        outs.append(_dot(p.astype(BF16), v_ref[0, :, sl]))
    o = jnp.concatenate(outs, axis=1).astype(BF16)
    o_ref[...] = x + _dot(o, wo_ref[...])


def _xa(x2, nx, wq, qn, k, v, wo, s):
    t = x2.shape[0]
    tm = min(512, s)
    ml = k.shape[1]
    per_b = s // tm

    def full(shape):
        return pl.BlockSpec(shape, lambda i: (0,) * len(shape))

    return pl.pallas_call(
        _xa_kernel,
        grid=(t // tm,),
        in_specs=[pl.BlockSpec((tm, D_MODEL), lambda i: (i, 0)), full((1, D_MODEL)), full((D_MODEL, XA_WIDTH)),
                  full((1, XA_HEAD_DIM)),
                  pl.BlockSpec((1, ml, XA_WIDTH), lambda i: (i // per_b, 0, 0)),
                  pl.BlockSpec((1, ml, XA_WIDTH), lambda i: (i // per_b, 0, 0)),
                  full((XA_WIDTH, D_MODEL))],
        out_specs=pl.BlockSpec((tm, D_MODEL), lambda i: (i, 0)),
        out_shape=jax.ShapeDtypeStruct((t, D_MODEL), F32),
        compiler_params=_cparams(("parallel",)),
        name="cross_attn",
    )(x2, nx, wq, qn, k, v, wo)


def _ffn_kernel(x_ref, nw_ref, wg_ref, wu_ref, wd_ref, o_ref, h_ref, acc_ref):
    j = pl.program_id(1)

    @pl.when(j == 0)
    def _():
        h_ref[...] = _rms(x_ref[...], nw_ref[...]).astype(BF16)
        acc_ref[...] = jnp.zeros_like(acc_ref)

    h = h_ref[...]
    a = _silu(_dot(h, wg_ref[...])) * _dot(h, wu_ref[...])
    acc_ref[...] += _dot(a.astype(BF16), wd_ref[...])

    @pl.when(j == pl.num_programs(1) - 1)
    def _():
        o_ref[...] = x_ref[...] + acc_ref[...]


def _ffn(x2, nw, wg, wu, wd):
    t = x2.shape[0]
    tm = min(1024, t)
    th = FFN_HIDDEN // 2
    return pl.pallas_call(
        _ffn_kernel,
        grid=(t // tm, FFN_HIDDEN // th),
        in_specs=[pl.BlockSpec((tm, D_MODEL), lambda i, j: (i, 0)),
                  pl.BlockSpec((1, D_MODEL), lambda i, j: (0, 0)),
                  pl.BlockSpec((D_MODEL, th), lambda i, j: (0, j)),
                  pl.BlockSpec((D_MODEL, th), lambda i, j: (0, j)),
                  pl.BlockSpec((th, D_MODEL), lambda i, j: (j, 0))],
        out_specs=pl.BlockSpec((tm, D_MODEL), lambda i, j: (i, 0)),
        out_shape=jax.ShapeDtypeStruct((t, D_MODEL), F32),
        scratch_shapes=[pltpu.VMEM((tm, D_MODEL), BF16), pltpu.VMEM((tm, D_MODEL), F32)],
        compiler_params=_cparams(("parallel", "arbitrary")),
        name="ffn",
    )(x2, nw, wg, wu, wd)


def _pack_w_in(w):
    parts = {}
    off = 0
    names = ("q", "kc", "vc", "ks", "vs", "kw", "vw", "ng", "z", "xbc", "dt", "hq", "hf", "hi", "hg",
             "ga", "gb", "gc")
    for name, size in zip(names, SPLIT_SIZES):
        parts[name] = w[:, off:off + size]
        off += size

    def padded(a, width):
        return jnp.pad(a, ((0, 0), (0, width - a.shape[1])))

    cols = [parts["ga"], parts["gb"], parts["gc"], parts["q"], parts["z"], parts["hq"], parts["hf"],
            parts["hi"], parts["hg"], parts["xbc"], parts["kc"], parts["vc"], parts["ks"], parts["vs"],
            parts["kw"], parts["vw"], padded(parts["ng"], LANES), padded(parts["dt"], LANES)]
    packed = jnp.concatenate(cols, axis=1)
    return padded(packed, PROJ_WIDTH).astype(BF16)


def _rope_tables(pos):
    half = NSA_HEAD_DIM // 2
    inv = 1.0 / (ROPE_THETA ** (jnp.arange(half, dtype=F32) / half))
    ang = pos.astype(F32)[:, None] * inv[None, :]
    reps = LANES // half
    return jnp.tile(jnp.cos(ang), (1, reps)), jnp.tile(jnp.sin(ang), (1, reps))


def _block_diag2(w):
    z = jnp.zeros_like(w)
    return jnp.concatenate([jnp.concatenate([w, z], axis=-1), jnp.concatenate([z, w], axis=-1)], axis=-2)


def kernel(x, mem, norm_mix, w_in, nsa_q_norm, nsa_k_norm, nsa_cmp_pos_k, nsa_cmp_pos_v, nsa_cmp_w1_k, nsa_cmp_w2_k, nsa_cmp_w1_v, nsa_cmp_w2_v, w_nsa_o, ssm_conv_w, ssm_conv_b, ssm_dt_bias, ssm_a_log, ssm_d, ssm_norm, w_ssm_o, hg_lb_logits, hg_norm, w_hg_o, w_out, norm_xa, norm_mem, xa_w_q, xa_w_k, xa_w_v, xa_q_norm, xa_k_norm, xa_w_o, norm_ffn, ffn_w_gate, ffn_w_up, ffn_w_down):
    bsz, s, d = x.shape
    depth = w_in.shape[0]
    assert d == D_MODEL and s % (8 * TOK) == 0 and s // TOK <= TOK
    t = bsz * s
    hd = NSA_HEAD_DIM

    cos_t, sin_t = _rope_tables(jnp.arange(s))
    nbp = s // NSA_CMP_STRIDE
    cos_c, sin_c = _rope_tables(jnp.arange(nbp) * NSA_CMP_STRIDE + NSA_CMP_BLOCK - 1)
    tri = jnp.tril(jnp.ones((TOK, TOK), F32)).astype(BF16)
    rows = np.arange(LANES)[:, None]
    cols = np.arange(NSA_Q_WIDTH)[None, :]
    e8 = jnp.asarray(rows == cols // SSM_HEAD_DIM, F32).astype(BF16)

    lb_sm = jax.nn.softmax(hg_lb_logits.astype(F32), axis=0)
    lb_all = jnp.cumsum(lb_sm, axis=0) - lb_sm[0:1]

    def pad_lanes(v):
        return jnp.pad(v, (0, LANES - v.shape[0]))[None, :]

    x2 = x.reshape(t, d)
    for l in range(depth):
        gates, proj = _in_proj(x2, norm_mix[l][None, :], _pack_w_in(w_in[l]))

        qn2 = jnp.tile(nsa_q_norm[l], 2)[None, :]
        kn2 = jnp.tile(nsa_k_norm[l], 2)[None, :]
        qp, ksn, kwn, vst, vwt, gt = _nsa_prep(proj, cos_t, sin_t, qn2, kn2, bsz, s)
        w1k = _block_diag2(nsa_cmp_w1_k[l].reshape(NSA_CMP_BLOCK, hd, hd)).astype(BF16)
        w1v = _block_diag2(nsa_cmp_w1_v[l].reshape(NSA_CMP_BLOCK, hd, hd)).astype(BF16)
        kc, vct = _compress(proj, jnp.tile(nsa_cmp_pos_k[l], (1, 2)), jnp.tile(nsa_cmp_pos_v[l], (1, 2)),
                            w1k, _block_diag2(nsa_cmp_w2_k[l]).astype(BF16),
                            w1v, _block_diag2(nsa_cmp_w2_v[l]).astype(BF16), kn2, cos_c, sin_c, bsz, s)
        o_c, sel_bias = _cmp_topk(qp, kc, vct, gt, bsz, s)
        tile_list, tile_count = _active_tiles(sel_bias, bsz, s)
        o_s = _selected(qp, ksn, vst, sel_bias, gt, tile_list, tile_count, bsz, s)
        o_w = _window(qp, kwn, vwt, gt, bsz, s)

        y_b, y_c = _recurrent_mixers(
            proj, ssm_conv_w[l], ssm_conv_b[l][None, :], pad_lanes(ssm_dt_bias[l]), pad_lanes(ssm_a_log[l]),
            jnp.repeat(ssm_d[l], SSM_HEAD_DIM)[None, :], ssm_norm[l][None, :], e8,
            lb_all[l][None, :], hg_norm[l][None, :], tri, bsz, s)

        x2 = _merge(gates, o_c.reshape(t, -1), o_s.reshape(t, -1), o_w.reshape(t, -1), y_b, y_c, x2,
                    w_nsa_o[l].astype(BF16), w_ssm_o[l].astype(BF16), w_hg_o[l].astype(BF16),
                    w_out[l].astype(BF16))

        k_m, v_m = _xa_mem(mem, norm_mem[l][None, :], xa_w_k[l].astype(BF16), xa_w_v[l].astype(BF16),
                           xa_k_norm[l][None, :])
        x2 = _xa(x2, norm_xa[l][None, :], xa_w_q[l].astype(BF16), xa_q_norm[l][None, :], k_m, v_m,
                 xa_w_o[l].astype(BF16), s)
        x2 = _ffn(x2, norm_ffn[l][None, :], ffn_w_gate[l].astype(BF16), ffn_w_up[l].astype(BF16),
                  ffn_w_down[l].astype(BF16))
    return x2.reshape(bsz, s, d)
```

```python
import functools
import math

import numpy as np
import jax
import jax.numpy as jnp
from jax import lax
from jax.experimental import pallas as pl
from jax.experimental.pallas import tpu as pltpu

F32 = jnp.float32
BF16 = jnp.bfloat16

D_MODEL = 1024
NORM_EPS = 1e-6
ROPE_THETA = 10000.0
NEG_INF = -1e30

NSA_HEADS = 8
NSA_KV_GROUPS = 2
NSA_REP = NSA_HEADS // NSA_KV_GROUPS
NSA_HEAD_DIM = 64
NSA_CMP_BLOCK = 32
NSA_CMP_STRIDE = 16
NSA_SEL_BLOCK = 64
NSA_TOPK = 16
N_FORCED = 3
NSA_WINDOW = 512
NSA_Q_WIDTH = NSA_HEADS * NSA_HEAD_DIM
NSA_KV_WIDTH = NSA_KV_GROUPS * NSA_HEAD_DIM

SSM_INNER = 512
SSM_HEAD_DIM = 64
SSM_HEADS = SSM_INNER // SSM_HEAD_DIM
SSM_GROUPS = 2
SSM_STATE = 64
SSM_CONV = 4
SSM_CONV_DIM = SSM_INNER + 2 * SSM_GROUPS * SSM_STATE

HG_HEADS = 4
HG_KEY_DIM = 128
HG_VAL_DIM = 128
HG_WIDTH = HG_HEADS * HG_KEY_DIM
HG_SUB = 16

XA_HEADS = 4
XA_HEAD_DIM = 128
XA_WIDTH = XA_HEADS * XA_HEAD_DIM

FFN_HIDDEN = -(-(8 * D_MODEL) // (3 * 256)) * 256

SPLIT_SIZES = (
    NSA_Q_WIDTH, NSA_KV_WIDTH, NSA_KV_WIDTH, NSA_KV_WIDTH, NSA_KV_WIDTH, NSA_KV_WIDTH, NSA_KV_WIDTH,
    3 * NSA_HEADS, SSM_INNER, SSM_CONV_DIM, SSM_HEADS, HG_WIDTH, HG_WIDTH, HG_HEADS * HG_VAL_DIM,
    HG_HEADS * HG_VAL_DIM, D_MODEL, D_MODEL, D_MODEL,
)

LANES = 128
SUBLANES = 8
TOK = 128
NSA_BRANCHES = 3
GATE_ROWS = -(-NSA_BRANCHES * NSA_HEADS // SUBLANES) * SUBLANES
PREP_TILES = 4
SEL_Q = 4
SEL_TILE = 512
SEL_V_ROWS = NSA_HEAD_DIM + 16
WIN_TILES = NSA_WINDOW // TOK + 1
LOG2_E = math.log2(math.e)
MASK_FLOOR = -1e20
VMEM_LIMIT = 56 * 1024 * 1024

PROJ_WIDTH = 8192
GATE_WIDTH = 3 * D_MODEL
COL_GA, COL_GB, COL_GC = 0, 1024, 2048
COL_Q = 0
COL_Z = 512
COL_HQ, COL_HF, COL_HI, COL_HG = 1024, 1536, 2048, 2560
COL_XBC = 3072
COL_KC, COL_VC, COL_KS, COL_VS, COL_KW, COL_VW = 3840, 3968, 4096, 4224, 4352, 4480
COL_NG = 4608
COL_DT = 4736


def _cparams(sem):
    return pltpu.CompilerParams(dimension_semantics=sem, vmem_limit_bytes=VMEM_LIMIT)


def _dot(a, b):
    return jnp.dot(a, b, preferred_element_type=F32)


def _dot_nt(a, b):
    return lax.dot_general(a, b, (((1,), (1,)), ((), ())), preferred_element_type=F32)


def _split3(a):
    hi = a.astype(BF16)
    r1 = a - hi.astype(F32)
    mid = r1.astype(BF16)
    lo = (r1 - mid.astype(F32)).astype(BF16)
    return hi, mid, lo


def _dot_sel_r(a, sel):
    hi, mid, lo = _split3(a)
    return _dot(hi, sel) + _dot(mid, sel) + _dot(lo, sel)


def _dot_sel_l(sel, a):
    hi, mid, lo = _split3(a)
    return _dot(sel, hi) + _dot(sel, mid) + _dot(sel, lo)


def _silu(x):
    return x * jax.nn.sigmoid(x)


def _rms(x, w):
    return x * lax.rsqrt(jnp.mean(x * x, axis=-1, keepdims=True) + NORM_EPS) * w


def _inproj_kernel(x_ref, nw_ref, w_ref, og_ref, of_ref, h_ref, *, n_gate):
    j = pl.program_id(1)

    @pl.when(j == 0)
    def _():
        h_ref[...] = _rms(x_ref[...], nw_ref[...]).astype(BF16)

    @pl.when(j < n_gate)
    def _():
        og_ref[...] = _dot(h_ref[...], w_ref[...]).astype(BF16)

    @pl.when(j >= n_gate)
    def _():
        of_ref[...] = _dot(h_ref[...], w_ref[...])


def _in_proj(x2, norm_w, w_packed):
    t = x2.shape[0]
    tm = min(2048, t)
    tn = 1024
    n_gate = GATE_WIDTH // tn
    return pl.pallas_call(
        functools.partial(_inproj_kernel, n_gate=n_gate),
        grid=(t // tm, PROJ_WIDTH // tn),
        in_specs=[
            pl.BlockSpec((tm, D_MODEL), lambda i, j: (i, 0)),
            pl.BlockSpec((1, D_MODEL), lambda i, j: (0, 0)),
            pl.BlockSpec((D_MODEL, tn), lambda i, j: (0, j)),
        ],
        out_specs=[pl.BlockSpec((tm, tn), lambda i, j: (i, jnp.minimum(j, n_gate - 1))),
                   pl.BlockSpec((tm, tn), lambda i, j: (i, jnp.maximum(j - n_gate, 0)))],
        out_shape=[jax.ShapeDtypeStruct((t, GATE_WIDTH), BF16),
                   jax.ShapeDtypeStruct((t, PROJ_WIDTH - GATE_WIDTH), F32)],
        scratch_shapes=[pltpu.VMEM((tm, D_MODEL), BF16)],
        compiler_params=_cparams(("parallel", "arbitrary")),
        name="in_proj",
    )(x2, norm_w, w_packed)


def _norm_rope_pair(x, w, cos, sin):
    lane = lax.broadcasted_iota(jnp.int32, x.shape, 1)
    lo = lane < NSA_HEAD_DIM
    sq = x * x
    s_lo = jnp.sum(jnp.where(lo, sq, 0.0), axis=1, keepdims=True)
    s_hi = jnp.sum(jnp.where(lo, 0.0, sq), axis=1, keepdims=True)
    ms = jnp.where(lo, s_lo, s_hi) * (1.0 / NSA_HEAD_DIM)
    y = x * lax.rsqrt(ms + NORM_EPS) * w
    half = NSA_HEAD_DIM // 2
    first = (lane % NSA_HEAD_DIM) < half
    rot = jnp.where(first, -pltpu.roll(y, LANES - half, 1), pltpu.roll(y, half, 1))
    return y * cos + rot * sin


def _nsa_prep_kernel(q_ref, ks_ref, vs_ref, kw_ref, vw_ref, ng_ref, cos_ref, sin_ref, qn_ref, kn_ref,
                     qp_ref, ksn_ref, kwn_ref, vst_ref, vwt_ref, gt_ref):
    lane = lax.broadcasted_iota(jnp.int32, (TOK, LANES), 1)
    row = lax.broadcasted_iota(jnp.int32, (TOK, LANES), 0)
    scale = NSA_HEAD_DIM ** -0.5 * LOG2_E
    ones_rows = (lax.broadcasted_iota(jnp.int32, (SEL_V_ROWS - NSA_HEAD_DIM, TOK), 0) == 0).astype(BF16)
    for u in range(PREP_TILES):
        rows = slice(u * TOK, (u + 1) * TOK)
        cos = cos_ref[rows, :]
        sin = sin_ref[rows, :]
        for c in range(NSA_HEADS // 2):
            y = _norm_rope_pair(q_ref[rows, c * LANES:(c + 1) * LANES], qn_ref[...], cos, sin) * scale
            y_sw = pltpu.roll(y, NSA_HEAD_DIM, 1)
            for hh in range(2):
                h = 2 * c + hh
                g, r = divmod(h, NSA_REP)
                src = y if hh == g else y_sw
                keep = (lane // NSA_HEAD_DIM) == g
                qp_ref[0, g, rows, r * LANES:(r + 1) * LANES] = jnp.where(keep, src, 0.0).astype(BF16)
        ks_n = _norm_rope_pair(ks_ref[rows, :], kn_ref[...], cos, sin)
        tile = pl.program_id(1) * PREP_TILES + u
        blk = (tile * (TOK // NSA_SEL_BLOCK) + row // NSA_SEL_BLOCK) % NSA_HEAD_DIM
        onehot = ((lane % NSA_HEAD_DIM) == blk).astype(F32)
        for g in range(NSA_KV_GROUPS):
            ksn_ref[0, g, rows, :] = jnp.where((lane // NSA_HEAD_DIM) == g, ks_n, onehot).astype(BF16)
        kwn_ref[0, rows, :] = _norm_rope_pair(kw_ref[rows, :], kn_ref[...], cos, sin).astype(BF16)
        vs_t = vs_ref[rows, :].T.astype(BF16)
        vw_t = vw_ref[rows, :].T.astype(BF16)
        for g in range(NSA_KV_GROUPS):
            vst_ref[0, g, 0, 0:NSA_HEAD_DIM, rows] = vs_t[g * NSA_HEAD_DIM:(g + 1) * NSA_HEAD_DIM, :]
            vst_ref[0, g, 0, NSA_HEAD_DIM:SEL_V_ROWS, rows] = ones_rows
            vwt_ref[0, g, u] = vw_t[g * NSA_HEAD_DIM:(g + 1) * NSA_HEAD_DIM, :]
        gt_ref[0, u] = jax.nn.sigmoid(ng_ref[rows, :]).T[0:GATE_ROWS, :]


def _nsa_prep(proj, cos_t, sin_t, qn2, kn2, bsz, s):
    tq = PREP_TILES * TOK
    n_step = s // tq

    def col(c0, w):
        return pl.BlockSpec((tq, w), lambda b, i: (b * n_step + i, c0 // w))

    tab = pl.BlockSpec((tq, LANES), lambda b, i: (i, 0))
    vec = pl.BlockSpec((1, LANES), lambda b, i: (0, 0))
    return pl.pallas_call(
        _nsa_prep_kernel,
        grid=(bsz, n_step),
        in_specs=[col(COL_Q, NSA_Q_WIDTH), col(COL_KS, LANES), col(COL_VS, LANES), col(COL_KW, LANES),
                  col(COL_VW, LANES), col(COL_NG, LANES), tab, tab, vec, vec],
        out_specs=[
            pl.BlockSpec((1, NSA_KV_GROUPS, tq, NSA_REP * LANES), lambda b, i: (b, 0, i, 0)),
            pl.BlockSpec((1, NSA_KV_GROUPS, tq, LANES), lambda b, i: (b, 0, i, 0)),
            pl.BlockSpec((1, tq, LANES), lambda b, i: (b, i, 0)),
            pl.BlockSpec((1, NSA_KV_GROUPS, 1, SEL_V_ROWS, SEL_TILE), lambda b, i: (b, 0, i, 0, 0)),
            pl.BlockSpec((1, NSA_KV_GROUPS, PREP_TILES, NSA_HEAD_DIM, TOK), lambda b, i: (b, 0, i, 0, 0)),
            pl.BlockSpec((1, PREP_TILES, GATE_ROWS, TOK), lambda b, i: (b, i, 0, 0)),
        ],
        out_shape=[
            jax.ShapeDtypeStruct((bsz, NSA_KV_GROUPS, s, NSA_REP * LANES), BF16),
            jax.ShapeDtypeStruct((bsz, NSA_KV_GROUPS, s, LANES), BF16),
            jax.ShapeDtypeStruct((bsz, s, LANES), BF16),
            jax.ShapeDtypeStruct((bsz, NSA_KV_GROUPS, s // SEL_TILE, SEL_V_ROWS, SEL_TILE), BF16),
            jax.ShapeDtypeStruct((bsz, NSA_KV_GROUPS, s // TOK, NSA_HEAD_DIM, TOK), BF16),
            jax.ShapeDtypeStruct((bsz, s // TOK, GATE_ROWS, TOK), F32),
        ],
        compiler_params=_cparams(("parallel", "parallel")),
        name="nsa_prep",
    )(proj, proj, proj, proj, proj, proj, cos_t, sin_t, qn2, kn2)


def _compress_kernel(kc_ref, vc_ref, posk_ref, posv_ref, w1k_ref, w2k_ref, w1v_ref, w2v_ref, kn_ref,
                     cos_ref, sin_ref, kco_ref, vct_ref, sh_ref, *, nbp):
    half_blk = NSA_CMP_BLOCK // 2

    def compress(x_ref, pos_ref, w1_ref, w2_ref):
        acc_a = jnp.zeros((nbp, LANES), F32)
        acc_b = jnp.zeros((nbp, LANES), F32)
        for l in range(half_blk):
            xl = x_ref[pl.ds(l, nbp, stride=NSA_CMP_STRIDE), :]
            acc_a = acc_a + _dot((xl + pos_ref[l:l + 1, :]).astype(BF16), w1_ref[l])
            acc_b = acc_b + _dot((xl + pos_ref[half_blk + l:half_blk + l + 1, :]).astype(BF16),
                                 w1_ref[half_blk + l])
        sh_ref[pl.ds(0, nbp), :] = acc_b
        sh_ref[pl.ds(nbp, SUBLANES), :] = jnp.zeros((SUBLANES, LANES), F32)
        hid = acc_a + sh_ref[pl.ds(1, nbp), :]
        return _dot(_silu(hid).astype(BF16), w2_ref[...])

    kc = compress(kc_ref, posk_ref, w1k_ref, w2k_ref)
    kco_ref[0] = _norm_rope_pair(kc, kn_ref[...], cos_ref[...], sin_ref[...]).astype(BF16)
    vc = compress(vc_ref, posv_ref, w1v_ref, w2v_ref)
    vct_ref[0] = vc.T.astype(BF16)


def _compress(proj, posk, posv, w1k, w2k, w1v, w2v, kn2, cos_c, sin_c, bsz, s):
    nbp = s // NSA_CMP_STRIDE

    def full(shape):
        return pl.BlockSpec(shape, lambda b: (0,) * len(shape))

    return pl.pallas_call(
        functools.partial(_compress_kernel, nbp=nbp),
        grid=(bsz,),
        in_specs=[
            pl.BlockSpec((s, LANES), lambda b: (b, COL_KC // LANES)),
            pl.BlockSpec((s, LANES), lambda b: (b, COL_VC // LANES)),
            full((NSA_CMP_BLOCK, LANES)), full((NSA_CMP_BLOCK, LANES)),
            full((NSA_CMP_BLOCK, LANES, LANES)), full((LANES, LANES)),
            full((NSA_CMP_BLOCK, LANES, LANES)), full((LANES, LANES)),
            full((1, LANES)), full((nbp, LANES)), full((nbp, LANES)),
        ],
        out_specs=[
            pl.BlockSpec((1, nbp, LANES), lambda b: (b, 0, 0)),
            pl.BlockSpec((1, LANES, nbp), lambda b: (b, 0, 0)),
        ],
        out_shape=[
            jax.ShapeDtypeStruct((bsz, nbp, LANES), BF16),
            jax.ShapeDtypeStruct((bsz, LANES, nbp), BF16),
        ],
        scratch_shapes=[pltpu.VMEM((nbp + SUBLANES, LANES), F32)],
        compiler_params=_cparams(("parallel",)),
        name="nsa_compress",
    )(proj, proj, posk, posv, w1k, w2k, w1v, w2v, kn2, cos_c, sin_c)


def _q_rows(qp):
    return jnp.concatenate([qp[:, r * LANES:(r + 1) * LANES] for r in range(NSA_REP)], axis=0)


def _gate_row(gt, g, branch):
    rows = [(g * NSA_REP + r) * NSA_BRANCHES + branch for r in range(NSA_REP)]
    return jnp.concatenate([gt[i:i + 1, :] for i in rows], axis=1)


def _heads_to_token_major(acc_g):
    outs = []
    for p in range(NSA_REP // 2):
        blk = jnp.concatenate([acc_g[:, (2 * p) * TOK:(2 * p + 1) * TOK],
                               acc_g[:, (2 * p + 1) * TOK:(2 * p + 2) * TOK]], axis=0)
        outs.append(blk.T)
    return jnp.concatenate(outs, axis=1)


def _cmp_topk_kernel(qp_ref, kc_ref, vct_ref, gt_ref, oc_ref, mask_ref, ps_ref, *, nbp, nsel):
    qi = pl.program_id(1)
    ncol = NSA_REP * TOK
    per = NSA_SEL_BLOCK // NSA_CMP_STRIDE

    def attend(nrows):
        nblk = nrows // per
        j_io = lax.broadcasted_iota(jnp.int32, (nblk, TOK), 0)
        t_sel = qi * TOK + lax.broadcasted_iota(jnp.int32, (nblk, TOK), 1)
        cur = t_sel // NSA_SEL_BLOCK
        forced = (j_io == 0) | (j_io == cur) | (j_io == cur - 1)
        valid = j_io * NSA_SEL_BLOCK <= t_sel
        rel = (lax.broadcasted_iota(jnp.int32, (nrows, ncol), 0) * NSA_CMP_STRIDE + (NSA_CMP_BLOCK - 1)
               - lax.broadcasted_iota(jnp.int32, (nrows, ncol), 1) % TOK)
        allowed = rel <= qi * TOK
        kc = kc_ref[0, 0:nrows, :]
        vct = vct_ref[0, :, 0:nrows]
        for g in range(NSA_KV_GROUPS):
            ps_ref[g, pl.ds(0, SUBLANES), :] = jnp.zeros((SUBLANES, TOK), F32)
        scores = [_dot_nt(kc, _q_rows(qp_ref[0, g])) for g in range(NSA_KV_GROUPS)]
        outs = []
        imps = []
        for g in range(NSA_KV_GROUPS):
            s_t = jnp.where(allowed, scores[g], NEG_INF)
            m = jnp.max(s_t, axis=0, keepdims=True)
            p = jnp.exp2(s_t - jnp.maximum(m, MASK_FLOOR))
            l = jnp.sum(p, axis=0, keepdims=True)
            inv = jnp.where(l > 0.0, 1.0 / l, 0.0)
            o_t = _dot(vct, p.astype(BF16)) * inv
            gate = _gate_row(gt_ref[0, 0], g, 0)
            outs.append(_heads_to_token_major(o_t[g * NSA_HEAD_DIM:(g + 1) * NSA_HEAD_DIM, :] * gate))
            pn = p * inv
            psum = pn[:, 0:TOK]
            for r in range(1, NSA_REP):
                psum = psum + pn[:, r * TOK:(r + 1) * TOK]
            ps_ref[g, pl.ds(SUBLANES, nrows), :] = psum
            imp = (ps_ref[g, pl.ds(SUBLANES - 1, nblk, stride=per), :]
                   + ps_ref[g, pl.ds(SUBLANES + per - 1, nblk, stride=per), :])
            for c in range(per - 1):
                imp = imp + 2.0 * ps_ref[g, pl.ds(SUBLANES + c, nblk, stride=per), :]
            imps.append(jnp.where(forced, -jnp.inf, jnp.where(valid, imp, NEG_INF)))
        oc_ref[0] = jnp.concatenate(outs, axis=1)

        def pick(_, carry):
            nxt = []
            for imp_c in carry:
                mx = jnp.max(imp_c, axis=0, keepdims=True)
                idx = jnp.min(jnp.where(imp_c == mx, j_io, nblk), axis=0, keepdims=True)
                nxt.append(jnp.where(j_io == idx, -jnp.inf, imp_c))
            return tuple(nxt)

        picked = lax.fori_loop(0, min(NSA_TOPK, nsel) - N_FORCED, pick, tuple(imps))
        for g in range(NSA_KV_GROUPS):
            mask_ref[0, g, 0, 0:nblk, :] = jnp.where(valid & (picked[g] == -jnp.inf), 0.0, NEG_INF)
            if nblk < nsel:
                mask_ref[0, g, 0, nblk:nsel, :] = jnp.full((nsel - nblk, TOK), NEG_INF, F32)

    n_vis = (qi * TOK + TOK - NSA_CMP_BLOCK) // NSA_CMP_STRIDE + 1
    quarter = nbp // 4
    for k in range(1, 5):
        @pl.when((n_vis > (k - 1) * quarter) & (n_vis <= k * quarter))
        def _(k=k):
            attend(k * quarter)


def _cmp_topk(qp, kc, vct, gt, bsz, s):
    nt = s // TOK
    nbp = s // NSA_CMP_STRIDE
    nsel = s // NSA_SEL_BLOCK
    return pl.pallas_call(
        functools.partial(_cmp_topk_kernel, nbp=nbp, nsel=nsel),
        grid=(bsz, nt),
        in_specs=[
            pl.BlockSpec((1, NSA_KV_GROUPS, TOK, NSA_REP * LANES), lambda b, i: (b, 0, i, 0)),
            pl.BlockSpec((1, nbp, LANES), lambda b, i: (b, 0, 0)),
            pl.BlockSpec((1, LANES, nbp), lambda b, i: (b, 0, 0)),
            pl.BlockSpec((1, 1, GATE_ROWS, TOK), lambda b, i: (b, i, 0, 0)),
        ],
        out_specs=[
            pl.BlockSpec((1, TOK, NSA_Q_WIDTH), lambda b, i: (b, i, 0)),
            pl.BlockSpec((1, NSA_KV_GROUPS, 1, nsel, TOK), lambda b, i: (b, 0, i, 0, 0)),
        ],
        out_shape=[
            jax.ShapeDtypeStruct((bsz, s, NSA_Q_WIDTH), F32),
            jax.ShapeDtypeStruct((bsz, NSA_KV_GROUPS, nt, nsel, TOK), F32),
        ],
        scratch_shapes=[pltpu.VMEM((NSA_KV_GROUPS, nbp + SUBLANES, TOK), F32)],
        compiler_params=_cparams(("parallel", "parallel")),
        name="nsa_cmp_topk",
    )(qp, kc, vct, gt)


def _window_kernel(qp_ref, k_ref, vt_ref, gt_ref, o_ref):
    qi = pl.program_id(1)
    ncol = NSA_REP * TOK
    wk = WIN_TILES * TOK

    def run(first_tile, interior):
        kwin = k_ref[0, pl.ds(pl.multiple_of(first_tile * TOK, TOK), wk), :]
        scores = [_dot_nt(kwin, _q_rows(qp_ref[0, g])) for g in range(NSA_KV_GROUPS)]
        probs = []
        scales = []
        for g in range(NSA_KV_GROUPS):
            s_t = scores[g]
            if interior:
                row = lax.broadcasted_iota(jnp.int32, (TOK, ncol), 0)
                col = lax.broadcasted_iota(jnp.int32, (TOK, ncol), 1) % TOK
                oldest = jnp.where(row > col, s_t[:TOK], NEG_INF)
                newest = jnp.where(row <= col, s_t[wk - TOK:], NEG_INF)
                s_t = jnp.concatenate([oldest, s_t[TOK:wk - TOK], newest], axis=0)
            else:
                key = first_tile * TOK + lax.broadcasted_iota(jnp.int32, (wk, ncol), 0)
                t_io = qi * TOK + lax.broadcasted_iota(jnp.int32, (wk, ncol), 1) % TOK
                s_t = jnp.where((key <= t_io) & (t_io - key < NSA_WINDOW), s_t, NEG_INF)
            m = jnp.max(s_t, axis=0, keepdims=True)
            p = jnp.exp2(s_t - m)
            l = jnp.sum(p, axis=0, keepdims=True)
            probs.append(p.astype(BF16))
            scales.append(_gate_row(gt_ref[0, 0], g, 2) / l)
        outs = []
        for g in range(NSA_KV_GROUPS):
            vt = jnp.concatenate([vt_ref[0, g, first_tile + k] for k in range(WIN_TILES)], axis=1)
            outs.append(_heads_to_token_major(_dot(vt, probs[g]) * scales[g]))
        o_ref[0] = jnp.concatenate(outs, axis=1)

    @pl.when(qi >= WIN_TILES - 1)
    def _():
        run(qi - (WIN_TILES - 1), True)

    @pl.when(qi < WIN_TILES - 1)
    def _():
        run(0, False)


def _window(qp, k, vt, gt, bsz, s):
    nt = s // TOK
    return pl.pallas_call(
        _window_kernel,
        grid=(bsz, nt),
        in_specs=[
            pl.BlockSpec((1, NSA_KV_GROUPS, TOK, NSA_REP * LANES), lambda b, i: (b, 0, i, 0)),
            pl.BlockSpec((1, s, LANES), lambda b, i: (b, 0, 0)),
            pl.BlockSpec((1, NSA_KV_GROUPS, nt, NSA_HEAD_DIM, TOK), lambda b, i: (b, 0, 0, 0, 0)),
            pl.BlockSpec((1, 1, GATE_ROWS, TOK), lambda b, i: (b, i, 0, 0)),
        ],
        out_specs=pl.BlockSpec((1, TOK, NSA_Q_WIDTH), lambda b, i: (b, i, 0)),
        out_shape=jax.ShapeDtypeStruct((bsz, s, NSA_Q_WIDTH), F32),
        compiler_params=_cparams(("parallel", "parallel")),
        name="nsa_window",
    )(qp, k, vt, gt)


def _selected_kernel(list_ref, cnt_ref, qp_ref, k_ref, vt_ref, bias_ref, gt_ref, o_ref, m_ref, acc_ref,
                     s_ref, p_ref, a_ref, qa_ref, *, nt, nsel):
    b = pl.program_id(0)
    qi = pl.program_id(1)
    ncol = NSA_REP * TOK
    blk_per_tile = SEL_TILE // NSA_SEL_BLOCK
    n_kt = nt // (SEL_TILE // TOK)
    n_q = nt // SEL_Q
    units = [(g, h) for g in range(NSA_KV_GROUPS) for h in range(SEL_Q)]
    m_ref[...] = jnp.full(m_ref.shape, NEG_INF, F32)
    acc_ref[...] = jnp.zeros(acc_ref.shape, F32)
    p_ref[...] = jnp.zeros(p_ref.shape, BF16)
    a_ref[...] = jnp.ones(a_ref.shape, F32)
    base = (b * n_q + qi) * n_kt
    count = cnt_ref[b * n_q + qi]
    n_diag = (qi * SEL_Q * TOK) // SEL_TILE

    lane = lax.broadcasted_iota(jnp.int32, (TOK, LANES), 1)
    n_var = qa_ref.shape[0]
    for g, h in units:
        bias_g = bias_ref[0, g, h]
        if nsel % LANES:
            bias_g = jnp.concatenate([bias_g, jnp.zeros((LANES - nsel % LANES, TOK), F32)], axis=0)
        own = (lane // NSA_HEAD_DIM) == g
        for v in range(n_var):
            src = bias_g[(v // 2) * LANES:(v // 2 + 1) * LANES, :].T
            if v % 2 != 1 - g:
                src = pltpu.roll(src, NSA_HEAD_DIM, 1)
            src = src.astype(BF16)
            for r in range(NSA_REP):
                q_r = qp_ref[0, g, h * TOK:(h + 1) * TOK, r * LANES:(r + 1) * LANES]
                qa_ref[v, g, h, r * TOK:(r + 1) * TOK, :] = jnp.where(own, q_r, src)

    def tile_at(i):
        return jnp.where(i < count, list_ref[base + jnp.minimum(i, n_kt - 1)], n_diag)

    def scores(j):
        v = (j * blk_per_tile) // NSA_HEAD_DIM
        for g in range(NSA_KV_GROUPS):
            kb = k_ref[0, g, pl.ds(pl.multiple_of(j * SEL_TILE, SEL_TILE), SEL_TILE), :]
            for h in range(SEL_Q):
                s_ref[g, h] = _dot_nt(kb, qa_ref[v, g, h])

    def values(j):
        for g, h in units:
            acc_ref[g, h] = acc_ref[g, h] * a_ref[g, h] + _dot(vt_ref[0, g, j], p_ref[g, h])

    def softmax(j, causal):
        for g, h in units:
            s_t = s_ref[g, h]
            if causal:
                t_io = ((qi * SEL_Q + h) * TOK
                        + lax.broadcasted_iota(jnp.int32, (SEL_TILE, ncol), 1) % TOK)
                key_pos = j * SEL_TILE + lax.broadcasted_iota(jnp.int32, (SEL_TILE, ncol), 0)
                s_t = jnp.where(key_pos <= t_io, s_t, NEG_INF)
            m_old = m_ref[g, h]
            m_new = jnp.maximum(m_old, jnp.max(s_t, axis=0, keepdims=True))
            alpha = jnp.exp2(m_old - m_new)
            p = jnp.exp2(s_t - jnp.maximum(m_new, MASK_FLOOR))
            m_ref[g, h] = m_new
            p_ref[g, h] = p.astype(BF16)
            a_ref[g, h] = alpha

    scores(tile_at(0))

    def body(i, c):
        values(tile_at(jnp.maximum(i - 1, 0)))
        softmax(tile_at(i), False)
        scores(tile_at(i + 1))
        return c

    lax.fori_loop(0, count, body, 0)
    values(tile_at(jnp.maximum(count - 1, 0)))
    softmax(n_diag, True)
    values(n_diag)
    for h in range(SEL_Q):
        outs = []
        for g in range(NSA_KV_GROUPS):
            acc = acc_ref[g, h]
            gate = _gate_row(gt_ref[0, h], g, 1)
            outs.append(_heads_to_token_major(acc[:NSA_HEAD_DIM] * (gate / acc[NSA_HEAD_DIM:NSA_HEAD_DIM + 1])))
        o_ref[0, h * TOK:(h + 1) * TOK, :] = jnp.concatenate(outs, axis=1)


def _selected(qp, k, vt, bias, gt, tile_list, tile_count, bsz, s):
    nt = s // TOK
    nsel = s // NSA_SEL_BLOCK
    ncol = NSA_REP * TOK
    unit = (NSA_KV_GROUPS, SEL_Q)
    grid_spec = pltpu.PrefetchScalarGridSpec(
        num_scalar_prefetch=2,
        grid=(bsz, nt // SEL_Q),
        in_specs=[
            pl.BlockSpec((1, NSA_KV_GROUPS, SEL_Q * TOK, NSA_REP * LANES), lambda b, i, tl, tc: (b, 0, i, 0)),
            pl.BlockSpec((1, NSA_KV_GROUPS, s, LANES), lambda b, i, tl, tc: (b, 0, 0, 0)),
            pl.BlockSpec((1, NSA_KV_GROUPS, s // SEL_TILE, SEL_V_ROWS, SEL_TILE),
                         lambda b, i, tl, tc: (b, 0, 0, 0, 0)),
            pl.BlockSpec((1, NSA_KV_GROUPS, SEL_Q, nsel, TOK), lambda b, i, tl, tc: (b, 0, i, 0, 0)),
            pl.BlockSpec((1, SEL_Q, GATE_ROWS, TOK), lambda b, i, tl, tc: (b, i, 0, 0)),
        ],
        out_specs=pl.BlockSpec((1, SEL_Q * TOK, NSA_Q_WIDTH), lambda b, i, tl, tc: (b, i, 0)),
        scratch_shapes=[pltpu.VMEM(unit + (1, ncol), F32),
                        pltpu.VMEM(unit + (SEL_V_ROWS, ncol), F32),
                        pltpu.VMEM(unit + (SEL_TILE, ncol), F32),
                        pltpu.VMEM(unit + (SEL_TILE, ncol), BF16),
                        pltpu.VMEM(unit + (1, ncol), F32),
                        pltpu.VMEM((max(nsel // NSA_HEAD_DIM, 1),) + unit + (ncol, LANES), BF16)],
    )
    return pl.pallas_call(
        functools.partial(_selected_kernel, nt=nt, nsel=nsel),
        grid_spec=grid_spec,
        out_shape=jax.ShapeDtypeStruct((bsz, s, NSA_Q_WIDTH), F32),
        compiler_params=_cparams(("parallel", "parallel")),
        name="nsa_selected",
    )(tile_list, tile_count, qp, k, vt, bias, gt)


def _active_tiles(bias, bsz, s):
    n_q = s // (SEL_Q * TOK)
    n_kt = s // SEL_TILE
    rows = SEL_TILE // NSA_SEL_BLOCK * TOK
    act = bias.reshape(bsz, NSA_KV_GROUPS, n_q, SEL_Q, n_kt, rows).max(axis=(1, 3, 5)) > -1.0
    n_diag = (jnp.arange(n_q) * SEL_Q * TOK) // SEL_TILE
    act = act & (jnp.arange(n_kt)[None, None, :] < n_diag[None, :, None])
    order = jnp.argsort(jnp.where(act, 0, 1), axis=-1, stable=True).astype(jnp.int32)
    return order.reshape(-1), jnp.sum(act, axis=-1, dtype=jnp.int32).reshape(-1)


def _ssd_kernel(z_ref, xbc_ref, dt_ref, cw_ref, cb_ref, dtb_ref, alog_ref, dskip_ref, nw_ref, e8_ref,
                tri_ref, y_ref, ext_ref, st_ref):
    @pl.when(pl.program_id(0) == 0)
    def _():
        ext_ref[pl.ds(0, SUBLANES), :] = jnp.zeros((SUBLANES, SSM_CONV_DIM), F32)
        st_ref[...] = jnp.zeros_like(st_ref)

    xb = xbc_ref[...]
    ext_ref[pl.ds(SUBLANES, TOK), :] = xb
    conv = cb_ref[...] + xb * cw_ref[SSM_CONV - 1:SSM_CONV, :]
    for k in range(SSM_CONV - 1):
        conv = conv + ext_ref[pl.ds(SUBLANES - (SSM_CONV - 1) + k, TOK), :] * cw_ref[k:k + 1, :]
    ext_ref[pl.ds(0, SUBLANES), :] = xb[TOK - SUBLANES:TOK, :]
    u = _silu(conv)
    xs = u[:, :SSM_INNER]
    gn = SSM_GROUPS * SSM_STATE
    bm = u[:, SSM_INNER:SSM_INNER + gn]
    cm = u[:, SSM_INNER + gn:]
    dt = jax.nn.softplus(dt_ref[...] + dtb_ref[...])
    da = dt * (-jnp.exp(alog_ref[...]))
    tri = tri_ref[...]
    e8 = e8_ref[...]
    cum = _dot_sel_l(tri, da)
    cum_t = cum.T
    cum_e = _dot_sel_r(cum, e8)
    dt_e = _dot_sel_r(dt, e8)
    xdt = xs * dt_e
    lane = lax.broadcasted_iota(jnp.int32, (TOK, LANES), 1)
    t_io = lax.broadcasted_iota(jnp.int32, (TOK, TOK), 0)
    s_io = lax.broadcasted_iota(jnp.int32, (TOK, TOK), 1)
    causal = s_io <= t_io
    hg = SSM_HEADS // SSM_GROUPS
    y_parts = []
    cbs = []
    for g in range(SSM_GROUPS):
        in_g = (lane // SSM_STATE) == g
        cbs.append(_dot_nt(jnp.where(in_g, cm, 0.0).astype(BF16), bm.astype(BF16)))
    for c in range(SSM_HEADS // 2):
        acc = jnp.zeros((TOK, LANES), F32)
        xpair = xdt[:, c * LANES:(c + 1) * LANES]
        for hh in range(2):
            h = 2 * c + hh
            seg = cum[:, h:h + 1] - cum_t[h:h + 1, :]
            decay = jnp.where(causal, jnp.exp(jnp.where(causal, seg, 0.0)), 0.0)
            mat = (cbs[h // hg] * decay).astype(BF16)
            xh = jnp.where((lane // SSM_HEAD_DIM) == hh, xpair, 0.0).astype(BF16)
            acc = acc + _dot(mat, xh)
        y_parts.append(acc)
    y_diag = jnp.concatenate(y_parts, axis=1)
    prev = st_ref[...]
    y_off = _dot(cm.astype(BF16), prev.astype(BF16)) * jnp.exp(cum_e)
    cum_last = cum_e[TOK - 1:TOK, :]
    xdec = (xdt * jnp.exp(cum_last - cum_e)).astype(BF16)
    new = _dot(bm.T.astype(BF16), xdec)
    row_g = lax.broadcasted_iota(jnp.int32, (LANES, SSM_INNER), 0) // SSM_STATE
    col_g = lax.broadcasted_iota(jnp.int32, (LANES, SSM_INNER), 1) // (hg * SSM_HEAD_DIM)
    st_ref[...] = prev * jnp.exp(cum_last) + jnp.where(row_g == col_g, new, 0.0)
    y = (y_diag + y_off + xs * dskip_ref[...]) * _silu(z_ref[...])
    gw = SSM_INNER // SSM_GROUPS
    outs = []
    for g in range(SSM_GROUPS):
        yg = y[:, g * gw:(g + 1) * gw]
        outs.append(_rms(yg, nw_ref[:, g * gw:(g + 1) * gw]))
    y_ref[...] = jnp.concatenate(outs, axis=1).astype(BF16)


SSD_INPUTS = 11


def _hgrn2_kernel(q_ref, f_ref, i_ref, g_ref, lb_ref, nw_ref, tri_ref, y_ref, st_ref, kp_ref, bp_ref, vp_ref):
    @pl.when(pl.program_id(0) == 0)
    def _():
        st_ref[...] = jnp.zeros_like(st_ref)
        kp_ref[...] = jnp.zeros_like(kp_ref)
        bp_ref[...] = jnp.zeros_like(bp_ref)
        vp_ref[...] = jnp.zeros_like(vp_ref)

    nsub = TOK // HG_SUB
    in_sub = lax.broadcasted_iota(jnp.int32, (TOK, HG_KEY_DIM), 0) % HG_SUB
    tri = tri_ref[...]
    outs = []
    for h in range(HG_HEADS):
        sl = slice(h * HG_KEY_DIM, (h + 1) * HG_KEY_DIM)
        lb = lb_ref[:, sl]
        fr = f_ref[:, sl]
        q = _silu(q_ref[:, sl]) * (HG_KEY_DIM ** -0.5)
        log_f = jnp.log(lb + (1.0 - lb) * jax.nn.sigmoid(fr))
        k = (1.0 - lb) * jax.nn.sigmoid(-fr)
        v = i_ref[:, sl]
        bcum = _dot_sel_l(tri, log_f)
        st = st_ref[h]
        o = _dot_nt((q * jnp.exp(bcum)).astype(BF16), st.astype(BF16))
        ends = [bcum[(j + 1) * HG_SUB - 1:(j + 1) * HG_SUB, :] for j in range(nsub)]
        blk = (HG_SUB, HG_KEY_DIM)
        zero_blk = jnp.zeros(blk, BF16)
        e_prev = jnp.concatenate([jnp.zeros(blk, F32)] + [jnp.broadcast_to(ends[j], blk) for j in range(nsub - 1)],
                                 axis=0)
        e_own = jnp.concatenate([jnp.broadcast_to(ends[j], blk) for j in range(nsub)], axis=0)
        q_loc = (q * jnp.exp(bcum - e_prev)).astype(BF16)
        k_loc = k * jnp.exp(e_own - bcum)
        q_cols = []
        k_cols = []
        for i in range(1, nsub):
            q_cols.append(jnp.concatenate(
                [zero_blk] * i + [q_loc[i * HG_SUB:(i + 1) * HG_SUB]] + [zero_blk] * (nsub - 1 - i), axis=0))
            parts = []
            for j in range(i):
                kj = k_loc[j * HG_SUB:(j + 1) * HG_SUB]
                if j < i - 1:
                    kj = kj * jnp.exp(ends[i - 1] - ends[j])
                parts.append(kj.astype(BF16))
            k_cols.append(jnp.concatenate(parts + [zero_blk] * (nsub - i), axis=0))
        q_aug = jnp.concatenate(q_cols, axis=1)
        k_aug = jnp.concatenate(k_cols, axis=1)
        att = _dot_nt(q_aug, k_aug)
        o = o + _dot(att.astype(BF16), v.astype(BF16))
        kp_ref[pl.ds(HG_SUB, TOK), :] = k
        bp_ref[pl.ds(HG_SUB, TOK), :] = bcum
        vp_ref[pl.ds(HG_SUB, TOK), :] = v
        o = o + jnp.sum(q * k, axis=1, keepdims=True) * v
        for d in range(1, HG_SUB):
            ks = kp_ref[pl.ds(HG_SUB - d, TOK), :]
            bs = bp_ref[pl.ds(HG_SUB - d, TOK), :]
            vs = vp_ref[pl.ds(HG_SUB - d, TOK), :]
            ok = in_sub >= d
            w = jnp.sum(q * ks * jnp.exp(bcum - bs), axis=1, keepdims=True)
            o = o + jnp.where(ok, w * vs, 0.0)
        b_last = bcum[TOK - 1:TOK, :]
        kdec = (k * jnp.exp(b_last - bcum)).astype(BF16)
        st_ref[h] = st * jnp.exp(b_last) + _dot(v.T.astype(BF16), kdec)
        outs.append(_rms(o, nw_ref[...]) * _silu(g_ref[:, sl]))
    y_ref[...] = jnp.concatenate(outs, axis=1).astype(BF16)


HG_INPUTS = 7


SSD_BLOCKED = 3
HG_BLOCKED = 4


def _recurrent_kernel(*refs, bsz):
    n_in = SSD_INPUTS + HG_INPUTS
    ssd_out, hg_out = refs[n_in], refs[n_in + 1]
    scratch = refs[n_in + 2:]
    for b in range(bsz):
        ssd_in = [r.at[b] for r in refs[:SSD_BLOCKED]] + list(refs[SSD_BLOCKED:SSD_INPUTS])
        hg_in = ([r.at[b] for r in refs[SSD_INPUTS:SSD_INPUTS + HG_BLOCKED]]
                 + list(refs[SSD_INPUTS + HG_BLOCKED:n_in]))
        _ssd_kernel(*ssd_in, ssd_out.at[b], *(r.at[b] for r in scratch[:2]))
        _hgrn2_kernel(*hg_in, hg_out.at[b], *(r.at[b] for r in scratch[2:]))


def _recurrent_mixers(proj, cw, cb, dtb, alog, dskip, ssd_nw, e8, lb, hg_nw, tri, bsz, s):
    nt = s // TOK
    proj = proj.reshape(bsz, s, proj.shape[-1])

    def col(c0, w):
        return pl.BlockSpec((bsz, TOK, w), lambda i: (0, i, c0 // w))

    def full(shape):
        return pl.BlockSpec(shape, lambda i: (0,) * len(shape))

    def per_batch(shape):
        return pltpu.VMEM((bsz,) + shape, F32)

    pad = per_batch((TOK + HG_SUB, HG_KEY_DIM))
    ssd_specs = [col(COL_Z, SSM_INNER), col(COL_XBC, SSM_CONV_DIM), col(COL_DT, LANES),
                 full((SSM_CONV, SSM_CONV_DIM)), full((1, SSM_CONV_DIM)), full((1, LANES)), full((1, LANES)),
                 full((1, SSM_INNER)), full((1, SSM_INNER)), full((LANES, SSM_INNER)), full((TOK, TOK))]
    hg_specs = [col(COL_HQ, HG_WIDTH), col(COL_HF, HG_WIDTH), col(COL_HI, HG_WIDTH), col(COL_HG, HG_WIDTH),
                full((1, HG_WIDTH)), full((1, HG_VAL_DIM)), full((TOK, TOK))]
    assert len(ssd_specs) == SSD_INPUTS and len(hg_specs) == HG_INPUTS
    y_b, y_c = pl.pallas_call(
        functools.partial(_recurrent_kernel, bsz=bsz),
        grid=(nt,),
        in_specs=ssd_specs + hg_specs,
        out_specs=[pl.BlockSpec((bsz, TOK, SSM_INNER), lambda i: (0, i, 0)),
                   pl.BlockSpec((bsz, TOK, HG_WIDTH), lambda i: (0, i, 0))],
        out_shape=[jax.ShapeDtypeStruct((bsz, s, SSM_INNER), BF16),
                   jax.ShapeDtypeStruct((bsz, s, HG_WIDTH), BF16)],
        scratch_shapes=[per_batch((TOK + SUBLANES, SSM_CONV_DIM)), per_batch((LANES, SSM_INNER)),
                        per_batch((HG_HEADS, HG_VAL_DIM, HG_KEY_DIM)), pad, pad, pad],
        compiler_params=_cparams(("arbitrary",)),
        name="ssd_hgrn2",
    )(proj, proj, proj, cw, cb, dtb, alog, dskip, ssd_nw, e8, tri, proj, proj, proj, proj, lb, hg_nw, tri)
    return y_b.reshape(bsz * s, SSM_INNER), y_c.reshape(bsz * s, HG_WIDTH)


def _merge_kernel(ga_ref, gb_ref, gc_ref, oc_ref, os_ref, ow_ref, yb_ref, yc_ref, x_ref,
                  wa_ref, wb_ref, wc_ref, wo_ref, o_ref):
    nsa = oc_ref[...] + os_ref[...] + ow_ref[...]
    ya = _dot(nsa.astype(BF16), wa_ref[...])
    yb = _dot(yb_ref[...], wb_ref[...])
    yc = _dot(yc_ref[...], wc_ref[...])
    merged = (jax.nn.sigmoid(ga_ref[...].astype(F32)) * ya + jax.nn.sigmoid(gb_ref[...].astype(F32)) * yb
              + jax.nn.sigmoid(gc_ref[...].astype(F32)) * yc)
    o_ref[...] = x_ref[...] + _dot(merged.astype(BF16), wo_ref[...])


def _merge(gates, oc, os_, ow, yb, yc, x2, wa, wb, wc, wo):
    t = x2.shape[0]
    tm = min(512, t)

    def col(c0, w):
        return pl.BlockSpec((tm, w), lambda i: (i, c0 // w))

    def full(shape):
        return pl.BlockSpec(shape, lambda i: (0,) * len(shape))

    row512 = pl.BlockSpec((tm, 512), lambda i: (i, 0))
    return pl.pallas_call(
        _merge_kernel,
        grid=(t // tm,),
        in_specs=[col(COL_GA, D_MODEL), col(COL_GB, D_MODEL), col(COL_GC, D_MODEL),
                  row512, row512, row512, row512, row512,
                  pl.BlockSpec((tm, D_MODEL), lambda i: (i, 0)),
                  full((NSA_Q_WIDTH, D_MODEL)), full((SSM_INNER, D_MODEL)),
                  full((HG_WIDTH, D_MODEL)), full((D_MODEL, D_MODEL))],
        out_specs=pl.BlockSpec((tm, D_MODEL), lambda i: (i, 0)),
        out_shape=jax.ShapeDtypeStruct((t, D_MODEL), F32),
        compiler_params=_cparams(("parallel",)),
        name="merge",
    )(gates, gates, gates, oc, os_, ow, yb, yc, x2, wa, wb, wc, wo)


def _xa_mem_kernel(mem_ref, nm_ref, wk_ref, wv_ref, kn_ref, k_ref, v_ref):
    m = _rms(mem_ref[0], nm_ref[...]).astype(BF16)
    k = _dot(m, wk_ref[...])
    ks = [_rms(k[:, h * XA_HEAD_DIM:(h + 1) * XA_HEAD_DIM], kn_ref[...]) for h in range(XA_HEADS)]
    k_ref[0] = jnp.concatenate(ks, axis=1).astype(BF16)
    v_ref[0] = _dot(m, wv_ref[...]).astype(BF16)


def _xa_mem(mem, nm, wk, wv, kn):
    bsz, ml, _ = mem.shape

    def full(shape):
        return pl.BlockSpec(shape, lambda b: (0,) * len(shape))

    return pl.pallas_call(
        _xa_mem_kernel,
        grid=(bsz,),
        in_specs=[pl.BlockSpec((1, ml, D_MODEL), lambda b: (b, 0, 0)), full((1, D_MODEL)),
                  full((D_MODEL, XA_WIDTH)), full((D_MODEL, XA_WIDTH)), full((1, XA_HEAD_DIM))],
        out_specs=[pl.BlockSpec((1, ml, XA_WIDTH), lambda b: (b, 0, 0)),
                   pl.BlockSpec((1, ml, XA_WIDTH), lambda b: (b, 0, 0))],
        out_shape=[jax.ShapeDtypeStruct((bsz, ml, XA_WIDTH), BF16),
                   jax.ShapeDtypeStruct((bsz, ml, XA_WIDTH), BF16)],
        compiler_params=_cparams(("parallel",)),
        name="xa_mem",
    )(mem, nm, wk, wv, kn)


def _xa_kernel(x_ref, nx_ref, wq_ref, qn_ref, k_ref, v_ref, wo_ref, o_ref):
    x = x_ref[...]
    h = _rms(x, nx_ref[...]).astype(BF16)
    q = _dot(h, wq_ref[...])
    outs = []
    for hd in range(XA_HEADS):
        sl = slice(hd * XA_HEAD_DIM, (hd + 1) * XA_HEAD_DIM)
        qh = _rms(q[:, sl], qn_ref[...]).astype(BF16)
        sc = _dot_nt(qh, k_ref[0, :, sl]) * (XA_HEAD_DIM ** -0.5)
        sc = sc - jnp.max(sc, axis=-1, keepdims=True)
        p = jnp.exp(sc)
        p = p / jnp.sum(p, axis=-1, keepdims=True)
        outs.append(_dot(p.astype(BF16), v_ref[0, :, sl]))
    o = jnp.concatenate(outs, axis=1).astype(BF16)
    o_ref[...] = x + _dot(o, wo_ref[...])


def _xa(x2, nx, wq, qn, k, v, wo, s):
    t = x2.shape[0]
    tm = min(512, s)
    ml = k.shape[1]
    per_b = s // tm

    def full(shape):
        return pl.BlockSpec(shape, lambda i: (0,) * len(shape))

    return pl.pallas_call(
        _xa_kernel,
        grid=(t // tm,),
        in_specs=[pl.BlockSpec((tm, D_MODEL), lambda i: (i, 0)), full((1, D_MODEL)), full((D_MODEL, XA_WIDTH)),
                  full((1, XA_HEAD_DIM)),
                  pl.BlockSpec((1, ml, XA_WIDTH), lambda i: (i // per_b, 0, 0)),
                  pl.BlockSpec((1, ml, XA_WIDTH), lambda i: (i // per_b, 0, 0)),
                  full((XA_WIDTH, D_MODEL))],
        out_specs=pl.BlockSpec((tm, D_MODEL), lambda i: (i, 0)),
        out_shape=jax.ShapeDtypeStruct((t, D_MODEL), F32),
        compiler_params=_cparams(("parallel",)),
        name="cross_attn",
    )(x2, nx, wq, qn, k, v, wo)


def _ffn_kernel(x_ref, nw_ref, wg_ref, wu_ref, wd_ref, o_ref, h_ref, acc_ref):
    j = pl.program_id(1)

    @pl.when(j == 0)
    def _():
        h_ref[...] = _rms(x_ref[...], nw_ref[...]).astype(BF16)
        acc_ref[...] = jnp.zeros_like(acc_ref)

    h = h_ref[...]
    a = _silu(_dot(h, wg_ref[...])) * _dot(h, wu_ref[...])
    acc_ref[...] += _dot(a.astype(BF16), wd_ref[...])

    @pl.when(j == pl.num_programs(1) - 1)
    def _():
        o_ref[...] = x_ref[...] + acc_ref[...]


def _ffn(x2, nw, wg, wu, wd):
    t = x2.shape[0]
    tm = min(1024, t)
    th = FFN_HIDDEN // 2
    return pl.pallas_call(
        _ffn_kernel,
        grid=(t // tm, FFN_HIDDEN // th),
        in_specs=[pl.BlockSpec((tm, D_MODEL), lambda i, j: (i, 0)),
                  pl.BlockSpec((1, D_MODEL), lambda i, j: (0, 0)),
                  pl.BlockSpec((D_MODEL, th), lambda i, j: (0, j)),
                  pl.BlockSpec((D_MODEL, th), lambda i, j: (0, j)),
                  pl.BlockSpec((th, D_MODEL), lambda i, j: (j, 0))],
        out_specs=pl.BlockSpec((tm, D_MODEL), lambda i, j: (i, 0)),
        out_shape=jax.ShapeDtypeStruct((t, D_MODEL), F32),
        scratch_shapes=[pltpu.VMEM((tm, D_MODEL), BF16), pltpu.VMEM((tm, D_MODEL), F32)],
        compiler_params=_cparams(("parallel", "arbitrary")),
        name="ffn",
    )(x2, nw, wg, wu, wd)


def _pack_w_in(w):
    parts = {}
    off = 0
    names = ("q", "kc", "vc", "ks", "vs", "kw", "vw", "ng", "z", "xbc", "dt", "hq", "hf", "hi", "hg",
             "ga", "gb", "gc")
    for name, size in zip(names, SPLIT_SIZES):
        parts[name] = w[:, off:off + size]
        off += size

    def padded(a, width):
        return jnp.pad(a, ((0, 0), (0, width - a.shape[1])))

    cols = [parts["ga"], parts["gb"], parts["gc"], parts["q"], parts["z"], parts["hq"], parts["hf"],
            parts["hi"], parts["hg"], parts["xbc"], parts["kc"], parts["vc"], parts["ks"], parts["vs"],
            parts["kw"], parts["vw"], padded(parts["ng"], LANES), padded(parts["dt"], LANES)]
    packed = jnp.concatenate(cols, axis=1)
    return padded(packed, PROJ_WIDTH).astype(BF16)


def _rope_tables(pos):
    half = NSA_HEAD_DIM // 2
    inv = 1.0 / (ROPE_THETA ** (jnp.arange(half, dtype=F32) / half))
    ang = pos.astype(F32)[:, None] * inv[None, :]
    reps = LANES // half
    return jnp.tile(jnp.cos(ang), (1, reps)), jnp.tile(jnp.sin(ang), (1, reps))


def _block_diag2(w):
    z = jnp.zeros_like(w)
    return jnp.concatenate([jnp.concatenate([w, z], axis=-1), jnp.concatenate([z, w], axis=-1)], axis=-2)


def kernel(x, mem, norm_mix, w_in, nsa_q_norm, nsa_k_norm, nsa_cmp_pos_k, nsa_cmp_pos_v, nsa_cmp_w1_k, nsa_cmp_w2_k, nsa_cmp_w1_v, nsa_cmp_w2_v, w_nsa_o, ssm_conv_w, ssm_conv_b, ssm_dt_bias, ssm_a_log, ssm_d, ssm_norm, w_ssm_o, hg_lb_logits, hg_norm, w_hg_o, w_out, norm_xa, norm_mem, xa_w_q, xa_w_k, xa_w_v, xa_q_norm, xa_k_norm, xa_w_o, norm_ffn, ffn_w_gate, ffn_w_up, ffn_w_down):
    bsz, s, d = x.shape
    depth = w_in.shape[0]
    assert d == D_MODEL and PREP_TILES * TOK == SEL_TILE
    assert s % SEL_TILE == 0 and s % (SEL_Q * TOK) == 0 and s // TOK >= WIN_TILES
    assert s // NSA_SEL_BLOCK >= NSA_TOPK and (s // NSA_CMP_STRIDE) % (4 * SUBLANES) == 0
    t = bsz * s
    hd = NSA_HEAD_DIM

    cos_t, sin_t = _rope_tables(jnp.arange(s))
    nbp = s // NSA_CMP_STRIDE
    cos_c, sin_c = _rope_tables(jnp.arange(nbp) * NSA_CMP_STRIDE + NSA_CMP_BLOCK - 1)
    tri = jnp.tril(jnp.ones((TOK, TOK), F32)).astype(BF16)
    rows = np.arange(LANES)[:, None]
    cols = np.arange(NSA_Q_WIDTH)[None, :]
    e8 = jnp.asarray(rows == cols // SSM_HEAD_DIM, F32).astype(BF16)

    lb_sm = jax.nn.softmax(hg_lb_logits.astype(F32), axis=0)
    lb_all = jnp.cumsum(lb_sm, axis=0) - lb_sm[0:1]

    def pad_lanes(v):
        return jnp.pad(v, (0, LANES - v.shape[0]))[None, :]

    x2 = x.reshape(t, d)
    for l in range(depth):
        gates, proj = _in_proj(x2, norm_mix[l][None, :], _pack_w_in(w_in[l]))

        qn2 = jnp.tile(nsa_q_norm[l], 2)[None, :]
        kn2 = jnp.tile(nsa_k_norm[l], 2)[None, :]
        qp, ksn, kwn, vst, vwt, gt = _nsa_prep(proj, cos_t, sin_t, qn2, kn2, bsz, s)
        w1k = _block_diag2(nsa_cmp_w1_k[l].reshape(NSA_CMP_BLOCK, hd, hd)).astype(BF16)
        w1v = _block_diag2(nsa_cmp_w1_v[l].reshape(NSA_CMP_BLOCK, hd, hd)).astype(BF16)
        kc, vct = _compress(proj, jnp.tile(nsa_cmp_pos_k[l], (1, 2)), jnp.tile(nsa_cmp_pos_v[l], (1, 2)),
                            w1k, _block_diag2(nsa_cmp_w2_k[l]).astype(BF16),
                            w1v, _block_diag2(nsa_cmp_w2_v[l]).astype(BF16), kn2, cos_c, sin_c, bsz, s)
        o_c, sel_bias = _cmp_topk(qp, kc, vct, gt, bsz, s)
        tile_list, tile_count = _active_tiles(sel_bias, bsz, s)
        o_s = _selected(qp, ksn, vst, sel_bias, gt, tile_list, tile_count, bsz, s)
        o_w = _window(qp, kwn, vwt, gt, bsz, s)

        y_b, y_c = _recurrent_mixers(
            proj, ssm_conv_w[l], ssm_conv_b[l][None, :], pad_lanes(ssm_dt_bias[l]), pad_lanes(ssm_a_log[l]),
            jnp.repeat(ssm_d[l], SSM_HEAD_DIM)[None, :], ssm_norm[l][None, :], e8,
            lb_all[l][None, :], hg_norm[l][None, :], tri, bsz, s)

        x2 = _merge(gates, o_c.reshape(t, -1), o_s.reshape(t, -1), o_w.reshape(t, -1), y_b, y_c, x2,
                    w_nsa_o[l].astype(BF16), w_ssm_o[l].astype(BF16), w_hg_o[l].astype(BF16),
                    w_out[l].astype(BF16))

        k_m, v_m = _xa_mem(mem, norm_mem[l][None, :], xa_w_k[l].astype(BF16), xa_w_v[l].astype(BF16),
                           xa_k_norm[l][None, :])
        x2 = _xa(x2, norm_xa[l][None, :], xa_w_q[l].astype(BF16), xa_q_norm[l][None, :], k_m, v_m,
                 xa_w_o[l].astype(BF16), s)
        x2 = _ffn(x2, norm_ffn[l][None, :], ffn_w_gate[l].astype(BF16), ffn_w_up[l].astype(BF16),
                  ffn_w_down[l].astype(BF16))
    return x2.reshape(bsz, s, d)
```

```python
import functools
import math

import numpy as np
import jax
import jax.numpy as jnp
from jax import lax
from jax.experimental import pallas as pl
from jax.experimental.pallas import tpu as pltpu

F32 = jnp.float32
BF16 = jnp.bfloat16

D_MODEL = 1024
NORM_EPS = 1e-6
ROPE_THETA = 10000.0
NEG_INF = -1e30

NSA_HEADS = 8
NSA_KV_GROUPS = 2
NSA_REP = NSA_HEADS // NSA_KV_GROUPS
NSA_HEAD_DIM = 64
NSA_CMP_BLOCK = 32
NSA_CMP_STRIDE = 16
NSA_SEL_BLOCK = 64
NSA_TOPK = 16
N_FORCED = 3
NSA_WINDOW = 512
NSA_Q_WIDTH = NSA_HEADS * NSA_HEAD_DIM
NSA_KV_WIDTH = NSA_KV_GROUPS * NSA_HEAD_DIM

SSM_INNER = 512
SSM_HEAD_DIM = 64
SSM_HEADS = SSM_INNER // SSM_HEAD_DIM
SSM_GROUPS = 2
SSM_STATE = 64
SSM_CONV = 4
SSM_CONV_DIM = SSM_INNER + 2 * SSM_GROUPS * SSM_STATE

HG_HEADS = 4
HG_KEY_DIM = 128
HG_VAL_DIM = 128
HG_WIDTH = HG_HEADS * HG_KEY_DIM
HG_SUB = 16

XA_HEADS = 4
XA_HEAD_DIM = 128
XA_WIDTH = XA_HEADS * XA_HEAD_DIM

FFN_HIDDEN = -(-(8 * D_MODEL) // (3 * 256)) * 256

SPLIT_SIZES = (
    NSA_Q_WIDTH, NSA_KV_WIDTH, NSA_KV_WIDTH, NSA_KV_WIDTH, NSA_KV_WIDTH, NSA_KV_WIDTH, NSA_KV_WIDTH,
    3 * NSA_HEADS, SSM_INNER, SSM_CONV_DIM, SSM_HEADS, HG_WIDTH, HG_WIDTH, HG_HEADS * HG_VAL_DIM,
    HG_HEADS * HG_VAL_DIM, D_MODEL, D_MODEL, D_MODEL,
)

LANES = 128
SUBLANES = 8
TOK = 128
NSA_BRANCHES = 3
GATE_ROWS = -(-NSA_BRANCHES * NSA_HEADS // SUBLANES) * SUBLANES
PREP_TILES = 4
SEL_Q = 4
SEL_TILE = 512
SEL_V_ROWS = NSA_HEAD_DIM + 16
WIN_TILES = NSA_WINDOW // TOK + 1
LOG2_E = math.log2(math.e)
MASK_FLOOR = -1e20
VMEM_LIMIT = 56 * 1024 * 1024

PROJ_WIDTH = 8192
GATE_WIDTH = 3 * D_MODEL
COL_GA, COL_GB, COL_GC = 0, 1024, 2048
COL_Q = 0
COL_Z = 512
COL_HQ, COL_HF, COL_HI, COL_HG = 1024, 1536, 2048, 2560
COL_XBC = 3072
COL_KC, COL_VC, COL_KS, COL_VS, COL_KW, COL_VW = 3840, 3968, 4096, 4224, 4352, 4480
COL_NG = 4608
COL_DT = 4736


def _cparams(sem):
    return pltpu.CompilerParams(dimension_semantics=sem, vmem_limit_bytes=VMEM_LIMIT)


def _dot(a, b):
    return jnp.dot(a, b, preferred_element_type=F32)


def _dot_nt(a, b):
    return lax.dot_general(a, b, (((1,), (1,)), ((), ())), preferred_element_type=F32)


def _split3(a):
    hi = a.astype(BF16)
    r1 = a - hi.astype(F32)
    mid = r1.astype(BF16)
    lo = (r1 - mid.astype(F32)).astype(BF16)
    return hi, mid, lo


def _dot_sel_r(a, sel):
    hi, mid, lo = _split3(a)
    return _dot(hi, sel) + _dot(mid, sel) + _dot(lo, sel)


def _dot_sel_l(sel, a):
    hi, mid, lo = _split3(a)
    return _dot(sel, hi) + _dot(sel, mid) + _dot(sel, lo)


def _silu(x):
    return x * jax.nn.sigmoid(x)


def _rms(x, w):
    return x * lax.rsqrt(jnp.mean(x * x, axis=-1, keepdims=True) + NORM_EPS) * w


def _inproj_kernel(x_ref, nw_ref, w_ref, og_ref, of_ref, h_ref, *, n_gate):
    j = pl.program_id(1)

    @pl.when(j == 0)
    def _():
        h_ref[...] = _rms(x_ref[...], nw_ref[...]).astype(BF16)

    @pl.when(j < n_gate)
    def _():
        og_ref[...] = _dot(h_ref[...], w_ref[...]).astype(BF16)

    @pl.when(j >= n_gate)
    def _():
        of_ref[...] = _dot(h_ref[...], w_ref[...])


def _in_proj(x2, norm_w, w_packed):
    t = x2.shape[0]
    tm = min(2048, t)
    tn = 1024
    n_gate = GATE_WIDTH // tn
    return pl.pallas_call(
        functools.partial(_inproj_kernel, n_gate=n_gate),
        grid=(t // tm, PROJ_WIDTH // tn),
        in_specs=[
            pl.BlockSpec((tm, D_MODEL), lambda i, j: (i, 0)),
            pl.BlockSpec((1, D_MODEL), lambda i, j: (0, 0)),
            pl.BlockSpec((D_MODEL, tn), lambda i, j: (0, j)),
        ],
        out_specs=[pl.BlockSpec((tm, tn), lambda i, j: (i, jnp.minimum(j, n_gate - 1))),
                   pl.BlockSpec((tm, tn), lambda i, j: (i, jnp.maximum(j - n_gate, 0)))],
        out_shape=[jax.ShapeDtypeStruct((t, GATE_WIDTH), BF16),
                   jax.ShapeDtypeStruct((t, PROJ_WIDTH - GATE_WIDTH), F32)],
        scratch_shapes=[pltpu.VMEM((tm, D_MODEL), BF16)],
        compiler_params=_cparams(("parallel", "arbitrary")),
        name="in_proj",
    )(x2, norm_w, w_packed)


def _norm_rope_pair(x, w, cos, sin):
    lane = lax.broadcasted_iota(jnp.int32, x.shape, 1)
    lo = lane < NSA_HEAD_DIM
    sq = x * x
    s_lo = jnp.sum(jnp.where(lo, sq, 0.0), axis=1, keepdims=True)
    s_hi = jnp.sum(jnp.where(lo, 0.0, sq), axis=1, keepdims=True)
    ms = jnp.where(lo, s_lo, s_hi) * (1.0 / NSA_HEAD_DIM)
    y = x * lax.rsqrt(ms + NORM_EPS) * w
    half = NSA_HEAD_DIM // 2
    first = (lane % NSA_HEAD_DIM) < half
    rot = jnp.where(first, -pltpu.roll(y, LANES - half, 1), pltpu.roll(y, half, 1))
    return y * cos + rot * sin


def _nsa_prep_kernel(q_ref, ks_ref, vs_ref, kw_ref, vw_ref, ng_ref, cos_ref, sin_ref, qn_ref, kn_ref,
                     qp_ref, ksn_ref, kwn_ref, vst_ref, vwt_ref, gt_ref):
    lane = lax.broadcasted_iota(jnp.int32, (TOK, LANES), 1)
    row = lax.broadcasted_iota(jnp.int32, (TOK, LANES), 0)
    scale = NSA_HEAD_DIM ** -0.5 * LOG2_E
    ones_rows = (lax.broadcasted_iota(jnp.int32, (SEL_V_ROWS - NSA_HEAD_DIM, TOK), 0) == 0).astype(BF16)
    for u in range(PREP_TILES):
        rows = slice(u * TOK, (u + 1) * TOK)
        cos = cos_ref[rows, :]
        sin = sin_ref[rows, :]
        for c in range(NSA_HEADS // 2):
            y = _norm_rope_pair(q_ref[rows, c * LANES:(c + 1) * LANES], qn_ref[...], cos, sin) * scale
            y_sw = pltpu.roll(y, NSA_HEAD_DIM, 1)
            for hh in range(2):
                h = 2 * c + hh
                g, r = divmod(h, NSA_REP)
                src = y if hh == g else y_sw
                keep = (lane // NSA_HEAD_DIM) == g
                qp_ref[0, g, rows, r * LANES:(r + 1) * LANES] = jnp.where(keep, src, 0.0).astype(BF16)
        ks_n = _norm_rope_pair(ks_ref[rows, :], kn_ref[...], cos, sin)
        tile = pl.program_id(1) * PREP_TILES + u
        blk = (tile * (TOK // NSA_SEL_BLOCK) + row // NSA_SEL_BLOCK) % NSA_HEAD_DIM
        onehot = ((lane % NSA_HEAD_DIM) == blk).astype(F32)
        for g in range(NSA_KV_GROUPS):
            ksn_ref[0, g, rows, :] = jnp.where((lane // NSA_HEAD_DIM) == g, ks_n, onehot).astype(BF16)
        kwn_ref[0, rows, :] = _norm_rope_pair(kw_ref[rows, :], kn_ref[...], cos, sin).astype(BF16)
        vs_t = vs_ref[rows, :].T.astype(BF16)
        vwt_ref[0, rows, :] = vw_ref[rows, :].astype(BF16)
        for g in range(NSA_KV_GROUPS):
            vst_ref[0, g, 0, 0:NSA_HEAD_DIM, rows] = vs_t[g * NSA_HEAD_DIM:(g + 1) * NSA_HEAD_DIM, :]
            vst_ref[0, g, 0, NSA_HEAD_DIM:SEL_V_ROWS, rows] = ones_rows
        gt_ref[0, u] = jax.nn.sigmoid(ng_ref[rows, :]).T[0:GATE_ROWS, :]


def _nsa_prep(proj, cos_t, sin_t, qn2, kn2, bsz, s):
    tq = PREP_TILES * TOK
    n_step = s // tq

    def col(c0, w):
        return pl.BlockSpec((tq, w), lambda b, i: (b * n_step + i, c0 // w))

    tab = pl.BlockSpec((tq, LANES), lambda b, i: (i, 0))
    vec = pl.BlockSpec((1, LANES), lambda b, i: (0, 0))
    return pl.pallas_call(
        _nsa_prep_kernel,
        grid=(bsz, n_step),
        in_specs=[col(COL_Q, NSA_Q_WIDTH), col(COL_KS, LANES), col(COL_VS, LANES), col(COL_KW, LANES),
                  col(COL_VW, LANES), col(COL_NG, LANES), tab, tab, vec, vec],
        out_specs=[
            pl.BlockSpec((1, NSA_KV_GROUPS, tq, NSA_REP * LANES), lambda b, i: (b, 0, i, 0)),
            pl.BlockSpec((1, NSA_KV_GROUPS, tq, LANES), lambda b, i: (b, 0, i, 0)),
            pl.BlockSpec((1, tq, LANES), lambda b, i: (b, i, 0)),
            pl.BlockSpec((1, NSA_KV_GROUPS, 1, SEL_V_ROWS, SEL_TILE), lambda b, i: (b, 0, i, 0, 0)),
            pl.BlockSpec((1, tq, LANES), lambda b, i: (b, i, 0)),
            pl.BlockSpec((1, PREP_TILES, GATE_ROWS, TOK), lambda b, i: (b, i, 0, 0)),
        ],
        out_shape=[
            jax.ShapeDtypeStruct((bsz, NSA_KV_GROUPS, s, NSA_REP * LANES), BF16),
            jax.ShapeDtypeStruct((bsz, NSA_KV_GROUPS, s, LANES), BF16),
            jax.ShapeDtypeStruct((bsz, s, LANES), BF16),
            jax.ShapeDtypeStruct((bsz, NSA_KV_GROUPS, s // SEL_TILE, SEL_V_ROWS, SEL_TILE), BF16),
            jax.ShapeDtypeStruct((bsz, s, LANES), BF16),
            jax.ShapeDtypeStruct((bsz, s // TOK, GATE_ROWS, TOK), F32),
        ],
        compiler_params=_cparams(("parallel", "parallel")),
        name="nsa_prep",
    )(proj, proj, proj, proj, proj, proj, cos_t, sin_t, qn2, kn2)


def _compress_kernel(kc_ref, vc_ref, posk_ref, posv_ref, w1k_ref, w2k_ref, w1v_ref, w2v_ref, kn_ref,
                     cos_ref, sin_ref, kco_ref, vct_ref, sh_ref, *, nbp):
    half_blk = NSA_CMP_BLOCK // 2

    def compress(x_ref, pos_ref, w1_ref, w2_ref):
        acc_a = jnp.zeros((nbp, LANES), F32)
        acc_b = jnp.zeros((nbp, LANES), F32)
        for l in range(half_blk):
            xl = x_ref[pl.ds(l, nbp, stride=NSA_CMP_STRIDE), :]
            acc_a = acc_a + _dot((xl + pos_ref[l:l + 1, :]).astype(BF16), w1_ref[l])
            acc_b = acc_b + _dot((xl + pos_ref[half_blk + l:half_blk + l + 1, :]).astype(BF16),
                                 w1_ref[half_blk + l])
        sh_ref[pl.ds(0, nbp), :] = acc_b
        sh_ref[pl.ds(nbp, SUBLANES), :] = jnp.zeros((SUBLANES, LANES), F32)
        hid = acc_a + sh_ref[pl.ds(1, nbp), :]
        return _dot(_silu(hid).astype(BF16), w2_ref[...])

    kc = compress(kc_ref, posk_ref, w1k_ref, w2k_ref)
    kco_ref[0] = _norm_rope_pair(kc, kn_ref[...], cos_ref[...], sin_ref[...]).astype(BF16)
    vc = compress(vc_ref, posv_ref, w1v_ref, w2v_ref)
    vct_ref[0] = vc.T.astype(BF16)


def _compress(proj, posk, posv, w1k, w2k, w1v, w2v, kn2, cos_c, sin_c, bsz, s):
    nbp = s // NSA_CMP_STRIDE

    def full(shape):
        return pl.BlockSpec(shape, lambda b: (0,) * len(shape))

    return pl.pallas_call(
        functools.partial(_compress_kernel, nbp=nbp),
        grid=(bsz,),
        in_specs=[
            pl.BlockSpec((s, LANES), lambda b: (b, COL_KC // LANES)),
            pl.BlockSpec((s, LANES), lambda b: (b, COL_VC // LANES)),
            full((NSA_CMP_BLOCK, LANES)), full((NSA_CMP_BLOCK, LANES)),
            full((NSA_CMP_BLOCK, LANES, LANES)), full((LANES, LANES)),
            full((NSA_CMP_BLOCK, LANES, LANES)), full((LANES, LANES)),
            full((1, LANES)), full((nbp, LANES)), full((nbp, LANES)),
        ],
        out_specs=[
            pl.BlockSpec((1, nbp, LANES), lambda b: (b, 0, 0)),
            pl.BlockSpec((1, LANES, nbp), lambda b: (b, 0, 0)),
        ],
        out_shape=[
            jax.ShapeDtypeStruct((bsz, nbp, LANES), BF16),
            jax.ShapeDtypeStruct((bsz, LANES, nbp), BF16),
        ],
        scratch_shapes=[pltpu.VMEM((nbp + SUBLANES, LANES), F32)],
        compiler_params=_cparams(("parallel",)),
        name="nsa_compress",
    )(proj, proj, posk, posv, w1k, w2k, w1v, w2v, kn2, cos_c, sin_c)


def _q_rows(qp):
    return jnp.concatenate([qp[:, r * LANES:(r + 1) * LANES] for r in range(NSA_REP)], axis=0)


def _gate_row(gt, g, branch):
    rows = [(g * NSA_REP + r) * NSA_BRANCHES + branch for r in range(NSA_REP)]
    return jnp.concatenate([gt[i:i + 1, :] for i in rows], axis=1)


def _heads_to_token_major(acc_g):
    outs = []
    for p in range(NSA_REP // 2):
        blk = jnp.concatenate([acc_g[:, (2 * p) * TOK:(2 * p + 1) * TOK],
                               acc_g[:, (2 * p + 1) * TOK:(2 * p + 2) * TOK]], axis=0)
        outs.append(blk.T)
    return jnp.concatenate(outs, axis=1)


def _cmp_topk_kernel(qp_ref, kc_ref, vct_ref, gt_ref, oc_ref, mask_ref, ps_ref, *, nbp, nsel):
    qi = pl.program_id(1)
    ncol = NSA_REP * TOK
    per = NSA_SEL_BLOCK // NSA_CMP_STRIDE

    def attend(nrows):
        nblk = nrows // per
        j_io = lax.broadcasted_iota(jnp.int32, (nblk, TOK), 0)
        t_sel = qi * TOK + lax.broadcasted_iota(jnp.int32, (nblk, TOK), 1)
        cur = t_sel // NSA_SEL_BLOCK
        forced = (j_io == 0) | (j_io == cur) | (j_io == cur - 1)
        valid = j_io * NSA_SEL_BLOCK <= t_sel
        rel = (lax.broadcasted_iota(jnp.int32, (nrows, ncol), 0) * NSA_CMP_STRIDE + (NSA_CMP_BLOCK - 1)
               - lax.broadcasted_iota(jnp.int32, (nrows, ncol), 1) % TOK)
        allowed = rel <= qi * TOK
        kc = kc_ref[0, 0:nrows, :]
        vct = vct_ref[0, :, 0:nrows]
        for g in range(NSA_KV_GROUPS):
            ps_ref[g, pl.ds(0, SUBLANES), :] = jnp.zeros((SUBLANES, TOK), F32)
        scores = [_dot_nt(kc, _q_rows(qp_ref[0, g])) for g in range(NSA_KV_GROUPS)]
        outs = []
        imps = []
        for g in range(NSA_KV_GROUPS):
            s_t = jnp.where(allowed, scores[g], NEG_INF)
            m = jnp.max(s_t, axis=0, keepdims=True)
            p = jnp.exp2(s_t - jnp.maximum(m, MASK_FLOOR))
            l = jnp.sum(p, axis=0, keepdims=True)
            inv = jnp.where(l > 0.0, 1.0 / l, 0.0)
            o_t = _dot(vct, p.astype(BF16)) * inv
            gate = _gate_row(gt_ref[0, 0], g, 0)
            outs.append(_heads_to_token_major(o_t[g * NSA_HEAD_DIM:(g + 1) * NSA_HEAD_DIM, :] * gate))
            pn = p * inv
            psum = pn[:, 0:TOK]
            for r in range(1, NSA_REP):
                psum = psum + pn[:, r * TOK:(r + 1) * TOK]
            ps_ref[g, pl.ds(SUBLANES, nrows), :] = psum
            imp = (ps_ref[g, pl.ds(SUBLANES - 1, nblk, stride=per), :]
                   + ps_ref[g, pl.ds(SUBLANES + per - 1, nblk, stride=per), :])
            for c in range(per - 1):
                imp = imp + 2.0 * ps_ref[g, pl.ds(SUBLANES + c, nblk, stride=per), :]
            imps.append(jnp.where(forced, -jnp.inf, jnp.where(valid, imp, NEG_INF)))
        oc_ref[0] = jnp.concatenate(outs, axis=1)

        def pick(_, carry):
            nxt = []
            for imp_c in carry:
                mx = jnp.max(imp_c, axis=0, keepdims=True)
                idx = jnp.min(jnp.where(imp_c == mx, j_io, nblk), axis=0, keepdims=True)
                nxt.append(jnp.where(j_io == idx, -jnp.inf, imp_c))
            return tuple(nxt)

        picked = lax.fori_loop(0, min(NSA_TOPK, nsel) - N_FORCED, pick, tuple(imps))
        for g in range(NSA_KV_GROUPS):
            mask_ref[0, g, 0, 0:nblk, :] = jnp.where(valid & (picked[g] == -jnp.inf), 0.0, NEG_INF)
            if nblk < nsel:
                mask_ref[0, g, 0, nblk:nsel, :] = jnp.full((nsel - nblk, TOK), NEG_INF, F32)

    n_vis = (qi * TOK + TOK - NSA_CMP_BLOCK) // NSA_CMP_STRIDE + 1
    quarter = nbp // 4
    for k in range(1, 5):
        @pl.when((n_vis > (k - 1) * quarter) & (n_vis <= k * quarter))
        def _(k=k):
            attend(k * quarter)


def _cmp_topk(qp, kc, vct, gt, bsz, s):
    nt = s // TOK
    nbp = s // NSA_CMP_STRIDE
    nsel = s // NSA_SEL_BLOCK
    return pl.pallas_call(
        functools.partial(_cmp_topk_kernel, nbp=nbp, nsel=nsel),
        grid=(bsz, nt),
        in_specs=[
            pl.BlockSpec((1, NSA_KV_GROUPS, TOK, NSA_REP * LANES), lambda b, i: (b, 0, i, 0)),
            pl.BlockSpec((1, nbp, LANES), lambda b, i: (b, 0, 0)),
            pl.BlockSpec((1, LANES, nbp), lambda b, i: (b, 0, 0)),
            pl.BlockSpec((1, 1, GATE_ROWS, TOK), lambda b, i: (b, i, 0, 0)),
        ],
        out_specs=[
            pl.BlockSpec((1, TOK, NSA_Q_WIDTH), lambda b, i: (b, i, 0)),
            pl.BlockSpec((1, NSA_KV_GROUPS, 1, nsel, TOK), lambda b, i: (b, 0, i, 0, 0)),
        ],
        out_shape=[
            jax.ShapeDtypeStruct((bsz, s, NSA_Q_WIDTH), F32),
            jax.ShapeDtypeStruct((bsz, NSA_KV_GROUPS, nt, nsel, TOK), F32),
        ],
        scratch_shapes=[pltpu.VMEM((NSA_KV_GROUPS, nbp + SUBLANES, TOK), F32)],
        compiler_params=_cparams(("parallel", "parallel")),
        name="nsa_cmp_topk",
    )(qp, kc, vct, gt)


def _window_kernel(qp_ref, k_ref, v_ref, ng_ref, o_ref):
    qi = pl.program_id(1)
    nrow = NSA_REP * TOK
    wk = WIN_TILES * TOK
    lane = lax.broadcasted_iota(jnp.int32, (TOK, LANES), 1)
    first_half = lane < NSA_HEAD_DIM
    sg = jax.nn.sigmoid(ng_ref[...])

    def run(first_tile, interior):
        start = pl.multiple_of(first_tile * TOK, TOK)
        kwin = k_ref[0, pl.ds(start, wk), :]
        vwin = v_ref[0, pl.ds(start, wk), :]
        v_lane = lax.broadcasted_iota(jnp.int32, (wk, LANES), 1) // NSA_HEAD_DIM
        scores = [_dot_nt(_q_rows(qp_ref[0, g]), kwin) for g in range(NSA_KV_GROUPS)]
        heads = [None] * NSA_HEADS
        for g in range(NSA_KV_GROUPS):
            s_q = scores[g]
            if interior:
                t_loc = lax.broadcasted_iota(jnp.int32, (nrow, TOK), 0) % TOK
                k_loc = lax.broadcasted_iota(jnp.int32, (nrow, TOK), 1)
                oldest = jnp.where(k_loc > t_loc, s_q[:, :TOK], NEG_INF)
                newest = jnp.where(k_loc <= t_loc, s_q[:, wk - TOK:], NEG_INF)
                s_q = jnp.concatenate([oldest, s_q[:, TOK:wk - TOK], newest], axis=1)
            else:
                key = first_tile * TOK + lax.broadcasted_iota(jnp.int32, (nrow, wk), 1)
                t_io = qi * TOK + lax.broadcasted_iota(jnp.int32, (nrow, wk), 0) % TOK
                s_q = jnp.where((key <= t_io) & (t_io - key < NSA_WINDOW), s_q, NEG_INF)
            m = jnp.max(s_q, axis=1, keepdims=True)
            p = jnp.exp2(s_q - m).astype(BF16)
            o = _dot(p, jnp.where(v_lane == g, vwin, jnp.ones_like(vwin)))
            o = o / pltpu.roll(o, NSA_HEAD_DIM, 1)
            for r in range(NSA_REP):
                h = g * NSA_REP + r
                gate = sg[:, NSA_BRANCHES * h + 2:NSA_BRANCHES * h + 3]
                y = o[r * TOK:(r + 1) * TOK, :] * gate
                heads[h] = y if h % 2 == g else pltpu.roll(y, NSA_HEAD_DIM, 1)
        o_ref[0] = jnp.concatenate(
            [jnp.where(first_half, heads[2 * c], heads[2 * c + 1]) for c in range(NSA_HEADS // 2)], axis=1)

    @pl.when(qi >= WIN_TILES - 1)
    def _():
        run(qi - (WIN_TILES - 1), True)

    @pl.when(qi < WIN_TILES - 1)
    def _():
        run(0, False)


def _window(qp, k, v, proj, bsz, s):
    nt = s // TOK
    return pl.pallas_call(
        _window_kernel,
        grid=(bsz, nt),
        in_specs=[
            pl.BlockSpec((1, NSA_KV_GROUPS, TOK, NSA_REP * LANES), lambda b, i: (b, 0, i, 0)),
            pl.BlockSpec((1, s, LANES), lambda b, i: (b, 0, 0)),
            pl.BlockSpec((1, s, LANES), lambda b, i: (b, 0, 0)),
            pl.BlockSpec((TOK, LANES), lambda b, i: (b * nt + i, COL_NG // LANES)),
        ],
        out_specs=pl.BlockSpec((1, TOK, NSA_Q_WIDTH), lambda b, i: (b, i, 0)),
        out_shape=jax.ShapeDtypeStruct((bsz, s, NSA_Q_WIDTH), F32),
        compiler_params=_cparams(("parallel", "parallel")),
        name="nsa_window",
    )(qp, k, v, proj)


def _selected_kernel(list_ref, cnt_ref, qp_ref, k_ref, vt_ref, bias_ref, gt_ref, o_ref, m_ref, acc_ref,
                     s_ref, p_ref, a_ref, qa_ref, *, nt, nsel):
    b = pl.program_id(0)
    qi = pl.program_id(1)
    ncol = NSA_REP * TOK
    blk_per_tile = SEL_TILE // NSA_SEL_BLOCK
    n_kt = nt // (SEL_TILE // TOK)
    n_q = nt // SEL_Q
    units = [(g, h) for g in range(NSA_KV_GROUPS) for h in range(SEL_Q)]
    m_ref[...] = jnp.full(m_ref.shape, NEG_INF, F32)
    acc_ref[...] = jnp.zeros(acc_ref.shape, F32)
    p_ref[...] = jnp.zeros(p_ref.shape, BF16)
    a_ref[...] = jnp.ones(a_ref.shape, F32)
    base = (b * n_q + qi) * n_kt
    count = cnt_ref[b * n_q + qi]
    n_diag = (qi * SEL_Q * TOK) // SEL_TILE

    lane = lax.broadcasted_iota(jnp.int32, (TOK, LANES), 1)
    n_var = qa_ref.shape[0]
    for g, h in units:
        bias_g = bias_ref[0, g, h]
        if nsel % LANES:
            bias_g = jnp.concatenate([bias_g, jnp.zeros((LANES - nsel % LANES, TOK), F32)], axis=0)
        own = (lane // NSA_HEAD_DIM) == g
        for v in range(n_var):
            src = bias_g[(v // 2) * LANES:(v // 2 + 1) * LANES, :].T
            if v % 2 != 1 - g:
                src = pltpu.roll(src, NSA_HEAD_DIM, 1)
            src = src.astype(BF16)
            for r in range(NSA_REP):
                q_r = qp_ref[0, g, h * TOK:(h + 1) * TOK, r * LANES:(r + 1) * LANES]
                qa_ref[v, g, h, r * TOK:(r + 1) * TOK, :] = jnp.where(own, q_r, src)

    def tile_at(i):
        return jnp.where(i < count, list_ref[base + jnp.minimum(i, n_kt - 1)], n_diag)

    def scores(j):
        v = (j * blk_per_tile) // NSA_HEAD_DIM
        for g in range(NSA_KV_GROUPS):
            kb = k_ref[0, g, pl.ds(pl.multiple_of(j * SEL_TILE, SEL_TILE), SEL_TILE), :]
            for h in range(SEL_Q):
                s_ref[g, h] = _dot_nt(kb, qa_ref[v, g, h])

    def values(j):
        for g, h in units:
            acc_ref[g, h] = acc_ref[g, h] * a_ref[g, h] + _dot(vt_ref[0, g, j], p_ref[g, h])

    def softmax(j, causal):
        for g, h in units:
            s_t = s_ref[g, h]
            if causal:
                t_io = ((qi * SEL_Q + h) * TOK
                        + lax.broadcasted_iota(jnp.int32, (SEL_TILE, ncol), 1) % TOK)
                key_pos = j * SEL_TILE + lax.broadcasted_iota(jnp.int32, (SEL_TILE, ncol), 0)
                s_t = jnp.where(key_pos <= t_io, s_t, NEG_INF)
            m_old = m_ref[g, h]
            m_new = jnp.maximum(m_old, jnp.max(s_t, axis=0, keepdims=True))
            alpha = jnp.exp2(m_old - m_new)
            p = jnp.exp2(s_t - jnp.maximum(m_new, MASK_FLOOR))
            m_ref[g, h] = m_new
            p_ref[g, h] = p.astype(BF16)
            a_ref[g, h] = alpha

    scores(tile_at(0))

    def body(i, c):
        values(tile_at(jnp.maximum(i - 1, 0)))
        softmax(tile_at(i), False)
        scores(tile_at(i + 1))
        return c

    lax.fori_loop(0, count, body, 0)
    values(tile_at(jnp.maximum(count - 1, 0)))
    softmax(n_diag, True)
    values(n_diag)
    for h in range(SEL_Q):
        outs = []
        for g in range(NSA_KV_GROUPS):
            acc = acc_ref[g, h]
            gate = _gate_row(gt_ref[0, h], g, 1)
            outs.append(_heads_to_token_major(acc[:NSA_HEAD_DIM] * (gate / acc[NSA_HEAD_DIM:NSA_HEAD_DIM + 1])))
        o_ref[0, h * TOK:(h + 1) * TOK, :] = jnp.concatenate(outs, axis=1)


def _selected(qp, k, vt, bias, gt, tile_list, tile_count, bsz, s):
    nt = s // TOK
    nsel = s // NSA_SEL_BLOCK
    ncol = NSA_REP * TOK
    unit = (NSA_KV_GROUPS, SEL_Q)
    grid_spec = pltpu.PrefetchScalarGridSpec(
        num_scalar_prefetch=2,
        grid=(bsz, nt // SEL_Q),
        in_specs=[
            pl.BlockSpec((1, NSA_KV_GROUPS, SEL_Q * TOK, NSA_REP * LANES), lambda b, i, tl, tc: (b, 0, i, 0)),
            pl.BlockSpec((1, NSA_KV_GROUPS, s, LANES), lambda b, i, tl, tc: (b, 0, 0, 0)),
            pl.BlockSpec((1, NSA_KV_GROUPS, s // SEL_TILE, SEL_V_ROWS, SEL_TILE),
                         lambda b, i, tl, tc: (b, 0, 0, 0, 0)),
            pl.BlockSpec((1, NSA_KV_GROUPS, SEL_Q, nsel, TOK), lambda b, i, tl, tc: (b, 0, i, 0, 0)),
            pl.BlockSpec((1, SEL_Q, GATE_ROWS, TOK), lambda b, i, tl, tc: (b, i, 0, 0)),
        ],
        out_specs=pl.BlockSpec((1, SEL_Q * TOK, NSA_Q_WIDTH), lambda b, i, tl, tc: (b, i, 0)),
        scratch_shapes=[pltpu.VMEM(unit + (1, ncol), F32),
                        pltpu.VMEM(unit + (SEL_V_ROWS, ncol), F32),
                        pltpu.VMEM(unit + (SEL_TILE, ncol), F32),
                        pltpu.VMEM(unit + (SEL_TILE, ncol), BF16),
                        pltpu.VMEM(unit + (1, ncol), F32),
                        pltpu.VMEM((max(nsel // NSA_HEAD_DIM, 1),) + unit + (ncol, LANES), BF16)],
    )
    return pl.pallas_call(
        functools.partial(_selected_kernel, nt=nt, nsel=nsel),
        grid_spec=grid_spec,
        out_shape=jax.ShapeDtypeStruct((bsz, s, NSA_Q_WIDTH), F32),
        compiler_params=_cparams(("parallel", "parallel")),
        name="nsa_selected",
    )(tile_list, tile_count, qp, k, vt, bias, gt)


def _active_tiles(bias, bsz, s):
    n_q = s // (SEL_Q * TOK)
    n_kt = s // SEL_TILE
    rows = SEL_TILE // NSA_SEL_BLOCK * TOK
    act = bias.reshape(bsz, NSA_KV_GROUPS, n_q, SEL_Q, n_kt, rows).max(axis=(1, 3, 5)) > -1.0
    n_diag = (jnp.arange(n_q) * SEL_Q * TOK) // SEL_TILE
    act = act & (jnp.arange(n_kt)[None, None, :] < n_diag[None, :, None])
    order = jnp.argsort(jnp.where(act, 0, 1), axis=-1, stable=True).astype(jnp.int32)
    return order.reshape(-1), jnp.sum(act, axis=-1, dtype=jnp.int32).reshape(-1)


def _ssd_kernel(z_ref, xbc_ref, dt_ref, cw_ref, cb_ref, dtb_ref, alog_ref, dskip_ref, nw_ref, e8_ref,
                tri_ref, y_ref, ext_ref, st_ref):
    @pl.when(pl.program_id(0) == 0)
    def _():
        ext_ref[pl.ds(0, SUBLANES), :] = jnp.zeros((SUBLANES, SSM_CONV_DIM), F32)
        st_ref[...] = jnp.zeros_like(st_ref)

    xb = xbc_ref[...]
    ext_ref[pl.ds(SUBLANES, TOK), :] = xb
    conv = cb_ref[...] + xb * cw_ref[SSM_CONV - 1:SSM_CONV, :]
    for k in range(SSM_CONV - 1):
        conv = conv + ext_ref[pl.ds(SUBLANES - (SSM_CONV - 1) + k, TOK), :] * cw_ref[k:k + 1, :]
    ext_ref[pl.ds(0, SUBLANES), :] = xb[TOK - SUBLANES:TOK, :]
    u = _silu(conv)
    xs = u[:, :SSM_INNER]
    gn = SSM_GROUPS * SSM_STATE
    bm = u[:, SSM_INNER:SSM_INNER + gn]
    cm = u[:, SSM_INNER + gn:]
    dt = jax.nn.softplus(dt_ref[...] + dtb_ref[...])
    da = dt * (-jnp.exp(alog_ref[...]))
    tri = tri_ref[...]
    e8 = e8_ref[...]
    cum = _dot_sel_l(tri, da)
    cum_t = cum.T
    cum_e = _dot_sel_r(cum, e8)
    dt_e = _dot_sel_r(dt, e8)
    xdt = xs * dt_e
    lane = lax.broadcasted_iota(jnp.int32, (TOK, LANES), 1)
    t_io = lax.broadcasted_iota(jnp.int32, (TOK, TOK), 0)
    s_io = lax.broadcasted_iota(jnp.int32, (TOK, TOK), 1)
    causal = s_io <= t_io
    hg = SSM_HEADS // SSM_GROUPS
    y_parts = []
    cbs = []
    for g in range(SSM_GROUPS):
        in_g = (lane // SSM_STATE) == g
        cbs.append(_dot_nt(jnp.where(in_g, cm, 0.0).astype(BF16), bm.astype(BF16)))
    for c in range(SSM_HEADS // 2):
        acc = jnp.zeros((TOK, LANES), F32)
        xpair = xdt[:, c * LANES:(c + 1) * LANES]
        for hh in range(2):
            h = 2 * c + hh
            seg = cum[:, h:h + 1] - cum_t[h:h + 1, :]
            decay = jnp.where(causal, jnp.exp(jnp.where(causal, seg, 0.0)), 0.0)
            mat = (cbs[h // hg] * decay).astype(BF16)
            xh = jnp.where((lane // SSM_HEAD_DIM) == hh, xpair, 0.0).astype(BF16)
            acc = acc + _dot(mat, xh)
        y_parts.append(acc)
    y_diag = jnp.concatenate(y_parts, axis=1)
    prev = st_ref[...]
    y_off = _dot(cm.astype(BF16), prev.astype(BF16)) * jnp.exp(cum_e)
    cum_last = cum_e[TOK - 1:TOK, :]
    xdec = (xdt * jnp.exp(cum_last - cum_e)).astype(BF16)
    new = _dot(bm.T.astype(BF16), xdec)
    row_g = lax.broadcasted_iota(jnp.int32, (LANES, SSM_INNER), 0) // SSM_STATE
    col_g = lax.broadcasted_iota(jnp.int32, (LANES, SSM_INNER), 1) // (hg * SSM_HEAD_DIM)
    st_ref[...] = prev * jnp.exp(cum_last) + jnp.where(row_g == col_g, new, 0.0)
    y = (y_diag + y_off + xs * dskip_ref[...]) * _silu(z_ref[...])
    gw = SSM_INNER // SSM_GROUPS
    outs = []
    for g in range(SSM_GROUPS):
        yg = y[:, g * gw:(g + 1) * gw]
        outs.append(_rms(yg, nw_ref[:, g * gw:(g + 1) * gw]))
    y_ref[...] = jnp.concatenate(outs, axis=1).astype(BF16)


SSD_INPUTS = 11


def _hgrn2_kernel(q_ref, f_ref, i_ref, g_ref, lb_ref, nw_ref, tri_ref, y_ref, st_ref, kp_ref, bp_ref, vp_ref):
    @pl.when(pl.program_id(0) == 0)
    def _():
        st_ref[...] = jnp.zeros_like(st_ref)
        kp_ref[...] = jnp.zeros_like(kp_ref)
        bp_ref[...] = jnp.zeros_like(bp_ref)
        vp_ref[...] = jnp.zeros_like(vp_ref)

    nsub = TOK // HG_SUB
    in_sub = lax.broadcasted_iota(jnp.int32, (TOK, HG_KEY_DIM), 0) % HG_SUB
    tri = tri_ref[...]
    outs = []
    for h in range(HG_HEADS):
        sl = slice(h * HG_KEY_DIM, (h + 1) * HG_KEY_DIM)
        lb = lb_ref[:, sl]
        fr = f_ref[:, sl]
        q = _silu(q_ref[:, sl]) * (HG_KEY_DIM ** -0.5)
        log_f = jnp.log(lb + (1.0 - lb) * jax.nn.sigmoid(fr))
        k = (1.0 - lb) * jax.nn.sigmoid(-fr)
        v = i_ref[:, sl]
        bcum = _dot_sel_l(tri, log_f)
        st = st_ref[h]
        o = _dot_nt((q * jnp.exp(bcum)).astype(BF16), st.astype(BF16))
        ends = [bcum[(j + 1) * HG_SUB - 1:(j + 1) * HG_SUB, :] for j in range(nsub)]
        blk = (HG_SUB, HG_KEY_DIM)
        zero_blk = jnp.zeros(blk, BF16)
        e_prev = jnp.concatenate([jnp.zeros(blk, F32)] + [jnp.broadcast_to(ends[j], blk) for j in range(nsub - 1)],
                                 axis=0)
        e_own = jnp.concatenate([jnp.broadcast_to(ends[j], blk) for j in range(nsub)], axis=0)
        q_loc = (q * jnp.exp(bcum - e_prev)).astype(BF16)
        k_loc = k * jnp.exp(e_own - bcum)
        q_cols = []
        k_cols = []
        for i in range(1, nsub):
            q_cols.append(jnp.concatenate(
                [zero_blk] * i + [q_loc[i * HG_SUB:(i + 1) * HG_SUB]] + [zero_blk] * (nsub - 1 - i), axis=0))
            parts = []
            for j in range(i):
                kj = k_loc[j * HG_SUB:(j + 1) * HG_SUB]
                if j < i - 1:
                    kj = kj * jnp.exp(ends[i - 1] - ends[j])
                parts.append(kj.astype(BF16))
            k_cols.append(jnp.concatenate(parts + [zero_blk] * (nsub - i), axis=0))
        q_aug = jnp.concatenate(q_cols, axis=1)
        k_aug = jnp.concatenate(k_cols, axis=1)
        att = _dot_nt(q_aug, k_aug)
        o = o + _dot(att.astype(BF16), v.astype(BF16))
        kp_ref[pl.ds(HG_SUB, TOK), :] = k
        bp_ref[pl.ds(HG_SUB, TOK), :] = bcum
        vp_ref[pl.ds(HG_SUB, TOK), :] = v
        o = o + jnp.sum(q * k, axis=1, keepdims=True) * v
        for d in range(1, HG_SUB):
            ks = kp_ref[pl.ds(HG_SUB - d, TOK), :]
            bs = bp_ref[pl.ds(HG_SUB - d, TOK), :]
            vs = vp_ref[pl.ds(HG_SUB - d, TOK), :]
            ok = in_sub >= d
            w = jnp.sum(q * ks * jnp.exp(bcum - bs), axis=1, keepdims=True)
            o = o + jnp.where(ok, w * vs, 0.0)
        b_last = bcum[TOK - 1:TOK, :]
        kdec = (k * jnp.exp(b_last - bcum)).astype(BF16)
        st_ref[h] = st * jnp.exp(b_last) + _dot(v.T.astype(BF16), kdec)
        outs.append(_rms(o, nw_ref[...]) * _silu(g_ref[:, sl]))
    y_ref[...] = jnp.concatenate(outs, axis=1).astype(BF16)


HG_INPUTS = 7


SSD_BLOCKED = 3
HG_BLOCKED = 4


def _recurrent_kernel(*refs, bsz):
    n_in = SSD_INPUTS + HG_INPUTS
    ssd_out, hg_out = refs[n_in], refs[n_in + 1]
    scratch = refs[n_in + 2:]
    for b in range(bsz):
        ssd_in = [r.at[b] for r in refs[:SSD_BLOCKED]] + list(refs[SSD_BLOCKED:SSD_INPUTS])
        hg_in = ([r.at[b] for r in refs[SSD_INPUTS:SSD_INPUTS + HG_BLOCKED]]
                 + list(refs[SSD_INPUTS + HG_BLOCKED:n_in]))
        _ssd_kernel(*ssd_in, ssd_out.at[b], *(r.at[b] for r in scratch[:2]))
        _hgrn2_kernel(*hg_in, hg_out.at[b], *(r.at[b] for r in scratch[2:]))


def _recurrent_mixers(proj, cw, cb, dtb, alog, dskip, ssd_nw, e8, lb, hg_nw, tri, bsz, s):
    nt = s // TOK
    proj = proj.reshape(bsz, s, proj.shape[-1])

    def col(c0, w):
        return pl.BlockSpec((bsz, TOK, w), lambda i: (0, i, c0 // w))

    def full(shape):
        return pl.BlockSpec(shape, lambda i: (0,) * len(shape))

    def per_batch(shape):
        return pltpu.VMEM((bsz,) + shape, F32)

    pad = per_batch((TOK + HG_SUB, HG_KEY_DIM))
    ssd_specs = [col(COL_Z, SSM_INNER), col(COL_XBC, SSM_CONV_DIM), col(COL_DT, LANES),
                 full((SSM_CONV, SSM_CONV_DIM)), full((1, SSM_CONV_DIM)), full((1, LANES)), full((1, LANES)),
                 full((1, SSM_INNER)), full((1, SSM_INNER)), full((LANES, SSM_INNER)), full((TOK, TOK))]
    hg_specs = [col(COL_HQ, HG_WIDTH), col(COL_HF, HG_WIDTH), col(COL_HI, HG_WIDTH), col(COL_HG, HG_WIDTH),
                full((1, HG_WIDTH)), full((1, HG_VAL_DIM)), full((TOK, TOK))]
    assert len(ssd_specs) == SSD_INPUTS and len(hg_specs) == HG_INPUTS
    y_b, y_c = pl.pallas_call(
        functools.partial(_recurrent_kernel, bsz=bsz),
        grid=(nt,),
        in_specs=ssd_specs + hg_specs,
        out_specs=[pl.BlockSpec((bsz, TOK, SSM_INNER), lambda i: (0, i, 0)),
                   pl.BlockSpec((bsz, TOK, HG_WIDTH), lambda i: (0, i, 0))],
        out_shape=[jax.ShapeDtypeStruct((bsz, s, SSM_INNER), BF16),
                   jax.ShapeDtypeStruct((bsz, s, HG_WIDTH), BF16)],
        scratch_shapes=[per_batch((TOK + SUBLANES, SSM_CONV_DIM)), per_batch((LANES, SSM_INNER)),
                        per_batch((HG_HEADS, HG_VAL_DIM, HG_KEY_DIM)), pad, pad, pad],
        compiler_params=_cparams(("arbitrary",)),
        name="ssd_hgrn2",
    )(proj, proj, proj, cw, cb, dtb, alog, dskip, ssd_nw, e8, tri, proj, proj, proj, proj, lb, hg_nw, tri)
    return y_b.reshape(bsz * s, SSM_INNER), y_c.reshape(bsz * s, HG_WIDTH)


def _merge_kernel(ga_ref, gb_ref, gc_ref, oc_ref, os_ref, ow_ref, yb_ref, yc_ref, x_ref,
                  wa_ref, wb_ref, wc_ref, wo_ref, o_ref):
    nsa = oc_ref[...] + os_ref[...] + ow_ref[...]
    ya = _dot(nsa.astype(BF16), wa_ref[...])
    yb = _dot(yb_ref[...], wb_ref[...])
    yc = _dot(yc_ref[...], wc_ref[...])
    merged = (jax.nn.sigmoid(ga_ref[...].astype(F32)) * ya + jax.nn.sigmoid(gb_ref[...].astype(F32)) * yb
              + jax.nn.sigmoid(gc_ref[...].astype(F32)) * yc)
    o_ref[...] = x_ref[...] + _dot(merged.astype(BF16), wo_ref[...])


def _merge(gates, oc, os_, ow, yb, yc, x2, wa, wb, wc, wo):
    t = x2.shape[0]
    tm = min(512, t)

    def col(c0, w):
        return pl.BlockSpec((tm, w), lambda i: (i, c0 // w))

    def full(shape):
        return pl.BlockSpec(shape, lambda i: (0,) * len(shape))

    row512 = pl.BlockSpec((tm, 512), lambda i: (i, 0))
    return pl.pallas_call(
        _merge_kernel,
        grid=(t // tm,),
        in_specs=[col(COL_GA, D_MODEL), col(COL_GB, D_MODEL), col(COL_GC, D_MODEL),
                  row512, row512, row512, row512, row512,
                  pl.BlockSpec((tm, D_MODEL), lambda i: (i, 0)),
                  full((NSA_Q_WIDTH, D_MODEL)), full((SSM_INNER, D_MODEL)),
                  full((HG_WIDTH, D_MODEL)), full((D_MODEL, D_MODEL))],
        out_specs=pl.BlockSpec((tm, D_MODEL), lambda i: (i, 0)),
        out_shape=jax.ShapeDtypeStruct((t, D_MODEL), F32),
        compiler_params=_cparams(("parallel",)),
        name="merge",
    )(gates, gates, gates, oc, os_, ow, yb, yc, x2, wa, wb, wc, wo)


def _xa_mem_kernel(mem_ref, nm_ref, wk_ref, wv_ref, kn_ref, k_ref, v_ref):
    m = _rms(mem_ref[0], nm_ref[...]).astype(BF16)
    k = _dot(m, wk_ref[...])
    ks = [_rms(k[:, h * XA_HEAD_DIM:(h + 1) * XA_HEAD_DIM], kn_ref[...]) for h in range(XA_HEADS)]
    k_ref[0] = jnp.concatenate(ks, axis=1).astype(BF16)
    v_ref[0] = _dot(m, wv_ref[...]).astype(BF16)


def _xa_mem(mem, nm, wk, wv, kn):
    bsz, ml, _ = mem.shape

    def full(shape):
        return pl.BlockSpec(shape, lambda b: (0,) * len(shape))

    return pl.pallas_call(
        _xa_mem_kernel,
        grid=(bsz,),
        in_specs=[pl.BlockSpec((1, ml, D_MODEL), lambda b: (b, 0, 0)), full((1, D_MODEL)),
                  full((D_MODEL, XA_WIDTH)), full((D_MODEL, XA_WIDTH)), full((1, XA_HEAD_DIM))],
        out_specs=[pl.BlockSpec((1, ml, XA_WIDTH), lambda b: (b, 0, 0)),
                   pl.BlockSpec((1, ml, XA_WIDTH), lambda b: (b, 0, 0))],
        out_shape=[jax.ShapeDtypeStruct((bsz, ml, XA_WIDTH), BF16),
                   jax.ShapeDtypeStruct((bsz, ml, XA_WIDTH), BF16)],
        compiler_params=_cparams(("parallel",)),
        name="xa_mem",
    )(mem, nm, wk, wv, kn)


def _xa_kernel(x_ref, nx_ref, wq_ref, qn_ref, k_ref, v_ref, wo_ref, o_ref):
    x = x_ref[...]
    h = _rms(x, nx_ref[...]).astype(BF16)
    q = _dot(h, wq_ref[...])
    outs = []
    for hd in range(XA_HEADS):
        sl = slice(hd * XA_HEAD_DIM, (hd + 1) * XA_HEAD_DIM)
        qh = _rms(q[:, sl], qn_ref[...]).astype(BF16)
        sc = _dot_nt(qh, k_ref[0, :, sl]) * (XA_HEAD_DIM ** -0.5)
        sc = sc - jnp.max(sc, axis=-1, keepdims=True)
        p = jnp.exp(sc)
        p = p / jnp.sum(p, axis=-1, keepdims=True)
        outs.append(_dot(p.astype(BF16), v_ref[0, :, sl]))
    o = jnp.concatenate(outs, axis=1).astype(BF16)
    o_ref[...] = x + _dot(o, wo_ref[...])


def _xa(x2, nx, wq, qn, k, v, wo, s):
    t = x2.shape[0]
    tm = min(512, s)
    ml = k.shape[1]
    per_b = s // tm

    def full(shape):
        return pl.BlockSpec(shape, lambda i: (0,) * len(shape))

    return pl.pallas_call(
        _xa_kernel,
        grid=(t // tm,),
        in_specs=[pl.BlockSpec((tm, D_MODEL), lambda i: (i, 0)), full((1, D_MODEL)), full((D_MODEL, XA_WIDTH)),
                  full((1, XA_HEAD_DIM)),
                  pl.BlockSpec((1, ml, XA_WIDTH), lambda i: (i // per_b, 0, 0)),
                  pl.BlockSpec((1, ml, XA_WIDTH), lambda i: (i // per_b, 0, 0)),
                  full((XA_WIDTH, D_MODEL))],
        out_specs=pl.BlockSpec((tm, D_MODEL), lambda i: (i, 0)),
        out_shape=jax.ShapeDtypeStruct((t, D_MODEL), F32),
        compiler_params=_cparams(("parallel",)),
        name="cross_attn",
    )(x2, nx, wq, qn, k, v, wo)


def _ffn_kernel(x_ref, nw_ref, wg_ref, wu_ref, wd_ref, o_ref, h_ref, acc_ref):
    j = pl.program_id(1)

    @pl.when(j == 0)
    def _():
        h_ref[...] = _rms(x_ref[...], nw_ref[...]).astype(BF16)
        acc_ref[...] = jnp.zeros_like(acc_ref)

    h = h_ref[...]
    a = _silu(_dot(h, wg_ref[...])) * _dot(h, wu_ref[...])
    acc_ref[...] += _dot(a.astype(BF16), wd_ref[...])

    @pl.when(j == pl.num_programs(1) - 1)
    def _():
        o_ref[...] = x_ref[...] + acc_ref[...]


def _ffn(x2, nw, wg, wu, wd):
    t = x2.shape[0]
    tm = min(1024, t)
    th = FFN_HIDDEN // 2
    return pl.pallas_call(
        _ffn_kernel,
        grid=(t // tm, FFN_HIDDEN // th),
        in_specs=[pl.BlockSpec((tm, D_MODEL), lambda i, j: (i, 0)),
                  pl.BlockSpec((1, D_MODEL), lambda i, j: (0, 0)),
                  pl.BlockSpec((D_MODEL, th), lambda i, j: (0, j)),
                  pl.BlockSpec((D_MODEL, th), lambda i, j: (0, j)),
                  pl.BlockSpec((th, D_MODEL), lambda i, j: (j, 0))],
        out_specs=pl.BlockSpec((tm, D_MODEL), lambda i, j: (i, 0)),
        out_shape=jax.ShapeDtypeStruct((t, D_MODEL), F32),
        scratch_shapes=[pltpu.VMEM((tm, D_MODEL), BF16), pltpu.VMEM((tm, D_MODEL), F32)],
        compiler_params=_cparams(("parallel", "arbitrary")),
        name="ffn",
    )(x2, nw, wg, wu, wd)


def _pack_w_in(w):
    parts = {}
    off = 0
    names = ("q", "kc", "vc", "ks", "vs", "kw", "vw", "ng", "z", "xbc", "dt", "hq", "hf", "hi", "hg",
             "ga", "gb", "gc")
    for name, size in zip(names, SPLIT_SIZES):
        parts[name] = w[:, off:off + size]
        off += size

    def padded(a, width):
        return jnp.pad(a, ((0, 0), (0, width - a.shape[1])))

    cols = [parts["ga"], parts["gb"], parts["gc"], parts["q"], parts["z"], parts["hq"], parts["hf"],
            parts["hi"], parts["hg"], parts["xbc"], parts["kc"], parts["vc"], parts["ks"], parts["vs"],
            parts["kw"], parts["vw"], padded(parts["ng"], LANES), padded(parts["dt"], LANES)]
    packed = jnp.concatenate(cols, axis=1)
    return padded(packed, PROJ_WIDTH).astype(BF16)


def _rope_tables(pos):
    half = NSA_HEAD_DIM // 2
    inv = 1.0 / (ROPE_THETA ** (jnp.arange(half, dtype=F32) / half))
    ang = pos.astype(F32)[:, None] * inv[None, :]
    reps = LANES // half
    return jnp.tile(jnp.cos(ang), (1, reps)), jnp.tile(jnp.sin(ang), (1, reps))


def _block_diag2(w):
    z = jnp.zeros_like(w)
    return jnp.concatenate([jnp.concatenate([w, z], axis=-1), jnp.concatenate([z, w], axis=-1)], axis=-2)


def kernel(x, mem, norm_mix, w_in, nsa_q_norm, nsa_k_norm, nsa_cmp_pos_k, nsa_cmp_pos_v, nsa_cmp_w1_k, nsa_cmp_w2_k, nsa_cmp_w1_v, nsa_cmp_w2_v, w_nsa_o, ssm_conv_w, ssm_conv_b, ssm_dt_bias, ssm_a_log, ssm_d, ssm_norm, w_ssm_o, hg_lb_logits, hg_norm, w_hg_o, w_out, norm_xa, norm_mem, xa_w_q, xa_w_k, xa_w_v, xa_q_norm, xa_k_norm, xa_w_o, norm_ffn, ffn_w_gate, ffn_w_up, ffn_w_down):
    bsz, s, d = x.shape
    depth = w_in.shape[0]
    assert d == D_MODEL and PREP_TILES * TOK == SEL_TILE
    assert s % SEL_TILE == 0 and s % (SEL_Q * TOK) == 0 and s // TOK >= WIN_TILES
    assert s // NSA_SEL_BLOCK >= NSA_TOPK and (s // NSA_CMP_STRIDE) % (4 * SUBLANES) == 0
    t = bsz * s
    hd = NSA_HEAD_DIM

    cos_t, sin_t = _rope_tables(jnp.arange(s))
    nbp = s // NSA_CMP_STRIDE
    cos_c, sin_c = _rope_tables(jnp.arange(nbp) * NSA_CMP_STRIDE + NSA_CMP_BLOCK - 1)
    tri = jnp.tril(jnp.ones((TOK, TOK), F32)).astype(BF16)
    rows = np.arange(LANES)[:, None]
    cols = np.arange(NSA_Q_WIDTH)[None, :]
    e8 = jnp.asarray(rows == cols // SSM_HEAD_DIM, F32).astype(BF16)

    lb_sm = jax.nn.softmax(hg_lb_logits.astype(F32), axis=0)
    lb_all = jnp.cumsum(lb_sm, axis=0) - lb_sm[0:1]

    def pad_lanes(v):
        return jnp.pad(v, (0, LANES - v.shape[0]))[None, :]

    x2 = x.reshape(t, d)
    for l in range(depth):
        gates, proj = _in_proj(x2, norm_mix[l][None, :], _pack_w_in(w_in[l]))

        qn2 = jnp.tile(nsa_q_norm[l], 2)[None, :]
        kn2 = jnp.tile(nsa_k_norm[l], 2)[None, :]
        qp, ksn, kwn, vst, vwt, gt = _nsa_prep(proj, cos_t, sin_t, qn2, kn2, bsz, s)
        w1k = _block_diag2(nsa_cmp_w1_k[l].reshape(NSA_CMP_BLOCK, hd, hd)).astype(BF16)
        w1v = _block_diag2(nsa_cmp_w1_v[l].reshape(NSA_CMP_BLOCK, hd, hd)).astype(BF16)
        kc, vct = _compress(proj, jnp.tile(nsa_cmp_pos_k[l], (1, 2)), jnp.tile(nsa_cmp_pos_v[l], (1, 2)),
                            w1k, _block_diag2(nsa_cmp_w2_k[l]).astype(BF16),
                            w1v, _block_diag2(nsa_cmp_w2_v[l]).astype(BF16), kn2, cos_c, sin_c, bsz, s)
        o_c, sel_bias = _cmp_topk(qp, kc, vct, gt, bsz, s)
        tile_list, tile_count = _active_tiles(sel_bias, bsz, s)
        o_s = _selected(qp, ksn, vst, sel_bias, gt, tile_list, tile_count, bsz, s)
        o_w = _window(qp, kwn, vwt, proj, bsz, s)

        y_b, y_c = _recurrent_mixers(
            proj, ssm_conv_w[l], ssm_conv_b[l][None, :], pad_lanes(ssm_dt_bias[l]), pad_lanes(ssm_a_log[l]),
            jnp.repeat(ssm_d[l], SSM_HEAD_DIM)[None, :], ssm_norm[l][None, :], e8,
            lb_all[l][None, :], hg_norm[l][None, :], tri, bsz, s)

        x2 = _merge(gates, o_c.reshape(t, -1), o_s.reshape(t, -1), o_w.reshape(t, -1), y_b, y_c, x2,
                    w_nsa_o[l].astype(BF16), w_ssm_o[l].astype(BF16), w_hg_o[l].astype(BF16),
                    w_out[l].astype(BF16))

        k_m, v_m = _xa_mem(mem, norm_mem[l][None, :], xa_w_k[l].astype(BF16), xa_w_v[l].astype(BF16),
                           xa_k_norm[l][None, :])
        x2 = _xa(x2, norm_xa[l][None, :], xa_w_q[l].astype(BF16), xa_q_norm[l][None, :], k_m, v_m,
                 xa_w_o[l].astype(BF16), s)
        x2 = _ffn(x2, norm_ffn[l][None, :], ffn_w_gate[l].astype(BF16), ffn_w_up[l].astype(BF16),
                  ffn_w_down[l].astype(BF16))
    return x2.reshape(bsz, s, d)
```

```python
import functools
import math

import numpy as np
import jax
import jax.numpy as jnp
from jax import lax
from jax.experimental import pallas as pl
from jax.experimental.pallas import tpu as pltpu

F32 = jnp.float32
BF16 = jnp.bfloat16

D_MODEL = 1024
NORM_EPS = 1e-6
ROPE_THETA = 10000.0
NEG_INF = -1e30

NSA_HEADS = 8
NSA_KV_GROUPS = 2
NSA_REP = NSA_HEADS // NSA_KV_GROUPS
NSA_HEAD_DIM = 64
NSA_CMP_BLOCK = 32
NSA_CMP_STRIDE = 16
NSA_SEL_BLOCK = 64
NSA_TOPK = 16
N_FORCED = 3
NSA_WINDOW = 512
NSA_Q_WIDTH = NSA_HEADS * NSA_HEAD_DIM
NSA_KV_WIDTH = NSA_KV_GROUPS * NSA_HEAD_DIM

SSM_INNER = 512
SSM_HEAD_DIM = 64
SSM_HEADS = SSM_INNER // SSM_HEAD_DIM
SSM_GROUPS = 2
SSM_STATE = 64
SSM_CONV = 4
SSM_CONV_DIM = SSM_INNER + 2 * SSM_GROUPS * SSM_STATE

HG_HEADS = 4
HG_KEY_DIM = 128
HG_VAL_DIM = 128
HG_WIDTH = HG_HEADS * HG_KEY_DIM
HG_SUB = 16

XA_HEADS = 4
XA_HEAD_DIM = 128
XA_WIDTH = XA_HEADS * XA_HEAD_DIM

FFN_HIDDEN = -(-(8 * D_MODEL) // (3 * 256)) * 256

SPLIT_SIZES = (
    NSA_Q_WIDTH, NSA_KV_WIDTH, NSA_KV_WIDTH, NSA_KV_WIDTH, NSA_KV_WIDTH, NSA_KV_WIDTH, NSA_KV_WIDTH,
    3 * NSA_HEADS, SSM_INNER, SSM_CONV_DIM, SSM_HEADS, HG_WIDTH, HG_WIDTH, HG_HEADS * HG_VAL_DIM,
    HG_HEADS * HG_VAL_DIM, D_MODEL, D_MODEL, D_MODEL,
)

LANES = 128
SUBLANES = 8
TOK = 128
NSA_BRANCHES = 3
GATE_ROWS = -(-NSA_BRANCHES * NSA_HEADS // SUBLANES) * SUBLANES
PREP_TILES = 4
SEL_Q = 4
SEL_TILE = 512
SEL_V_ROWS = NSA_HEAD_DIM + 16
WIN_TILES = NSA_WINDOW // TOK + 1
LOG2_E = math.log2(math.e)
MASK_FLOOR = -1e20
VMEM_LIMIT = 56 * 1024 * 1024

PROJ_WIDTH = 8192
GATE_WIDTH = 3 * D_MODEL
COL_GA, COL_GB, COL_GC = 0, 1024, 2048
COL_Q = 0
COL_Z = 512
COL_HQ, COL_HF, COL_HI, COL_HG = 1024, 1536, 2048, 2560
COL_XBC = 3072
COL_KC, COL_VC, COL_KS, COL_VS, COL_KW, COL_VW = 3840, 3968, 4096, 4224, 4352, 4480
COL_NG = 4608
COL_DT = 4736


def _cparams(sem):
    return pltpu.CompilerParams(dimension_semantics=sem, vmem_limit_bytes=VMEM_LIMIT)


def _dot(a, b):
    return jnp.dot(a, b, preferred_element_type=F32)


def _dot_nt(a, b):
    return lax.dot_general(a, b, (((1,), (1,)), ((), ())), preferred_element_type=F32)


def _split3(a):
    hi = a.astype(BF16)
    r1 = a - hi.astype(F32)
    mid = r1.astype(BF16)
    lo = (r1 - mid.astype(F32)).astype(BF16)
    return hi, mid, lo


def _dot_sel_r(a, sel):
    hi, mid, lo = _split3(a)
    return _dot(hi, sel) + _dot(mid, sel) + _dot(lo, sel)


def _dot_sel_l(sel, a):
    hi, mid, lo = _split3(a)
    return _dot(sel, hi) + _dot(sel, mid) + _dot(sel, lo)


def _silu(x):
    return x * jax.nn.sigmoid(x)


def _rms(x, w):
    return x * lax.rsqrt(jnp.mean(x * x, axis=-1, keepdims=True) + NORM_EPS) * w


def _inproj_kernel(x_ref, nw_ref, w_ref, og_ref, of_ref, h_ref, *, n_gate):
    j = pl.program_id(1)

    @pl.when(j == 0)
    def _():
        h_ref[...] = _rms(x_ref[...], nw_ref[...]).astype(BF16)

    @pl.when(j < n_gate)
    def _():
        og_ref[...] = _dot(h_ref[...], w_ref[...]).astype(BF16)

    @pl.when(j >= n_gate)
    def _():
        of_ref[...] = _dot(h_ref[...], w_ref[...])


def _in_proj(x2, norm_w, w_packed):
    t = x2.shape[0]
    tm = min(2048, t)
    tn = 1024
    n_gate = GATE_WIDTH // tn
    return pl.pallas_call(
        functools.partial(_inproj_kernel, n_gate=n_gate),
        grid=(t // tm, PROJ_WIDTH // tn),
        in_specs=[
            pl.BlockSpec((tm, D_MODEL), lambda i, j: (i, 0)),
            pl.BlockSpec((1, D_MODEL), lambda i, j: (0, 0)),
            pl.BlockSpec((D_MODEL, tn), lambda i, j: (0, j)),
        ],
        out_specs=[pl.BlockSpec((tm, tn), lambda i, j: (i, jnp.minimum(j, n_gate - 1))),
                   pl.BlockSpec((tm, tn), lambda i, j: (i, jnp.maximum(j - n_gate, 0)))],
        out_shape=[jax.ShapeDtypeStruct((t, GATE_WIDTH), BF16),
                   jax.ShapeDtypeStruct((t, PROJ_WIDTH - GATE_WIDTH), F32)],
        scratch_shapes=[pltpu.VMEM((tm, D_MODEL), BF16)],
        compiler_params=_cparams(("parallel", "arbitrary")),
        name="in_proj",
    )(x2, norm_w, w_packed)


def _norm_rope_pair(x, w, cos, sin):
    lane = lax.broadcasted_iota(jnp.int32, x.shape, 1)
    lo = lane < NSA_HEAD_DIM
    sq = x * x
    s_lo = jnp.sum(jnp.where(lo, sq, 0.0), axis=1, keepdims=True)
    s_hi = jnp.sum(jnp.where(lo, 0.0, sq), axis=1, keepdims=True)
    ms = jnp.where(lo, s_lo, s_hi) * (1.0 / NSA_HEAD_DIM)
    y = x * lax.rsqrt(ms + NORM_EPS) * w
    half = NSA_HEAD_DIM // 2
    first = (lane % NSA_HEAD_DIM) < half
    rot = jnp.where(first, -pltpu.roll(y, LANES - half, 1), pltpu.roll(y, half, 1))
    return y * cos + rot * sin


def _norm_rope_pair_mxu(x, w, cos, sin, head_sum, rot_half):
    ms = _dot_sel_r(x * x, head_sum) * (1.0 / NSA_HEAD_DIM)
    y = x * lax.rsqrt(ms + NORM_EPS) * w
    return y * cos + _dot_sel_r(y, rot_half) * sin


def _nsa_prep_kernel(q_ref, ks_ref, vs_ref, kw_ref, vw_ref, ng_ref, cos_ref, sin_ref, qn_ref, kn_ref,
                     hsum_ref, rot_ref, qp_ref, ksn_ref, kwn_ref, vst_ref, vwt_ref, gt_ref):
    norm_rope = functools.partial(_norm_rope_pair_mxu, head_sum=hsum_ref[...], rot_half=rot_ref[...])
    lane = lax.broadcasted_iota(jnp.int32, (TOK, LANES), 1)
    row = lax.broadcasted_iota(jnp.int32, (TOK, LANES), 0)
    scale = NSA_HEAD_DIM ** -0.5 * LOG2_E
    ones_rows = (lax.broadcasted_iota(jnp.int32, (SEL_V_ROWS - NSA_HEAD_DIM, TOK), 0) == 0).astype(BF16)
    for u in range(PREP_TILES):
        rows = slice(u * TOK, (u + 1) * TOK)
        cos = cos_ref[rows, :]
        sin = sin_ref[rows, :]
        for c in range(NSA_HEADS // 2):
            y = norm_rope(q_ref[rows, c * LANES:(c + 1) * LANES], qn_ref[...], cos, sin) * scale
            y_sw = pltpu.roll(y, NSA_HEAD_DIM, 1)
            for hh in range(2):
                h = 2 * c + hh
                g, r = divmod(h, NSA_REP)
                src = y if hh == g else y_sw
                keep = (lane // NSA_HEAD_DIM) == g
                qp_ref[0, g, rows, r * LANES:(r + 1) * LANES] = jnp.where(keep, src, 0.0).astype(BF16)
        ks_n = norm_rope(ks_ref[rows, :], kn_ref[...], cos, sin)
        tile = pl.program_id(1) * PREP_TILES + u
        blk = (tile * (TOK // NSA_SEL_BLOCK) + row // NSA_SEL_BLOCK) % NSA_HEAD_DIM
        onehot = ((lane % NSA_HEAD_DIM) == blk).astype(F32)
        for g in range(NSA_KV_GROUPS):
            ksn_ref[0, g, rows, :] = jnp.where((lane // NSA_HEAD_DIM) == g, ks_n, onehot).astype(BF16)
        kwn_ref[0, rows, :] = norm_rope(kw_ref[rows, :], kn_ref[...], cos, sin).astype(BF16)
        vs_t = vs_ref[rows, :].T.astype(BF16)
        vw_t = vw_ref[rows, :].T.astype(BF16)
        for g in range(NSA_KV_GROUPS):
            vst_ref[0, g, 0, 0:NSA_HEAD_DIM, rows] = vs_t[g * NSA_HEAD_DIM:(g + 1) * NSA_HEAD_DIM, :]
            vst_ref[0, g, 0, NSA_HEAD_DIM:SEL_V_ROWS, rows] = ones_rows
            vwt_ref[0, g, u] = vw_t[g * NSA_HEAD_DIM:(g + 1) * NSA_HEAD_DIM, :]
        gt_ref[0, u] = jax.nn.sigmoid(ng_ref[rows, :]).T[0:GATE_ROWS, :]


def _nsa_prep(proj, cos_t, sin_t, qn2, kn2, head_sum, rot_half, bsz, s):
    tq = PREP_TILES * TOK
    n_step = s // tq

    def col(c0, w):
        return pl.BlockSpec((tq, w), lambda b, i: (b * n_step + i, c0 // w))

    tab = pl.BlockSpec((tq, LANES), lambda b, i: (i, 0))
    vec = pl.BlockSpec((1, LANES), lambda b, i: (0, 0))
    sq = pl.BlockSpec((LANES, LANES), lambda b, i: (0, 0))
    return pl.pallas_call(
        _nsa_prep_kernel,
        grid=(bsz, n_step),
        in_specs=[col(COL_Q, NSA_Q_WIDTH), col(COL_KS, LANES), col(COL_VS, LANES), col(COL_KW, LANES),
                  col(COL_VW, LANES), col(COL_NG, LANES), tab, tab, vec, vec, sq, sq],
        out_specs=[
            pl.BlockSpec((1, NSA_KV_GROUPS, tq, NSA_REP * LANES), lambda b, i: (b, 0, i, 0)),
            pl.BlockSpec((1, NSA_KV_GROUPS, tq, LANES), lambda b, i: (b, 0, i, 0)),
            pl.BlockSpec((1, tq, LANES), lambda b, i: (b, i, 0)),
            pl.BlockSpec((1, NSA_KV_GROUPS, 1, SEL_V_ROWS, SEL_TILE), lambda b, i: (b, 0, i, 0, 0)),
            pl.BlockSpec((1, NSA_KV_GROUPS, PREP_TILES, NSA_HEAD_DIM, TOK), lambda b, i: (b, 0, i, 0, 0)),
            pl.BlockSpec((1, PREP_TILES, GATE_ROWS, TOK), lambda b, i: (b, i, 0, 0)),
        ],
        out_shape=[
            jax.ShapeDtypeStruct((bsz, NSA_KV_GROUPS, s, NSA_REP * LANES), BF16),
            jax.ShapeDtypeStruct((bsz, NSA_KV_GROUPS, s, LANES), BF16),
            jax.ShapeDtypeStruct((bsz, s, LANES), BF16),
            jax.ShapeDtypeStruct((bsz, NSA_KV_GROUPS, s // SEL_TILE, SEL_V_ROWS, SEL_TILE), BF16),
            jax.ShapeDtypeStruct((bsz, NSA_KV_GROUPS, s // TOK, NSA_HEAD_DIM, TOK), BF16),
            jax.ShapeDtypeStruct((bsz, s // TOK, GATE_ROWS, TOK), F32),
        ],
        compiler_params=_cparams(("parallel", "parallel")),
        name="nsa_prep",
    )(proj, proj, proj, proj, proj, proj, cos_t, sin_t, qn2, kn2, head_sum, rot_half)


def _compress_kernel(kc_ref, vc_ref, posk_ref, posv_ref, w1k_ref, w2k_ref, w1v_ref, w2v_ref, kn_ref,
                     cos_ref, sin_ref, kco_ref, vct_ref, sh_ref, *, nbp):
    half_blk = NSA_CMP_BLOCK // 2

    def compress(x_ref, pos_ref, w1_ref, w2_ref):
        acc_a = jnp.zeros((nbp, LANES), F32)
        acc_b = jnp.zeros((nbp, LANES), F32)
        for l in range(half_blk):
            xl = x_ref[pl.ds(l, nbp, stride=NSA_CMP_STRIDE), :]
            acc_a = acc_a + _dot((xl + pos_ref[l:l + 1, :]).astype(BF16), w1_ref[l])
            acc_b = acc_b + _dot((xl + pos_ref[half_blk + l:half_blk + l + 1, :]).astype(BF16),
                                 w1_ref[half_blk + l])
        sh_ref[pl.ds(0, nbp), :] = acc_b
        sh_ref[pl.ds(nbp, SUBLANES), :] = jnp.zeros((SUBLANES, LANES), F32)
        hid = acc_a + sh_ref[pl.ds(1, nbp), :]
        return _dot(_silu(hid).astype(BF16), w2_ref[...])

    kc = compress(kc_ref, posk_ref, w1k_ref, w2k_ref)
    kco_ref[0] = _norm_rope_pair(kc, kn_ref[...], cos_ref[...], sin_ref[...]).astype(BF16)
    vc = compress(vc_ref, posv_ref, w1v_ref, w2v_ref)
    vct_ref[0] = vc.T.astype(BF16)


def _compress(proj, posk, posv, w1k, w2k, w1v, w2v, kn2, cos_c, sin_c, bsz, s):
    nbp = s // NSA_CMP_STRIDE

    def full(shape):
        return pl.BlockSpec(shape, lambda b: (0,) * len(shape))

    return pl.pallas_call(
        functools.partial(_compress_kernel, nbp=nbp),
        grid=(bsz,),
        in_specs=[
            pl.BlockSpec((s, LANES), lambda b: (b, COL_KC // LANES)),
            pl.BlockSpec((s, LANES), lambda b: (b, COL_VC // LANES)),
            full((NSA_CMP_BLOCK, LANES)), full((NSA_CMP_BLOCK, LANES)),
            full((NSA_CMP_BLOCK, LANES, LANES)), full((LANES, LANES)),
            full((NSA_CMP_BLOCK, LANES, LANES)), full((LANES, LANES)),
            full((1, LANES)), full((nbp, LANES)), full((nbp, LANES)),
        ],
        out_specs=[
            pl.BlockSpec((1, nbp, LANES), lambda b: (b, 0, 0)),
            pl.BlockSpec((1, LANES, nbp), lambda b: (b, 0, 0)),
        ],
        out_shape=[
            jax.ShapeDtypeStruct((bsz, nbp, LANES), BF16),
            jax.ShapeDtypeStruct((bsz, LANES, nbp), BF16),
        ],
        scratch_shapes=[pltpu.VMEM((nbp + SUBLANES, LANES), F32)],
        compiler_params=_cparams(("parallel",)),
        name="nsa_compress",
    )(proj, proj, posk, posv, w1k, w2k, w1v, w2v, kn2, cos_c, sin_c)


def _q_rows(qp):
    return jnp.concatenate([qp[:, r * LANES:(r + 1) * LANES] for r in range(NSA_REP)], axis=0)


def _gate_row(gt, g, branch):
    rows = [(g * NSA_REP + r) * NSA_BRANCHES + branch for r in range(NSA_REP)]
    return jnp.concatenate([gt[i:i + 1, :] for i in rows], axis=1)


def _heads_to_token_major(acc_g):
    outs = []
    for p in range(NSA_REP // 2):
        blk = jnp.concatenate([acc_g[:, (2 * p) * TOK:(2 * p + 1) * TOK],
                               acc_g[:, (2 * p + 1) * TOK:(2 * p + 2) * TOK]], axis=0)
        outs.append(blk.T)
    return jnp.concatenate(outs, axis=1)


def _cmp_topk_kernel(qp_ref, kc_ref, vct_ref, gt_ref, oc_ref, mask_ref, ps_ref, *, nbp, nsel):
    qi = pl.program_id(1)
    ncol = NSA_REP * TOK
    per = NSA_SEL_BLOCK // NSA_CMP_STRIDE

    def attend(nrows):
        nblk = nrows // per
        j_io = lax.broadcasted_iota(jnp.int32, (nblk, TOK), 0)
        t_sel = qi * TOK + lax.broadcasted_iota(jnp.int32, (nblk, TOK), 1)
        cur = t_sel // NSA_SEL_BLOCK
        forced = (j_io == 0) | (j_io == cur) | (j_io == cur - 1)
        valid = j_io * NSA_SEL_BLOCK <= t_sel
        rel = (lax.broadcasted_iota(jnp.int32, (nrows, ncol), 0) * NSA_CMP_STRIDE + (NSA_CMP_BLOCK - 1)
               - lax.broadcasted_iota(jnp.int32, (nrows, ncol), 1) % TOK)
        allowed = rel <= qi * TOK
        kc = kc_ref[0, 0:nrows, :]
        vct = vct_ref[0, :, 0:nrows]
        for g in range(NSA_KV_GROUPS):
            ps_ref[g, pl.ds(0, SUBLANES), :] = jnp.zeros((SUBLANES, TOK), F32)
        scores = [_dot_nt(kc, _q_rows(qp_ref[0, g])) for g in range(NSA_KV_GROUPS)]
        outs = []
        imps = []
        for g in range(NSA_KV_GROUPS):
            s_t = jnp.where(allowed, scores[g], NEG_INF)
            m = jnp.max(s_t, axis=0, keepdims=True)
            p = jnp.exp2(s_t - jnp.maximum(m, MASK_FLOOR))
            l = jnp.sum(p, axis=0, keepdims=True)
            inv = jnp.where(l > 0.0, 1.0 / l, 0.0)
            o_t = _dot(vct, p.astype(BF16)) * inv
            gate = _gate_row(gt_ref[0, 0], g, 0)
            outs.append(_heads_to_token_major(o_t[g * NSA_HEAD_DIM:(g + 1) * NSA_HEAD_DIM, :] * gate))
            pn = p * inv
            psum = pn[:, 0:TOK]
            for r in range(1, NSA_REP):
                psum = psum + pn[:, r * TOK:(r + 1) * TOK]
            ps_ref[g, pl.ds(SUBLANES, nrows), :] = psum
            imp = (ps_ref[g, pl.ds(SUBLANES - 1, nblk, stride=per), :]
                   + ps_ref[g, pl.ds(SUBLANES + per - 1, nblk, stride=per), :])
            for c in range(per - 1):
                imp = imp + 2.0 * ps_ref[g, pl.ds(SUBLANES + c, nblk, stride=per), :]
            imps.append(jnp.where(forced, -jnp.inf, jnp.where(valid, imp, NEG_INF)))
        oc_ref[0] = jnp.concatenate(outs, axis=1)

        def pick(_, carry):
            nxt = []
            for imp_c in carry:
                mx = jnp.max(imp_c, axis=0, keepdims=True)
                idx = jnp.min(jnp.where(imp_c == mx, j_io, nblk), axis=0, keepdims=True)
                nxt.append(jnp.where(j_io == idx, -jnp.inf, imp_c))
            return tuple(nxt)

        picked = lax.fori_loop(0, min(NSA_TOPK, nsel) - N_FORCED, pick, tuple(imps))
        for g in range(NSA_KV_GROUPS):
            mask_ref[0, g, 0, 0:nblk, :] = jnp.where(valid & (picked[g] == -jnp.inf), 0.0, NEG_INF)
            if nblk < nsel:
                mask_ref[0, g, 0, nblk:nsel, :] = jnp.full((nsel - nblk, TOK), NEG_INF, F32)

    n_vis = (qi * TOK + TOK - NSA_CMP_BLOCK) // NSA_CMP_STRIDE + 1
    quarter = nbp // 4
    for k in range(1, 5):
        @pl.when((n_vis > (k - 1) * quarter) & (n_vis <= k * quarter))
        def _(k=k):
            attend(k * quarter)


def _cmp_topk(qp, kc, vct, gt, bsz, s):
    nt = s // TOK
    nbp = s // NSA_CMP_STRIDE
    nsel = s // NSA_SEL_BLOCK
    return pl.pallas_call(
        functools.partial(_cmp_topk_kernel, nbp=nbp, nsel=nsel),
        grid=(bsz, nt),
        in_specs=[
            pl.BlockSpec((1, NSA_KV_GROUPS, TOK, NSA_REP * LANES), lambda b, i: (b, 0, i, 0)),
            pl.BlockSpec((1, nbp, LANES), lambda b, i: (b, 0, 0)),
            pl.BlockSpec((1, LANES, nbp), lambda b, i: (b, 0, 0)),
            pl.BlockSpec((1, 1, GATE_ROWS, TOK), lambda b, i: (b, i, 0, 0)),
        ],
        out_specs=[
            pl.BlockSpec((1, TOK, NSA_Q_WIDTH), lambda b, i: (b, i, 0)),
            pl.BlockSpec((1, NSA_KV_GROUPS, 1, nsel, TOK), lambda b, i: (b, 0, i, 0, 0)),
        ],
        out_shape=[
            jax.ShapeDtypeStruct((bsz, s, NSA_Q_WIDTH), F32),
            jax.ShapeDtypeStruct((bsz, NSA_KV_GROUPS, nt, nsel, TOK), F32),
        ],
        scratch_shapes=[pltpu.VMEM((NSA_KV_GROUPS, nbp + SUBLANES, TOK), F32)],
        compiler_params=_cparams(("parallel", "parallel")),
        name="nsa_cmp_topk",
    )(qp, kc, vct, gt)


def _window_kernel(qp_ref, k_ref, vt_ref, gt_ref, o_ref):
    qi = pl.program_id(1)
    ncol = NSA_REP * TOK
    wk = WIN_TILES * TOK

    def run(first_tile, interior):
        kwin = k_ref[0, pl.ds(pl.multiple_of(first_tile * TOK, TOK), wk), :]
        scores = [_dot_nt(kwin, _q_rows(qp_ref[0, g])) for g in range(NSA_KV_GROUPS)]
        probs = []
        scales = []
        for g in range(NSA_KV_GROUPS):
            s_t = scores[g]
            if interior:
                row = lax.broadcasted_iota(jnp.int32, (TOK, ncol), 0)
                col = lax.broadcasted_iota(jnp.int32, (TOK, ncol), 1) % TOK
                oldest = jnp.where(row > col, s_t[:TOK], NEG_INF)
                newest = jnp.where(row <= col, s_t[wk - TOK:], NEG_INF)
                s_t = jnp.concatenate([oldest, s_t[TOK:wk - TOK], newest], axis=0)
            else:
                key = first_tile * TOK + lax.broadcasted_iota(jnp.int32, (wk, ncol), 0)
                t_io = qi * TOK + lax.broadcasted_iota(jnp.int32, (wk, ncol), 1) % TOK
                s_t = jnp.where((key <= t_io) & (t_io - key < NSA_WINDOW), s_t, NEG_INF)
            m = jnp.max(s_t, axis=0, keepdims=True)
            p = jnp.exp2(s_t - m)
            l = jnp.sum(p, axis=0, keepdims=True)
            probs.append(p.astype(BF16))
            scales.append(_gate_row(gt_ref[0, 0], g, 2) / l)
        outs = []
        for g in range(NSA_KV_GROUPS):
            vt = jnp.concatenate([vt_ref[0, g, first_tile + k] for k in range(WIN_TILES)], axis=1)
            outs.append(_heads_to_token_major(_dot(vt, probs[g]) * scales[g]))
        o_ref[0] = jnp.concatenate(outs, axis=1)

    @pl.when(qi >= WIN_TILES - 1)
    def _():
        run(qi - (WIN_TILES - 1), True)

    @pl.when(qi < WIN_TILES - 1)
    def _():
        run(0, False)


def _window(qp, k, vt, gt, bsz, s):
    nt = s // TOK
    return pl.pallas_call(
        _window_kernel,
        grid=(bsz, nt),
        in_specs=[
            pl.BlockSpec((1, NSA_KV_GROUPS, TOK, NSA_REP * LANES), lambda b, i: (b, 0, i, 0)),
            pl.BlockSpec((1, s, LANES), lambda b, i: (b, 0, 0)),
            pl.BlockSpec((1, NSA_KV_GROUPS, nt, NSA_HEAD_DIM, TOK), lambda b, i: (b, 0, 0, 0, 0)),
            pl.BlockSpec((1, 1, GATE_ROWS, TOK), lambda b, i: (b, i, 0, 0)),
        ],
        out_specs=pl.BlockSpec((1, TOK, NSA_Q_WIDTH), lambda b, i: (b, i, 0)),
        out_shape=jax.ShapeDtypeStruct((bsz, s, NSA_Q_WIDTH), F32),
        compiler_params=_cparams(("parallel", "parallel")),
        name="nsa_window",
    )(qp, k, vt, gt)


def _selected_kernel(list_ref, cnt_ref, qp_ref, k_ref, vt_ref, bias_ref, gt_ref, o_ref, m_ref, acc_ref,
                     s_ref, p_ref, a_ref, qa_ref, *, nt, nsel):
    b = pl.program_id(0)
    qi = pl.program_id(1)
    ncol = NSA_REP * TOK
    blk_per_tile = SEL_TILE // NSA_SEL_BLOCK
    n_kt = nt // (SEL_TILE // TOK)
    n_q = nt // SEL_Q
    units = [(g, h) for g in range(NSA_KV_GROUPS) for h in range(SEL_Q)]
    m_ref[...] = jnp.full(m_ref.shape, NEG_INF, F32)
    acc_ref[...] = jnp.zeros(acc_ref.shape, F32)
    p_ref[...] = jnp.zeros(p_ref.shape, BF16)
    a_ref[...] = jnp.ones(a_ref.shape, F32)
    base = (b * n_q + qi) * n_kt
    count = cnt_ref[b * n_q + qi]
    n_diag = (qi * SEL_Q * TOK) // SEL_TILE

    lane = lax.broadcasted_iota(jnp.int32, (TOK, LANES), 1)
    n_var = qa_ref.shape[0]
    for g, h in units:
        bias_g = bias_ref[0, g, h]
        if nsel % LANES:
            bias_g = jnp.concatenate([bias_g, jnp.zeros((LANES - nsel % LANES, TOK), F32)], axis=0)
        own = (lane // NSA_HEAD_DIM) == g
        for v in range(n_var):
            src = bias_g[(v // 2) * LANES:(v // 2 + 1) * LANES, :].T
            if v % 2 != 1 - g:
                src = pltpu.roll(src, NSA_HEAD_DIM, 1)
            src = src.astype(BF16)
            for r in range(NSA_REP):
                q_r = qp_ref[0, g, h * TOK:(h + 1) * TOK, r * LANES:(r + 1) * LANES]
                qa_ref[v, g, h, r * TOK:(r + 1) * TOK, :] = jnp.where(own, q_r, src)

    def tile_at(i):
        return jnp.where(i < count, list_ref[base + jnp.minimum(i, n_kt - 1)], n_diag)

    def scores(j):
        v = (j * blk_per_tile) // NSA_HEAD_DIM
        for g in range(NSA_KV_GROUPS):
            kb = k_ref[0, g, pl.ds(pl.multiple_of(j * SEL_TILE, SEL_TILE), SEL_TILE), :]
            for h in range(SEL_Q):
                s_ref[g, h] = _dot_nt(kb, qa_ref[v, g, h])

    def values(j):
        for g, h in units:
            acc_ref[g, h] = acc_ref[g, h] * a_ref[g, h] + _dot(vt_ref[0, g, j], p_ref[g, h])

    def softmax(j, causal):
        for g, h in units:
            s_t = s_ref[g, h]
            if causal:
                t_io = ((qi * SEL_Q + h) * TOK
                        + lax.broadcasted_iota(jnp.int32, (SEL_TILE, ncol), 1) % TOK)
                key_pos = j * SEL_TILE + lax.broadcasted_iota(jnp.int32, (SEL_TILE, ncol), 0)
                s_t = jnp.where(key_pos <= t_io, s_t, NEG_INF)
            m_old = m_ref[g, h]
            m_new = jnp.maximum(m_old, jnp.max(s_t, axis=0, keepdims=True))
            alpha = jnp.exp2(m_old - m_new)
            p = jnp.exp2(s_t - jnp.maximum(m_new, MASK_FLOOR))
            m_ref[g, h] = m_new
            p_ref[g, h] = p.astype(BF16)
            a_ref[g, h] = alpha

    scores(tile_at(0))

    def body(i, c):
        values(tile_at(jnp.maximum(i - 1, 0)))
        softmax(tile_at(i), False)
        scores(tile_at(i + 1))
        return c

    lax.fori_loop(0, count, body, 0)
    values(tile_at(jnp.maximum(count - 1, 0)))
    softmax(n_diag, True)
    values(n_diag)
    for h in range(SEL_Q):
        outs = []
        for g in range(NSA_KV_GROUPS):
            acc = acc_ref[g, h]
            gate = _gate_row(gt_ref[0, h], g, 1)
            outs.append(_heads_to_token_major(acc[:NSA_HEAD_DIM] * (gate / acc[NSA_HEAD_DIM:NSA_HEAD_DIM + 1])))
        o_ref[0, h * TOK:(h + 1) * TOK, :] = jnp.concatenate(outs, axis=1)


def _selected(qp, k, vt, bias, gt, tile_list, tile_count, bsz, s):
    nt = s // TOK
    nsel = s // NSA_SEL_BLOCK
    ncol = NSA_REP * TOK
    unit = (NSA_KV_GROUPS, SEL_Q)
    grid_spec = pltpu.PrefetchScalarGridSpec(
        num_scalar_prefetch=2,
        grid=(bsz, nt // SEL_Q),
        in_specs=[
            pl.BlockSpec((1, NSA_KV_GROUPS, SEL_Q * TOK, NSA_REP * LANES), lambda b, i, tl, tc: (b, 0, i, 0)),
            pl.BlockSpec((1, NSA_KV_GROUPS, s, LANES), lambda b, i, tl, tc: (b, 0, 0, 0)),
            pl.BlockSpec((1, NSA_KV_GROUPS, s // SEL_TILE, SEL_V_ROWS, SEL_TILE),
                         lambda b, i, tl, tc: (b, 0, 0, 0, 0)),
            pl.BlockSpec((1, NSA_KV_GROUPS, SEL_Q, nsel, TOK), lambda b, i, tl, tc: (b, 0, i, 0, 0)),
            pl.BlockSpec((1, SEL_Q, GATE_ROWS, TOK), lambda b, i, tl, tc: (b, i, 0, 0)),
        ],
        out_specs=pl.BlockSpec((1, SEL_Q * TOK, NSA_Q_WIDTH), lambda b, i, tl, tc: (b, i, 0)),
        scratch_shapes=[pltpu.VMEM(unit + (1, ncol), F32),
                        pltpu.VMEM(unit + (SEL_V_ROWS, ncol), F32),
                        pltpu.VMEM(unit + (SEL_TILE, ncol), F32),
                        pltpu.VMEM(unit + (SEL_TILE, ncol), BF16),
                        pltpu.VMEM(unit + (1, ncol), F32),
                        pltpu.VMEM((max(nsel // NSA_HEAD_DIM, 1),) + unit + (ncol, LANES), BF16)],
    )
    return pl.pallas_call(
        functools.partial(_selected_kernel, nt=nt, nsel=nsel),
        grid_spec=grid_spec,
        out_shape=jax.ShapeDtypeStruct((bsz, s, NSA_Q_WIDTH), F32),
        compiler_params=_cparams(("parallel", "parallel")),
        name="nsa_selected",
    )(tile_list, tile_count, qp, k, vt, bias, gt)


def _active_tiles(bias, bsz, s):
    n_q = s // (SEL_Q * TOK)
    n_kt = s // SEL_TILE
    rows = SEL_TILE // NSA_SEL_BLOCK * TOK
    act = bias.reshape(bsz, NSA_KV_GROUPS, n_q, SEL_Q, n_kt, rows).max(axis=(1, 3, 5)) > -1.0
    n_diag = (jnp.arange(n_q) * SEL_Q * TOK) // SEL_TILE
    act = act & (jnp.arange(n_kt)[None, None, :] < n_diag[None, :, None])
    order = jnp.argsort(jnp.where(act, 0, 1), axis=-1, stable=True).astype(jnp.int32)
    return order.reshape(-1), jnp.sum(act, axis=-1, dtype=jnp.int32).reshape(-1)


def _ssd_kernel(z_ref, xbc_ref, dt_ref, cw_ref, cb_ref, dtb_ref, alog_ref, dskip_ref, nw_ref, e8_ref,
                tri_ref, y_ref, ext_ref, st_ref):
    @pl.when(pl.program_id(0) == 0)
    def _():
        ext_ref[pl.ds(0, SUBLANES), :] = jnp.zeros((SUBLANES, SSM_CONV_DIM), F32)
        st_ref[...] = jnp.zeros_like(st_ref)

    xb = xbc_ref[...]
    ext_ref[pl.ds(SUBLANES, TOK), :] = xb
    conv = cb_ref[...] + xb * cw_ref[SSM_CONV - 1:SSM_CONV, :]
    for k in range(SSM_CONV - 1):
        conv = conv + ext_ref[pl.ds(SUBLANES - (SSM_CONV - 1) + k, TOK), :] * cw_ref[k:k + 1, :]
    ext_ref[pl.ds(0, SUBLANES), :] = xb[TOK - SUBLANES:TOK, :]
    u = _silu(conv)
    xs = u[:, :SSM_INNER]
    gn = SSM_GROUPS * SSM_STATE
    bm = u[:, SSM_INNER:SSM_INNER + gn]
    cm = u[:, SSM_INNER + gn:]
    dt = jax.nn.softplus(dt_ref[...] + dtb_ref[...])
    da = dt * (-jnp.exp(alog_ref[...]))
    tri = tri_ref[...]
    e8 = e8_ref[...]
    cum = _dot_sel_l(tri, da)
    cum_t = cum.T
    cum_e = _dot_sel_r(cum, e8)
    dt_e = _dot_sel_r(dt, e8)
    xdt = xs * dt_e
    lane = lax.broadcasted_iota(jnp.int32, (TOK, LANES), 1)
    t_io = lax.broadcasted_iota(jnp.int32, (TOK, TOK), 0)
    s_io = lax.broadcasted_iota(jnp.int32, (TOK, TOK), 1)
    causal = s_io <= t_io
    hg = SSM_HEADS // SSM_GROUPS
    y_parts = []
    cbs = []
    for g in range(SSM_GROUPS):
        in_g = (lane // SSM_STATE) == g
        cbs.append(_dot_nt(jnp.where(in_g, cm, 0.0).astype(BF16), bm.astype(BF16)))
    for c in range(SSM_HEADS // 2):
        acc = jnp.zeros((TOK, LANES), F32)
        xpair = xdt[:, c * LANES:(c + 1) * LANES]
        for hh in range(2):
            h = 2 * c + hh
            seg = cum[:, h:h + 1] - cum_t[h:h + 1, :]
            decay = jnp.where(causal, jnp.exp(jnp.where(causal, seg, 0.0)), 0.0)
            mat = (cbs[h // hg] * decay).astype(BF16)
            xh = jnp.where((lane // SSM_HEAD_DIM) == hh, xpair, 0.0).astype(BF16)
            acc = acc + _dot(mat, xh)
        y_parts.append(acc)
    y_diag = jnp.concatenate(y_parts, axis=1)
    prev = st_ref[...]
    y_off = _dot(cm.astype(BF16), prev.astype(BF16)) * jnp.exp(cum_e)
    cum_last = cum_e[TOK - 1:TOK, :]
    xdec = (xdt * jnp.exp(cum_last - cum_e)).astype(BF16)
    new = _dot(bm.T.astype(BF16), xdec)
    row_g = lax.broadcasted_iota(jnp.int32, (LANES, SSM_INNER), 0) // SSM_STATE
    col_g = lax.broadcasted_iota(jnp.int32, (LANES, SSM_INNER), 1) // (hg * SSM_HEAD_DIM)
    st_ref[...] = prev * jnp.exp(cum_last) + jnp.where(row_g == col_g, new, 0.0)
    y = (y_diag + y_off + xs * dskip_ref[...]) * _silu(z_ref[...])
    gw = SSM_INNER // SSM_GROUPS
    outs = []
    for g in range(SSM_GROUPS):
        yg = y[:, g * gw:(g + 1) * gw]
        outs.append(_rms(yg, nw_ref[:, g * gw:(g + 1) * gw]))
    y_ref[...] = jnp.concatenate(outs, axis=1).astype(BF16)


SSD_INPUTS = 11


def _hgrn2_kernel(q_ref, f_ref, i_ref, g_ref, lb_ref, nw_ref, tri_ref, y_ref, st_ref, kp_ref, bp_ref, vp_ref):
    @pl.when(pl.program_id(0) == 0)
    def _():
        st_ref[...] = jnp.zeros_like(st_ref)
        kp_ref[...] = jnp.zeros_like(kp_ref)
        bp_ref[...] = jnp.zeros_like(bp_ref)
        vp_ref[...] = jnp.zeros_like(vp_ref)

    nsub = TOK // HG_SUB
    in_sub = lax.broadcasted_iota(jnp.int32, (TOK, HG_KEY_DIM), 0) % HG_SUB
    tri = tri_ref[...]
    outs = []
    for h in range(HG_HEADS):
        sl = slice(h * HG_KEY_DIM, (h + 1) * HG_KEY_DIM)
        lb = lb_ref[:, sl]
        fr = f_ref[:, sl]
        q = _silu(q_ref[:, sl]) * (HG_KEY_DIM ** -0.5)
        log_f = jnp.log(lb + (1.0 - lb) * jax.nn.sigmoid(fr))
        k = (1.0 - lb) * jax.nn.sigmoid(-fr)
        v = i_ref[:, sl]
        bcum = _dot_sel_l(tri, log_f)
        st = st_ref[h]
        o = _dot_nt((q * jnp.exp(bcum)).astype(BF16), st.astype(BF16))
        ends = [bcum[(j + 1) * HG_SUB - 1:(j + 1) * HG_SUB, :] for j in range(nsub)]
        blk = (HG_SUB, HG_KEY_DIM)
        zero_blk = jnp.zeros(blk, BF16)
        e_prev = jnp.concatenate([jnp.zeros(blk, F32)] + [jnp.broadcast_to(ends[j], blk) for j in range(nsub - 1)],
                                 axis=0)
        e_own = jnp.concatenate([jnp.broadcast_to(ends[j], blk) for j in range(nsub)], axis=0)
        q_loc = (q * jnp.exp(bcum - e_prev)).astype(BF16)
        k_loc = k * jnp.exp(e_own - bcum)
        q_cols = []
        k_cols = []
        for i in range(1, nsub):
            q_cols.append(jnp.concatenate(
                [zero_blk] * i + [q_loc[i * HG_SUB:(i + 1) * HG_SUB]] + [zero_blk] * (nsub - 1 - i), axis=0))
            parts = []
            for j in range(i):
                kj = k_loc[j * HG_SUB:(j + 1) * HG_SUB]
                if j < i - 1:
                    kj = kj * jnp.exp(ends[i - 1] - ends[j])
                parts.append(kj.astype(BF16))
            k_cols.append(jnp.concatenate(parts + [zero_blk] * (nsub - i), axis=0))
        q_aug = jnp.concatenate(q_cols, axis=1)
        k_aug = jnp.concatenate(k_cols, axis=1)
        att = _dot_nt(q_aug, k_aug)
        o = o + _dot(att.astype(BF16), v.astype(BF16))
        kp_ref[pl.ds(HG_SUB, TOK), :] = k
        bp_ref[pl.ds(HG_SUB, TOK), :] = bcum
        vp_ref[pl.ds(HG_SUB, TOK), :] = v
        o = o + jnp.sum(q * k, axis=1, keepdims=True) * v
        for d in range(1, HG_SUB):
            ks = kp_ref[pl.ds(HG_SUB - d, TOK), :]
            bs = bp_ref[pl.ds(HG_SUB - d, TOK), :]
            vs = vp_ref[pl.ds(HG_SUB - d, TOK), :]
            ok = in_sub >= d
            w = jnp.sum(q * ks * jnp.exp(bcum - bs), axis=1, keepdims=True)
            o = o + jnp.where(ok, w * vs, 0.0)
        b_last = bcum[TOK - 1:TOK, :]
        kdec = (k * jnp.exp(b_last - bcum)).astype(BF16)
        st_ref[h] = st * jnp.exp(b_last) + _dot(v.T.astype(BF16), kdec)
        outs.append(_rms(o, nw_ref[...]) * _silu(g_ref[:, sl]))
    y_ref[...] = jnp.concatenate(outs, axis=1).astype(BF16)


HG_INPUTS = 7


SSD_BLOCKED = 3
HG_BLOCKED = 4


def _recurrent_kernel(*refs, bsz):
    n_in = SSD_INPUTS + HG_INPUTS
    ssd_out, hg_out = refs[n_in], refs[n_in + 1]
    scratch = refs[n_in + 2:]
    for b in range(bsz):
        ssd_in = [r.at[b] for r in refs[:SSD_BLOCKED]] + list(refs[SSD_BLOCKED:SSD_INPUTS])
        hg_in = ([r.at[b] for r in refs[SSD_INPUTS:SSD_INPUTS + HG_BLOCKED]]
                 + list(refs[SSD_INPUTS + HG_BLOCKED:n_in]))
        _ssd_kernel(*ssd_in, ssd_out.at[b], *(r.at[b] for r in scratch[:2]))
        _hgrn2_kernel(*hg_in, hg_out.at[b], *(r.at[b] for r in scratch[2:]))


def _recurrent_mixers(proj, cw, cb, dtb, alog, dskip, ssd_nw, e8, lb, hg_nw, tri, bsz, s):
    nt = s // TOK
    proj = proj.reshape(bsz, s, proj.shape[-1])

    def col(c0, w):
        return pl.BlockSpec((bsz, TOK, w), lambda i: (0, i, c0 // w))

    def full(shape):
        return pl.BlockSpec(shape, lambda i: (0,) * len(shape))

    def per_batch(shape):
        return pltpu.VMEM((bsz,) + shape, F32)

    pad = per_batch((TOK + HG_SUB, HG_KEY_DIM))
    ssd_specs = [col(COL_Z, SSM_INNER), col(COL_XBC, SSM_CONV_DIM), col(COL_DT, LANES),
                 full((SSM_CONV, SSM_CONV_DIM)), full((1, SSM_CONV_DIM)), full((1, LANES)), full((1, LANES)),
                 full((1, SSM_INNER)), full((1, SSM_INNER)), full((LANES, SSM_INNER)), full((TOK, TOK))]
    hg_specs = [col(COL_HQ, HG_WIDTH), col(COL_HF, HG_WIDTH), col(COL_HI, HG_WIDTH), col(COL_HG, HG_WIDTH),
                full((1, HG_WIDTH)), full((1, HG_VAL_DIM)), full((TOK, TOK))]
    assert len(ssd_specs) == SSD_INPUTS and len(hg_specs) == HG_INPUTS
    y_b, y_c = pl.pallas_call(
        functools.partial(_recurrent_kernel, bsz=bsz),
        grid=(nt,),
        in_specs=ssd_specs + hg_specs,
        out_specs=[pl.BlockSpec((bsz, TOK, SSM_INNER), lambda i: (0, i, 0)),
                   pl.BlockSpec((bsz, TOK, HG_WIDTH), lambda i: (0, i, 0))],
        out_shape=[jax.ShapeDtypeStruct((bsz, s, SSM_INNER), BF16),
                   jax.ShapeDtypeStruct((bsz, s, HG_WIDTH), BF16)],
        scratch_shapes=[per_batch((TOK + SUBLANES, SSM_CONV_DIM)), per_batch((LANES, SSM_INNER)),
                        per_batch((HG_HEADS, HG_VAL_DIM, HG_KEY_DIM)), pad, pad, pad],
        compiler_params=_cparams(("arbitrary",)),
        name="ssd_hgrn2",
    )(proj, proj, proj, cw, cb, dtb, alog, dskip, ssd_nw, e8, tri, proj, proj, proj, proj, lb, hg_nw, tri)
    return y_b.reshape(bsz * s, SSM_INNER), y_c.reshape(bsz * s, HG_WIDTH)


def _merge_kernel(ga_ref, gb_ref, gc_ref, oc_ref, os_ref, ow_ref, yb_ref, yc_ref, x_ref,
                  wa_ref, wb_ref, wc_ref, wo_ref, o_ref):
    nsa = oc_ref[...] + os_ref[...] + ow_ref[...]
    ya = _dot(nsa.astype(BF16), wa_ref[...])
    yb = _dot(yb_ref[...], wb_ref[...])
    yc = _dot(yc_ref[...], wc_ref[...])
    merged = (jax.nn.sigmoid(ga_ref[...].astype(F32)) * ya + jax.nn.sigmoid(gb_ref[...].astype(F32)) * yb
              + jax.nn.sigmoid(gc_ref[...].astype(F32)) * yc)
    o_ref[...] = x_ref[...] + _dot(merged.astype(BF16), wo_ref[...])


def _merge(gates, oc, os_, ow, yb, yc, x2, wa, wb, wc, wo):
    t = x2.shape[0]
    tm = min(512, t)

    def col(c0, w):
        return pl.BlockSpec((tm, w), lambda i: (i, c0 // w))

    def full(shape):
        return pl.BlockSpec(shape, lambda i: (0,) * len(shape))

    row512 = pl.BlockSpec((tm, 512), lambda i: (i, 0))
    return pl.pallas_call(
        _merge_kernel,
        grid=(t // tm,),
        in_specs=[col(COL_GA, D_MODEL), col(COL_GB, D_MODEL), col(COL_GC, D_MODEL),
                  row512, row512, row512, row512, row512,
                  pl.BlockSpec((tm, D_MODEL), lambda i: (i, 0)),
                  full((NSA_Q_WIDTH, D_MODEL)), full((SSM_INNER, D_MODEL)),
                  full((HG_WIDTH, D_MODEL)), full((D_MODEL, D_MODEL))],
        out_specs=pl.BlockSpec((tm, D_MODEL), lambda i: (i, 0)),
        out_shape=jax.ShapeDtypeStruct((t, D_MODEL), F32),
        compiler_params=_cparams(("parallel",)),
        name="merge",
    )(gates, gates, gates, oc, os_, ow, yb, yc, x2, wa, wb, wc, wo)


def _xa_mem_kernel(mem_ref, nm_ref, wk_ref, wv_ref, kn_ref, k_ref, v_ref):
    m = _rms(mem_ref[0], nm_ref[...]).astype(BF16)
    k = _dot(m, wk_ref[...])
    ks = [_rms(k[:, h * XA_HEAD_DIM:(h + 1) * XA_HEAD_DIM], kn_ref[...]) for h in range(XA_HEADS)]
    k_ref[0] = jnp.concatenate(ks, axis=1).astype(BF16)
    v_ref[0] = _dot(m, wv_ref[...]).astype(BF16)


def _xa_mem(mem, nm, wk, wv, kn):
    bsz, ml, _ = mem.shape

    def full(shape):
        return pl.BlockSpec(shape, lambda b: (0,) * len(shape))

    return pl.pallas_call(
        _xa_mem_kernel,
        grid=(bsz,),
        in_specs=[pl.BlockSpec((1, ml, D_MODEL), lambda b: (b, 0, 0)), full((1, D_MODEL)),
                  full((D_MODEL, XA_WIDTH)), full((D_MODEL, XA_WIDTH)), full((1, XA_HEAD_DIM))],
        out_specs=[pl.BlockSpec((1, ml, XA_WIDTH), lambda b: (b, 0, 0)),
                   pl.BlockSpec((1, ml, XA_WIDTH), lambda b: (b, 0, 0))],
        out_shape=[jax.ShapeDtypeStruct((bsz, ml, XA_WIDTH), BF16),
                   jax.ShapeDtypeStruct((bsz, ml, XA_WIDTH), BF16)],
        compiler_params=_cparams(("parallel",)),
        name="xa_mem",
    )(mem, nm, wk, wv, kn)


def _xa_kernel(x_ref, nx_ref, wq_ref, qn_ref, k_ref, v_ref, wo_ref, o_ref):
    x = x_ref[...]
    h = _rms(x, nx_ref[...]).astype(BF16)
    q = _dot(h, wq_ref[...])
    outs = []
    for hd in range(XA_HEADS):
        sl = slice(hd * XA_HEAD_DIM, (hd + 1) * XA_HEAD_DIM)
        qh = _rms(q[:, sl], qn_ref[...]).astype(BF16)
        sc = _dot_nt(qh, k_ref[0, :, sl]) * (XA_HEAD_DIM ** -0.5)
        sc = sc - jnp.max(sc, axis=-1, keepdims=True)
        p = jnp.exp(sc)
        p = p / jnp.sum(p, axis=-1, keepdims=True)
        outs.append(_dot(p.astype(BF16), v_ref[0, :, sl]))
    o = jnp.concatenate(outs, axis=1).astype(BF16)
    o_ref[...] = x + _dot(o, wo_ref[...])


def _xa(x2, nx, wq, qn, k, v, wo, s):
    t = x2.shape[0]
    tm = min(512, s)
    ml = k.shape[1]
    per_b = s // tm

    def full(shape):
        return pl.BlockSpec(shape, lambda i: (0,) * len(shape))

    return pl.pallas_call(
        _xa_kernel,
        grid=(t // tm,),
        in_specs=[pl.BlockSpec((tm, D_MODEL), lambda i: (i, 0)), full((1, D_MODEL)), full((D_MODEL, XA_WIDTH)),
                  full((1, XA_HEAD_DIM)),
                  pl.BlockSpec((1, ml, XA_WIDTH), lambda i: (i // per_b, 0, 0)),
                  pl.BlockSpec((1, ml, XA_WIDTH), lambda i: (i // per_b, 0, 0)),
                  full((XA_WIDTH, D_MODEL))],
        out_specs=pl.BlockSpec((tm, D_MODEL), lambda i: (i, 0)),
        out_shape=jax.ShapeDtypeStruct((t, D_MODEL), F32),
        compiler_params=_cparams(("parallel",)),
        name="cross_attn",
    )(x2, nx, wq, qn, k, v, wo)


def _ffn_kernel(x_ref, nw_ref, wg_ref, wu_ref, wd_ref, o_ref, h_ref, acc_ref):
    j = pl.program_id(1)

    @pl.when(j == 0)
    def _():
        h_ref[...] = _rms(x_ref[...], nw_ref[...]).astype(BF16)
        acc_ref[...] = jnp.zeros_like(acc_ref)

    h = h_ref[...]
    a = _silu(_dot(h, wg_ref[...])) * _dot(h, wu_ref[...])
    acc_ref[...] += _dot(a.astype(BF16), wd_ref[...])

    @pl.when(j == pl.num_programs(1) - 1)
    def _():
        o_ref[...] = x_ref[...] + acc_ref[...]


def _ffn(x2, nw, wg, wu, wd):
    t = x2.shape[0]
    tm = min(1024, t)
    th = FFN_HIDDEN // 2
    return pl.pallas_call(
        _ffn_kernel,
        grid=(t // tm, FFN_HIDDEN // th),
        in_specs=[pl.BlockSpec((tm, D_MODEL), lambda i, j: (i, 0)),
                  pl.BlockSpec((1, D_MODEL), lambda i, j: (0, 0)),
                  pl.BlockSpec((D_MODEL, th), lambda i, j: (0, j)),
                  pl.BlockSpec((D_MODEL, th), lambda i, j: (0, j)),
                  pl.BlockSpec((th, D_MODEL), lambda i, j: (j, 0))],
        out_specs=pl.BlockSpec((tm, D_MODEL), lambda i, j: (i, 0)),
        out_shape=jax.ShapeDtypeStruct((t, D_MODEL), F32),
        scratch_shapes=[pltpu.VMEM((tm, D_MODEL), BF16), pltpu.VMEM((tm, D_MODEL), F32)],
        compiler_params=_cparams(("parallel", "arbitrary")),
        name="ffn",
    )(x2, nw, wg, wu, wd)


def _pack_w_in(w):
    parts = {}
    off = 0
    names = ("q", "kc", "vc", "ks", "vs", "kw", "vw", "ng", "z", "xbc", "dt", "hq", "hf", "hi", "hg",
             "ga", "gb", "gc")
    for name, size in zip(names, SPLIT_SIZES):
        parts[name] = w[:, off:off + size]
        off += size

    def padded(a, width):
        return jnp.pad(a, ((0, 0), (0, width - a.shape[1])))

    cols = [parts["ga"], parts["gb"], parts["gc"], parts["q"], parts["z"], parts["hq"], parts["hf"],
            parts["hi"], parts["hg"], parts["xbc"], parts["kc"], parts["vc"], parts["ks"], parts["vs"],
            parts["kw"], parts["vw"], padded(parts["ng"], LANES), padded(parts["dt"], LANES)]
    packed = jnp.concatenate(cols, axis=1)
    return padded(packed, PROJ_WIDTH).astype(BF16)


def _rope_tables(pos):
    half = NSA_HEAD_DIM // 2
    inv = 1.0 / (ROPE_THETA ** (jnp.arange(half, dtype=F32) / half))
    ang = pos.astype(F32)[:, None] * inv[None, :]
    reps = LANES // half
    return jnp.tile(jnp.cos(ang), (1, reps)), jnp.tile(jnp.sin(ang), (1, reps))


def _block_diag2(w):
    z = jnp.zeros_like(w)
    return jnp.concatenate([jnp.concatenate([w, z], axis=-1), jnp.concatenate([z, w], axis=-1)], axis=-2)


def kernel(x, mem, norm_mix, w_in, nsa_q_norm, nsa_k_norm, nsa_cmp_pos_k, nsa_cmp_pos_v, nsa_cmp_w1_k, nsa_cmp_w2_k, nsa_cmp_w1_v, nsa_cmp_w2_v, w_nsa_o, ssm_conv_w, ssm_conv_b, ssm_dt_bias, ssm_a_log, ssm_d, ssm_norm, w_ssm_o, hg_lb_logits, hg_norm, w_hg_o, w_out, norm_xa, norm_mem, xa_w_q, xa_w_k, xa_w_v, xa_q_norm, xa_k_norm, xa_w_o, norm_ffn, ffn_w_gate, ffn_w_up, ffn_w_down):
    bsz, s, d = x.shape
    depth = w_in.shape[0]
    assert d == D_MODEL and PREP_TILES * TOK == SEL_TILE
    assert s % SEL_TILE == 0 and s % (SEL_Q * TOK) == 0 and s // TOK >= WIN_TILES
    assert s // NSA_SEL_BLOCK >= NSA_TOPK and (s // NSA_CMP_STRIDE) % (4 * SUBLANES) == 0
    t = bsz * s
    hd = NSA_HEAD_DIM

    cos_t, sin_t = _rope_tables(jnp.arange(s))
    nbp = s // NSA_CMP_STRIDE
    cos_c, sin_c = _rope_tables(jnp.arange(nbp) * NSA_CMP_STRIDE + NSA_CMP_BLOCK - 1)
    tri = jnp.tril(jnp.ones((TOK, TOK), F32)).astype(BF16)
    rows = np.arange(LANES)[:, None]
    cols = np.arange(NSA_Q_WIDTH)[None, :]
    e8 = jnp.asarray(rows == cols // SSM_HEAD_DIM, F32).astype(BF16)
    li = np.arange(LANES)[:, None]
    lo = np.arange(LANES)[None, :]
    head_sum = jnp.asarray(li // hd == lo // hd, F32).astype(BF16)
    rot_half = jnp.asarray(np.where(lo % hd < hd // 2, -1.0 * (li == lo + hd // 2), 1.0 * (li == lo - hd // 2)),
                           F32).astype(BF16)

    lb_sm = jax.nn.softmax(hg_lb_logits.astype(F32), axis=0)
    lb_all = jnp.cumsum(lb_sm, axis=0) - lb_sm[0:1]

    def pad_lanes(v):
        return jnp.pad(v, (0, LANES - v.shape[0]))[None, :]

    x2 = x.reshape(t, d)
    for l in range(depth):
        gates, proj = _in_proj(x2, norm_mix[l][None, :], _pack_w_in(w_in[l]))

        qn2 = jnp.tile(nsa_q_norm[l], 2)[None, :]
        kn2 = jnp.tile(nsa_k_norm[l], 2)[None, :]
        qp, ksn, kwn, vst, vwt, gt = _nsa_prep(proj, cos_t, sin_t, qn2, kn2, head_sum, rot_half, bsz, s)
        w1k = _block_diag2(nsa_cmp_w1_k[l].reshape(NSA_CMP_BLOCK, hd, hd)).astype(BF16)
        w1v = _block_diag2(nsa_cmp_w1_v[l].reshape(NSA_CMP_BLOCK, hd, hd)).astype(BF16)
        kc, vct = _compress(proj, jnp.tile(nsa_cmp_pos_k[l], (1, 2)), jnp.tile(nsa_cmp_pos_v[l], (1, 2)),
                            w1k, _block_diag2(nsa_cmp_w2_k[l]).astype(BF16),
                            w1v, _block_diag2(nsa_cmp_w2_v[l]).astype(BF16), kn2, cos_c, sin_c, bsz, s)
        o_c, sel_bias = _cmp_topk(qp, kc, vct, gt, bsz, s)
        tile_list, tile_count = _active_tiles(sel_bias, bsz, s)
        o_s = _selected(qp, ksn, vst, sel_bias, gt, tile_list, tile_count, bsz, s)
        o_w = _window(qp, kwn, vwt, gt, bsz, s)

        y_b, y_c = _recurrent_mixers(
            proj, ssm_conv_w[l], ssm_conv_b[l][None, :], pad_lanes(ssm_dt_bias[l]), pad_lanes(ssm_a_log[l]),
            jnp.repeat(ssm_d[l], SSM_HEAD_DIM)[None, :], ssm_norm[l][None, :], e8,
            lb_all[l][None, :], hg_norm[l][None, :], tri, bsz, s)

        x2 = _merge(gates, o_c.reshape(t, -1), o_s.reshape(t, -1), o_w.reshape(t, -1), y_b, y_c, x2,
                    w_nsa_o[l].astype(BF16), w_ssm_o[l].astype(BF16), w_hg_o[l].astype(BF16),
                    w_out[l].astype(BF16))

        k_m, v_m = _xa_mem(mem, norm_mem[l][None, :], xa_w_k[l].astype(BF16), xa_w_v[l].astype(BF16),
                           xa_k_norm[l][None, :])
        x2 = _xa(x2, norm_xa[l][None, :], xa_w_q[l].astype(BF16), xa_q_norm[l][None, :], k_m, v_m,
                 xa_w_o[l].astype(BF16), s)
        x2 = _ffn(x2, norm_ffn[l][None, :], ffn_w_gate[l].astype(BF16), ffn_w_up[l].astype(BF16),
                  ffn_w_down[l].astype(BF16))
    return x2.reshape(bsz, s, d)
```

```python
import functools
import math

import numpy as np
import jax
import jax.numpy as jnp
from jax import lax
from jax.experimental import pallas as pl
from jax.experimental.pallas import tpu as pltpu

F32 = jnp.float32
BF16 = jnp.bfloat16

D_MODEL = 1024
NORM_EPS = 1e-6
ROPE_THETA = 10000.0
NEG_INF = -1e30

NSA_HEADS = 8
NSA_KV_GROUPS = 2
NSA_REP = NSA_HEADS // NSA_KV_GROUPS
NSA_HEAD_DIM = 64
NSA_CMP_BLOCK = 32
NSA_CMP_STRIDE = 16
NSA_SEL_BLOCK = 64
NSA_TOPK = 16
N_FORCED = 3
NSA_WINDOW = 512
NSA_Q_WIDTH = NSA_HEADS * NSA_HEAD_DIM
NSA_KV_WIDTH = NSA_KV_GROUPS * NSA_HEAD_DIM

SSM_INNER = 512
SSM_HEAD_DIM = 64
SSM_HEADS = SSM_INNER // SSM_HEAD_DIM
SSM_GROUPS = 2
SSM_STATE = 64
SSM_CONV = 4
SSM_CONV_DIM = SSM_INNER + 2 * SSM_GROUPS * SSM_STATE

HG_HEADS = 4
HG_KEY_DIM = 128
HG_VAL_DIM = 128
HG_WIDTH = HG_HEADS * HG_KEY_DIM
HG_SUB = 16

XA_HEADS = 4
XA_HEAD_DIM = 128
XA_WIDTH = XA_HEADS * XA_HEAD_DIM

FFN_HIDDEN = -(-(8 * D_MODEL) // (3 * 256)) * 256

SPLIT_SIZES = (
    NSA_Q_WIDTH, NSA_KV_WIDTH, NSA_KV_WIDTH, NSA_KV_WIDTH, NSA_KV_WIDTH, NSA_KV_WIDTH, NSA_KV_WIDTH,
    3 * NSA_HEADS, SSM_INNER, SSM_CONV_DIM, SSM_HEADS, HG_WIDTH, HG_WIDTH, HG_HEADS * HG_VAL_DIM,
    HG_HEADS * HG_VAL_DIM, D_MODEL, D_MODEL, D_MODEL,
)

LANES = 128
SUBLANES = 8
TOK = 128
NSA_BRANCHES = 3
GATE_ROWS = -(-NSA_BRANCHES * NSA_HEADS // SUBLANES) * SUBLANES
PREP_TILES = 4
SEL_Q = 4
SEL_TILE = 512
SEL_V_ROWS = NSA_HEAD_DIM + 16
WIN_TILES = NSA_WINDOW // TOK + 1
LOG2_E = math.log2(math.e)
MASK_FLOOR = -1e20
VMEM_LIMIT = 56 * 1024 * 1024

PROJ_WIDTH = 8192
GATE_WIDTH = 3 * D_MODEL
COL_GA, COL_GB, COL_GC = 0, 1024, 2048
COL_Q = 0
COL_Z = 512
COL_HQ, COL_HF, COL_HI, COL_HG = 1024, 1536, 2048, 2560
COL_XBC = 3072
COL_KC, COL_VC, COL_KS, COL_VS, COL_KW, COL_VW = 3840, 3968, 4096, 4224, 4352, 4480
COL_NG = 4608
COL_DT = 4736


def _cparams(sem):
    return pltpu.CompilerParams(dimension_semantics=sem, vmem_limit_bytes=VMEM_LIMIT)


def _dot(a, b):
    return jnp.dot(a, b, preferred_element_type=F32)


def _dot_nt(a, b):
    return lax.dot_general(a, b, (((1,), (1,)), ((), ())), preferred_element_type=F32)


def _split3(a):
    hi = a.astype(BF16)
    r1 = a - hi.astype(F32)
    mid = r1.astype(BF16)
    lo = (r1 - mid.astype(F32)).astype(BF16)
    return hi, mid, lo


def _dot_sel_r(a, sel):
    hi, mid, lo = _split3(a)
    return _dot(hi, sel) + _dot(mid, sel) + _dot(lo, sel)


def _dot_sel_l(sel, a):
    hi, mid, lo = _split3(a)
    return _dot(sel, hi) + _dot(sel, mid) + _dot(sel, lo)


def _silu(x):
    return x * jax.nn.sigmoid(x)


def _rms(x, w):
    return x * lax.rsqrt(jnp.mean(x * x, axis=-1, keepdims=True) + NORM_EPS) * w


def _inproj_kernel(x_ref, nw_ref, w_ref, og_ref, of_ref, h_ref, *, n_gate):
    j = pl.program_id(1)

    @pl.when(j == 0)
    def _():
        h_ref[...] = _rms(x_ref[...], nw_ref[...]).astype(BF16)

    @pl.when(j < n_gate)
    def _():
        og_ref[...] = _dot(h_ref[...], w_ref[...]).astype(BF16)

    @pl.when(j >= n_gate)
    def _():
        of_ref[...] = _dot(h_ref[...], w_ref[...])


def _in_proj(x2, norm_w, w_packed):
    t = x2.shape[0]
    tm = min(2048, t)
    tn = 1024
    n_gate = GATE_WIDTH // tn
    return pl.pallas_call(
        functools.partial(_inproj_kernel, n_gate=n_gate),
        grid=(t // tm, PROJ_WIDTH // tn),
        in_specs=[
            pl.BlockSpec((tm, D_MODEL), lambda i, j: (i, 0)),
            pl.BlockSpec((1, D_MODEL), lambda i, j: (0, 0)),
            pl.BlockSpec((D_MODEL, tn), lambda i, j: (0, j)),
        ],
        out_specs=[pl.BlockSpec((tm, tn), lambda i, j: (i, jnp.minimum(j, n_gate - 1))),
                   pl.BlockSpec((tm, tn), lambda i, j: (i, jnp.maximum(j - n_gate, 0)))],
        out_shape=[jax.ShapeDtypeStruct((t, GATE_WIDTH), BF16),
                   jax.ShapeDtypeStruct((t, PROJ_WIDTH - GATE_WIDTH), F32)],
        scratch_shapes=[pltpu.VMEM((tm, D_MODEL), BF16)],
        compiler_params=_cparams(("parallel", "arbitrary")),
        name="in_proj",
    )(x2, norm_w, w_packed)


def _norm_rope_pair(x, w, cos, sin):
    lane = lax.broadcasted_iota(jnp.int32, x.shape, 1)
    lo = lane < NSA_HEAD_DIM
    sq = x * x
    s_lo = jnp.sum(jnp.where(lo, sq, 0.0), axis=1, keepdims=True)
    s_hi = jnp.sum(jnp.where(lo, 0.0, sq), axis=1, keepdims=True)
    ms = jnp.where(lo, s_lo, s_hi) * (1.0 / NSA_HEAD_DIM)
    y = x * lax.rsqrt(ms + NORM_EPS) * w
    half = NSA_HEAD_DIM // 2
    first = (lane % NSA_HEAD_DIM) < half
    rot = jnp.where(first, -pltpu.roll(y, LANES - half, 1), pltpu.roll(y, half, 1))
    return y * cos + rot * sin


def _norm_rope_pair_mxu(x, w, cos, sin, head_sum, rot_half):
    ms = _dot_sel_r(x * x, head_sum) * (1.0 / NSA_HEAD_DIM)
    y = x * lax.rsqrt(ms + NORM_EPS) * w
    return y * cos + _dot_sel_r(y, rot_half) * sin


def _nsa_prep_kernel(q_ref, ks_ref, vs_ref, kw_ref, vw_ref, ng_ref, cos_ref, sin_ref, qn_ref, kn_ref,
                     hsum_ref, rot_ref, qp_ref, ksn_ref, kwn_ref, vst_ref, vwt_ref, gt_ref):
    norm_rope = functools.partial(_norm_rope_pair_mxu, head_sum=hsum_ref[...], rot_half=rot_ref[...])
    lane = lax.broadcasted_iota(jnp.int32, (TOK, LANES), 1)
    row = lax.broadcasted_iota(jnp.int32, (TOK, LANES), 0)
    scale = NSA_HEAD_DIM ** -0.5 * LOG2_E
    ones_rows = (lax.broadcasted_iota(jnp.int32, (SEL_V_ROWS - NSA_HEAD_DIM, TOK), 0) == 0).astype(BF16)
    for u in range(PREP_TILES):
        rows = slice(u * TOK, (u + 1) * TOK)
        cos = cos_ref[rows, :]
        sin = sin_ref[rows, :]
        for c in range(NSA_HEADS // 2):
            y = norm_rope(q_ref[rows, c * LANES:(c + 1) * LANES], qn_ref[...], cos, sin) * scale
            y_sw = pltpu.roll(y, NSA_HEAD_DIM, 1)
            for hh in range(2):
                h = 2 * c + hh
                g, r = divmod(h, NSA_REP)
                src = y if hh == g else y_sw
                keep = (lane // NSA_HEAD_DIM) == g
                qp_ref[0, g, rows, r * LANES:(r + 1) * LANES] = jnp.where(keep, src, 0.0).astype(BF16)
        ks_n = norm_rope(ks_ref[rows, :], kn_ref[...], cos, sin)
        tile = pl.program_id(1) * PREP_TILES + u
        blk = (tile * (TOK // NSA_SEL_BLOCK) + row // NSA_SEL_BLOCK) % NSA_HEAD_DIM
        onehot = ((lane % NSA_HEAD_DIM) == blk).astype(F32)
        for g in range(NSA_KV_GROUPS):
            ksn_ref[0, g, rows, :] = jnp.where((lane // NSA_HEAD_DIM) == g, ks_n, onehot).astype(BF16)
        kwn_ref[0, rows, :] = norm_rope(kw_ref[rows, :], kn_ref[...], cos, sin).astype(BF16)
        vs_t = vs_ref[rows, :].T.astype(BF16)
        vw_t = vw_ref[rows, :].T.astype(BF16)
        for g in range(NSA_KV_GROUPS):
            vst_ref[0, g, 0, 0:NSA_HEAD_DIM, rows] = vs_t[g * NSA_HEAD_DIM:(g + 1) * NSA_HEAD_DIM, :]
            vst_ref[0, g, 0, NSA_HEAD_DIM:SEL_V_ROWS, rows] = ones_rows
            vwt_ref[0, g, u] = vw_t[g * NSA_HEAD_DIM:(g + 1) * NSA_HEAD_DIM, :]
        gt_ref[0, u] = jax.nn.sigmoid(ng_ref[rows, :]).T[0:GATE_ROWS, :]


def _nsa_prep(proj, cos_t, sin_t, qn2, kn2, head_sum, rot_half, bsz, s):
    tq = PREP_TILES * TOK
    n_step = s // tq

    def col(c0, w):
        return pl.BlockSpec((tq, w), lambda b, i: (b * n_step + i, c0 // w))

    tab = pl.BlockSpec((tq, LANES), lambda b, i: (i, 0))
    vec = pl.BlockSpec((1, LANES), lambda b, i: (0, 0))
    sq = pl.BlockSpec((LANES, LANES), lambda b, i: (0, 0))
    return pl.pallas_call(
        _nsa_prep_kernel,
        grid=(bsz, n_step),
        in_specs=[col(COL_Q, NSA_Q_WIDTH), col(COL_KS, LANES), col(COL_VS, LANES), col(COL_KW, LANES),
                  col(COL_VW, LANES), col(COL_NG, LANES), tab, tab, vec, vec, sq, sq],
        out_specs=[
            pl.BlockSpec((1, NSA_KV_GROUPS, tq, NSA_REP * LANES), lambda b, i: (b, 0, i, 0)),
            pl.BlockSpec((1, NSA_KV_GROUPS, tq, LANES), lambda b, i: (b, 0, i, 0)),
            pl.BlockSpec((1, tq, LANES), lambda b, i: (b, i, 0)),
            pl.BlockSpec((1, NSA_KV_GROUPS, 1, SEL_V_ROWS, SEL_TILE), lambda b, i: (b, 0, i, 0, 0)),
            pl.BlockSpec((1, NSA_KV_GROUPS, PREP_TILES, NSA_HEAD_DIM, TOK), lambda b, i: (b, 0, i, 0, 0)),
            pl.BlockSpec((1, PREP_TILES, GATE_ROWS, TOK), lambda b, i: (b, i, 0, 0)),
        ],
        out_shape=[
            jax.ShapeDtypeStruct((bsz, NSA_KV_GROUPS, s, NSA_REP * LANES), BF16),
            jax.ShapeDtypeStruct((bsz, NSA_KV_GROUPS, s, LANES), BF16),
            jax.ShapeDtypeStruct((bsz, s, LANES), BF16),
            jax.ShapeDtypeStruct((bsz, NSA_KV_GROUPS, s // SEL_TILE, SEL_V_ROWS, SEL_TILE), BF16),
            jax.ShapeDtypeStruct((bsz, NSA_KV_GROUPS, s // TOK, NSA_HEAD_DIM, TOK), BF16),
            jax.ShapeDtypeStruct((bsz, s // TOK, GATE_ROWS, TOK), F32),
        ],
        compiler_params=_cparams(("parallel", "parallel")),
        name="nsa_prep",
    )(proj, proj, proj, proj, proj, proj, cos_t, sin_t, qn2, kn2, head_sum, rot_half)


def _compress_kernel(kc_ref, vc_ref, posk_ref, posv_ref, w1k_ref, w2k_ref, w1v_ref, w2v_ref, kn_ref,
                     cos_ref, sin_ref, kco_ref, vct_ref, sh_ref, *, nbp):
    half_blk = NSA_CMP_BLOCK // 2

    def compress(x_ref, pos_ref, w1_ref, w2_ref):
        acc_a = jnp.zeros((nbp, LANES), F32)
        acc_b = jnp.zeros((nbp, LANES), F32)
        for l in range(half_blk):
            xl = x_ref[pl.ds(l, nbp, stride=NSA_CMP_STRIDE), :]
            acc_a = acc_a + _dot((xl + pos_ref[l:l + 1, :]).astype(BF16), w1_ref[l])
            acc_b = acc_b + _dot((xl + pos_ref[half_blk + l:half_blk + l + 1, :]).astype(BF16),
                                 w1_ref[half_blk + l])
        sh_ref[pl.ds(0, nbp), :] = acc_b
        sh_ref[pl.ds(nbp, SUBLANES), :] = jnp.zeros((SUBLANES, LANES), F32)
        hid = acc_a + sh_ref[pl.ds(1, nbp), :]
        return _dot(_silu(hid).astype(BF16), w2_ref[...])

    kc = compress(kc_ref, posk_ref, w1k_ref, w2k_ref)
    kco_ref[0] = _norm_rope_pair(kc, kn_ref[...], cos_ref[...], sin_ref[...]).astype(BF16)
    vc = compress(vc_ref, posv_ref, w1v_ref, w2v_ref)
    vct_ref[0] = vc.T.astype(BF16)


def _compress(proj, posk, posv, w1k, w2k, w1v, w2v, kn2, cos_c, sin_c, bsz, s):
    nbp = s // NSA_CMP_STRIDE

    def full(shape):
        return pl.BlockSpec(shape, lambda b: (0,) * len(shape))

    return pl.pallas_call(
        functools.partial(_compress_kernel, nbp=nbp),
        grid=(bsz,),
        in_specs=[
            pl.BlockSpec((s, LANES), lambda b: (b, COL_KC // LANES)),
            pl.BlockSpec((s, LANES), lambda b: (b, COL_VC // LANES)),
            full((NSA_CMP_BLOCK, LANES)), full((NSA_CMP_BLOCK, LANES)),
            full((NSA_CMP_BLOCK, LANES, LANES)), full((LANES, LANES)),
            full((NSA_CMP_BLOCK, LANES, LANES)), full((LANES, LANES)),
            full((1, LANES)), full((nbp, LANES)), full((nbp, LANES)),
        ],
        out_specs=[
            pl.BlockSpec((1, nbp, LANES), lambda b: (b, 0, 0)),
            pl.BlockSpec((1, LANES, nbp), lambda b: (b, 0, 0)),
        ],
        out_shape=[
            jax.ShapeDtypeStruct((bsz, nbp, LANES), BF16),
            jax.ShapeDtypeStruct((bsz, LANES, nbp), BF16),
        ],
        scratch_shapes=[pltpu.VMEM((nbp + SUBLANES, LANES), F32)],
        compiler_params=_cparams(("parallel",)),
        name="nsa_compress",
    )(proj, proj, posk, posv, w1k, w2k, w1v, w2v, kn2, cos_c, sin_c)


def _q_rows(qp):
    return jnp.concatenate([qp[:, r * LANES:(r + 1) * LANES] for r in range(NSA_REP)], axis=0)


def _gate_row(gt, g, branch):
    rows = [(g * NSA_REP + r) * NSA_BRANCHES + branch for r in range(NSA_REP)]
    return jnp.concatenate([gt[i:i + 1, :] for i in rows], axis=1)


def _heads_to_token_major(acc_g):
    outs = []
    for p in range(NSA_REP // 2):
        blk = jnp.concatenate([acc_g[:, (2 * p) * TOK:(2 * p + 1) * TOK],
                               acc_g[:, (2 * p + 1) * TOK:(2 * p + 2) * TOK]], axis=0)
        outs.append(blk.T)
    return jnp.concatenate(outs, axis=1)


def _cmp_topk_kernel(qp_ref, kc_ref, vct_ref, gt_ref, oc_ref, mask_ref, ps_ref, *, nbp, nsel):
    qi = pl.program_id(1)
    ncol = NSA_REP * TOK
    per = NSA_SEL_BLOCK // NSA_CMP_STRIDE

    def attend(nrows):
        nblk = nrows // per
        j_io = lax.broadcasted_iota(jnp.int32, (nblk, TOK), 0)
        t_sel = qi * TOK + lax.broadcasted_iota(jnp.int32, (nblk, TOK), 1)
        cur = t_sel // NSA_SEL_BLOCK
        forced = (j_io == 0) | (j_io == cur) | (j_io == cur - 1)
        valid = j_io * NSA_SEL_BLOCK <= t_sel
        rel = (lax.broadcasted_iota(jnp.int32, (nrows, ncol), 0) * NSA_CMP_STRIDE + (NSA_CMP_BLOCK - 1)
               - lax.broadcasted_iota(jnp.int32, (nrows, ncol), 1) % TOK)
        allowed = rel <= qi * TOK
        kc = kc_ref[0, 0:nrows, :]
        vct = vct_ref[0, :, 0:nrows]
        for g in range(NSA_KV_GROUPS):
            ps_ref[g, pl.ds(0, SUBLANES), :] = jnp.zeros((SUBLANES, TOK), F32)
        scores = [_dot_nt(kc, _q_rows(qp_ref[0, g])) for g in range(NSA_KV_GROUPS)]
        outs = []
        imps = []
        for g in range(NSA_KV_GROUPS):
            s_t = jnp.where(allowed, scores[g], NEG_INF)
            m = jnp.max(s_t, axis=0, keepdims=True)
            p = jnp.exp2(s_t - jnp.maximum(m, MASK_FLOOR))
            l = jnp.sum(p, axis=0, keepdims=True)
            inv = jnp.where(l > 0.0, 1.0 / l, 0.0)
            o_t = _dot(vct, p.astype(BF16)) * inv
            gate = _gate_row(gt_ref[0, 0], g, 0)
            outs.append(_heads_to_token_major(o_t[g * NSA_HEAD_DIM:(g + 1) * NSA_HEAD_DIM, :] * gate))
            pn = p * inv
            psum = pn[:, 0:TOK]
            for r in range(1, NSA_REP):
                psum = psum + pn[:, r * TOK:(r + 1) * TOK]
            ps_ref[g, pl.ds(SUBLANES, nrows), :] = psum
            imp = (ps_ref[g, pl.ds(SUBLANES - 1, nblk, stride=per), :]
                   + ps_ref[g, pl.ds(SUBLANES + per - 1, nblk, stride=per), :])
            for c in range(per - 1):
                imp = imp + 2.0 * ps_ref[g, pl.ds(SUBLANES + c, nblk, stride=per), :]
            imps.append(jnp.where(forced, -jnp.inf, jnp.where(valid, imp, NEG_INF)))
        oc_ref[0] = jnp.concatenate(outs, axis=1).astype(BF16)

        def pick(_, carry):
            nxt = []
            for imp_c in carry:
                mx = jnp.max(imp_c, axis=0, keepdims=True)
                idx = jnp.min(jnp.where(imp_c == mx, j_io, nblk), axis=0, keepdims=True)
                nxt.append(jnp.where(j_io == idx, -jnp.inf, imp_c))
            return tuple(nxt)

        picked = lax.fori_loop(0, min(NSA_TOPK, nsel) - N_FORCED, pick, tuple(imps))
        for g in range(NSA_KV_GROUPS):
            mask_ref[0, g, 0, 0:nblk, :] = jnp.where(valid & (picked[g] == -jnp.inf), 0.0, NEG_INF)
            if nblk < nsel:
                mask_ref[0, g, 0, nblk:nsel, :] = jnp.full((nsel - nblk, TOK), NEG_INF, F32)

    n_vis = (qi * TOK + TOK - NSA_CMP_BLOCK) // NSA_CMP_STRIDE + 1
    quarter = nbp // 4
    for k in range(1, 5):
        @pl.when((n_vis > (k - 1) * quarter) & (n_vis <= k * quarter))
        def _(k=k):
            attend(k * quarter)


def _cmp_topk(qp, kc, vct, gt, bsz, s):
    nt = s // TOK
    nbp = s // NSA_CMP_STRIDE
    nsel = s // NSA_SEL_BLOCK
    return pl.pallas_call(
        functools.partial(_cmp_topk_kernel, nbp=nbp, nsel=nsel),
        grid=(bsz, nt),
        in_specs=[
            pl.BlockSpec((1, NSA_KV_GROUPS, TOK, NSA_REP * LANES), lambda b, i: (b, 0, i, 0)),
            pl.BlockSpec((1, nbp, LANES), lambda b, i: (b, 0, 0)),
            pl.BlockSpec((1, LANES, nbp), lambda b, i: (b, 0, 0)),
            pl.BlockSpec((1, 1, GATE_ROWS, TOK), lambda b, i: (b, i, 0, 0)),
        ],
        out_specs=[
            pl.BlockSpec((1, TOK, NSA_Q_WIDTH), lambda b, i: (b, i, 0)),
            pl.BlockSpec((1, NSA_KV_GROUPS, 1, nsel, TOK), lambda b, i: (b, 0, i, 0, 0)),
        ],
        out_shape=[
            jax.ShapeDtypeStruct((bsz, s, NSA_Q_WIDTH), BF16),
            jax.ShapeDtypeStruct((bsz, NSA_KV_GROUPS, nt, nsel, TOK), F32),
        ],
        scratch_shapes=[pltpu.VMEM((NSA_KV_GROUPS, nbp + SUBLANES, TOK), F32)],
        compiler_params=_cparams(("parallel", "parallel")),
        name="nsa_cmp_topk",
    )(qp, kc, vct, gt)


def _window_kernel(qp_ref, k_ref, vt_ref, gt_ref, o_ref):
    qi = pl.program_id(1)
    ncol = NSA_REP * TOK
    wk = WIN_TILES * TOK

    def run(first_tile, interior):
        kwin = k_ref[0, pl.ds(pl.multiple_of(first_tile * TOK, TOK), wk), :]
        scores = [_dot_nt(kwin, _q_rows(qp_ref[0, g])) for g in range(NSA_KV_GROUPS)]
        probs = []
        scales = []
        for g in range(NSA_KV_GROUPS):
            s_t = scores[g]
            if interior:
                row = lax.broadcasted_iota(jnp.int32, (TOK, ncol), 0)
                col = lax.broadcasted_iota(jnp.int32, (TOK, ncol), 1) % TOK
                oldest = jnp.where(row > col, s_t[:TOK], NEG_INF)
                newest = jnp.where(row <= col, s_t[wk - TOK:], NEG_INF)
                s_t = jnp.concatenate([oldest, s_t[TOK:wk - TOK], newest], axis=0)
            else:
                key = first_tile * TOK + lax.broadcasted_iota(jnp.int32, (wk, ncol), 0)
                t_io = qi * TOK + lax.broadcasted_iota(jnp.int32, (wk, ncol), 1) % TOK
                s_t = jnp.where((key <= t_io) & (t_io - key < NSA_WINDOW), s_t, NEG_INF)
            m = jnp.max(s_t, axis=0, keepdims=True)
            p = jnp.exp2(s_t - m)
            l = jnp.sum(p, axis=0, keepdims=True)
            probs.append(p.astype(BF16))
            scales.append(_gate_row(gt_ref[0, 0], g, 2) / l)
        outs = []
        for g in range(NSA_KV_GROUPS):
            vt = jnp.concatenate([vt_ref[0, g, first_tile + k] for k in range(WIN_TILES)], axis=1)
            outs.append(_heads_to_token_major(_dot(vt, probs[g]) * scales[g]))
        o_ref[0] = jnp.concatenate(outs, axis=1).astype(BF16)

    @pl.when(qi >= WIN_TILES - 1)
    def _():
        run(qi - (WIN_TILES - 1), True)

    @pl.when(qi < WIN_TILES - 1)
    def _():
        run(0, False)


def _window(qp, k, vt, gt, bsz, s):
    nt = s // TOK
    return pl.pallas_call(
        _window_kernel,
        grid=(bsz, nt),
        in_specs=[
            pl.BlockSpec((1, NSA_KV_GROUPS, TOK, NSA_REP * LANES), lambda b, i: (b, 0, i, 0)),
            pl.BlockSpec((1, s, LANES), lambda b, i: (b, 0, 0)),
            pl.BlockSpec((1, NSA_KV_GROUPS, nt, NSA_HEAD_DIM, TOK), lambda b, i: (b, 0, 0, 0, 0)),
            pl.BlockSpec((1, 1, GATE_ROWS, TOK), lambda b, i: (b, i, 0, 0)),
        ],
        out_specs=pl.BlockSpec((1, TOK, NSA_Q_WIDTH), lambda b, i: (b, i, 0)),
        out_shape=jax.ShapeDtypeStruct((bsz, s, NSA_Q_WIDTH), BF16),
        compiler_params=_cparams(("parallel", "parallel")),
        name="nsa_window",
    )(qp, k, vt, gt)


def _selected_kernel(list_ref, cnt_ref, qp_ref, k_ref, vt_ref, bias_ref, gt_ref, o_ref, m_ref, acc_ref,
                     s_ref, p_ref, a_ref, qa_ref, *, nt, nsel):
    b = pl.program_id(0)
    qi = pl.program_id(1)
    ncol = NSA_REP * TOK
    blk_per_tile = SEL_TILE // NSA_SEL_BLOCK
    n_kt = nt // (SEL_TILE // TOK)
    n_q = nt // SEL_Q
    units = [(g, h) for g in range(NSA_KV_GROUPS) for h in range(SEL_Q)]
    m_ref[...] = jnp.full(m_ref.shape, NEG_INF, F32)
    acc_ref[...] = jnp.zeros(acc_ref.shape, F32)
    p_ref[...] = jnp.zeros(p_ref.shape, BF16)
    a_ref[...] = jnp.ones(a_ref.shape, F32)
    base = (b * n_q + qi) * n_kt
    count = cnt_ref[b * n_q + qi]
    n_diag = (qi * SEL_Q * TOK) // SEL_TILE

    lane = lax.broadcasted_iota(jnp.int32, (TOK, LANES), 1)
    n_var = qa_ref.shape[0]
    for g, h in units:
        bias_g = bias_ref[0, g, h]
        if nsel % LANES:
            bias_g = jnp.concatenate([bias_g, jnp.zeros((LANES - nsel % LANES, TOK), F32)], axis=0)
        own = (lane // NSA_HEAD_DIM) == g
        for v in range(n_var):
            src = bias_g[(v // 2) * LANES:(v // 2 + 1) * LANES, :].T
            if v % 2 != 1 - g:
                src = pltpu.roll(src, NSA_HEAD_DIM, 1)
            src = src.astype(BF16)
            for r in range(NSA_REP):
                q_r = qp_ref[0, g, h * TOK:(h + 1) * TOK, r * LANES:(r + 1) * LANES]
                qa_ref[v, g, h, r * TOK:(r + 1) * TOK, :] = jnp.where(own, q_r, src)

    def tile_at(i):
        return jnp.where(i < count, list_ref[base + jnp.minimum(i, n_kt - 1)], n_diag)

    def scores(j):
        v = (j * blk_per_tile) // NSA_HEAD_DIM
        for g in range(NSA_KV_GROUPS):
            kb = k_ref[0, g, pl.ds(pl.multiple_of(j * SEL_TILE, SEL_TILE), SEL_TILE), :]
            for h in range(SEL_Q):
                s_ref[g, h] = _dot_nt(kb, qa_ref[v, g, h])

    def values(j):
        for g, h in units:
            acc_ref[g, h] = acc_ref[g, h] * a_ref[g, h] + _dot(vt_ref[0, g, j], p_ref[g, h])

    def softmax(j, causal):
        for g, h in units:
            s_t = s_ref[g, h]
            if causal:
                t_io = ((qi * SEL_Q + h) * TOK
                        + lax.broadcasted_iota(jnp.int32, (SEL_TILE, ncol), 1) % TOK)
                key_pos = j * SEL_TILE + lax.broadcasted_iota(jnp.int32, (SEL_TILE, ncol), 0)
                s_t = jnp.where(key_pos <= t_io, s_t, NEG_INF)
            m_old = m_ref[g, h]
            m_new = jnp.maximum(m_old, jnp.max(s_t, axis=0, keepdims=True))
            alpha = jnp.exp2(m_old - m_new)
            p = jnp.exp2(s_t - jnp.maximum(m_new, MASK_FLOOR))
            m_ref[g, h] = m_new
            p_ref[g, h] = p.astype(BF16)
            a_ref[g, h] = alpha

    scores(tile_at(0))

    def body(i, c):
        values(tile_at(jnp.maximum(i - 1, 0)))
        softmax(tile_at(i), False)
        scores(tile_at(i + 1))
        return c

    lax.fori_loop(0, count, body, 0)
    values(tile_at(jnp.maximum(count - 1, 0)))
    softmax(n_diag, True)
    values(n_diag)
    for h in range(SEL_Q):
        outs = []
        for g in range(NSA_KV_GROUPS):
            acc = acc_ref[g, h]
            gate = _gate_row(gt_ref[0, h], g, 1)
            outs.append(_heads_to_token_major(acc[:NSA_HEAD_DIM] * (gate / acc[NSA_HEAD_DIM:NSA_HEAD_DIM + 1])))
        o_ref[0, h * TOK:(h + 1) * TOK, :] = jnp.concatenate(outs, axis=1).astype(BF16)


def _selected(qp, k, vt, bias, gt, tile_list, tile_count, bsz, s):
    nt = s // TOK
    nsel = s // NSA_SEL_BLOCK
    ncol = NSA_REP * TOK
    unit = (NSA_KV_GROUPS, SEL_Q)
    grid_spec = pltpu.PrefetchScalarGridSpec(
        num_scalar_prefetch=2,
        grid=(bsz, nt // SEL_Q),
        in_specs=[
            pl.BlockSpec((1, NSA_KV_GROUPS, SEL_Q * TOK, NSA_REP * LANES), lambda b, i, tl, tc: (b, 0, i, 0)),
            pl.BlockSpec((1, NSA_KV_GROUPS, s, LANES), lambda b, i, tl, tc: (b, 0, 0, 0)),
            pl.BlockSpec((1, NSA_KV_GROUPS, s // SEL_TILE, SEL_V_ROWS, SEL_TILE),
                         lambda b, i, tl, tc: (b, 0, 0, 0, 0)),
            pl.BlockSpec((1, NSA_KV_GROUPS, SEL_Q, nsel, TOK), lambda b, i, tl, tc: (b, 0, i, 0, 0)),
            pl.BlockSpec((1, SEL_Q, GATE_ROWS, TOK), lambda b, i, tl, tc: (b, i, 0, 0)),
        ],
        out_specs=pl.BlockSpec((1, SEL_Q * TOK, NSA_Q_WIDTH), lambda b, i, tl, tc: (b, i, 0)),
        scratch_shapes=[pltpu.VMEM(unit + (1, ncol), F32),
                        pltpu.VMEM(unit + (SEL_V_ROWS, ncol), F32),
                        pltpu.VMEM(unit + (SEL_TILE, ncol), F32),
                        pltpu.VMEM(unit + (SEL_TILE, ncol), BF16),
                        pltpu.VMEM(unit + (1, ncol), F32),
                        pltpu.VMEM((max(nsel // NSA_HEAD_DIM, 1),) + unit + (ncol, LANES), BF16)],
    )
    return pl.pallas_call(
        functools.partial(_selected_kernel, nt=nt, nsel=nsel),
        grid_spec=grid_spec,
        out_shape=jax.ShapeDtypeStruct((bsz, s, NSA_Q_WIDTH), BF16),
        compiler_params=_cparams(("parallel", "parallel")),
        name="nsa_selected",
    )(tile_list, tile_count, qp, k, vt, bias, gt)


def _active_tiles(bias, bsz, s):
    n_q = s // (SEL_Q * TOK)
    n_kt = s // SEL_TILE
    rows = SEL_TILE // NSA_SEL_BLOCK * TOK
    act = bias.reshape(bsz, NSA_KV_GROUPS, n_q, SEL_Q, n_kt, rows).max(axis=(1, 3, 5)) > -1.0
    n_diag = (jnp.arange(n_q) * SEL_Q * TOK) // SEL_TILE
    act = act & (jnp.arange(n_kt)[None, None, :] < n_diag[None, :, None])
    order = jnp.argsort(jnp.where(act, 0, 1), axis=-1, stable=True).astype(jnp.int32)
    return order.reshape(-1), jnp.sum(act, axis=-1, dtype=jnp.int32).reshape(-1)


def _ssd_kernel(z_ref, xbc_ref, dt_ref, cw_ref, cb_ref, dtb_ref, alog_ref, dskip_ref, nw_ref, e8_ref,
                tri_ref, y_ref, ext_ref, st_ref):
    @pl.when(pl.program_id(0) == 0)
    def _():
        ext_ref[pl.ds(0, SUBLANES), :] = jnp.zeros((SUBLANES, SSM_CONV_DIM), F32)
        st_ref[...] = jnp.zeros_like(st_ref)

    xb = xbc_ref[...]
    ext_ref[pl.ds(SUBLANES, TOK), :] = xb
    conv = cb_ref[...] + xb * cw_ref[SSM_CONV - 1:SSM_CONV, :]
    for k in range(SSM_CONV - 1):
        conv = conv + ext_ref[pl.ds(SUBLANES - (SSM_CONV - 1) + k, TOK), :] * cw_ref[k:k + 1, :]
    ext_ref[pl.ds(0, SUBLANES), :] = xb[TOK - SUBLANES:TOK, :]
    u = _silu(conv)
    xs = u[:, :SSM_INNER]
    gn = SSM_GROUPS * SSM_STATE
    bm = u[:, SSM_INNER:SSM_INNER + gn]
    cm = u[:, SSM_INNER + gn:]
    dt = jax.nn.softplus(dt_ref[...] + dtb_ref[...])
    da = dt * (-jnp.exp(alog_ref[...]))
    tri = tri_ref[...]
    e8 = e8_ref[...]
    cum = _dot_sel_l(tri, da)
    cum_t = cum.T
    cum_e = _dot_sel_r(cum, e8)
    dt_e = _dot_sel_r(dt, e8)
    xdt = xs * dt_e
    lane = lax.broadcasted_iota(jnp.int32, (TOK, LANES), 1)
    t_io = lax.broadcasted_iota(jnp.int32, (TOK, TOK), 0)
    s_io = lax.broadcasted_iota(jnp.int32, (TOK, TOK), 1)
    causal = s_io <= t_io
    hg = SSM_HEADS // SSM_GROUPS
    y_parts = []
    cbs = []
    for g in range(SSM_GROUPS):
        in_g = (lane // SSM_STATE) == g
        cbs.append(_dot_nt(jnp.where(in_g, cm, 0.0).astype(BF16), bm.astype(BF16)))
    for c in range(SSM_HEADS // 2):
        acc = jnp.zeros((TOK, LANES), F32)
        xpair = xdt[:, c * LANES:(c + 1) * LANES]
        for hh in range(2):
            h = 2 * c + hh
            seg = cum[:, h:h + 1] - cum_t[h:h + 1, :]
            decay = jnp.where(causal, jnp.exp(jnp.where(causal, seg, 0.0)), 0.0)
            mat = (cbs[h // hg] * decay).astype(BF16)
            xh = jnp.where((lane // SSM_HEAD_DIM) == hh, xpair, 0.0).astype(BF16)
            acc = acc + _dot(mat, xh)
        y_parts.append(acc)
    y_diag = jnp.concatenate(y_parts, axis=1)
    prev = st_ref[...]
    y_off = _dot(cm.astype(BF16), prev.astype(BF16)) * jnp.exp(cum_e)
    cum_last = cum_e[TOK - 1:TOK, :]
    xdec = (xdt * jnp.exp(cum_last - cum_e)).astype(BF16)
    new = _dot(bm.T.astype(BF16), xdec)
    row_g = lax.broadcasted_iota(jnp.int32, (LANES, SSM_INNER), 0) // SSM_STATE
    col_g = lax.broadcasted_iota(jnp.int32, (LANES, SSM_INNER), 1) // (hg * SSM_HEAD_DIM)
    st_ref[...] = prev * jnp.exp(cum_last) + jnp.where(row_g == col_g, new, 0.0)
    y = (y_diag + y_off + xs * dskip_ref[...]) * _silu(z_ref[...])
    gw = SSM_INNER // SSM_GROUPS
    outs = []
    for g in range(SSM_GROUPS):
        yg = y[:, g * gw:(g + 1) * gw]
        outs.append(_rms(yg, nw_ref[:, g * gw:(g + 1) * gw]))
    y_ref[...] = jnp.concatenate(outs, axis=1).astype(BF16)


SSD_INPUTS = 11


def _hgrn2_kernel(q_ref, f_ref, i_ref, g_ref, lb_ref, nw_ref, tri_ref, y_ref, st_ref, kp_ref, bp_ref, vp_ref):
    @pl.when(pl.program_id(0) == 0)
    def _():
        st_ref[...] = jnp.zeros_like(st_ref)
        kp_ref[...] = jnp.zeros_like(kp_ref)
        bp_ref[...] = jnp.zeros_like(bp_ref)
        vp_ref[...] = jnp.zeros_like(vp_ref)

    nsub = TOK // HG_SUB
    in_sub = lax.broadcasted_iota(jnp.int32, (TOK, HG_KEY_DIM), 0) % HG_SUB
    tri = tri_ref[...]
    outs = []
    for h in range(HG_HEADS):
        sl = slice(h * HG_KEY_DIM, (h + 1) * HG_KEY_DIM)
        lb = lb_ref[:, sl]
        fr = f_ref[:, sl]
        q = _silu(q_ref[:, sl]) * (HG_KEY_DIM ** -0.5)
        log_f = jnp.log(lb + (1.0 - lb) * jax.nn.sigmoid(fr))
        k = (1.0 - lb) * jax.nn.sigmoid(-fr)
        v = i_ref[:, sl]
        bcum = _dot_sel_l(tri, log_f)
        st = st_ref[h]
        o = _dot_nt((q * jnp.exp(bcum)).astype(BF16), st.astype(BF16))
        ends = [bcum[(j + 1) * HG_SUB - 1:(j + 1) * HG_SUB, :] for j in range(nsub)]
        blk = (HG_SUB, HG_KEY_DIM)
        zero_blk = jnp.zeros(blk, BF16)
        e_prev = jnp.concatenate([jnp.zeros(blk, F32)] + [jnp.broadcast_to(ends[j], blk) for j in range(nsub - 1)],
                                 axis=0)
        e_own = jnp.concatenate([jnp.broadcast_to(ends[j], blk) for j in range(nsub)], axis=0)
        q_loc = (q * jnp.exp(bcum - e_prev)).astype(BF16)
        k_loc = k * jnp.exp(e_own - bcum)
        q_cols = []
        k_cols = []
        for i in range(1, nsub):
            q_cols.append(jnp.concatenate(
                [zero_blk] * i + [q_loc[i * HG_SUB:(i + 1) * HG_SUB]] + [zero_blk] * (nsub - 1 - i), axis=0))
            parts = []
            for j in range(i):
                kj = k_loc[j * HG_SUB:(j + 1) * HG_SUB]
                if j < i - 1:
                    kj = kj * jnp.exp(ends[i - 1] - ends[j])
                parts.append(kj.astype(BF16))
            k_cols.append(jnp.concatenate(parts + [zero_blk] * (nsub - i), axis=0))
        q_aug = jnp.concatenate(q_cols, axis=1)
        k_aug = jnp.concatenate(k_cols, axis=1)
        att = _dot_nt(q_aug, k_aug)
        o = o + _dot(att.astype(BF16), v.astype(BF16))
        kp_ref[pl.ds(HG_SUB, TOK), :] = k
        bp_ref[pl.ds(HG_SUB, TOK), :] = bcum
        vp_ref[pl.ds(HG_SUB, TOK), :] = v
        o = o + jnp.sum(q * k, axis=1, keepdims=True) * v
        for d in range(1, HG_SUB):
            ks = kp_ref[pl.ds(HG_SUB - d, TOK), :]
            bs = bp_ref[pl.ds(HG_SUB - d, TOK), :]
            vs = vp_ref[pl.ds(HG_SUB - d, TOK), :]
            ok = in_sub >= d
            w = jnp.sum(q * ks * jnp.exp(bcum - bs), axis=1, keepdims=True)
            o = o + jnp.where(ok, w * vs, 0.0)
        b_last = bcum[TOK - 1:TOK, :]
        kdec = (k * jnp.exp(b_last - bcum)).astype(BF16)
        st_ref[h] = st * jnp.exp(b_last) + _dot(v.T.astype(BF16), kdec)
        outs.append(_rms(o, nw_ref[...]) * _silu(g_ref[:, sl]))
    y_ref[...] = jnp.concatenate(outs, axis=1).astype(BF16)


HG_INPUTS = 7


SSD_BLOCKED = 3
HG_BLOCKED = 4


def _recurrent_kernel(*refs, bsz):
    n_in = SSD_INPUTS + HG_INPUTS
    ssd_out, hg_out = refs[n_in], refs[n_in + 1]
    scratch = refs[n_in + 2:]
    for b in range(bsz):
        ssd_in = [r.at[b] for r in refs[:SSD_BLOCKED]] + list(refs[SSD_BLOCKED:SSD_INPUTS])
        hg_in = ([r.at[b] for r in refs[SSD_INPUTS:SSD_INPUTS + HG_BLOCKED]]
                 + list(refs[SSD_INPUTS + HG_BLOCKED:n_in]))
        _ssd_kernel(*ssd_in, ssd_out.at[b], *(r.at[b] for r in scratch[:2]))
        _hgrn2_kernel(*hg_in, hg_out.at[b], *(r.at[b] for r in scratch[2:]))


def _recurrent_mixers(proj, cw, cb, dtb, alog, dskip, ssd_nw, e8, lb, hg_nw, tri, bsz, s):
    nt = s // TOK
    proj = proj.reshape(bsz, s, proj.shape[-1])

    def col(c0, w):
        return pl.BlockSpec((bsz, TOK, w), lambda i: (0, i, c0 // w))

    def full(shape):
        return pl.BlockSpec(shape, lambda i: (0,) * len(shape))

    def per_batch(shape):
        return pltpu.VMEM((bsz,) + shape, F32)

    pad = per_batch((TOK + HG_SUB, HG_KEY_DIM))
    ssd_specs = [col(COL_Z, SSM_INNER), col(COL_XBC, SSM_CONV_DIM), col(COL_DT, LANES),
                 full((SSM_CONV, SSM_CONV_DIM)), full((1, SSM_CONV_DIM)), full((1, LANES)), full((1, LANES)),
                 full((1, SSM_INNER)), full((1, SSM_INNER)), full((LANES, SSM_INNER)), full((TOK, TOK))]
    hg_specs = [col(COL_HQ, HG_WIDTH), col(COL_HF, HG_WIDTH), col(COL_HI, HG_WIDTH), col(COL_HG, HG_WIDTH),
                full((1, HG_WIDTH)), full((1, HG_VAL_DIM)), full((TOK, TOK))]
    assert len(ssd_specs) == SSD_INPUTS and len(hg_specs) == HG_INPUTS
    y_b, y_c = pl.pallas_call(
        functools.partial(_recurrent_kernel, bsz=bsz),
        grid=(nt,),
        in_specs=ssd_specs + hg_specs,
        out_specs=[pl.BlockSpec((bsz, TOK, SSM_INNER), lambda i: (0, i, 0)),
                   pl.BlockSpec((bsz, TOK, HG_WIDTH), lambda i: (0, i, 0))],
        out_shape=[jax.ShapeDtypeStruct((bsz, s, SSM_INNER), BF16),
                   jax.ShapeDtypeStruct((bsz, s, HG_WIDTH), BF16)],
        scratch_shapes=[per_batch((TOK + SUBLANES, SSM_CONV_DIM)), per_batch((LANES, SSM_INNER)),
                        per_batch((HG_HEADS, HG_VAL_DIM, HG_KEY_DIM)), pad, pad, pad],
        compiler_params=_cparams(("arbitrary",)),
        name="ssd_hgrn2",
    )(proj, proj, proj, cw, cb, dtb, alog, dskip, ssd_nw, e8, tri, proj, proj, proj, proj, lb, hg_nw, tri)
    return y_b.reshape(bsz * s, SSM_INNER), y_c.reshape(bsz * s, HG_WIDTH)


def _merge_kernel(ga_ref, gb_ref, gc_ref, oc_ref, os_ref, ow_ref, yb_ref, yc_ref, x_ref,
                  wa_ref, wb_ref, wc_ref, wo_ref, o_ref):
    nsa = oc_ref[...].astype(F32) + os_ref[...].astype(F32) + ow_ref[...].astype(F32)
    ya = _dot(nsa.astype(BF16), wa_ref[...])
    yb = _dot(yb_ref[...], wb_ref[...])
    yc = _dot(yc_ref[...], wc_ref[...])
    merged = (jax.nn.sigmoid(ga_ref[...].astype(F32)) * ya + jax.nn.sigmoid(gb_ref[...].astype(F32)) * yb
              + jax.nn.sigmoid(gc_ref[...].astype(F32)) * yc)
    o_ref[...] = x_ref[...] + _dot(merged.astype(BF16), wo_ref[...])


def _merge(gates, oc, os_, ow, yb, yc, x2, wa, wb, wc, wo):
    t = x2.shape[0]
    tm = min(512, t)

    def col(c0, w):
        return pl.BlockSpec((tm, w), lambda i: (i, c0 // w))

    def full(shape):
        return pl.BlockSpec(shape, lambda i: (0,) * len(shape))

    row512 = pl.BlockSpec((tm, 512), lambda i: (i, 0))
    return pl.pallas_call(
        _merge_kernel,
        grid=(t // tm,),
        in_specs=[col(COL_GA, D_MODEL), col(COL_GB, D_MODEL), col(COL_GC, D_MODEL),
                  row512, row512, row512, row512, row512,
                  pl.BlockSpec((tm, D_MODEL), lambda i: (i, 0)),
                  full((NSA_Q_WIDTH, D_MODEL)), full((SSM_INNER, D_MODEL)),
                  full((HG_WIDTH, D_MODEL)), full((D_MODEL, D_MODEL))],
        out_specs=pl.BlockSpec((tm, D_MODEL), lambda i: (i, 0)),
        out_shape=jax.ShapeDtypeStruct((t, D_MODEL), F32),
        compiler_params=_cparams(("parallel",)),
        name="merge",
    )(gates, gates, gates, oc, os_, ow, yb, yc, x2, wa, wb, wc, wo)


def _xa_mem_kernel(mem_ref, nm_ref, wk_ref, wv_ref, kn_ref, k_ref, v_ref):
    m = _rms(mem_ref[0], nm_ref[...]).astype(BF16)
    k = _dot(m, wk_ref[...])
    ks = [_rms(k[:, h * XA_HEAD_DIM:(h + 1) * XA_HEAD_DIM], kn_ref[...]) for h in range(XA_HEADS)]
    k_ref[0] = jnp.concatenate(ks, axis=1).astype(BF16)
    v_ref[0] = _dot(m, wv_ref[...]).astype(BF16)


def _xa_mem(mem, nm, wk, wv, kn):
    bsz, ml, _ = mem.shape

    def full(shape):
        return pl.BlockSpec(shape, lambda b: (0,) * len(shape))

    return pl.pallas_call(
        _xa_mem_kernel,
        grid=(bsz,),
        in_specs=[pl.BlockSpec((1, ml, D_MODEL), lambda b: (b, 0, 0)), full((1, D_MODEL)),
                  full((D_MODEL, XA_WIDTH)), full((D_MODEL, XA_WIDTH)), full((1, XA_HEAD_DIM))],
        out_specs=[pl.BlockSpec((1, ml, XA_WIDTH), lambda b: (b, 0, 0)),
                   pl.BlockSpec((1, ml, XA_WIDTH), lambda b: (b, 0, 0))],
        out_shape=[jax.ShapeDtypeStruct((bsz, ml, XA_WIDTH), BF16),
                   jax.ShapeDtypeStruct((bsz, ml, XA_WIDTH), BF16)],
        compiler_params=_cparams(("parallel",)),
        name="xa_mem",
    )(mem, nm, wk, wv, kn)


def _xa_kernel(x_ref, nx_ref, wq_ref, qn_ref, k_ref, v_ref, wo_ref, o_ref):
    x = x_ref[...]
    h = _rms(x, nx_ref[...]).astype(BF16)
    q = _dot(h, wq_ref[...])
    outs = []
    for hd in range(XA_HEADS):
        sl = slice(hd * XA_HEAD_DIM, (hd + 1) * XA_HEAD_DIM)
        qh = _rms(q[:, sl], qn_ref[...]).astype(BF16)
        sc = _dot_nt(qh, k_ref[0, :, sl]) * (XA_HEAD_DIM ** -0.5)
        sc = sc - jnp.max(sc, axis=-1, keepdims=True)
        p = jnp.exp(sc)
        p = p / jnp.sum(p, axis=-1, keepdims=True)
        outs.append(_dot(p.astype(BF16), v_ref[0, :, sl]))
    o = jnp.concatenate(outs, axis=1).astype(BF16)
    o_ref[...] = x + _dot(o, wo_ref[...])


def _xa(x2, nx, wq, qn, k, v, wo, s):
    t = x2.shape[0]
    tm = min(512, s)
    ml = k.shape[1]
    per_b = s // tm

    def full(shape):
        return pl.BlockSpec(shape, lambda i: (0,) * len(shape))

    return pl.pallas_call(
        _xa_kernel,
        grid=(t // tm,),
        in_specs=[pl.BlockSpec((tm, D_MODEL), lambda i: (i, 0)), full((1, D_MODEL)), full((D_MODEL, XA_WIDTH)),
                  full((1, XA_HEAD_DIM)),
                  pl.BlockSpec((1, ml, XA_WIDTH), lambda i: (i // per_b, 0, 0)),
                  pl.BlockSpec((1, ml, XA_WIDTH), lambda i: (i // per_b, 0, 0)),
                  full((XA_WIDTH, D_MODEL))],
        out_specs=pl.BlockSpec((tm, D_MODEL), lambda i: (i, 0)),
        out_shape=jax.ShapeDtypeStruct((t, D_MODEL), F32),
        compiler_params=_cparams(("parallel",)),
        name="cross_attn",
    )(x2, nx, wq, qn, k, v, wo)


def _ffn_kernel(x_ref, nw_ref, wg_ref, wu_ref, wd_ref, o_ref, h_ref, acc_ref):
    j = pl.program_id(1)

    @pl.when(j == 0)
    def _():
        h_ref[...] = _rms(x_ref[...], nw_ref[...]).astype(BF16)
        acc_ref[...] = jnp.zeros_like(acc_ref)

    h = h_ref[...]
    a = _silu(_dot(h, wg_ref[...])) * _dot(h, wu_ref[...])
    acc_ref[...] += _dot(a.astype(BF16), wd_ref[...])

    @pl.when(j == pl.num_programs(1) - 1)
    def _():
        o_ref[...] = x_ref[...] + acc_ref[...]


def _ffn(x2, nw, wg, wu, wd):
    t = x2.shape[0]
    tm = min(1024, t)
    th = FFN_HIDDEN // 2
    return pl.pallas_call(
        _ffn_kernel,
        grid=(t // tm, FFN_HIDDEN // th),
        in_specs=[pl.BlockSpec((tm, D_MODEL), lambda i, j: (i, 0)),
                  pl.BlockSpec((1, D_MODEL), lambda i, j: (0, 0)),
                  pl.BlockSpec((D_MODEL, th), lambda i, j: (0, j)),
                  pl.BlockSpec((D_MODEL, th), lambda i, j: (0, j)),
                  pl.BlockSpec((th, D_MODEL), lambda i, j: (j, 0))],
        out_specs=pl.BlockSpec((tm, D_MODEL), lambda i, j: (i, 0)),
        out_shape=jax.ShapeDtypeStruct((t, D_MODEL), F32),
        scratch_shapes=[pltpu.VMEM((tm, D_MODEL), BF16), pltpu.VMEM((tm, D_MODEL), F32)],
        compiler_params=_cparams(("parallel", "arbitrary")),
        name="ffn",
    )(x2, nw, wg, wu, wd)


def _pack_w_in(w):
    parts = {}
    off = 0
    names = ("q", "kc", "vc", "ks", "vs", "kw", "vw", "ng", "z", "xbc", "dt", "hq", "hf", "hi", "hg",
             "ga", "gb", "gc")
    for name, size in zip(names, SPLIT_SIZES):
        parts[name] = w[:, off:off + size]
        off += size

    def padded(a, width):
        return jnp.pad(a, ((0, 0), (0, width - a.shape[1])))

    cols = [parts["ga"], parts["gb"], parts["gc"], parts["q"], parts["z"], parts["hq"], parts["hf"],
            parts["hi"], parts["hg"], parts["xbc"], parts["kc"], parts["vc"], parts["ks"], parts["vs"],
            parts["kw"], parts["vw"], padded(parts["ng"], LANES), padded(parts["dt"], LANES)]
    packed = jnp.concatenate(cols, axis=1)
    return padded(packed, PROJ_WIDTH).astype(BF16)


def _rope_tables(pos):
    half = NSA_HEAD_DIM // 2
    inv = 1.0 / (ROPE_THETA ** (jnp.arange(half, dtype=F32) / half))
    ang = pos.astype(F32)[:, None] * inv[None, :]
    reps = LANES // half
    return jnp.tile(jnp.cos(ang), (1, reps)), jnp.tile(jnp.sin(ang), (1, reps))


def _block_diag2(w):
    z = jnp.zeros_like(w)
    return jnp.concatenate([jnp.concatenate([w, z], axis=-1), jnp.concatenate([z, w], axis=-1)], axis=-2)


def kernel(x, mem, norm_mix, w_in, nsa_q_norm, nsa_k_norm, nsa_cmp_pos_k, nsa_cmp_pos_v, nsa_cmp_w1_k, nsa_cmp_w2_k, nsa_cmp_w1_v, nsa_cmp_w2_v, w_nsa_o, ssm_conv_w, ssm_conv_b, ssm_dt_bias, ssm_a_log, ssm_d, ssm_norm, w_ssm_o, hg_lb_logits, hg_norm, w_hg_o, w_out, norm_xa, norm_mem, xa_w_q, xa_w_k, xa_w_v, xa_q_norm, xa_k_norm, xa_w_o, norm_ffn, ffn_w_gate, ffn_w_up, ffn_w_down):
    bsz, s, d = x.shape
    depth = w_in.shape[0]
    assert d == D_MODEL and PREP_TILES * TOK == SEL_TILE
    assert s % SEL_TILE == 0 and s % (SEL_Q * TOK) == 0 and s // TOK >= WIN_TILES
    assert s // NSA_SEL_BLOCK >= NSA_TOPK and (s // NSA_CMP_STRIDE) % (4 * SUBLANES) == 0
    t = bsz * s
    hd = NSA_HEAD_DIM

    cos_t, sin_t = _rope_tables(jnp.arange(s))
    nbp = s // NSA_CMP_STRIDE
    cos_c, sin_c = _rope_tables(jnp.arange(nbp) * NSA_CMP_STRIDE + NSA_CMP_BLOCK - 1)
    tri = jnp.tril(jnp.ones((TOK, TOK), F32)).astype(BF16)
    rows = np.arange(LANES)[:, None]
    cols = np.arange(NSA_Q_WIDTH)[None, :]
    e8 = jnp.asarray(rows == cols // SSM_HEAD_DIM, F32).astype(BF16)
    li = np.arange(LANES)[:, None]
    lo = np.arange(LANES)[None, :]
    head_sum = jnp.asarray(li // hd == lo // hd, F32).astype(BF16)
    rot_half = jnp.asarray(np.where(lo % hd < hd // 2, -1.0 * (li == lo + hd // 2), 1.0 * (li == lo - hd // 2)),
                           F32).astype(BF16)

    lb_sm = jax.nn.softmax(hg_lb_logits.astype(F32), axis=0)
    lb_all = jnp.cumsum(lb_sm, axis=0) - lb_sm[0:1]

    def pad_lanes(v):
        return jnp.pad(v, (0, LANES - v.shape[0]))[None, :]

    x2 = x.reshape(t, d)
    for l in range(depth):
        gates, proj = _in_proj(x2, norm_mix[l][None, :], _pack_w_in(w_in[l]))

        qn2 = jnp.tile(nsa_q_norm[l], 2)[None, :]
        kn2 = jnp.tile(nsa_k_norm[l], 2)[None, :]
        qp, ksn, kwn, vst, vwt, gt = _nsa_prep(proj, cos_t, sin_t, qn2, kn2, head_sum, rot_half, bsz, s)
        w1k = _block_diag2(nsa_cmp_w1_k[l].reshape(NSA_CMP_BLOCK, hd, hd)).astype(BF16)
        w1v = _block_diag2(nsa_cmp_w1_v[l].reshape(NSA_CMP_BLOCK, hd, hd)).astype(BF16)
        kc, vct = _compress(proj, jnp.tile(nsa_cmp_pos_k[l], (1, 2)), jnp.tile(nsa_cmp_pos_v[l], (1, 2)),
                            w1k, _block_diag2(nsa_cmp_w2_k[l]).astype(BF16),
                            w1v, _block_diag2(nsa_cmp_w2_v[l]).astype(BF16), kn2, cos_c, sin_c, bsz, s)
        o_c, sel_bias = _cmp_topk(qp, kc, vct, gt, bsz, s)
        tile_list, tile_count = _active_tiles(sel_bias, bsz, s)
        o_s = _selected(qp, ksn, vst, sel_bias, gt, tile_list, tile_count, bsz, s)
        o_w = _window(qp, kwn, vwt, gt, bsz, s)

        y_b, y_c = _recurrent_mixers(
            proj, ssm_conv_w[l], ssm_conv_b[l][None, :], pad_lanes(ssm_dt_bias[l]), pad_lanes(ssm_a_log[l]),
            jnp.repeat(ssm_d[l], SSM_HEAD_DIM)[None, :], ssm_norm[l][None, :], e8,
            lb_all[l][None, :], hg_norm[l][None, :], tri, bsz, s)

        x2 = _merge(gates, o_c.reshape(t, -1), o_s.reshape(t, -1), o_w.reshape(t, -1), y_b, y_c, x2,
                    w_nsa_o[l].astype(BF16), w_ssm_o[l].astype(BF16), w_hg_o[l].astype(BF16),
                    w_out[l].astype(BF16))

        k_m, v_m = _xa_mem(mem, norm_mem[l][None, :], xa_w_k[l].astype(BF16), xa_w_v[l].astype(BF16),
                           xa_k_norm[l][None, :])
        x2 = _xa(x2, norm_xa[l][None, :], xa_w_q[l].astype(BF16), xa_q_norm[l][None, :], k_m, v_m,
                 xa_w_o[l].astype(BF16), s)
        x2 = _ffn(x2, norm_ffn[l][None, :], ffn_w_gate[l].astype(BF16), ffn_w_up[l].astype(BF16),
                  ffn_w_down[l].astype(BF16))
    return x2.reshape(bsz, s, d)
```

```python
import functools
import math

import numpy as np
import jax
import jax.numpy as jnp
from jax import lax
from jax.experimental import pallas as pl
from jax.experimental.pallas import tpu as pltpu

F32 = jnp.float32
BF16 = jnp.bfloat16

D_MODEL = 1024
NORM_EPS = 1e-6
ROPE_THETA = 10000.0
NEG_INF = -1e30

NSA_HEADS = 8
NSA_KV_GROUPS = 2
NSA_REP = NSA_HEADS // NSA_KV_GROUPS
NSA_HEAD_DIM = 64
NSA_CMP_BLOCK = 32
NSA_CMP_STRIDE = 16
NSA_SEL_BLOCK = 64
NSA_TOPK = 16
N_FORCED = 3
NSA_WINDOW = 512
NSA_Q_WIDTH = NSA_HEADS * NSA_HEAD_DIM
NSA_KV_WIDTH = NSA_KV_GROUPS * NSA_HEAD_DIM

SSM_INNER = 512
SSM_HEAD_DIM = 64
SSM_HEADS = SSM_INNER // SSM_HEAD_DIM
SSM_GROUPS = 2
SSM_STATE = 64
SSM_CONV = 4
SSM_CONV_DIM = SSM_INNER + 2 * SSM_GROUPS * SSM_STATE

HG_HEADS = 4
HG_KEY_DIM = 128
HG_VAL_DIM = 128
HG_WIDTH = HG_HEADS * HG_KEY_DIM
HG_SUB = 16

XA_HEADS = 4
XA_HEAD_DIM = 128
XA_WIDTH = XA_HEADS * XA_HEAD_DIM

FFN_HIDDEN = -(-(8 * D_MODEL) // (3 * 256)) * 256

SPLIT_SIZES = (
    NSA_Q_WIDTH, NSA_KV_WIDTH, NSA_KV_WIDTH, NSA_KV_WIDTH, NSA_KV_WIDTH, NSA_KV_WIDTH, NSA_KV_WIDTH,
    3 * NSA_HEADS, SSM_INNER, SSM_CONV_DIM, SSM_HEADS, HG_WIDTH, HG_WIDTH, HG_HEADS * HG_VAL_DIM,
    HG_HEADS * HG_VAL_DIM, D_MODEL, D_MODEL, D_MODEL,
)

LANES = 128
SUBLANES = 8
TOK = 128
NSA_BRANCHES = 3
GATE_ROWS = -(-NSA_BRANCHES * NSA_HEADS // SUBLANES) * SUBLANES
CMP_Q = 2
PREP_TILES = 4
SEL_Q = 4
SEL_TILE = 512
SEL_V_ROWS = NSA_HEAD_DIM + 16
WIN_TILES = NSA_WINDOW // TOK + 1
LOG2_E = math.log2(math.e)
MASK_FLOOR = -1e20
VMEM_LIMIT = 56 * 1024 * 1024

PROJ_WIDTH = 8192
GATE_WIDTH = 3 * D_MODEL
COL_GA, COL_GB, COL_GC = 0, 1024, 2048
COL_Q = 0
COL_Z = 512
COL_HQ, COL_HF, COL_HI, COL_HG = 1024, 1536, 2048, 2560
COL_XBC = 3072
COL_KC, COL_VC, COL_KS, COL_VS, COL_KW, COL_VW = 3840, 3968, 4096, 4224, 4352, 4480
COL_NG = 4608
COL_DT = 4736


def _cparams(sem):
    return pltpu.CompilerParams(dimension_semantics=sem, vmem_limit_bytes=VMEM_LIMIT)


def _dot(a, b):
    return jnp.dot(a, b, preferred_element_type=F32)


def _dot_nt(a, b):
    return lax.dot_general(a, b, (((1,), (1,)), ((), ())), preferred_element_type=F32)


def _split3(a):
    hi = a.astype(BF16)
    r1 = a - hi.astype(F32)
    mid = r1.astype(BF16)
    lo = (r1 - mid.astype(F32)).astype(BF16)
    return hi, mid, lo


def _dot_sel_r(a, sel):
    hi, mid, lo = _split3(a)
    return _dot(hi, sel) + _dot(mid, sel) + _dot(lo, sel)


def _dot_sel_l(sel, a):
    hi, mid, lo = _split3(a)
    return _dot(sel, hi) + _dot(sel, mid) + _dot(sel, lo)


def _silu(x):
    return x * jax.nn.sigmoid(x)


def _rms(x, w):
    return x * lax.rsqrt(jnp.mean(x * x, axis=-1, keepdims=True) + NORM_EPS) * w


def _inproj_kernel(x_ref, nw_ref, w_ref, og_ref, of_ref, h_ref, *, n_gate):
    j = pl.program_id(1)

    @pl.when(j == 0)
    def _():
        h_ref[...] = _rms(x_ref[...], nw_ref[...]).astype(BF16)

    @pl.when(j < n_gate)
    def _():
        og_ref[...] = _dot(h_ref[...], w_ref[...]).astype(BF16)

    @pl.when(j >= n_gate)
    def _():
        of_ref[...] = _dot(h_ref[...], w_ref[...])


def _in_proj(x2, norm_w, w_packed):
    t = x2.shape[0]
    tm = min(2048, t)
    tn = 1024
    n_gate = GATE_WIDTH // tn
    return pl.pallas_call(
        functools.partial(_inproj_kernel, n_gate=n_gate),
        grid=(t // tm, PROJ_WIDTH // tn),
        in_specs=[
            pl.BlockSpec((tm, D_MODEL), lambda i, j: (i, 0)),
            pl.BlockSpec((1, D_MODEL), lambda i, j: (0, 0)),
            pl.BlockSpec((D_MODEL, tn), lambda i, j: (0, j)),
        ],
        out_specs=[pl.BlockSpec((tm, tn), lambda i, j: (i, jnp.minimum(j, n_gate - 1))),
                   pl.BlockSpec((tm, tn), lambda i, j: (i, jnp.maximum(j - n_gate, 0)))],
        out_shape=[jax.ShapeDtypeStruct((t, GATE_WIDTH), BF16),
                   jax.ShapeDtypeStruct((t, PROJ_WIDTH - GATE_WIDTH), F32)],
        scratch_shapes=[pltpu.VMEM((tm, D_MODEL), BF16)],
        compiler_params=_cparams(("parallel", "arbitrary")),
        name="in_proj",
    )(x2, norm_w, w_packed)


def _norm_rope_pair(x, w, cos, sin):
    lane = lax.broadcasted_iota(jnp.int32, x.shape, 1)
    lo = lane < NSA_HEAD_DIM
    sq = x * x
    s_lo = jnp.sum(jnp.where(lo, sq, 0.0), axis=1, keepdims=True)
    s_hi = jnp.sum(jnp.where(lo, 0.0, sq), axis=1, keepdims=True)
    ms = jnp.where(lo, s_lo, s_hi) * (1.0 / NSA_HEAD_DIM)
    y = x * lax.rsqrt(ms + NORM_EPS) * w
    half = NSA_HEAD_DIM // 2
    first = (lane % NSA_HEAD_DIM) < half
    rot = jnp.where(first, -pltpu.roll(y, LANES - half, 1), pltpu.roll(y, half, 1))
    return y * cos + rot * sin


def _norm_rope_pair_mxu(x, w, cos, sin, head_sum, rot_half):
    ms = _dot_sel_r(x * x, head_sum) * (1.0 / NSA_HEAD_DIM)
    y = x * lax.rsqrt(ms + NORM_EPS) * w
    return y * cos + _dot_sel_r(y, rot_half) * sin


def _nsa_prep_kernel(q_ref, ks_ref, vs_ref, kw_ref, vw_ref, ng_ref, cos_ref, sin_ref, qn_ref, kn_ref,
                     hsum_ref, rot_ref, qp_ref, ksn_ref, kwn_ref, vst_ref, vwt_ref, gt_ref):
    norm_rope = functools.partial(_norm_rope_pair_mxu, head_sum=hsum_ref[...], rot_half=rot_ref[...])
    lane = lax.broadcasted_iota(jnp.int32, (TOK, LANES), 1)
    row = lax.broadcasted_iota(jnp.int32, (TOK, LANES), 0)
    scale = NSA_HEAD_DIM ** -0.5 * LOG2_E
    ones_rows = (lax.broadcasted_iota(jnp.int32, (SEL_V_ROWS - NSA_HEAD_DIM, TOK), 0) == 0).astype(BF16)
    for u in range(PREP_TILES):
        rows = slice(u * TOK, (u + 1) * TOK)
        cos = cos_ref[rows, :]
        sin = sin_ref[rows, :]
        for c in range(NSA_HEADS // 2):
            y = norm_rope(q_ref[rows, c * LANES:(c + 1) * LANES], qn_ref[...], cos, sin) * scale
            y_sw = pltpu.roll(y, NSA_HEAD_DIM, 1)
            for hh in range(2):
                h = 2 * c + hh
                g, r = divmod(h, NSA_REP)
                src = y if hh == g else y_sw
                keep = (lane // NSA_HEAD_DIM) == g
                qp_ref[0, g, rows, r * LANES:(r + 1) * LANES] = jnp.where(keep, src, 0.0).astype(BF16)
        ks_n = norm_rope(ks_ref[rows, :], kn_ref[...], cos, sin)
        tile = pl.program_id(1) * PREP_TILES + u
        blk = (tile * (TOK // NSA_SEL_BLOCK) + row // NSA_SEL_BLOCK) % NSA_HEAD_DIM
        onehot = ((lane % NSA_HEAD_DIM) == blk).astype(F32)
        for g in range(NSA_KV_GROUPS):
            ksn_ref[0, g, rows, :] = jnp.where((lane // NSA_HEAD_DIM) == g, ks_n, onehot).astype(BF16)
        kwn_ref[0, rows, :] = norm_rope(kw_ref[rows, :], kn_ref[...], cos, sin).astype(BF16)
        vs_t = vs_ref[rows, :].T.astype(BF16)
        vw_t = vw_ref[rows, :].T.astype(BF16)
        for g in range(NSA_KV_GROUPS):
            vst_ref[0, g, 0, 0:NSA_HEAD_DIM, rows] = vs_t[g * NSA_HEAD_DIM:(g + 1) * NSA_HEAD_DIM, :]
            vst_ref[0, g, 0, NSA_HEAD_DIM:SEL_V_ROWS, rows] = ones_rows
            vwt_ref[0, g, u] = vw_t[g * NSA_HEAD_DIM:(g + 1) * NSA_HEAD_DIM, :]
        gt_ref[0, u] = jax.nn.sigmoid(ng_ref[rows, :]).T[0:GATE_ROWS, :]


def _nsa_prep(proj, cos_t, sin_t, qn2, kn2, head_sum, rot_half, bsz, s):
    tq = PREP_TILES * TOK
    n_step = s // tq

    def col(c0, w):
        return pl.BlockSpec((tq, w), lambda b, i: (b * n_step + i, c0 // w))

    tab = pl.BlockSpec((tq, LANES), lambda b, i: (i, 0))
    vec = pl.BlockSpec((1, LANES), lambda b, i: (0, 0))
    sq = pl.BlockSpec((LANES, LANES), lambda b, i: (0, 0))
    return pl.pallas_call(
        _nsa_prep_kernel,
        grid=(bsz, n_step),
        in_specs=[col(COL_Q, NSA_Q_WIDTH), col(COL_KS, LANES), col(COL_VS, LANES), col(COL_KW, LANES),
                  col(COL_VW, LANES), col(COL_NG, LANES), tab, tab, vec, vec, sq, sq],
        out_specs=[
            pl.BlockSpec((1, NSA_KV_GROUPS, tq, NSA_REP * LANES), lambda b, i: (b, 0, i, 0)),
            pl.BlockSpec((1, NSA_KV_GROUPS, tq, LANES), lambda b, i: (b, 0, i, 0)),
            pl.BlockSpec((1, tq, LANES), lambda b, i: (b, i, 0)),
            pl.BlockSpec((1, NSA_KV_GROUPS, 1, SEL_V_ROWS, SEL_TILE), lambda b, i: (b, 0, i, 0, 0)),
            pl.BlockSpec((1, NSA_KV_GROUPS, PREP_TILES, NSA_HEAD_DIM, TOK), lambda b, i: (b, 0, i, 0, 0)),
            pl.BlockSpec((1, PREP_TILES, GATE_ROWS, TOK), lambda b, i: (b, i, 0, 0)),
        ],
        out_shape=[
            jax.ShapeDtypeStruct((bsz, NSA_KV_GROUPS, s, NSA_REP * LANES), BF16),
            jax.ShapeDtypeStruct((bsz, NSA_KV_GROUPS, s, LANES), BF16),
            jax.ShapeDtypeStruct((bsz, s, LANES), BF16),
            jax.ShapeDtypeStruct((bsz, NSA_KV_GROUPS, s // SEL_TILE, SEL_V_ROWS, SEL_TILE), BF16),
            jax.ShapeDtypeStruct((bsz, NSA_KV_GROUPS, s // TOK, NSA_HEAD_DIM, TOK), BF16),
            jax.ShapeDtypeStruct((bsz, s // TOK, GATE_ROWS, TOK), F32),
        ],
        compiler_params=_cparams(("parallel", "parallel")),
        name="nsa_prep",
    )(proj, proj, proj, proj, proj, proj, cos_t, sin_t, qn2, kn2, head_sum, rot_half)


def _compress_kernel(kc_ref, vc_ref, posk_ref, posv_ref, w1k_ref, w2k_ref, w1v_ref, w2v_ref, kn_ref,
                     cos_ref, sin_ref, kco_ref, vct_ref, sh_ref, *, nbp):
    half_blk = NSA_CMP_BLOCK // 2

    def compress(x_ref, pos_ref, w1_ref, w2_ref):
        acc_a = jnp.zeros((nbp, LANES), F32)
        acc_b = jnp.zeros((nbp, LANES), F32)
        for l in range(half_blk):
            xl = x_ref[pl.ds(l, nbp, stride=NSA_CMP_STRIDE), :]
            acc_a = acc_a + _dot((xl + pos_ref[l:l + 1, :]).astype(BF16), w1_ref[l])
            acc_b = acc_b + _dot((xl + pos_ref[half_blk + l:half_blk + l + 1, :]).astype(BF16),
                                 w1_ref[half_blk + l])
        sh_ref[pl.ds(0, nbp), :] = acc_b
        sh_ref[pl.ds(nbp, SUBLANES), :] = jnp.zeros((SUBLANES, LANES), F32)
        hid = acc_a + sh_ref[pl.ds(1, nbp), :]
        return _dot(_silu(hid).astype(BF16), w2_ref[...])

    kc = compress(kc_ref, posk_ref, w1k_ref, w2k_ref)
    kco_ref[0] = _norm_rope_pair(kc, kn_ref[...], cos_ref[...], sin_ref[...]).astype(BF16)
    vc = compress(vc_ref, posv_ref, w1v_ref, w2v_ref)
    vct_ref[0] = vc.T.astype(BF16)


def _compress(proj, posk, posv, w1k, w2k, w1v, w2v, kn2, cos_c, sin_c, bsz, s):
    nbp = s // NSA_CMP_STRIDE

    def full(shape):
        return pl.BlockSpec(shape, lambda b: (0,) * len(shape))

    return pl.pallas_call(
        functools.partial(_compress_kernel, nbp=nbp),
        grid=(bsz,),
        in_specs=[
            pl.BlockSpec((s, LANES), lambda b: (b, COL_KC // LANES)),
            pl.BlockSpec((s, LANES), lambda b: (b, COL_VC // LANES)),
            full((NSA_CMP_BLOCK, LANES)), full((NSA_CMP_BLOCK, LANES)),
            full((NSA_CMP_BLOCK, LANES, LANES)), full((LANES, LANES)),
            full((NSA_CMP_BLOCK, LANES, LANES)), full((LANES, LANES)),
            full((1, LANES)), full((nbp, LANES)), full((nbp, LANES)),
        ],
        out_specs=[
            pl.BlockSpec((1, nbp, LANES), lambda b: (b, 0, 0)),
            pl.BlockSpec((1, LANES, nbp), lambda b: (b, 0, 0)),
        ],
        out_shape=[
            jax.ShapeDtypeStruct((bsz, nbp, LANES), BF16),
            jax.ShapeDtypeStruct((bsz, LANES, nbp), BF16),
        ],
        scratch_shapes=[pltpu.VMEM((nbp + SUBLANES, LANES), F32)],
        compiler_params=_cparams(("parallel",)),
        name="nsa_compress",
    )(proj, proj, posk, posv, w1k, w2k, w1v, w2v, kn2, cos_c, sin_c)


def _q_rows(qp):
    return jnp.concatenate([qp[:, r * LANES:(r + 1) * LANES] for r in range(NSA_REP)], axis=0)


def _gate_row(gt, g, branch):
    rows = [(g * NSA_REP + r) * NSA_BRANCHES + branch for r in range(NSA_REP)]
    return jnp.concatenate([gt[i:i + 1, :] for i in rows], axis=1)


def _heads_to_token_major(acc_g):
    outs = []
    for p in range(NSA_REP // 2):
        blk = jnp.concatenate([acc_g[:, (2 * p) * TOK:(2 * p + 1) * TOK],
                               acc_g[:, (2 * p + 1) * TOK:(2 * p + 2) * TOK]], axis=0)
        outs.append(blk.T)
    return jnp.concatenate(outs, axis=1)


def _cmp_topk_kernel(qp_ref, kc_ref, vct_ref, gt_ref, oc_ref, mask_ref, ps_ref, *, nbp, nsel):
    qi = pl.program_id(1)
    ncol = NSA_REP * TOK
    per = NSA_SEL_BLOCK // NSA_CMP_STRIDE

    def attend(nrows):
        nblk = nrows // per
        units = [(h, g) for h in range(CMP_Q) for g in range(NSA_KV_GROUPS)]
        j_io = lax.broadcasted_iota(jnp.int32, (nblk, TOK), 0)
        rel = (lax.broadcasted_iota(jnp.int32, (nrows, ncol), 0) * NSA_CMP_STRIDE + (NSA_CMP_BLOCK - 1)
               - lax.broadcasted_iota(jnp.int32, (nrows, ncol), 1) % TOK)
        kc = kc_ref[0, 0:nrows, :]
        vct = vct_ref[0, :, 0:nrows]
        forced, valid, allowed = [], [], []
        for h in range(CMP_Q):
            t_sel = (qi * CMP_Q + h) * TOK + lax.broadcasted_iota(jnp.int32, (nblk, TOK), 1)
            cur = t_sel // NSA_SEL_BLOCK
            forced.append((j_io == 0) | (j_io == cur) | (j_io == cur - 1))
            valid.append(j_io * NSA_SEL_BLOCK <= t_sel)
            allowed.append(rel <= (qi * CMP_Q + h) * TOK)
        for h, g in units:
            ps_ref[h, g, pl.ds(0, SUBLANES), :] = jnp.zeros((SUBLANES, TOK), F32)
        scores = [_dot_nt(kc, _q_rows(qp_ref[0, g, h * TOK:(h + 1) * TOK, :])) for h, g in units]
        outs = {}
        imps = []
        for u, (h, g) in enumerate(units):
            s_t = jnp.where(allowed[h], scores[u], NEG_INF)
            m = jnp.max(s_t, axis=0, keepdims=True)
            p = jnp.exp2(s_t - jnp.maximum(m, MASK_FLOOR))
            l = jnp.sum(p, axis=0, keepdims=True)
            inv = jnp.where(l > 0.0, 1.0 / l, 0.0)
            o_t = _dot(vct, p.astype(BF16)) * inv
            gate = _gate_row(gt_ref[0, h], g, 0)
            outs[h, g] = _heads_to_token_major(o_t[g * NSA_HEAD_DIM:(g + 1) * NSA_HEAD_DIM, :] * gate)
            pn = p * inv
            psum = pn[:, 0:TOK]
            for r in range(1, NSA_REP):
                psum = psum + pn[:, r * TOK:(r + 1) * TOK]
            ps_ref[h, g, pl.ds(SUBLANES, nrows), :] = psum
            imp = (ps_ref[h, g, pl.ds(SUBLANES - 1, nblk, stride=per), :]
                   + ps_ref[h, g, pl.ds(SUBLANES + per - 1, nblk, stride=per), :])
            for c in range(per - 1):
                imp = imp + 2.0 * ps_ref[h, g, pl.ds(SUBLANES + c, nblk, stride=per), :]
            imps.append(jnp.where(forced[h], -jnp.inf, jnp.where(valid[h], imp, NEG_INF)))
        for h in range(CMP_Q):
            oc_ref[0, h * TOK:(h + 1) * TOK, :] = jnp.concatenate(
                [outs[h, g] for g in range(NSA_KV_GROUPS)], axis=1)

        def pick(_, carry):
            nxt = []
            for imp_c in carry:
                mx = jnp.max(imp_c, axis=0, keepdims=True)
                idx = jnp.min(jnp.where(imp_c == mx, j_io, nblk), axis=0, keepdims=True)
                nxt.append(jnp.where(j_io == idx, -jnp.inf, imp_c))
            return tuple(nxt)

        picked = lax.fori_loop(0, min(NSA_TOPK, nsel) - N_FORCED, pick, tuple(imps))
        for u, (h, g) in enumerate(units):
            mask_ref[0, g, h, 0:nblk, :] = jnp.where(valid[h] & (picked[u] == -jnp.inf), 0.0, NEG_INF)
            if nblk < nsel:
                mask_ref[0, g, h, nblk:nsel, :] = jnp.full((nsel - nblk, TOK), NEG_INF, F32)

    n_vis = ((qi + 1) * CMP_Q * TOK - NSA_CMP_BLOCK) // NSA_CMP_STRIDE + 1
    quarter = nbp // 4
    for k in range(1, 5):
        @pl.when((n_vis > (k - 1) * quarter) & (n_vis <= k * quarter))
        def _(k=k):
            attend(k * quarter)


def _cmp_topk(qp, kc, vct, gt, bsz, s):
    nt = s // TOK
    nbp = s // NSA_CMP_STRIDE
    nsel = s // NSA_SEL_BLOCK
    return pl.pallas_call(
        functools.partial(_cmp_topk_kernel, nbp=nbp, nsel=nsel),
        grid=(bsz, nt // CMP_Q),
        in_specs=[
            pl.BlockSpec((1, NSA_KV_GROUPS, CMP_Q * TOK, NSA_REP * LANES), lambda b, i: (b, 0, i, 0)),
            pl.BlockSpec((1, nbp, LANES), lambda b, i: (b, 0, 0)),
            pl.BlockSpec((1, LANES, nbp), lambda b, i: (b, 0, 0)),
            pl.BlockSpec((1, CMP_Q, GATE_ROWS, TOK), lambda b, i: (b, i, 0, 0)),
        ],
        out_specs=[
            pl.BlockSpec((1, CMP_Q * TOK, NSA_Q_WIDTH), lambda b, i: (b, i, 0)),
            pl.BlockSpec((1, NSA_KV_GROUPS, CMP_Q, nsel, TOK), lambda b, i: (b, 0, i, 0, 0)),
        ],
        out_shape=[
            jax.ShapeDtypeStruct((bsz, s, NSA_Q_WIDTH), F32),
            jax.ShapeDtypeStruct((bsz, NSA_KV_GROUPS, nt, nsel, TOK), F32),
        ],
        scratch_shapes=[pltpu.VMEM((CMP_Q, NSA_KV_GROUPS, nbp + SUBLANES, TOK), F32)],
        compiler_params=_cparams(("parallel", "parallel")),
        name="nsa_cmp_topk",
    )(qp, kc, vct, gt)


def _window_kernel(qp_ref, k_ref, vt_ref, gt_ref, o_ref):
    qi = pl.program_id(1)
    ncol = NSA_REP * TOK
    wk = WIN_TILES * TOK

    def run(first_tile, interior):
        kwin = k_ref[0, pl.ds(pl.multiple_of(first_tile * TOK, TOK), wk), :]
        scores = [_dot_nt(kwin, _q_rows(qp_ref[0, g])) for g in range(NSA_KV_GROUPS)]
        probs = []
        scales = []
        for g in range(NSA_KV_GROUPS):
            s_t = scores[g]
            if interior:
                row = lax.broadcasted_iota(jnp.int32, (TOK, ncol), 0)
                col = lax.broadcasted_iota(jnp.int32, (TOK, ncol), 1) % TOK
                oldest = jnp.where(row > col, s_t[:TOK], NEG_INF)
                newest = jnp.where(row <= col, s_t[wk - TOK:], NEG_INF)
                s_t = jnp.concatenate([oldest, s_t[TOK:wk - TOK], newest], axis=0)
            else:
                key = first_tile * TOK + lax.broadcasted_iota(jnp.int32, (wk, ncol), 0)
                t_io = qi * TOK + lax.broadcasted_iota(jnp.int32, (wk, ncol), 1) % TOK
                s_t = jnp.where((key <= t_io) & (t_io - key < NSA_WINDOW), s_t, NEG_INF)
            m = jnp.max(s_t, axis=0, keepdims=True)
            p = jnp.exp2(s_t - m)
            l = jnp.sum(p, axis=0, keepdims=True)
            probs.append(p.astype(BF16))
            scales.append(_gate_row(gt_ref[0, 0], g, 2) / l)
        outs = []
        for g in range(NSA_KV_GROUPS):
            vt = jnp.concatenate([vt_ref[0, g, first_tile + k] for k in range(WIN_TILES)], axis=1)
            outs.append(_heads_to_token_major(_dot(vt, probs[g]) * scales[g]))
        o_ref[0] = jnp.concatenate(outs, axis=1)

    @pl.when(qi >= WIN_TILES - 1)
    def _():
        run(qi - (WIN_TILES - 1), True)

    @pl.when(qi < WIN_TILES - 1)
    def _():
        run(0, False)


def _window(qp, k, vt, gt, bsz, s):
    nt = s // TOK
    return pl.pallas_call(
        _window_kernel,
        grid=(bsz, nt),
        in_specs=[
            pl.BlockSpec((1, NSA_KV_GROUPS, TOK, NSA_REP * LANES), lambda b, i: (b, 0, i, 0)),
            pl.BlockSpec((1, s, LANES), lambda b, i: (b, 0, 0)),
            pl.BlockSpec((1, NSA_KV_GROUPS, nt, NSA_HEAD_DIM, TOK), lambda b, i: (b, 0, 0, 0, 0)),
            pl.BlockSpec((1, 1, GATE_ROWS, TOK), lambda b, i: (b, i, 0, 0)),
        ],
        out_specs=pl.BlockSpec((1, TOK, NSA_Q_WIDTH), lambda b, i: (b, i, 0)),
        out_shape=jax.ShapeDtypeStruct((bsz, s, NSA_Q_WIDTH), F32),
        compiler_params=_cparams(("parallel", "parallel")),
        name="nsa_window",
    )(qp, k, vt, gt)


def _selected_kernel(list_ref, cnt_ref, qp_ref, k_ref, vt_ref, bias_ref, gt_ref, o_ref, m_ref, acc_ref,
                     s_ref, p_ref, a_ref, qa_ref, *, nt, nsel):
    b = pl.program_id(0)
    qi = pl.program_id(1)
    ncol = NSA_REP * TOK
    blk_per_tile = SEL_TILE // NSA_SEL_BLOCK
    n_kt = nt // (SEL_TILE // TOK)
    n_q = nt // SEL_Q
    units = [(g, h) for g in range(NSA_KV_GROUPS) for h in range(SEL_Q)]
    m_ref[...] = jnp.full(m_ref.shape, NEG_INF, F32)
    acc_ref[...] = jnp.zeros(acc_ref.shape, F32)
    p_ref[...] = jnp.zeros(p_ref.shape, BF16)
    a_ref[...] = jnp.ones(a_ref.shape, F32)
    base = (b * n_q + qi) * n_kt
    count = cnt_ref[b * n_q + qi]
    n_diag = (qi * SEL_Q * TOK) // SEL_TILE

    lane = lax.broadcasted_iota(jnp.int32, (TOK, LANES), 1)
    n_var = qa_ref.shape[0]
    for g, h in units:
        bias_g = bias_ref[0, g, h]
        if nsel % LANES:
            bias_g = jnp.concatenate([bias_g, jnp.zeros((LANES - nsel % LANES, TOK), F32)], axis=0)
        own = (lane // NSA_HEAD_DIM) == g
        for v in range(n_var):
            src = bias_g[(v // 2) * LANES:(v // 2 + 1) * LANES, :].T
            if v % 2 != 1 - g:
                src = pltpu.roll(src, NSA_HEAD_DIM, 1)
            src = src.astype(BF16)
            for r in range(NSA_REP):
                q_r = qp_ref[0, g, h * TOK:(h + 1) * TOK, r * LANES:(r + 1) * LANES]
                qa_ref[v, g, h, r * TOK:(r + 1) * TOK, :] = jnp.where(own, q_r, src)

    def tile_at(i):
        return jnp.where(i < count, list_ref[base + jnp.minimum(i, n_kt - 1)], n_diag)

    def scores(j):
        v = (j * blk_per_tile) // NSA_HEAD_DIM
        for g in range(NSA_KV_GROUPS):
            kb = k_ref[0, g, pl.ds(pl.multiple_of(j * SEL_TILE, SEL_TILE), SEL_TILE), :]
            for h in range(SEL_Q):
                s_ref[g, h] = _dot_nt(kb, qa_ref[v, g, h])

    def values(j):
        for g, h in units:
            acc_ref[g, h] = acc_ref[g, h] * a_ref[g, h] + _dot(vt_ref[0, g, j], p_ref[g, h])

    def softmax(j, causal):
        for g, h in units:
            s_t = s_ref[g, h]
            if causal:
                t_io = ((qi * SEL_Q + h) * TOK
                        + lax.broadcasted_iota(jnp.int32, (SEL_TILE, ncol), 1) % TOK)
                key_pos = j * SEL_TILE + lax.broadcasted_iota(jnp.int32, (SEL_TILE, ncol), 0)
                s_t = jnp.where(key_pos <= t_io, s_t, NEG_INF)
            m_old = m_ref[g, h]
            m_new = jnp.maximum(m_old, jnp.max(s_t, axis=0, keepdims=True))
            alpha = jnp.exp2(m_old - m_new)
            p = jnp.exp2(s_t - jnp.maximum(m_new, MASK_FLOOR))
            m_ref[g, h] = m_new
            p_ref[g, h] = p.astype(BF16)
            a_ref[g, h] = alpha

    scores(tile_at(0))

    def body(i, c):
        values(tile_at(jnp.maximum(i - 1, 0)))
        softmax(tile_at(i), False)
        scores(tile_at(i + 1))
        return c

    lax.fori_loop(0, count, body, 0)
    values(tile_at(jnp.maximum(count - 1, 0)))
    softmax(n_diag, True)
    values(n_diag)
    for h in range(SEL_Q):
        outs = []
        for g in range(NSA_KV_GROUPS):
            acc = acc_ref[g, h]
            gate = _gate_row(gt_ref[0, h], g, 1)
            outs.append(_heads_to_token_major(acc[:NSA_HEAD_DIM] * (gate / acc[NSA_HEAD_DIM:NSA_HEAD_DIM + 1])))
        o_ref[0, h * TOK:(h + 1) * TOK, :] = jnp.concatenate(outs, axis=1)


def _selected(qp, k, vt, bias, gt, tile_list, tile_count, bsz, s):
    nt = s // TOK
    nsel = s // NSA_SEL_BLOCK
    ncol = NSA_REP * TOK
    unit = (NSA_KV_GROUPS, SEL_Q)
    grid_spec = pltpu.PrefetchScalarGridSpec(
        num_scalar_prefetch=2,
        grid=(bsz, nt // SEL_Q),
        in_specs=[
            pl.BlockSpec((1, NSA_KV_GROUPS, SEL_Q * TOK, NSA_REP * LANES), lambda b, i, tl, tc: (b, 0, i, 0)),
            pl.BlockSpec((1, NSA_KV_GROUPS, s, LANES), lambda b, i, tl, tc: (b, 0, 0, 0)),
            pl.BlockSpec((1, NSA_KV_GROUPS, s // SEL_TILE, SEL_V_ROWS, SEL_TILE),
                         lambda b, i, tl, tc: (b, 0, 0, 0, 0)),
            pl.BlockSpec((1, NSA_KV_GROUPS, SEL_Q, nsel, TOK), lambda b, i, tl, tc: (b, 0, i, 0, 0)),
            pl.BlockSpec((1, SEL_Q, GATE_ROWS, TOK), lambda b, i, tl, tc: (b, i, 0, 0)),
        ],
        out_specs=pl.BlockSpec((1, SEL_Q * TOK, NSA_Q_WIDTH), lambda b, i, tl, tc: (b, i, 0)),
        scratch_shapes=[pltpu.VMEM(unit + (1, ncol), F32),
                        pltpu.VMEM(unit + (SEL_V_ROWS, ncol), F32),
                        pltpu.VMEM(unit + (SEL_TILE, ncol), F32),
                        pltpu.VMEM(unit + (SEL_TILE, ncol), BF16),
                        pltpu.VMEM(unit + (1, ncol), F32),
                        pltpu.VMEM((max(nsel // NSA_HEAD_DIM, 1),) + unit + (ncol, LANES), BF16)],
    )
    return pl.pallas_call(
        functools.partial(_selected_kernel, nt=nt, nsel=nsel),
        grid_spec=grid_spec,
        out_shape=jax.ShapeDtypeStruct((bsz, s, NSA_Q_WIDTH), F32),
        compiler_params=_cparams(("parallel", "parallel")),
        name="nsa_selected",
    )(tile_list, tile_count, qp, k, vt, bias, gt)


def _active_tiles(bias, bsz, s):
    n_q = s // (SEL_Q * TOK)
    n_kt = s // SEL_TILE
    rows = SEL_TILE // NSA_SEL_BLOCK * TOK
    act = bias.reshape(bsz, NSA_KV_GROUPS, n_q, SEL_Q, n_kt, rows).max(axis=(1, 3, 5)) > -1.0
    n_diag = (jnp.arange(n_q) * SEL_Q * TOK) // SEL_TILE
    act = act & (jnp.arange(n_kt)[None, None, :] < n_diag[None, :, None])
    order = jnp.argsort(jnp.where(act, 0, 1), axis=-1, stable=True).astype(jnp.int32)
    return order.reshape(-1), jnp.sum(act, axis=-1, dtype=jnp.int32).reshape(-1)


def _ssd_kernel(z_ref, xbc_ref, dt_ref, cw_ref, cb_ref, dtb_ref, alog_ref, dskip_ref, nw_ref, e8_ref,
                tri_ref, y_ref, ext_ref, st_ref):
    @pl.when(pl.program_id(0) == 0)
    def _():
        ext_ref[pl.ds(0, SUBLANES), :] = jnp.zeros((SUBLANES, SSM_CONV_DIM), F32)
        st_ref[...] = jnp.zeros_like(st_ref)

    xb = xbc_ref[...]
    ext_ref[pl.ds(SUBLANES, TOK), :] = xb
    conv = cb_ref[...] + xb * cw_ref[SSM_CONV - 1:SSM_CONV, :]
    for k in range(SSM_CONV - 1):
        conv = conv + ext_ref[pl.ds(SUBLANES - (SSM_CONV - 1) + k, TOK), :] * cw_ref[k:k + 1, :]
    ext_ref[pl.ds(0, SUBLANES), :] = xb[TOK - SUBLANES:TOK, :]
    u = _silu(conv)
    xs = u[:, :SSM_INNER]
    gn = SSM_GROUPS * SSM_STATE
    bm = u[:, SSM_INNER:SSM_INNER + gn]
    cm = u[:, SSM_INNER + gn:]
    dt = jax.nn.softplus(dt_ref[...] + dtb_ref[...])
    da = dt * (-jnp.exp(alog_ref[...]))
    tri = tri_ref[...]
    e8 = e8_ref[...]
    cum = _dot_sel_l(tri, da)
    cum_t = cum.T
    cum_e = _dot_sel_r(cum, e8)
    dt_e = _dot_sel_r(dt, e8)
    xdt = xs * dt_e
    lane = lax.broadcasted_iota(jnp.int32, (TOK, LANES), 1)
    t_io = lax.broadcasted_iota(jnp.int32, (TOK, TOK), 0)
    s_io = lax.broadcasted_iota(jnp.int32, (TOK, TOK), 1)
    causal = s_io <= t_io
    hg = SSM_HEADS // SSM_GROUPS
    y_parts = []
    cbs = []
    for g in range(SSM_GROUPS):
        in_g = (lane // SSM_STATE) == g
        cbs.append(_dot_nt(jnp.where(in_g, cm, 0.0).astype(BF16), bm.astype(BF16)))
    for c in range(SSM_HEADS // 2):
        acc = jnp.zeros((TOK, LANES), F32)
        xpair = xdt[:, c * LANES:(c + 1) * LANES]
        for hh in range(2):
            h = 2 * c + hh
            seg = cum[:, h:h + 1] - cum_t[h:h + 1, :]
            decay = jnp.where(causal, jnp.exp(jnp.where(causal, seg, 0.0)), 0.0)
            mat = (cbs[h // hg] * decay).astype(BF16)
            xh = jnp.where((lane // SSM_HEAD_DIM) == hh, xpair, 0.0).astype(BF16)
            acc = acc + _dot(mat, xh)
        y_parts.append(acc)
    y_diag = jnp.concatenate(y_parts, axis=1)
    prev = st_ref[...]
    y_off = _dot(cm.astype(BF16), prev.astype(BF16)) * jnp.exp(cum_e)
    cum_last = cum_e[TOK - 1:TOK, :]
    xdec = (xdt * jnp.exp(cum_last - cum_e)).astype(BF16)
    new = _dot(bm.T.astype(BF16), xdec)
    row_g = lax.broadcasted_iota(jnp.int32, (LANES, SSM_INNER), 0) // SSM_STATE
    col_g = lax.broadcasted_iota(jnp.int32, (LANES, SSM_INNER), 1) // (hg * SSM_HEAD_DIM)
    st_ref[...] = prev * jnp.exp(cum_last) + jnp.where(row_g == col_g, new, 0.0)
    y = (y_diag + y_off + xs * dskip_ref[...]) * _silu(z_ref[...])
    gw = SSM_INNER // SSM_GROUPS
    outs = []
    for g in range(SSM_GROUPS):
        yg = y[:, g * gw:(g + 1) * gw]
        outs.append(_rms(yg, nw_ref[:, g * gw:(g + 1) * gw]))
    y_ref[...] = jnp.concatenate(outs, axis=1).astype(BF16)


SSD_INPUTS = 11


def _hgrn2_kernel(q_ref, f_ref, i_ref, g_ref, lb_ref, nw_ref, tri_ref, y_ref, st_ref, kp_ref, bp_ref, vp_ref):
    @pl.when(pl.program_id(0) == 0)
    def _():
        st_ref[...] = jnp.zeros_like(st_ref)
        kp_ref[...] = jnp.zeros_like(kp_ref)
        bp_ref[...] = jnp.zeros_like(bp_ref)
        vp_ref[...] = jnp.zeros_like(vp_ref)

    nsub = TOK // HG_SUB
    in_sub = lax.broadcasted_iota(jnp.int32, (TOK, HG_KEY_DIM), 0) % HG_SUB
    tri = tri_ref[...]
    outs = []
    for h in range(HG_HEADS):
        sl = slice(h * HG_KEY_DIM, (h + 1) * HG_KEY_DIM)
        lb = lb_ref[:, sl]
        fr = f_ref[:, sl]
        q = _silu(q_ref[:, sl]) * (HG_KEY_DIM ** -0.5)
        log_f = jnp.log(lb + (1.0 - lb) * jax.nn.sigmoid(fr))
        k = (1.0 - lb) * jax.nn.sigmoid(-fr)
        v = i_ref[:, sl]
        bcum = _dot_sel_l(tri, log_f)
        st = st_ref[h]
        o = _dot_nt((q * jnp.exp(bcum)).astype(BF16), st.astype(BF16))
        ends = [bcum[(j + 1) * HG_SUB - 1:(j + 1) * HG_SUB, :] for j in range(nsub)]
        blk = (HG_SUB, HG_KEY_DIM)
        zero_blk = jnp.zeros(blk, BF16)
        e_prev = jnp.concatenate([jnp.zeros(blk, F32)] + [jnp.broadcast_to(ends[j], blk) for j in range(nsub - 1)],
                                 axis=0)
        e_own = jnp.concatenate([jnp.broadcast_to(ends[j], blk) for j in range(nsub)], axis=0)
        q_loc = (q * jnp.exp(bcum - e_prev)).astype(BF16)
        k_loc = k * jnp.exp(e_own - bcum)
        q_cols = []
        k_cols = []
        for i in range(1, nsub):
            q_cols.append(jnp.concatenate(
                [zero_blk] * i + [q_loc[i * HG_SUB:(i + 1) * HG_SUB]] + [zero_blk] * (nsub - 1 - i), axis=0))
            parts = []
            for j in range(i):
                kj = k_loc[j * HG_SUB:(j + 1) * HG_SUB]
                if j < i - 1:
                    kj = kj * jnp.exp(ends[i - 1] - ends[j])
                parts.append(kj.astype(BF16))
            k_cols.append(jnp.concatenate(parts + [zero_blk] * (nsub - i), axis=0))
        q_aug = jnp.concatenate(q_cols, axis=1)
        k_aug = jnp.concatenate(k_cols, axis=1)
        att = _dot_nt(q_aug, k_aug)
        o = o + _dot(att.astype(BF16), v.astype(BF16))
        kp_ref[pl.ds(HG_SUB, TOK), :] = k
        bp_ref[pl.ds(HG_SUB, TOK), :] = bcum
        vp_ref[pl.ds(HG_SUB, TOK), :] = v
        o = o + jnp.sum(q * k, axis=1, keepdims=True) * v
        for d in range(1, HG_SUB):
            ks = kp_ref[pl.ds(HG_SUB - d, TOK), :]
            bs = bp_ref[pl.ds(HG_SUB - d, TOK), :]
            vs = vp_ref[pl.ds(HG_SUB - d, TOK), :]
            ok = in_sub >= d
            w = jnp.sum(q * ks * jnp.exp(bcum - bs), axis=1, keepdims=True)
            o = o + jnp.where(ok, w * vs, 0.0)
        b_last = bcum[TOK - 1:TOK, :]
        kdec = (k * jnp.exp(b_last - bcum)).astype(BF16)
        st_ref[h] = st * jnp.exp(b_last) + _dot(v.T.astype(BF16), kdec)
        outs.append(_rms(o, nw_ref[...]) * _silu(g_ref[:, sl]))
    y_ref[...] = jnp.concatenate(outs, axis=1).astype(BF16)


HG_INPUTS = 7


SSD_BLOCKED = 3
HG_BLOCKED = 4


def _recurrent_kernel(*refs, bsz):
    n_in = SSD_INPUTS + HG_INPUTS
    ssd_out, hg_out = refs[n_in], refs[n_in + 1]
    scratch = refs[n_in + 2:]
    for b in range(bsz):
        ssd_in = [r.at[b] for r in refs[:SSD_BLOCKED]] + list(refs[SSD_BLOCKED:SSD_INPUTS])
        hg_in = ([r.at[b] for r in refs[SSD_INPUTS:SSD_INPUTS + HG_BLOCKED]]
                 + list(refs[SSD_INPUTS + HG_BLOCKED:n_in]))
        _ssd_kernel(*ssd_in, ssd_out.at[b], *(r.at[b] for r in scratch[:2]))
        _hgrn2_kernel(*hg_in, hg_out.at[b], *(r.at[b] for r in scratch[2:]))


def _recurrent_mixers(proj, cw, cb, dtb, alog, dskip, ssd_nw, e8, lb, hg_nw, tri, bsz, s):
    nt = s // TOK
    proj = proj.reshape(bsz, s, proj.shape[-1])

    def col(c0, w):
        return pl.BlockSpec((bsz, TOK, w), lambda i: (0, i, c0 // w))

    def full(shape):
        return pl.BlockSpec(shape, lambda i: (0,) * len(shape))

    def per_batch(shape):
        return pltpu.VMEM((bsz,) + shape, F32)

    pad = per_batch((TOK + HG_SUB, HG_KEY_DIM))
    ssd_specs = [col(COL_Z, SSM_INNER), col(COL_XBC, SSM_CONV_DIM), col(COL_DT, LANES),
                 full((SSM_CONV, SSM_CONV_DIM)), full((1, SSM_CONV_DIM)), full((1, LANES)), full((1, LANES)),
                 full((1, SSM_INNER)), full((1, SSM_INNER)), full((LANES, SSM_INNER)), full((TOK, TOK))]
    hg_specs = [col(COL_HQ, HG_WIDTH), col(COL_HF, HG_WIDTH), col(COL_HI, HG_WIDTH), col(COL_HG, HG_WIDTH),
                full((1, HG_WIDTH)), full((1, HG_VAL_DIM)), full((TOK, TOK))]
    assert len(ssd_specs) == SSD_INPUTS and len(hg_specs) == HG_INPUTS
    y_b, y_c = pl.pallas_call(
        functools.partial(_recurrent_kernel, bsz=bsz),
        grid=(nt,),
        in_specs=ssd_specs + hg_specs,
        out_specs=[pl.BlockSpec((bsz, TOK, SSM_INNER), lambda i: (0, i, 0)),
                   pl.BlockSpec((bsz, TOK, HG_WIDTH), lambda i: (0, i, 0))],
        out_shape=[jax.ShapeDtypeStruct((bsz, s, SSM_INNER), BF16),
                   jax.ShapeDtypeStruct((bsz, s, HG_WIDTH), BF16)],
        scratch_shapes=[per_batch((TOK + SUBLANES, SSM_CONV_DIM)), per_batch((LANES, SSM_INNER)),
                        per_batch((HG_HEADS, HG_VAL_DIM, HG_KEY_DIM)), pad, pad, pad],
        compiler_params=_cparams(("arbitrary",)),
        name="ssd_hgrn2",
    )(proj, proj, proj, cw, cb, dtb, alog, dskip, ssd_nw, e8, tri, proj, proj, proj, proj, lb, hg_nw, tri)
    return y_b.reshape(bsz * s, SSM_INNER), y_c.reshape(bsz * s, HG_WIDTH)


def _merge_kernel(ga_ref, gb_ref, gc_ref, oc_ref, os_ref, ow_ref, yb_ref, yc_ref, x_ref,
                  wa_ref, wb_ref, wc_ref, wo_ref, o_ref):
    nsa = oc_ref[...] + os_ref[...] + ow_ref[...]
    ya = _dot(nsa.astype(BF16), wa_ref[...])
    yb = _dot(yb_ref[...], wb_ref[...])
    yc = _dot(yc_ref[...], wc_ref[...])
    merged = (jax.nn.sigmoid(ga_ref[...].astype(F32)) * ya + jax.nn.sigmoid(gb_ref[...].astype(F32)) * yb
              + jax.nn.sigmoid(gc_ref[...].astype(F32)) * yc)
    o_ref[...] = x_ref[...] + _dot(merged.astype(BF16), wo_ref[...])


def _merge(gates, oc, os_, ow, yb, yc, x2, wa, wb, wc, wo):
    t = x2.shape[0]
    tm = min(512, t)

    def col(c0, w):
        return pl.BlockSpec((tm, w), lambda i: (i, c0 // w))

    def full(shape):
        return pl.BlockSpec(shape, lambda i: (0,) * len(shape))

    row512 = pl.BlockSpec((tm, 512), lambda i: (i, 0))
    return pl.pallas_call(
        _merge_kernel,
        grid=(t // tm,),
        in_specs=[col(COL_GA, D_MODEL), col(COL_GB, D_MODEL), col(COL_GC, D_MODEL),
                  row512, row512, row512, row512, row512,
                  pl.BlockSpec((tm, D_MODEL), lambda i: (i, 0)),
                  full((NSA_Q_WIDTH, D_MODEL)), full((SSM_INNER, D_MODEL)),
                  full((HG_WIDTH, D_MODEL)), full((D_MODEL, D_MODEL))],
        out_specs=pl.BlockSpec((tm, D_MODEL), lambda i: (i, 0)),
        out_shape=jax.ShapeDtypeStruct((t, D_MODEL), F32),
        compiler_params=_cparams(("parallel",)),
        name="merge",
    )(gates, gates, gates, oc, os_, ow, yb, yc, x2, wa, wb, wc, wo)


def _xa_mem_kernel(mem_ref, nm_ref, wk_ref, wv_ref, kn_ref, k_ref, v_ref):
    m = _rms(mem_ref[0], nm_ref[...]).astype(BF16)
    k = _dot(m, wk_ref[...])
    ks = [_rms(k[:, h * XA_HEAD_DIM:(h + 1) * XA_HEAD_DIM], kn_ref[...]) for h in range(XA_HEADS)]
    k_ref[0] = jnp.concatenate(ks, axis=1).astype(BF16)
    v_ref[0] = _dot(m, wv_ref[...]).astype(BF16)


def _xa_mem(mem, nm, wk, wv, kn):
    bsz, ml, _ = mem.shape

    def full(shape):
        return pl.BlockSpec(shape, lambda b: (0,) * len(shape))

    return pl.pallas_call(
        _xa_mem_kernel,
        grid=(bsz,),
        in_specs=[pl.BlockSpec((1, ml, D_MODEL), lambda b: (b, 0, 0)), full((1, D_MODEL)),
                  full((D_MODEL, XA_WIDTH)), full((D_MODEL, XA_WIDTH)), full((1, XA_HEAD_DIM))],
        out_specs=[pl.BlockSpec((1, ml, XA_WIDTH), lambda b: (b, 0, 0)),
                   pl.BlockSpec((1, ml, XA_WIDTH), lambda b: (b, 0, 0))],
        out_shape=[jax.ShapeDtypeStruct((bsz, ml, XA_WIDTH), BF16),
                   jax.ShapeDtypeStruct((bsz, ml, XA_WIDTH), BF16)],
        compiler_params=_cparams(("parallel",)),
        name="xa_mem",
    )(mem, nm, wk, wv, kn)


def _xa_kernel(x_ref, nx_ref, wq_ref, qn_ref, k_ref, v_ref, wo_ref, o_ref):
    x = x_ref[...]
    h = _rms(x, nx_ref[...]).astype(BF16)
    q = _dot(h, wq_ref[...])
    outs = []
    for hd in range(XA_HEADS):
        sl = slice(hd * XA_HEAD_DIM, (hd + 1) * XA_HEAD_DIM)
        qh = _rms(q[:, sl], qn_ref[...]).astype(BF16)
        sc = _dot_nt(qh, k_ref[0, :, sl]) * (XA_HEAD_DIM ** -0.5)
        sc = sc - jnp.max(sc, axis=-1, keepdims=True)
        p = jnp.exp(sc)
        p = p / jnp.sum(p, axis=-1, keepdims=True)
        outs.append(_dot(p.astype(BF16), v_ref[0, :, sl]))
    o = jnp.concatenate(outs, axis=1).astype(BF16)
    o_ref[...] = x + _dot(o, wo_ref[...])


def _xa(x2, nx, wq, qn, k, v, wo, s):
    t = x2.shape[0]
    tm = min(512, s)
    ml = k.shape[1]
    per_b = s // tm

    def full(shape):
        return pl.BlockSpec(shape, lambda i: (0,) * len(shape))

    return pl.pallas_call(
        _xa_kernel,
        grid=(t // tm,),
        in_specs=[pl.BlockSpec((tm, D_MODEL), lambda i: (i, 0)), full((1, D_MODEL)), full((D_MODEL, XA_WIDTH)),
                  full((1, XA_HEAD_DIM)),
                  pl.BlockSpec((1, ml, XA_WIDTH), lambda i: (i // per_b, 0, 0)),
                  pl.BlockSpec((1, ml, XA_WIDTH), lambda i: (i // per_b, 0, 0)),
                  full((XA_WIDTH, D_MODEL))],
        out_specs=pl.BlockSpec((tm, D_MODEL), lambda i: (i, 0)),
        out_shape=jax.ShapeDtypeStruct((t, D_MODEL), F32),
        compiler_params=_cparams(("parallel",)),
        name="cross_attn",
    )(x2, nx, wq, qn, k, v, wo)


def _ffn_kernel(x_ref, nw_ref, wg_ref, wu_ref, wd_ref, o_ref, h_ref, acc_ref):
    j = pl.program_id(1)

    @pl.when(j == 0)
    def _():
        h_ref[...] = _rms(x_ref[...], nw_ref[...]).astype(BF16)
        acc_ref[...] = jnp.zeros_like(acc_ref)

    h = h_ref[...]
    a = _silu(_dot(h, wg_ref[...])) * _dot(h, wu_ref[...])
    acc_ref[...] += _dot(a.astype(BF16), wd_ref[...])

    @pl.when(j == pl.num_programs(1) - 1)
    def _():
        o_ref[...] = x_ref[...] + acc_ref[...]


def _ffn(x2, nw, wg, wu, wd):
    t = x2.shape[0]
    tm = min(1024, t)
    th = FFN_HIDDEN // 2
    return pl.pallas_call(
        _ffn_kernel,
        grid=(t // tm, FFN_HIDDEN // th),
        in_specs=[pl.BlockSpec((tm, D_MODEL), lambda i, j: (i, 0)),
                  pl.BlockSpec((1, D_MODEL), lambda i, j: (0, 0)),
                  pl.BlockSpec((D_MODEL, th), lambda i, j: (0, j)),
                  pl.BlockSpec((D_MODEL, th), lambda i, j: (0, j)),
                  pl.BlockSpec((th, D_MODEL), lambda i, j: (j, 0))],
        out_specs=pl.BlockSpec((tm, D_MODEL), lambda i, j: (i, 0)),
        out_shape=jax.ShapeDtypeStruct((t, D_MODEL), F32),
        scratch_shapes=[pltpu.VMEM((tm, D_MODEL), BF16), pltpu.VMEM((tm, D_MODEL), F32)],
        compiler_params=_cparams(("parallel", "arbitrary")),
        name="ffn",
    )(x2, nw, wg, wu, wd)


def _pack_w_in(w):
    parts = {}
    off = 0
    names = ("q", "kc", "vc", "ks", "vs", "kw", "vw", "ng", "z", "xbc", "dt", "hq", "hf", "hi", "hg",
             "ga", "gb", "gc")
    for name, size in zip(names, SPLIT_SIZES):
        parts[name] = w[:, off:off + size]
        off += size

    def padded(a, width):
        return jnp.pad(a, ((0, 0), (0, width - a.shape[1])))

    cols = [parts["ga"], parts["gb"], parts["gc"], parts["q"], parts["z"], parts["hq"], parts["hf"],
            parts["hi"], parts["hg"], parts["xbc"], parts["kc"], parts["vc"], parts["ks"], parts["vs"],
            parts["kw"], parts["vw"], padded(parts["ng"], LANES), padded(parts["dt"], LANES)]
    packed = jnp.concatenate(cols, axis=1)
    return padded(packed, PROJ_WIDTH).astype(BF16)


def _rope_tables(pos):
    half = NSA_HEAD_DIM // 2
    inv = 1.0 / (ROPE_THETA ** (jnp.arange(half, dtype=F32) / half))
    ang = pos.astype(F32)[:, None] * inv[None, :]
    reps = LANES // half
    return jnp.tile(jnp.cos(ang), (1, reps)), jnp.tile(jnp.sin(ang), (1, reps))


def _block_diag2(w):
    z = jnp.zeros_like(w)
    return jnp.concatenate([jnp.concatenate([w, z], axis=-1), jnp.concatenate([z, w], axis=-1)], axis=-2)


def kernel(x, mem, norm_mix, w_in, nsa_q_norm, nsa_k_norm, nsa_cmp_pos_k, nsa_cmp_pos_v, nsa_cmp_w1_k, nsa_cmp_w2_k, nsa_cmp_w1_v, nsa_cmp_w2_v, w_nsa_o, ssm_conv_w, ssm_conv_b, ssm_dt_bias, ssm_a_log, ssm_d, ssm_norm, w_ssm_o, hg_lb_logits, hg_norm, w_hg_o, w_out, norm_xa, norm_mem, xa_w_q, xa_w_k, xa_w_v, xa_q_norm, xa_k_norm, xa_w_o, norm_ffn, ffn_w_gate, ffn_w_up, ffn_w_down):
    bsz, s, d = x.shape
    depth = w_in.shape[0]
    assert d == D_MODEL and PREP_TILES * TOK == SEL_TILE
    assert s % SEL_TILE == 0 and s % (SEL_Q * TOK) == 0 and s // TOK >= WIN_TILES
    assert s // NSA_SEL_BLOCK >= NSA_TOPK and (s // NSA_CMP_STRIDE) % (4 * SUBLANES) == 0
    t = bsz * s
    hd = NSA_HEAD_DIM

    cos_t, sin_t = _rope_tables(jnp.arange(s))
    nbp = s // NSA_CMP_STRIDE
    cos_c, sin_c = _rope_tables(jnp.arange(nbp) * NSA_CMP_STRIDE + NSA_CMP_BLOCK - 1)
    tri = jnp.tril(jnp.ones((TOK, TOK), F32)).astype(BF16)
    rows = np.arange(LANES)[:, None]
    cols = np.arange(NSA_Q_WIDTH)[None, :]
    e8 = jnp.asarray(rows == cols // SSM_HEAD_DIM, F32).astype(BF16)
    li = np.arange(LANES)[:, None]
    lo = np.arange(LANES)[None, :]
    head_sum = jnp.asarray(li // hd == lo // hd, F32).astype(BF16)
    rot_half = jnp.asarray(np.where(lo % hd < hd // 2, -1.0 * (li == lo + hd // 2), 1.0 * (li == lo - hd // 2)),
                           F32).astype(BF16)

    lb_sm = jax.nn.softmax(hg_lb_logits.astype(F32), axis=0)
    lb_all = jnp.cumsum(lb_sm, axis=0) - lb_sm[0:1]

    def pad_lanes(v):
        return jnp.pad(v, (0, LANES - v.shape[0]))[None, :]

    x2 = x.reshape(t, d)
    for l in range(depth):
        gates, proj = _in_proj(x2, norm_mix[l][None, :], _pack_w_in(w_in[l]))

        qn2 = jnp.tile(nsa_q_norm[l], 2)[None, :]
        kn2 = jnp.tile(nsa_k_norm[l], 2)[None, :]
        qp, ksn, kwn, vst, vwt, gt = _nsa_prep(proj, cos_t, sin_t, qn2, kn2, head_sum, rot_half, bsz, s)
        w1k = _block_diag2(nsa_cmp_w1_k[l].reshape(NSA_CMP_BLOCK, hd, hd)).astype(BF16)
        w1v = _block_diag2(nsa_cmp_w1_v[l].reshape(NSA_CMP_BLOCK, hd, hd)).astype(BF16)
        kc, vct = _compress(proj, jnp.tile(nsa_cmp_pos_k[l], (1, 2)), jnp.tile(nsa_cmp_pos_v[l], (1, 2)),
                            w1k, _block_diag2(nsa_cmp_w2_k[l]).astype(BF16),
                            w1v, _block_diag2(nsa_cmp_w2_v[l]).astype(BF16), kn2, cos_c, sin_c, bsz, s)
        o_c, sel_bias = _cmp_topk(qp, kc, vct, gt, bsz, s)
        tile_list, tile_count = _active_tiles(sel_bias, bsz, s)
        o_s = _selected(qp, ksn, vst, sel_bias, gt, tile_list, tile_count, bsz, s)
        o_w = _window(qp, kwn, vwt, gt, bsz, s)

        y_b, y_c = _recurrent_mixers(
            proj, ssm_conv_w[l], ssm_conv_b[l][None, :], pad_lanes(ssm_dt_bias[l]), pad_lanes(ssm_a_log[l]),
            jnp.repeat(ssm_d[l], SSM_HEAD_DIM)[None, :], ssm_norm[l][None, :], e8,
            lb_all[l][None, :], hg_norm[l][None, :], tri, bsz, s)

        x2 = _merge(gates, o_c.reshape(t, -1), o_s.reshape(t, -1), o_w.reshape(t, -1), y_b, y_c, x2,
                    w_nsa_o[l].astype(BF16), w_ssm_o[l].astype(BF16), w_hg_o[l].astype(BF16),
                    w_out[l].astype(BF16))

        k_m, v_m = _xa_mem(mem, norm_mem[l][None, :], xa_w_k[l].astype(BF16), xa_w_v[l].astype(BF16),
                           xa_k_norm[l][None, :])
        x2 = _xa(x2, norm_xa[l][None, :], xa_w_q[l].astype(BF16), xa_q_norm[l][None, :], k_m, v_m,
                 xa_w_o[l].astype(BF16), s)
        x2 = _ffn(x2, norm_ffn[l][None, :], ffn_w_gate[l].astype(BF16), ffn_w_up[l].astype(BF16),
                  ffn_w_down[l].astype(BF16))
    return x2.reshape(bsz, s, d)
```

```python
import functools
import math

import numpy as np
import jax
import jax.numpy as jnp
from jax import lax
from jax.experimental import pallas as pl
from jax.experimental.pallas import tpu as pltpu

F32 = jnp.float32
BF16 = jnp.bfloat16

D_MODEL = 1024
NORM_EPS = 1e-6
ROPE_THETA = 10000.0
NEG_INF = -1e30

NSA_HEADS = 8
NSA_KV_GROUPS = 2
NSA_REP = NSA_HEADS // NSA_KV_GROUPS
NSA_HEAD_DIM = 64
NSA_CMP_BLOCK = 32
NSA_CMP_STRIDE = 16
NSA_SEL_BLOCK = 64
NSA_TOPK = 16
N_FORCED = 3
NSA_WINDOW = 512
NSA_Q_WIDTH = NSA_HEADS * NSA_HEAD_DIM
NSA_KV_WIDTH = NSA_KV_GROUPS * NSA_HEAD_DIM

SSM_INNER = 512
SSM_HEAD_DIM = 64
SSM_HEADS = SSM_INNER // SSM_HEAD_DIM
SSM_GROUPS = 2
SSM_STATE = 64
SSM_CONV = 4
SSM_CONV_DIM = SSM_INNER + 2 * SSM_GROUPS * SSM_STATE

HG_HEADS = 4
HG_KEY_DIM = 128
HG_VAL_DIM = 128
HG_WIDTH = HG_HEADS * HG_KEY_DIM
HG_SUB = 16

XA_HEADS = 4
XA_HEAD_DIM = 128
XA_WIDTH = XA_HEADS * XA_HEAD_DIM

FFN_HIDDEN = -(-(8 * D_MODEL) // (3 * 256)) * 256

SPLIT_SIZES = (
    NSA_Q_WIDTH, NSA_KV_WIDTH, NSA_KV_WIDTH, NSA_KV_WIDTH, NSA_KV_WIDTH, NSA_KV_WIDTH, NSA_KV_WIDTH,
    3 * NSA_HEADS, SSM_INNER, SSM_CONV_DIM, SSM_HEADS, HG_WIDTH, HG_WIDTH, HG_HEADS * HG_VAL_DIM,
    HG_HEADS * HG_VAL_DIM, D_MODEL, D_MODEL, D_MODEL,
)

LANES = 128
SUBLANES = 8
TOK = 128
NSA_BRANCHES = 3
GATE_ROWS = -(-NSA_BRANCHES * NSA_HEADS // SUBLANES) * SUBLANES
WIN_Q = 2
CMP_Q = 2
PREP_TILES = 4
SEL_Q = 4
SEL_TILE = 512
SEL_V_ROWS = NSA_HEAD_DIM + 16
WIN_TILES = NSA_WINDOW // TOK + 1
LOG2_E = math.log2(math.e)
MASK_FLOOR = -1e20
VMEM_LIMIT = 56 * 1024 * 1024

PROJ_WIDTH = 8192
GATE_WIDTH = 3 * D_MODEL
COL_GA, COL_GB, COL_GC = 0, 1024, 2048
COL_Q = 0
COL_Z = 512
COL_HQ, COL_HF, COL_HI, COL_HG = 1024, 1536, 2048, 2560
COL_XBC = 3072
COL_KC, COL_VC, COL_KS, COL_VS, COL_KW, COL_VW = 3840, 3968, 4096, 4224, 4352, 4480
COL_NG = 4608
COL_DT = 4736


def _cparams(sem):
    return pltpu.CompilerParams(dimension_semantics=sem, vmem_limit_bytes=VMEM_LIMIT)


def _dot(a, b):
    return jnp.dot(a, b, preferred_element_type=F32)


def _dot_nt(a, b):
    return lax.dot_general(a, b, (((1,), (1,)), ((), ())), preferred_element_type=F32)


def _split3(a):
    hi = a.astype(BF16)
    r1 = a - hi.astype(F32)
    mid = r1.astype(BF16)
    lo = (r1 - mid.astype(F32)).astype(BF16)
    return hi, mid, lo


def _dot_sel_r(a, sel):
    hi, mid, lo = _split3(a)
    return _dot(hi, sel) + _dot(mid, sel) + _dot(lo, sel)


def _dot_sel_l(sel, a):
    hi, mid, lo = _split3(a)
    return _dot(sel, hi) + _dot(sel, mid) + _dot(sel, lo)


def _silu(x):
    return x * jax.nn.sigmoid(x)


def _rms(x, w):
    return x * lax.rsqrt(jnp.mean(x * x, axis=-1, keepdims=True) + NORM_EPS) * w


def _inproj_kernel(x_ref, nw_ref, w_ref, og_ref, of_ref, h_ref, *, n_gate):
    j = pl.program_id(1)

    @pl.when(j == 0)
    def _():
        h_ref[...] = _rms(x_ref[...], nw_ref[...]).astype(BF16)

    @pl.when(j < n_gate)
    def _():
        og_ref[...] = _dot(h_ref[...], w_ref[...]).astype(BF16)

    @pl.when(j >= n_gate)
    def _():
        of_ref[...] = _dot(h_ref[...], w_ref[...])


def _in_proj(x2, norm_w, w_packed):
    t = x2.shape[0]
    tm = min(2048, t)
    tn = 1024
    n_gate = GATE_WIDTH // tn
    return pl.pallas_call(
        functools.partial(_inproj_kernel, n_gate=n_gate),
        grid=(t // tm, PROJ_WIDTH // tn),
        in_specs=[
            pl.BlockSpec((tm, D_MODEL), lambda i, j: (i, 0)),
            pl.BlockSpec((1, D_MODEL), lambda i, j: (0, 0)),
            pl.BlockSpec((D_MODEL, tn), lambda i, j: (0, j)),
        ],
        out_specs=[pl.BlockSpec((tm, tn), lambda i, j: (i, jnp.minimum(j, n_gate - 1))),
                   pl.BlockSpec((tm, tn), lambda i, j: (i, jnp.maximum(j - n_gate, 0)))],
        out_shape=[jax.ShapeDtypeStruct((t, GATE_WIDTH), BF16),
                   jax.ShapeDtypeStruct((t, PROJ_WIDTH - GATE_WIDTH), F32)],
        scratch_shapes=[pltpu.VMEM((tm, D_MODEL), BF16)],
        compiler_params=_cparams(("parallel", "arbitrary")),
        name="in_proj",
    )(x2, norm_w, w_packed)


def _norm_rope_pair(x, w, cos, sin):
    lane = lax.broadcasted_iota(jnp.int32, x.shape, 1)
    lo = lane < NSA_HEAD_DIM
    sq = x * x
    s_lo = jnp.sum(jnp.where(lo, sq, 0.0), axis=1, keepdims=True)
    s_hi = jnp.sum(jnp.where(lo, 0.0, sq), axis=1, keepdims=True)
    ms = jnp.where(lo, s_lo, s_hi) * (1.0 / NSA_HEAD_DIM)
    y = x * lax.rsqrt(ms + NORM_EPS) * w
    half = NSA_HEAD_DIM // 2
    first = (lane % NSA_HEAD_DIM) < half
    rot = jnp.where(first, -pltpu.roll(y, LANES - half, 1), pltpu.roll(y, half, 1))
    return y * cos + rot * sin


def _norm_rope_pair_mxu(x, w, cos, sin, head_sum, rot_half):
    ms = _dot_sel_r(x * x, head_sum) * (1.0 / NSA_HEAD_DIM)
    y = x * lax.rsqrt(ms + NORM_EPS) * w
    return y * cos + _dot_sel_r(y, rot_half) * sin


def _nsa_prep_kernel(q_ref, ks_ref, vs_ref, kw_ref, vw_ref, ng_ref, cos_ref, sin_ref, qn_ref, kn_ref,
                     hsum_ref, rot_ref, qp_ref, ksn_ref, kwn_ref, vst_ref, vwt_ref, gt_ref):
    norm_rope = functools.partial(_norm_rope_pair_mxu, head_sum=hsum_ref[...], rot_half=rot_ref[...])
    lane = lax.broadcasted_iota(jnp.int32, (TOK, LANES), 1)
    row = lax.broadcasted_iota(jnp.int32, (TOK, LANES), 0)
    scale = NSA_HEAD_DIM ** -0.5 * LOG2_E
    ones_rows = (lax.broadcasted_iota(jnp.int32, (SEL_V_ROWS - NSA_HEAD_DIM, TOK), 0) == 0).astype(BF16)
    for u in range(PREP_TILES):
        rows = slice(u * TOK, (u + 1) * TOK)
        cos = cos_ref[rows, :]
        sin = sin_ref[rows, :]
        for c in range(NSA_HEADS // 2):
            y = norm_rope(q_ref[rows, c * LANES:(c + 1) * LANES], qn_ref[...], cos, sin) * scale
            y_sw = pltpu.roll(y, NSA_HEAD_DIM, 1)
            for hh in range(2):
                h = 2 * c + hh
                g, r = divmod(h, NSA_REP)
                src = y if hh == g else y_sw
                keep = (lane // NSA_HEAD_DIM) == g
                qp_ref[0, g, rows, r * LANES:(r + 1) * LANES] = jnp.where(keep, src, 0.0).astype(BF16)
        ks_n = norm_rope(ks_ref[rows, :], kn_ref[...], cos, sin)
        tile = pl.program_id(1) * PREP_TILES + u
        blk = (tile * (TOK // NSA_SEL_BLOCK) + row // NSA_SEL_BLOCK) % NSA_HEAD_DIM
        onehot = ((lane % NSA_HEAD_DIM) == blk).astype(F32)
        for g in range(NSA_KV_GROUPS):
            ksn_ref[0, g, rows, :] = jnp.where((lane // NSA_HEAD_DIM) == g, ks_n, onehot).astype(BF16)
        kwn_ref[0, rows, :] = norm_rope(kw_ref[rows, :], kn_ref[...], cos, sin).astype(BF16)
        vs_t = vs_ref[rows, :].T.astype(BF16)
        vw_t = vw_ref[rows, :].T.astype(BF16)
        for g in range(NSA_KV_GROUPS):
            vst_ref[0, g, 0, 0:NSA_HEAD_DIM, rows] = vs_t[g * NSA_HEAD_DIM:(g + 1) * NSA_HEAD_DIM, :]
            vst_ref[0, g, 0, NSA_HEAD_DIM:SEL_V_ROWS, rows] = ones_rows
            vwt_ref[0, g, u] = vw_t[g * NSA_HEAD_DIM:(g + 1) * NSA_HEAD_DIM, :]
        gt_ref[0, u] = jax.nn.sigmoid(ng_ref[rows, :]).T[0:GATE_ROWS, :]


def _nsa_prep(proj, cos_t, sin_t, qn2, kn2, head_sum, rot_half, bsz, s):
    tq = PREP_TILES * TOK
    n_step = s // tq

    def col(c0, w):
        return pl.BlockSpec((tq, w), lambda b, i: (b * n_step + i, c0 // w))

    tab = pl.BlockSpec((tq, LANES), lambda b, i: (i, 0))
    vec = pl.BlockSpec((1, LANES), lambda b, i: (0, 0))
    sq = pl.BlockSpec((LANES, LANES), lambda b, i: (0, 0))
    return pl.pallas_call(
        _nsa_prep_kernel,
        grid=(bsz, n_step),
        in_specs=[col(COL_Q, NSA_Q_WIDTH), col(COL_KS, LANES), col(COL_VS, LANES), col(COL_KW, LANES),
                  col(COL_VW, LANES), col(COL_NG, LANES), tab, tab, vec, vec, sq, sq],
        out_specs=[
            pl.BlockSpec((1, NSA_KV_GROUPS, tq, NSA_REP * LANES), lambda b, i: (b, 0, i, 0)),
            pl.BlockSpec((1, NSA_KV_GROUPS, tq, LANES), lambda b, i: (b, 0, i, 0)),
            pl.BlockSpec((1, tq, LANES), lambda b, i: (b, i, 0)),
            pl.BlockSpec((1, NSA_KV_GROUPS, 1, SEL_V_ROWS, SEL_TILE), lambda b, i: (b, 0, i, 0, 0)),
            pl.BlockSpec((1, NSA_KV_GROUPS, PREP_TILES, NSA_HEAD_DIM, TOK), lambda b, i: (b, 0, i, 0, 0)),
            pl.BlockSpec((1, PREP_TILES, GATE_ROWS, TOK), lambda b, i: (b, i, 0, 0)),
        ],
        out_shape=[
            jax.ShapeDtypeStruct((bsz, NSA_KV_GROUPS, s, NSA_REP * LANES), BF16),
            jax.ShapeDtypeStruct((bsz, NSA_KV_GROUPS, s, LANES), BF16),
            jax.ShapeDtypeStruct((bsz, s, LANES), BF16),
            jax.ShapeDtypeStruct((bsz, NSA_KV_GROUPS, s // SEL_TILE, SEL_V_ROWS, SEL_TILE), BF16),
            jax.ShapeDtypeStruct((bsz, NSA_KV_GROUPS, s // TOK, NSA_HEAD_DIM, TOK), BF16),
            jax.ShapeDtypeStruct((bsz, s // TOK, GATE_ROWS, TOK), F32),
        ],
        compiler_params=_cparams(("parallel", "parallel")),
        name="nsa_prep",
    )(proj, proj, proj, proj, proj, proj, cos_t, sin_t, qn2, kn2, head_sum, rot_half)


def _compress_kernel(kc_ref, vc_ref, posk_ref, posv_ref, w1k_ref, w2k_ref, w1v_ref, w2v_ref, kn_ref,
                     cos_ref, sin_ref, kco_ref, vct_ref, sh_ref, *, nbp):
    half_blk = NSA_CMP_BLOCK // 2

    def compress(x_ref, pos_ref, w1_ref, w2_ref):
        acc_a = jnp.zeros((nbp, LANES), F32)
        acc_b = jnp.zeros((nbp, LANES), F32)
        for l in range(half_blk):
            xl = x_ref[pl.ds(l, nbp, stride=NSA_CMP_STRIDE), :]
            acc_a = acc_a + _dot((xl + pos_ref[l:l + 1, :]).astype(BF16), w1_ref[l])
            acc_b = acc_b + _dot((xl + pos_ref[half_blk + l:half_blk + l + 1, :]).astype(BF16),
                                 w1_ref[half_blk + l])
        sh_ref[pl.ds(0, nbp), :] = acc_b
        sh_ref[pl.ds(nbp, SUBLANES), :] = jnp.zeros((SUBLANES, LANES), F32)
        hid = acc_a + sh_ref[pl.ds(1, nbp), :]
        return _dot(_silu(hid).astype(BF16), w2_ref[...])

    kc = compress(kc_ref, posk_ref, w1k_ref, w2k_ref)
    kco_ref[0] = _norm_rope_pair(kc, kn_ref[...], cos_ref[...], sin_ref[...]).astype(BF16)
    vc = compress(vc_ref, posv_ref, w1v_ref, w2v_ref)
    vct_ref[0] = vc.T.astype(BF16)


def _compress(proj, posk, posv, w1k, w2k, w1v, w2v, kn2, cos_c, sin_c, bsz, s):
    nbp = s // NSA_CMP_STRIDE

    def full(shape):
        return pl.BlockSpec(shape, lambda b: (0,) * len(shape))

    return pl.pallas_call(
        functools.partial(_compress_kernel, nbp=nbp),
        grid=(bsz,),
        in_specs=[
            pl.BlockSpec((s, LANES), lambda b: (b, COL_KC // LANES)),
            pl.BlockSpec((s, LANES), lambda b: (b, COL_VC // LANES)),
            full((NSA_CMP_BLOCK, LANES)), full((NSA_CMP_BLOCK, LANES)),
            full((NSA_CMP_BLOCK, LANES, LANES)), full((LANES, LANES)),
            full((NSA_CMP_BLOCK, LANES, LANES)), full((LANES, LANES)),
            full((1, LANES)), full((nbp, LANES)), full((nbp, LANES)),
        ],
        out_specs=[
            pl.BlockSpec((1, nbp, LANES), lambda b: (b, 0, 0)),
            pl.BlockSpec((1, LANES, nbp), lambda b: (b, 0, 0)),
        ],
        out_shape=[
            jax.ShapeDtypeStruct((bsz, nbp, LANES), BF16),
            jax.ShapeDtypeStruct((bsz, LANES, nbp), BF16),
        ],
        scratch_shapes=[pltpu.VMEM((nbp + SUBLANES, LANES), F32)],
        compiler_params=_cparams(("parallel",)),
        name="nsa_compress",
    )(proj, proj, posk, posv, w1k, w2k, w1v, w2v, kn2, cos_c, sin_c)


def _q_rows(qp):
    return jnp.concatenate([qp[:, r * LANES:(r + 1) * LANES] for r in range(NSA_REP)], axis=0)


def _gate_row(gt, g, branch):
    rows = [(g * NSA_REP + r) * NSA_BRANCHES + branch for r in range(NSA_REP)]
    return jnp.concatenate([gt[i:i + 1, :] for i in rows], axis=1)


def _heads_to_token_major(acc_g):
    outs = []
    for p in range(NSA_REP // 2):
        blk = jnp.concatenate([acc_g[:, (2 * p) * TOK:(2 * p + 1) * TOK],
                               acc_g[:, (2 * p + 1) * TOK:(2 * p + 2) * TOK]], axis=0)
        outs.append(blk.T)
    return jnp.concatenate(outs, axis=1)


def _cmp_topk_kernel(qp_ref, kc_ref, vct_ref, gt_ref, oc_ref, mask_ref, ps_ref, *, nbp, nsel):
    qi = pl.program_id(1)
    ncol = NSA_REP * TOK
    per = NSA_SEL_BLOCK // NSA_CMP_STRIDE

    def attend(nrows):
        nblk = nrows // per
        units = [(h, g) for h in range(CMP_Q) for g in range(NSA_KV_GROUPS)]
        j_io = lax.broadcasted_iota(jnp.int32, (nblk, TOK), 0)
        rel = (lax.broadcasted_iota(jnp.int32, (nrows, ncol), 0) * NSA_CMP_STRIDE + (NSA_CMP_BLOCK - 1)
               - lax.broadcasted_iota(jnp.int32, (nrows, ncol), 1) % TOK)
        kc = kc_ref[0, 0:nrows, :]
        vct = vct_ref[0, :, 0:nrows]
        forced, valid, allowed = [], [], []
        for h in range(CMP_Q):
            t_sel = (qi * CMP_Q + h) * TOK + lax.broadcasted_iota(jnp.int32, (nblk, TOK), 1)
            cur = t_sel // NSA_SEL_BLOCK
            forced.append((j_io == 0) | (j_io == cur) | (j_io == cur - 1))
            valid.append(j_io * NSA_SEL_BLOCK <= t_sel)
            allowed.append(rel <= (qi * CMP_Q + h) * TOK)
        for h, g in units:
            ps_ref[h, g, pl.ds(0, SUBLANES), :] = jnp.zeros((SUBLANES, TOK), F32)
        scores = [_dot_nt(kc, _q_rows(qp_ref[0, g, h * TOK:(h + 1) * TOK, :])) for h, g in units]
        outs = {}
        imps = []
        for u, (h, g) in enumerate(units):
            s_t = jnp.where(allowed[h], scores[u], NEG_INF)
            m = jnp.max(s_t, axis=0, keepdims=True)
            p = jnp.exp2(s_t - jnp.maximum(m, MASK_FLOOR))
            l = jnp.sum(p, axis=0, keepdims=True)
            inv = jnp.where(l > 0.0, 1.0 / l, 0.0)
            o_t = _dot(vct, p.astype(BF16)) * inv
            gate = _gate_row(gt_ref[0, h], g, 0)
            outs[h, g] = _heads_to_token_major(o_t[g * NSA_HEAD_DIM:(g + 1) * NSA_HEAD_DIM, :] * gate)
            pn = p * inv
            psum = pn[:, 0:TOK]
            for r in range(1, NSA_REP):
                psum = psum + pn[:, r * TOK:(r + 1) * TOK]
            ps_ref[h, g, pl.ds(SUBLANES, nrows), :] = psum
            imp = (ps_ref[h, g, pl.ds(SUBLANES - 1, nblk, stride=per), :]
                   + ps_ref[h, g, pl.ds(SUBLANES + per - 1, nblk, stride=per), :])
            for c in range(per - 1):
                imp = imp + 2.0 * ps_ref[h, g, pl.ds(SUBLANES + c, nblk, stride=per), :]
            imps.append(jnp.where(forced[h], -jnp.inf, jnp.where(valid[h], imp, NEG_INF)))
        for h in range(CMP_Q):
            oc_ref[0, h * TOK:(h + 1) * TOK, :] = jnp.concatenate(
                [outs[h, g] for g in range(NSA_KV_GROUPS)], axis=1)

        def pick(_, carry):
            nxt = []
            for imp_c in carry:
                mx = jnp.max(imp_c, axis=0, keepdims=True)
                idx = jnp.min(jnp.where(imp_c == mx, j_io, nblk), axis=0, keepdims=True)
                nxt.append(jnp.where(j_io == idx, -jnp.inf, imp_c))
            return tuple(nxt)

        picked = lax.fori_loop(0, min(NSA_TOPK, nsel) - N_FORCED, pick, tuple(imps))
        for u, (h, g) in enumerate(units):
            mask_ref[0, g, h, 0:nblk, :] = jnp.where(valid[h] & (picked[u] == -jnp.inf), 0.0, NEG_INF)
            if nblk < nsel:
                mask_ref[0, g, h, nblk:nsel, :] = jnp.full((nsel - nblk, TOK), NEG_INF, F32)

    n_vis = ((qi + 1) * CMP_Q * TOK - NSA_CMP_BLOCK) // NSA_CMP_STRIDE + 1
    quarter = nbp // 4
    for k in range(1, 5):
        @pl.when((n_vis > (k - 1) * quarter) & (n_vis <= k * quarter))
        def _(k=k):
            attend(k * quarter)


def _cmp_topk(qp, kc, vct, gt, bsz, s):
    nt = s // TOK
    nbp = s // NSA_CMP_STRIDE
    nsel = s // NSA_SEL_BLOCK
    return pl.pallas_call(
        functools.partial(_cmp_topk_kernel, nbp=nbp, nsel=nsel),
        grid=(bsz, nt // CMP_Q),
        in_specs=[
            pl.BlockSpec((1, NSA_KV_GROUPS, CMP_Q * TOK, NSA_REP * LANES), lambda b, i: (b, 0, i, 0)),
            pl.BlockSpec((1, nbp, LANES), lambda b, i: (b, 0, 0)),
            pl.BlockSpec((1, LANES, nbp), lambda b, i: (b, 0, 0)),
            pl.BlockSpec((1, CMP_Q, GATE_ROWS, TOK), lambda b, i: (b, i, 0, 0)),
        ],
        out_specs=[
            pl.BlockSpec((1, CMP_Q * TOK, NSA_Q_WIDTH), lambda b, i: (b, i, 0)),
            pl.BlockSpec((1, NSA_KV_GROUPS, CMP_Q, nsel, TOK), lambda b, i: (b, 0, i, 0, 0)),
        ],
        out_shape=[
            jax.ShapeDtypeStruct((bsz, s, NSA_Q_WIDTH), F32),
            jax.ShapeDtypeStruct((bsz, NSA_KV_GROUPS, nt, nsel, TOK), F32),
        ],
        scratch_shapes=[pltpu.VMEM((CMP_Q, NSA_KV_GROUPS, nbp + SUBLANES, TOK), F32)],
        compiler_params=_cparams(("parallel", "parallel")),
        name="nsa_cmp_topk",
    )(qp, kc, vct, gt)


def _window_kernel(qp_ref, k_ref, vt_ref, gt_ref, o_ref):
    step = pl.program_id(1)
    ncol = NSA_REP * TOK
    wk = WIN_TILES * TOK
    units = [(h, g) for h in range(WIN_Q) for g in range(NSA_KV_GROUPS)]

    def run(interior):
        first = [step * WIN_Q + h - (WIN_TILES - 1) if interior else 0 for h in range(WIN_Q)]
        kwin = [k_ref[0, pl.ds(pl.multiple_of(first[h] * TOK, TOK), wk), :] for h in range(WIN_Q)]
        scores = [_dot_nt(kwin[h], _q_rows(qp_ref[0, g, h * TOK:(h + 1) * TOK, :])) for h, g in units]
        probs = []
        scales = []
        for u, (h, g) in enumerate(units):
            s_t = scores[u]
            qi = step * WIN_Q + h
            first_tile = first[h]
            if interior:
                row = lax.broadcasted_iota(jnp.int32, (TOK, ncol), 0)
                col = lax.broadcasted_iota(jnp.int32, (TOK, ncol), 1) % TOK
                oldest = jnp.where(row > col, s_t[:TOK], NEG_INF)
                newest = jnp.where(row <= col, s_t[wk - TOK:], NEG_INF)
                s_t = jnp.concatenate([oldest, s_t[TOK:wk - TOK], newest], axis=0)
            else:
                key = first_tile * TOK + lax.broadcasted_iota(jnp.int32, (wk, ncol), 0)
                t_io = qi * TOK + lax.broadcasted_iota(jnp.int32, (wk, ncol), 1) % TOK
                s_t = jnp.where((key <= t_io) & (t_io - key < NSA_WINDOW), s_t, NEG_INF)
            m = jnp.max(s_t, axis=0, keepdims=True)
            p = jnp.exp2(s_t - m)
            l = jnp.sum(p, axis=0, keepdims=True)
            probs.append(p.astype(BF16))
            scales.append(_gate_row(gt_ref[0, h], g, 2) / l)
        outs = {}
        for u, (h, g) in enumerate(units):
            vt = jnp.concatenate([vt_ref[0, g, first[h] + k] for k in range(WIN_TILES)], axis=1)
            outs[h, g] = _heads_to_token_major(_dot(vt, probs[u]) * scales[u])
        for h in range(WIN_Q):
            o_ref[0, h * TOK:(h + 1) * TOK, :] = jnp.concatenate(
                [outs[h, g] for g in range(NSA_KV_GROUPS)], axis=1)

    @pl.when(step * WIN_Q >= WIN_TILES - 1)
    def _():
        run(True)

    @pl.when(step * WIN_Q < WIN_TILES - 1)
    def _():
        run(False)


def _window(qp, k, vt, gt, bsz, s):
    nt = s // TOK
    return pl.pallas_call(
        _window_kernel,
        grid=(bsz, nt // WIN_Q),
        in_specs=[
            pl.BlockSpec((1, NSA_KV_GROUPS, WIN_Q * TOK, NSA_REP * LANES), lambda b, i: (b, 0, i, 0)),
            pl.BlockSpec((1, s, LANES), lambda b, i: (b, 0, 0)),
            pl.BlockSpec((1, NSA_KV_GROUPS, nt, NSA_HEAD_DIM, TOK), lambda b, i: (b, 0, 0, 0, 0)),
            pl.BlockSpec((1, WIN_Q, GATE_ROWS, TOK), lambda b, i: (b, i, 0, 0)),
        ],
        out_specs=pl.BlockSpec((1, WIN_Q * TOK, NSA_Q_WIDTH), lambda b, i: (b, i, 0)),
        out_shape=jax.ShapeDtypeStruct((bsz, s, NSA_Q_WIDTH), F32),
        compiler_params=_cparams(("parallel", "parallel")),
        name="nsa_window",
    )(qp, k, vt, gt)


def _selected_kernel(list_ref, cnt_ref, qp_ref, k_ref, vt_ref, bias_ref, gt_ref, o_ref, m_ref, acc_ref,
                     s_ref, p_ref, a_ref, qa_ref, *, nt, nsel):
    b = pl.program_id(0)
    qi = pl.program_id(1)
    ncol = NSA_REP * TOK
    blk_per_tile = SEL_TILE // NSA_SEL_BLOCK
    n_kt = nt // (SEL_TILE // TOK)
    n_q = nt // SEL_Q
    units = [(g, h) for g in range(NSA_KV_GROUPS) for h in range(SEL_Q)]
    m_ref[...] = jnp.full(m_ref.shape, NEG_INF, F32)
    acc_ref[...] = jnp.zeros(acc_ref.shape, F32)
    p_ref[...] = jnp.zeros(p_ref.shape, BF16)
    a_ref[...] = jnp.ones(a_ref.shape, F32)
    base = (b * n_q + qi) * n_kt
    count = cnt_ref[b * n_q + qi]
    n_diag = (qi * SEL_Q * TOK) // SEL_TILE

    lane = lax.broadcasted_iota(jnp.int32, (TOK, LANES), 1)
    n_var = qa_ref.shape[0]
    for g, h in units:
        bias_g = bias_ref[0, g, h]
        if nsel % LANES:
            bias_g = jnp.concatenate([bias_g, jnp.zeros((LANES - nsel % LANES, TOK), F32)], axis=0)
        own = (lane // NSA_HEAD_DIM) == g
        for v in range(n_var):
            src = bias_g[(v // 2) * LANES:(v // 2 + 1) * LANES, :].T
            if v % 2 != 1 - g:
                src = pltpu.roll(src, NSA_HEAD_DIM, 1)
            src = src.astype(BF16)
            for r in range(NSA_REP):
                q_r = qp_ref[0, g, h * TOK:(h + 1) * TOK, r * LANES:(r + 1) * LANES]
                qa_ref[v, g, h, r * TOK:(r + 1) * TOK, :] = jnp.where(own, q_r, src)

    def tile_at(i):
        return jnp.where(i < count, list_ref[base + jnp.minimum(i, n_kt - 1)], n_diag)

    def scores(j):
        v = (j * blk_per_tile) // NSA_HEAD_DIM
        for g in range(NSA_KV_GROUPS):
            kb = k_ref[0, g, pl.ds(pl.multiple_of(j * SEL_TILE, SEL_TILE), SEL_TILE), :]
            for h in range(SEL_Q):
                s_ref[g, h] = _dot_nt(kb, qa_ref[v, g, h])

    def values(j):
        for g, h in units:
            acc_ref[g, h] = acc_ref[g, h] * a_ref[g, h] + _dot(vt_ref[0, g, j], p_ref[g, h])

    def softmax(j, causal):
        for g, h in units:
            s_t = s_ref[g, h]
            if causal:
                t_io = ((qi * SEL_Q + h) * TOK
                        + lax.broadcasted_iota(jnp.int32, (SEL_TILE, ncol), 1) % TOK)
                key_pos = j * SEL_TILE + lax.broadcasted_iota(jnp.int32, (SEL_TILE, ncol), 0)
                s_t = jnp.where(key_pos <= t_io, s_t, NEG_INF)
            m_old = m_ref[g, h]
            m_new = jnp.maximum(m_old, jnp.max(s_t, axis=0, keepdims=True))
            alpha = jnp.exp2(m_old - m_new)
            p = jnp.exp2(s_t - jnp.maximum(m_new, MASK_FLOOR))
            m_ref[g, h] = m_new
            p_ref[g, h] = p.astype(BF16)
            a_ref[g, h] = alpha

    scores(tile_at(0))

    def body(i, c):
        values(tile_at(jnp.maximum(i - 1, 0)))
        softmax(tile_at(i), False)
        scores(tile_at(i + 1))
        return c

    lax.fori_loop(0, count, body, 0)
    values(tile_at(jnp.maximum(count - 1, 0)))
    softmax(n_diag, True)
    values(n_diag)
    for h in range(SEL_Q):
        outs = []
        for g in range(NSA_KV_GROUPS):
            acc = acc_ref[g, h]
            gate = _gate_row(gt_ref[0, h], g, 1)
            outs.append(_heads_to_token_major(acc[:NSA_HEAD_DIM] * (gate / acc[NSA_HEAD_DIM:NSA_HEAD_DIM + 1])))
        o_ref[0, h * TOK:(h + 1) * TOK, :] = jnp.concatenate(outs, axis=1)


def _selected(qp, k, vt, bias, gt, tile_list, tile_count, bsz, s):
    nt = s // TOK
    nsel = s // NSA_SEL_BLOCK
    ncol = NSA_REP * TOK
    unit = (NSA_KV_GROUPS, SEL_Q)
    grid_spec = pltpu.PrefetchScalarGridSpec(
        num_scalar_prefetch=2,
        grid=(bsz, nt // SEL_Q),
        in_specs=[
            pl.BlockSpec((1, NSA_KV_GROUPS, SEL_Q * TOK, NSA_REP * LANES), lambda b, i, tl, tc: (b, 0, i, 0)),
            pl.BlockSpec((1, NSA_KV_GROUPS, s, LANES), lambda b, i, tl, tc: (b, 0, 0, 0)),
            pl.BlockSpec((1, NSA_KV_GROUPS, s // SEL_TILE, SEL_V_ROWS, SEL_TILE),
                         lambda b, i, tl, tc: (b, 0, 0, 0, 0)),
            pl.BlockSpec((1, NSA_KV_GROUPS, SEL_Q, nsel, TOK), lambda b, i, tl, tc: (b, 0, i, 0, 0)),
            pl.BlockSpec((1, SEL_Q, GATE_ROWS, TOK), lambda b, i, tl, tc: (b, i, 0, 0)),
        ],
        out_specs=pl.BlockSpec((1, SEL_Q * TOK, NSA_Q_WIDTH), lambda b, i, tl, tc: (b, i, 0)),
        scratch_shapes=[pltpu.VMEM(unit + (1, ncol), F32),
                        pltpu.VMEM(unit + (SEL_V_ROWS, ncol), F32),
                        pltpu.VMEM(unit + (SEL_TILE, ncol), F32),
                        pltpu.VMEM(unit + (SEL_TILE, ncol), BF16),
                        pltpu.VMEM(unit + (1, ncol), F32),
                        pltpu.VMEM((max(nsel // NSA_HEAD_DIM, 1),) + unit + (ncol, LANES), BF16)],
    )
    return pl.pallas_call(
        functools.partial(_selected_kernel, nt=nt, nsel=nsel),
        grid_spec=grid_spec,
        out_shape=jax.ShapeDtypeStruct((bsz, s, NSA_Q_WIDTH), F32),
        compiler_params=_cparams(("parallel", "parallel")),
        name="nsa_selected",
    )(tile_list, tile_count, qp, k, vt, bias, gt)


def _active_tiles(bias, bsz, s):
    n_q = s // (SEL_Q * TOK)
    n_kt = s // SEL_TILE
    rows = SEL_TILE // NSA_SEL_BLOCK * TOK
    act = bias.reshape(bsz, NSA_KV_GROUPS, n_q, SEL_Q, n_kt, rows).max(axis=(1, 3, 5)) > -1.0
    n_diag = (jnp.arange(n_q) * SEL_Q * TOK) // SEL_TILE
    act = act & (jnp.arange(n_kt)[None, None, :] < n_diag[None, :, None])
    order = jnp.argsort(jnp.where(act, 0, 1), axis=-1, stable=True).astype(jnp.int32)
    return order.reshape(-1), jnp.sum(act, axis=-1, dtype=jnp.int32).reshape(-1)


def _ssd_kernel(z_ref, xbc_ref, dt_ref, cw_ref, cb_ref, dtb_ref, alog_ref, dskip_ref, nw_ref, e8_ref,
                tri_ref, y_ref, ext_ref, st_ref):
    @pl.when(pl.program_id(0) == 0)
    def _():
        ext_ref[pl.ds(0, SUBLANES), :] = jnp.zeros((SUBLANES, SSM_CONV_DIM), F32)
        st_ref[...] = jnp.zeros_like(st_ref)

    xb = xbc_ref[...]
    ext_ref[pl.ds(SUBLANES, TOK), :] = xb
    conv = cb_ref[...] + xb * cw_ref[SSM_CONV - 1:SSM_CONV, :]
    for k in range(SSM_CONV - 1):
        conv = conv + ext_ref[pl.ds(SUBLANES - (SSM_CONV - 1) + k, TOK), :] * cw_ref[k:k + 1, :]
    ext_ref[pl.ds(0, SUBLANES), :] = xb[TOK - SUBLANES:TOK, :]
    u = _silu(conv)
    xs = u[:, :SSM_INNER]
    gn = SSM_GROUPS * SSM_STATE
    bm = u[:, SSM_INNER:SSM_INNER + gn]
    cm = u[:, SSM_INNER + gn:]
    dt = jax.nn.softplus(dt_ref[...] + dtb_ref[...])
    da = dt * (-jnp.exp(alog_ref[...]))
    tri = tri_ref[...]
    e8 = e8_ref[...]
    cum = _dot_sel_l(tri, da)
    cum_t = cum.T
    cum_e = _dot_sel_r(cum, e8)
    dt_e = _dot_sel_r(dt, e8)
    xdt = xs * dt_e
    lane = lax.broadcasted_iota(jnp.int32, (TOK, LANES), 1)
    t_io = lax.broadcasted_iota(jnp.int32, (TOK, TOK), 0)
    s_io = lax.broadcasted_iota(jnp.int32, (TOK, TOK), 1)
    causal = s_io <= t_io
    hg = SSM_HEADS // SSM_GROUPS
    y_parts = []
    cbs = []
    for g in range(SSM_GROUPS):
        in_g = (lane // SSM_STATE) == g
        cbs.append(_dot_nt(jnp.where(in_g, cm, 0.0).astype(BF16), bm.astype(BF16)))
    for c in range(SSM_HEADS // 2):
        acc = jnp.zeros((TOK, LANES), F32)
        xpair = xdt[:, c * LANES:(c + 1) * LANES]
        for hh in range(2):
            h = 2 * c + hh
            seg = cum[:, h:h + 1] - cum_t[h:h + 1, :]
            decay = jnp.where(causal, jnp.exp(jnp.where(causal, seg, 0.0)), 0.0)
            mat = (cbs[h // hg] * decay).astype(BF16)
            xh = jnp.where((lane // SSM_HEAD_DIM) == hh, xpair, 0.0).astype(BF16)
            acc = acc + _dot(mat, xh)
        y_parts.append(acc)
    y_diag = jnp.concatenate(y_parts, axis=1)
    prev = st_ref[...]
    y_off = _dot(cm.astype(BF16), prev.astype(BF16)) * jnp.exp(cum_e)
    cum_last = cum_e[TOK - 1:TOK, :]
    xdec = (xdt * jnp.exp(cum_last - cum_e)).astype(BF16)
    new = _dot(bm.T.astype(BF16), xdec)
    row_g = lax.broadcasted_iota(jnp.int32, (LANES, SSM_INNER), 0) // SSM_STATE
    col_g = lax.broadcasted_iota(jnp.int32, (LANES, SSM_INNER), 1) // (hg * SSM_HEAD_DIM)
    st_ref[...] = prev * jnp.exp(cum_last) + jnp.where(row_g == col_g, new, 0.0)
    y = (y_diag + y_off + xs * dskip_ref[...]) * _silu(z_ref[...])
    gw = SSM_INNER // SSM_GROUPS
    outs = []
    for g in range(SSM_GROUPS):
        yg = y[:, g * gw:(g + 1) * gw]
        outs.append(_rms(yg, nw_ref[:, g * gw:(g + 1) * gw]))
    y_ref[...] = jnp.concatenate(outs, axis=1).astype(BF16)


SSD_INPUTS = 11


def _hgrn2_kernel(q_ref, f_ref, i_ref, g_ref, lb_ref, nw_ref, tri_ref, y_ref, st_ref, kp_ref, bp_ref, vp_ref):
    @pl.when(pl.program_id(0) == 0)
    def _():
        st_ref[...] = jnp.zeros_like(st_ref)
        kp_ref[...] = jnp.zeros_like(kp_ref)
        bp_ref[...] = jnp.zeros_like(bp_ref)
        vp_ref[...] = jnp.zeros_like(vp_ref)

    nsub = TOK // HG_SUB
    in_sub = lax.broadcasted_iota(jnp.int32, (TOK, HG_KEY_DIM), 0) % HG_SUB
    tri = tri_ref[...]
    outs = []
    for h in range(HG_HEADS):
        sl = slice(h * HG_KEY_DIM, (h + 1) * HG_KEY_DIM)
        lb = lb_ref[:, sl]
        fr = f_ref[:, sl]
        q = _silu(q_ref[:, sl]) * (HG_KEY_DIM ** -0.5)
        log_f = jnp.log(lb + (1.0 - lb) * jax.nn.sigmoid(fr))
        k = (1.0 - lb) * jax.nn.sigmoid(-fr)
        v = i_ref[:, sl]
        bcum = _dot_sel_l(tri, log_f)
        st = st_ref[h]
        o = _dot_nt((q * jnp.exp(bcum)).astype(BF16), st.astype(BF16))
        ends = [bcum[(j + 1) * HG_SUB - 1:(j + 1) * HG_SUB, :] for j in range(nsub)]
        blk = (HG_SUB, HG_KEY_DIM)
        zero_blk = jnp.zeros(blk, BF16)
        e_prev = jnp.concatenate([jnp.zeros(blk, F32)] + [jnp.broadcast_to(ends[j], blk) for j in range(nsub - 1)],
                                 axis=0)
        e_own = jnp.concatenate([jnp.broadcast_to(ends[j], blk) for j in range(nsub)], axis=0)
        q_loc = (q * jnp.exp(bcum - e_prev)).astype(BF16)
        k_loc = k * jnp.exp(e_own - bcum)
        q_cols = []
        k_cols = []
        for i in range(1, nsub):
            q_cols.append(jnp.concatenate(
                [zero_blk] * i + [q_loc[i * HG_SUB:(i + 1) * HG_SUB]] + [zero_blk] * (nsub - 1 - i), axis=0))
            parts = []
            for j in range(i):
                kj = k_loc[j * HG_SUB:(j + 1) * HG_SUB]
                if j < i - 1:
                    kj = kj * jnp.exp(ends[i - 1] - ends[j])
                parts.append(kj.astype(BF16))
            k_cols.append(jnp.concatenate(parts + [zero_blk] * (nsub - i), axis=0))
        q_aug = jnp.concatenate(q_cols, axis=1)
        k_aug = jnp.concatenate(k_cols, axis=1)
        att = _dot_nt(q_aug, k_aug)
        o = o + _dot(att.astype(BF16), v.astype(BF16))
        kp_ref[pl.ds(HG_SUB, TOK), :] = k
        bp_ref[pl.ds(HG_SUB, TOK), :] = bcum
        vp_ref[pl.ds(HG_SUB, TOK), :] = v
        o = o + jnp.sum(q * k, axis=1, keepdims=True) * v
        for d in range(1, HG_SUB):
            ks = kp_ref[pl.ds(HG_SUB - d, TOK), :]
            bs = bp_ref[pl.ds(HG_SUB - d, TOK), :]
            vs = vp_ref[pl.ds(HG_SUB - d, TOK), :]
            ok = in_sub >= d
            w = jnp.sum(q * ks * jnp.exp(bcum - bs), axis=1, keepdims=True)
            o = o + jnp.where(ok, w * vs, 0.0)
        b_last = bcum[TOK - 1:TOK, :]
        kdec = (k * jnp.exp(b_last - bcum)).astype(BF16)
        st_ref[h] = st * jnp.exp(b_last) + _dot(v.T.astype(BF16), kdec)
        outs.append(_rms(o, nw_ref[...]) * _silu(g_ref[:, sl]))
    y_ref[...] = jnp.concatenate(outs, axis=1).astype(BF16)


HG_INPUTS = 7


SSD_BLOCKED = 3
HG_BLOCKED = 4


def _recurrent_kernel(*refs, bsz):
    n_in = SSD_INPUTS + HG_INPUTS
    ssd_out, hg_out = refs[n_in], refs[n_in + 1]
    scratch = refs[n_in + 2:]
    for b in range(bsz):
        ssd_in = [r.at[b] for r in refs[:SSD_BLOCKED]] + list(refs[SSD_BLOCKED:SSD_INPUTS])
        hg_in = ([r.at[b] for r in refs[SSD_INPUTS:SSD_INPUTS + HG_BLOCKED]]
                 + list(refs[SSD_INPUTS + HG_BLOCKED:n_in]))
        _ssd_kernel(*ssd_in, ssd_out.at[b], *(r.at[b] for r in scratch[:2]))
        _hgrn2_kernel(*hg_in, hg_out.at[b], *(r.at[b] for r in scratch[2:]))


def _recurrent_mixers(proj, cw, cb, dtb, alog, dskip, ssd_nw, e8, lb, hg_nw, tri, bsz, s):
    nt = s // TOK
    proj = proj.reshape(bsz, s, proj.shape[-1])

    def col(c0, w):
        return pl.BlockSpec((bsz, TOK, w), lambda i: (0, i, c0 // w))

    def full(shape):
        return pl.BlockSpec(shape, lambda i: (0,) * len(shape))

    def per_batch(shape):
        return pltpu.VMEM((bsz,) + shape, F32)

    pad = per_batch((TOK + HG_SUB, HG_KEY_DIM))
    ssd_specs = [col(COL_Z, SSM_INNER), col(COL_XBC, SSM_CONV_DIM), col(COL_DT, LANES),
                 full((SSM_CONV, SSM_CONV_DIM)), full((1, SSM_CONV_DIM)), full((1, LANES)), full((1, LANES)),
                 full((1, SSM_INNER)), full((1, SSM_INNER)), full((LANES, SSM_INNER)), full((TOK, TOK))]
    hg_specs = [col(COL_HQ, HG_WIDTH), col(COL_HF, HG_WIDTH), col(COL_HI, HG_WIDTH), col(COL_HG, HG_WIDTH),
                full((1, HG_WIDTH)), full((1, HG_VAL_DIM)), full((TOK, TOK))]
    assert len(ssd_specs) == SSD_INPUTS and len(hg_specs) == HG_INPUTS
    y_b, y_c = pl.pallas_call(
        functools.partial(_recurrent_kernel, bsz=bsz),
        grid=(nt,),
        in_specs=ssd_specs + hg_specs,
        out_specs=[pl.BlockSpec((bsz, TOK, SSM_INNER), lambda i: (0, i, 0)),
                   pl.BlockSpec((bsz, TOK, HG_WIDTH), lambda i: (0, i, 0))],
        out_shape=[jax.ShapeDtypeStruct((bsz, s, SSM_INNER), BF16),
                   jax.ShapeDtypeStruct((bsz, s, HG_WIDTH), BF16)],
        scratch_shapes=[per_batch((TOK + SUBLANES, SSM_CONV_DIM)), per_batch((LANES, SSM_INNER)),
                        per_batch((HG_HEADS, HG_VAL_DIM, HG_KEY_DIM)), pad, pad, pad],
        compiler_params=_cparams(("arbitrary",)),
        name="ssd_hgrn2",
    )(proj, proj, proj, cw, cb, dtb, alog, dskip, ssd_nw, e8, tri, proj, proj, proj, proj, lb, hg_nw, tri)
    return y_b.reshape(bsz * s, SSM_INNER), y_c.reshape(bsz * s, HG_WIDTH)


def _merge_kernel(ga_ref, gb_ref, gc_ref, oc_ref, os_ref, ow_ref, yb_ref, yc_ref, x_ref,
                  wa_ref, wb_ref, wc_ref, wo_ref, o_ref):
    nsa = oc_ref[...] + os_ref[...] + ow_ref[...]
    ya = _dot(nsa.astype(BF16), wa_ref[...])
    yb = _dot(yb_ref[...], wb_ref[...])
    yc = _dot(yc_ref[...], wc_ref[...])
    merged = (jax.nn.sigmoid(ga_ref[...].astype(F32)) * ya + jax.nn.sigmoid(gb_ref[...].astype(F32)) * yb
              + jax.nn.sigmoid(gc_ref[...].astype(F32)) * yc)
    o_ref[...] = x_ref[...] + _dot(merged.astype(BF16), wo_ref[...])


def _merge(gates, oc, os_, ow, yb, yc, x2, wa, wb, wc, wo):
    t = x2.shape[0]
    tm = min(512, t)

    def col(c0, w):
        return pl.BlockSpec((tm, w), lambda i: (i, c0 // w))

    def full(shape):
        return pl.BlockSpec(shape, lambda i: (0,) * len(shape))

    row512 = pl.BlockSpec((tm, 512), lambda i: (i, 0))
    return pl.pallas_call(
        _merge_kernel,
        grid=(t // tm,),
        in_specs=[col(COL_GA, D_MODEL), col(COL_GB, D_MODEL), col(COL_GC, D_MODEL),
                  row512, row512, row512, row512, row512,
                  pl.BlockSpec((tm, D_MODEL), lambda i: (i, 0)),
                  full((NSA_Q_WIDTH, D_MODEL)), full((SSM_INNER, D_MODEL)),
                  full((HG_WIDTH, D_MODEL)), full((D_MODEL, D_MODEL))],
        out_specs=pl.BlockSpec((tm, D_MODEL), lambda i: (i, 0)),
        out_shape=jax.ShapeDtypeStruct((t, D_MODEL), F32),
        compiler_params=_cparams(("parallel",)),
        name="merge",
    )(gates, gates, gates, oc, os_, ow, yb, yc, x2, wa, wb, wc, wo)


def _xa_mem_kernel(mem_ref, nm_ref, wk_ref, wv_ref, kn_ref, k_ref, v_ref):
    m = _rms(mem_ref[0], nm_ref[...]).astype(BF16)
    k = _dot(m, wk_ref[...])
    ks = [_rms(k[:, h * XA_HEAD_DIM:(h + 1) * XA_HEAD_DIM], kn_ref[...]) for h in range(XA_HEADS)]
    k_ref[0] = jnp.concatenate(ks, axis=1).astype(BF16)
    v_ref[0] = _dot(m, wv_ref[...]).astype(BF16)


def _xa_mem(mem, nm, wk, wv, kn):
    bsz, ml, _ = mem.shape

    def full(shape):
        return pl.BlockSpec(shape, lambda b: (0,) * len(shape))

    return pl.pallas_call(
        _xa_mem_kernel,
        grid=(bsz,),
        in_specs=[pl.BlockSpec((1, ml, D_MODEL), lambda b: (b, 0, 0)), full((1, D_MODEL)),
                  full((D_MODEL, XA_WIDTH)), full((D_MODEL, XA_WIDTH)), full((1, XA_HEAD_DIM))],
        out_specs=[pl.BlockSpec((1, ml, XA_WIDTH), lambda b: (b, 0, 0)),
                   pl.BlockSpec((1, ml, XA_WIDTH), lambda b: (b, 0, 0))],
        out_shape=[jax.ShapeDtypeStruct((bsz, ml, XA_WIDTH), BF16),
                   jax.ShapeDtypeStruct((bsz, ml, XA_WIDTH), BF16)],
        compiler_params=_cparams(("parallel",)),
        name="xa_mem",
    )(mem, nm, wk, wv, kn)


def _xa_kernel(x_ref, nx_ref, wq_ref, qn_ref, k_ref, v_ref, wo_ref, o_ref):
    x = x_ref[...]
    h = _rms(x, nx_ref[...]).astype(BF16)
    q = _dot(h, wq_ref[...])
    outs = []
    for hd in range(XA_HEADS):
        sl = slice(hd * XA_HEAD_DIM, (hd + 1) * XA_HEAD_DIM)
        qh = _rms(q[:, sl], qn_ref[...]).astype(BF16)
        sc = _dot_nt(qh, k_ref[0, :, sl]) * (XA_HEAD_DIM ** -0.5)
        sc = sc - jnp.max(sc, axis=-1, keepdims=True)
        p = jnp.exp(sc)
        p = p / jnp.sum(p, axis=-1, keepdims=True)
        outs.append(_dot(p.astype(BF16), v_ref[0, :, sl]))
    o = jnp.concatenate(outs, axis=1).astype(BF16)
    o_ref[...] = x + _dot(o, wo_ref[...])


def _xa(x2, nx, wq, qn, k, v, wo, s):
    t = x2.shape[0]
    tm = min(512, s)
    ml = k.shape[1]
    per_b = s // tm

    def full(shape):
        return pl.BlockSpec(shape, lambda i: (0,) * len(shape))

    return pl.pallas_call(
        _xa_kernel,
        grid=(t // tm,),
        in_specs=[pl.BlockSpec((tm, D_MODEL), lambda i: (i, 0)), full((1, D_MODEL)), full((D_MODEL, XA_WIDTH)),
                  full((1, XA_HEAD_DIM)),
                  pl.BlockSpec((1, ml, XA_WIDTH), lambda i: (i // per_b, 0, 0)),
                  pl.BlockSpec((1, ml, XA_WIDTH), lambda i: (i // per_b, 0, 0)),
                  full((XA_WIDTH, D_MODEL))],
        out_specs=pl.BlockSpec((tm, D_MODEL), lambda i: (i, 0)),
        out_shape=jax.ShapeDtypeStruct((t, D_MODEL), F32),
        compiler_params=_cparams(("parallel",)),
        name="cross_attn",
    )(x2, nx, wq, qn, k, v, wo)


def _ffn_kernel(x_ref, nw_ref, wg_ref, wu_ref, wd_ref, o_ref, h_ref, acc_ref):
    j = pl.program_id(1)

    @pl.when(j == 0)
    def _():
        h_ref[...] = _rms(x_ref[...], nw_ref[...]).astype(BF16)
        acc_ref[...] = jnp.zeros_like(acc_ref)

    h = h_ref[...]
    a = _silu(_dot(h, wg_ref[...])) * _dot(h, wu_ref[...])
    acc_ref[...] += _dot(a.astype(BF16), wd_ref[...])

    @pl.when(j == pl.num_programs(1) - 1)
    def _():
        o_ref[...] = x_ref[...] + acc_ref[...]


def _ffn(x2, nw, wg, wu, wd):
    t = x2.shape[0]
    tm = min(1024, t)
    th = FFN_HIDDEN // 2
    return pl.pallas_call(
        _ffn_kernel,
        grid=(t // tm, FFN_HIDDEN // th),
        in_specs=[pl.BlockSpec((tm, D_MODEL), lambda i, j: (i, 0)),
                  pl.BlockSpec((1, D_MODEL), lambda i, j: (0, 0)),
                  pl.BlockSpec((D_MODEL, th), lambda i, j: (0, j)),
                  pl.BlockSpec((D_MODEL, th), lambda i, j: (0, j)),
                  pl.BlockSpec((th, D_MODEL), lambda i, j: (j, 0))],
        out_specs=pl.BlockSpec((tm, D_MODEL), lambda i, j: (i, 0)),
        out_shape=jax.ShapeDtypeStruct((t, D_MODEL), F32),
        scratch_shapes=[pltpu.VMEM((tm, D_MODEL), BF16), pltpu.VMEM((tm, D_MODEL), F32)],
        compiler_params=_cparams(("parallel", "arbitrary")),
        name="ffn",
    )(x2, nw, wg, wu, wd)


def _pack_w_in(w):
    parts = {}
    off = 0
    names = ("q", "kc", "vc", "ks", "vs", "kw", "vw", "ng", "z", "xbc", "dt", "hq", "hf", "hi", "hg",
             "ga", "gb", "gc")
    for name, size in zip(names, SPLIT_SIZES):
        parts[name] = w[:, off:off + size]
        off += size

    def padded(a, width):
        return jnp.pad(a, ((0, 0), (0, width - a.shape[1])))

    cols = [parts["ga"], parts["gb"], parts["gc"], parts["q"], parts["z"], parts["hq"], parts["hf"],
            parts["hi"], parts["hg"], parts["xbc"], parts["kc"], parts["vc"], parts["ks"], parts["vs"],
            parts["kw"], parts["vw"], padded(parts["ng"], LANES), padded(parts["dt"], LANES)]
    packed = jnp.concatenate(cols, axis=1)
    return padded(packed, PROJ_WIDTH).astype(BF16)


def _rope_tables(pos):
    half = NSA_HEAD_DIM // 2
    inv = 1.0 / (ROPE_THETA ** (jnp.arange(half, dtype=F32) / half))
    ang = pos.astype(F32)[:, None] * inv[None, :]
    reps = LANES // half
    return jnp.tile(jnp.cos(ang), (1, reps)), jnp.tile(jnp.sin(ang), (1, reps))


def _block_diag2(w):
    z = jnp.zeros_like(w)
    return jnp.concatenate([jnp.concatenate([w, z], axis=-1), jnp.concatenate([z, w], axis=-1)], axis=-2)


def kernel(x, mem, norm_mix, w_in, nsa_q_norm, nsa_k_norm, nsa_cmp_pos_k, nsa_cmp_pos_v, nsa_cmp_w1_k, nsa_cmp_w2_k, nsa_cmp_w1_v, nsa_cmp_w2_v, w_nsa_o, ssm_conv_w, ssm_conv_b, ssm_dt_bias, ssm_a_log, ssm_d, ssm_norm, w_ssm_o, hg_lb_logits, hg_norm, w_hg_o, w_out, norm_xa, norm_mem, xa_w_q, xa_w_k, xa_w_v, xa_q_norm, xa_k_norm, xa_w_o, norm_ffn, ffn_w_gate, ffn_w_up, ffn_w_down):
    bsz, s, d = x.shape
    depth = w_in.shape[0]
    assert d == D_MODEL and PREP_TILES * TOK == SEL_TILE
    assert s % SEL_TILE == 0 and s % (SEL_Q * TOK) == 0 and s // TOK >= WIN_TILES
    assert s // NSA_SEL_BLOCK >= NSA_TOPK and (s // NSA_CMP_STRIDE) % (4 * SUBLANES) == 0
    t = bsz * s
    hd = NSA_HEAD_DIM

    cos_t, sin_t = _rope_tables(jnp.arange(s))
    nbp = s // NSA_CMP_STRIDE
    cos_c, sin_c = _rope_tables(jnp.arange(nbp) * NSA_CMP_STRIDE + NSA_CMP_BLOCK - 1)
    tri = jnp.tril(jnp.ones((TOK, TOK), F32)).astype(BF16)
    rows = np.arange(LANES)[:, None]
    cols = np.arange(NSA_Q_WIDTH)[None, :]
    e8 = jnp.asarray(rows == cols // SSM_HEAD_DIM, F32).astype(BF16)
    li = np.arange(LANES)[:, None]
    lo = np.arange(LANES)[None, :]
    head_sum = jnp.asarray(li // hd == lo // hd, F32).astype(BF16)
    rot_half = jnp.asarray(np.where(lo % hd < hd // 2, -1.0 * (li == lo + hd // 2), 1.0 * (li == lo - hd // 2)),
                           F32).astype(BF16)

    lb_sm = jax.nn.softmax(hg_lb_logits.astype(F32), axis=0)
    lb_all = jnp.cumsum(lb_sm, axis=0) - lb_sm[0:1]

    def pad_lanes(v):
        return jnp.pad(v, (0, LANES - v.shape[0]))[None, :]

    x2 = x.reshape(t, d)
    for l in range(depth):
        gates, proj = _in_proj(x2, norm_mix[l][None, :], _pack_w_in(w_in[l]))

        qn2 = jnp.tile(nsa_q_norm[l], 2)[None, :]
        kn2 = jnp.tile(nsa_k_norm[l], 2)[None, :]
        qp, ksn, kwn, vst, vwt, gt = _nsa_prep(proj, cos_t, sin_t, qn2, kn2, head_sum, rot_half, bsz, s)
        w1k = _block_diag2(nsa_cmp_w1_k[l].reshape(NSA_CMP_BLOCK, hd, hd)).astype(BF16)
        w1v = _block_diag2(nsa_cmp_w1_v[l].reshape(NSA_CMP_BLOCK, hd, hd)).astype(BF16)
        kc, vct = _compress(proj, jnp.tile(nsa_cmp_pos_k[l], (1, 2)), jnp.tile(nsa_cmp_pos_v[l], (1, 2)),
                            w1k, _block_diag2(nsa_cmp_w2_k[l]).astype(BF16),
                            w1v, _block_diag2(nsa_cmp_w2_v[l]).astype(BF16), kn2, cos_c, sin_c, bsz, s)
        o_c, sel_bias = _cmp_topk(qp, kc, vct, gt, bsz, s)
        tile_list, tile_count = _active_tiles(sel_bias, bsz, s)
        o_s = _selected(qp, ksn, vst, sel_bias, gt, tile_list, tile_count, bsz, s)
        o_w = _window(qp, kwn, vwt, gt, bsz, s)

        y_b, y_c = _recurrent_mixers(
            proj, ssm_conv_w[l], ssm_conv_b[l][None, :], pad_lanes(ssm_dt_bias[l]), pad_lanes(ssm_a_log[l]),
            jnp.repeat(ssm_d[l], SSM_HEAD_DIM)[None, :], ssm_norm[l][None, :], e8,
            lb_all[l][None, :], hg_norm[l][None, :], tri, bsz, s)

        x2 = _merge(gates, o_c.reshape(t, -1), o_s.reshape(t, -1), o_w.reshape(t, -1), y_b, y_c, x2,
                    w_nsa_o[l].astype(BF16), w_ssm_o[l].astype(BF16), w_hg_o[l].astype(BF16),
                    w_out[l].astype(BF16))

        k_m, v_m = _xa_mem(mem, norm_mem[l][None, :], xa_w_k[l].astype(BF16), xa_w_v[l].astype(BF16),
                           xa_k_norm[l][None, :])
        x2 = _xa(x2, norm_xa[l][None, :], xa_w_q[l].astype(BF16), xa_q_norm[l][None, :], k_m, v_m,
                 xa_w_o[l].astype(BF16), s)
        x2 = _ffn(x2, norm_ffn[l][None, :], ffn_w_gate[l].astype(BF16), ffn_w_up[l].astype(BF16),
                  ffn_w_down[l].astype(BF16))
    return x2.reshape(bsz, s, d)
```

```python
import functools
import math

import numpy as np
import jax
import jax.numpy as jnp
from jax import lax
from jax.experimental import pallas as pl
from jax.experimental.pallas import tpu as pltpu

F32 = jnp.float32
BF16 = jnp.bfloat16

D_MODEL = 1024
NORM_EPS = 1e-6
ROPE_THETA = 10000.0
NEG_INF = -1e30

NSA_HEADS = 8
NSA_KV_GROUPS = 2
NSA_REP = NSA_HEADS // NSA_KV_GROUPS
NSA_HEAD_DIM = 64
NSA_CMP_BLOCK = 32
NSA_CMP_STRIDE = 16
NSA_SEL_BLOCK = 64
NSA_TOPK = 16
N_FORCED = 3
NSA_WINDOW = 512
NSA_Q_WIDTH = NSA_HEADS * NSA_HEAD_DIM
NSA_KV_WIDTH = NSA_KV_GROUPS * NSA_HEAD_DIM

SSM_INNER = 512
SSM_HEAD_DIM = 64
SSM_HEADS = SSM_INNER // SSM_HEAD_DIM
SSM_GROUPS = 2
SSM_STATE = 64
SSM_CONV = 4
SSM_CONV_DIM = SSM_INNER + 2 * SSM_GROUPS * SSM_STATE

HG_HEADS = 4
HG_KEY_DIM = 128
HG_VAL_DIM = 128
HG_WIDTH = HG_HEADS * HG_KEY_DIM
HG_SUB = 16

XA_HEADS = 4
XA_HEAD_DIM = 128
XA_WIDTH = XA_HEADS * XA_HEAD_DIM

FFN_HIDDEN = -(-(8 * D_MODEL) // (3 * 256)) * 256

SPLIT_SIZES = (
    NSA_Q_WIDTH, NSA_KV_WIDTH, NSA_KV_WIDTH, NSA_KV_WIDTH, NSA_KV_WIDTH, NSA_KV_WIDTH, NSA_KV_WIDTH,
    3 * NSA_HEADS, SSM_INNER, SSM_CONV_DIM, SSM_HEADS, HG_WIDTH, HG_WIDTH, HG_HEADS * HG_VAL_DIM,
    HG_HEADS * HG_VAL_DIM, D_MODEL, D_MODEL, D_MODEL,
)

LANES = 128
SUBLANES = 8
TOK = 128
NSA_BRANCHES = 3
GATE_ROWS = -(-NSA_BRANCHES * NSA_HEADS // SUBLANES) * SUBLANES
WIN_Q = 2
CMP_Q = 4
PREP_TILES = 4
SEL_Q = 4
SEL_TILE = 512
SEL_V_ROWS = NSA_HEAD_DIM + 16
WIN_TILES = NSA_WINDOW // TOK + 1
LOG2_E = math.log2(math.e)
MASK_FLOOR = -1e20
VMEM_LIMIT = 56 * 1024 * 1024

PROJ_WIDTH = 8192
GATE_WIDTH = 3 * D_MODEL
COL_GA, COL_GB, COL_GC = 0, 1024, 2048
COL_Q = 0
COL_Z = 512
COL_HQ, COL_HF, COL_HI, COL_HG = 1024, 1536, 2048, 2560
COL_XBC = 3072
COL_KC, COL_VC, COL_KS, COL_VS, COL_KW, COL_VW = 3840, 3968, 4096, 4224, 4352, 4480
COL_NG = 4608
COL_DT = 4736


def _cparams(sem):
    return pltpu.CompilerParams(dimension_semantics=sem, vmem_limit_bytes=VMEM_LIMIT)


def _dot(a, b):
    return jnp.dot(a, b, preferred_element_type=F32)


def _dot_nt(a, b):
    return lax.dot_general(a, b, (((1,), (1,)), ((), ())), preferred_element_type=F32)


def _split3(a):
    hi = a.astype(BF16)
    r1 = a - hi.astype(F32)
    mid = r1.astype(BF16)
    lo = (r1 - mid.astype(F32)).astype(BF16)
    return hi, mid, lo


def _dot_sel_r(a, sel):
    hi, mid, lo = _split3(a)
    return _dot(hi, sel) + _dot(mid, sel) + _dot(lo, sel)


def _dot_sel_l(sel, a):
    hi, mid, lo = _split3(a)
    return _dot(sel, hi) + _dot(sel, mid) + _dot(sel, lo)


def _silu(x):
    return x * jax.nn.sigmoid(x)


def _rms(x, w):
    return x * lax.rsqrt(jnp.mean(x * x, axis=-1, keepdims=True) + NORM_EPS) * w


def _inproj_kernel(x_ref, nw_ref, w_ref, og_ref, of_ref, h_ref, *, n_gate):
    j = pl.program_id(1)

    @pl.when(j == 0)
    def _():
        h_ref[...] = _rms(x_ref[...], nw_ref[...]).astype(BF16)

    @pl.when(j < n_gate)
    def _():
        og_ref[...] = _dot(h_ref[...], w_ref[...]).astype(BF16)

    @pl.when(j >= n_gate)
    def _():
        of_ref[...] = _dot(h_ref[...], w_ref[...])


def _in_proj(x2, norm_w, w_packed):
    t = x2.shape[0]
    tm = min(2048, t)
    tn = 1024
    n_gate = GATE_WIDTH // tn
    return pl.pallas_call(
        functools.partial(_inproj_kernel, n_gate=n_gate),
        grid=(t // tm, PROJ_WIDTH // tn),
        in_specs=[
            pl.BlockSpec((tm, D_MODEL), lambda i, j: (i, 0)),
            pl.BlockSpec((1, D_MODEL), lambda i, j: (0, 0)),
            pl.BlockSpec((D_MODEL, tn), lambda i, j: (0, j)),
        ],
        out_specs=[pl.BlockSpec((tm, tn), lambda i, j: (i, jnp.minimum(j, n_gate - 1))),
                   pl.BlockSpec((tm, tn), lambda i, j: (i, jnp.maximum(j - n_gate, 0)))],
        out_shape=[jax.ShapeDtypeStruct((t, GATE_WIDTH), BF16),
                   jax.ShapeDtypeStruct((t, PROJ_WIDTH - GATE_WIDTH), F32)],
        scratch_shapes=[pltpu.VMEM((tm, D_MODEL), BF16)],
        compiler_params=_cparams(("parallel", "arbitrary")),
        name="in_proj",
    )(x2, norm_w, w_packed)


def _norm_rope_pair(x, w, cos, sin):
    lane = lax.broadcasted_iota(jnp.int32, x.shape, 1)
    lo = lane < NSA_HEAD_DIM
    sq = x * x
    s_lo = jnp.sum(jnp.where(lo, sq, 0.0), axis=1, keepdims=True)
    s_hi = jnp.sum(jnp.where(lo, 0.0, sq), axis=1, keepdims=True)
    ms = jnp.where(lo, s_lo, s_hi) * (1.0 / NSA_HEAD_DIM)
    y = x * lax.rsqrt(ms + NORM_EPS) * w
    half = NSA_HEAD_DIM // 2
    first = (lane % NSA_HEAD_DIM) < half
    rot = jnp.where(first, -pltpu.roll(y, LANES - half, 1), pltpu.roll(y, half, 1))
    return y * cos + rot * sin


def _norm_rope_pair_mxu(x, w, cos, sin, head_sum, rot_half):
    ms = _dot_sel_r(x * x, head_sum) * (1.0 / NSA_HEAD_DIM)
    y = x * lax.rsqrt(ms + NORM_EPS) * w
    return y * cos + _dot_sel_r(y, rot_half) * sin


def _nsa_prep_kernel(q_ref, ks_ref, vs_ref, kw_ref, vw_ref, ng_ref, cos_ref, sin_ref, qn_ref, kn_ref,
                     hsum_ref, rot_ref, qp_ref, ksn_ref, kwn_ref, vst_ref, vwt_ref, gt_ref):
    norm_rope = functools.partial(_norm_rope_pair_mxu, head_sum=hsum_ref[...], rot_half=rot_ref[...])
    lane = lax.broadcasted_iota(jnp.int32, (TOK, LANES), 1)
    row = lax.broadcasted_iota(jnp.int32, (TOK, LANES), 0)
    scale = NSA_HEAD_DIM ** -0.5 * LOG2_E
    ones_rows = (lax.broadcasted_iota(jnp.int32, (SEL_V_ROWS - NSA_HEAD_DIM, TOK), 0) == 0).astype(BF16)
    for u in range(PREP_TILES):
        rows = slice(u * TOK, (u + 1) * TOK)
        cos = cos_ref[rows, :]
        sin = sin_ref[rows, :]
        for c in range(NSA_HEADS // 2):
            y = norm_rope(q_ref[rows, c * LANES:(c + 1) * LANES], qn_ref[...], cos, sin) * scale
            y_sw = pltpu.roll(y, NSA_HEAD_DIM, 1)
            for hh in range(2):
                h = 2 * c + hh
                g, r = divmod(h, NSA_REP)
                src = y if hh == g else y_sw
                keep = (lane // NSA_HEAD_DIM) == g
                qp_ref[0, g, rows, r * LANES:(r + 1) * LANES] = jnp.where(keep, src, 0.0).astype(BF16)
        ks_n = norm_rope(ks_ref[rows, :], kn_ref[...], cos, sin)
        tile = pl.program_id(1) * PREP_TILES + u
        blk = (tile * (TOK // NSA_SEL_BLOCK) + row // NSA_SEL_BLOCK) % NSA_HEAD_DIM
        onehot = ((lane % NSA_HEAD_DIM) == blk).astype(F32)
        for g in range(NSA_KV_GROUPS):
            ksn_ref[0, g, rows, :] = jnp.where((lane // NSA_HEAD_DIM) == g, ks_n, onehot).astype(BF16)
        kwn_ref[0, rows, :] = norm_rope(kw_ref[rows, :], kn_ref[...], cos, sin).astype(BF16)
        vs_t = vs_ref[rows, :].T.astype(BF16)
        vw_t = vw_ref[rows, :].T.astype(BF16)
        for g in range(NSA_KV_GROUPS):
            vst_ref[0, g, 0, 0:NSA_HEAD_DIM, rows] = vs_t[g * NSA_HEAD_DIM:(g + 1) * NSA_HEAD_DIM, :]
            vst_ref[0, g, 0, NSA_HEAD_DIM:SEL_V_ROWS, rows] = ones_rows
            vwt_ref[0, g, u] = vw_t[g * NSA_HEAD_DIM:(g + 1) * NSA_HEAD_DIM, :]
        gt_ref[0, u] = jax.nn.sigmoid(ng_ref[rows, :]).T[0:GATE_ROWS, :]


def _nsa_prep(proj, cos_t, sin_t, qn2, kn2, head_sum, rot_half, bsz, s):
    tq = PREP_TILES * TOK
    n_step = s // tq

    def col(c0, w):
        return pl.BlockSpec((tq, w), lambda b, i: (b * n_step + i, c0 // w))

    tab = pl.BlockSpec((tq, LANES), lambda b, i: (i, 0))
    vec = pl.BlockSpec((1, LANES), lambda b, i: (0, 0))
    sq = pl.BlockSpec((LANES, LANES), lambda b, i: (0, 0))
    return pl.pallas_call(
        _nsa_prep_kernel,
        grid=(bsz, n_step),
        in_specs=[col(COL_Q, NSA_Q_WIDTH), col(COL_KS, LANES), col(COL_VS, LANES), col(COL_KW, LANES),
                  col(COL_VW, LANES), col(COL_NG, LANES), tab, tab, vec, vec, sq, sq],
        out_specs=[
            pl.BlockSpec((1, NSA_KV_GROUPS, tq, NSA_REP * LANES), lambda b, i: (b, 0, i, 0)),
            pl.BlockSpec((1, NSA_KV_GROUPS, tq, LANES), lambda b, i: (b, 0, i, 0)),
            pl.BlockSpec((1, tq, LANES), lambda b, i: (b, i, 0)),
            pl.BlockSpec((1, NSA_KV_GROUPS, 1, SEL_V_ROWS, SEL_TILE), lambda b, i: (b, 0, i, 0, 0)),
            pl.BlockSpec((1, NSA_KV_GROUPS, PREP_TILES, NSA_HEAD_DIM, TOK), lambda b, i: (b, 0, i, 0, 0)),
            pl.BlockSpec((1, PREP_TILES, GATE_ROWS, TOK), lambda b, i: (b, i, 0, 0)),
        ],
        out_shape=[
            jax.ShapeDtypeStruct((bsz, NSA_KV_GROUPS, s, NSA_REP * LANES), BF16),
            jax.ShapeDtypeStruct((bsz, NSA_KV_GROUPS, s, LANES), BF16),
            jax.ShapeDtypeStruct((bsz, s, LANES), BF16),
            jax.ShapeDtypeStruct((bsz, NSA_KV_GROUPS, s // SEL_TILE, SEL_V_ROWS, SEL_TILE), BF16),
            jax.ShapeDtypeStruct((bsz, NSA_KV_GROUPS, s // TOK, NSA_HEAD_DIM, TOK), BF16),
            jax.ShapeDtypeStruct((bsz, s // TOK, GATE_ROWS, TOK), F32),
        ],
        compiler_params=_cparams(("parallel", "parallel")),
        name="nsa_prep",
    )(proj, proj, proj, proj, proj, proj, cos_t, sin_t, qn2, kn2, head_sum, rot_half)


def _compress_kernel(kc_ref, vc_ref, posk_ref, posv_ref, w1k_ref, w2k_ref, w1v_ref, w2v_ref, kn_ref,
                     cos_ref, sin_ref, kco_ref, vct_ref, sh_ref, *, nbp):
    half_blk = NSA_CMP_BLOCK // 2

    def compress(x_ref, pos_ref, w1_ref, w2_ref):
        acc_a = jnp.zeros((nbp, LANES), F32)
        acc_b = jnp.zeros((nbp, LANES), F32)
        for l in range(half_blk):
            xl = x_ref[pl.ds(l, nbp, stride=NSA_CMP_STRIDE), :]
            acc_a = acc_a + _dot((xl + pos_ref[l:l + 1, :]).astype(BF16), w1_ref[l])
            acc_b = acc_b + _dot((xl + pos_ref[half_blk + l:half_blk + l + 1, :]).astype(BF16),
                                 w1_ref[half_blk + l])
        sh_ref[pl.ds(0, nbp), :] = acc_b
        sh_ref[pl.ds(nbp, SUBLANES), :] = jnp.zeros((SUBLANES, LANES), F32)
        hid = acc_a + sh_ref[pl.ds(1, nbp), :]
        return _dot(_silu(hid).astype(BF16), w2_ref[...])

    kc = compress(kc_ref, posk_ref, w1k_ref, w2k_ref)
    kco_ref[0] = _norm_rope_pair(kc, kn_ref[...], cos_ref[...], sin_ref[...]).astype(BF16)
    vc = compress(vc_ref, posv_ref, w1v_ref, w2v_ref)
    vct_ref[0] = vc.T.astype(BF16)


def _compress(proj, posk, posv, w1k, w2k, w1v, w2v, kn2, cos_c, sin_c, bsz, s):
    nbp = s // NSA_CMP_STRIDE

    def full(shape):
        return pl.BlockSpec(shape, lambda b: (0,) * len(shape))

    return pl.pallas_call(
        functools.partial(_compress_kernel, nbp=nbp),
        grid=(bsz,),
        in_specs=[
            pl.BlockSpec((s, LANES), lambda b: (b, COL_KC // LANES)),
            pl.BlockSpec((s, LANES), lambda b: (b, COL_VC // LANES)),
            full((NSA_CMP_BLOCK, LANES)), full((NSA_CMP_BLOCK, LANES)),
            full((NSA_CMP_BLOCK, LANES, LANES)), full((LANES, LANES)),
            full((NSA_CMP_BLOCK, LANES, LANES)), full((LANES, LANES)),
            full((1, LANES)), full((nbp, LANES)), full((nbp, LANES)),
        ],
        out_specs=[
            pl.BlockSpec((1, nbp, LANES), lambda b: (b, 0, 0)),
            pl.BlockSpec((1, LANES, nbp), lambda b: (b, 0, 0)),
        ],
        out_shape=[
            jax.ShapeDtypeStruct((bsz, nbp, LANES), BF16),
            jax.ShapeDtypeStruct((bsz, LANES, nbp), BF16),
        ],
        scratch_shapes=[pltpu.VMEM((nbp + SUBLANES, LANES), F32)],
        compiler_params=_cparams(("parallel",)),
        name="nsa_compress",
    )(proj, proj, posk, posv, w1k, w2k, w1v, w2v, kn2, cos_c, sin_c)


def _q_rows(qp):
    return jnp.concatenate([qp[:, r * LANES:(r + 1) * LANES] for r in range(NSA_REP)], axis=0)


def _gate_row(gt, g, branch):
    rows = [(g * NSA_REP + r) * NSA_BRANCHES + branch for r in range(NSA_REP)]
    return jnp.concatenate([gt[i:i + 1, :] for i in rows], axis=1)


def _heads_to_token_major(acc_g):
    outs = []
    for p in range(NSA_REP // 2):
        blk = jnp.concatenate([acc_g[:, (2 * p) * TOK:(2 * p + 1) * TOK],
                               acc_g[:, (2 * p + 1) * TOK:(2 * p + 2) * TOK]], axis=0)
        outs.append(blk.T)
    return jnp.concatenate(outs, axis=1)


def _cmp_topk_kernel(qp_ref, kc_ref, vct_ref, gt_ref, oc_ref, mask_ref, ps_ref, *, nbp, nsel):
    qi = pl.program_id(1)
    ncol = NSA_REP * TOK
    per = NSA_SEL_BLOCK // NSA_CMP_STRIDE

    def attend(nrows):
        nblk = nrows // per
        units = [(h, g) for h in range(CMP_Q) for g in range(NSA_KV_GROUPS)]
        j_io = lax.broadcasted_iota(jnp.int32, (nblk, TOK), 0)
        rel = (lax.broadcasted_iota(jnp.int32, (nrows, ncol), 0) * NSA_CMP_STRIDE + (NSA_CMP_BLOCK - 1)
               - lax.broadcasted_iota(jnp.int32, (nrows, ncol), 1) % TOK)
        kc = kc_ref[0, 0:nrows, :]
        vct = vct_ref[0, :, 0:nrows]
        forced, valid, allowed = [], [], []
        for h in range(CMP_Q):
            t_sel = (qi * CMP_Q + h) * TOK + lax.broadcasted_iota(jnp.int32, (nblk, TOK), 1)
            cur = t_sel // NSA_SEL_BLOCK
            forced.append((j_io == 0) | (j_io == cur) | (j_io == cur - 1))
            valid.append(j_io * NSA_SEL_BLOCK <= t_sel)
            allowed.append(rel <= (qi * CMP_Q + h) * TOK)
        for h, g in units:
            ps_ref[h, g, pl.ds(0, SUBLANES), :] = jnp.zeros((SUBLANES, TOK), F32)
        scores = [_dot_nt(kc, _q_rows(qp_ref[0, g, h * TOK:(h + 1) * TOK, :])) for h, g in units]
        outs = {}
        imps = []
        for u, (h, g) in enumerate(units):
            s_t = jnp.where(allowed[h], scores[u], NEG_INF)
            m = jnp.max(s_t, axis=0, keepdims=True)
            p = jnp.exp2(s_t - jnp.maximum(m, MASK_FLOOR))
            l = jnp.sum(p, axis=0, keepdims=True)
            inv = jnp.where(l > 0.0, 1.0 / l, 0.0)
            o_t = _dot(vct, p.astype(BF16)) * inv
            gate = _gate_row(gt_ref[0, h], g, 0)
            outs[h, g] = _heads_to_token_major(o_t[g * NSA_HEAD_DIM:(g + 1) * NSA_HEAD_DIM, :] * gate)
            pn = p * inv
            psum = pn[:, 0:TOK]
            for r in range(1, NSA_REP):
                psum = psum + pn[:, r * TOK:(r + 1) * TOK]
            ps_ref[h, g, pl.ds(SUBLANES, nrows), :] = psum
            imp = (ps_ref[h, g, pl.ds(SUBLANES - 1, nblk, stride=per), :]
                   + ps_ref[h, g, pl.ds(SUBLANES + per - 1, nblk, stride=per), :])
            for c in range(per - 1):
                imp = imp + 2.0 * ps_ref[h, g, pl.ds(SUBLANES + c, nblk, stride=per), :]
            imps.append(jnp.where(forced[h], -jnp.inf, jnp.where(valid[h], imp, NEG_INF)))
        for h in range(CMP_Q):
            oc_ref[0, h * TOK:(h + 1) * TOK, :] = jnp.concatenate(
                [outs[h, g] for g in range(NSA_KV_GROUPS)], axis=1)

        def pick(_, carry):
            nxt = []
            for imp_c in carry:
                mx = jnp.max(imp_c, axis=0, keepdims=True)
                idx = jnp.min(jnp.where(imp_c == mx, j_io, nblk), axis=0, keepdims=True)
                nxt.append(jnp.where(j_io == idx, -jnp.inf, imp_c))
            return tuple(nxt)

        picked = lax.fori_loop(0, min(NSA_TOPK, nsel) - N_FORCED, pick, tuple(imps))
        for u, (h, g) in enumerate(units):
            mask_ref[0, g, h, 0:nblk, :] = jnp.where(valid[h] & (picked[u] == -jnp.inf), 0.0, NEG_INF)
            if nblk < nsel:
                mask_ref[0, g, h, nblk:nsel, :] = jnp.full((nsel - nblk, TOK), NEG_INF, F32)

    n_vis = ((qi + 1) * CMP_Q * TOK - NSA_CMP_BLOCK) // NSA_CMP_STRIDE + 1
    quarter = nbp // 4
    for k in range(1, 5):
        @pl.when((n_vis > (k - 1) * quarter) & (n_vis <= k * quarter))
        def _(k=k):
            attend(k * quarter)


def _cmp_topk(qp, kc, vct, gt, bsz, s):
    nt = s // TOK
    nbp = s // NSA_CMP_STRIDE
    nsel = s // NSA_SEL_BLOCK
    return pl.pallas_call(
        functools.partial(_cmp_topk_kernel, nbp=nbp, nsel=nsel),
        grid=(bsz, nt // CMP_Q),
        in_specs=[
            pl.BlockSpec((1, NSA_KV_GROUPS, CMP_Q * TOK, NSA_REP * LANES), lambda b, i: (b, 0, i, 0)),
            pl.BlockSpec((1, nbp, LANES), lambda b, i: (b, 0, 0)),
            pl.BlockSpec((1, LANES, nbp), lambda b, i: (b, 0, 0)),
            pl.BlockSpec((1, CMP_Q, GATE_ROWS, TOK), lambda b, i: (b, i, 0, 0)),
        ],
        out_specs=[
            pl.BlockSpec((1, CMP_Q * TOK, NSA_Q_WIDTH), lambda b, i: (b, i, 0)),
            pl.BlockSpec((1, NSA_KV_GROUPS, CMP_Q, nsel, TOK), lambda b, i: (b, 0, i, 0, 0)),
        ],
        out_shape=[
            jax.ShapeDtypeStruct((bsz, s, NSA_Q_WIDTH), F32),
            jax.ShapeDtypeStruct((bsz, NSA_KV_GROUPS, nt, nsel, TOK), F32),
        ],
        scratch_shapes=[pltpu.VMEM((CMP_Q, NSA_KV_GROUPS, nbp + SUBLANES, TOK), F32)],
        compiler_params=_cparams(("parallel", "parallel")),
        name="nsa_cmp_topk",
    )(qp, kc, vct, gt)


def _window_kernel(qp_ref, k_ref, vt_ref, gt_ref, o_ref):
    step = pl.program_id(1)
    ncol = NSA_REP * TOK
    wk = WIN_TILES * TOK
    units = [(h, g) for h in range(WIN_Q) for g in range(NSA_KV_GROUPS)]

    def run(interior):
        first = [step * WIN_Q + h - (WIN_TILES - 1) if interior else 0 for h in range(WIN_Q)]
        kwin = [k_ref[0, pl.ds(pl.multiple_of(first[h] * TOK, TOK), wk), :] for h in range(WIN_Q)]
        scores = [_dot_nt(kwin[h], _q_rows(qp_ref[0, g, h * TOK:(h + 1) * TOK, :])) for h, g in units]
        probs = []
        scales = []
        for u, (h, g) in enumerate(units):
            s_t = scores[u]
            qi = step * WIN_Q + h
            first_tile = first[h]
            if interior:
                row = lax.broadcasted_iota(jnp.int32, (TOK, ncol), 0)
                col = lax.broadcasted_iota(jnp.int32, (TOK, ncol), 1) % TOK
                oldest = jnp.where(row > col, s_t[:TOK], NEG_INF)
                newest = jnp.where(row <= col, s_t[wk - TOK:], NEG_INF)
                s_t = jnp.concatenate([oldest, s_t[TOK:wk - TOK], newest], axis=0)
            else:
                key = first_tile * TOK + lax.broadcasted_iota(jnp.int32, (wk, ncol), 0)
                t_io = qi * TOK + lax.broadcasted_iota(jnp.int32, (wk, ncol), 1) % TOK
                s_t = jnp.where((key <= t_io) & (t_io - key < NSA_WINDOW), s_t, NEG_INF)
            m = jnp.max(s_t, axis=0, keepdims=True)
            p = jnp.exp2(s_t - m)
            l = jnp.sum(p, axis=0, keepdims=True)
            probs.append(p.astype(BF16))
            scales.append(_gate_row(gt_ref[0, h], g, 2) / l)
        outs = {}
        for u, (h, g) in enumerate(units):
            vt = jnp.concatenate([vt_ref[0, g, first[h] + k] for k in range(WIN_TILES)], axis=1)
            outs[h, g] = _heads_to_token_major(_dot(vt, probs[u]) * scales[u])
        for h in range(WIN_Q):
            o_ref[0, h * TOK:(h + 1) * TOK, :] = jnp.concatenate(
                [outs[h, g] for g in range(NSA_KV_GROUPS)], axis=1)

    @pl.when(step * WIN_Q >= WIN_TILES - 1)
    def _():
        run(True)

    @pl.when(step * WIN_Q < WIN_TILES - 1)
    def _():
        run(False)


def _window(qp, k, vt, gt, bsz, s):
    nt = s // TOK
    return pl.pallas_call(
        _window_kernel,
        grid=(bsz, nt // WIN_Q),
        in_specs=[
            pl.BlockSpec((1, NSA_KV_GROUPS, WIN_Q * TOK, NSA_REP * LANES), lambda b, i: (b, 0, i, 0)),
            pl.BlockSpec((1, s, LANES), lambda b, i: (b, 0, 0)),
            pl.BlockSpec((1, NSA_KV_GROUPS, nt, NSA_HEAD_DIM, TOK), lambda b, i: (b, 0, 0, 0, 0)),
            pl.BlockSpec((1, WIN_Q, GATE_ROWS, TOK), lambda b, i: (b, i, 0, 0)),
        ],
        out_specs=pl.BlockSpec((1, WIN_Q * TOK, NSA_Q_WIDTH), lambda b, i: (b, i, 0)),
        out_shape=jax.ShapeDtypeStruct((bsz, s, NSA_Q_WIDTH), F32),
        compiler_params=_cparams(("parallel", "parallel")),
        name="nsa_window",
    )(qp, k, vt, gt)


def _selected_kernel(list_ref, cnt_ref, qp_ref, k_ref, vt_ref, bias_ref, gt_ref, o_ref, m_ref, acc_ref,
                     s_ref, p_ref, a_ref, qa_ref, *, nt, nsel):
    b = pl.program_id(0)
    qi = pl.program_id(1)
    ncol = NSA_REP * TOK
    blk_per_tile = SEL_TILE // NSA_SEL_BLOCK
    n_kt = nt // (SEL_TILE // TOK)
    n_q = nt // SEL_Q
    units = [(g, h) for g in range(NSA_KV_GROUPS) for h in range(SEL_Q)]
    m_ref[...] = jnp.full(m_ref.shape, NEG_INF, F32)
    acc_ref[...] = jnp.zeros(acc_ref.shape, F32)
    p_ref[...] = jnp.zeros(p_ref.shape, BF16)
    a_ref[...] = jnp.ones(a_ref.shape, F32)
    base = (b * n_q + qi) * n_kt
    count = cnt_ref[b * n_q + qi]
    n_diag = (qi * SEL_Q * TOK) // SEL_TILE

    lane = lax.broadcasted_iota(jnp.int32, (TOK, LANES), 1)
    n_var = qa_ref.shape[0]
    for g, h in units:
        bias_g = bias_ref[0, g, h]
        if nsel % LANES:
            bias_g = jnp.concatenate([bias_g, jnp.zeros((LANES - nsel % LANES, TOK), F32)], axis=0)
        own = (lane // NSA_HEAD_DIM) == g
        for v in range(n_var):
            src = bias_g[(v // 2) * LANES:(v // 2 + 1) * LANES, :].T
            if v % 2 != 1 - g:
                src = pltpu.roll(src, NSA_HEAD_DIM, 1)
            src = src.astype(BF16)
            for r in range(NSA_REP):
                q_r = qp_ref[0, g, h * TOK:(h + 1) * TOK, r * LANES:(r + 1) * LANES]
                qa_ref[v, g, h, r * TOK:(r + 1) * TOK, :] = jnp.where(own, q_r, src)

    def tile_at(i):
        return jnp.where(i < count, list_ref[base + jnp.minimum(i, n_kt - 1)], n_diag)

    def scores(j):
        v = (j * blk_per_tile) // NSA_HEAD_DIM
        for g in range(NSA_KV_GROUPS):
            kb = k_ref[0, g, pl.ds(pl.multiple_of(j * SEL_TILE, SEL_TILE), SEL_TILE), :]
            for h in range(SEL_Q):
                s_ref[g, h] = _dot_nt(kb, qa_ref[v, g, h])

    def values(j):
        for g, h in units:
            acc_ref[g, h] = acc_ref[g, h] * a_ref[g, h] + _dot(vt_ref[0, g, j], p_ref[g, h])

    def softmax(j, causal):
        for g, h in units:
            s_t = s_ref[g, h]
            if causal:
                t_io = ((qi * SEL_Q + h) * TOK
                        + lax.broadcasted_iota(jnp.int32, (SEL_TILE, ncol), 1) % TOK)
                key_pos = j * SEL_TILE + lax.broadcasted_iota(jnp.int32, (SEL_TILE, ncol), 0)
                s_t = jnp.where(key_pos <= t_io, s_t, NEG_INF)
            m_old = m_ref[g, h]
            m_new = jnp.maximum(m_old, jnp.max(s_t, axis=0, keepdims=True))
            alpha = jnp.exp2(m_old - m_new)
            p = jnp.exp2(s_t - jnp.maximum(m_new, MASK_FLOOR))
            m_ref[g, h] = m_new
            p_ref[g, h] = p.astype(BF16)
            a_ref[g, h] = alpha

    scores(tile_at(0))

    def body(i, c):
        values(tile_at(jnp.maximum(i - 1, 0)))
        softmax(tile_at(i), False)
        scores(tile_at(i + 1))
        return c

    lax.fori_loop(0, count, body, 0)
    values(tile_at(jnp.maximum(count - 1, 0)))
    softmax(n_diag, True)
    values(n_diag)
    for h in range(SEL_Q):
        outs = []
        for g in range(NSA_KV_GROUPS):
            acc = acc_ref[g, h]
            gate = _gate_row(gt_ref[0, h], g, 1)
            outs.append(_heads_to_token_major(acc[:NSA_HEAD_DIM] * (gate / acc[NSA_HEAD_DIM:NSA_HEAD_DIM + 1])))
        o_ref[0, h * TOK:(h + 1) * TOK, :] = jnp.concatenate(outs, axis=1)


def _selected(qp, k, vt, bias, gt, tile_list, tile_count, bsz, s):
    nt = s // TOK
    nsel = s // NSA_SEL_BLOCK
    ncol = NSA_REP * TOK
    unit = (NSA_KV_GROUPS, SEL_Q)
    grid_spec = pltpu.PrefetchScalarGridSpec(
        num_scalar_prefetch=2,
        grid=(bsz, nt // SEL_Q),
        in_specs=[
            pl.BlockSpec((1, NSA_KV_GROUPS, SEL_Q * TOK, NSA_REP * LANES), lambda b, i, tl, tc: (b, 0, i, 0)),
            pl.BlockSpec((1, NSA_KV_GROUPS, s, LANES), lambda b, i, tl, tc: (b, 0, 0, 0)),
            pl.BlockSpec((1, NSA_KV_GROUPS, s // SEL_TILE, SEL_V_ROWS, SEL_TILE),
                         lambda b, i, tl, tc: (b, 0, 0, 0, 0)),
            pl.BlockSpec((1, NSA_KV_GROUPS, SEL_Q, nsel, TOK), lambda b, i, tl, tc: (b, 0, i, 0, 0)),
            pl.BlockSpec((1, SEL_Q, GATE_ROWS, TOK), lambda b, i, tl, tc: (b, i, 0, 0)),
        ],
        out_specs=pl.BlockSpec((1, SEL_Q * TOK, NSA_Q_WIDTH), lambda b, i, tl, tc: (b, i, 0)),
        scratch_shapes=[pltpu.VMEM(unit + (1, ncol), F32),
                        pltpu.VMEM(unit + (SEL_V_ROWS, ncol), F32),
                        pltpu.VMEM(unit + (SEL_TILE, ncol), F32),
                        pltpu.VMEM(unit + (SEL_TILE, ncol), BF16),
                        pltpu.VMEM(unit + (1, ncol), F32),
                        pltpu.VMEM((max(nsel // NSA_HEAD_DIM, 1),) + unit + (ncol, LANES), BF16)],
    )
    return pl.pallas_call(
        functools.partial(_selected_kernel, nt=nt, nsel=nsel),
        grid_spec=grid_spec,
        out_shape=jax.ShapeDtypeStruct((bsz, s, NSA_Q_WIDTH), F32),
        compiler_params=_cparams(("parallel", "parallel")),
        name="nsa_selected",
    )(tile_list, tile_count, qp, k, vt, bias, gt)


def _active_tiles(bias, bsz, s):
    n_q = s // (SEL_Q * TOK)
    n_kt = s // SEL_TILE
    rows = SEL_TILE // NSA_SEL_BLOCK * TOK
    act = bias.reshape(bsz, NSA_KV_GROUPS, n_q, SEL_Q, n_kt, rows).max(axis=(1, 3, 5)) > -1.0
    n_diag = (jnp.arange(n_q) * SEL_Q * TOK) // SEL_TILE
    act = act & (jnp.arange(n_kt)[None, None, :] < n_diag[None, :, None])
    order = jnp.argsort(jnp.where(act, 0, 1), axis=-1, stable=True).astype(jnp.int32)
    return order.reshape(-1), jnp.sum(act, axis=-1, dtype=jnp.int32).reshape(-1)


def _ssd_kernel(z_ref, xbc_ref, dt_ref, cw_ref, cb_ref, dtb_ref, alog_ref, dskip_ref, nw_ref, e8_ref,
                tri_ref, y_ref, ext_ref, st_ref):
    @pl.when(pl.program_id(0) == 0)
    def _():
        ext_ref[pl.ds(0, SUBLANES), :] = jnp.zeros((SUBLANES, SSM_CONV_DIM), F32)
        st_ref[...] = jnp.zeros_like(st_ref)

    xb = xbc_ref[...]
    ext_ref[pl.ds(SUBLANES, TOK), :] = xb
    conv = cb_ref[...] + xb * cw_ref[SSM_CONV - 1:SSM_CONV, :]
    for k in range(SSM_CONV - 1):
        conv = conv + ext_ref[pl.ds(SUBLANES - (SSM_CONV - 1) + k, TOK), :] * cw_ref[k:k + 1, :]
    ext_ref[pl.ds(0, SUBLANES), :] = xb[TOK - SUBLANES:TOK, :]
    u = _silu(conv)
    xs = u[:, :SSM_INNER]
    gn = SSM_GROUPS * SSM_STATE
    bm = u[:, SSM_INNER:SSM_INNER + gn]
    cm = u[:, SSM_INNER + gn:]
    dt = jax.nn.softplus(dt_ref[...] + dtb_ref[...])
    da = dt * (-jnp.exp(alog_ref[...]))
    tri = tri_ref[...]
    e8 = e8_ref[...]
    cum = _dot_sel_l(tri, da)
    cum_t = cum.T
    cum_e = _dot_sel_r(cum, e8)
    dt_e = _dot_sel_r(dt, e8)
    xdt = xs * dt_e
    lane = lax.broadcasted_iota(jnp.int32, (TOK, LANES), 1)
    t_io = lax.broadcasted_iota(jnp.int32, (TOK, TOK), 0)
    s_io = lax.broadcasted_iota(jnp.int32, (TOK, TOK), 1)
    causal = s_io <= t_io
    hg = SSM_HEADS // SSM_GROUPS
    y_parts = []
    cbs = []
    for g in range(SSM_GROUPS):
        in_g = (lane // SSM_STATE) == g
        cbs.append(_dot_nt(jnp.where(in_g, cm, 0.0).astype(BF16), bm.astype(BF16)))
    for c in range(SSM_HEADS // 2):
        acc = jnp.zeros((TOK, LANES), F32)
        xpair = xdt[:, c * LANES:(c + 1) * LANES]
        for hh in range(2):
            h = 2 * c + hh
            seg = cum[:, h:h + 1] - cum_t[h:h + 1, :]
            decay = jnp.where(causal, jnp.exp(jnp.where(causal, seg, 0.0)), 0.0)
            mat = (cbs[h // hg] * decay).astype(BF16)
            xh = jnp.where((lane // SSM_HEAD_DIM) == hh, xpair, 0.0).astype(BF16)
            acc = acc + _dot(mat, xh)
        y_parts.append(acc)
    y_diag = jnp.concatenate(y_parts, axis=1)
    prev = st_ref[...]
    y_off = _dot(cm.astype(BF16), prev.astype(BF16)) * jnp.exp(cum_e)
    cum_last = cum_e[TOK - 1:TOK, :]
    xdec = (xdt * jnp.exp(cum_last - cum_e)).astype(BF16)
    new = _dot(bm.T.astype(BF16), xdec)
    row_g = lax.broadcasted_iota(jnp.int32, (LANES, SSM_INNER), 0) // SSM_STATE
    col_g = lax.broadcasted_iota(jnp.int32, (LANES, SSM_INNER), 1) // (hg * SSM_HEAD_DIM)
    st_ref[...] = prev * jnp.exp(cum_last) + jnp.where(row_g == col_g, new, 0.0)
    y = (y_diag + y_off + xs * dskip_ref[...]) * _silu(z_ref[...])
    gw = SSM_INNER // SSM_GROUPS
    outs = []
    for g in range(SSM_GROUPS):
        yg = y[:, g * gw:(g + 1) * gw]
        outs.append(_rms(yg, nw_ref[:, g * gw:(g + 1) * gw]))
    y_ref[...] = jnp.concatenate(outs, axis=1).astype(BF16)


SSD_INPUTS = 11


def _hgrn2_kernel(q_ref, f_ref, i_ref, g_ref, lb_ref, nw_ref, tri_ref, y_ref, st_ref, kp_ref, bp_ref, vp_ref):
    @pl.when(pl.program_id(0) == 0)
    def _():
        st_ref[...] = jnp.zeros_like(st_ref)
        kp_ref[...] = jnp.zeros_like(kp_ref)
        bp_ref[...] = jnp.zeros_like(bp_ref)
        vp_ref[...] = jnp.zeros_like(vp_ref)

    nsub = TOK // HG_SUB
    in_sub = lax.broadcasted_iota(jnp.int32, (TOK, HG_KEY_DIM), 0) % HG_SUB
    tri = tri_ref[...]
    outs = []
    for h in range(HG_HEADS):
        sl = slice(h * HG_KEY_DIM, (h + 1) * HG_KEY_DIM)
        lb = lb_ref[:, sl]
        fr = f_ref[:, sl]
        q = _silu(q_ref[:, sl]) * (HG_KEY_DIM ** -0.5)
        log_f = jnp.log(lb + (1.0 - lb) * jax.nn.sigmoid(fr))
        k = (1.0 - lb) * jax.nn.sigmoid(-fr)
        v = i_ref[:, sl]
        bcum = _dot_sel_l(tri, log_f)
        st = st_ref[h]
        o = _dot_nt((q * jnp.exp(bcum)).astype(BF16), st.astype(BF16))
        ends = [bcum[(j + 1) * HG_SUB - 1:(j + 1) * HG_SUB, :] for j in range(nsub)]
        blk = (HG_SUB, HG_KEY_DIM)
        zero_blk = jnp.zeros(blk, BF16)
        e_prev = jnp.concatenate([jnp.zeros(blk, F32)] + [jnp.broadcast_to(ends[j], blk) for j in range(nsub - 1)],
                                 axis=0)
        e_own = jnp.concatenate([jnp.broadcast_to(ends[j], blk) for j in range(nsub)], axis=0)
        q_loc = (q * jnp.exp(bcum - e_prev)).astype(BF16)
        k_loc = k * jnp.exp(e_own - bcum)
        q_cols = []
        k_cols = []
        for i in range(1, nsub):
            q_cols.append(jnp.concatenate(
                [zero_blk] * i + [q_loc[i * HG_SUB:(i + 1) * HG_SUB]] + [zero_blk] * (nsub - 1 - i), axis=0))
            parts = []
            for j in range(i):
                kj = k_loc[j * HG_SUB:(j + 1) * HG_SUB]
                if j < i - 1:
                    kj = kj * jnp.exp(ends[i - 1] - ends[j])
                parts.append(kj.astype(BF16))
            k_cols.append(jnp.concatenate(parts + [zero_blk] * (nsub - i), axis=0))
        q_aug = jnp.concatenate(q_cols, axis=1)
        k_aug = jnp.concatenate(k_cols, axis=1)
        att = _dot_nt(q_aug, k_aug)
        o = o + _dot(att.astype(BF16), v.astype(BF16))
        kp_ref[pl.ds(HG_SUB, TOK), :] = k
        bp_ref[pl.ds(HG_SUB, TOK), :] = bcum
        vp_ref[pl.ds(HG_SUB, TOK), :] = v
        o = o + jnp.sum(q * k, axis=1, keepdims=True) * v
        for d in range(1, HG_SUB):
            ks = kp_ref[pl.ds(HG_SUB - d, TOK), :]
            bs = bp_ref[pl.ds(HG_SUB - d, TOK), :]
            vs = vp_ref[pl.ds(HG_SUB - d, TOK), :]
            ok = in_sub >= d
            w = jnp.sum(q * ks * jnp.exp(bcum - bs), axis=1, keepdims=True)
            o = o + jnp.where(ok, w * vs, 0.0)
        b_last = bcum[TOK - 1:TOK, :]
        kdec = (k * jnp.exp(b_last - bcum)).astype(BF16)
        st_ref[h] = st * jnp.exp(b_last) + _dot(v.T.astype(BF16), kdec)
        outs.append(_rms(o, nw_ref[...]) * _silu(g_ref[:, sl]))
    y_ref[...] = jnp.concatenate(outs, axis=1).astype(BF16)


HG_INPUTS = 7


SSD_BLOCKED = 3
HG_BLOCKED = 4


def _recurrent_kernel(*refs, bsz):
    n_in = SSD_INPUTS + HG_INPUTS
    ssd_out, hg_out = refs[n_in], refs[n_in + 1]
    scratch = refs[n_in + 2:]
    for b in range(bsz):
        ssd_in = [r.at[b] for r in refs[:SSD_BLOCKED]] + list(refs[SSD_BLOCKED:SSD_INPUTS])
        hg_in = ([r.at[b] for r in refs[SSD_INPUTS:SSD_INPUTS + HG_BLOCKED]]
                 + list(refs[SSD_INPUTS + HG_BLOCKED:n_in]))
        _ssd_kernel(*ssd_in, ssd_out.at[b], *(r.at[b] for r in scratch[:2]))
        _hgrn2_kernel(*hg_in, hg_out.at[b], *(r.at[b] for r in scratch[2:]))


def _recurrent_mixers(proj, cw, cb, dtb, alog, dskip, ssd_nw, e8, lb, hg_nw, tri, bsz, s):
    nt = s // TOK
    proj = proj.reshape(bsz, s, proj.shape[-1])

    def col(c0, w):
        return pl.BlockSpec((bsz, TOK, w), lambda i: (0, i, c0 // w))

    def full(shape):
        return pl.BlockSpec(shape, lambda i: (0,) * len(shape))

    def per_batch(shape):
        return pltpu.VMEM((bsz,) + shape, F32)

    pad = per_batch((TOK + HG_SUB, HG_KEY_DIM))
    ssd_specs = [col(COL_Z, SSM_INNER), col(COL_XBC, SSM_CONV_DIM), col(COL_DT, LANES),
                 full((SSM_CONV, SSM_CONV_DIM)), full((1, SSM_CONV_DIM)), full((1, LANES)), full((1, LANES)),
                 full((1, SSM_INNER)), full((1, SSM_INNER)), full((LANES, SSM_INNER)), full((TOK, TOK))]
    hg_specs = [col(COL_HQ, HG_WIDTH), col(COL_HF, HG_WIDTH), col(COL_HI, HG_WIDTH), col(COL_HG, HG_WIDTH),
                full((1, HG_WIDTH)), full((1, HG_VAL_DIM)), full((TOK, TOK))]
    assert len(ssd_specs) == SSD_INPUTS and len(hg_specs) == HG_INPUTS
    y_b, y_c = pl.pallas_call(
        functools.partial(_recurrent_kernel, bsz=bsz),
        grid=(nt,),
        in_specs=ssd_specs + hg_specs,
        out_specs=[pl.BlockSpec((bsz, TOK, SSM_INNER), lambda i: (0, i, 0)),
                   pl.BlockSpec((bsz, TOK, HG_WIDTH), lambda i: (0, i, 0))],
        out_shape=[jax.ShapeDtypeStruct((bsz, s, SSM_INNER), BF16),
                   jax.ShapeDtypeStruct((bsz, s, HG_WIDTH), BF16)],
        scratch_shapes=[per_batch((TOK + SUBLANES, SSM_CONV_DIM)), per_batch((LANES, SSM_INNER)),
                        per_batch((HG_HEADS, HG_VAL_DIM, HG_KEY_DIM)), pad, pad, pad],
        compiler_params=_cparams(("arbitrary",)),
        name="ssd_hgrn2",
    )(proj, proj, proj, cw, cb, dtb, alog, dskip, ssd_nw, e8, tri, proj, proj, proj, proj, lb, hg_nw, tri)
    return y_b.reshape(bsz * s, SSM_INNER), y_c.reshape(bsz * s, HG_WIDTH)


def _merge_kernel(ga_ref, gb_ref, gc_ref, oc_ref, os_ref, ow_ref, yb_ref, yc_ref, x_ref,
                  wa_ref, wb_ref, wc_ref, wo_ref, o_ref):
    nsa = oc_ref[...] + os_ref[...] + ow_ref[...]
    ya = _dot(nsa.astype(BF16), wa_ref[...])
    yb = _dot(yb_ref[...], wb_ref[...])
    yc = _dot(yc_ref[...], wc_ref[...])
    merged = (jax.nn.sigmoid(ga_ref[...].astype(F32)) * ya + jax.nn.sigmoid(gb_ref[...].astype(F32)) * yb
              + jax.nn.sigmoid(gc_ref[...].astype(F32)) * yc)
    o_ref[...] = x_ref[...] + _dot(merged.astype(BF16), wo_ref[...])


def _merge(gates, oc, os_, ow, yb, yc, x2, wa, wb, wc, wo):
    t = x2.shape[0]
    tm = min(512, t)

    def col(c0, w):
        return pl.BlockSpec((tm, w), lambda i: (i, c0 // w))

    def full(shape):
        return pl.BlockSpec(shape, lambda i: (0,) * len(shape))

    row512 = pl.BlockSpec((tm, 512), lambda i: (i, 0))
    return pl.pallas_call(
        _merge_kernel,
        grid=(t // tm,),
        in_specs=[col(COL_GA, D_MODEL), col(COL_GB, D_MODEL), col(COL_GC, D_MODEL),
                  row512, row512, row512, row512, row512,
                  pl.BlockSpec((tm, D_MODEL), lambda i: (i, 0)),
                  full((NSA_Q_WIDTH, D_MODEL)), full((SSM_INNER, D_MODEL)),
                  full((HG_WIDTH, D_MODEL)), full((D_MODEL, D_MODEL))],
        out_specs=pl.BlockSpec((tm, D_MODEL), lambda i: (i, 0)),
        out_shape=jax.ShapeDtypeStruct((t, D_MODEL), F32),
        compiler_params=_cparams(("parallel",)),
        name="merge",
    )(gates, gates, gates, oc, os_, ow, yb, yc, x2, wa, wb, wc, wo)


def _xa_mem_kernel(mem_ref, nm_ref, wk_ref, wv_ref, kn_ref, k_ref, v_ref):
    m = _rms(mem_ref[0], nm_ref[...]).astype(BF16)
    k = _dot(m, wk_ref[...])
    ks = [_rms(k[:, h * XA_HEAD_DIM:(h + 1) * XA_HEAD_DIM], kn_ref[...]) for h in range(XA_HEADS)]
    k_ref[0] = jnp.concatenate(ks, axis=1).astype(BF16)
    v_ref[0] = _dot(m, wv_ref[...]).astype(BF16)


def _xa_mem(mem, nm, wk, wv, kn):
    bsz, ml, _ = mem.shape

    def full(shape):
        return pl.BlockSpec(shape, lambda b: (0,) * len(shape))

    return pl.pallas_call(
        _xa_mem_kernel,
        grid=(bsz,),
        in_specs=[pl.BlockSpec((1, ml, D_MODEL), lambda b: (b, 0, 0)), full((1, D_MODEL)),
                  full((D_MODEL, XA_WIDTH)), full((D_MODEL, XA_WIDTH)), full((1, XA_HEAD_DIM))],
        out_specs=[pl.BlockSpec((1, ml, XA_WIDTH), lambda b: (b, 0, 0)),
                   pl.BlockSpec((1, ml, XA_WIDTH), lambda b: (b, 0, 0))],
        out_shape=[jax.ShapeDtypeStruct((bsz, ml, XA_WIDTH), BF16),
                   jax.ShapeDtypeStruct((bsz, ml, XA_WIDTH), BF16)],
        compiler_params=_cparams(("parallel",)),
        name="xa_mem",
    )(mem, nm, wk, wv, kn)


def _xa_kernel(x_ref, nx_ref, wq_ref, qn_ref, k_ref, v_ref, wo_ref, o_ref):
    x = x_ref[...]
    h = _rms(x, nx_ref[...]).astype(BF16)
    q = _dot(h, wq_ref[...])
    outs = []
    for hd in range(XA_HEADS):
        sl = slice(hd * XA_HEAD_DIM, (hd + 1) * XA_HEAD_DIM)
        qh = _rms(q[:, sl], qn_ref[...]).astype(BF16)
        sc = _dot_nt(qh, k_ref[0, :, sl]) * (XA_HEAD_DIM ** -0.5)
        sc = sc - jnp.max(sc, axis=-1, keepdims=True)
        p = jnp.exp(sc)
        p = p / jnp.sum(p, axis=-1, keepdims=True)
        outs.append(_dot(p.astype(BF16), v_ref[0, :, sl]))
    o = jnp.concatenate(outs, axis=1).astype(BF16)
    o_ref[...] = x + _dot(o, wo_ref[...])


def _xa(x2, nx, wq, qn, k, v, wo, s):
    t = x2.shape[0]
    tm = min(512, s)
    ml = k.shape[1]
    per_b = s // tm

    def full(shape):
        return pl.BlockSpec(shape, lambda i: (0,) * len(shape))

    return pl.pallas_call(
        _xa_kernel,
        grid=(t // tm,),
        in_specs=[pl.BlockSpec((tm, D_MODEL), lambda i: (i, 0)), full((1, D_MODEL)), full((D_MODEL, XA_WIDTH)),
                  full((1, XA_HEAD_DIM)),
                  pl.BlockSpec((1, ml, XA_WIDTH), lambda i: (i // per_b, 0, 0)),
                  pl.BlockSpec((1, ml, XA_WIDTH), lambda i: (i // per_b, 0, 0)),
                  full((XA_WIDTH, D_MODEL))],
        out_specs=pl.BlockSpec((tm, D_MODEL), lambda i: (i, 0)),
        out_shape=jax.ShapeDtypeStruct((t, D_MODEL), F32),
        compiler_params=_cparams(("parallel",)),
        name="cross_attn",
    )(x2, nx, wq, qn, k, v, wo)


def _ffn_kernel(x_ref, nw_ref, wg_ref, wu_ref, wd_ref, o_ref, h_ref, acc_ref):
    j = pl.program_id(1)

    @pl.when(j == 0)
    def _():
        h_ref[...] = _rms(x_ref[...], nw_ref[...]).astype(BF16)
        acc_ref[...] = jnp.zeros_like(acc_ref)

    h = h_ref[...]
    a = _silu(_dot(h, wg_ref[...])) * _dot(h, wu_ref[...])
    acc_ref[...] += _dot(a.astype(BF16), wd_ref[...])

    @pl.when(j == pl.num_programs(1) - 1)
    def _():
        o_ref[...] = x_ref[...] + acc_ref[...]


def _ffn(x2, nw, wg, wu, wd):
    t = x2.shape[0]
    tm = min(1024, t)
    th = FFN_HIDDEN // 2
    return pl.pallas_call(
        _ffn_kernel,
        grid=(t // tm, FFN_HIDDEN // th),
        in_specs=[pl.BlockSpec((tm, D_MODEL), lambda i, j: (i, 0)),
                  pl.BlockSpec((1, D_MODEL), lambda i, j: (0, 0)),
                  pl.BlockSpec((D_MODEL, th), lambda i, j: (0, j)),
                  pl.BlockSpec((D_MODEL, th), lambda i, j: (0, j)),
                  pl.BlockSpec((th, D_MODEL), lambda i, j: (j, 0))],
        out_specs=pl.BlockSpec((tm, D_MODEL), lambda i, j: (i, 0)),
        out_shape=jax.ShapeDtypeStruct((t, D_MODEL), F32),
        scratch_shapes=[pltpu.VMEM((tm, D_MODEL), BF16), pltpu.VMEM((tm, D_MODEL), F32)],
        compiler_params=_cparams(("parallel", "arbitrary")),
        name="ffn",
    )(x2, nw, wg, wu, wd)


def _pack_w_in(w):
    parts = {}
    off = 0
    names = ("q", "kc", "vc", "ks", "vs", "kw", "vw", "ng", "z", "xbc", "dt", "hq", "hf", "hi", "hg",
             "ga", "gb", "gc")
    for name, size in zip(names, SPLIT_SIZES):
        parts[name] = w[:, off:off + size]
        off += size

    def padded(a, width):
        return jnp.pad(a, ((0, 0), (0, width - a.shape[1])))

    cols = [parts["ga"], parts["gb"], parts["gc"], parts["q"], parts["z"], parts["hq"], parts["hf"],
            parts["hi"], parts["hg"], parts["xbc"], parts["kc"], parts["vc"], parts["ks"], parts["vs"],
            parts["kw"], parts["vw"], padded(parts["ng"], LANES), padded(parts["dt"], LANES)]
    packed = jnp.concatenate(cols, axis=1)
    return padded(packed, PROJ_WIDTH).astype(BF16)


def _rope_tables(pos):
    half = NSA_HEAD_DIM // 2
    inv = 1.0 / (ROPE_THETA ** (jnp.arange(half, dtype=F32) / half))
    ang = pos.astype(F32)[:, None] * inv[None, :]
    reps = LANES // half
    return jnp.tile(jnp.cos(ang), (1, reps)), jnp.tile(jnp.sin(ang), (1, reps))


def _block_diag2(w):
    z = jnp.zeros_like(w)
    return jnp.concatenate([jnp.concatenate([w, z], axis=-1), jnp.concatenate([z, w], axis=-1)], axis=-2)


def kernel(x, mem, norm_mix, w_in, nsa_q_norm, nsa_k_norm, nsa_cmp_pos_k, nsa_cmp_pos_v, nsa_cmp_w1_k, nsa_cmp_w2_k, nsa_cmp_w1_v, nsa_cmp_w2_v, w_nsa_o, ssm_conv_w, ssm_conv_b, ssm_dt_bias, ssm_a_log, ssm_d, ssm_norm, w_ssm_o, hg_lb_logits, hg_norm, w_hg_o, w_out, norm_xa, norm_mem, xa_w_q, xa_w_k, xa_w_v, xa_q_norm, xa_k_norm, xa_w_o, norm_ffn, ffn_w_gate, ffn_w_up, ffn_w_down):
    bsz, s, d = x.shape
    depth = w_in.shape[0]
    assert d == D_MODEL and PREP_TILES * TOK == SEL_TILE
    assert s % SEL_TILE == 0 and s % (SEL_Q * TOK) == 0 and s // TOK >= WIN_TILES
    assert s // NSA_SEL_BLOCK >= NSA_TOPK and (s // NSA_CMP_STRIDE) % (4 * SUBLANES) == 0
    t = bsz * s
    hd = NSA_HEAD_DIM

    cos_t, sin_t = _rope_tables(jnp.arange(s))
    nbp = s // NSA_CMP_STRIDE
    cos_c, sin_c = _rope_tables(jnp.arange(nbp) * NSA_CMP_STRIDE + NSA_CMP_BLOCK - 1)
    tri = jnp.tril(jnp.ones((TOK, TOK), F32)).astype(BF16)
    rows = np.arange(LANES)[:, None]
    cols = np.arange(NSA_Q_WIDTH)[None, :]
    e8 = jnp.asarray(rows == cols // SSM_HEAD_DIM, F32).astype(BF16)
    li = np.arange(LANES)[:, None]
    lo = np.arange(LANES)[None, :]
    head_sum = jnp.asarray(li // hd == lo // hd, F32).astype(BF16)
    rot_half = jnp.asarray(np.where(lo % hd < hd // 2, -1.0 * (li == lo + hd // 2), 1.0 * (li == lo - hd // 2)),
                           F32).astype(BF16)

    lb_sm = jax.nn.softmax(hg_lb_logits.astype(F32), axis=0)
    lb_all = jnp.cumsum(lb_sm, axis=0) - lb_sm[0:1]

    def pad_lanes(v):
        return jnp.pad(v, (0, LANES - v.shape[0]))[None, :]

    x2 = x.reshape(t, d)
    for l in range(depth):
        gates, proj = _in_proj(x2, norm_mix[l][None, :], _pack_w_in(w_in[l]))

        qn2 = jnp.tile(nsa_q_norm[l], 2)[None, :]
        kn2 = jnp.tile(nsa_k_norm[l], 2)[None, :]
        qp, ksn, kwn, vst, vwt, gt = _nsa_prep(proj, cos_t, sin_t, qn2, kn2, head_sum, rot_half, bsz, s)
        w1k = _block_diag2(nsa_cmp_w1_k[l].reshape(NSA_CMP_BLOCK, hd, hd)).astype(BF16)
        w1v = _block_diag2(nsa_cmp_w1_v[l].reshape(NSA_CMP_BLOCK, hd, hd)).astype(BF16)
        kc, vct = _compress(proj, jnp.tile(nsa_cmp_pos_k[l], (1, 2)), jnp.tile(nsa_cmp_pos_v[l], (1, 2)),
                            w1k, _block_diag2(nsa_cmp_w2_k[l]).astype(BF16),
                            w1v, _block_diag2(nsa_cmp_w2_v[l]).astype(BF16), kn2, cos_c, sin_c, bsz, s)
        o_c, sel_bias = _cmp_topk(qp, kc, vct, gt, bsz, s)
        tile_list, tile_count = _active_tiles(sel_bias, bsz, s)
        o_s = _selected(qp, ksn, vst, sel_bias, gt, tile_list, tile_count, bsz, s)
        o_w = _window(qp, kwn, vwt, gt, bsz, s)

        y_b, y_c = _recurrent_mixers(
            proj, ssm_conv_w[l], ssm_conv_b[l][None, :], pad_lanes(ssm_dt_bias[l]), pad_lanes(ssm_a_log[l]),
            jnp.repeat(ssm_d[l], SSM_HEAD_DIM)[None, :], ssm_norm[l][None, :], e8,
            lb_all[l][None, :], hg_norm[l][None, :], tri, bsz, s)

        x2 = _merge(gates, o_c.reshape(t, -1), o_s.reshape(t, -1), o_w.reshape(t, -1), y_b, y_c, x2,
                    w_nsa_o[l].astype(BF16), w_ssm_o[l].astype(BF16), w_hg_o[l].astype(BF16),
                    w_out[l].astype(BF16))

        k_m, v_m = _xa_mem(mem, norm_mem[l][None, :], xa_w_k[l].astype(BF16), xa_w_v[l].astype(BF16),
                           xa_k_norm[l][None, :])
        x2 = _xa(x2, norm_xa[l][None, :], xa_w_q[l].astype(BF16), xa_q_norm[l][None, :], k_m, v_m,
                 xa_w_o[l].astype(BF16), s)
        x2 = _ffn(x2, norm_ffn[l][None, :], ffn_w_gate[l].astype(BF16), ffn_w_up[l].astype(BF16),
                  ffn_w_down[l].astype(BF16))
    return x2.reshape(bsz, s, d)
```
